```python
import jax, jax.numpy as jnp
from jax import lax
import numpy as np

D_MODEL = 1024
BATCH = 8
SEQ = 2048
DEPTH = 4

HEAD_DIM = 64
MIX_WIDTH = D_MODEL
RET_WIDTH = MIX_WIDTH // 4
RET_HEADS = RET_WIDTH // HEAD_DIM
RET_CHUNK = 128
CONV_WIDTH = MIX_WIDTH // 4
CONV_K = 3
NSA_WIDTH = MIX_WIDTH // 2
NSA_HEADS = NSA_WIDTH // HEAD_DIM
NSA_KV_HEADS = 2
NSA_GROUP = NSA_HEADS // NSA_KV_HEADS
NSA_KV_WIDTH = NSA_KV_HEADS * HEAD_DIM
CMP_LEN = 32
CMP_STRIDE = 16
CMP_HIDDEN = 2 * HEAD_DIM
SEL_LEN = 64
SEL_TOP = 8
WINDOW = 512
Q_BLOCK = 128
ROPE_THETA = 10000.0
N_EXPERTS = 32
TOP_K = 4
D_EXPERT = D_MODEL
SWIGLU_LIMIT = 7.0
SWIGLU_ALPHA = 1.702
MOE_BLOCK = 128
DEEPNORM_ALPHA = (2 * DEPTH) ** 0.25
DEEPNORM_BETA = (8 * DEPTH) ** -0.25
LN_EPS = 1e-5
NEG_INF = -1e30
FORCE_SCORE = 1e9
MIX_SPLITS = (RET_WIDTH,) * 4 + (CONV_WIDTH,) * 3 + (NSA_WIDTH,) + (NSA_KV_WIDTH,) * 6 + (NSA_HEADS * 3,)
MIX_VALUE_COLS = (2, 6, 9, 11, 13)

kernel_name = 'hybrid_retention_conv_nsa_moe_trunk'


def layer_norm(x):
    xf = x.astype(jnp.float32)
    mu = jnp.mean(xf, -1, keepdims=True)
    var = jnp.mean(jnp.square(xf - mu), -1, keepdims=True)
    return (xf - mu) * lax.rsqrt(var + LN_EPS)


def rope(t, positions):
    half = t.shape[-1] // 2
    inv = ROPE_THETA ** (-jnp.arange(half, dtype=jnp.float32) / half)
    ang = positions.astype(jnp.float32)[..., None] * inv
    cos = jnp.cos(ang)[:, :, None, :]
    sin = jnp.sin(ang)[:, :, None, :]
    t1 = t[..., :half].astype(jnp.float32)
    t2 = t[..., half:].astype(jnp.float32)
    return jnp.concatenate([t1 * cos - t2 * sin, t2 * cos + t1 * sin], -1).astype(t.dtype)


def retention(q, k, v, positions, gn_w):
    B, S, H, Dh = q.shape
    f32 = jnp.float32
    n_chunks = S // RET_CHUNK
    q = rope(q, positions).astype(f32)
    k = rope(k, positions).astype(f32) * Dh ** -0.5
    v = v.astype(f32)
    log_gamma = jnp.log(1.0 - jnp.power(2.0, -5.0 - jnp.arange(H, dtype=f32)))
    i = jnp.arange(RET_CHUNK, dtype=f32)
    diff = i[:, None] - i[None, :]
    intra = jnp.where(diff >= 0, jnp.exp(diff * log_gamma[:, None, None]), 0.0)
    q_dec = jnp.exp((i + 1.0) * log_gamma[:, None])[..., None]
    k_dec = jnp.exp((RET_CHUNK - 1.0 - i) * log_gamma[:, None])[..., None]
    c_dec = jnp.exp(RET_CHUNK * log_gamma)[:, None, None]

    def chunks(t):
        return t.reshape(B, n_chunks, RET_CHUNK, H, Dh).transpose(1, 0, 3, 2, 4)

    def step(state, qkv):
        qc, kc, vc = qkv
        scores = jnp.einsum('bhid,bhjd->bhij', qc, kc) * intra
        out = jnp.einsum('bhij,bhje->bhie', scores, vc) + jnp.einsum('bhid,bhde->bhie', qc * q_dec, state)
        state = state * c_dec + jnp.einsum('bhjd,bhje->bhde', kc * k_dec, vc)
        return state, out

    state0 = jnp.zeros((B, H, Dh, Dh), f32)
    _, out = lax.scan(step, state0, (chunks(q), chunks(k), chunks(v)))
    out = out.transpose(1, 0, 3, 2, 4)
    return layer_norm(out).reshape(B, S, H * Dh) * gn_w


def short_conv(b_gate, c_gate, h, conv_w):
    u = c_gate * h
    y = lax.conv_general_dilated(u, conv_w[:, None, :].astype(u.dtype), (1,), [(CONV_K - 1, 0)],
                                 dimension_numbers=('NWC', 'WIO', 'NWC'), feature_group_count=CONV_WIDTH)
    return b_gate * y


def nsa_attention(q, k_cmp, v_cmp, k_slc, v_slc, k_win, v_win, gate_logits, positions, cmp_pos, cmp_w1, cmp_w2):
    B, S = q.shape[:2]
    f32 = jnp.float32
    n_cmp = (S - CMP_LEN) // CMP_STRIDE + 1
    n_slc = S // SEL_LEN
    k_sel = min(SEL_TOP, n_slc)
    n_qb = S // Q_BLOCK
    scale = HEAD_DIM ** -0.5

    cidx = np.arange(n_cmp)[:, None] * CMP_STRIDE + np.arange(CMP_LEN)[None, :]

    def compress(t, pos, w1, w2):
        blocks = t[:, cidx] + pos[:, None, :]
        blocks = blocks.transpose(0, 1, 3, 2, 4).reshape(B, n_cmp, NSA_KV_HEADS, CMP_LEN * HEAD_DIM)
        return jax.nn.gelu(blocks @ w1) @ w2

    kc = compress(k_cmp, cmp_pos[0], cmp_w1[0], cmp_w2[0])
    vc = compress(v_cmp, cmp_pos[1], cmp_w1[1], cmp_w2[1])
    cmp_start = jnp.arange(n_cmp) * CMP_STRIDE
    cmp_last = cmp_start + CMP_LEN - 1
    sel_start = jnp.arange(n_slc) * SEL_LEN
    overlap = ((cmp_start[:, None] < sel_start[None, :] + SEL_LEN) &
               (sel_start[None, :] <= cmp_last[:, None])).astype(f32)

    q_grp = q.reshape(B, S, NSA_KV_HEADS, NSA_GROUP, HEAD_DIM)
    q_rot = rope(q, positions).reshape(B, S, NSA_KV_HEADS, NSA_GROUP, HEAD_DIM)
    k_slc = rope(k_slc, positions)
    k_win = rope(k_win, positions)
    ks_b = k_slc.reshape(B, n_slc, SEL_LEN, NSA_KV_HEADS, HEAD_DIM).transpose(0, 3, 1, 2, 4)
    vs_b = v_slc.reshape(B, n_slc, SEL_LEN, NSA_KV_HEADS, HEAD_DIM).transpose(0, 3, 1, 2, 4)
    pad = ((0, 0), (WINDOW, 0), (0, 0), (0, 0))
    kw_pad = jnp.pad(k_win, pad)
    vw_pad = jnp.pad(v_win, pad)
    gates = jax.nn.sigmoid(gate_logits.reshape(B, S, NSA_KV_HEADS, NSA_GROUP, 3))
    b_ix = jnp.arange(B)[:, None, None, None]
    h_ix = jnp.arange(NSA_KV_HEADS)[None, :, None, None]
    blk = jnp.arange(n_slc)

    def block(qb_idx):
        q0 = qb_idx * Q_BLOCK
        t = q0 + jnp.arange(Q_BLOCK)
        qu = lax.dynamic_slice_in_dim(q_grp, q0, Q_BLOCK, axis=1)
        qr = lax.dynamic_slice_in_dim(q_rot, q0, Q_BLOCK, axis=1)
        gb = lax.dynamic_slice_in_dim(gates, q0, Q_BLOCK, axis=1)
        s = jnp.einsum('bqhgd,bchd->bhgqc', qu, kc, preferred_element_type=f32) * scale
        cvalid = cmp_last[None, :] <= t[:, None]
        p_c = jax.nn.softmax(jnp.where(cvalid, s, NEG_INF), axis=-1) * cvalid
        o_c = jnp.einsum('bhgqc,bchd->bqhgd', p_c.astype(vc.dtype), vc)
        imp = jnp.einsum('bhgqc,cj->bhqj', p_c, overlap)
        forced = (blk[None, :] == 0) | (blk[None, :] == (t // SEL_LEN)[:, None])
        svalid = blk[None, :] * SEL_LEN <= t[:, None]
        imp = jnp.where(forced, FORCE_SCORE, jnp.where(svalid, imp, -FORCE_SCORE))
        _, sel = lax.top_k(imp, k_sel)
        kg = ks_b[b_ix, h_ix, sel]
        vg = vs_b[b_ix, h_ix, sel]
        kpos = sel[..., None] * SEL_LEN + jnp.arange(SEL_LEN)
        smask = (kpos <= t[None, None, :, None, None])[:, :, None]
        s = jnp.einsum('bqhgd,bhqkld->bhgqkl', qr, kg, preferred_element_type=f32) * scale
        s = jnp.where(smask, s, NEG_INF).reshape(B, NSA_KV_HEADS, NSA_GROUP, Q_BLOCK, k_sel * SEL_LEN)
        p_s = jax.nn.softmax(s, axis=-1).reshape(B, NSA_KV_HEADS, NSA_GROUP, Q_BLOCK, k_sel, SEL_LEN)
        o_s = jnp.einsum('bhgqkl,bhqkld->bqhgd', p_s.astype(vg.dtype), vg)
        kw = lax.dynamic_slice_in_dim(kw_pad, q0, WINDOW + Q_BLOCK, axis=1)
        vw = lax.dynamic_slice_in_dim(vw_pad, q0, WINDOW + Q_BLOCK, axis=1)
        wpos = q0 - WINDOW + jnp.arange(WINDOW + Q_BLOCK)
        rel = t[:, None] - wpos[None, :]
        wmask = (wpos[None, :] >= 0) & (rel >= 0) & (rel < WINDOW)
        s = jnp.einsum('bqhgd,bkhd->bhgqk', qr, kw, preferred_element_type=f32) * scale
        p_w = jax.nn.softmax(jnp.where(wmask, s, NEG_INF), axis=-1)
        o_w = jnp.einsum('bhgqk,bkhd->bqhgd', p_w.astype(vw.dtype), vw)
        o = gb[..., 0:1] * o_c + gb[..., 1:2] * o_s + gb[..., 2:3] * o_w
        return o.reshape(B, Q_BLOCK, NSA_WIDTH)

    out = lax.map(block, jnp.arange(n_qb))
    return out.transpose(1, 0, 2, 3).reshape(B, S, NSA_WIDTH)


def hybrid_mixer(h, positions, w_in, w_out, ret_gn_w, conv_w, cmp_pos, cmp_w1, cmp_w2):
    B, S, _ = h.shape
    points = [int(p) for p in np.cumsum(MIX_SPLITS)[:-1]]
    (rq, rk, rv, rg, cb, cc, ch, nq, nkc, nvc, nks, nvs, nkw, nvw, ng) = jnp.split(h @ w_in, points, axis=-1)

    def heads(t, n):
        return t.reshape(B, S, n, HEAD_DIM)

    y_ret = jax.nn.silu(rg) * retention(heads(rq, RET_HEADS), heads(rk, RET_HEADS), heads(rv, RET_HEADS),
                                        positions, ret_gn_w).astype(h.dtype)
    y_conv = short_conv(cb, cc, ch, conv_w)
    y_nsa = nsa_attention(heads(nq, NSA_HEADS), heads(nkc, NSA_KV_HEADS), heads(nvc, NSA_KV_HEADS),
                          heads(nks, NSA_KV_HEADS), heads(nvs, NSA_KV_HEADS), heads(nkw, NSA_KV_HEADS),
                          heads(nvw, NSA_KV_HEADS), ng, positions, cmp_pos, cmp_w1, cmp_w2)
    return jnp.concatenate([y_ret, y_conv, y_nsa], axis=-1) @ w_out


def moe_ffn(h, router_w, router_b, w_gu, b_gu, w_down, b_down):
    B, S, D = h.shape
    T = B * S
    xt = h.reshape(T, D)
    logits = (xt @ router_w).astype(jnp.float32) + router_b.astype(jnp.float32)
    top_val, top_idx = lax.top_k(logits, TOP_K)
    gate = jax.nn.softmax(top_val, axis=-1)
    flat_e = top_idx.reshape(-1)
    flat_tok = jnp.repeat(jnp.arange(T, dtype=jnp.int32), TOP_K)
    flat_w = gate.reshape(-1)
    order = jnp.argsort(flat_e)
    se = flat_e[order]
    counts = jnp.bincount(flat_e, length=N_EXPERTS)
    start = jnp.cumsum(counts) - counts
    padded = (counts + MOE_BLOCK - 1) // MOE_BLOCK * MOE_BLOCK
    pend = jnp.cumsum(padded)
    pstart = pend - padded
    dest = pstart[se] + jnp.arange(T * TOP_K) - start[se]
    n_rows = T * TOP_K + N_EXPERTS * MOE_BLOCK
    n_blocks = n_rows // MOE_BLOCK
    row_tok = jnp.full((n_rows,), T, jnp.int32).at[dest].set(flat_tok[order])
    row_w = jnp.zeros((n_rows,), jnp.float32).at[dest].set(flat_w[order])
    block_e = jnp.minimum(jnp.searchsorted(pend, jnp.arange(n_blocks) * MOE_BLOCK, side='right'), N_EXPERTS - 1)
    rows = jnp.concatenate([xt, jnp.zeros((1, D), xt.dtype)], 0)[row_tok].reshape(n_blocks, MOE_BLOCK, D)

    def expert_block(args):
        r, e = args
        gu = r @ w_gu[e] + b_gu[e]
        g = jnp.minimum(gu[:, :D_EXPERT], SWIGLU_LIMIT)
        u = jnp.clip(gu[:, D_EXPERT:], -SWIGLU_LIMIT, SWIGLU_LIMIT)
        act = (u + 1.0) * (g * jax.nn.sigmoid(SWIGLU_ALPHA * g))
        return act @ w_down[e] + b_down[e]

    y = lax.map(expert_block, (rows, block_e)).reshape(n_rows, D)
    y = jax.ops.segment_sum(y.astype(jnp.float32) * row_w[:, None], row_tok, num_segments=T + 1)[:T]
    return y.astype(h.dtype).reshape(B, S, D)


def setup_inputs(seed: int = 0) -> dict:
    key = jax.random.key(seed)
    ks = jax.random.split(key, 20)
    f32 = jnp.float32

    def nrm(k, shape, s):
        return jax.random.normal(k, shape, f32) * s

    n_in = sum(MIX_SPLITS)
    col_scale = np.concatenate([np.full((w,), DEEPNORM_BETA if i in MIX_VALUE_COLS else 1.0, np.float32)
                                for i, w in enumerate(MIX_SPLITS)])
    x = nrm(ks[0], (BATCH, SEQ, D_MODEL), 1.0)
    c = nrm(ks[1], (BATCH, D_MODEL), 1.0)
    positions = (jnp.arange(SEQ, dtype=jnp.int32)[None, :] +
                 jax.random.randint(ks[2], (BATCH, 1), 0, 4096, dtype=jnp.int32))
    w_in = nrm(ks[3], (DEPTH, D_MODEL, n_in), D_MODEL ** -0.5) * jnp.asarray(col_scale)
    w_out = nrm(ks[4], (DEPTH, MIX_WIDTH, D_MODEL), MIX_WIDTH ** -0.5 * DEEPNORM_BETA)
    ret_gn_w = 1.0 + nrm(ks[5], (DEPTH, RET_WIDTH), 0.02)
    conv_w = nrm(ks[6], (DEPTH, CONV_K, CONV_WIDTH), CONV_K ** -0.5)
    cmp_pos = nrm(ks[7], (DEPTH, 2, CMP_LEN, HEAD_DIM), 0.02)
    cmp_w1 = nrm(ks[8], (DEPTH, 2, CMP_LEN * HEAD_DIM, CMP_HIDDEN), (CMP_LEN * HEAD_DIM) ** -0.5)
    cmp_w2 = nrm(ks[9], (DEPTH, 2, CMP_HIDDEN, HEAD_DIM), CMP_HIDDEN ** -0.5)
    ada_w = nrm(ks[10], (DEPTH, D_MODEL, 6 * D_MODEL), 0.01)
    ada_b = nrm(ks[11], (DEPTH, 6 * D_MODEL), 0.01)
    ln_g = 1.0 + nrm(ks[12], (DEPTH, 2, D_MODEL), 0.02)
    ln_b = nrm(ks[13], (DEPTH, 2, D_MODEL), 0.02)
    router_w = nrm(ks[14], (DEPTH, D_MODEL, N_EXPERTS), D_MODEL ** -0.5)
    router_b = nrm(ks[15], (DEPTH, N_EXPERTS), 0.01)
    w_gate_up = nrm(ks[16], (DEPTH, N_EXPERTS, D_MODEL, 2 * D_EXPERT), D_MODEL ** -0.5 * DEEPNORM_BETA)
    b_gate_up = nrm(ks[17], (DEPTH, N_EXPERTS, 2 * D_EXPERT), 0.01)
    w_down = nrm(ks[18], (DEPTH, N_EXPERTS, D_EXPERT, D_MODEL), D_EXPERT ** -0.5 * DEEPNORM_BETA)
    b_down = nrm(ks[19], (DEPTH, N_EXPERTS, D_MODEL), 0.01)
    return {'x': x, 'c': c, 'positions': positions, 'w_in': w_in, 'w_out': w_out, 'ret_gn_w': ret_gn_w,
            'conv_w': conv_w, 'cmp_pos': cmp_pos, 'cmp_w1': cmp_w1, 'cmp_w2': cmp_w2, 'ada_w': ada_w,
            'ada_b': ada_b, 'ln_g': ln_g, 'ln_b': ln_b, 'router_w': router_w, 'router_b': router_b,
            'w_gate_up': w_gate_up, 'b_gate_up': b_gate_up, 'w_down': w_down, 'b_down': b_down}


def reference(x, c, positions, w_in, w_out, ret_gn_w, conv_w, cmp_pos, cmp_w1, cmp_w2, ada_w, ada_b,
              ln_g, ln_b, router_w, router_b, w_gate_up, b_gate_up, w_down, b_down):
    c_act = jax.nn.silu(c)
    for l in range(DEPTH):
        mod = c_act @ ada_w[l] + ada_b[l]
        sh1, sc1, g1, sh2, sc2, g2 = jnp.split(mod, 6, axis=-1)
        h = (layer_norm(x) * (1.0 + sc1[:, None, :]) + sh1[:, None, :]).astype(x.dtype)
        mix = hybrid_mixer(h, positions, w_in[l], w_out[l], ret_gn_w[l], conv_w[l], cmp_pos[l], cmp_w1[l], cmp_w2[l])
        x = (layer_norm(DEEPNORM_ALPHA * x + (1.0 + g1[:, None, :]) * mix) * ln_g[l, 0] + ln_b[l, 0]).astype(x.dtype)
        h = (layer_norm(x) * (1.0 + sc2[:, None, :]) + sh2[:, None, :]).astype(x.dtype)
        ffn = moe_ffn(h, router_w[l], router_b[l], w_gate_up[l], b_gate_up[l], w_down[l], b_down[l])
        x = (layer_norm(DEEPNORM_ALPHA * x + (1.0 + g2[:, None, :]) * ffn) * ln_g[l, 1] + ln_b[l, 1]).astype(x.dtype)
    return x
```

```python
import functools

import numpy as np
import jax
import jax.numpy as jnp
from jax import lax
from jax.experimental import pallas as pl
from jax.experimental.pallas import tpu as pltpu

F32 = jnp.float32
BF16 = jnp.bfloat16

D_MODEL = 1024
HEAD_DIM = 64
RET_WIDTH = 256
RET_HEADS = 4
RET_CHUNK = 128
CONV_WIDTH = 256
CONV_K = 3
NSA_WIDTH = 512
NSA_HEADS = 8
NSA_KV_HEADS = 2
NSA_GROUP = 4
NSA_KV_WIDTH = 128
CMP_LEN = 32
CMP_STRIDE = 16
CMP_HIDDEN = 128
SEL_LEN = 64
SEL_TOP = 8
WINDOW = 512
Q_BLOCK = 128
ROPE_THETA = 10000.0
N_EXPERTS = 32
TOP_K = 4
D_EXPERT = 1024
SWIGLU_LIMIT = 7.0
SWIGLU_ALPHA = 1.702
LN_EPS = 1e-5
NEG_INF = -1e30
FORCE_SCORE = 1e9

LANES = 128
VMEM_LIMIT = 56 * 1024 * 1024

OFF_RQ, OFF_RK, OFF_RV, OFF_RG = 0, 256, 512, 768
OFF_CB, OFF_CC, OFF_CH = 1024, 1280, 1536
OFF_NQ = 1792
OFF_KC, OFF_VC = 2304, 2432
OFF_KS, OFF_VS, OFF_KW, OFF_VW = 2560, 2816, 3072, 3328
OFF_NG = 3584
N_COLS = 3712
N_GATE = NSA_HEADS * 3

ROW_TILE = 256
MOE_TILE = 256
SEL_CHUNK = 512
WIN_KEYS = WINDOW + Q_BLOCK


def _dot(a, b):
    return jnp.dot(a, b, preferred_element_type=F32)


def _dot_nt(a, b):
    return lax.dot_general(a, b, (((1,), (1,)), ((), ())), preferred_element_type=F32)


def _layer_norm(x):
    mu = jnp.mean(x, axis=-1, keepdims=True)
    xc = x - mu
    var = jnp.mean(xc * xc, axis=-1, keepdims=True)
    return xc * lax.rsqrt(var + LN_EPS)


def _params(*sem):
    return pltpu.CompilerParams(dimension_semantics=sem, vmem_limit_bytes=VMEM_LIMIT)


def _mod_kernel(c_ref, w_ref, b_ref, o_ref):
    c = c_ref[...]
    ca = (c * jax.nn.sigmoid(c)).astype(BF16)
    o_ref[0, 0] = _dot(ca, w_ref[0].astype(BF16)) + b_ref[0]


def _modulation(c, ada_w, ada_b):
    depth = ada_w.shape[0]
    batch = c.shape[0]
    out = pl.pallas_call(
        _mod_kernel,
        grid=(depth, 6),
        in_specs=[pl.BlockSpec((batch, D_MODEL), lambda l, j: (0, 0)),
                  pl.BlockSpec((1, D_MODEL, D_MODEL), lambda l, j: (l, 0, j)),
                  pl.BlockSpec((1, 1, D_MODEL), lambda l, j: (l * 6 + j, 0, 0))],
        out_specs=pl.BlockSpec((1, 1, batch, D_MODEL), lambda l, j: (l, j, 0, 0)),
        out_shape=jax.ShapeDtypeStruct((depth, 6, batch, D_MODEL), F32),
        compiler_params=_params("arbitrary", "arbitrary"),
        name="adaln_mod",
    )(c, ada_w, ada_b.reshape(depth * 6, 1, D_MODEL))
    return out.reshape(depth, 6, batch, 1, D_MODEL)


def _rope_table_kernel(pos_ref, inv_ref, cos_ref, sin_ref):
    ang = pos_ref[...] * inv_ref[...]
    cos_ref[...] = jnp.cos(ang)
    sin_ref[...] = jnp.sin(ang)


def _rope_tables(positions):
    half = HEAD_DIM // 2
    per_row = LANES // half
    tokens = positions.size
    pos4 = jnp.repeat(positions.reshape(tokens // per_row, per_row).astype(F32), half, axis=1)
    inv = ROPE_THETA ** (-jnp.arange(half, dtype=F32) / half)
    inv4 = jnp.tile(inv, per_row)[None, :]
    rows = tokens // per_row
    tile = min(rows, 1024)
    cos4, sin4 = pl.pallas_call(
        _rope_table_kernel,
        grid=(rows // tile,),
        in_specs=[pl.BlockSpec((tile, LANES), lambda i: (i, 0)),
                  pl.BlockSpec((1, LANES), lambda i: (0, 0))],
        out_specs=[pl.BlockSpec((tile, LANES), lambda i: (i, 0))] * 2,
        out_shape=[jax.ShapeDtypeStruct((rows, LANES), F32)] * 2,
        compiler_params=_params("arbitrary"),
        name="rope_tables",
    )(pos4, inv4)
    cos = jnp.tile(cos4.reshape(tokens, half), (1, per_row))
    sign = jnp.tile(jnp.concatenate([-jnp.ones((half,), F32), jnp.ones((half,), F32)]), LANES // HEAD_DIM)
    sin = jnp.tile(sin4.reshape(tokens, half), (1, per_row)) * sign[None, :]
    return cos, sin


def _inproj_kernel(x_ref, mod_ref, w_ref, cos_ref, sin_ref, convw_ref,
                   rq_ref, rk_ref, rv_ref, rg_ref, yc_ref, qu_ref, qr_ref, kc_ref, vc_ref,
                   ks_ref, vs_ref, kw_ref, vw_ref, ng_ref, carry_ref, *, tiles_per_seq):
    i = pl.program_id(0)
    tm = x_ref.shape[0]
    h = (_layer_norm(x_ref[...]) * (1.0 + mod_ref[1, 0]) + mod_ref[0, 0]).astype(BF16)
    cosf = cos_ref[...]
    sinf = sin_ref[...]
    lane = lax.broadcasted_iota(jnp.int32, (tm, LANES), 1)
    first_half = (lane % HEAD_DIM) < (HEAD_DIM // 2)

    def proj(off, width):
        return _dot(h, w_ref[0, :, off:off + width])

    def rope(c):
        cols = []
        for j in range(c.shape[1] // LANES):
            cj = c[:, j * LANES:(j + 1) * LANES]
            swapped = jnp.where(first_half, pltpu.roll(cj, LANES - HEAD_DIM // 2, 1),
                                pltpu.roll(cj, HEAD_DIM // 2, 1))
            cols.append(cj * cosf + swapped * sinf)
        return jnp.concatenate(cols, axis=1) if len(cols) > 1 else cols[0]

    scale = HEAD_DIM ** -0.5
    rq_ref[...] = rope(proj(OFF_RQ, RET_WIDTH)).astype(BF16)
    rk_ref[...] = (rope(proj(OFF_RK, RET_WIDTH)) * scale).astype(BF16)
    rv_ref[...] = proj(OFF_RV, RET_WIDTH).astype(BF16)
    rg = proj(OFF_RG, RET_WIDTH)
    rg_ref[...] = (rg * jax.nn.sigmoid(rg)).astype(BF16)

    cb = proj(OFF_CB, CONV_WIDTH)
    u = proj(OFF_CC, CONV_WIDTH) * proj(OFF_CH, CONV_WIDTH)

    @pl.when(i % tiles_per_seq == 0)
    def _():
        carry_ref[...] = jnp.zeros_like(carry_ref)

    carry = carry_ref[...]
    row = lax.broadcasted_iota(jnp.int32, (tm, CONV_WIDTH), 0)
    prev1 = jnp.where(row == 0, carry[7:8], pltpu.roll(u, 1, 0))
    prev2 = jnp.where(row == 0, carry[6:7], jnp.where(row == 1, carry[7:8], pltpu.roll(u, 2, 0)))
    cw = convw_ref[0]
    yc_ref[...] = (cb * (cw[0:1] * prev2 + cw[1:2] * prev1 + cw[2:3] * u)).astype(BF16)
    carry_ref[...] = u[tm - 8:tm]

    nq = proj(OFF_NQ, NSA_WIDTH) * scale
    qu_ref[...] = nq.astype(BF16)
    qr_ref[...] = rope(nq).astype(BF16)
    kc_ref[...] = proj(OFF_KC, NSA_KV_WIDTH)
    vc_ref[...] = proj(OFF_VC, NSA_KV_WIDTH)
    ks_ref[...] = rope(proj(OFF_KS, 2 * NSA_KV_WIDTH)).astype(BF16)
    vs_ref[...] = proj(OFF_VS, 2 * NSA_KV_WIDTH).astype(BF16)
    kw_ref[...] = rope(proj(OFF_KW, 2 * NSA_KV_WIDTH)).astype(BF16)
    vw_ref[...] = proj(OFF_VW, 2 * NSA_KV_WIDTH).astype(BF16)
    ng_ref[...] = proj(OFF_NG, LANES)


def _input_projection(x, mod_l, w_l, cos, sin, conv_w, layer, seq):
    tokens = x.shape[0]
    tm = ROW_TILE
    tiles_per_seq = seq // tm
    row = lambda w: pl.BlockSpec((tm, w), lambda i: (i, 0))
    widths = [(RET_WIDTH, BF16)] * 4 + [(CONV_WIDTH, BF16), (NSA_WIDTH, BF16), (NSA_WIDTH, BF16),
                                       (NSA_KV_WIDTH, F32), (NSA_KV_WIDTH, F32)] + \
             [(2 * NSA_KV_WIDTH, BF16)] * 4 + [(LANES, F32)]
    return pl.pallas_call(
        functools.partial(_inproj_kernel, tiles_per_seq=tiles_per_seq),
        grid=(tokens // tm,),
        in_specs=[row(D_MODEL),
                  pl.BlockSpec((6, 1, 1, D_MODEL), lambda i: (0, i // tiles_per_seq, 0, 0)),
                  pl.BlockSpec((1, D_MODEL, N_COLS), lambda i: (layer, 0, 0)),
                  row(LANES), row(LANES),
                  pl.BlockSpec((1, CONV_K, CONV_WIDTH), lambda i: (layer, 0, 0))],
        out_specs=[row(w) for w, _ in widths],
        out_shape=[jax.ShapeDtypeStruct((tokens, w), dt) for w, dt in widths],
        scratch_shapes=[pltpu.VMEM((8, CONV_WIDTH), F32)],
        compiler_params=_params("arbitrary"),
        name="ln_inproj",
    )(x, mod_l, w_l, cos, sin, conv_w)


def _retention_kernel(q_ref, k_ref, v_ref, g_ref, intra_ref, qdec_ref, kdec_ref, cdec_ref, gn_ref,
                      o_ref, state_ref):
    @pl.when(pl.program_id(1) == 0)
    def _():
        state_ref[...] = jnp.zeros_like(state_ref)

    n_pairs = RET_WIDTH // LANES
    lane = lax.broadcasted_iota(jnp.int32, (RET_CHUNK, LANES), 1)
    low = lane < HEAD_DIM
    blk_r = lax.broadcasted_iota(jnp.int32, (LANES, LANES), 0) < HEAD_DIM
    blk_c = lax.broadcasted_iota(jnp.int32, (LANES, LANES), 1) < HEAD_DIM
    same_head = blk_r == blk_c
    for ch in range(q_ref.shape[0] // RET_CHUNK):
        rows = slice(ch * RET_CHUNK, (ch + 1) * RET_CHUNK)
        outs = []
        for p in range(n_pairs):
            cols = slice(p * LANES, (p + 1) * LANES)
            qp = q_ref[rows, cols]
            kp = k_ref[rows, cols]
            vp = v_ref[rows, cols]
            out = jnp.zeros((RET_CHUNK, LANES), F32)
            for hh in range(2):
                keep = low if hh == 0 else jnp.logical_not(low)
                qm = jnp.where(keep, qp, jnp.zeros_like(qp))
                s = _dot_nt(qm, kp) * intra_ref[2 * p + hh]
                out = jnp.where(keep, _dot(s.astype(BF16), vp), out)
            state = state_ref[p]
            qd = (qp.astype(F32) * qdec_ref[:, cols]).astype(BF16)
            out = out + _dot(qd, state.astype(BF16))
            kd_t = (kp.astype(F32) * kdec_ref[:, cols]).T.astype(BF16)
            upd = _dot(kd_t, vp)
            state_ref[p] = state * cdec_ref[p] + jnp.where(same_head, upd, 0.0)
            def half_mean(t):
                lo_sum = jnp.sum(jnp.where(low, t, 0.0), axis=-1, keepdims=True)
                hi_sum = jnp.sum(jnp.where(low, 0.0, t), axis=-1, keepdims=True)
                return jnp.where(low, lo_sum, hi_sum) * (1.0 / HEAD_DIM)
            oc = out - half_mean(out)
            outs.append(oc * lax.rsqrt(half_mean(oc * oc) + LN_EPS))
        normed = jnp.concatenate(outs, axis=1)
        o_ref[rows, :] = (normed * gn_ref[0] * g_ref[rows, :].astype(F32)).astype(BF16)


def _retention_consts():
    heads = jnp.arange(RET_HEADS, dtype=F32)
    log_gamma = jnp.log(1.0 - jnp.power(2.0, -5.0 - heads))
    i = jnp.arange(RET_CHUNK, dtype=F32)
    diff = i[:, None] - i[None, :]
    intra = jnp.where(diff >= 0, jnp.exp(diff * log_gamma[:, None, None]), 0.0)
    q_dec = jnp.exp((i + 1.0) * log_gamma[:, None])
    k_dec = jnp.exp((RET_CHUNK - 1.0 - i) * log_gamma[:, None])
    c_dec = jnp.exp(RET_CHUNK * log_gamma)
    expand = lambda t: jnp.repeat(t.T, HEAD_DIM, axis=1)
    c_rows = jnp.repeat(c_dec, HEAD_DIM).reshape(RET_WIDTH // LANES, LANES, 1)
    c_blk = jnp.broadcast_to(c_rows, (RET_WIDTH // LANES, LANES, LANES))
    return intra, expand(q_dec), expand(k_dec), c_blk


def _retention(rq, rk, rv, rg, consts, gn_w, layer, batch, seq):
    intra, q_dec, k_dec, c_blk = consts
    rows = 512
    steps = seq // rows
    blk = pl.BlockSpec((rows, RET_WIDTH), lambda b, s: (b * steps + s, 0))
    full = lambda a: pl.BlockSpec(a.shape, lambda b, s: (0,) * a.ndim)
    return pl.pallas_call(
        _retention_kernel,
        grid=(batch, steps),
        in_specs=[blk, blk, blk, blk, full(intra), full(q_dec), full(k_dec), full(c_blk),
                  pl.BlockSpec((1, 1, RET_WIDTH), lambda b, s: (layer, 0, 0))],
        out_specs=blk,
        out_shape=jax.ShapeDtypeStruct(rq.shape, BF16),
        scratch_shapes=[pltpu.VMEM((RET_WIDTH // LANES, LANES, LANES), F32)],
        compiler_params=_params("arbitrary", "arbitrary"),
        name="retention",
    )(rq, rk, rv, rg, intra, q_dec, k_dec, c_blk, gn_w)


def _compress_kernel(xk_ref, xv_ref, posa_ref, posb_ref, w1a_ref, w1b_ref, w2_ref, kc_ref, vc_ref):
    n_grp = xk_ref.shape[1]
    for kv, (x_ref, o_ref) in enumerate(((xk_ref, kc_ref), (xv_ref, vc_ref))):
        x = x_ref[0]
        ya = _dot((x + posa_ref[0, kv]).astype(BF16), w1a_ref[0, kv])
        yb = _dot((x + posb_ref[0, kv]).astype(BF16), w1b_ref[0, kv])
        hidden = ya + pltpu.roll(yb, n_grp - 1, 0)
        act = jax.nn.gelu(hidden)
        o_ref[0] = _dot(act.astype(BF16), w2_ref[0, kv]).astype(BF16)


def _compress_weights(cmp_pos, cmp_w1, cmp_w2):
    depth = cmp_w1.shape[0]
    eye = jnp.eye(NSA_KV_HEADS, dtype=F32)
    a = cmp_w1.reshape(depth, 2, 2, CMP_STRIDE, HEAD_DIM, CMP_HIDDEN)
    w1 = jnp.einsum('lkardj,hg->lkarhdgj', a, eye)
    w1 = w1.reshape(depth, 2, 2, CMP_STRIDE * NSA_KV_WIDTH, NSA_KV_HEADS * CMP_HIDDEN).astype(BF16)
    pos = cmp_pos.reshape(depth, 2, 2, CMP_STRIDE, 1, HEAD_DIM)
    pos = jnp.broadcast_to(pos, (depth, 2, 2, CMP_STRIDE, NSA_KV_HEADS, HEAD_DIM))
    pos = pos.reshape(depth, 2, 2, 1, CMP_STRIDE * NSA_KV_WIDTH)
    w2 = jnp.einsum('lkje,hg,r->lkhjgre', cmp_w2, eye, jnp.ones((2,), F32))
    w2 = w2.reshape(depth, 2, NSA_KV_HEADS * CMP_HIDDEN, 2 * NSA_KV_WIDTH).astype(BF16)
    return w1[:, :, 0], w1[:, :, 1], pos[:, :, 0], pos[:, :, 1], w2


def _compress(kcx, vcx, cw, layer, batch, seq):
    w1a, w1b, posa, posb, w2 = cw
    n_grp = seq // CMP_STRIDE
    kdim = CMP_STRIDE * NSA_KV_WIDTH
    xk = kcx.reshape(batch, n_grp, kdim)
    xv = vcx.reshape(batch, n_grp, kdim)
    xblk = pl.BlockSpec((1, n_grp, kdim), lambda b: (b, 0, 0))
    lay = lambda a: pl.BlockSpec((1,) + a.shape[1:], lambda b: (layer,) + (0,) * (a.ndim - 1))
    oblk = pl.BlockSpec((1, n_grp, 2 * NSA_KV_WIDTH), lambda b: (b, 0, 0))
    return pl.pallas_call(
        _compress_kernel,
        grid=(batch,),
        in_specs=[xblk, xblk, lay(posa), lay(posb), lay(w1a), lay(w1b), lay(w2)],
        out_specs=[oblk, oblk],
        out_shape=[jax.ShapeDtypeStruct((batch, n_grp, 2 * NSA_KV_WIDTH), BF16)] * 2,
        compiler_params=_params("arbitrary"),
        name="nsa_compress",
    )(xk, xv, posa, posb, w1a, w1b, w2)


def _nsa_kernel(qu_ref, qr_ref, kc_ref, vc_ref, ks_ref, vs_ref, kw_ref, vw_ref, ng_ref, ovt_ref, exp_ref,
                o_ref):
    q0 = pl.program_id(1) * Q_BLOCK
    n_sel = ovt_ref.shape[0]
    rows4 = NSA_GROUP * Q_BLOCK
    low = lax.broadcasted_iota(jnp.int32, (Q_BLOCK, LANES), 1) < HEAD_DIM
    gsig = jax.nn.sigmoid(ng_ref[...])

    def stack(ref, h):
        parts = []
        for g in range(NSA_GROUP):
            hq = h * NSA_GROUP + g
            col = ref[:, (hq // 2) * LANES:(hq // 2 + 1) * LANES]
            keep = low if hq % 2 == 0 else jnp.logical_not(low)
            parts.append(jnp.where(keep, col, jnp.zeros_like(col)))
        return jnp.concatenate(parts, axis=0)

    def tile4(t):
        return jnp.concatenate([t] * NSA_GROUP, axis=0)

    cols_out = []
    for h in range(NSA_KV_HEADS):
        hc = slice(h * LANES, (h + 1) * LANES)
        qus = stack(qu_ref, h)
        qrs = stack(qr_ref, h)

        s = _dot_nt(qus, kc_ref[0, :, hc])
        tq = q0 + lax.broadcasted_iota(jnp.int32, (Q_BLOCK, LANES), 0)
        cid = lax.broadcasted_iota(jnp.int32, (Q_BLOCK, LANES), 1)
        cvalid = tile4(jnp.where(cid * CMP_STRIDE + (CMP_LEN - 1) <= tq, 1.0, 0.0))
        sm = jnp.where(cvalid > 0.5, s, NEG_INF)
        e = jnp.exp(sm - jnp.max(sm, axis=-1, keepdims=True)) * cvalid
        l = jnp.sum(e, axis=-1, keepdims=True)
        p = e * (1.0 / jnp.where(l > 0.0, l, 1.0))
        o_c = _dot(p.astype(BF16), vc_ref[0, :, hc])

        psum = p[0:Q_BLOCK]
        for g in range(1, NSA_GROUP):
            psum = psum + p[g * Q_BLOCK:(g + 1) * Q_BLOCK]
        p_hi = psum.astype(BF16)
        rem = psum - p_hi.astype(F32)
        p_mid = rem.astype(BF16)
        p_lo = (rem - p_mid.astype(F32)).astype(BF16)
        ovt = ovt_ref[...]
        imp = _dot_nt(ovt, p_hi) + _dot_nt(ovt, p_mid) + _dot_nt(ovt, p_lo)
        jid = lax.broadcasted_iota(jnp.int32, (n_sel, Q_BLOCK), 0)
        tid = q0 + lax.broadcasted_iota(jnp.int32, (n_sel, Q_BLOCK), 1)
        forced = (jid == 0) | (jid == jnp.right_shift(tid, 6))
        imp = jnp.where(forced, FORCE_SCORE, jnp.where(jid * SEL_LEN <= tid, imp, -FORCE_SCORE))
        rank = jnp.zeros((n_sel, Q_BLOCK), F32)
        for i in range(n_sel):
            ri = imp[i:i + 1, :]
            beats = (ri > imp) | ((ri == imp) & (jid > i))
            rank = rank + jnp.where(beats, 1.0, 0.0)
        sel_t = jnp.where(rank < float(SEL_TOP), 1.0, 0.0)
        sel = jnp.concatenate([sel_t, jnp.zeros((LANES - n_sel, Q_BLOCK), F32)], axis=0).T.astype(BF16)

        def sel_step(ci, carry):
            m, l, acc = carry
            k0 = pl.multiple_of(ci * SEL_CHUNK, SEL_CHUNK)
            s = _dot_nt(qrs, ks_ref[pl.ds(k0, SEL_CHUNK), hc])
            member = _dot(sel, exp_ref[ci])
            kpos = k0 + lax.broadcasted_iota(jnp.int32, (Q_BLOCK, SEL_CHUNK), 1)
            tq = q0 + lax.broadcasted_iota(jnp.int32, (Q_BLOCK, SEL_CHUNK), 0)
            bias = jnp.where((member > 0.5) & (kpos <= tq), 0.0, NEG_INF)
            sm = s + tile4(bias)
            m_new = jnp.maximum(m, jnp.max(sm, axis=-1, keepdims=True))
            alpha = jnp.exp(m - m_new)
            e = jnp.exp(sm - m_new)
            l = alpha * l + jnp.sum(e, axis=-1, keepdims=True)
            acc = alpha * acc + _dot(e.astype(BF16), vs_ref[pl.ds(k0, SEL_CHUNK), hc])
            return m_new, l, acc

        n_chunks = (q0 + Q_BLOCK + SEL_CHUNK - 1) // SEL_CHUNK
        init = (jnp.full((rows4, 1), NEG_INF, F32), jnp.zeros((rows4, 1), F32), jnp.zeros((rows4, LANES), F32))
        _, l_s, acc_s = lax.fori_loop(0, n_chunks, sel_step, init)
        o_s = acc_s * (1.0 / l_s)

        ws = pl.multiple_of(jnp.maximum(q0 - WINDOW, 0), Q_BLOCK)
        s = _dot_nt(qrs, kw_ref[pl.ds(ws, WIN_KEYS), hc])
        kpos = ws + lax.broadcasted_iota(jnp.int32, (Q_BLOCK, WIN_KEYS), 1)
        tq = q0 + lax.broadcasted_iota(jnp.int32, (Q_BLOCK, WIN_KEYS), 0)
        bias = jnp.where((kpos <= tq) & (kpos > tq - WINDOW), 0.0, NEG_INF)
        sm = s + tile4(bias)
        e = jnp.exp(sm - jnp.max(sm, axis=-1, keepdims=True))
        l_w = jnp.sum(e, axis=-1, keepdims=True)
        o_w = _dot(e.astype(BF16), vw_ref[pl.ds(ws, WIN_KEYS), hc]) * (1.0 / l_w)

        heads = []
        for g in range(NSA_GROUP):
            hq = h * NSA_GROUP + g
            r = slice(g * Q_BLOCK, (g + 1) * Q_BLOCK)
            heads.append(gsig[:, 3 * hq:3 * hq + 1] * o_c[r] + gsig[:, 3 * hq + 1:3 * hq + 2] * o_s[r]
                         + gsig[:, 3 * hq + 2:3 * hq + 3] * o_w[r])
        cols_out.append(jnp.where(low, heads[0], heads[1]))
        cols_out.append(jnp.where(low, heads[2], heads[3]))
    o_ref[...] = jnp.concatenate(cols_out, axis=1).astype(BF16)


def _nsa_consts(seq):
    n_cmp_pad = seq // CMP_STRIDE
    n_sel = seq // SEL_LEN
    c = np.arange(n_cmp_pad)
    j = np.arange(n_sel)
    n_cmp = (seq - CMP_LEN) // CMP_STRIDE + 1
    ov = ((c[None, :] * CMP_STRIDE < j[:, None] * SEL_LEN + SEL_LEN) &
          (j[:, None] * SEL_LEN <= c[None, :] * CMP_STRIDE + CMP_LEN - 1) & (c[None, :] < n_cmp))
    n_chunks = seq // SEL_CHUNK
    k = np.arange(SEL_CHUNK)
    jj = np.arange(LANES)
    expand = np.stack([(jj[:, None] == ci * (SEL_CHUNK // SEL_LEN) + k[None, :] // SEL_LEN)
                       for ci in range(n_chunks)])
    return jnp.asarray(ov, BF16), jnp.asarray(expand, BF16)


def _nsa(qu, qr, kc, vc, ks, vs, kw, vw, ng, consts, batch, seq):
    ovt, expand = consts
    n_qb = seq // Q_BLOCK
    qblk = pl.BlockSpec((Q_BLOCK, NSA_WIDTH), lambda b, q: (b * n_qb + q, 0))
    cblk = pl.BlockSpec((1, seq // CMP_STRIDE, 2 * NSA_KV_WIDTH), lambda b, q: (b, 0, 0))
    kvblk = pl.BlockSpec((seq, 2 * NSA_KV_WIDTH), lambda b, q: (b, 0))
    full = lambda a: pl.BlockSpec(a.shape, lambda b, q: (0,) * a.ndim)
    return pl.pallas_call(
        _nsa_kernel,
        grid=(batch, n_qb),
        in_specs=[qblk, qblk, cblk, cblk, kvblk, kvblk, kvblk, kvblk,
                  pl.BlockSpec((Q_BLOCK, LANES), lambda b, q: (b * n_qb + q, 0)), full(ovt), full(expand)],
        out_specs=qblk,
        out_shape=jax.ShapeDtypeStruct(qu.shape, BF16),
        compiler_params=_params("arbitrary", "arbitrary"),
        name="nsa_attention",
    )(qu, qr, kc, vc, ks, vs, kw, vw, ng, ovt, expand)


def _outproj_kernel(yr_ref, yc_ref, yn_ref, x_ref, mod_ref, w_ref, lng_ref, lnb_ref, rwh_ref, rwl_ref, rb_ref,
                    x1_ref, h2_ref, lg_ref, *, alpha):
    mix = (_dot(yr_ref[...], w_ref[0, 0:RET_WIDTH, :])
           + _dot(yc_ref[...], w_ref[0, RET_WIDTH:RET_WIDTH + CONV_WIDTH, :])
           + _dot(yn_ref[...], w_ref[0, RET_WIDTH + CONV_WIDTH:, :]))
    x1 = _layer_norm(alpha * x_ref[...] + (1.0 + mod_ref[2, 0]) * mix) * lng_ref[0, 0] + lnb_ref[0, 0]
    x1_ref[...] = x1
    h2 = _layer_norm(x1) * (1.0 + mod_ref[4, 0]) + mod_ref[3, 0]
    h_hi = h2.astype(BF16)
    h2_ref[...] = h_hi
    h_lo = (h2 - h_hi.astype(F32)).astype(BF16)
    lg_ref[...] = _dot(h_hi, rwh_ref[0]) + _dot(h_lo, rwh_ref[0]) + _dot(h_hi, rwl_ref[0]) + rb_ref[0]


def _out_projection(y_ret, y_conv, y_nsa, x, mod_l, w_out, ln_g, ln_b, rw_hi, rw_lo, rb, layer, seq, alpha):
    tokens = x.shape[0]
    tm = ROW_TILE
    tiles_per_seq = seq // tm
    row = lambda w: pl.BlockSpec((tm, w), lambda i: (i, 0))
    lay3 = lambda a: pl.BlockSpec((1,) + a.shape[1:], lambda i: (layer,) + (0,) * (a.ndim - 1))
    return pl.pallas_call(
        functools.partial(_outproj_kernel, alpha=alpha),
        grid=(tokens // tm,),
        in_specs=[row(RET_WIDTH), row(CONV_WIDTH), row(NSA_WIDTH), row(D_MODEL),
                  pl.BlockSpec((6, 1, 1, D_MODEL), lambda i: (0, i // tiles_per_seq, 0, 0)),
                  lay3(w_out),
                  pl.BlockSpec((1, 1, 1, D_MODEL), lambda i: (layer, 0, 0, 0)),
                  pl.BlockSpec((1, 1, 1, D_MODEL), lambda i: (layer, 0, 0, 0)),
                  lay3(rw_hi), lay3(rw_lo), lay3(rb)],
        out_specs=[row(D_MODEL), row(D_MODEL), row(LANES)],
        out_shape=[jax.ShapeDtypeStruct((tokens, D_MODEL), F32), jax.ShapeDtypeStruct((tokens, D_MODEL), BF16),
                   jax.ShapeDtypeStruct((tokens, LANES), F32)],
        compiler_params=_params("arbitrary"),
        name="outproj_norm_router",
    )(y_ret, y_conv, y_nsa, x, mod_l, w_out, ln_g, ln_b, rw_hi, rw_lo, rb)


def _moe_kernel(be_ref, nb_ref, rows_ref, wgu_ref, bgu_ref, wd_ref, bd_ref, o_ref, wgu_bf, wd_bf):
    i = pl.program_id(0)
    expert = be_ref[i]
    first = (i == 0) | (expert != be_ref[jnp.maximum(i - 1, 0)])
    valid = i < nb_ref[0]

    @pl.when(valid & first)
    def _():
        wgu_bf[...] = wgu_ref[0, 0].astype(BF16)
        wd_bf[...] = wd_ref[0, 0].astype(BF16)

    @pl.when(valid)
    def _():
        gu = _dot(rows_ref[...], wgu_bf[...]) + bgu_ref[0, 0]
        g = jnp.minimum(gu[:, :D_EXPERT], SWIGLU_LIMIT)
        u = jnp.clip(gu[:, D_EXPERT:], -SWIGLU_LIMIT, SWIGLU_LIMIT)
        act = (u + 1.0) * (g * jax.nn.sigmoid(SWIGLU_ALPHA * g))
        o_ref[...] = _dot(act.astype(BF16), wd_bf[...]) + bd_ref[0, 0]

    @pl.when(jnp.logical_not(valid))
    def _():
        o_ref[...] = jnp.zeros_like(o_ref)


def _expert_ffn(rows, block_e, n_used, w_gu, b_gu, w_down, b_down, layer):
    n_rows = rows.shape[0]
    tm = MOE_TILE
    grid_spec = pltpu.PrefetchScalarGridSpec(
        num_scalar_prefetch=2,
        grid=(n_rows // tm,),
        in_specs=[pl.BlockSpec((tm, D_MODEL), lambda i, be, nb: (i, 0)),
                  pl.BlockSpec((1, 1, D_MODEL, 2 * D_EXPERT), lambda i, be, nb: (layer, be[i], 0, 0)),
                  pl.BlockSpec((1, 1, 1, 2 * D_EXPERT), lambda i, be, nb: (layer, be[i], 0, 0)),
                  pl.BlockSpec((1, 1, D_EXPERT, D_MODEL), lambda i, be, nb: (layer, be[i], 0, 0)),
                  pl.BlockSpec((1, 1, 1, D_MODEL), lambda i, be, nb: (layer, be[i], 0, 0))],
        out_specs=pl.BlockSpec((tm, D_MODEL), lambda i, be, nb: (i, 0)),
        scratch_shapes=[pltpu.VMEM((D_MODEL, 2 * D_EXPERT), BF16), pltpu.VMEM((D_EXPERT, D_MODEL), BF16)])
    return pl.pallas_call(
        _moe_kernel,
        grid_spec=grid_spec,
        out_shape=jax.ShapeDtypeStruct((n_rows, D_MODEL), F32),
        compiler_params=_params("arbitrary"),
        name="expert_ffn",
    )(block_e, n_used, rows, w_gu, b_gu, w_down, b_down)


def _route(logits, tokens):
    tm = MOE_TILE
    top_val, top_idx = lax.top_k(logits, TOP_K)
    gate = jax.nn.softmax(top_val, axis=-1)
    flat_e = top_idx.reshape(-1)
    order = jnp.argsort(flat_e)
    se = flat_e[order]
    counts = jnp.bincount(flat_e, length=N_EXPERTS)
    start = jnp.cumsum(counts) - counts
    padded = (counts + tm - 1) // tm * tm
    pend = jnp.cumsum(padded)
    pstart = pend - padded
    n_pairs = tokens * TOP_K
    dest = (pstart[se] + jnp.arange(n_pairs) - start[se]).astype(jnp.int32)
    n_rows = n_pairs + N_EXPERTS * tm
    n_blocks = n_rows // tm
    row_tok = jnp.zeros((n_rows,), jnp.int32).at[dest].set((order // TOP_K).astype(jnp.int32))
    pos = jnp.zeros((n_pairs,), jnp.int32).at[order].set(dest).reshape(tokens, TOP_K)
    block_e = jnp.minimum(jnp.searchsorted(pend, jnp.arange(n_blocks) * tm, side='right'),
                          N_EXPERTS - 1).astype(jnp.int32)
    n_used = (pend[-1:] // tm).astype(jnp.int32)
    return gate, row_tok, pos, block_e, n_used


def _final_kernel(x_ref, f_ref, mod_ref, lng_ref, lnb_ref, o_ref, *, alpha):
    o_ref[...] = (_layer_norm(alpha * x_ref[...] + (1.0 + mod_ref[5, 0]) * f_ref[...]) * lng_ref[0, 0]
                  + lnb_ref[0, 0])


def _final_norm(x1, ffn, mod_l, ln_g, ln_b, layer, seq, alpha):
    tokens = x1.shape[0]
    tm = 512
    tiles_per_seq = seq // tm
    row = pl.BlockSpec((tm, D_MODEL), lambda i: (i, 0))
    vec = pl.BlockSpec((1, 1, 1, D_MODEL), lambda i: (layer, 1, 0, 0))
    return pl.pallas_call(
        functools.partial(_final_kernel, alpha=alpha),
        grid=(tokens // tm,),
        in_specs=[row, row, pl.BlockSpec((6, 1, 1, D_MODEL), lambda i: (0, i // tiles_per_seq, 0, 0)), vec, vec],
        out_specs=row,
        out_shape=jax.ShapeDtypeStruct(x1.shape, F32),
        compiler_params=_params("arbitrary"),
        name="final_norm",
    )(x1, ffn, mod_l, ln_g, ln_b)


def _inproj_columns():
    def dup(off):
        a = np.arange(off, off + HEAD_DIM)
        b = np.arange(off + HEAD_DIM, off + 2 * HEAD_DIM)
        return np.concatenate([a, a, b, b])
    src = [np.arange(0, 2304), np.arange(2304, 2560), dup(2560), dup(2688), dup(2816), dup(2944),
           np.arange(3072, 3072 + N_GATE), np.zeros((LANES - N_GATE,), np.int64)]
    cols = np.concatenate(src)
    live = np.ones((N_COLS,), np.float32)
    live[OFF_NG + N_GATE:] = 0.0
    return cols.astype(np.int32), live


def kernel(x, c, positions, w_in, w_out, ret_gn_w, conv_w, cmp_pos, cmp_w1, cmp_w2, ada_w, ada_b, ln_g, ln_b,
           router_w, router_b, w_gate_up, b_gate_up, w_down, b_down):
    batch, seq, _ = x.shape
    depth = w_in.shape[0]
    tokens = batch * seq
    alpha = float((2 * depth) ** 0.25)

    cols, live = _inproj_columns()
    w_in_r = (jnp.take(w_in, jnp.asarray(cols), axis=2) * jnp.asarray(live)).astype(BF16)
    w_out_b = w_out.astype(BF16)
    rw = jnp.pad(router_w, ((0, 0), (0, 0), (0, LANES - N_EXPERTS)))
    rw_hi = rw.astype(BF16)
    rw_lo = (rw - rw_hi.astype(F32)).astype(BF16)
    rb = jnp.pad(router_b, ((0, 0), (0, LANES - N_EXPERTS))).reshape(depth, 1, LANES)
    cw = _compress_weights(cmp_pos, cmp_w1, cmp_w2)
    ret_consts = _retention_consts()
    nsa_consts = _nsa_consts(seq)
    gn_w = ret_gn_w.reshape(depth, 1, RET_WIDTH)
    ln_g4 = ln_g.reshape(depth, 2, 1, D_MODEL)
    ln_b4 = ln_b.reshape(depth, 2, 1, D_MODEL)
    b_gu4 = b_gate_up.reshape(depth, N_EXPERTS, 1, 2 * D_EXPERT)
    b_d4 = b_down.reshape(depth, N_EXPERTS, 1, D_MODEL)

    mod = _modulation(c, ada_w, ada_b)
    cos, sin = _rope_tables(positions)
    xt = x.reshape(tokens, D_MODEL)
    for l in range(depth):
        (rq, rk, rv, rg, y_conv, qu, qr, kcx, vcx, ks, vs, kw, vw, ng) = _input_projection(
            xt, mod[l], w_in_r, cos, sin, conv_w, l, seq)
        y_ret = _retention(rq, rk, rv, rg, ret_consts, gn_w, l, batch, seq)
        kc, vc = _compress(kcx, vcx, cw, l, batch, seq)
        y_nsa = _nsa(qu, qr, kc, vc, ks, vs, kw, vw, ng, nsa_consts, batch, seq)
        x1, h2, logits = _out_projection(y_ret, y_conv, y_nsa, xt, mod[l], w_out_b, ln_g4, ln_b4,
                                         rw_hi, rw_lo, rb, l, seq, alpha)
        gate, row_tok, pos, block_e, n_used = _route(logits[:, :N_EXPERTS], tokens)
        rows = jnp.take(h2, row_tok, axis=0)
        y = _expert_ffn(rows, block_e, n_used, w_gate_up, b_gu4, w_down, b_d4, l)
        ffn = jnp.sum(jnp.take(y, pos, axis=0) * gate[:, :, None], axis=1)
        xt = _final_norm(x1, ffn, mod[l], ln_g4, ln_b4, l, seq, alpha)
    return xt.reshape(batch, seq, D_MODEL)
```

```python
import functools

import numpy as np
import jax
import jax.numpy as jnp
from jax import lax
from jax.experimental import pallas as pl
from jax.experimental.pallas import tpu as pltpu
from jax.experimental.pallas import tpu_sc as plsc

F32 = jnp.float32
BF16 = jnp.bfloat16

D_MODEL = 1024
HEAD_DIM = 64
RET_WIDTH = 256
RET_HEADS = 4
RET_CHUNK = 128
CONV_WIDTH = 256
CONV_K = 3
NSA_WIDTH = 512
NSA_HEADS = 8
NSA_KV_HEADS = 2
NSA_GROUP = 4
NSA_KV_WIDTH = 128
CMP_LEN = 32
CMP_STRIDE = 16
CMP_HIDDEN = 128
SEL_LEN = 64
SEL_TOP = 8
WINDOW = 512
Q_BLOCK = 128
ROPE_THETA = 10000.0
N_EXPERTS = 32
TOP_K = 4
D_EXPERT = 1024
SWIGLU_LIMIT = 7.0
SWIGLU_ALPHA = 1.702
LN_EPS = 1e-5
NEG_INF = -1e30
FORCE_SCORE = 1e9

LANES = 128
VMEM_LIMIT = 56 * 1024 * 1024

OFF_RQ, OFF_RK, OFF_RV, OFF_RG = 0, 256, 512, 768
OFF_CB, OFF_CC, OFF_CH = 1024, 1280, 1536
OFF_NQ = 1792
OFF_KC, OFF_VC = 2304, 2432
OFF_KS, OFF_VS, OFF_KW, OFF_VW = 2560, 2816, 3072, 3328
OFF_NG = 3584
N_COLS = 3712
N_GATE = NSA_HEADS * 3

ROW_TILE = 256
MOE_TILE = 256
SEL_CHUNK = 512
WIN_KEYS = WINDOW + Q_BLOCK
SC_WINDOW = LANES
SC_SPLIT = 4


def _dot(a, b):
    return jnp.dot(a, b, preferred_element_type=F32)


def _dot_nt(a, b):
    return lax.dot_general(a, b, (((1,), (1,)), ((), ())), preferred_element_type=F32)


def _layer_norm(x):
    mu = jnp.mean(x, axis=-1, keepdims=True)
    xc = x - mu
    var = jnp.mean(xc * xc, axis=-1, keepdims=True)
    return xc * lax.rsqrt(var + LN_EPS)


def _params(*sem):
    return pltpu.CompilerParams(dimension_semantics=sem, vmem_limit_bytes=VMEM_LIMIT)


def _mod_kernel(c_ref, w_ref, b_ref, o_ref):
    c = c_ref[...]
    ca = (c * jax.nn.sigmoid(c)).astype(BF16)
    o_ref[0, 0] = _dot(ca, w_ref[0].astype(BF16)) + b_ref[0]


def _modulation(c, ada_w, ada_b):
    depth = ada_w.shape[0]
    batch = c.shape[0]
    out = pl.pallas_call(
        _mod_kernel,
        grid=(depth, 6),
        in_specs=[pl.BlockSpec((batch, D_MODEL), lambda l, j: (0, 0)),
                  pl.BlockSpec((1, D_MODEL, D_MODEL), lambda l, j: (l, 0, j)),
                  pl.BlockSpec((1, 1, D_MODEL), lambda l, j: (l * 6 + j, 0, 0))],
        out_specs=pl.BlockSpec((1, 1, batch, D_MODEL), lambda l, j: (l, j, 0, 0)),
        out_shape=jax.ShapeDtypeStruct((depth, 6, batch, D_MODEL), F32),
        compiler_params=_params("arbitrary", "arbitrary"),
        name="adaln_mod",
    )(c, ada_w, ada_b.reshape(depth * 6, 1, D_MODEL))
    return out.reshape(depth, 6, batch, 1, D_MODEL)


def _rope_table_kernel(pos_ref, inv_ref, cos_ref, sin_ref):
    ang = pos_ref[...] * inv_ref[...]
    cos_ref[...] = jnp.cos(ang)
    sin_ref[...] = jnp.sin(ang)


def _rope_tables(positions):
    half = HEAD_DIM // 2
    per_row = LANES // half
    tokens = positions.size
    pos4 = jnp.repeat(positions.reshape(tokens // per_row, per_row).astype(F32), half, axis=1)
    inv = ROPE_THETA ** (-jnp.arange(half, dtype=F32) / half)
    inv4 = jnp.tile(inv, per_row)[None, :]
    rows = tokens // per_row
    tile = min(rows, 1024)
    cos4, sin4 = pl.pallas_call(
        _rope_table_kernel,
        grid=(rows // tile,),
        in_specs=[pl.BlockSpec((tile, LANES), lambda i: (i, 0)),
                  pl.BlockSpec((1, LANES), lambda i: (0, 0))],
        out_specs=[pl.BlockSpec((tile, LANES), lambda i: (i, 0))] * 2,
        out_shape=[jax.ShapeDtypeStruct((rows, LANES), F32)] * 2,
        compiler_params=_params("arbitrary"),
        name="rope_tables",
    )(pos4, inv4)
    cos = jnp.tile(cos4.reshape(tokens, half), (1, per_row))
    sign = jnp.tile(jnp.concatenate([-jnp.ones((half,), F32), jnp.ones((half,), F32)]), LANES // HEAD_DIM)
    sin = jnp.tile(sin4.reshape(tokens, half), (1, per_row)) * sign[None, :]
    return cos, sin


def _inproj_kernel(x_ref, mod_ref, w_ref, cos_ref, sin_ref, convw_ref,
                   rq_ref, rk_ref, rv_ref, rg_ref, yc_ref, qu_ref, qr_ref, kc_ref, vc_ref,
                   ks_ref, vs_ref, kw_ref, vw_ref, ng_ref, carry_ref, *, tiles_per_seq):
    i = pl.program_id(0)
    tm = x_ref.shape[0]
    h = (_layer_norm(x_ref[...]) * (1.0 + mod_ref[1, 0]) + mod_ref[0, 0]).astype(BF16)
    cosf = cos_ref[...]
    sinf = sin_ref[...]
    lane = lax.broadcasted_iota(jnp.int32, (tm, LANES), 1)
    first_half = (lane % HEAD_DIM) < (HEAD_DIM // 2)

    def proj(off, width):
        return _dot(h, w_ref[0, :, off:off + width])

    def rope(c):
        cols = []
        for j in range(c.shape[1] // LANES):
            cj = c[:, j * LANES:(j + 1) * LANES]
            swapped = jnp.where(first_half, pltpu.roll(cj, LANES - HEAD_DIM // 2, 1),
                                pltpu.roll(cj, HEAD_DIM // 2, 1))
            cols.append(cj * cosf + swapped * sinf)
        return jnp.concatenate(cols, axis=1) if len(cols) > 1 else cols[0]

    scale = HEAD_DIM ** -0.5
    rq_ref[...] = rope(proj(OFF_RQ, RET_WIDTH)).astype(BF16)
    rk_ref[...] = (rope(proj(OFF_RK, RET_WIDTH)) * scale).astype(BF16)
    rv_ref[...] = proj(OFF_RV, RET_WIDTH).astype(BF16)
    rg = proj(OFF_RG, RET_WIDTH)
    rg_ref[...] = (rg * jax.nn.sigmoid(rg)).astype(BF16)

    cb = proj(OFF_CB, CONV_WIDTH)
    u = proj(OFF_CC, CONV_WIDTH) * proj(OFF_CH, CONV_WIDTH)

    @pl.when(i % tiles_per_seq == 0)
    def _():
        carry_ref[...] = jnp.zeros_like(carry_ref)

    carry = carry_ref[...]
    row = lax.broadcasted_iota(jnp.int32, (tm, CONV_WIDTH), 0)
    prev1 = jnp.where(row == 0, carry[7:8], pltpu.roll(u, 1, 0))
    prev2 = jnp.where(row == 0, carry[6:7], jnp.where(row == 1, carry[7:8], pltpu.roll(u, 2, 0)))
    cw = convw_ref[0]
    yc_ref[...] = (cb * (cw[0:1] * prev2 + cw[1:2] * prev1 + cw[2:3] * u)).astype(BF16)
    carry_ref[...] = u[tm - 8:tm]

    nq = proj(OFF_NQ, NSA_WIDTH) * scale
    qu_ref[...] = nq.astype(BF16)
    qr_ref[...] = rope(nq).astype(BF16)
    kc_ref[...] = proj(OFF_KC, NSA_KV_WIDTH)
    vc_ref[...] = proj(OFF_VC, NSA_KV_WIDTH)
    ks_ref[...] = rope(proj(OFF_KS, 2 * NSA_KV_WIDTH)).astype(BF16)
    vs_ref[...] = proj(OFF_VS, 2 * NSA_KV_WIDTH).astype(BF16)
    kw_ref[...] = rope(proj(OFF_KW, 2 * NSA_KV_WIDTH)).astype(BF16)
    vw_ref[...] = proj(OFF_VW, 2 * NSA_KV_WIDTH).astype(BF16)
    ng_ref[...] = proj(OFF_NG, LANES)


def _input_projection(x, mod_l, w_l, cos, sin, conv_w, layer, seq):
    tokens = x.shape[0]
    tm = ROW_TILE
    tiles_per_seq = seq // tm
    row = lambda w: pl.BlockSpec((tm, w), lambda i: (i, 0))
    widths = [(RET_WIDTH, BF16)] * 4 + [(CONV_WIDTH, BF16), (NSA_WIDTH, BF16), (NSA_WIDTH, BF16),
                                       (NSA_KV_WIDTH, F32), (NSA_KV_WIDTH, F32)] + \
             [(2 * NSA_KV_WIDTH, BF16)] * 4 + [(LANES, F32)]
    return pl.pallas_call(
        functools.partial(_inproj_kernel, tiles_per_seq=tiles_per_seq),
        grid=(tokens // tm,),
        in_specs=[row(D_MODEL),
                  pl.BlockSpec((6, 1, 1, D_MODEL), lambda i: (0, i // tiles_per_seq, 0, 0)),
                  pl.BlockSpec((1, D_MODEL, N_COLS), lambda i: (layer, 0, 0)),
                  row(LANES), row(LANES),
                  pl.BlockSpec((1, CONV_K, CONV_WIDTH), lambda i: (layer, 0, 0))],
        out_specs=[row(w) for w, _ in widths],
        out_shape=[jax.ShapeDtypeStruct((tokens, w), dt) for w, dt in widths],
        scratch_shapes=[pltpu.VMEM((8, CONV_WIDTH), F32)],
        compiler_params=_params("arbitrary"),
        name="ln_inproj",
    )(x, mod_l, w_l, cos, sin, conv_w)


def _retention_kernel(q_ref, k_ref, v_ref, g_ref, intra_ref, qdec_ref, kdec_ref, cdec_ref, gn_ref,
                      o_ref, state_ref):
    @pl.when(pl.program_id(1) == 0)
    def _():
        state_ref[...] = jnp.zeros_like(state_ref)

    n_pairs = RET_WIDTH // LANES
    lane = lax.broadcasted_iota(jnp.int32, (RET_CHUNK, LANES), 1)
    low = lane < HEAD_DIM
    blk_r = lax.broadcasted_iota(jnp.int32, (LANES, LANES), 0) < HEAD_DIM
    blk_c = lax.broadcasted_iota(jnp.int32, (LANES, LANES), 1) < HEAD_DIM
    same_head = blk_r == blk_c
    for ch in range(q_ref.shape[0] // RET_CHUNK):
        rows = slice(ch * RET_CHUNK, (ch + 1) * RET_CHUNK)
        outs = []
        for p in range(n_pairs):
            cols = slice(p * LANES, (p + 1) * LANES)
            qp = q_ref[rows, cols]
            kp = k_ref[rows, cols]
            vp = v_ref[rows, cols]
            out = jnp.zeros((RET_CHUNK, LANES), F32)
            for hh in range(2):
                keep = low if hh == 0 else jnp.logical_not(low)
                qm = jnp.where(keep, qp, jnp.zeros_like(qp))
                s = _dot_nt(qm, kp) * intra_ref[2 * p + hh]
                out = jnp.where(keep, _dot(s.astype(BF16), vp), out)
            state = state_ref[p]
            qd = (qp.astype(F32) * qdec_ref[:, cols]).astype(BF16)
            out = out + _dot(qd, state.astype(BF16))
            kd_t = (kp.astype(F32) * kdec_ref[:, cols]).T.astype(BF16)
            upd = _dot(kd_t, vp)
            state_ref[p] = state * cdec_ref[p] + jnp.where(same_head, upd, 0.0)
            def half_mean(t):
                lo_sum = jnp.sum(jnp.where(low, t, 0.0), axis=-1, keepdims=True)
                hi_sum = jnp.sum(jnp.where(low, 0.0, t), axis=-1, keepdims=True)
                return jnp.where(low, lo_sum, hi_sum) * (1.0 / HEAD_DIM)
            oc = out - half_mean(out)
            outs.append(oc * lax.rsqrt(half_mean(oc * oc) + LN_EPS))
        normed = jnp.concatenate(outs, axis=1)
        o_ref[rows, :] = (normed * gn_ref[0] * g_ref[rows, :].astype(F32)).astype(BF16)


def _retention_consts():
    heads = jnp.arange(RET_HEADS, dtype=F32)
    log_gamma = jnp.log(1.0 - jnp.power(2.0, -5.0 - heads))
    i = jnp.arange(RET_CHUNK, dtype=F32)
    diff = i[:, None] - i[None, :]
    intra = jnp.where(diff >= 0, jnp.exp(diff * log_gamma[:, None, None]), 0.0)
    q_dec = jnp.exp((i + 1.0) * log_gamma[:, None])
    k_dec = jnp.exp((RET_CHUNK - 1.0 - i) * log_gamma[:, None])
    c_dec = jnp.exp(RET_CHUNK * log_gamma)
    expand = lambda t: jnp.repeat(t.T, HEAD_DIM, axis=1)
    c_rows = jnp.repeat(c_dec, HEAD_DIM).reshape(RET_WIDTH // LANES, LANES, 1)
    c_blk = jnp.broadcast_to(c_rows, (RET_WIDTH // LANES, LANES, LANES))
    return intra, expand(q_dec), expand(k_dec), c_blk


def _retention(rq, rk, rv, rg, consts, gn_w, layer, batch, seq):
    intra, q_dec, k_dec, c_blk = consts
    rows = 512
    steps = seq // rows
    blk = pl.BlockSpec((rows, RET_WIDTH), lambda b, s: (b * steps + s, 0))
    full = lambda a: pl.BlockSpec(a.shape, lambda b, s: (0,) * a.ndim)
    return pl.pallas_call(
        _retention_kernel,
        grid=(batch, steps),
        in_specs=[blk, blk, blk, blk, full(intra), full(q_dec), full(k_dec), full(c_blk),
                  pl.BlockSpec((1, 1, RET_WIDTH), lambda b, s: (layer, 0, 0))],
        out_specs=blk,
        out_shape=jax.ShapeDtypeStruct(rq.shape, BF16),
        scratch_shapes=[pltpu.VMEM((RET_WIDTH // LANES, LANES, LANES), F32)],
        compiler_params=_params("arbitrary", "arbitrary"),
        name="retention",
    )(rq, rk, rv, rg, intra, q_dec, k_dec, c_blk, gn_w)


def _compress_kernel(xk_ref, xv_ref, posa_ref, posb_ref, w1a_ref, w1b_ref, w2_ref, kc_ref, vc_ref):
    n_grp = xk_ref.shape[1]
    for kv, (x_ref, o_ref) in enumerate(((xk_ref, kc_ref), (xv_ref, vc_ref))):
        x = x_ref[0]
        ya = _dot((x + posa_ref[0, kv]).astype(BF16), w1a_ref[0, kv])
        yb = _dot((x + posb_ref[0, kv]).astype(BF16), w1b_ref[0, kv])
        hidden = ya + pltpu.roll(yb, n_grp - 1, 0)
        act = jax.nn.gelu(hidden)
        o_ref[0] = _dot(act.astype(BF16), w2_ref[0, kv]).astype(BF16)


def _compress_weights(cmp_pos, cmp_w1, cmp_w2):
    depth = cmp_w1.shape[0]
    eye = jnp.eye(NSA_KV_HEADS, dtype=F32)
    a = cmp_w1.reshape(depth, 2, 2, CMP_STRIDE, HEAD_DIM, CMP_HIDDEN)
    w1 = jnp.einsum('lkardj,hg->lkarhdgj', a, eye)
    w1 = w1.reshape(depth, 2, 2, CMP_STRIDE * NSA_KV_WIDTH, NSA_KV_HEADS * CMP_HIDDEN).astype(BF16)
    pos = cmp_pos.reshape(depth, 2, 2, CMP_STRIDE, 1, HEAD_DIM)
    pos = jnp.broadcast_to(pos, (depth, 2, 2, CMP_STRIDE, NSA_KV_HEADS, HEAD_DIM))
    pos = pos.reshape(depth, 2, 2, 1, CMP_STRIDE * NSA_KV_WIDTH)
    w2 = jnp.einsum('lkje,hg,r->lkhjgre', cmp_w2, eye, jnp.ones((2,), F32))
    w2 = w2.reshape(depth, 2, NSA_KV_HEADS * CMP_HIDDEN, 2 * NSA_KV_WIDTH).astype(BF16)
    return w1[:, :, 0], w1[:, :, 1], pos[:, :, 0], pos[:, :, 1], w2


def _compress(kcx, vcx, cw, layer, batch, seq):
    w1a, w1b, posa, posb, w2 = cw
    n_grp = seq // CMP_STRIDE
    kdim = CMP_STRIDE * NSA_KV_WIDTH
    xk = kcx.reshape(batch, n_grp, kdim)
    xv = vcx.reshape(batch, n_grp, kdim)
    xblk = pl.BlockSpec((1, n_grp, kdim), lambda b: (b, 0, 0))
    lay = lambda a: pl.BlockSpec((1,) + a.shape[1:], lambda b: (layer,) + (0,) * (a.ndim - 1))
    oblk = pl.BlockSpec((1, n_grp, 2 * NSA_KV_WIDTH), lambda b: (b, 0, 0))
    return pl.pallas_call(
        _compress_kernel,
        grid=(batch,),
        in_specs=[xblk, xblk, lay(posa), lay(posb), lay(w1a), lay(w1b), lay(w2)],
        out_specs=[oblk, oblk],
        out_shape=[jax.ShapeDtypeStruct((batch, n_grp, 2 * NSA_KV_WIDTH), BF16)] * 2,
        compiler_params=_params("arbitrary"),
        name="nsa_compress",
    )(xk, xv, posa, posb, w1a, w1b, w2)


def _nsa_kernel(qu_ref, qr_ref, kc_ref, vc_ref, ks_ref, vs_ref, kw_ref, vw_ref, ng_ref, ovt_ref, exp_ref,
                o_ref):
    q0 = pl.program_id(1) * Q_BLOCK
    n_sel = ovt_ref.shape[0]
    rows4 = NSA_GROUP * Q_BLOCK
    low = lax.broadcasted_iota(jnp.int32, (Q_BLOCK, LANES), 1) < HEAD_DIM
    gsig = jax.nn.sigmoid(ng_ref[...])

    def stack(ref, h):
        parts = []
        for g in range(NSA_GROUP):
            hq = h * NSA_GROUP + g
            col = ref[:, (hq // 2) * LANES:(hq // 2 + 1) * LANES]
            keep = low if hq % 2 == 0 else jnp.logical_not(low)
            parts.append(jnp.where(keep, col, jnp.zeros_like(col)))
        return jnp.concatenate(parts, axis=0)

    def tile4(t):
        return jnp.concatenate([t] * NSA_GROUP, axis=0)

    cols_out = []
    for h in range(NSA_KV_HEADS):
        hc = slice(h * LANES, (h + 1) * LANES)
        qus = stack(qu_ref, h)
        qrs = stack(qr_ref, h)

        s = _dot_nt(qus, kc_ref[0, :, hc])
        tq = q0 + lax.broadcasted_iota(jnp.int32, (Q_BLOCK, LANES), 0)
        cid = lax.broadcasted_iota(jnp.int32, (Q_BLOCK, LANES), 1)
        cvalid = tile4(jnp.where(cid * CMP_STRIDE + (CMP_LEN - 1) <= tq, 1.0, 0.0))
        sm = jnp.where(cvalid > 0.5, s, NEG_INF)
        e = jnp.exp(sm - jnp.max(sm, axis=-1, keepdims=True)) * cvalid
        l = jnp.sum(e, axis=-1, keepdims=True)
        p = e * (1.0 / jnp.where(l > 0.0, l, 1.0))
        o_c = _dot(p.astype(BF16), vc_ref[0, :, hc])

        psum = p[0:Q_BLOCK]
        for g in range(1, NSA_GROUP):
            psum = psum + p[g * Q_BLOCK:(g + 1) * Q_BLOCK]
        p_hi = psum.astype(BF16)
        rem = psum - p_hi.astype(F32)
        p_mid = rem.astype(BF16)
        p_lo = (rem - p_mid.astype(F32)).astype(BF16)
        ovt = ovt_ref[...]
        imp = _dot_nt(ovt, p_hi) + _dot_nt(ovt, p_mid) + _dot_nt(ovt, p_lo)
        jid = lax.broadcasted_iota(jnp.int32, (n_sel, Q_BLOCK), 0)
        tid = q0 + lax.broadcasted_iota(jnp.int32, (n_sel, Q_BLOCK), 1)
        forced = (jid == 0) | (jid == jnp.right_shift(tid, 6))
        imp = jnp.where(forced, FORCE_SCORE, jnp.where(jid * SEL_LEN <= tid, imp, -FORCE_SCORE))
        rank = jnp.zeros((n_sel, Q_BLOCK), F32)
        for i in range(n_sel):
            ri = imp[i:i + 1, :]
            beats = (ri > imp) | ((ri == imp) & (jid > i))
            rank = rank + jnp.where(beats, 1.0, 0.0)
        sel_t = jnp.where(rank < float(SEL_TOP), 1.0, 0.0)
        sel = jnp.concatenate([sel_t, jnp.zeros((LANES - n_sel, Q_BLOCK), F32)], axis=0).T.astype(BF16)

        def sel_step(ci, carry):
            m, l, acc = carry
            k0 = pl.multiple_of(ci * SEL_CHUNK, SEL_CHUNK)
            s = _dot_nt(qrs, ks_ref[pl.ds(k0, SEL_CHUNK), hc])
            member = _dot(sel, exp_ref[ci])
            kpos = k0 + lax.broadcasted_iota(jnp.int32, (Q_BLOCK, SEL_CHUNK), 1)
            tq = q0 + lax.broadcasted_iota(jnp.int32, (Q_BLOCK, SEL_CHUNK), 0)
            bias = jnp.where((member > 0.5) & (kpos <= tq), 0.0, NEG_INF)
            sm = s + tile4(bias)
            m_new = jnp.maximum(m, jnp.max(sm, axis=-1, keepdims=True))
            alpha = jnp.exp(m - m_new)
            e = jnp.exp(sm - m_new)
            l = alpha * l + jnp.sum(e, axis=-1, keepdims=True)
            acc = alpha * acc + _dot(e.astype(BF16), vs_ref[pl.ds(k0, SEL_CHUNK), hc])
            return m_new, l, acc

        n_chunks = (q0 + Q_BLOCK + SEL_CHUNK - 1) // SEL_CHUNK
        init = (jnp.full((rows4, 1), NEG_INF, F32), jnp.zeros((rows4, 1), F32), jnp.zeros((rows4, LANES), F32))
        _, l_s, acc_s = lax.fori_loop(0, n_chunks, sel_step, init)
        o_s = acc_s * (1.0 / l_s)

        ws = pl.multiple_of(jnp.maximum(q0 - WINDOW, 0), Q_BLOCK)
        s = _dot_nt(qrs, kw_ref[pl.ds(ws, WIN_KEYS), hc])
        kpos = ws + lax.broadcasted_iota(jnp.int32, (Q_BLOCK, WIN_KEYS), 1)
        tq = q0 + lax.broadcasted_iota(jnp.int32, (Q_BLOCK, WIN_KEYS), 0)
        bias = jnp.where((kpos <= tq) & (kpos > tq - WINDOW), 0.0, NEG_INF)
        sm = s + tile4(bias)
        e = jnp.exp(sm - jnp.max(sm, axis=-1, keepdims=True))
        l_w = jnp.sum(e, axis=-1, keepdims=True)
        o_w = _dot(e.astype(BF16), vw_ref[pl.ds(ws, WIN_KEYS), hc]) * (1.0 / l_w)

        heads = []
        for g in range(NSA_GROUP):
            hq = h * NSA_GROUP + g
            r = slice(g * Q_BLOCK, (g + 1) * Q_BLOCK)
            heads.append(gsig[:, 3 * hq:3 * hq + 1] * o_c[r] + gsig[:, 3 * hq + 1:3 * hq + 2] * o_s[r]
                         + gsig[:, 3 * hq + 2:3 * hq + 3] * o_w[r])
        cols_out.append(jnp.where(low, heads[0], heads[1]))
        cols_out.append(jnp.where(low, heads[2], heads[3]))
    o_ref[...] = jnp.concatenate(cols_out, axis=1).astype(BF16)


def _nsa_consts(seq):
    n_cmp_pad = seq // CMP_STRIDE
    n_sel = seq // SEL_LEN
    c = np.arange(n_cmp_pad)
    j = np.arange(n_sel)
    n_cmp = (seq - CMP_LEN) // CMP_STRIDE + 1
    ov = ((c[None, :] * CMP_STRIDE < j[:, None] * SEL_LEN + SEL_LEN) &
          (j[:, None] * SEL_LEN <= c[None, :] * CMP_STRIDE + CMP_LEN - 1) & (c[None, :] < n_cmp))
    n_chunks = seq // SEL_CHUNK
    k = np.arange(SEL_CHUNK)
    jj = np.arange(LANES)
    expand = np.stack([(jj[:, None] == ci * (SEL_CHUNK // SEL_LEN) + k[None, :] // SEL_LEN)
                       for ci in range(n_chunks)])
    return jnp.asarray(ov, BF16), jnp.asarray(expand, BF16)


def _nsa(qu, qr, kc, vc, ks, vs, kw, vw, ng, consts, batch, seq):
    ovt, expand = consts
    n_qb = seq // Q_BLOCK
    qblk = pl.BlockSpec((Q_BLOCK, NSA_WIDTH), lambda b, q: (b * n_qb + q, 0))
    cblk = pl.BlockSpec((1, seq // CMP_STRIDE, 2 * NSA_KV_WIDTH), lambda b, q: (b, 0, 0))
    kvblk = pl.BlockSpec((seq, 2 * NSA_KV_WIDTH), lambda b, q: (b, 0))
    full = lambda a: pl.BlockSpec(a.shape, lambda b, q: (0,) * a.ndim)
    return pl.pallas_call(
        _nsa_kernel,
        grid=(batch, n_qb),
        in_specs=[qblk, qblk, cblk, cblk, kvblk, kvblk, kvblk, kvblk,
                  pl.BlockSpec((Q_BLOCK, LANES), lambda b, q: (b * n_qb + q, 0)), full(ovt), full(expand)],
        out_specs=qblk,
        out_shape=jax.ShapeDtypeStruct(qu.shape, BF16),
        compiler_params=_params("arbitrary", "arbitrary"),
        name="nsa_attention",
    )(qu, qr, kc, vc, ks, vs, kw, vw, ng, ovt, expand)


def _outproj_kernel(yr_ref, yc_ref, yn_ref, x_ref, mod_ref, w_ref, lng_ref, lnb_ref, rwh_ref, rwl_ref, rb_ref,
                    x1_ref, h2_ref, ti_ref, tw_ref, cnt_ref, *, alpha):
    tm = x_ref.shape[0]
    mix = (_dot(yr_ref[...], w_ref[0, 0:RET_WIDTH, :])
           + _dot(yc_ref[...], w_ref[0, RET_WIDTH:RET_WIDTH + CONV_WIDTH, :])
           + _dot(yn_ref[...], w_ref[0, RET_WIDTH + CONV_WIDTH:, :]))
    x1 = _layer_norm(alpha * x_ref[...] + (1.0 + mod_ref[2, 0]) * mix) * lng_ref[0, 0] + lnb_ref[0, 0]
    x1_ref[...] = x1
    h2 = _layer_norm(x1) * (1.0 + mod_ref[4, 0]) + mod_ref[3, 0]
    h2_ref[...] = h2
    h_hi = h2.astype(BF16)
    h_lo = (h2 - h_hi.astype(F32)).astype(BF16)
    logits = _dot(h_hi, rwh_ref[0]) + _dot(h_lo, rwh_ref[0]) + _dot(h_hi, rwl_ref[0]) + rb_ref[0]

    lane = lax.broadcasted_iota(jnp.int32, (tm, LANES), 1)
    lanef = lane.astype(F32)
    rest = jnp.where(lane < N_EXPERTS, logits, -jnp.inf)
    vals, idxs = [], []
    for _ in range(TOP_K):
        top = jnp.max(rest, axis=-1, keepdims=True)
        idx = jnp.min(jnp.where(rest == top, lanef, float(LANES)), axis=-1, keepdims=True)
        vals.append(top)
        idxs.append(idx)
        rest = jnp.where(lanef == idx, -jnp.inf, rest)
    exps = [jnp.exp(v - vals[0]) for v in vals]
    inv = 1.0 / functools.reduce(lambda a, b: a + b, exps)
    top_w = jnp.zeros((tm, LANES), F32)
    top_i = jnp.zeros((tm, LANES), F32)
    member = jnp.zeros((tm, LANES), F32)
    for k in range(TOP_K):
        top_w = jnp.where(lane == k, exps[k] * inv, top_w)
        top_i = jnp.where(lane == k, idxs[k], top_i)
        member = member + jnp.where(lanef == idxs[k], 1.0, 0.0)
    tw_ref[...] = top_w
    ti_ref[...] = top_i.astype(jnp.int32)

    @pl.when(pl.program_id(0) == 0)
    def _():
        cnt_ref[...] = jnp.zeros_like(cnt_ref)

    cnt_ref[...] += jnp.broadcast_to(jnp.sum(member, axis=0, keepdims=True), cnt_ref.shape)


def _out_projection(y_ret, y_conv, y_nsa, x, mod_l, w_out, ln_g, ln_b, rw_hi, rw_lo, rb, layer, seq, alpha):
    tokens = x.shape[0]
    tm = ROW_TILE
    tiles_per_seq = seq // tm
    row = lambda w: pl.BlockSpec((tm, w), lambda i: (i, 0))
    lay3 = lambda a: pl.BlockSpec((1,) + a.shape[1:], lambda i: (layer,) + (0,) * (a.ndim - 1))
    return pl.pallas_call(
        functools.partial(_outproj_kernel, alpha=alpha),
        grid=(tokens // tm,),
        in_specs=[row(RET_WIDTH), row(CONV_WIDTH), row(NSA_WIDTH), row(D_MODEL),
                  pl.BlockSpec((6, 1, 1, D_MODEL), lambda i: (0, i // tiles_per_seq, 0, 0)),
                  lay3(w_out),
                  pl.BlockSpec((1, 1, 1, D_MODEL), lambda i: (layer, 0, 0, 0)),
                  pl.BlockSpec((1, 1, 1, D_MODEL), lambda i: (layer, 0, 0, 0)),
                  lay3(rw_hi), lay3(rw_lo), lay3(rb)],
        out_specs=[row(D_MODEL), row(D_MODEL), row(LANES), row(LANES),
                   pl.BlockSpec((8, LANES), lambda i: (0, 0))],
        out_shape=[jax.ShapeDtypeStruct((tokens, D_MODEL), F32), jax.ShapeDtypeStruct((tokens, D_MODEL), F32),
                   jax.ShapeDtypeStruct((tokens, LANES), jnp.int32), jax.ShapeDtypeStruct((tokens, LANES), F32),
                   jax.ShapeDtypeStruct((8, LANES), F32)],
        compiler_params=_params("arbitrary"),
        name="outproj_norm_router",
    )(y_ret, y_conv, y_nsa, x, mod_l, w_out, ln_g, ln_b, rw_hi, rw_lo, rb)


def _route_kernel(ti_ref, pstart_ref, tri_ref, pos_ref, carry_ref):
    @pl.when(pl.program_id(0) == 0)
    def _():
        carry_ref[...] = jnp.zeros_like(carry_ref)

    tm = ti_ref.shape[0]
    lane = lax.broadcasted_iota(jnp.int32, (tm, LANES), 1)
    top_i = ti_ref[...]
    onehots = [lane == top_i[:, k:k + 1] for k in range(TOP_K)]
    member = functools.reduce(lambda a, b: a + b, [jnp.where(o, 1.0, 0.0) for o in onehots])
    base = pstart_ref[...] + carry_ref[0:1] + _dot(tri_ref[...], member.astype(BF16))
    pos = jnp.zeros((tm, LANES), F32)
    for k in range(TOP_K):
        pos_k = jnp.sum(jnp.where(onehots[k], base, 0.0), axis=-1, keepdims=True)
        pos = jnp.where(lane == k, pos_k, pos)
    pos_ref[...] = pos.astype(jnp.int32)
    carry_ref[...] += jnp.broadcast_to(jnp.sum(member, axis=0, keepdims=True), carry_ref.shape)


def _route_positions(top_i, pstart):
    tokens = top_i.shape[0]
    tm = ROW_TILE
    r = np.arange(tm)
    tri = jnp.asarray(r[None, :] < r[:, None], BF16)
    return pl.pallas_call(
        _route_kernel,
        grid=(tokens // tm,),
        in_specs=[pl.BlockSpec((tm, LANES), lambda i: (i, 0)),
                  pl.BlockSpec((1, LANES), lambda i: (0, 0)),
                  pl.BlockSpec((tm, tm), lambda i: (0, 0))],
        out_specs=pl.BlockSpec((tm, LANES), lambda i: (i, 0)),
        out_shape=jax.ShapeDtypeStruct((tokens, LANES), jnp.int32),
        scratch_shapes=[pltpu.VMEM((8, LANES), F32)],
        compiler_params=_params("arbitrary"),
        name="route_positions",
    )(top_i, pstart, tri)


def _block_table(counts, n_blocks):
    tm = MOE_TILE
    cnt = counts[0, :N_EXPERTS].astype(jnp.int32)
    nblk = (cnt + tm - 1) // tm
    bend = jnp.cumsum(nblk)
    bstart = bend - nblk
    blocks = jnp.arange(n_blocks, dtype=jnp.int32)
    block_e = jnp.minimum(jnp.sum((bend[None, :] <= blocks[:, None]).astype(jnp.int32), axis=1), N_EXPERTS - 1)
    n_valid = jnp.clip(cnt[block_e] - (blocks - bstart[block_e]) * tm, 0, tm).astype(jnp.int32)
    n_valid = jnp.where(blocks < bend[-1], n_valid, 0)
    pstart = jnp.pad((bstart * tm).astype(F32), (0, LANES - N_EXPERTS))[None, :]
    return pstart, block_e.astype(jnp.int32), bend[-1:].astype(jnp.int32), n_valid


def _sc_mesh():
    return plsc.VectorSubcoreMesh(core_axis_name="core", subcore_axis_name="subcore")


def _split_rows(x, idx):
    rows, width = x.shape
    sub = jnp.arange(SC_SPLIT, dtype=jnp.int32)
    idx = (idx[..., None] * SC_SPLIT + sub).reshape(idx.shape[:-1] + (idx.shape[-1] * SC_SPLIT,))
    return x.reshape(rows * SC_SPLIT, width // SC_SPLIT), idx


def _scatter_rows(x, idx_t, n_rows):
    full_width = x.shape[1]
    x, idx_t = _split_rows(x, idx_t)
    n_rows = n_rows * SC_SPLIT
    tokens, width = x.shape
    n_idx = idx_t.shape[0]
    win = SC_WINDOW

    @functools.partial(pl.kernel, out_type=jax.ShapeDtypeStruct((n_rows, width), x.dtype), mesh=_sc_mesh(),
                       scratch_types=[], name="dispatch_scatter")
    def scatter(x_hbm, i_hbm, o_hbm):
        def body(x_vmem, *i_vmems):
            for i_vmem in i_vmems:
                pltpu.sync_copy(x_vmem, o_hbm.at[i_vmem.at[0]])

        idx_specs = [pl.BlockSpec((1, win), functools.partial(lambda i, k: (k, i), k=k)) for k in range(n_idx)]
        pltpu.emit_pipeline(
            body,
            grid=(tokens // win,),
            in_specs=[pl.BlockSpec((win, width), lambda i: (i, 0))] + idx_specs,
            out_specs=[],
            core_axis_name=("core", "subcore"),
            dimension_semantics=(pltpu.PARALLEL,),
        )(x_hbm, *([i_hbm] * n_idx))

    return scatter(x, idx_t).reshape(n_rows // SC_SPLIT, full_width)


def _gather_rows(x, idx):
    full_width = x.shape[1]
    x, idx = _split_rows(x, idx)
    width = x.shape[1]
    n = idx.shape[0]
    win = SC_WINDOW

    @functools.partial(pl.kernel, out_type=jax.ShapeDtypeStruct((n, width), x.dtype), mesh=_sc_mesh(),
                       scratch_types=[], name="combine_gather")
    def gather(x_hbm, i_hbm, o_hbm):
        def body(i_vmem, o_vmem):
            pltpu.sync_copy(x_hbm.at[i_vmem.at[0]], o_vmem)

        pltpu.emit_pipeline(
            body,
            grid=(n // win,),
            in_specs=[pl.BlockSpec((1, win), lambda i: (0, i))],
            out_specs=[pl.BlockSpec((win, width), lambda i: (i, 0))],
            core_axis_name=("core", "subcore"),
            dimension_semantics=(pltpu.PARALLEL,),
        )(i_hbm, o_hbm)

    return gather(x, idx.reshape(1, n)).reshape(n // SC_SPLIT, full_width)


def _moe_kernel(be_ref, nb_ref, nv_ref, rows_ref, wgu_ref, bgu_ref, wd_ref, bd_ref, o_ref, wgu_bf, wd_bf):
    i = pl.program_id(0)
    expert = be_ref[i]
    first = (i == 0) | (expert != be_ref[jnp.maximum(i - 1, 0)])
    valid = i < nb_ref[0]

    @pl.when(valid & first)
    def _():
        wgu_bf[...] = wgu_ref[0, 0].astype(BF16)
        wd_bf[...] = wd_ref[0, 0].astype(BF16)

    @pl.when(valid)
    def _():
        row = lax.broadcasted_iota(jnp.int32, rows_ref.shape, 0)
        rows = jnp.where(row < nv_ref[i], rows_ref[...], 0.0).astype(BF16)
        gu = _dot(rows, wgu_bf[...]) + bgu_ref[0, 0]
        g = jnp.minimum(gu[:, :D_EXPERT], SWIGLU_LIMIT)
        u = jnp.clip(gu[:, D_EXPERT:], -SWIGLU_LIMIT, SWIGLU_LIMIT)
        act = (u + 1.0) * (g * jax.nn.sigmoid(SWIGLU_ALPHA * g))
        o_ref[...] = _dot(act.astype(BF16), wd_bf[...]) + bd_ref[0, 0]

    @pl.when(jnp.logical_not(valid))
    def _():
        o_ref[...] = jnp.zeros_like(o_ref)


def _expert_ffn(rows, block_e, n_used, n_valid, w_gu, b_gu, w_down, b_down, layer):
    n_rows = rows.shape[0]
    tm = MOE_TILE
    grid_spec = pltpu.PrefetchScalarGridSpec(
        num_scalar_prefetch=3,
        grid=(n_rows // tm,),
        in_specs=[pl.BlockSpec((tm, D_MODEL), lambda i, be, nb, nv: (i, 0)),
                  pl.BlockSpec((1, 1, D_MODEL, 2 * D_EXPERT), lambda i, be, nb, nv: (layer, be[i], 0, 0)),
                  pl.BlockSpec((1, 1, 1, 2 * D_EXPERT), lambda i, be, nb, nv: (layer, be[i], 0, 0)),
                  pl.BlockSpec((1, 1, D_EXPERT, D_MODEL), lambda i, be, nb, nv: (layer, be[i], 0, 0)),
                  pl.BlockSpec((1, 1, 1, D_MODEL), lambda i, be, nb, nv: (layer, be[i], 0, 0))],
        out_specs=pl.BlockSpec((tm, D_MODEL), lambda i, be, nb, nv: (i, 0)),
        scratch_shapes=[pltpu.VMEM((D_MODEL, 2 * D_EXPERT), BF16), pltpu.VMEM((D_EXPERT, D_MODEL), BF16)])
    return pl.pallas_call(
        _moe_kernel,
        grid_spec=grid_spec,
        out_shape=jax.ShapeDtypeStruct((n_rows, D_MODEL), F32),
        compiler_params=_params("arbitrary"),
        name="expert_ffn",
    )(block_e, n_used, n_valid, rows, w_gu, b_gu, w_down, b_down)


def _final_kernel(x_ref, y_ref, tw_ref, mod_ref, lng_ref, lnb_ref, o_ref, *, alpha):
    top_w = tw_ref[...]
    ffn = top_w[:, 0:1] * y_ref[:, 0:D_MODEL]
    for k in range(1, TOP_K):
        ffn = ffn + top_w[:, k:k + 1] * y_ref[:, k * D_MODEL:(k + 1) * D_MODEL]
    o_ref[...] = (_layer_norm(alpha * x_ref[...] + (1.0 + mod_ref[5, 0]) * ffn) * lng_ref[0, 0] + lnb_ref[0, 0])


def _final_norm(x1, y_tok, top_w, mod_l, ln_g, ln_b, layer, seq, alpha):
    tokens = x1.shape[0]
    tm = ROW_TILE
    tiles_per_seq = seq // tm
    row = lambda w: pl.BlockSpec((tm, w), lambda i: (i, 0))
    vec = pl.BlockSpec((1, 1, 1, D_MODEL), lambda i: (layer, 1, 0, 0))
    return pl.pallas_call(
        functools.partial(_final_kernel, alpha=alpha),
        grid=(tokens // tm,),
        in_specs=[row(D_MODEL), row(TOP_K * D_MODEL), row(LANES),
                  pl.BlockSpec((6, 1, 1, D_MODEL), lambda i: (0, i // tiles_per_seq, 0, 0)), vec, vec],
        out_specs=row(D_MODEL),
        out_shape=jax.ShapeDtypeStruct(x1.shape, F32),
        compiler_params=_params("arbitrary"),
        name="combine_final_norm",
    )(x1, y_tok, top_w, mod_l, ln_g, ln_b)


def _inproj_columns():
    def dup(off):
        a = np.arange(off, off + HEAD_DIM)
        b = np.arange(off + HEAD_DIM, off + 2 * HEAD_DIM)
        return np.concatenate([a, a, b, b])
    src = [np.arange(0, 2304), np.arange(2304, 2560), dup(2560), dup(2688), dup(2816), dup(2944),
           np.arange(3072, 3072 + N_GATE), np.zeros((LANES - N_GATE,), np.int64)]
    cols = np.concatenate(src)
    live = np.ones((N_COLS,), np.float32)
    live[OFF_NG + N_GATE:] = 0.0
    return cols.astype(np.int32), live


def kernel(x, c, positions, w_in, w_out, ret_gn_w, conv_w, cmp_pos, cmp_w1, cmp_w2, ada_w, ada_b, ln_g, ln_b,
           router_w, router_b, w_gate_up, b_gate_up, w_down, b_down):
    batch, seq, _ = x.shape
    depth = w_in.shape[0]
    tokens = batch * seq
    n_rows = tokens * TOP_K + N_EXPERTS * MOE_TILE
    alpha = float((2 * depth) ** 0.25)

    cols, live = _inproj_columns()
    w_in_r = (jnp.take(w_in, jnp.asarray(cols), axis=2) * jnp.asarray(live)).astype(BF16)
    w_out_b = w_out.astype(BF16)
    rw = jnp.pad(router_w, ((0, 0), (0, 0), (0, LANES - N_EXPERTS)))
    rw_hi = rw.astype(BF16)
    rw_lo = (rw - rw_hi.astype(F32)).astype(BF16)
    rb = jnp.pad(router_b, ((0, 0), (0, LANES - N_EXPERTS))).reshape(depth, 1, LANES)
    cw = _compress_weights(cmp_pos, cmp_w1, cmp_w2)
    ret_consts = _retention_consts()
    nsa_consts = _nsa_consts(seq)
    gn_w = ret_gn_w.reshape(depth, 1, RET_WIDTH)
    ln_g4 = ln_g.reshape(depth, 2, 1, D_MODEL)
    ln_b4 = ln_b.reshape(depth, 2, 1, D_MODEL)
    b_gu4 = b_gate_up.reshape(depth, N_EXPERTS, 1, 2 * D_EXPERT)
    b_d4 = b_down.reshape(depth, N_EXPERTS, 1, D_MODEL)

    mod = _modulation(c, ada_w, ada_b)
    cos, sin = _rope_tables(positions)
    xt = x.reshape(tokens, D_MODEL)
    for l in range(depth):
        (rq, rk, rv, rg, y_conv, qu, qr, kcx, vcx, ks, vs, kw, vw, ng) = _input_projection(
            xt, mod[l], w_in_r, cos, sin, conv_w, l, seq)
        y_ret = _retention(rq, rk, rv, rg, ret_consts, gn_w, l, batch, seq)
        kc, vc = _compress(kcx, vcx, cw, l, batch, seq)
        y_nsa = _nsa(qu, qr, kc, vc, ks, vs, kw, vw, ng, nsa_consts, batch, seq)
        x1, h2, top_i, top_w, counts = _out_projection(y_ret, y_conv, y_nsa, xt, mod[l], w_out_b, ln_g4, ln_b4,
                                                       rw_hi, rw_lo, rb, l, seq, alpha)
        pstart, block_e, n_used, n_valid = _block_table(counts, n_rows // MOE_TILE)
        pos = _route_positions(top_i, pstart)[:, :TOP_K]
        rows = _scatter_rows(h2, pos.T, n_rows)
        y = _expert_ffn(rows, block_e, n_used, n_valid, w_gate_up, b_gu4, w_down, b_d4, l)
        y_tok = _gather_rows(y, pos.reshape(-1)).reshape(tokens, TOP_K * D_MODEL)
        xt = _final_norm(x1, y_tok, top_w, mod[l], ln_g4, ln_b4, l, seq, alpha)
    return xt.reshape(batch, seq, D_MODEL)
```

```python
import functools

import numpy as np
import jax
import jax.numpy as jnp
from jax import lax
from jax.experimental import pallas as pl
from jax.experimental.pallas import tpu as pltpu
from jax.experimental.pallas import tpu_sc as plsc

F32 = jnp.float32
BF16 = jnp.bfloat16

D_MODEL = 1024
HEAD_DIM = 64
RET_WIDTH = 256
RET_HEADS = 4
RET_CHUNK = 128
CONV_WIDTH = 256
CONV_K = 3
NSA_WIDTH = 512
NSA_HEADS = 8
NSA_KV_HEADS = 2
NSA_GROUP = 4
NSA_KV_WIDTH = 128
CMP_LEN = 32
CMP_STRIDE = 16
CMP_HIDDEN = 128
SEL_LEN = 64
SEL_TOP = 8
WINDOW = 512
Q_BLOCK = 128
ROPE_THETA = 10000.0
N_EXPERTS = 32
TOP_K = 4
D_EXPERT = 1024
SWIGLU_LIMIT = 7.0
SWIGLU_ALPHA = 1.702
LN_EPS = 1e-5
NEG_INF = -1e30
FORCE_SCORE = 1e9

LANES = 128
VMEM_LIMIT = 56 * 1024 * 1024

OFF_RQ, OFF_RK, OFF_RV, OFF_RG = 0, 256, 512, 768
OFF_CB, OFF_CC, OFF_CH = 1024, 1280, 1536
OFF_NQ = 1792
OFF_KC, OFF_VC = 2304, 2432
OFF_KS, OFF_VS, OFF_KW, OFF_VW = 2560, 2816, 3072, 3328
OFF_NG = 3584
N_COLS = 3712
N_GATE = NSA_HEADS * 3

ROW_TILE = 256
MOE_TILE = 256
SEL_CHUNK = 512
WIN_KEYS = WINDOW + Q_BLOCK
SC_WINDOW = LANES
SC_SPLIT = 4
SPLIT_WIDTH = D_MODEL // SC_SPLIT


def _dot(a, b):
    return jnp.dot(a, b, preferred_element_type=F32)


def _dot_nt(a, b):
    return lax.dot_general(a, b, (((1,), (1,)), ((), ())), preferred_element_type=F32)


def _layer_norm(x):
    mu = jnp.mean(x, axis=-1, keepdims=True)
    xc = x - mu
    var = jnp.mean(xc * xc, axis=-1, keepdims=True)
    return xc * lax.rsqrt(var + LN_EPS)


def _params(*sem):
    return pltpu.CompilerParams(dimension_semantics=sem, vmem_limit_bytes=VMEM_LIMIT)


def _mod_kernel(c_ref, w_ref, b_ref, o_ref):
    c = c_ref[...]
    ca = (c * jax.nn.sigmoid(c)).astype(BF16)
    o_ref[0, 0] = _dot(ca, w_ref[0].astype(BF16)) + b_ref[0]


def _modulation(c, ada_w, ada_b):
    depth = ada_w.shape[0]
    batch = c.shape[0]
    out = pl.pallas_call(
        _mod_kernel,
        grid=(depth, 6),
        in_specs=[pl.BlockSpec((batch, D_MODEL), lambda l, j: (0, 0)),
                  pl.BlockSpec((1, D_MODEL, D_MODEL), lambda l, j: (l, 0, j)),
                  pl.BlockSpec((1, 1, D_MODEL), lambda l, j: (l * 6 + j, 0, 0))],
        out_specs=pl.BlockSpec((1, 1, batch, D_MODEL), lambda l, j: (l, j, 0, 0)),
        out_shape=jax.ShapeDtypeStruct((depth, 6, batch, D_MODEL), F32),
        compiler_params=_params("arbitrary", "arbitrary"),
        name="adaln_mod",
    )(c, ada_w, ada_b.reshape(depth * 6, 1, D_MODEL))
    return out.reshape(depth, 6, batch, 1, D_MODEL)


def _rope_table_kernel(pos_ref, inv_ref, cos_ref, sin_ref):
    ang = pos_ref[...] * inv_ref[...]
    cos_ref[...] = jnp.cos(ang)
    sin_ref[...] = jnp.sin(ang)


def _rope_tables(positions):
    half = HEAD_DIM // 2
    per_row = LANES // half
    tokens = positions.size
    pos4 = jnp.repeat(positions.reshape(tokens // per_row, per_row).astype(F32), half, axis=1)
    inv = ROPE_THETA ** (-jnp.arange(half, dtype=F32) / half)
    inv4 = jnp.tile(inv, per_row)[None, :]
    rows = tokens // per_row
    tile = min(rows, 1024)
    cos4, sin4 = pl.pallas_call(
        _rope_table_kernel,
        grid=(rows // tile,),
        in_specs=[pl.BlockSpec((tile, LANES), lambda i: (i, 0)),
                  pl.BlockSpec((1, LANES), lambda i: (0, 0))],
        out_specs=[pl.BlockSpec((tile, LANES), lambda i: (i, 0))] * 2,
        out_shape=[jax.ShapeDtypeStruct((rows, LANES), F32)] * 2,
        compiler_params=_params("arbitrary"),
        name="rope_tables",
    )(pos4, inv4)
    cos = jnp.tile(cos4.reshape(tokens, half), (1, per_row))
    sign = jnp.tile(jnp.concatenate([-jnp.ones((half,), F32), jnp.ones((half,), F32)]), LANES // HEAD_DIM)
    sin = jnp.tile(sin4.reshape(tokens, half), (1, per_row)) * sign[None, :]
    return cos, sin


def _inproj_kernel(x_ref, mod_ref, w_ref, cos_ref, sin_ref, convw_ref,
                   rq_ref, rk_ref, rv_ref, rg_ref, yc_ref, qu_ref, qr_ref, kc_ref, vc_ref,
                   ks_ref, vs_ref, kw_ref, vw_ref, ng_ref, carry_ref, *, tiles_per_seq):
    i = pl.program_id(0)
    tm = x_ref.shape[0]
    h = (_layer_norm(x_ref[...]) * (1.0 + mod_ref[1, 0]) + mod_ref[0, 0]).astype(BF16)
    cosf = cos_ref[...]
    sinf = sin_ref[...]
    lane = lax.broadcasted_iota(jnp.int32, (tm, LANES), 1)
    first_half = (lane % HEAD_DIM) < (HEAD_DIM // 2)

    def proj(off, width):
        return _dot(h, w_ref[0, :, off:off + width])

    def rope(c):
        cols = []
        for j in range(c.shape[1] // LANES):
            cj = c[:, j * LANES:(j + 1) * LANES]
            swapped = jnp.where(first_half, pltpu.roll(cj, LANES - HEAD_DIM // 2, 1),
                                pltpu.roll(cj, HEAD_DIM // 2, 1))
            cols.append(cj * cosf + swapped * sinf)
        return jnp.concatenate(cols, axis=1) if len(cols) > 1 else cols[0]

    scale = HEAD_DIM ** -0.5
    rq_ref[...] = rope(proj(OFF_RQ, RET_WIDTH)).astype(BF16)
    rk_ref[...] = (rope(proj(OFF_RK, RET_WIDTH)) * scale).astype(BF16)
    rv_ref[...] = proj(OFF_RV, RET_WIDTH).astype(BF16)
    rg = proj(OFF_RG, RET_WIDTH)
    rg_ref[...] = (rg * jax.nn.sigmoid(rg)).astype(BF16)

    cb = proj(OFF_CB, CONV_WIDTH)
    u = proj(OFF_CC, CONV_WIDTH) * proj(OFF_CH, CONV_WIDTH)

    @pl.when(i % tiles_per_seq == 0)
    def _():
        carry_ref[...] = jnp.zeros_like(carry_ref)

    carry = carry_ref[...]
    row = lax.broadcasted_iota(jnp.int32, (tm, CONV_WIDTH), 0)
    prev1 = jnp.where(row == 0, carry[7:8], pltpu.roll(u, 1, 0))
    prev2 = jnp.where(row == 0, carry[6:7], jnp.where(row == 1, carry[7:8], pltpu.roll(u, 2, 0)))
    cw = convw_ref[0]
    yc_ref[...] = (cb * (cw[0:1] * prev2 + cw[1:2] * prev1 + cw[2:3] * u)).astype(BF16)
    carry_ref[...] = u[tm - 8:tm]

    nq = proj(OFF_NQ, NSA_WIDTH) * scale
    qu_ref[...] = nq.astype(BF16)
    qr_ref[...] = rope(nq).astype(BF16)
    kc_ref[...] = proj(OFF_KC, NSA_KV_WIDTH)
    vc_ref[...] = proj(OFF_VC, NSA_KV_WIDTH)
    ks_ref[...] = rope(proj(OFF_KS, 2 * NSA_KV_WIDTH)).astype(BF16)
    vs_ref[...] = proj(OFF_VS, 2 * NSA_KV_WIDTH).astype(BF16)
    kw_ref[...] = rope(proj(OFF_KW, 2 * NSA_KV_WIDTH)).astype(BF16)
    vw_ref[...] = proj(OFF_VW, 2 * NSA_KV_WIDTH).astype(BF16)
    ng_ref[...] = proj(OFF_NG, LANES)


def _input_projection(x, mod_l, w_l, cos, sin, conv_w, layer, seq):
    tokens = x.shape[0]
    tm = ROW_TILE
    tiles_per_seq = seq // tm
    row = lambda w: pl.BlockSpec((tm, w), lambda i: (i, 0))
    widths = [(RET_WIDTH, BF16)] * 4 + [(CONV_WIDTH, BF16), (NSA_WIDTH, BF16), (NSA_WIDTH, BF16),
                                       (NSA_KV_WIDTH, F32), (NSA_KV_WIDTH, F32)] + \
             [(2 * NSA_KV_WIDTH, BF16)] * 4 + [(LANES, F32)]
    return pl.pallas_call(
        functools.partial(_inproj_kernel, tiles_per_seq=tiles_per_seq),
        grid=(tokens // tm,),
        in_specs=[row(D_MODEL),
                  pl.BlockSpec((6, 1, 1, D_MODEL), lambda i: (0, i // tiles_per_seq, 0, 0)),
                  pl.BlockSpec((1, D_MODEL, N_COLS), lambda i: (layer, 0, 0)),
                  row(LANES), row(LANES),
                  pl.BlockSpec((1, CONV_K, CONV_WIDTH), lambda i: (layer, 0, 0))],
        out_specs=[row(w) for w, _ in widths],
        out_shape=[jax.ShapeDtypeStruct((tokens, w), dt) for w, dt in widths],
        scratch_shapes=[pltpu.VMEM((8, CONV_WIDTH), F32)],
        compiler_params=_params("arbitrary"),
        name="ln_inproj",
    )(x, mod_l, w_l, cos, sin, conv_w)


def _retention_kernel(q_ref, k_ref, v_ref, g_ref, intra_ref, qdec_ref, kdec_ref, cdec_ref, gn_ref,
                      o_ref, state_ref):
    @pl.when(pl.program_id(1) == 0)
    def _():
        state_ref[...] = jnp.zeros_like(state_ref)

    n_pairs = RET_WIDTH // LANES
    lane = lax.broadcasted_iota(jnp.int32, (RET_CHUNK, LANES), 1)
    low = lane < HEAD_DIM
    blk_r = lax.broadcasted_iota(jnp.int32, (LANES, LANES), 0) < HEAD_DIM
    blk_c = lax.broadcasted_iota(jnp.int32, (LANES, LANES), 1) < HEAD_DIM
    same_head = blk_r == blk_c
    for ch in range(q_ref.shape[0] // RET_CHUNK):
        rows = slice(ch * RET_CHUNK, (ch + 1) * RET_CHUNK)
        outs = []
        for p in range(n_pairs):
            cols = slice(p * LANES, (p + 1) * LANES)
            qp = q_ref[rows, cols]
            kp = k_ref[rows, cols]
            vp = v_ref[rows, cols]
            out = jnp.zeros((RET_CHUNK, LANES), F32)
            for hh in range(2):
                keep = low if hh == 0 else jnp.logical_not(low)
                qm = jnp.where(keep, qp, jnp.zeros_like(qp))
                s = _dot_nt(qm, kp) * intra_ref[2 * p + hh]
                out = jnp.where(keep, _dot(s.astype(BF16), vp), out)
            state = state_ref[p]
            qd = (qp.astype(F32) * qdec_ref[:, cols]).astype(BF16)
            out = out + _dot(qd, state.astype(BF16))
            kd_t = (kp.astype(F32) * kdec_ref[:, cols]).T.astype(BF16)
            upd = _dot(kd_t, vp)
            state_ref[p] = state * cdec_ref[p] + jnp.where(same_head, upd, 0.0)
            def half_mean(t):
                lo_sum = jnp.sum(jnp.where(low, t, 0.0), axis=-1, keepdims=True)
                hi_sum = jnp.sum(jnp.where(low, 0.0, t), axis=-1, keepdims=True)
                return jnp.where(low, lo_sum, hi_sum) * (1.0 / HEAD_DIM)
            oc = out - half_mean(out)
            outs.append(oc * lax.rsqrt(half_mean(oc * oc) + LN_EPS))
        normed = jnp.concatenate(outs, axis=1)
        o_ref[rows, :] = (normed * gn_ref[0] * g_ref[rows, :].astype(F32)).astype(BF16)


def _retention_consts():
    heads = jnp.arange(RET_HEADS, dtype=F32)
    log_gamma = jnp.log(1.0 - jnp.power(2.0, -5.0 - heads))
    i = jnp.arange(RET_CHUNK, dtype=F32)
    diff = i[:, None] - i[None, :]
    intra = jnp.where(diff >= 0, jnp.exp(diff * log_gamma[:, None, None]), 0.0)
    q_dec = jnp.exp((i + 1.0) * log_gamma[:, None])
    k_dec = jnp.exp((RET_CHUNK - 1.0 - i) * log_gamma[:, None])
    c_dec = jnp.exp(RET_CHUNK * log_gamma)
    expand = lambda t: jnp.repeat(t.T, HEAD_DIM, axis=1)
    c_rows = jnp.repeat(c_dec, HEAD_DIM).reshape(RET_WIDTH // LANES, LANES, 1)
    c_blk = jnp.broadcast_to(c_rows, (RET_WIDTH // LANES, LANES, LANES))
    return intra, expand(q_dec), expand(k_dec), c_blk


def _retention(rq, rk, rv, rg, consts, gn_w, layer, batch, seq):
    intra, q_dec, k_dec, c_blk = consts
    rows = 512
    steps = seq // rows
    blk = pl.BlockSpec((rows, RET_WIDTH), lambda b, s: (b * steps + s, 0))
    full = lambda a: pl.BlockSpec(a.shape, lambda b, s: (0,) * a.ndim)
    return pl.pallas_call(
        _retention_kernel,
        grid=(batch, steps),
        in_specs=[blk, blk, blk, blk, full(intra), full(q_dec), full(k_dec), full(c_blk),
                  pl.BlockSpec((1, 1, RET_WIDTH), lambda b, s: (layer, 0, 0))],
        out_specs=blk,
        out_shape=jax.ShapeDtypeStruct(rq.shape, BF16),
        scratch_shapes=[pltpu.VMEM((RET_WIDTH // LANES, LANES, LANES), F32)],
        compiler_params=_params("arbitrary", "arbitrary"),
        name="retention",
    )(rq, rk, rv, rg, intra, q_dec, k_dec, c_blk, gn_w)


def _compress_kernel(xk_ref, xv_ref, posa_ref, posb_ref, w1a_ref, w1b_ref, w2_ref, kc_ref, vc_ref):
    n_grp = xk_ref.shape[0] // CMP_STRIDE
    for kv, (x_ref, o_ref) in enumerate(((xk_ref, kc_ref), (xv_ref, vc_ref))):
        ya = jnp.zeros((n_grp, NSA_KV_HEADS * CMP_HIDDEN), F32)
        yb = jnp.zeros((n_grp, NSA_KV_HEADS * CMP_HIDDEN), F32)
        for r in range(CMP_STRIDE):
            x = x_ref[pl.ds(r, n_grp, stride=CMP_STRIDE), :]
            cols = slice(r * NSA_KV_WIDTH, (r + 1) * NSA_KV_WIDTH)
            ya = ya + _dot((x + posa_ref[0, kv, :, cols]).astype(BF16), w1a_ref[0, kv, cols, :])
            yb = yb + _dot((x + posb_ref[0, kv, :, cols]).astype(BF16), w1b_ref[0, kv, cols, :])
        hidden = ya + pltpu.roll(yb, n_grp - 1, 0)
        act = jax.nn.gelu(hidden)
        o_ref[0] = _dot(act.astype(BF16), w2_ref[0, kv]).astype(BF16)


def _compress_weights(cmp_pos, cmp_w1, cmp_w2):
    depth = cmp_w1.shape[0]
    eye = jnp.eye(NSA_KV_HEADS, dtype=F32)
    a = cmp_w1.reshape(depth, 2, 2, CMP_STRIDE, HEAD_DIM, CMP_HIDDEN)
    w1 = jnp.einsum('lkardj,hg->lkarhdgj', a, eye)
    w1 = w1.reshape(depth, 2, 2, CMP_STRIDE * NSA_KV_WIDTH, NSA_KV_HEADS * CMP_HIDDEN).astype(BF16)
    pos = cmp_pos.reshape(depth, 2, 2, CMP_STRIDE, 1, HEAD_DIM)
    pos = jnp.broadcast_to(pos, (depth, 2, 2, CMP_STRIDE, NSA_KV_HEADS, HEAD_DIM))
    pos = pos.reshape(depth, 2, 2, 1, CMP_STRIDE * NSA_KV_WIDTH)
    w2 = jnp.einsum('lkje,hg,r->lkhjgre', cmp_w2, eye, jnp.ones((2,), F32))
    w2 = w2.reshape(depth, 2, NSA_KV_HEADS * CMP_HIDDEN, 2 * NSA_KV_WIDTH).astype(BF16)
    return w1[:, :, 0], w1[:, :, 1], pos[:, :, 0], pos[:, :, 1], w2


def _compress(kcx, vcx, cw, layer, batch, seq):
    w1a, w1b, posa, posb, w2 = cw
    n_grp = seq // CMP_STRIDE
    xblk = pl.BlockSpec((seq, NSA_KV_WIDTH), lambda b: (b, 0))
    lay = lambda a: pl.BlockSpec((1,) + a.shape[1:], lambda b: (layer,) + (0,) * (a.ndim - 1))
    oblk = pl.BlockSpec((1, n_grp, 2 * NSA_KV_WIDTH), lambda b: (b, 0, 0))
    return pl.pallas_call(
        _compress_kernel,
        grid=(batch,),
        in_specs=[xblk, xblk, lay(posa), lay(posb), lay(w1a), lay(w1b), lay(w2)],
        out_specs=[oblk, oblk],
        out_shape=[jax.ShapeDtypeStruct((batch, n_grp, 2 * NSA_KV_WIDTH), BF16)] * 2,
        compiler_params=_params("arbitrary"),
        name="nsa_compress",
    )(kcx, vcx, posa, posb, w1a, w1b, w2)


def _nsa_kernel(qu_ref, qr_ref, kc_ref, vc_ref, ks_ref, vs_ref, kw_ref, vw_ref, ng_ref, ovt_ref, exp_ref,
                o_ref):
    q0 = pl.program_id(1) * Q_BLOCK
    n_sel = ovt_ref.shape[0]
    rows4 = NSA_GROUP * Q_BLOCK
    low = lax.broadcasted_iota(jnp.int32, (Q_BLOCK, LANES), 1) < HEAD_DIM
    gsig = jax.nn.sigmoid(ng_ref[...])

    def stack(ref, h):
        parts = []
        for g in range(NSA_GROUP):
            hq = h * NSA_GROUP + g
            col = ref[:, (hq // 2) * LANES:(hq // 2 + 1) * LANES]
            keep = low if hq % 2 == 0 else jnp.logical_not(low)
            parts.append(jnp.where(keep, col, jnp.zeros_like(col)))
        return jnp.concatenate(parts, axis=0)

    def tile4(t):
        return jnp.concatenate([t] * NSA_GROUP, axis=0)

    cols_out = []
    for h in range(NSA_KV_HEADS):
        hc = slice(h * LANES, (h + 1) * LANES)
        qus = stack(qu_ref, h)
        qrs = stack(qr_ref, h)

        s = _dot_nt(qus, kc_ref[0, :, hc])
        tq = q0 + lax.broadcasted_iota(jnp.int32, (Q_BLOCK, LANES), 0)
        cid = lax.broadcasted_iota(jnp.int32, (Q_BLOCK, LANES), 1)
        cvalid = tile4(jnp.where(cid * CMP_STRIDE + (CMP_LEN - 1) <= tq, 1.0, 0.0))
        sm = jnp.where(cvalid > 0.5, s, NEG_INF)
        e = jnp.exp(sm - jnp.max(sm, axis=-1, keepdims=True)) * cvalid
        l = jnp.sum(e, axis=-1, keepdims=True)
        p = e * (1.0 / jnp.where(l > 0.0, l, 1.0))
        o_c = _dot(p.astype(BF16), vc_ref[0, :, hc])

        psum = p[0:Q_BLOCK]
        for g in range(1, NSA_GROUP):
            psum = psum + p[g * Q_BLOCK:(g + 1) * Q_BLOCK]
        p_hi = psum.astype(BF16)
        rem = psum - p_hi.astype(F32)
        p_mid = rem.astype(BF16)
        p_lo = (rem - p_mid.astype(F32)).astype(BF16)
        ovt = ovt_ref[...]
        imp = _dot_nt(ovt, p_hi) + _dot_nt(ovt, p_mid) + _dot_nt(ovt, p_lo)
        jid = lax.broadcasted_iota(jnp.int32, (n_sel, Q_BLOCK), 0)
        tid = q0 + lax.broadcasted_iota(jnp.int32, (n_sel, Q_BLOCK), 1)
        forced = (jid == 0) | (jid == jnp.right_shift(tid, 6))
        imp = jnp.where(forced, FORCE_SCORE, jnp.where(jid * SEL_LEN <= tid, imp, -FORCE_SCORE))
        rank = jnp.zeros((n_sel, Q_BLOCK), F32)
        for i in range(n_sel):
            ri = imp[i:i + 1, :]
            beats = (ri > imp) | ((ri == imp) & (jid > i))
            rank = rank + jnp.where(beats, 1.0, 0.0)
        sel_t = jnp.where(rank < float(SEL_TOP), 1.0, 0.0)
        sel = jnp.concatenate([sel_t, jnp.zeros((LANES - n_sel, Q_BLOCK), F32)], axis=0).T.astype(BF16)

        def sel_step(ci, carry):
            m, l, acc = carry
            k0 = pl.multiple_of(ci * SEL_CHUNK, SEL_CHUNK)
            s = _dot_nt(qrs, ks_ref[pl.ds(k0, SEL_CHUNK), hc])
            member = _dot(sel, exp_ref[ci])
            kpos = k0 + lax.broadcasted_iota(jnp.int32, (Q_BLOCK, SEL_CHUNK), 1)
            tq = q0 + lax.broadcasted_iota(jnp.int32, (Q_BLOCK, SEL_CHUNK), 0)
            bias = jnp.where((member > 0.5) & (kpos <= tq), 0.0, NEG_INF)
            sm = s + tile4(bias)
            m_new = jnp.maximum(m, jnp.max(sm, axis=-1, keepdims=True))
            alpha = jnp.exp(m - m_new)
            e = jnp.exp(sm - m_new)
            l = alpha * l + jnp.sum(e, axis=-1, keepdims=True)
            acc = alpha * acc + _dot(e.astype(BF16), vs_ref[pl.ds(k0, SEL_CHUNK), hc])
            return m_new, l, acc

        n_chunks = (q0 + Q_BLOCK + SEL_CHUNK - 1) // SEL_CHUNK
        init = (jnp.full((rows4, 1), NEG_INF, F32), jnp.zeros((rows4, 1), F32), jnp.zeros((rows4, LANES), F32))
        _, l_s, acc_s = lax.fori_loop(0, n_chunks, sel_step, init)
        o_s = acc_s * (1.0 / l_s)

        ws = pl.multiple_of(jnp.maximum(q0 - WINDOW, 0), Q_BLOCK)
        s = _dot_nt(qrs, kw_ref[pl.ds(ws, WIN_KEYS), hc])
        kpos = ws + lax.broadcasted_iota(jnp.int32, (Q_BLOCK, WIN_KEYS), 1)
        tq = q0 + lax.broadcasted_iota(jnp.int32, (Q_BLOCK, WIN_KEYS), 0)
        bias = jnp.where((kpos <= tq) & (kpos > tq - WINDOW), 0.0, NEG_INF)
        sm = s + tile4(bias)
        e = jnp.exp(sm - jnp.max(sm, axis=-1, keepdims=True))
        l_w = jnp.sum(e, axis=-1, keepdims=True)
        o_w = _dot(e.astype(BF16), vw_ref[pl.ds(ws, WIN_KEYS), hc]) * (1.0 / l_w)

        heads = []
        for g in range(NSA_GROUP):
            hq = h * NSA_GROUP + g
            r = slice(g * Q_BLOCK, (g + 1) * Q_BLOCK)
            heads.append(gsig[:, 3 * hq:3 * hq + 1] * o_c[r] + gsig[:, 3 * hq + 1:3 * hq + 2] * o_s[r]
                         + gsig[:, 3 * hq + 2:3 * hq + 3] * o_w[r])
        cols_out.append(jnp.where(low, heads[0], heads[1]))
        cols_out.append(jnp.where(low, heads[2], heads[3]))
    o_ref[...] = jnp.concatenate(cols_out, axis=1).astype(BF16)


def _nsa_consts(seq):
    n_cmp_pad = seq // CMP_STRIDE
    n_sel = seq // SEL_LEN
    c = np.arange(n_cmp_pad)
    j = np.arange(n_sel)
    n_cmp = (seq - CMP_LEN) // CMP_STRIDE + 1
    ov = ((c[None, :] * CMP_STRIDE < j[:, None] * SEL_LEN + SEL_LEN) &
          (j[:, None] * SEL_LEN <= c[None, :] * CMP_STRIDE + CMP_LEN - 1) & (c[None, :] < n_cmp))
    n_chunks = seq // SEL_CHUNK
    k = np.arange(SEL_CHUNK)
    jj = np.arange(LANES)
    expand = np.stack([(jj[:, None] == ci * (SEL_CHUNK // SEL_LEN) + k[None, :] // SEL_LEN)
                       for ci in range(n_chunks)])
    return jnp.asarray(ov, BF16), jnp.asarray(expand, BF16)


def _nsa(qu, qr, kc, vc, ks, vs, kw, vw, ng, consts, batch, seq):
    ovt, expand = consts
    n_qb = seq // Q_BLOCK
    qblk = pl.BlockSpec((Q_BLOCK, NSA_WIDTH), lambda b, q: (b * n_qb + q, 0))
    cblk = pl.BlockSpec((1, seq // CMP_STRIDE, 2 * NSA_KV_WIDTH), lambda b, q: (b, 0, 0))
    kvblk = pl.BlockSpec((seq, 2 * NSA_KV_WIDTH), lambda b, q: (b, 0))
    full = lambda a: pl.BlockSpec(a.shape, lambda b, q: (0,) * a.ndim)
    return pl.pallas_call(
        _nsa_kernel,
        grid=(batch, n_qb),
        in_specs=[qblk, qblk, cblk, cblk, kvblk, kvblk, kvblk, kvblk,
                  pl.BlockSpec((Q_BLOCK, LANES), lambda b, q: (b * n_qb + q, 0)), full(ovt), full(expand)],
        out_specs=qblk,
        out_shape=jax.ShapeDtypeStruct(qu.shape, BF16),
        compiler_params=_params("arbitrary", "arbitrary"),
        name="nsa_attention",
    )(qu, qr, kc, vc, ks, vs, kw, vw, ng, ovt, expand)


def _outproj_kernel(yr_ref, yc_ref, yn_ref, x_ref, mod_ref, w_ref, lng_ref, lnb_ref, rwh_ref, rwl_ref, rb_ref,
                    x1_ref, h2a_ref, h2b_ref, h2c_ref, h2d_ref, ti_ref, tw_ref, cnt_ref, *, alpha):
    tm = x_ref.shape[0]
    mix = (_dot(yr_ref[...], w_ref[0, 0:RET_WIDTH, :])
           + _dot(yc_ref[...], w_ref[0, RET_WIDTH:RET_WIDTH + CONV_WIDTH, :])
           + _dot(yn_ref[...], w_ref[0, RET_WIDTH + CONV_WIDTH:, :]))
    x1 = _layer_norm(alpha * x_ref[...] + (1.0 + mod_ref[2, 0]) * mix) * lng_ref[0, 0] + lnb_ref[0, 0]
    x1_ref[...] = x1
    h2 = _layer_norm(x1) * (1.0 + mod_ref[4, 0]) + mod_ref[3, 0]
    for j, part_ref in enumerate((h2a_ref, h2b_ref, h2c_ref, h2d_ref)):
        part_ref[...] = h2[:, j * SPLIT_WIDTH:(j + 1) * SPLIT_WIDTH]
    h_hi = h2.astype(BF16)
    h_lo = (h2 - h_hi.astype(F32)).astype(BF16)
    logits = _dot(h_hi, rwh_ref[0]) + _dot(h_lo, rwh_ref[0]) + _dot(h_hi, rwl_ref[0]) + rb_ref[0]

    lane = lax.broadcasted_iota(jnp.int32, (tm, LANES), 1)
    lanef = lane.astype(F32)
    rest = jnp.where(lane < N_EXPERTS, logits, -jnp.inf)
    vals, idxs = [], []
    for _ in range(TOP_K):
        top = jnp.max(rest, axis=-1, keepdims=True)
        idx = jnp.min(jnp.where(rest == top, lanef, float(LANES)), axis=-1, keepdims=True)
        vals.append(top)
        idxs.append(idx)
        rest = jnp.where(lanef == idx, -jnp.inf, rest)
    exps = [jnp.exp(v - vals[0]) for v in vals]
    inv = 1.0 / functools.reduce(lambda a, b: a + b, exps)
    top_w = jnp.zeros((tm, LANES), F32)
    top_i = jnp.zeros((tm, LANES), F32)
    member = jnp.zeros((tm, LANES), F32)
    for k in range(TOP_K):
        top_w = jnp.where(lane == k, exps[k] * inv, top_w)
        top_i = jnp.where(lane == k, idxs[k], top_i)
        member = member + jnp.where(lanef == idxs[k], 1.0, 0.0)
    tw_ref[...] = top_w
    ti_ref[...] = top_i.astype(jnp.int32)

    @pl.when(pl.program_id(0) == 0)
    def _():
        cnt_ref[...] = jnp.zeros_like(cnt_ref)

    cnt_ref[...] += jnp.broadcast_to(jnp.sum(member, axis=0, keepdims=True), cnt_ref.shape)


def _out_projection(y_ret, y_conv, y_nsa, x, mod_l, w_out, ln_g, ln_b, rw_hi, rw_lo, rb, layer, seq, alpha):
    tokens = x.shape[0]
    tm = ROW_TILE
    tiles_per_seq = seq // tm
    row = lambda w: pl.BlockSpec((tm, w), lambda i: (i, 0))
    lay3 = lambda a: pl.BlockSpec((1,) + a.shape[1:], lambda i: (layer,) + (0,) * (a.ndim - 1))
    return pl.pallas_call(
        functools.partial(_outproj_kernel, alpha=alpha),
        grid=(tokens // tm,),
        in_specs=[row(RET_WIDTH), row(CONV_WIDTH), row(NSA_WIDTH), row(D_MODEL),
                  pl.BlockSpec((6, 1, 1, D_MODEL), lambda i: (0, i // tiles_per_seq, 0, 0)),
                  lay3(w_out),
                  pl.BlockSpec((1, 1, 1, D_MODEL), lambda i: (layer, 0, 0, 0)),
                  pl.BlockSpec((1, 1, 1, D_MODEL), lambda i: (layer, 0, 0, 0)),
                  lay3(rw_hi), lay3(rw_lo), lay3(rb)],
        out_specs=[row(D_MODEL)] + [row(SPLIT_WIDTH)] * SC_SPLIT + [row(LANES), row(LANES),
                   pl.BlockSpec((8, LANES), lambda i: (0, 0))],
        out_shape=[jax.ShapeDtypeStruct((tokens, D_MODEL), F32)]
                  + [jax.ShapeDtypeStruct((tokens, SPLIT_WIDTH), F32)] * SC_SPLIT
                  + [jax.ShapeDtypeStruct((tokens, LANES), jnp.int32), jax.ShapeDtypeStruct((tokens, LANES), F32),
                   jax.ShapeDtypeStruct((8, LANES), F32)],
        compiler_params=_params("arbitrary"),
        name="outproj_norm_router",
    )(y_ret, y_conv, y_nsa, x, mod_l, w_out, ln_g, ln_b, rw_hi, rw_lo, rb)


def _route_kernel(ti_ref, pstart_ref, tri_ref, pos_ref, carry_ref):
    @pl.when(pl.program_id(0) == 0)
    def _():
        carry_ref[...] = jnp.zeros_like(carry_ref)

    tm = ti_ref.shape[0]
    lane = lax.broadcasted_iota(jnp.int32, (tm, LANES), 1)
    top_i = ti_ref[...]
    onehots = [lane == top_i[:, k:k + 1] for k in range(TOP_K)]
    member = functools.reduce(lambda a, b: a + b, [jnp.where(o, 1.0, 0.0) for o in onehots])
    base = pstart_ref[...] + carry_ref[0:1] + _dot(tri_ref[...], member.astype(BF16))
    pos = jnp.zeros((tm, LANES), F32)
    for k in range(TOP_K):
        pos_k = jnp.sum(jnp.where(onehots[k], base, 0.0), axis=-1, keepdims=True)
        pos = jnp.where(lane == k, pos_k, pos)
    pos_ref[...] = pos.astype(jnp.int32)
    carry_ref[...] += jnp.broadcast_to(jnp.sum(member, axis=0, keepdims=True), carry_ref.shape)


def _route_positions(top_i, pstart):
    tokens = top_i.shape[0]
    tm = ROW_TILE
    r = np.arange(tm)
    tri = jnp.asarray(r[None, :] < r[:, None], BF16)
    return pl.pallas_call(
        _route_kernel,
        grid=(tokens // tm,),
        in_specs=[pl.BlockSpec((tm, LANES), lambda i: (i, 0)),
                  pl.BlockSpec((1, LANES), lambda i: (0, 0)),
                  pl.BlockSpec((tm, tm), lambda i: (0, 0))],
        out_specs=pl.BlockSpec((tm, LANES), lambda i: (i, 0)),
        out_shape=jax.ShapeDtypeStruct((tokens, LANES), jnp.int32),
        scratch_shapes=[pltpu.VMEM((8, LANES), F32)],
        compiler_params=_params("arbitrary"),
        name="route_positions",
    )(top_i, pstart, tri)


def _block_table(counts, n_blocks):
    tm = MOE_TILE
    cnt = counts[0, :N_EXPERTS].astype(jnp.int32)
    nblk = (cnt + tm - 1) // tm
    bend = jnp.cumsum(nblk)
    bstart = bend - nblk
    blocks = jnp.arange(n_blocks, dtype=jnp.int32)
    block_e = jnp.minimum(jnp.sum((bend[None, :] <= blocks[:, None]).astype(jnp.int32), axis=1), N_EXPERTS - 1)
    n_valid = jnp.clip(cnt[block_e] - (blocks - bstart[block_e]) * tm, 0, tm).astype(jnp.int32)
    n_valid = jnp.where(blocks < bend[-1], n_valid, 0)
    pstart = jnp.pad((bstart * tm).astype(F32), (0, LANES - N_EXPERTS))[None, :]
    return pstart, block_e.astype(jnp.int32), bend[-1:].astype(jnp.int32), n_valid


def _sc_mesh():
    return plsc.VectorSubcoreMesh(core_axis_name="core", subcore_axis_name="subcore")


def _scatter_rows(xs, idx_t, n_rows):
    tokens, width = xs[0].shape
    n_idx = idx_t.shape[0]
    n_x = len(xs)
    win = SC_WINDOW

    @functools.partial(pl.kernel, out_type=[jax.ShapeDtypeStruct((n_rows, width), xs[0].dtype)] * n_x,
                       mesh=_sc_mesh(), scratch_types=[], name="dispatch_scatter")
    def scatter(*refs):
        x_hbms, i_hbm, o_hbms = refs[:n_x], refs[n_x], refs[n_x + 1:]
        idx_specs = [pl.BlockSpec((1, win), functools.partial(lambda i, k: (k, i), k=k)) for k in range(n_idx)]
        for x_hbm, o_hbm in zip(x_hbms, o_hbms):
            def body(x_vmem, *i_vmems, o_hbm=o_hbm):
                for i_vmem in i_vmems:
                    pltpu.sync_copy(x_vmem, o_hbm.at[i_vmem.at[0]])

            pltpu.emit_pipeline(
                body,
                grid=(tokens // win,),
                in_specs=[pl.BlockSpec((win, width), lambda i: (i, 0))] + idx_specs,
                out_specs=[],
                core_axis_name=("core", "subcore"),
                dimension_semantics=(pltpu.PARALLEL,),
            )(x_hbm, *([i_hbm] * n_idx))

    return scatter(*xs, idx_t)


def _gather_rows(xs, idx):
    width = xs[0].shape[1]
    n = idx.shape[0]
    n_x = len(xs)
    win = SC_WINDOW

    @functools.partial(pl.kernel, out_type=[jax.ShapeDtypeStruct((n, width), xs[0].dtype)] * n_x,
                       mesh=_sc_mesh(), scratch_types=[], name="combine_gather")
    def gather(*refs):
        x_hbms, i_hbm, o_hbms = refs[:n_x], refs[n_x], refs[n_x + 1:]
        for x_hbm, o_hbm in zip(x_hbms, o_hbms):
            def body(i_vmem, o_vmem, x_hbm=x_hbm):
                pltpu.sync_copy(x_hbm.at[i_vmem.at[0]], o_vmem)

            pltpu.emit_pipeline(
                body,
                grid=(n // win,),
                in_specs=[pl.BlockSpec((1, win), lambda i: (0, i))],
                out_specs=[pl.BlockSpec((win, width), lambda i: (i, 0))],
                core_axis_name=("core", "subcore"),
                dimension_semantics=(pltpu.PARALLEL,),
            )(i_hbm, o_hbm)

    return gather(*xs, idx.reshape(1, n))


def _moe_kernel(be_ref, nb_ref, nv_ref, ra_ref, rb_ref, rc_ref, rd_ref, wgu_ref, bgu_ref, wd_ref, bd_ref,
                oa_ref, ob_ref, oc_ref, od_ref, wgu_bf, wd_bf):
    row_refs = (ra_ref, rb_ref, rc_ref, rd_ref)
    out_refs = (oa_ref, ob_ref, oc_ref, od_ref)
    i = pl.program_id(0)
    expert = be_ref[i]
    first = (i == 0) | (expert != be_ref[jnp.maximum(i - 1, 0)])
    valid = i < nb_ref[0]

    @pl.when(valid & first)
    def _():
        wgu_bf[...] = wgu_ref[0, 0].astype(BF16)
        wd_bf[...] = wd_ref[0, 0].astype(BF16)

    @pl.when(valid)
    def _():
        live = lax.broadcasted_iota(jnp.int32, ra_ref.shape, 0) < nv_ref[i]
        gu = bgu_ref[0, 0]
        for j, r_ref in enumerate(row_refs):
            rows = jnp.where(live, r_ref[...], 0.0).astype(BF16)
            gu = gu + _dot(rows, wgu_bf[j * SPLIT_WIDTH:(j + 1) * SPLIT_WIDTH, :])
        g = jnp.minimum(gu[:, :D_EXPERT], SWIGLU_LIMIT)
        u = jnp.clip(gu[:, D_EXPERT:], -SWIGLU_LIMIT, SWIGLU_LIMIT)
        act = (u + 1.0) * (g * jax.nn.sigmoid(SWIGLU_ALPHA * g))
        y = _dot(act.astype(BF16), wd_bf[...]) + bd_ref[0, 0]
        for j, o_ref in enumerate(out_refs):
            o_ref[...] = y[:, j * SPLIT_WIDTH:(j + 1) * SPLIT_WIDTH]

    @pl.when(jnp.logical_not(valid))
    def _():
        for o_ref in out_refs:
            o_ref[...] = jnp.zeros_like(o_ref)


def _expert_ffn(rows, block_e, n_used, n_valid, w_gu, b_gu, w_down, b_down, layer):
    n_rows = rows[0].shape[0]
    tm = MOE_TILE
    part = pl.BlockSpec((tm, SPLIT_WIDTH), lambda i, be, nb, nv: (i, 0))
    grid_spec = pltpu.PrefetchScalarGridSpec(
        num_scalar_prefetch=3,
        grid=(n_rows // tm,),
        in_specs=[part] * SC_SPLIT + [
                  pl.BlockSpec((1, 1, D_MODEL, 2 * D_EXPERT), lambda i, be, nb, nv: (layer, be[i], 0, 0)),
                  pl.BlockSpec((1, 1, 1, 2 * D_EXPERT), lambda i, be, nb, nv: (layer, be[i], 0, 0)),
                  pl.BlockSpec((1, 1, D_EXPERT, D_MODEL), lambda i, be, nb, nv: (layer, be[i], 0, 0)),
                  pl.BlockSpec((1, 1, 1, D_MODEL), lambda i, be, nb, nv: (layer, be[i], 0, 0))],
        out_specs=[part] * SC_SPLIT,
        scratch_shapes=[pltpu.VMEM((D_MODEL, 2 * D_EXPERT), BF16), pltpu.VMEM((D_EXPERT, D_MODEL), BF16)])
    return pl.pallas_call(
        _moe_kernel,
        grid_spec=grid_spec,
        out_shape=[jax.ShapeDtypeStruct((n_rows, SPLIT_WIDTH), F32)] * SC_SPLIT,
        compiler_params=_params("arbitrary"),
        name="expert_ffn",
    )(block_e, n_used, n_valid, *rows, w_gu, b_gu, w_down, b_down)


def _final_kernel(x_ref, tw_ref, mod_ref, lng_ref, lnb_ref, *rest, alpha):
    y_refs, o_ref = rest[:-1], rest[-1]
    top_w = tw_ref[...]
    parts = []
    for j in range(SC_SPLIT):
        acc = top_w[:, 0:1] * y_refs[j * TOP_K][...]
        for k in range(1, TOP_K):
            acc = acc + top_w[:, k:k + 1] * y_refs[j * TOP_K + k][...]
        parts.append(acc)
    ffn = jnp.concatenate(parts, axis=1)
    o_ref[...] = (_layer_norm(alpha * x_ref[...] + (1.0 + mod_ref[5, 0]) * ffn) * lng_ref[0, 0] + lnb_ref[0, 0])


def _final_norm(x1, y_parts, top_w, mod_l, ln_g, ln_b, layer, seq, alpha):
    tokens = x1.shape[0]
    tm = ROW_TILE
    tiles = tokens // tm
    tiles_per_seq = seq // tm
    row = lambda w: pl.BlockSpec((tm, w), lambda i: (i, 0))
    vec = pl.BlockSpec((1, 1, 1, D_MODEL), lambda i: (layer, 1, 0, 0))
    y_specs, y_args = [], []
    for j in range(SC_SPLIT):
        for k in range(TOP_K):
            y_specs.append(pl.BlockSpec((tm, SPLIT_WIDTH), functools.partial(lambda i, k: (k * tiles + i, 0), k=k)))
            y_args.append(y_parts[j])
    return pl.pallas_call(
        functools.partial(_final_kernel, alpha=alpha),
        grid=(tiles,),
        in_specs=[row(D_MODEL), row(LANES),
                  pl.BlockSpec((6, 1, 1, D_MODEL), lambda i: (0, i // tiles_per_seq, 0, 0)), vec, vec] + y_specs,
        out_specs=row(D_MODEL),
        out_shape=jax.ShapeDtypeStruct(x1.shape, F32),
        compiler_params=_params("arbitrary"),
        name="combine_final_norm",
    )(x1, top_w, mod_l, ln_g, ln_b, *y_args)


def _inproj_columns():
    def dup(off):
        a = np.arange(off, off + HEAD_DIM)
        b = np.arange(off + HEAD_DIM, off + 2 * HEAD_DIM)
        return np.concatenate([a, a, b, b])
    src = [np.arange(0, 2304), np.arange(2304, 2560), dup(2560), dup(2688), dup(2816), dup(2944),
           np.arange(3072, 3072 + N_GATE), np.zeros((LANES - N_GATE,), np.int64)]
    cols = np.concatenate(src)
    live = np.ones((N_COLS,), np.float32)
    live[OFF_NG + N_GATE:] = 0.0
    return cols.astype(np.int32), live


def kernel(x, c, positions, w_in, w_out, ret_gn_w, conv_w, cmp_pos, cmp_w1, cmp_w2, ada_w, ada_b, ln_g, ln_b,
           router_w, router_b, w_gate_up, b_gate_up, w_down, b_down):
    batch, seq, _ = x.shape
    depth = w_in.shape[0]
    tokens = batch * seq
    n_rows = tokens * TOP_K + N_EXPERTS * MOE_TILE
    alpha = float((2 * depth) ** 0.25)

    cols, live = _inproj_columns()
    w_in_r = (jnp.take(w_in, jnp.asarray(cols), axis=2) * jnp.asarray(live)).astype(BF16)
    w_out_b = w_out.astype(BF16)
    rw = jnp.pad(router_w, ((0, 0), (0, 0), (0, LANES - N_EXPERTS)))
    rw_hi = rw.astype(BF16)
    rw_lo = (rw - rw_hi.astype(F32)).astype(BF16)
    rb = jnp.pad(router_b, ((0, 0), (0, LANES - N_EXPERTS))).reshape(depth, 1, LANES)
    cw = _compress_weights(cmp_pos, cmp_w1, cmp_w2)
    ret_consts = _retention_consts()
    nsa_consts = _nsa_consts(seq)
    gn_w = ret_gn_w.reshape(depth, 1, RET_WIDTH)
    ln_g4 = ln_g.reshape(depth, 2, 1, D_MODEL)
    ln_b4 = ln_b.reshape(depth, 2, 1, D_MODEL)
    b_gu4 = b_gate_up.reshape(depth, N_EXPERTS, 1, 2 * D_EXPERT)
    b_d4 = b_down.reshape(depth, N_EXPERTS, 1, D_MODEL)

    mod = _modulation(c, ada_w, ada_b)
    cos, sin = _rope_tables(positions)
    xt = x.reshape(tokens, D_MODEL)
    for l in range(depth):
        (rq, rk, rv, rg, y_conv, qu, qr, kcx, vcx, ks, vs, kw, vw, ng) = _input_projection(
            xt, mod[l], w_in_r, cos, sin, conv_w, l, seq)
        y_ret = _retention(rq, rk, rv, rg, ret_consts, gn_w, l, batch, seq)
        kc, vc = _compress(kcx, vcx, cw, l, batch, seq)
        y_nsa = _nsa(qu, qr, kc, vc, ks, vs, kw, vw, ng, nsa_consts, batch, seq)
        x1, *h2, top_i, top_w, counts = _out_projection(y_ret, y_conv, y_nsa, xt, mod[l], w_out_b, ln_g4, ln_b4,
                                                        rw_hi, rw_lo, rb, l, seq, alpha)
        pstart, block_e, n_used, n_valid = _block_table(counts, n_rows // MOE_TILE)
        pos_t = _route_positions(top_i, pstart)[:, :TOP_K].T
        rows = _scatter_rows(h2, pos_t, n_rows)
        y = _expert_ffn(rows, block_e, n_used, n_valid, w_gate_up, b_gu4, w_down, b_d4, l)
        y_tok = _gather_rows(y, pos_t.reshape(-1))
        xt = _final_norm(x1, y_tok, top_w, mod[l], ln_g4, ln_b4, l, seq, alpha)
    return xt.reshape(batch, seq, D_MODEL)
```

```python
import functools

import numpy as np
import jax
import jax.numpy as jnp
from jax import lax
from jax.experimental import pallas as pl
from jax.experimental.pallas import tpu as pltpu
from jax.experimental.pallas import tpu_sc as plsc

F32 = jnp.float32
BF16 = jnp.bfloat16

D_MODEL = 1024
HEAD_DIM = 64
RET_WIDTH = 256
RET_HEADS = 4
RET_CHUNK = 128
CONV_WIDTH = 256
CONV_K = 3
NSA_WIDTH = 512
NSA_HEADS = 8
NSA_KV_HEADS = 2
NSA_GROUP = 4
NSA_KV_WIDTH = 128
CMP_LEN = 32
CMP_STRIDE = 16
CMP_HIDDEN = 128
SEL_LEN = 64
SEL_TOP = 8
WINDOW = 512
Q_BLOCK = 128
ROPE_THETA = 10000.0
N_EXPERTS = 32
TOP_K = 4
D_EXPERT = 1024
SWIGLU_LIMIT = 7.0
SWIGLU_ALPHA = 1.702
LN_EPS = 1e-5
NEG_INF = -1e30
FORCE_SCORE = 1e9

LANES = 128
VMEM_LIMIT = 56 * 1024 * 1024

OFF_RQ, OFF_RK, OFF_RV, OFF_RG = 0, 256, 512, 768
OFF_CB, OFF_CC, OFF_CH = 1024, 1280, 1536
OFF_NQ = 1792
OFF_KC, OFF_VC = 2304, 2432
OFF_KS, OFF_VS, OFF_KW, OFF_VW = 2560, 2688, 2816, 2944
OFF_NG = 3072
N_GATE = NSA_HEADS * 3
N_COLS = OFF_NG + LANES
LOG2E = 1.4426950408889634

ROW_TILE = 256
MOE_TILE = 256
SEL_GROUP = 4
SC_WINDOW = LANES
SC_SPLIT = 4
SPLIT_WIDTH = D_MODEL // SC_SPLIT


def _dot(a, b):
    return jnp.dot(a, b, preferred_element_type=F32)


def _dot_nt(a, b):
    return lax.dot_general(a, b, (((1,), (1,)), ((), ())), preferred_element_type=F32)


def _layer_norm(x):
    mu = jnp.mean(x, axis=-1, keepdims=True)
    xc = x - mu
    var = jnp.mean(xc * xc, axis=-1, keepdims=True)
    return xc * lax.rsqrt(var + LN_EPS)


def _params(*sem):
    return pltpu.CompilerParams(dimension_semantics=sem, vmem_limit_bytes=VMEM_LIMIT)


def _mod_kernel(c_ref, w_ref, b_ref, o_ref):
    c = c_ref[...]
    ca = (c * jax.nn.sigmoid(c)).astype(BF16)
    o_ref[0, 0] = _dot(ca, w_ref[0].astype(BF16)) + b_ref[0]


def _modulation(c, ada_w, ada_b):
    depth = ada_w.shape[0]
    batch = c.shape[0]
    out = pl.pallas_call(
        _mod_kernel,
        grid=(depth, 6),
        in_specs=[pl.BlockSpec((batch, D_MODEL), lambda l, j: (0, 0)),
                  pl.BlockSpec((1, D_MODEL, D_MODEL), lambda l, j: (l, 0, j)),
                  pl.BlockSpec((1, 1, D_MODEL), lambda l, j: (l * 6 + j, 0, 0))],
        out_specs=pl.BlockSpec((1, 1, batch, D_MODEL), lambda l, j: (l, j, 0, 0)),
        out_shape=jax.ShapeDtypeStruct((depth, 6, batch, D_MODEL), F32),
        compiler_params=_params("arbitrary", "arbitrary"),
        name="adaln_mod",
    )(c, ada_w, ada_b.reshape(depth * 6, 1, D_MODEL))
    return out.reshape(depth, 6, batch, 1, D_MODEL)


def _rope_table_kernel(pos_ref, inv_ref, cos_ref, sin_ref):
    ang = pos_ref[...] * inv_ref[...]
    cos_ref[...] = jnp.cos(ang)
    sin_ref[...] = jnp.sin(ang)


def _rope_tables(positions):
    half = HEAD_DIM // 2
    per_row = LANES // half
    tokens = positions.size
    pos4 = jnp.repeat(positions.reshape(tokens // per_row, per_row).astype(F32), half, axis=1)
    inv = ROPE_THETA ** (-jnp.arange(half, dtype=F32) / half)
    inv4 = jnp.tile(inv, per_row)[None, :]
    rows = tokens // per_row
    tile = min(rows, 1024)
    cos4, sin4 = pl.pallas_call(
        _rope_table_kernel,
        grid=(rows // tile,),
        in_specs=[pl.BlockSpec((tile, LANES), lambda i: (i, 0)),
                  pl.BlockSpec((1, LANES), lambda i: (0, 0))],
        out_specs=[pl.BlockSpec((tile, LANES), lambda i: (i, 0))] * 2,
        out_shape=[jax.ShapeDtypeStruct((rows, LANES), F32)] * 2,
        compiler_params=_params("arbitrary"),
        name="rope_tables",
    )(pos4, inv4)
    cos = jnp.tile(cos4.reshape(tokens, half), (1, per_row))
    sign = jnp.tile(jnp.concatenate([-jnp.ones((half,), F32), jnp.ones((half,), F32)]), LANES // HEAD_DIM)
    sin = jnp.tile(sin4.reshape(tokens, half), (1, per_row)) * sign[None, :]
    return cos, sin


def _inproj_kernel(x_ref, mod_ref, w_ref, cos_ref, sin_ref, convw_ref,
                   rq_ref, rk_ref, rv_ref, rg_ref, yc_ref, qu_ref, qr_ref, kc_ref, vc_ref,
                   ks_ref, vs_ref, kw_ref, vw_ref, ng_ref, carry_ref, *, tiles_per_seq):
    i = pl.program_id(0)
    tm = x_ref.shape[0]
    h = (_layer_norm(x_ref[...]) * (1.0 + mod_ref[1, 0]) + mod_ref[0, 0]).astype(BF16)
    cosf = cos_ref[...]
    sinf = sin_ref[...]
    lane = lax.broadcasted_iota(jnp.int32, (tm, LANES), 1)
    first_half = (lane % HEAD_DIM) < (HEAD_DIM // 2)

    def proj(off, width):
        return _dot(h, w_ref[0, :, off:off + width])

    def rope(c):
        cols = []
        for j in range(c.shape[1] // LANES):
            cj = c[:, j * LANES:(j + 1) * LANES]
            swapped = jnp.where(first_half, pltpu.roll(cj, LANES - HEAD_DIM // 2, 1),
                                pltpu.roll(cj, HEAD_DIM // 2, 1))
            cols.append(cj * cosf + swapped * sinf)
        return jnp.concatenate(cols, axis=1) if len(cols) > 1 else cols[0]

    scale = HEAD_DIM ** -0.5
    rq_ref[...] = rope(proj(OFF_RQ, RET_WIDTH)).astype(BF16)
    rk_ref[...] = (rope(proj(OFF_RK, RET_WIDTH)) * scale).astype(BF16)
    rv_ref[...] = proj(OFF_RV, RET_WIDTH).astype(BF16)
    rg = proj(OFF_RG, RET_WIDTH)
    rg_ref[...] = (rg * jax.nn.sigmoid(rg)).astype(BF16)

    cb = proj(OFF_CB, CONV_WIDTH)
    u = proj(OFF_CC, CONV_WIDTH) * proj(OFF_CH, CONV_WIDTH)

    @pl.when(i % tiles_per_seq == 0)
    def _():
        carry_ref[...] = jnp.zeros_like(carry_ref)

    carry = carry_ref[...]
    row = lax.broadcasted_iota(jnp.int32, (tm, CONV_WIDTH), 0)
    prev1 = jnp.where(row == 0, carry[7:8], pltpu.roll(u, 1, 0))
    prev2 = jnp.where(row == 0, carry[6:7], jnp.where(row == 1, carry[7:8], pltpu.roll(u, 2, 0)))
    cw = convw_ref[0]
    yc_ref[...] = (cb * (cw[0:1] * prev2 + cw[1:2] * prev1 + cw[2:3] * u)).astype(BF16)
    carry_ref[...] = u[tm - 8:tm]

    nq = proj(OFF_NQ, NSA_WIDTH) * (scale * LOG2E)
    qu_ref[...] = nq.T.astype(BF16)
    qr_ref[...] = rope(nq).T.astype(BF16)
    kc_ref[...] = proj(OFF_KC, NSA_KV_WIDTH)
    vc_ref[...] = proj(OFF_VC, NSA_KV_WIDTH)
    ks_ref[...] = rope(proj(OFF_KS, NSA_KV_WIDTH)).astype(BF16)
    vs_ref[...] = proj(OFF_VS, NSA_KV_WIDTH).T.astype(BF16)
    kw_ref[...] = rope(proj(OFF_KW, NSA_KV_WIDTH)).astype(BF16)
    vw_ref[...] = proj(OFF_VW, NSA_KV_WIDTH).T.astype(BF16)
    ng_ref[...] = proj(OFF_NG, LANES).T


def _input_projection(x, mod_l, w_l, cos, sin, conv_w, layer, seq):
    tokens = x.shape[0]
    tm = ROW_TILE
    tiles_per_seq = seq // tm
    row = lambda w: pl.BlockSpec((tm, w), lambda i: (i, 0))
    col = lambda w: pl.BlockSpec((w, tm), lambda i: (0, i))
    outs = [(RET_WIDTH, BF16, False)] * 4 + [(CONV_WIDTH, BF16, False), (NSA_WIDTH, BF16, True),
                                            (NSA_WIDTH, BF16, True), (NSA_KV_WIDTH, F32, False),
                                            (NSA_KV_WIDTH, F32, False), (NSA_KV_WIDTH, BF16, False),
                                            (NSA_KV_WIDTH, BF16, True), (NSA_KV_WIDTH, BF16, False),
                                            (NSA_KV_WIDTH, BF16, True), (LANES, F32, True)]
    return pl.pallas_call(
        functools.partial(_inproj_kernel, tiles_per_seq=tiles_per_seq),
        grid=(tokens // tm,),
        in_specs=[row(D_MODEL),
                  pl.BlockSpec((6, 1, 1, D_MODEL), lambda i: (0, i // tiles_per_seq, 0, 0)),
                  pl.BlockSpec((1, D_MODEL, N_COLS), lambda i: (layer, 0, 0)),
                  row(LANES), row(LANES),
                  pl.BlockSpec((1, CONV_K, CONV_WIDTH), lambda i: (layer, 0, 0))],
        out_specs=[col(w) if t else row(w) for w, _, t in outs],
        out_shape=[jax.ShapeDtypeStruct((w, tokens) if t else (tokens, w), dt) for w, dt, t in outs],
        scratch_shapes=[pltpu.VMEM((8, CONV_WIDTH), F32)],
        compiler_params=_params("arbitrary"),
        name="ln_inproj",
    )(x, mod_l, w_l, cos, sin, conv_w)


def _retention_kernel(q_ref, k_ref, v_ref, g_ref, intra_ref, qdec_ref, kdec_ref, cdec_ref, gn_ref,
                      o_ref, state_ref):
    @pl.when(pl.program_id(1) == 0)
    def _():
        state_ref[...] = jnp.zeros_like(state_ref)

    n_pairs = RET_WIDTH // LANES
    lane = lax.broadcasted_iota(jnp.int32, (RET_CHUNK, LANES), 1)
    low = lane < HEAD_DIM
    blk_r = lax.broadcasted_iota(jnp.int32, (LANES, LANES), 0) < HEAD_DIM
    blk_c = lax.broadcasted_iota(jnp.int32, (LANES, LANES), 1) < HEAD_DIM
    same_head = blk_r == blk_c
    for ch in range(q_ref.shape[0] // RET_CHUNK):
        rows = slice(ch * RET_CHUNK, (ch + 1) * RET_CHUNK)
        outs = []
        for p in range(n_pairs):
            cols = slice(p * LANES, (p + 1) * LANES)
            qp = q_ref[rows, cols]
            kp = k_ref[rows, cols]
            vp = v_ref[rows, cols]
            out = jnp.zeros((RET_CHUNK, LANES), F32)
            for hh in range(2):
                keep = low if hh == 0 else jnp.logical_not(low)
                qm = jnp.where(keep, qp, jnp.zeros_like(qp))
                s = _dot_nt(qm, kp) * intra_ref[2 * p + hh]
                out = jnp.where(keep, _dot(s.astype(BF16), vp), out)
            state = state_ref[p]
            qd = (qp.astype(F32) * qdec_ref[:, cols]).astype(BF16)
            out = out + _dot(qd, state.astype(BF16))
            kd_t = (kp.astype(F32) * kdec_ref[:, cols]).T.astype(BF16)
            upd = _dot(kd_t, vp)
            state_ref[p] = state * cdec_ref[p] + jnp.where(same_head, upd, 0.0)
            def half_mean(t):
                lo_sum = jnp.sum(jnp.where(low, t, 0.0), axis=-1, keepdims=True)
                hi_sum = jnp.sum(jnp.where(low, 0.0, t), axis=-1, keepdims=True)
                return jnp.where(low, lo_sum, hi_sum) * (1.0 / HEAD_DIM)
            oc = out - half_mean(out)
            outs.append(oc * lax.rsqrt(half_mean(oc * oc) + LN_EPS))
        normed = jnp.concatenate(outs, axis=1)
        o_ref[rows, :] = (normed * gn_ref[0] * g_ref[rows, :].astype(F32)).astype(BF16)


def _retention_consts():
    heads = jnp.arange(RET_HEADS, dtype=F32)
    log_gamma = jnp.log(1.0 - jnp.power(2.0, -5.0 - heads))
    i = jnp.arange(RET_CHUNK, dtype=F32)
    diff = i[:, None] - i[None, :]
    intra = jnp.where(diff >= 0, jnp.exp(diff * log_gamma[:, None, None]), 0.0)
    q_dec = jnp.exp((i + 1.0) * log_gamma[:, None])
    k_dec = jnp.exp((RET_CHUNK - 1.0 - i) * log_gamma[:, None])
    c_dec = jnp.exp(RET_CHUNK * log_gamma)
    expand = lambda t: jnp.repeat(t.T, HEAD_DIM, axis=1)
    c_rows = jnp.repeat(c_dec, HEAD_DIM).reshape(RET_WIDTH // LANES, LANES, 1)
    c_blk = jnp.broadcast_to(c_rows, (RET_WIDTH // LANES, LANES, LANES))
    return intra, expand(q_dec), expand(k_dec), c_blk


def _retention(rq, rk, rv, rg, consts, gn_w, layer, batch, seq):
    intra, q_dec, k_dec, c_blk = consts
    rows = 512
    steps = seq // rows
    blk = pl.BlockSpec((rows, RET_WIDTH), lambda b, s: (b * steps + s, 0))
    full = lambda a: pl.BlockSpec(a.shape, lambda b, s: (0,) * a.ndim)
    return pl.pallas_call(
        _retention_kernel,
        grid=(batch, steps),
        in_specs=[blk, blk, blk, blk, full(intra), full(q_dec), full(k_dec), full(c_blk),
                  pl.BlockSpec((1, 1, RET_WIDTH), lambda b, s: (layer, 0, 0))],
        out_specs=blk,
        out_shape=jax.ShapeDtypeStruct(rq.shape, BF16),
        scratch_shapes=[pltpu.VMEM((RET_WIDTH // LANES, LANES, LANES), F32)],
        compiler_params=_params("arbitrary", "arbitrary"),
        name="retention",
    )(rq, rk, rv, rg, intra, q_dec, k_dec, c_blk, gn_w)


def _compress_kernel(xk_ref, xv_ref, posa_ref, posb_ref, w1a_ref, w1b_ref, w2_ref, kc_ref, vc_ref):
    n_grp = xk_ref.shape[0] // CMP_STRIDE
    for kv, (x_ref, o_ref) in enumerate(((xk_ref, kc_ref), (xv_ref, vc_ref))):
        ya = jnp.zeros((n_grp, NSA_KV_HEADS * CMP_HIDDEN), F32)
        yb = jnp.zeros((n_grp, NSA_KV_HEADS * CMP_HIDDEN), F32)
        for r in range(CMP_STRIDE):
            x = x_ref[pl.ds(r, n_grp, stride=CMP_STRIDE), :]
            cols = slice(r * NSA_KV_WIDTH, (r + 1) * NSA_KV_WIDTH)
            ya = ya + _dot((x + posa_ref[0, kv, :, cols]).astype(BF16), w1a_ref[0, kv, cols, :])
            yb = yb + _dot((x + posb_ref[0, kv, :, cols]).astype(BF16), w1b_ref[0, kv, cols, :])
        hidden = ya + pltpu.roll(yb, n_grp - 1, 0)
        act = jax.nn.gelu(hidden)
        out = _dot(act.astype(BF16), w2_ref[0, kv])
        o_ref[0] = (out if kv == 0 else out.T).astype(BF16)


def _compress_weights(cmp_pos, cmp_w1, cmp_w2):
    depth = cmp_w1.shape[0]
    eye = jnp.eye(NSA_KV_HEADS, dtype=F32)
    a = cmp_w1.reshape(depth, 2, 2, CMP_STRIDE, HEAD_DIM, CMP_HIDDEN)
    w1 = jnp.einsum('lkardj,hg->lkarhdgj', a, eye)
    w1 = w1.reshape(depth, 2, 2, CMP_STRIDE * NSA_KV_WIDTH, NSA_KV_HEADS * CMP_HIDDEN).astype(BF16)
    pos = cmp_pos.reshape(depth, 2, 2, CMP_STRIDE, 1, HEAD_DIM)
    pos = jnp.broadcast_to(pos, (depth, 2, 2, CMP_STRIDE, NSA_KV_HEADS, HEAD_DIM))
    pos = pos.reshape(depth, 2, 2, 1, CMP_STRIDE * NSA_KV_WIDTH)
    w2 = jnp.einsum('lkje,hg->lkhjge', cmp_w2, eye)
    w2 = w2.reshape(depth, 2, NSA_KV_HEADS * CMP_HIDDEN, NSA_KV_WIDTH).astype(BF16)
    return w1[:, :, 0], w1[:, :, 1], pos[:, :, 0], pos[:, :, 1], w2


def _compress(kcx, vcx, cw, layer, batch, seq):
    w1a, w1b, posa, posb, w2 = cw
    n_grp = seq // CMP_STRIDE
    xblk = pl.BlockSpec((seq, NSA_KV_WIDTH), lambda b: (b, 0))
    lay = lambda a: pl.BlockSpec((1,) + a.shape[1:], lambda b: (layer,) + (0,) * (a.ndim - 1))
    oblk = pl.BlockSpec((1, n_grp, NSA_KV_WIDTH), lambda b: (b, 0, 0))
    return pl.pallas_call(
        _compress_kernel,
        grid=(batch,),
        in_specs=[xblk, xblk, lay(posa), lay(posb), lay(w1a), lay(w1b), lay(w2)],
        out_specs=[oblk, oblk],
        out_shape=[jax.ShapeDtypeStruct((batch, n_grp, NSA_KV_WIDTH), BF16)] * 2,
        compiler_params=_params("arbitrary"),
        name="nsa_compress",
    )(kcx, vcx, posa, posb, w1a, w1b, w2)


def _nsa_kernel(qu_ref, qr_ref, kc_ref, vc_ref, ks_ref, vs_ref, kw_ref, vw_ref, ng_ref, ov_ref, wbias_ref,
                o_ref, selb_ref):
    qb = pl.program_id(1)
    q0 = qb * Q_BLOCK
    n_sel = ov_ref.shape[0]
    cols4 = NSA_GROUP * Q_BLOCK
    n_past = WINDOW // Q_BLOCK
    gsig = jax.nn.sigmoid(ng_ref[...])

    def padded_q(ref, h):
        zeros = jnp.zeros((HEAD_DIM, Q_BLOCK), BF16)
        parts = []
        for g in range(NSA_GROUP):
            hq = h * NSA_GROUP + g
            x = ref[hq * HEAD_DIM:(hq + 1) * HEAD_DIM, :]
            parts.append(jnp.concatenate([x, zeros] if h == 0 else [zeros, x], axis=0))
        return jnp.concatenate(parts, axis=1)

    def tile4(t):
        return jnp.concatenate([t] * NSA_GROUP, axis=1)

    def attend(state, keys, values_t, bias, q_t):
        m, l, acc = state
        s = _dot(keys, q_t) + bias
        m_new = jnp.maximum(m, jnp.max(s, axis=0, keepdims=True))
        alpha = jnp.exp2(m - m_new)
        e = jnp.exp2(s - m_new)
        l = alpha * l + jnp.sum(e, axis=0, keepdims=True)
        acc = alpha * acc + _dot(values_t, e.astype(BF16))
        return m_new, l, acc

    init = (jnp.full((1, cols4), NEG_INF, F32), jnp.zeros((1, cols4), F32), jnp.zeros((HEAD_DIM, cols4), F32))
    head_rows = [slice(h * HEAD_DIM, (h + 1) * HEAD_DIM) for h in range(NSA_KV_HEADS)]
    q_rot = [padded_q(qr_ref, h) for h in range(NSA_KV_HEADS)]

    def position_bias(kb, windowed):
        return tile4(wbias_ref[jnp.clip(kb - qb + n_past, 0 if windowed else 1, n_past + 1)])

    o_cmp = []
    for h in range(NSA_KV_HEADS):
        hr = head_rows[h]
        qus = padded_q(qu_ref, h)

        s = _dot(kc_ref[0], qus)
        cid = lax.broadcasted_iota(jnp.int32, (LANES, Q_BLOCK), 0)
        tq = q0 + lax.broadcasted_iota(jnp.int32, (LANES, Q_BLOCK), 1)
        cvalid = tile4(jnp.where(cid * CMP_STRIDE + (CMP_LEN - 1) <= tq, 1.0, 0.0))
        sm = jnp.where(cvalid > 0.5, s, NEG_INF)
        e = jnp.exp2(sm - jnp.max(sm, axis=0, keepdims=True)) * cvalid
        l = jnp.sum(e, axis=0, keepdims=True)
        p = e * (1.0 / jnp.where(l > 0.0, l, 1.0))
        o_cmp.append(_dot(vc_ref[0, hr, :], p.astype(BF16)))

        psum = p[:, 0:Q_BLOCK]
        for g in range(1, NSA_GROUP):
            psum = psum + p[:, g * Q_BLOCK:(g + 1) * Q_BLOCK]
        p_hi = psum.astype(BF16)
        rem = psum - p_hi.astype(F32)
        p_mid = rem.astype(BF16)
        p_lo = (rem - p_mid.astype(F32)).astype(BF16)
        ov = ov_ref[...]
        imp = _dot(ov, p_hi) + _dot(ov, p_mid) + _dot(ov, p_lo)
        jid = lax.broadcasted_iota(jnp.int32, (n_sel, Q_BLOCK), 0)
        tid = q0 + lax.broadcasted_iota(jnp.int32, (n_sel, Q_BLOCK), 1)
        forced = (jid == 0) | (jid == jnp.right_shift(tid, 6))
        imp = jnp.where(forced, FORCE_SCORE, jnp.where(jid * SEL_LEN <= tid, imp, -FORCE_SCORE))
        rank = jnp.zeros((n_sel, Q_BLOCK), F32)
        for i in range(n_sel):
            ri = imp[i:i + 1, :]
            beats = (ri > imp) | ((ri == imp) & (jid > i))
            rank = rank + jnp.where(beats, 1.0, 0.0)
        sel_bias = tile4(jnp.where(rank < float(SEL_TOP), 0.0, NEG_INF))
        for j in range(n_sel):
            selb_ref[h, j] = jnp.broadcast_to(sel_bias[j:j + 1, :], (8, cols4))

    def selection_bias(h, kb):
        reps = SEL_LEN // 8
        return jnp.concatenate([jnp.tile(selb_ref[h, 2 * kb], (reps, 1)),
                                jnp.tile(selb_ref[h, 2 * kb + 1], (reps, 1))], axis=0)

    def sel_step(gi, states):
        k0 = pl.multiple_of(gi * (SEL_GROUP * Q_BLOCK), SEL_GROUP * Q_BLOCK)
        keys = ks_ref[pl.ds(k0, SEL_GROUP * Q_BLOCK), :]
        out = []
        for h in range(NSA_KV_HEADS):
            bias = jnp.concatenate([selection_bias(h, gi * SEL_GROUP + u) + position_bias(gi * SEL_GROUP + u, False)
                                    for u in range(SEL_GROUP)], axis=0)
            out.append(attend(states[h], keys, vs_ref[head_rows[h], pl.ds(k0, SEL_GROUP * Q_BLOCK)], bias, q_rot[h]))
        return tuple(out)

    sel_states = lax.fori_loop(0, (qb + SEL_GROUP) // SEL_GROUP, sel_step, (init,) * NSA_KV_HEADS)

    ws = pl.multiple_of(jnp.maximum(q0 - WINDOW, 0), Q_BLOCK)
    n_win = n_past + 1
    win_bias = jnp.concatenate([position_bias(ws // Q_BLOCK + u, True) for u in range(n_win)], axis=0)
    win_keys = kw_ref[pl.ds(ws, n_win * Q_BLOCK), :]
    heads = []
    for h in range(NSA_KV_HEADS):
        _, l_s, acc_s = sel_states[h]
        o_s = acc_s * (1.0 / l_s)
        _, l_w, acc_w = attend(init, win_keys, vw_ref[head_rows[h], pl.ds(ws, n_win * Q_BLOCK)], win_bias, q_rot[h])
        o_w = acc_w * (1.0 / l_w)
        o_c = o_cmp[h]
        for g in range(NSA_GROUP):
            hq = h * NSA_GROUP + g
            c = slice(g * Q_BLOCK, (g + 1) * Q_BLOCK)
            heads.append(gsig[3 * hq:3 * hq + 1, :] * o_c[:, c] + gsig[3 * hq + 1:3 * hq + 2, :] * o_s[:, c]
                         + gsig[3 * hq + 2:3 * hq + 3, :] * o_w[:, c])
    cols_out = [jnp.concatenate(heads[2 * m:2 * m + 2], axis=0).T for m in range(NSA_HEADS // 2)]
    o_ref[...] = jnp.concatenate(cols_out, axis=1).astype(BF16)


def _nsa_consts(seq):
    n_cmp_pad = seq // CMP_STRIDE
    n_sel = seq // SEL_LEN
    c = np.arange(n_cmp_pad)
    j = np.arange(n_sel)
    n_cmp = (seq - CMP_LEN) // CMP_STRIDE + 1
    ov = ((c[None, :] * CMP_STRIDE < j[:, None] * SEL_LEN + SEL_LEN) &
          (j[:, None] * SEL_LEN <= c[None, :] * CMP_STRIDE + CMP_LEN - 1) & (c[None, :] < n_cmp))
    kl = np.arange(Q_BLOCK)[:, None]
    tl = np.arange(Q_BLOCK)[None, :]
    n_past = WINDOW // Q_BLOCK
    valid = ([kl > tl] + [np.ones((Q_BLOCK, Q_BLOCK), bool)] * (n_past - 1) + [kl <= tl]
             + [np.zeros((Q_BLOCK, Q_BLOCK), bool)])
    wbias = np.where(np.stack(valid), 0.0, NEG_INF).astype(np.float32)
    return jnp.asarray(ov, BF16), jnp.asarray(wbias)


def _nsa(qu, qr, kc, vc, ks, vs, kw, vw, ng, consts, batch, seq):
    ov, wbias = consts
    n_qb = seq // Q_BLOCK
    tokens = batch * seq
    qblk = pl.BlockSpec((NSA_WIDTH, Q_BLOCK), lambda b, q: (0, b * n_qb + q))
    cblk = pl.BlockSpec((1, seq // CMP_STRIDE, NSA_KV_WIDTH), lambda b, q: (b, 0, 0))
    kblk = pl.BlockSpec((seq, NSA_KV_WIDTH), lambda b, q: (b, 0))
    vblk = pl.BlockSpec((NSA_KV_WIDTH, seq), lambda b, q: (0, b))
    full = lambda a: pl.BlockSpec(a.shape, lambda b, q: (0,) * a.ndim)
    return pl.pallas_call(
        _nsa_kernel,
        grid=(batch, n_qb),
        in_specs=[qblk, qblk, cblk, cblk, kblk, vblk, kblk, vblk,
                  pl.BlockSpec((LANES, Q_BLOCK), lambda b, q: (0, b * n_qb + q)), full(ov), full(wbias)],
        out_specs=pl.BlockSpec((Q_BLOCK, NSA_WIDTH), lambda b, q: (b * n_qb + q, 0)),
        out_shape=jax.ShapeDtypeStruct((tokens, NSA_WIDTH), BF16),
        scratch_shapes=[pltpu.VMEM((NSA_KV_HEADS, seq // SEL_LEN, 8, NSA_GROUP * Q_BLOCK), F32)],
        compiler_params=_params("arbitrary", "arbitrary"),
        name="nsa_attention",
    )(qu, qr, kc, vc, ks, vs, kw, vw, ng, ov, wbias)


def _outproj_kernel(yr_ref, yc_ref, yn_ref, x_ref, mod_ref, w_ref, lng_ref, lnb_ref, rwh_ref, rwl_ref, rb_ref,
                    x1_ref, h2a_ref, h2b_ref, h2c_ref, h2d_ref, ti_ref, tw_ref, cnt_ref, *, alpha):
    tm = x_ref.shape[0]
    mix = (_dot(yr_ref[...], w_ref[0, 0:RET_WIDTH, :])
           + _dot(yc_ref[...], w_ref[0, RET_WIDTH:RET_WIDTH + CONV_WIDTH, :])
           + _dot(yn_ref[...], w_ref[0, RET_WIDTH + CONV_WIDTH:, :]))
    x1 = _layer_norm(alpha * x_ref[...] + (1.0 + mod_ref[2, 0]) * mix) * lng_ref[0, 0] + lnb_ref[0, 0]
    x1_ref[...] = x1
    h2 = _layer_norm(x1) * (1.0 + mod_ref[4, 0]) + mod_ref[3, 0]
    for j, part_ref in enumerate((h2a_ref, h2b_ref, h2c_ref, h2d_ref)):
        part_ref[...] = h2[:, j * SPLIT_WIDTH:(j + 1) * SPLIT_WIDTH]
    h_hi = h2.astype(BF16)
    h_lo = (h2 - h_hi.astype(F32)).astype(BF16)
    logits = _dot(h_hi, rwh_ref[0]) + _dot(h_lo, rwh_ref[0]) + _dot(h_hi, rwl_ref[0]) + rb_ref[0]

    lane = lax.broadcasted_iota(jnp.int32, (tm, LANES), 1)
    lanef = lane.astype(F32)
    rest = jnp.where(lane < N_EXPERTS, logits, -jnp.inf)
    vals, idxs = [], []
    for _ in range(TOP_K):
        top = jnp.max(rest, axis=-1, keepdims=True)
        idx = jnp.min(jnp.where(rest == top, lanef, float(LANES)), axis=-1, keepdims=True)
        vals.append(top)
        idxs.append(idx)
        rest = jnp.where(lanef == idx, -jnp.inf, rest)
    exps = [jnp.exp(v - vals[0]) for v in vals]
    inv = 1.0 / functools.reduce(lambda a, b: a + b, exps)
    top_w = jnp.zeros((tm, LANES), F32)
    top_i = jnp.zeros((tm, LANES), F32)
    member = jnp.zeros((tm, LANES), F32)
    for k in range(TOP_K):
        top_w = jnp.where(lane == k, exps[k] * inv, top_w)
        top_i = jnp.where(lane == k, idxs[k], top_i)
        member = member + jnp.where(lanef == idxs[k], 1.0, 0.0)
    tw_ref[...] = top_w
    ti_ref[...] = top_i.astype(jnp.int32)

    @pl.when(pl.program_id(0) == 0)
    def _():
        cnt_ref[...] = jnp.zeros_like(cnt_ref)

    cnt_ref[...] += jnp.broadcast_to(jnp.sum(member, axis=0, keepdims=True), cnt_ref.shape)


def _out_projection(y_ret, y_conv, y_nsa, x, mod_l, w_out, ln_g, ln_b, rw_hi, rw_lo, rb, layer, seq, alpha):
    tokens = x.shape[0]
    tm = ROW_TILE
    tiles_per_seq = seq // tm
    row = lambda w: pl.BlockSpec((tm, w), lambda i: (i, 0))
    lay3 = lambda a: pl.BlockSpec((1,) + a.shape[1:], lambda i: (layer,) + (0,) * (a.ndim - 1))
    return pl.pallas_call(
        functools.partial(_outproj_kernel, alpha=alpha),
        grid=(tokens // tm,),
        in_specs=[row(RET_WIDTH), row(CONV_WIDTH), row(NSA_WIDTH), row(D_MODEL),
                  pl.BlockSpec((6, 1, 1, D_MODEL), lambda i: (0, i // tiles_per_seq, 0, 0)),
                  lay3(w_out),
                  pl.BlockSpec((1, 1, 1, D_MODEL), lambda i: (layer, 0, 0, 0)),
                  pl.BlockSpec((1, 1, 1, D_MODEL), lambda i: (layer, 0, 0, 0)),
                  lay3(rw_hi), lay3(rw_lo), lay3(rb)],
        out_specs=[row(D_MODEL)] + [row(SPLIT_WIDTH)] * SC_SPLIT + [row(LANES), row(LANES),
                   pl.BlockSpec((8, LANES), lambda i: (0, 0))],
        out_shape=[jax.ShapeDtypeStruct((tokens, D_MODEL), F32)]
                  + [jax.ShapeDtypeStruct((tokens, SPLIT_WIDTH), F32)] * SC_SPLIT
                  + [jax.ShapeDtypeStruct((tokens, LANES), jnp.int32), jax.ShapeDtypeStruct((tokens, LANES), F32),
                   jax.ShapeDtypeStruct((8, LANES), F32)],
        compiler_params=_params("arbitrary"),
        name="outproj_norm_router",
    )(y_ret, y_conv, y_nsa, x, mod_l, w_out, ln_g, ln_b, rw_hi, rw_lo, rb)


def _route_kernel(ti_ref, pstart_ref, tri_ref, pos_ref, carry_ref):
    @pl.when(pl.program_id(0) == 0)
    def _():
        carry_ref[...] = jnp.zeros_like(carry_ref)

    tm = ti_ref.shape[0]
    lane = lax.broadcasted_iota(jnp.int32, (tm, LANES), 1)
    top_i = ti_ref[...]
    onehots = [lane == top_i[:, k:k + 1] for k in range(TOP_K)]
    member = functools.reduce(lambda a, b: a + b, [jnp.where(o, 1.0, 0.0) for o in onehots])
    base = pstart_ref[...] + carry_ref[0:1] + _dot(tri_ref[...], member.astype(BF16))
    pos = jnp.zeros((tm, LANES), F32)
    for k in range(TOP_K):
        pos_k = jnp.sum(jnp.where(onehots[k], base, 0.0), axis=-1, keepdims=True)
        pos = jnp.where(lane == k, pos_k, pos)
    pos_ref[...] = pos.astype(jnp.int32)
    carry_ref[...] += jnp.broadcast_to(jnp.sum(member, axis=0, keepdims=True), carry_ref.shape)


def _route_positions(top_i, pstart):
    tokens = top_i.shape[0]
    tm = ROW_TILE
    r = np.arange(tm)
    tri = jnp.asarray(r[None, :] < r[:, None], BF16)
    return pl.pallas_call(
        _route_kernel,
        grid=(tokens // tm,),
        in_specs=[pl.BlockSpec((tm, LANES), lambda i: (i, 0)),
                  pl.BlockSpec((1, LANES), lambda i: (0, 0)),
                  pl.BlockSpec((tm, tm), lambda i: (0, 0))],
        out_specs=pl.BlockSpec((tm, LANES), lambda i: (i, 0)),
        out_shape=jax.ShapeDtypeStruct((tokens, LANES), jnp.int32),
        scratch_shapes=[pltpu.VMEM((8, LANES), F32)],
        compiler_params=_params("arbitrary"),
        name="route_positions",
    )(top_i, pstart, tri)


def _block_table(counts, n_blocks):
    tm = MOE_TILE
    cnt = counts[0, :N_EXPERTS].astype(jnp.int32)
    nblk = (cnt + tm - 1) // tm
    bend = jnp.cumsum(nblk)
    bstart = bend - nblk
    blocks = jnp.arange(n_blocks, dtype=jnp.int32)
    block_e = jnp.minimum(jnp.sum((bend[None, :] <= blocks[:, None]).astype(jnp.int32), axis=1), N_EXPERTS - 1)
    n_valid = jnp.clip(cnt[block_e] - (blocks - bstart[block_e]) * tm, 0, tm).astype(jnp.int32)
    n_valid = jnp.where(blocks < bend[-1], n_valid, 0)
    pstart = jnp.pad((bstart * tm).astype(F32), (0, LANES - N_EXPERTS))[None, :]
    return pstart, block_e.astype(jnp.int32), bend[-1:].astype(jnp.int32), n_valid


def _sc_mesh():
    return plsc.VectorSubcoreMesh(core_axis_name="core", subcore_axis_name="subcore")


def _scatter_rows(xs, idx_t, n_rows):
    tokens, width = xs[0].shape
    n_idx = idx_t.shape[0]
    n_x = len(xs)
    win = SC_WINDOW

    @functools.partial(pl.kernel, out_type=[jax.ShapeDtypeStruct((n_rows, width), xs[0].dtype)] * n_x,
                       mesh=_sc_mesh(), scratch_types=[], name="dispatch_scatter")
    def scatter(*refs):
        x_hbms, i_hbm, o_hbms = refs[:n_x], refs[n_x], refs[n_x + 1:]
        idx_specs = [pl.BlockSpec((1, win), functools.partial(lambda i, k: (k, i), k=k)) for k in range(n_idx)]
        for x_hbm, o_hbm in zip(x_hbms, o_hbms):
            def body(x_vmem, *i_vmems, o_hbm=o_hbm):
                for i_vmem in i_vmems:
                    pltpu.sync_copy(x_vmem, o_hbm.at[i_vmem.at[0]])

            pltpu.emit_pipeline(
                body,
                grid=(tokens // win,),
                in_specs=[pl.BlockSpec((win, width), lambda i: (i, 0))] + idx_specs,
                out_specs=[],
                core_axis_name=("core", "subcore"),
                dimension_semantics=(pltpu.PARALLEL,),
            )(x_hbm, *([i_hbm] * n_idx))

    return scatter(*xs, idx_t)


def _gather_rows(xs, idx):
    width = xs[0].shape[1]
    n = idx.shape[0]
    n_x = len(xs)
    win = SC_WINDOW

    @functools.partial(pl.kernel, out_type=[jax.ShapeDtypeStruct((n, width), xs[0].dtype)] * n_x,
                       mesh=_sc_mesh(), scratch_types=[], name="combine_gather")
    def gather(*refs):
        x_hbms, i_hbm, o_hbms = refs[:n_x], refs[n_x], refs[n_x + 1:]
        for x_hbm, o_hbm in zip(x_hbms, o_hbms):
            def body(i_vmem, o_vmem, x_hbm=x_hbm):
                pltpu.sync_copy(x_hbm.at[i_vmem.at[0]], o_vmem)

            pltpu.emit_pipeline(
                body,
                grid=(n // win,),
                in_specs=[pl.BlockSpec((1, win), lambda i: (0, i))],
                out_specs=[pl.BlockSpec((win, width), lambda i: (i, 0))],
                core_axis_name=("core", "subcore"),
                dimension_semantics=(pltpu.PARALLEL,),
            )(i_hbm, o_hbm)

    return gather(*xs, idx.reshape(1, n))


def _moe_kernel(be_ref, nb_ref, nv_ref, ra_ref, rb_ref, rc_ref, rd_ref, wgu_ref, bgu_ref, wd_ref, bd_ref,
                oa_ref, ob_ref, oc_ref, od_ref, wgu_bf, wd_bf):
    row_refs = (ra_ref, rb_ref, rc_ref, rd_ref)
    out_refs = (oa_ref, ob_ref, oc_ref, od_ref)
    i = pl.program_id(0)
    expert = be_ref[i]
    first = (i == 0) | (expert != be_ref[jnp.maximum(i - 1, 0)])
    valid = i < nb_ref[0]

    @pl.when(valid & first)
    def _():
        wgu_bf[...] = wgu_ref[0, 0].astype(BF16)
        wd_bf[...] = wd_ref[0, 0].astype(BF16)

    @pl.when(valid)
    def _():
        live = lax.broadcasted_iota(jnp.int32, ra_ref.shape, 0) < nv_ref[i]
        gu = bgu_ref[0, 0]
        for j, r_ref in enumerate(row_refs):
            rows = jnp.where(live, r_ref[...], 0.0).astype(BF16)
            gu = gu + _dot(rows, wgu_bf[j * SPLIT_WIDTH:(j + 1) * SPLIT_WIDTH, :])
        g = jnp.minimum(gu[:, :D_EXPERT], SWIGLU_LIMIT)
        u = jnp.clip(gu[:, D_EXPERT:], -SWIGLU_LIMIT, SWIGLU_LIMIT)
        act = (u + 1.0) * (g * jax.nn.sigmoid(SWIGLU_ALPHA * g))
        y = _dot(act.astype(BF16), wd_bf[...]) + bd_ref[0, 0]
        for j, o_ref in enumerate(out_refs):
            o_ref[...] = y[:, j * SPLIT_WIDTH:(j + 1) * SPLIT_WIDTH]

    @pl.when(jnp.logical_not(valid))
    def _():
        for o_ref in out_refs:
            o_ref[...] = jnp.zeros_like(o_ref)


def _expert_ffn(rows, block_e, n_used, n_valid, w_gu, b_gu, w_down, b_down, layer):
    n_rows = rows[0].shape[0]
    tm = MOE_TILE
    part = pl.BlockSpec((tm, SPLIT_WIDTH), lambda i, be, nb, nv: (i, 0))
    grid_spec = pltpu.PrefetchScalarGridSpec(
        num_scalar_prefetch=3,
        grid=(n_rows // tm,),
        in_specs=[part] * SC_SPLIT + [
                  pl.BlockSpec((1, 1, D_MODEL, 2 * D_EXPERT), lambda i, be, nb, nv: (layer, be[i], 0, 0)),
                  pl.BlockSpec((1, 1, 1, 2 * D_EXPERT), lambda i, be, nb, nv: (layer, be[i], 0, 0)),
                  pl.BlockSpec((1, 1, D_EXPERT, D_MODEL), lambda i, be, nb, nv: (layer, be[i], 0, 0)),
                  pl.BlockSpec((1, 1, 1, D_MODEL), lambda i, be, nb, nv: (layer, be[i], 0, 0))],
        out_specs=[part] * SC_SPLIT,
        scratch_shapes=[pltpu.VMEM((D_MODEL, 2 * D_EXPERT), BF16), pltpu.VMEM((D_EXPERT, D_MODEL), BF16)])
    return pl.pallas_call(
        _moe_kernel,
        grid_spec=grid_spec,
        out_shape=[jax.ShapeDtypeStruct((n_rows, SPLIT_WIDTH), F32)] * SC_SPLIT,
        compiler_params=_params("arbitrary"),
        name="expert_ffn",
    )(block_e, n_used, n_valid, *rows, w_gu, b_gu, w_down, b_down)


def _final_kernel(x_ref, tw_ref, mod_ref, lng_ref, lnb_ref, *rest, alpha):
    y_refs, o_ref = rest[:-1], rest[-1]
    top_w = tw_ref[...]
    parts = []
    for j in range(SC_SPLIT):
        acc = top_w[:, 0:1] * y_refs[j * TOP_K][...]
        for k in range(1, TOP_K):
            acc = acc + top_w[:, k:k + 1] * y_refs[j * TOP_K + k][...]
        parts.append(acc)
    ffn = jnp.concatenate(parts, axis=1)
    o_ref[...] = (_layer_norm(alpha * x_ref[...] + (1.0 + mod_ref[5, 0]) * ffn) * lng_ref[0, 0] + lnb_ref[0, 0])


def _final_norm(x1, y_parts, top_w, mod_l, ln_g, ln_b, layer, seq, alpha):
    tokens = x1.shape[0]
    tm = ROW_TILE
    tiles = tokens // tm
    tiles_per_seq = seq // tm
    row = lambda w: pl.BlockSpec((tm, w), lambda i: (i, 0))
    vec = pl.BlockSpec((1, 1, 1, D_MODEL), lambda i: (layer, 1, 0, 0))
    y_specs, y_args = [], []
    for j in range(SC_SPLIT):
        for k in range(TOP_K):
            y_specs.append(pl.BlockSpec((tm, SPLIT_WIDTH), functools.partial(lambda i, k: (k * tiles + i, 0), k=k)))
            y_args.append(y_parts[j])
    return pl.pallas_call(
        functools.partial(_final_kernel, alpha=alpha),
        grid=(tiles,),
        in_specs=[row(D_MODEL), row(LANES),
                  pl.BlockSpec((6, 1, 1, D_MODEL), lambda i: (0, i // tiles_per_seq, 0, 0)), vec, vec] + y_specs,
        out_specs=row(D_MODEL),
        out_shape=jax.ShapeDtypeStruct(x1.shape, F32),
        compiler_params=_params("arbitrary"),
        name="combine_final_norm",
    )(x1, top_w, mod_l, ln_g, ln_b, *y_args)


def kernel(x, c, positions, w_in, w_out, ret_gn_w, conv_w, cmp_pos, cmp_w1, cmp_w2, ada_w, ada_b, ln_g, ln_b,
           router_w, router_b, w_gate_up, b_gate_up, w_down, b_down):
    batch, seq, _ = x.shape
    depth = w_in.shape[0]
    tokens = batch * seq
    n_rows = tokens * TOP_K + N_EXPERTS * MOE_TILE
    alpha = float((2 * depth) ** 0.25)

    w_in_r = jnp.pad(w_in, ((0, 0), (0, 0), (0, N_COLS - w_in.shape[2]))).astype(BF16)
    w_out_b = w_out.astype(BF16)
    rw = jnp.pad(router_w, ((0, 0), (0, 0), (0, LANES - N_EXPERTS)))
    rw_hi = rw.astype(BF16)
    rw_lo = (rw - rw_hi.astype(F32)).astype(BF16)
    rb = jnp.pad(router_b, ((0, 0), (0, LANES - N_EXPERTS))).reshape(depth, 1, LANES)
    cw = _compress_weights(cmp_pos, cmp_w1, cmp_w2)
    ret_consts = _retention_consts()
    nsa_consts = _nsa_consts(seq)
    gn_w = ret_gn_w.reshape(depth, 1, RET_WIDTH)
    ln_g4 = ln_g.reshape(depth, 2, 1, D_MODEL)
    ln_b4 = ln_b.reshape(depth, 2, 1, D_MODEL)
    b_gu4 = b_gate_up.reshape(depth, N_EXPERTS, 1, 2 * D_EXPERT)
    b_d4 = b_down.reshape(depth, N_EXPERTS, 1, D_MODEL)

    mod = _modulation(c, ada_w, ada_b)
    cos, sin = _rope_tables(positions)
    xt = x.reshape(tokens, D_MODEL)
    for l in range(depth):
        (rq, rk, rv, rg, y_conv, qu, qr, kcx, vcx, ks, vs, kw, vw, ng) = _input_projection(
            xt, mod[l], w_in_r, cos, sin, conv_w, l, seq)
        y_ret = _retention(rq, rk, rv, rg, ret_consts, gn_w, l, batch, seq)
        kc, vc = _compress(kcx, vcx, cw, l, batch, seq)
        y_nsa = _nsa(qu, qr, kc, vc, ks, vs, kw, vw, ng, nsa_consts, batch, seq)
        x1, *h2, top_i, top_w, counts = _out_projection(y_ret, y_conv, y_nsa, xt, mod[l], w_out_b, ln_g4, ln_b4,
                                                        rw_hi, rw_lo, rb, l, seq, alpha)
        pstart, block_e, n_used, n_valid = _block_table(counts, n_rows // MOE_TILE)
        pos_t = _route_positions(top_i, pstart)[:, :TOP_K].T
        rows = _scatter_rows(h2, pos_t, n_rows)
        y = _expert_ffn(rows, block_e, n_used, n_valid, w_gate_up, b_gu4, w_down, b_d4, l)
        y_tok = _gather_rows(y, pos_t.reshape(-1))
        xt = _final_norm(x1, y_tok, top_w, mod[l], ln_g4, ln_b4, l, seq, alpha)
    return xt.reshape(batch, seq, D_MODEL)
```

```python
import functools

import numpy as np
import jax
import jax.numpy as jnp
from jax import lax
from jax.experimental import pallas as pl
from jax.experimental.pallas import tpu as pltpu
from jax.experimental.pallas import tpu_sc as plsc

F32 = jnp.float32
BF16 = jnp.bfloat16

D_MODEL = 1024
HEAD_DIM = 64
RET_WIDTH = 256
RET_HEADS = 4
RET_CHUNK = 128
CONV_WIDTH = 256
CONV_K = 3
NSA_WIDTH = 512
NSA_HEADS = 8
NSA_KV_HEADS = 2
NSA_GROUP = 4
NSA_KV_WIDTH = 128
CMP_LEN = 32
CMP_STRIDE = 16
CMP_HIDDEN = 128
SEL_LEN = 64
SEL_TOP = 8
WINDOW = 512
Q_BLOCK = 128
ROPE_THETA = 10000.0
N_EXPERTS = 32
TOP_K = 4
D_EXPERT = 1024
SWIGLU_LIMIT = 7.0
SWIGLU_ALPHA = 1.702
LN_EPS = 1e-5
NEG_INF = -1e30
FORCE_SCORE = 1e9

LANES = 128
VMEM_LIMIT = 56 * 1024 * 1024

OFF_RQ, OFF_RK, OFF_RV, OFF_RG = 0, 256, 512, 768
OFF_CB, OFF_CC, OFF_CH = 1024, 1280, 1536
OFF_NQ = 1792
OFF_KC, OFF_VC = 2304, 2432
OFF_KS, OFF_VS, OFF_KW, OFF_VW = 2560, 2688, 2816, 2944
OFF_NG = 3072
N_GATE = NSA_HEADS * 3
N_COLS = OFF_NG + LANES
LOG2E = 1.4426950408889634

ROW_TILE = 256
MOE_TILE = 256
SEL_GROUP = 4
V_ROWS = HEAD_DIM + 16
SC_WINDOW = LANES
SC_SPLIT = 2
SPLIT_WIDTH = D_MODEL // (2 * SC_SPLIT)


def _dot(a, b):
    return jnp.dot(a, b, preferred_element_type=F32)


def _dot_nt(a, b):
    return lax.dot_general(a, b, (((1,), (1,)), ((), ())), preferred_element_type=F32)


def _layer_norm(x):
    mu = jnp.mean(x, axis=-1, keepdims=True)
    xc = x - mu
    var = jnp.mean(xc * xc, axis=-1, keepdims=True)
    return xc * lax.rsqrt(var + LN_EPS)


def _pack_pairs(x):
    bits = lambda t: lax.bitcast_convert_type(t.astype(BF16).astype(F32), jnp.uint32)
    return (bits(x[:, SPLIT_WIDTH:]) & jnp.uint32(0xFFFF0000)) | (bits(x[:, :SPLIT_WIDTH]) >> 16)


def _unpack_pairs(p):
    low = lax.bitcast_convert_type(p << 16, F32)
    high = lax.bitcast_convert_type(p & jnp.uint32(0xFFFF0000), F32)
    return low, high


def _params(*sem):
    return pltpu.CompilerParams(dimension_semantics=sem, vmem_limit_bytes=VMEM_LIMIT)


def _mod_kernel(c_ref, w_ref, b_ref, o_ref):
    c = c_ref[...]
    ca = (c * jax.nn.sigmoid(c)).astype(BF16)
    o_ref[0, 0] = _dot(ca, w_ref[0].astype(BF16)) + b_ref[0]


def _modulation(c, ada_w, ada_b):
    depth = ada_w.shape[0]
    batch = c.shape[0]
    out = pl.pallas_call(
        _mod_kernel,
        grid=(depth, 6),
        in_specs=[pl.BlockSpec((batch, D_MODEL), lambda l, j: (0, 0)),
                  pl.BlockSpec((1, D_MODEL, D_MODEL), lambda l, j: (l, 0, j)),
                  pl.BlockSpec((1, 1, D_MODEL), lambda l, j: (l * 6 + j, 0, 0))],
        out_specs=pl.BlockSpec((1, 1, batch, D_MODEL), lambda l, j: (l, j, 0, 0)),
        out_shape=jax.ShapeDtypeStruct((depth, 6, batch, D_MODEL), F32),
        compiler_params=_params("arbitrary", "arbitrary"),
        name="adaln_mod",
    )(c, ada_w, ada_b.reshape(depth * 6, 1, D_MODEL))
    return out.reshape(depth, 6, batch, 1, D_MODEL)


def _rope_table_kernel(pos_ref, inv_ref, cos_ref, sin_ref):
    ang = pos_ref[...] * inv_ref[...]
    cos_ref[...] = jnp.cos(ang)
    sin_ref[...] = jnp.sin(ang)


def _rope_tables(positions):
    half = HEAD_DIM // 2
    per_row = LANES // half
    tokens = positions.size
    pos4 = jnp.repeat(positions.reshape(tokens // per_row, per_row).astype(F32), half, axis=1)
    inv = ROPE_THETA ** (-jnp.arange(half, dtype=F32) / half)
    inv4 = jnp.tile(inv, per_row)[None, :]
    rows = tokens // per_row
    tile = min(rows, 1024)
    cos4, sin4 = pl.pallas_call(
        _rope_table_kernel,
        grid=(rows // tile,),
        in_specs=[pl.BlockSpec((tile, LANES), lambda i: (i, 0)),
                  pl.BlockSpec((1, LANES), lambda i: (0, 0))],
        out_specs=[pl.BlockSpec((tile, LANES), lambda i: (i, 0))] * 2,
        out_shape=[jax.ShapeDtypeStruct((rows, LANES), F32)] * 2,
        compiler_params=_params("arbitrary"),
        name="rope_tables",
    )(pos4, inv4)
    cos = jnp.tile(cos4.reshape(tokens, half), (1, per_row))
    sign = jnp.tile(jnp.concatenate([-jnp.ones((half,), F32), jnp.ones((half,), F32)]), LANES // HEAD_DIM)
    sin = jnp.tile(sin4.reshape(tokens, half), (1, per_row)) * sign[None, :]
    return cos, sin


def _inproj_kernel(x_ref, mod_ref, w_ref, cos_ref, sin_ref, convw_ref,
                   rq_ref, rk_ref, rv_ref, rg_ref, yc_ref, qu_ref, qr_ref, kc_ref, vc_ref,
                   ks_ref, vs_ref, kw_ref, vw_ref, ng_ref, carry_ref, *, tiles_per_seq):
    i = pl.program_id(0)
    tm = x_ref.shape[0]
    h = (_layer_norm(x_ref[...]) * (1.0 + mod_ref[1, 0]) + mod_ref[0, 0]).astype(BF16)
    cosf = cos_ref[...]
    sinf = sin_ref[...]
    lane = lax.broadcasted_iota(jnp.int32, (tm, LANES), 1)
    first_half = (lane % HEAD_DIM) < (HEAD_DIM // 2)

    def proj(off, width):
        return _dot(h, w_ref[0, :, off:off + width])

    def rope(c):
        cols = []
        for j in range(c.shape[1] // LANES):
            cj = c[:, j * LANES:(j + 1) * LANES]
            swapped = jnp.where(first_half, pltpu.roll(cj, LANES - HEAD_DIM // 2, 1),
                                pltpu.roll(cj, HEAD_DIM // 2, 1))
            cols.append(cj * cosf + swapped * sinf)
        return jnp.concatenate(cols, axis=1) if len(cols) > 1 else cols[0]

    scale = HEAD_DIM ** -0.5
    rq_ref[...] = rope(proj(OFF_RQ, RET_WIDTH)).astype(BF16)
    rk_ref[...] = (rope(proj(OFF_RK, RET_WIDTH)) * scale).astype(BF16)
    rv_ref[...] = proj(OFF_RV, RET_WIDTH).astype(BF16)
    rg = proj(OFF_RG, RET_WIDTH)
    rg_ref[...] = (rg * jax.nn.sigmoid(rg)).astype(BF16)

    cb = proj(OFF_CB, CONV_WIDTH)
    u = proj(OFF_CC, CONV_WIDTH) * proj(OFF_CH, CONV_WIDTH)

    @pl.when(i % tiles_per_seq == 0)
    def _():
        carry_ref[...] = jnp.zeros_like(carry_ref)

    carry = carry_ref[...]
    row = lax.broadcasted_iota(jnp.int32, (tm, CONV_WIDTH), 0)
    prev1 = jnp.where(row == 0, carry[7:8], pltpu.roll(u, 1, 0))
    prev2 = jnp.where(row == 0, carry[6:7], jnp.where(row == 1, carry[7:8], pltpu.roll(u, 2, 0)))
    cw = convw_ref[0]
    yc_ref[...] = (cb * (cw[0:1] * prev2 + cw[1:2] * prev1 + cw[2:3] * u)).astype(BF16)
    carry_ref[...] = u[tm - 8:tm]

    nq = proj(OFF_NQ, NSA_WIDTH) * (scale * LOG2E)
    qu_ref[...] = nq.T.astype(BF16)
    qr_ref[...] = rope(nq).T.astype(BF16)
    kc_ref[...] = proj(OFF_KC, NSA_KV_WIDTH)
    vc_ref[...] = proj(OFF_VC, NSA_KV_WIDTH)
    def values_t(off):
        v_t = proj(off, NSA_KV_WIDTH).T
        ones = jnp.ones((V_ROWS - HEAD_DIM, tm), F32)
        parts = []
        for hd in range(NSA_KV_HEADS):
            parts += [v_t[hd * HEAD_DIM:(hd + 1) * HEAD_DIM], ones]
        return jnp.concatenate(parts, axis=0).astype(BF16)

    ks_ref[...] = rope(proj(OFF_KS, NSA_KV_WIDTH)).astype(BF16)
    vs_ref[...] = values_t(OFF_VS)
    kw_ref[...] = rope(proj(OFF_KW, NSA_KV_WIDTH)).astype(BF16)
    vw_ref[...] = values_t(OFF_VW)
    ng_ref[...] = proj(OFF_NG, LANES).T


def _input_projection(x, mod_l, w_l, cos, sin, conv_w, layer, seq):
    tokens = x.shape[0]
    tm = ROW_TILE
    tiles_per_seq = seq // tm
    row = lambda w: pl.BlockSpec((tm, w), lambda i: (i, 0))
    col = lambda w: pl.BlockSpec((w, tm), lambda i: (0, i))
    outs = [(RET_WIDTH, BF16, False)] * 4 + [(CONV_WIDTH, BF16, False), (NSA_WIDTH, BF16, True),
                                            (NSA_WIDTH, BF16, True), (NSA_KV_WIDTH, F32, False),
                                            (NSA_KV_WIDTH, F32, False), (NSA_KV_WIDTH, BF16, False),
                                            (NSA_KV_HEADS * V_ROWS, BF16, True), (NSA_KV_WIDTH, BF16, False),
                                            (NSA_KV_HEADS * V_ROWS, BF16, True), (LANES, F32, True)]
    return pl.pallas_call(
        functools.partial(_inproj_kernel, tiles_per_seq=tiles_per_seq),
        grid=(tokens // tm,),
        in_specs=[row(D_MODEL),
                  pl.BlockSpec((6, 1, 1, D_MODEL), lambda i: (0, i // tiles_per_seq, 0, 0)),
                  pl.BlockSpec((1, D_MODEL, N_COLS), lambda i: (layer, 0, 0)),
                  row(LANES), row(LANES),
                  pl.BlockSpec((1, CONV_K, CONV_WIDTH), lambda i: (layer, 0, 0))],
        out_specs=[col(w) if t else row(w) for w, _, t in outs],
        out_shape=[jax.ShapeDtypeStruct((w, tokens) if t else (tokens, w), dt) for w, dt, t in outs],
        scratch_shapes=[pltpu.VMEM((8, CONV_WIDTH), F32)],
        compiler_params=_params("arbitrary"),
        name="ln_inproj",
    )(x, mod_l, w_l, cos, sin, conv_w)


def _retention_kernel(q_ref, k_ref, v_ref, g_ref, intra_ref, qdec_ref, kdec_ref, cdec_ref, gn_ref,
                      o_ref, state_ref):
    @pl.when(pl.program_id(1) == 0)
    def _():
        state_ref[...] = jnp.zeros_like(state_ref)

    n_pairs = RET_WIDTH // LANES
    lane = lax.broadcasted_iota(jnp.int32, (RET_CHUNK, LANES), 1)
    low = lane < HEAD_DIM
    blk_r = lax.broadcasted_iota(jnp.int32, (LANES, LANES), 0) < HEAD_DIM
    blk_c = lax.broadcasted_iota(jnp.int32, (LANES, LANES), 1) < HEAD_DIM
    same_head = blk_r == blk_c
    for ch in range(q_ref.shape[0] // RET_CHUNK):
        rows = slice(ch * RET_CHUNK, (ch + 1) * RET_CHUNK)
        outs = []
        for p in range(n_pairs):
            cols = slice(p * LANES, (p + 1) * LANES)
            qp = q_ref[rows, cols]
            kp = k_ref[rows, cols]
            vp = v_ref[rows, cols]
            out = jnp.zeros((RET_CHUNK, LANES), F32)
            for hh in range(2):
                keep = low if hh == 0 else jnp.logical_not(low)
                qm = jnp.where(keep, qp, jnp.zeros_like(qp))
                s = _dot_nt(qm, kp) * intra_ref[2 * p + hh]
                out = jnp.where(keep, _dot(s.astype(BF16), vp), out)
            state = state_ref[p]
            qd = (qp.astype(F32) * qdec_ref[:, cols]).astype(BF16)
            out = out + _dot(qd, state.astype(BF16))
            kd_t = (kp.astype(F32) * kdec_ref[:, cols]).T.astype(BF16)
            upd = _dot(kd_t, vp)
            state_ref[p] = state * cdec_ref[p] + jnp.where(same_head, upd, 0.0)
            def half_mean(t):
                lo_sum = jnp.sum(jnp.where(low, t, 0.0), axis=-1, keepdims=True)
                hi_sum = jnp.sum(jnp.where(low, 0.0, t), axis=-1, keepdims=True)
                return jnp.where(low, lo_sum, hi_sum) * (1.0 / HEAD_DIM)
            oc = out - half_mean(out)
            outs.append(oc * lax.rsqrt(half_mean(oc * oc) + LN_EPS))
        normed = jnp.concatenate(outs, axis=1)
        o_ref[rows, :] = (normed * gn_ref[0] * g_ref[rows, :].astype(F32)).astype(BF16)


def _retention_consts():
    heads = jnp.arange(RET_HEADS, dtype=F32)
    log_gamma = jnp.log(1.0 - jnp.power(2.0, -5.0 - heads))
    i = jnp.arange(RET_CHUNK, dtype=F32)
    diff = i[:, None] - i[None, :]
    intra = jnp.where(diff >= 0, jnp.exp(diff * log_gamma[:, None, None]), 0.0)
    q_dec = jnp.exp((i + 1.0) * log_gamma[:, None])
    k_dec = jnp.exp((RET_CHUNK - 1.0 - i) * log_gamma[:, None])
    c_dec = jnp.exp(RET_CHUNK * log_gamma)
    expand = lambda t: jnp.repeat(t.T, HEAD_DIM, axis=1)
    c_rows = jnp.repeat(c_dec, HEAD_DIM).reshape(RET_WIDTH // LANES, LANES, 1)
    c_blk = jnp.broadcast_to(c_rows, (RET_WIDTH // LANES, LANES, LANES))
    return intra, expand(q_dec), expand(k_dec), c_blk


def _retention(rq, rk, rv, rg, consts, gn_w, layer, batch, seq):
    intra, q_dec, k_dec, c_blk = consts
    rows = 512
    steps = seq // rows
    blk = pl.BlockSpec((rows, RET_WIDTH), lambda b, s: (b * steps + s, 0))
    full = lambda a: pl.BlockSpec(a.shape, lambda b, s: (0,) * a.ndim)
    return pl.pallas_call(
        _retention_kernel,
        grid=(batch, steps),
        in_specs=[blk, blk, blk, blk, full(intra), full(q_dec), full(k_dec), full(c_blk),
                  pl.BlockSpec((1, 1, RET_WIDTH), lambda b, s: (layer, 0, 0))],
        out_specs=blk,
        out_shape=jax.ShapeDtypeStruct(rq.shape, BF16),
        scratch_shapes=[pltpu.VMEM((RET_WIDTH // LANES, LANES, LANES), F32)],
        compiler_params=_params("arbitrary", "arbitrary"),
        name="retention",
    )(rq, rk, rv, rg, intra, q_dec, k_dec, c_blk, gn_w)


def _compress_kernel(xk_ref, xv_ref, posa_ref, posb_ref, w1a_ref, w1b_ref, w2_ref, kc_ref, vc_ref):
    n_grp = xk_ref.shape[0] // CMP_STRIDE
    for kv, (x_ref, o_ref) in enumerate(((xk_ref, kc_ref), (xv_ref, vc_ref))):
        ya = jnp.zeros((n_grp, NSA_KV_HEADS * CMP_HIDDEN), F32)
        yb = jnp.zeros((n_grp, NSA_KV_HEADS * CMP_HIDDEN), F32)
        for r in range(CMP_STRIDE):
            x = x_ref[pl.ds(r, n_grp, stride=CMP_STRIDE), :]
            cols = slice(r * NSA_KV_WIDTH, (r + 1) * NSA_KV_WIDTH)
            ya = ya + _dot((x + posa_ref[0, kv, :, cols]).astype(BF16), w1a_ref[0, kv, cols, :])
            yb = yb + _dot((x + posb_ref[0, kv, :, cols]).astype(BF16), w1b_ref[0, kv, cols, :])
        hidden = ya + pltpu.roll(yb, n_grp - 1, 0)
        act = jax.nn.gelu(hidden)
        out = _dot(act.astype(BF16), w2_ref[0, kv])
        o_ref[0] = (out if kv == 0 else out.T).astype(BF16)


def _compress_weights(cmp_pos, cmp_w1, cmp_w2):
    depth = cmp_w1.shape[0]
    eye = jnp.eye(NSA_KV_HEADS, dtype=F32)
    a = cmp_w1.reshape(depth, 2, 2, CMP_STRIDE, HEAD_DIM, CMP_HIDDEN)
    w1 = jnp.einsum('lkardj,hg->lkarhdgj', a, eye)
    w1 = w1.reshape(depth, 2, 2, CMP_STRIDE * NSA_KV_WIDTH, NSA_KV_HEADS * CMP_HIDDEN).astype(BF16)
    pos = cmp_pos.reshape(depth, 2, 2, CMP_STRIDE, 1, HEAD_DIM)
    pos = jnp.broadcast_to(pos, (depth, 2, 2, CMP_STRIDE, NSA_KV_HEADS, HEAD_DIM))
    pos = pos.reshape(depth, 2, 2, 1, CMP_STRIDE * NSA_KV_WIDTH)
    w2 = jnp.einsum('lkje,hg->lkhjge', cmp_w2, eye)
    w2 = w2.reshape(depth, 2, NSA_KV_HEADS * CMP_HIDDEN, NSA_KV_WIDTH).astype(BF16)
    return w1[:, :, 0], w1[:, :, 1], pos[:, :, 0], pos[:, :, 1], w2


def _compress(kcx, vcx, cw, layer, batch, seq):
    w1a, w1b, posa, posb, w2 = cw
    n_grp = seq // CMP_STRIDE
    xblk = pl.BlockSpec((seq, NSA_KV_WIDTH), lambda b: (b, 0))
    lay = lambda a: pl.BlockSpec((1,) + a.shape[1:], lambda b: (layer,) + (0,) * (a.ndim - 1))
    oblk = pl.BlockSpec((1, n_grp, NSA_KV_WIDTH), lambda b: (b, 0, 0))
    return pl.pallas_call(
        _compress_kernel,
        grid=(batch,),
        in_specs=[xblk, xblk, lay(posa), lay(posb), lay(w1a), lay(w1b), lay(w2)],
        out_specs=[oblk, oblk],
        out_shape=[jax.ShapeDtypeStruct((batch, n_grp, NSA_KV_WIDTH), BF16)] * 2,
        compiler_params=_params("arbitrary"),
        name="nsa_compress",
    )(kcx, vcx, posa, posb, w1a, w1b, w2)


def _nsa_kernel(qu_ref, qr_ref, kc_ref, vc_ref, ks_ref, vs_ref, kw_ref, vw_ref, ng_ref, ov_ref, wbias_ref, ind_ref,
                o_ref):
    qb = pl.program_id(1)
    q0 = qb * Q_BLOCK
    n_sel = ov_ref.shape[0]
    cols4 = NSA_GROUP * Q_BLOCK
    n_past = WINDOW // Q_BLOCK
    gsig = jax.nn.sigmoid(ng_ref[...])

    def padded_q(ref, h):
        zeros = jnp.zeros((HEAD_DIM, Q_BLOCK), BF16)
        parts = []
        for g in range(NSA_GROUP):
            hq = h * NSA_GROUP + g
            x = ref[hq * HEAD_DIM:(hq + 1) * HEAD_DIM, :]
            parts.append(jnp.concatenate([x, zeros] if h == 0 else [zeros, x], axis=0))
        return jnp.concatenate(parts, axis=1)

    def tile4(t):
        return jnp.concatenate([t] * NSA_GROUP, axis=1)

    def attend(state, keys, values_t, bias, q_t):
        m, acc = state
        s = _dot(keys, q_t)
        if bias is not None:
            s = s + bias
        m_new = jnp.maximum(m, jnp.max(s, axis=0, keepdims=True))
        e = jnp.exp2(s - m_new)
        acc = jnp.exp2(m - m_new) * acc + _dot(values_t, e.astype(BF16))
        return m_new, acc

    def normalised(state):
        acc = state[1]
        return acc[0:HEAD_DIM] * (1.0 / acc[HEAD_DIM:HEAD_DIM + 1])

    init = (jnp.full((1, cols4), NEG_INF, F32), jnp.zeros((V_ROWS, cols4), F32))
    head_rows = [slice(h * V_ROWS, (h + 1) * V_ROWS) for h in range(NSA_KV_HEADS)]
    q_rot = [padded_q(qr_ref, h) for h in range(NSA_KV_HEADS)]

    def position_bias(kb, windowed):
        return tile4(wbias_ref[jnp.clip(kb - qb + n_past, 0 if windowed else 1, n_past + 1)])

    o_cmp = []
    q_sel = []
    for h in range(NSA_KV_HEADS):
        hr = slice(h * HEAD_DIM, (h + 1) * HEAD_DIM)
        qus = padded_q(qu_ref, h)

        s = _dot(kc_ref[0], qus)
        cid = lax.broadcasted_iota(jnp.int32, (LANES, Q_BLOCK), 0)
        tq = q0 + lax.broadcasted_iota(jnp.int32, (LANES, Q_BLOCK), 1)
        cvalid = tile4(jnp.where(cid * CMP_STRIDE + (CMP_LEN - 1) <= tq, 1.0, 0.0))
        sm = jnp.where(cvalid > 0.5, s, NEG_INF)
        e = jnp.exp2(sm - jnp.max(sm, axis=0, keepdims=True)) * cvalid
        l = jnp.sum(e, axis=0, keepdims=True)
        p = e * (1.0 / jnp.where(l > 0.0, l, 1.0))
        o_cmp.append(_dot(vc_ref[0, hr, :], p.astype(BF16)))

        psum = p[:, 0:Q_BLOCK]
        for g in range(1, NSA_GROUP):
            psum = psum + p[:, g * Q_BLOCK:(g + 1) * Q_BLOCK]
        p_hi = psum.astype(BF16)
        rem = psum - p_hi.astype(F32)
        p_mid = rem.astype(BF16)
        p_lo = (rem - p_mid.astype(F32)).astype(BF16)
        ov = ov_ref[...]
        imp = _dot(ov, p_hi) + _dot(ov, p_mid) + _dot(ov, p_lo)
        jid = lax.broadcasted_iota(jnp.int32, (n_sel, Q_BLOCK), 0)
        tid = q0 + lax.broadcasted_iota(jnp.int32, (n_sel, Q_BLOCK), 1)
        forced = (jid == 0) | (jid == jnp.right_shift(tid, 6))
        imp = jnp.where(forced, FORCE_SCORE, jnp.where(jid * SEL_LEN <= tid, imp, -FORCE_SCORE))
        rank = jnp.zeros((n_sel, Q_BLOCK), F32)
        for i in range(n_sel):
            ri = imp[i:i + 1, :]
            beats = (ri > imp) | ((ri == imp) & (jid > i))
            rank = rank + jnp.where(beats, 1.0, 0.0)
        sel_bias = tile4(jnp.where(rank < float(SEL_TOP), 0.0, NEG_INF)).astype(BF16)
        pad = jnp.zeros((LANES - n_sel, cols4), BF16)
        q_sel.append(jnp.concatenate([q_rot[h], sel_bias, pad], axis=0))

    def sel_step(gi, states, last):
        k0 = pl.multiple_of(gi * (SEL_GROUP * Q_BLOCK), SEL_GROUP * Q_BLOCK)
        keys = jnp.concatenate([ks_ref[pl.ds(k0, SEL_GROUP * Q_BLOCK), :], ind_ref[gi]], axis=1)
        bias = None
        if last:
            bias = jnp.concatenate([position_bias(gi * SEL_GROUP + u, False) for u in range(SEL_GROUP)], axis=0)
        return tuple(attend(states[h], keys, vs_ref[head_rows[h], pl.ds(k0, SEL_GROUP * Q_BLOCK)], bias, q_sel[h])
                     for h in range(NSA_KV_HEADS))

    last_group = qb // SEL_GROUP
    sel_states = lax.fori_loop(0, last_group, functools.partial(sel_step, last=False), (init,) * NSA_KV_HEADS)
    sel_states = sel_step(last_group, sel_states, True)

    ws = pl.multiple_of(jnp.maximum(q0 - WINDOW, 0), Q_BLOCK)
    n_win = n_past + 1
    win_bias = jnp.concatenate([position_bias(ws // Q_BLOCK + u, True) for u in range(n_win)], axis=0)
    win_keys = kw_ref[pl.ds(ws, n_win * Q_BLOCK), :]
    heads = []
    for h in range(NSA_KV_HEADS):
        o_s = normalised(sel_states[h])
        o_w = normalised(attend(init, win_keys, vw_ref[head_rows[h], pl.ds(ws, n_win * Q_BLOCK)], win_bias, q_rot[h]))
        o_c = o_cmp[h]
        for g in range(NSA_GROUP):
            hq = h * NSA_GROUP + g
            c = slice(g * Q_BLOCK, (g + 1) * Q_BLOCK)
            heads.append(gsig[3 * hq:3 * hq + 1, :] * o_c[:, c] + gsig[3 * hq + 1:3 * hq + 2, :] * o_s[:, c]
                         + gsig[3 * hq + 2:3 * hq + 3, :] * o_w[:, c])
    cols_out = [jnp.concatenate(heads[2 * m:2 * m + 2], axis=0).T for m in range(NSA_HEADS // 2)]
    o_ref[...] = jnp.concatenate(cols_out, axis=1).astype(BF16)


def _nsa_consts(seq):
    n_cmp_pad = seq // CMP_STRIDE
    n_sel = seq // SEL_LEN
    c = np.arange(n_cmp_pad)
    j = np.arange(n_sel)
    n_cmp = (seq - CMP_LEN) // CMP_STRIDE + 1
    ov = ((c[None, :] * CMP_STRIDE < j[:, None] * SEL_LEN + SEL_LEN) &
          (j[:, None] * SEL_LEN <= c[None, :] * CMP_STRIDE + CMP_LEN - 1) & (c[None, :] < n_cmp))
    kl = np.arange(Q_BLOCK)[:, None]
    tl = np.arange(Q_BLOCK)[None, :]
    n_past = WINDOW // Q_BLOCK
    valid = ([kl > tl] + [np.ones((Q_BLOCK, Q_BLOCK), bool)] * (n_past - 1) + [kl <= tl]
             + [np.zeros((Q_BLOCK, Q_BLOCK), bool)])
    wbias = np.where(np.stack(valid), 0.0, NEG_INF).astype(np.float32)
    key_block = np.arange(seq) // SEL_LEN
    onehot = (key_block[:, None] == np.arange(LANES)[None, :]).reshape(-1, SEL_GROUP * Q_BLOCK, LANES)
    return jnp.asarray(ov, BF16), jnp.asarray(wbias), jnp.asarray(onehot, BF16)


def _nsa(qu, qr, kc, vc, ks, vs, kw, vw, ng, consts, batch, seq):
    ov, wbias, onehot = consts
    n_qb = seq // Q_BLOCK
    tokens = batch * seq
    qblk = pl.BlockSpec((NSA_WIDTH, Q_BLOCK), lambda b, q: (0, b * n_qb + q))
    cblk = pl.BlockSpec((1, seq // CMP_STRIDE, NSA_KV_WIDTH), lambda b, q: (b, 0, 0))
    kblk = pl.BlockSpec((seq, NSA_KV_WIDTH), lambda b, q: (b, 0))
    vblk = pl.BlockSpec((NSA_KV_HEADS * V_ROWS, seq), lambda b, q: (0, b))
    full = lambda a: pl.BlockSpec(a.shape, lambda b, q: (0,) * a.ndim)
    return pl.pallas_call(
        _nsa_kernel,
        grid=(batch, n_qb),
        in_specs=[qblk, qblk, cblk, cblk, kblk, vblk, kblk, vblk,
                  pl.BlockSpec((LANES, Q_BLOCK), lambda b, q: (0, b * n_qb + q)), full(ov), full(wbias),
                  full(onehot)],
        out_specs=pl.BlockSpec((Q_BLOCK, NSA_WIDTH), lambda b, q: (b * n_qb + q, 0)),
        out_shape=jax.ShapeDtypeStruct((tokens, NSA_WIDTH), BF16),
        compiler_params=_params("arbitrary", "arbitrary"),
        name="nsa_attention",
    )(qu, qr, kc, vc, ks, vs, kw, vw, ng, ov, wbias, onehot)


def _outproj_kernel(yr_ref, yc_ref, yn_ref, x_ref, mod_ref, w_ref, lng_ref, lnb_ref, rwh_ref, rwl_ref, rb_ref,
                    x1_ref, h2a_ref, h2b_ref, ti_ref, tw_ref, cnt_ref, *, alpha):
    tm = x_ref.shape[0]
    mix = (_dot(yr_ref[...], w_ref[0, 0:RET_WIDTH, :])
           + _dot(yc_ref[...], w_ref[0, RET_WIDTH:RET_WIDTH + CONV_WIDTH, :])
           + _dot(yn_ref[...], w_ref[0, RET_WIDTH + CONV_WIDTH:, :]))
    x1 = _layer_norm(alpha * x_ref[...] + (1.0 + mod_ref[2, 0]) * mix) * lng_ref[0, 0] + lnb_ref[0, 0]
    x1_ref[...] = x1
    h2 = _layer_norm(x1) * (1.0 + mod_ref[4, 0]) + mod_ref[3, 0]
    for j, part_ref in enumerate((h2a_ref, h2b_ref)):
        part_ref[...] = _pack_pairs(h2[:, j * 2 * SPLIT_WIDTH:(j + 1) * 2 * SPLIT_WIDTH])
    h_hi = h2.astype(BF16)
    h_lo = (h2 - h_hi.astype(F32)).astype(BF16)
    logits = _dot(h_hi, rwh_ref[0]) + _dot(h_lo, rwh_ref[0]) + _dot(h_hi, rwl_ref[0]) + rb_ref[0]

    lane = lax.broadcasted_iota(jnp.int32, (tm, LANES), 1)
    lanef = lane.astype(F32)
    rest = jnp.where(lane < N_EXPERTS, logits, -jnp.inf)
    vals, idxs = [], []
    for _ in range(TOP_K):
        top = jnp.max(rest, axis=-1, keepdims=True)
        idx = jnp.min(jnp.where(rest == top, lanef, float(LANES)), axis=-1, keepdims=True)
        vals.append(top)
        idxs.append(idx)
        rest = jnp.where(lanef == idx, -jnp.inf, rest)
    exps = [jnp.exp(v - vals[0]) for v in vals]
    inv = 1.0 / functools.reduce(lambda a, b: a + b, exps)
    top_w = jnp.zeros((tm, LANES), F32)
    top_i = jnp.zeros((tm, LANES), F32)
    member = jnp.zeros((tm, LANES), F32)
    for k in range(TOP_K):
        top_w = jnp.where(lane == k, exps[k] * inv, top_w)
        top_i = jnp.where(lane == k, idxs[k], top_i)
        member = member + jnp.where(lanef == idxs[k], 1.0, 0.0)
    tw_ref[...] = top_w
    ti_ref[...] = top_i.astype(jnp.int32)

    @pl.when(pl.program_id(0) == 0)
    def _():
        cnt_ref[...] = jnp.zeros_like(cnt_ref)

    cnt_ref[...] += jnp.broadcast_to(jnp.sum(member, axis=0, keepdims=True), cnt_ref.shape)


def _out_projection(y_ret, y_conv, y_nsa, x, mod_l, w_out, ln_g, ln_b, rw_hi, rw_lo, rb, layer, seq, alpha):
    tokens = x.shape[0]
    tm = ROW_TILE
    tiles_per_seq = seq // tm
    row = lambda w: pl.BlockSpec((tm, w), lambda i: (i, 0))
    lay3 = lambda a: pl.BlockSpec((1,) + a.shape[1:], lambda i: (layer,) + (0,) * (a.ndim - 1))
    return pl.pallas_call(
        functools.partial(_outproj_kernel, alpha=alpha),
        grid=(tokens // tm,),
        in_specs=[row(RET_WIDTH), row(CONV_WIDTH), row(NSA_WIDTH), row(D_MODEL),
                  pl.BlockSpec((6, 1, 1, D_MODEL), lambda i: (0, i // tiles_per_seq, 0, 0)),
                  lay3(w_out),
                  pl.BlockSpec((1, 1, 1, D_MODEL), lambda i: (layer, 0, 0, 0)),
                  pl.BlockSpec((1, 1, 1, D_MODEL), lambda i: (layer, 0, 0, 0)),
                  lay3(rw_hi), lay3(rw_lo), lay3(rb)],
        out_specs=[row(D_MODEL)] + [row(SPLIT_WIDTH)] * SC_SPLIT + [row(LANES), row(LANES),
                   pl.BlockSpec((8, LANES), lambda i: (0, 0))],
        out_shape=[jax.ShapeDtypeStruct((tokens, D_MODEL), F32)]
                  + [jax.ShapeDtypeStruct((tokens, SPLIT_WIDTH), jnp.uint32)] * SC_SPLIT
                  + [jax.ShapeDtypeStruct((tokens, LANES), jnp.int32), jax.ShapeDtypeStruct((tokens, LANES), F32),
                   jax.ShapeDtypeStruct((8, LANES), F32)],
        compiler_params=_params("arbitrary"),
        name="outproj_norm_router",
    )(y_ret, y_conv, y_nsa, x, mod_l, w_out, ln_g, ln_b, rw_hi, rw_lo, rb)


def _route_kernel(ti_ref, pstart_ref, tri_ref, pos_ref, carry_ref):
    @pl.when(pl.program_id(0) == 0)
    def _():
        carry_ref[...] = jnp.zeros_like(carry_ref)

    tm = ti_ref.shape[0]
    lane = lax.broadcasted_iota(jnp.int32, (tm, LANES), 1)
    top_i = ti_ref[...]
    onehots = [lane == top_i[:, k:k + 1] for k in range(TOP_K)]
    member = functools.reduce(lambda a, b: a + b, [jnp.where(o, 1.0, 0.0) for o in onehots])
    base = pstart_ref[...] + carry_ref[0:1] + _dot(tri_ref[...], member.astype(BF16))
    pos = jnp.zeros((tm, LANES), F32)
    for k in range(TOP_K):
        pos_k = jnp.sum(jnp.where(onehots[k], base, 0.0), axis=-1, keepdims=True)
        pos = jnp.where(lane == k, pos_k, pos)
    pos_ref[...] = pos.astype(jnp.int32)
    carry_ref[...] += jnp.broadcast_to(jnp.sum(member, axis=0, keepdims=True), carry_ref.shape)


def _route_positions(top_i, pstart):
    tokens = top_i.shape[0]
    tm = ROW_TILE
    r = np.arange(tm)
    tri = jnp.asarray(r[None, :] < r[:, None], BF16)
    return pl.pallas_call(
        _route_kernel,
        grid=(tokens // tm,),
        in_specs=[pl.BlockSpec((tm, LANES), lambda i: (i, 0)),
                  pl.BlockSpec((1, LANES), lambda i: (0, 0)),
                  pl.BlockSpec((tm, tm), lambda i: (0, 0))],
        out_specs=pl.BlockSpec((tm, LANES), lambda i: (i, 0)),
        out_shape=jax.ShapeDtypeStruct((tokens, LANES), jnp.int32),
        scratch_shapes=[pltpu.VMEM((8, LANES), F32)],
        compiler_params=_params("arbitrary"),
        name="route_positions",
    )(top_i, pstart, tri)


def _block_table(counts, n_blocks):
    tm = MOE_TILE
    cnt = counts[0, :N_EXPERTS].astype(jnp.int32)
    nblk = (cnt + tm - 1) // tm
    bend = jnp.cumsum(nblk)
    bstart = bend - nblk
    blocks = jnp.arange(n_blocks, dtype=jnp.int32)
    block_e = jnp.minimum(jnp.sum((bend[None, :] <= blocks[:, None]).astype(jnp.int32), axis=1), N_EXPERTS - 1)
    n_valid = jnp.clip(cnt[block_e] - (blocks - bstart[block_e]) * tm, 0, tm).astype(jnp.int32)
    n_valid = jnp.where(blocks < bend[-1], n_valid, 0)
    pstart = jnp.pad((bstart * tm).astype(F32), (0, LANES - N_EXPERTS))[None, :]
    return pstart, block_e.astype(jnp.int32), bend[-1:].astype(jnp.int32), n_valid


def _sc_mesh():
    return plsc.VectorSubcoreMesh(core_axis_name="core", subcore_axis_name="subcore")


def _scatter_rows(xs, idx_t, n_rows):
    tokens, width = xs[0].shape
    n_idx = idx_t.shape[0]
    n_x = len(xs)
    win = SC_WINDOW

    @functools.partial(pl.kernel, out_type=[jax.ShapeDtypeStruct((n_rows, width), xs[0].dtype)] * n_x,
                       mesh=_sc_mesh(), scratch_types=[], name="dispatch_scatter")
    def scatter(*refs):
        x_hbms, i_hbm, o_hbms = refs[:n_x], refs[n_x], refs[n_x + 1:]
        idx_specs = [pl.BlockSpec((1, win), functools.partial(lambda i, k: (k, i), k=k)) for k in range(n_idx)]
        for x_hbm, o_hbm in zip(x_hbms, o_hbms):
            def body(x_vmem, *i_vmems, o_hbm=o_hbm):
                for i_vmem in i_vmems:
                    pltpu.sync_copy(x_vmem, o_hbm.at[i_vmem.at[0]])

            pltpu.emit_pipeline(
                body,
                grid=(tokens // win,),
                in_specs=[pl.BlockSpec((win, width), lambda i: (i, 0))] + idx_specs,
                out_specs=[],
                core_axis_name=("core", "subcore"),
                dimension_semantics=(pltpu.PARALLEL,),
            )(x_hbm, *([i_hbm] * n_idx))

    return scatter(*xs, idx_t)


def _gather_rows(xs, idx):
    width = xs[0].shape[1]
    n = idx.shape[0]
    n_x = len(xs)
    win = SC_WINDOW

    @functools.partial(pl.kernel, out_type=[jax.ShapeDtypeStruct((n, width), xs[0].dtype)] * n_x,
                       mesh=_sc_mesh(), scratch_types=[], name="combine_gather")
    def gather(*refs):
        x_hbms, i_hbm, o_hbms = refs[:n_x], refs[n_x], refs[n_x + 1:]
        for x_hbm, o_hbm in zip(x_hbms, o_hbms):
            def body(i_vmem, o_vmem, x_hbm=x_hbm):
                pltpu.sync_copy(x_hbm.at[i_vmem.at[0]], o_vmem)

            pltpu.emit_pipeline(
                body,
                grid=(n // win,),
                in_specs=[pl.BlockSpec((1, win), lambda i: (0, i))],
                out_specs=[pl.BlockSpec((win, width), lambda i: (i, 0))],
                core_axis_name=("core", "subcore"),
                dimension_semantics=(pltpu.PARALLEL,),
            )(i_hbm, o_hbm)

    return gather(*xs, idx.reshape(1, n))


def _moe_kernel(be_ref, nb_ref, nv_ref, ra_ref, rb_ref, wgu_ref, bgu_ref, wd_ref, bd_ref,
                oa_ref, ob_ref, wgu_bf, wd_bf):
    row_refs = (ra_ref, rb_ref)
    out_refs = (oa_ref, ob_ref)
    i = pl.program_id(0)
    expert = be_ref[i]
    first = (i == 0) | (expert != be_ref[jnp.maximum(i - 1, 0)])
    valid = i < nb_ref[0]

    @pl.when(valid & first)
    def _():
        wgu_bf[...] = wgu_ref[0, 0].astype(BF16)
        wd_bf[...] = wd_ref[0, 0].astype(BF16)

    @pl.when(valid)
    def _():
        live = lax.broadcasted_iota(jnp.int32, ra_ref.shape, 0) < nv_ref[i]
        gu = bgu_ref[0, 0]
        for j, r_ref in enumerate(row_refs):
            halves = _unpack_pairs(jnp.where(live, r_ref[...], jnp.uint32(0)))
            for k, rows in enumerate(halves):
                w0 = (2 * j + k) * SPLIT_WIDTH
                gu = gu + _dot(rows.astype(BF16), wgu_bf[w0:w0 + SPLIT_WIDTH, :])
        g = jnp.minimum(gu[:, :D_EXPERT], SWIGLU_LIMIT)
        u = jnp.clip(gu[:, D_EXPERT:], -SWIGLU_LIMIT, SWIGLU_LIMIT)
        act = (u + 1.0) * (g * jax.nn.sigmoid(SWIGLU_ALPHA * g))
        y = _dot(act.astype(BF16), wd_bf[...]) + bd_ref[0, 0]
        for j, o_ref in enumerate(out_refs):
            o_ref[...] = _pack_pairs(y[:, j * 2 * SPLIT_WIDTH:(j + 1) * 2 * SPLIT_WIDTH])

    @pl.when(jnp.logical_not(valid))
    def _():
        for o_ref in out_refs:
            o_ref[...] = jnp.zeros_like(o_ref)


def _expert_ffn(rows, block_e, n_used, n_valid, w_gu, b_gu, w_down, b_down, layer):
    n_rows = rows[0].shape[0]
    tm = MOE_TILE
    part = pl.BlockSpec((tm, SPLIT_WIDTH), lambda i, be, nb, nv: (i, 0))
    grid_spec = pltpu.PrefetchScalarGridSpec(
        num_scalar_prefetch=3,
        grid=(n_rows // tm,),
        in_specs=[part] * SC_SPLIT + [
                  pl.BlockSpec((1, 1, D_MODEL, 2 * D_EXPERT), lambda i, be, nb, nv: (layer, be[i], 0, 0)),
                  pl.BlockSpec((1, 1, 1, 2 * D_EXPERT), lambda i, be, nb, nv: (layer, be[i], 0, 0)),
                  pl.BlockSpec((1, 1, D_EXPERT, D_MODEL), lambda i, be, nb, nv: (layer, be[i], 0, 0)),
                  pl.BlockSpec((1, 1, 1, D_MODEL), lambda i, be, nb, nv: (layer, be[i], 0, 0))],
        out_specs=[part] * SC_SPLIT,
        scratch_shapes=[pltpu.VMEM((D_MODEL, 2 * D_EXPERT), BF16), pltpu.VMEM((D_EXPERT, D_MODEL), BF16)])
    return pl.pallas_call(
        _moe_kernel,
        grid_spec=grid_spec,
        out_shape=[jax.ShapeDtypeStruct((n_rows, SPLIT_WIDTH), jnp.uint32)] * SC_SPLIT,
        compiler_params=_params("arbitrary"),
        name="expert_ffn",
    )(block_e, n_used, n_valid, *rows, w_gu, b_gu, w_down, b_down)


def _final_kernel(x_ref, tw_ref, mod_ref, lng_ref, lnb_ref, *rest, alpha):
    y_refs, o_ref = rest[:-1], rest[-1]
    top_w = tw_ref[...]
    parts = []
    for j in range(SC_SPLIT):
        acc = [0.0, 0.0]
        for k in range(TOP_K):
            for half, y in enumerate(_unpack_pairs(y_refs[j * TOP_K + k][...])):
                acc[half] = acc[half] + top_w[:, k:k + 1] * y
        parts += acc
    ffn = jnp.concatenate(parts, axis=1)
    o_ref[...] = (_layer_norm(alpha * x_ref[...] + (1.0 + mod_ref[5, 0]) * ffn) * lng_ref[0, 0] + lnb_ref[0, 0])


def _final_norm(x1, y_parts, top_w, mod_l, ln_g, ln_b, layer, seq, alpha):
    tokens = x1.shape[0]
    tm = ROW_TILE
    tiles = tokens // tm
    tiles_per_seq = seq // tm
    row = lambda w: pl.BlockSpec((tm, w), lambda i: (i, 0))
    vec = pl.BlockSpec((1, 1, 1, D_MODEL), lambda i: (layer, 1, 0, 0))
    y_specs, y_args = [], []
    for j in range(SC_SPLIT):
        for k in range(TOP_K):
            y_specs.append(pl.BlockSpec((tm, SPLIT_WIDTH), functools.partial(lambda i, k: (k * tiles + i, 0), k=k)))
            y_args.append(y_parts[j])
    return pl.pallas_call(
        functools.partial(_final_kernel, alpha=alpha),
        grid=(tiles,),
        in_specs=[row(D_MODEL), row(LANES),
                  pl.BlockSpec((6, 1, 1, D_MODEL), lambda i: (0, i // tiles_per_seq, 0, 0)), vec, vec] + y_specs,
        out_specs=row(D_MODEL),
        out_shape=jax.ShapeDtypeStruct(x1.shape, F32),
        compiler_params=_params("arbitrary"),
        name="combine_final_norm",
    )(x1, top_w, mod_l, ln_g, ln_b, *y_args)


def kernel(x, c, positions, w_in, w_out, ret_gn_w, conv_w, cmp_pos, cmp_w1, cmp_w2, ada_w, ada_b, ln_g, ln_b,
           router_w, router_b, w_gate_up, b_gate_up, w_down, b_down):
    batch, seq, _ = x.shape
    depth = w_in.shape[0]
    tokens = batch * seq
    n_rows = tokens * TOP_K + N_EXPERTS * MOE_TILE
    alpha = float((2 * depth) ** 0.25)

    w_in_r = jnp.pad(w_in, ((0, 0), (0, 0), (0, N_COLS - w_in.shape[2]))).astype(BF16)
    w_out_b = w_out.astype(BF16)
    rw = jnp.pad(router_w, ((0, 0), (0, 0), (0, LANES - N_EXPERTS)))
    rw_hi = rw.astype(BF16)
    rw_lo = (rw - rw_hi.astype(F32)).astype(BF16)
    rb = jnp.pad(router_b, ((0, 0), (0, LANES - N_EXPERTS))).reshape(depth, 1, LANES)
    cw = _compress_weights(cmp_pos, cmp_w1, cmp_w2)
    ret_consts = _retention_consts()
    nsa_consts = _nsa_consts(seq)
    gn_w = ret_gn_w.reshape(depth, 1, RET_WIDTH)
    ln_g4 = ln_g.reshape(depth, 2, 1, D_MODEL)
    ln_b4 = ln_b.reshape(depth, 2, 1, D_MODEL)
    b_gu4 = b_gate_up.reshape(depth, N_EXPERTS, 1, 2 * D_EXPERT)
    b_d4 = b_down.reshape(depth, N_EXPERTS, 1, D_MODEL)

    mod = _modulation(c, ada_w, ada_b)
    cos, sin = _rope_tables(positions)
    xt = x.reshape(tokens, D_MODEL)
    for l in range(depth):
        (rq, rk, rv, rg, y_conv, qu, qr, kcx, vcx, ks, vs, kw, vw, ng) = _input_projection(
            xt, mod[l], w_in_r, cos, sin, conv_w, l, seq)
        y_ret = _retention(rq, rk, rv, rg, ret_consts, gn_w, l, batch, seq)
        kc, vc = _compress(kcx, vcx, cw, l, batch, seq)
        y_nsa = _nsa(qu, qr, kc, vc, ks, vs, kw, vw, ng, nsa_consts, batch, seq)
        x1, *h2, top_i, top_w, counts = _out_projection(y_ret, y_conv, y_nsa, xt, mod[l], w_out_b, ln_g4, ln_b4,
                                                        rw_hi, rw_lo, rb, l, seq, alpha)
        pstart, block_e, n_used, n_valid = _block_table(counts, n_rows // MOE_TILE)
        pos_t = _route_positions(top_i, pstart)[:, :TOP_K].T
        rows = _scatter_rows(h2, pos_t, n_rows)
        y = _expert_ffn(rows, block_e, n_used, n_valid, w_gate_up, b_gu4, w_down, b_d4, l)
        y_tok = _gather_rows(y, pos_t.reshape(-1))
        xt = _final_norm(x1, y_tok, top_w, mod[l], ln_g4, ln_b4, l, seq, alpha)
    return xt.reshape(batch, seq, D_MODEL)
```

```python
import functools

import numpy as np
import jax
import jax.numpy as jnp
from jax import lax
from jax.experimental import pallas as pl
from jax.experimental.pallas import tpu as pltpu
from jax.experimental.pallas import tpu_sc as plsc

F32 = jnp.float32
BF16 = jnp.bfloat16

D_MODEL = 1024
HEAD_DIM = 64
RET_WIDTH = 256
RET_HEADS = 4
RET_CHUNK = 128
CONV_WIDTH = 256
CONV_K = 3
NSA_WIDTH = 512
NSA_HEADS = 8
NSA_KV_HEADS = 2
NSA_GROUP = 4
NSA_KV_WIDTH = 128
CMP_LEN = 32
CMP_STRIDE = 16
CMP_HIDDEN = 128
SEL_LEN = 64
SEL_TOP = 8
WINDOW = 512
Q_BLOCK = 128
Q_TILE = 2 * Q_BLOCK
ROPE_THETA = 10000.0
N_EXPERTS = 32
TOP_K = 4
D_EXPERT = 1024
SWIGLU_LIMIT = 7.0
SWIGLU_ALPHA = 1.702
LN_EPS = 1e-5
NEG_INF = -1e30
FORCE_SCORE = 1e9

LANES = 128
VMEM_LIMIT = 56 * 1024 * 1024

OFF_RQ, OFF_RK, OFF_RV, OFF_RG = 0, 256, 512, 768
OFF_CB, OFF_CC, OFF_CH = 1024, 1280, 1536
OFF_NQ = 1792
OFF_KC, OFF_VC = 2304, 2432
OFF_KS, OFF_VS, OFF_KW, OFF_VW = 2560, 2688, 2816, 2944
OFF_NG = 3072
N_GATE = NSA_HEADS * 3
N_COLS = OFF_NG + LANES
LOG2E = 1.4426950408889634

RET_BATCH = 4
ROW_TILE = 256
MOE_TILE = 256
SEL_GROUP = 4
V_ROWS = HEAD_DIM + 16
SC_WINDOW = LANES
SC_SPLIT = 2
SPLIT_WIDTH = D_MODEL // (2 * SC_SPLIT)


def _dot(a, b):
    return jnp.dot(a, b, preferred_element_type=F32)


def _dot_nt(a, b):
    return lax.dot_general(a, b, (((1,), (1,)), ((), ())), preferred_element_type=F32)


def _layer_norm(x):
    mu = jnp.mean(x, axis=-1, keepdims=True)
    xc = x - mu
    var = jnp.mean(xc * xc, axis=-1, keepdims=True)
    return xc * lax.rsqrt(var + LN_EPS)


def _pack_pairs(x):
    bits = lambda t: lax.bitcast_convert_type(t.astype(BF16).astype(F32), jnp.uint32)
    return (bits(x[:, SPLIT_WIDTH:]) & jnp.uint32(0xFFFF0000)) | (bits(x[:, :SPLIT_WIDTH]) >> 16)


def _unpack_pairs(p):
    low = lax.bitcast_convert_type(p << 16, F32)
    high = lax.bitcast_convert_type(p & jnp.uint32(0xFFFF0000), F32)
    return low, high


def _params(*sem):
    return pltpu.CompilerParams(dimension_semantics=sem, vmem_limit_bytes=VMEM_LIMIT)


def _mod_kernel(c_ref, w_ref, b_ref, o_ref):
    c = c_ref[...]
    ca = (c * jax.nn.sigmoid(c)).astype(BF16)
    o_ref[0, 0] = _dot(ca, w_ref[0].astype(BF16)) + b_ref[0]


def _modulation(c, ada_w, ada_b):
    depth = ada_w.shape[0]
    batch = c.shape[0]
    out = pl.pallas_call(
        _mod_kernel,
        grid=(depth, 6),
        in_specs=[pl.BlockSpec((batch, D_MODEL), lambda l, j: (0, 0)),
                  pl.BlockSpec((1, D_MODEL, D_MODEL), lambda l, j: (l, 0, j)),
                  pl.BlockSpec((1, 1, D_MODEL), lambda l, j: (l * 6 + j, 0, 0))],
        out_specs=pl.BlockSpec((1, 1, batch, D_MODEL), lambda l, j: (l, j, 0, 0)),
        out_shape=jax.ShapeDtypeStruct((depth, 6, batch, D_MODEL), F32),
        compiler_params=_params("arbitrary", "arbitrary"),
        name="adaln_mod",
    )(c, ada_w, ada_b.reshape(depth * 6, 1, D_MODEL))
    return out.reshape(depth, 6, batch, 1, D_MODEL)


def _rope_table_kernel(pos_ref, inv_ref, cos_ref, sin_ref):
    ang = pos_ref[...] * inv_ref[...]
    cos_ref[...] = jnp.cos(ang)
    sin_ref[...] = jnp.sin(ang)


def _rope_tables(positions):
    half = HEAD_DIM // 2
    per_row = LANES // half
    tokens = positions.size
    pos4 = jnp.repeat(positions.reshape(tokens // per_row, per_row).astype(F32), half, axis=1)
    inv = ROPE_THETA ** (-jnp.arange(half, dtype=F32) / half)
    inv4 = jnp.tile(inv, per_row)[None, :]
    rows = tokens // per_row
    tile = min(rows, 1024)
    cos4, sin4 = pl.pallas_call(
        _rope_table_kernel,
        grid=(rows // tile,),
        in_specs=[pl.BlockSpec((tile, LANES), lambda i: (i, 0)),
                  pl.BlockSpec((1, LANES), lambda i: (0, 0))],
        out_specs=[pl.BlockSpec((tile, LANES), lambda i: (i, 0))] * 2,
        out_shape=[jax.ShapeDtypeStruct((rows, LANES), F32)] * 2,
        compiler_params=_params("arbitrary"),
        name="rope_tables",
    )(pos4, inv4)
    cos = jnp.tile(cos4.reshape(tokens, half), (1, per_row))
    sign = jnp.tile(jnp.concatenate([-jnp.ones((half,), F32), jnp.ones((half,), F32)]), LANES // HEAD_DIM)
    sin = jnp.tile(sin4.reshape(tokens, half), (1, per_row)) * sign[None, :]
    return cos, sin


def _inproj_kernel(x_ref, mod_ref, w_ref, cos_ref, sin_ref, convw_ref,
                   rq_ref, rk_ref, rv_ref, rg_ref, yc_ref, qu_ref, qr_ref, kc_ref, vc_ref,
                   ks_ref, vs_ref, kw_ref, vw_ref, ng_ref, carry_ref, *, tiles_per_seq):
    i = pl.program_id(0)
    tm = x_ref.shape[0]
    h = (_layer_norm(x_ref[...]) * (1.0 + mod_ref[1, 0]) + mod_ref[0, 0]).astype(BF16)
    cosf = cos_ref[...]
    sinf = sin_ref[...]
    lane = lax.broadcasted_iota(jnp.int32, (tm, LANES), 1)
    first_half = (lane % HEAD_DIM) < (HEAD_DIM // 2)

    def proj(off, width):
        return _dot(h, w_ref[0, :, off:off + width])

    def rope(c):
        cols = []
        for j in range(c.shape[1] // LANES):
            cj = c[:, j * LANES:(j + 1) * LANES]
            swapped = jnp.where(first_half, pltpu.roll(cj, LANES - HEAD_DIM // 2, 1),
                                pltpu.roll(cj, HEAD_DIM // 2, 1))
            cols.append(cj * cosf + swapped * sinf)
        return jnp.concatenate(cols, axis=1) if len(cols) > 1 else cols[0]

    scale = HEAD_DIM ** -0.5
    rq_ref[...] = rope(proj(OFF_RQ, RET_WIDTH)).astype(BF16)
    rk_ref[...] = (rope(proj(OFF_RK, RET_WIDTH)) * scale).astype(BF16)
    rv_ref[...] = proj(OFF_RV, RET_WIDTH).astype(BF16)
    rg = proj(OFF_RG, RET_WIDTH)
    rg_ref[...] = (rg * jax.nn.sigmoid(rg)).astype(BF16)

    cb = proj(OFF_CB, CONV_WIDTH)
    u = proj(OFF_CC, CONV_WIDTH) * proj(OFF_CH, CONV_WIDTH)

    @pl.when(i % tiles_per_seq == 0)
    def _():
        carry_ref[...] = jnp.zeros_like(carry_ref)

    carry = carry_ref[...]
    row = lax.broadcasted_iota(jnp.int32, (tm, CONV_WIDTH), 0)
    prev1 = jnp.where(row == 0, carry[7:8], pltpu.roll(u, 1, 0))
    prev2 = jnp.where(row == 0, carry[6:7], jnp.where(row == 1, carry[7:8], pltpu.roll(u, 2, 0)))
    cw = convw_ref[0]
    yc_ref[...] = (cb * (cw[0:1] * prev2 + cw[1:2] * prev1 + cw[2:3] * u)).astype(BF16)
    carry_ref[...] = u[tm - 8:tm]

    nq = proj(OFF_NQ, NSA_WIDTH) * (scale * LOG2E)
    qu_ref[...] = nq.T.astype(BF16)
    qr_ref[...] = rope(nq).T.astype(BF16)
    kc_ref[...] = proj(OFF_KC, NSA_KV_WIDTH)
    vc_ref[...] = proj(OFF_VC, NSA_KV_WIDTH)
    def values_t(off):
        v_t = proj(off, NSA_KV_WIDTH).T
        ones = jnp.ones((V_ROWS - HEAD_DIM, tm), F32)
        parts = []
        for hd in range(NSA_KV_HEADS):
            parts += [v_t[hd * HEAD_DIM:(hd + 1) * HEAD_DIM], ones]
        return jnp.concatenate(parts, axis=0).astype(BF16)

    ks_ref[...] = rope(proj(OFF_KS, NSA_KV_WIDTH)).astype(BF16)
    vs_ref[...] = values_t(OFF_VS)
    kw_ref[...] = rope(proj(OFF_KW, NSA_KV_WIDTH)).astype(BF16)
    vw_ref[...] = values_t(OFF_VW)
    ng_ref[...] = proj(OFF_NG, LANES).T


def _input_projection(x, mod_l, w_l, cos, sin, conv_w, layer, seq):
    tokens = x.shape[0]
    tm = ROW_TILE
    tiles_per_seq = seq // tm
    row = lambda w: pl.BlockSpec((tm, w), lambda i: (i, 0))
    col = lambda w: pl.BlockSpec((w, tm), lambda i: (0, i))
    outs = [(RET_WIDTH, BF16, False)] * 4 + [(CONV_WIDTH, BF16, False), (NSA_WIDTH, BF16, True),
                                            (NSA_WIDTH, BF16, True), (NSA_KV_WIDTH, F32, False),
                                            (NSA_KV_WIDTH, F32, False), (NSA_KV_WIDTH, BF16, False),
                                            (NSA_KV_HEADS * V_ROWS, BF16, True), (NSA_KV_WIDTH, BF16, False),
                                            (NSA_KV_HEADS * V_ROWS, BF16, True), (LANES, F32, True)]
    return pl.pallas_call(
        functools.partial(_inproj_kernel, tiles_per_seq=tiles_per_seq),
        grid=(tokens // tm,),
        in_specs=[row(D_MODEL),
                  pl.BlockSpec((6, 1, 1, D_MODEL), lambda i: (0, i // tiles_per_seq, 0, 0)),
                  pl.BlockSpec((1, D_MODEL, N_COLS), lambda i: (layer, 0, 0)),
                  row(LANES), row(LANES),
                  pl.BlockSpec((1, CONV_K, CONV_WIDTH), lambda i: (layer, 0, 0))],
        out_specs=[col(w) if t else row(w) for w, _, t in outs],
        out_shape=[jax.ShapeDtypeStruct((w, tokens) if t else (tokens, w), dt) for w, dt, t in outs],
        scratch_shapes=[pltpu.VMEM((8, CONV_WIDTH), F32)],
        compiler_params=_params("arbitrary"),
        name="ln_inproj",
    )(x, mod_l, w_l, cos, sin, conv_w)


def _retention_kernel(q_ref, k_ref, v_ref, g_ref, intra_ref, qdec_ref, kdec_ref, cdec_ref, gn_ref,
                      o_ref, state_ref):
    @pl.when(pl.program_id(1) == 0)
    def _():
        state_ref[...] = jnp.zeros_like(state_ref)

    n_pairs = RET_WIDTH // LANES
    lane = lax.broadcasted_iota(jnp.int32, (RET_CHUNK, LANES), 1)
    low = lane < HEAD_DIM
    blk_r = lax.broadcasted_iota(jnp.int32, (LANES, LANES), 0) < HEAD_DIM
    blk_c = lax.broadcasted_iota(jnp.int32, (LANES, LANES), 1) < HEAD_DIM
    same_head = blk_r == blk_c
    states = {(b, p): state_ref[b, p] for b in range(q_ref.shape[0]) for p in range(n_pairs)}
    for ch, b in [(ch, b) for ch in range(q_ref.shape[1] // RET_CHUNK) for b in range(q_ref.shape[0])]:
        rows = slice(ch * RET_CHUNK, (ch + 1) * RET_CHUNK)
        outs = []
        for p in range(n_pairs):
            cols = slice(p * LANES, (p + 1) * LANES)
            qp = q_ref[b, rows, cols]
            kp = k_ref[b, rows, cols]
            vp = v_ref[b, rows, cols]
            out = jnp.zeros((RET_CHUNK, LANES), F32)
            for hh in range(2):
                keep = low if hh == 0 else jnp.logical_not(low)
                qm = jnp.where(keep, qp, jnp.zeros_like(qp))
                s = _dot_nt(qm, kp) * intra_ref[2 * p + hh]
                out = jnp.where(keep, _dot(s.astype(BF16), vp), out)
            state = states[b, p]
            qd = (qp.astype(F32) * qdec_ref[:, cols]).astype(BF16)
            out = out + _dot(qd, state.astype(BF16))
            kd_t = (kp.astype(F32) * kdec_ref[:, cols]).T.astype(BF16)
            upd = _dot(kd_t, vp)
            states[b, p] = state * cdec_ref[p] + jnp.where(same_head, upd, 0.0)
            def half_mean(t):
                lo_sum = jnp.sum(jnp.where(low, t, 0.0), axis=-1, keepdims=True)
                hi_sum = jnp.sum(jnp.where(low, 0.0, t), axis=-1, keepdims=True)
                return jnp.where(low, lo_sum, hi_sum) * (1.0 / HEAD_DIM)
            oc = out - half_mean(out)
            outs.append(oc * lax.rsqrt(half_mean(oc * oc) + LN_EPS))
        normed = jnp.concatenate(outs, axis=1)
        o_ref[b, rows, :] = (normed * gn_ref[0] * g_ref[b, rows, :].astype(F32)).astype(BF16)
    for (b, p), state in states.items():
        state_ref[b, p] = state


def _retention_consts():
    heads = jnp.arange(RET_HEADS, dtype=F32)
    log_gamma = jnp.log(1.0 - jnp.power(2.0, -5.0 - heads))
    i = jnp.arange(RET_CHUNK, dtype=F32)
    diff = i[:, None] - i[None, :]
    intra = jnp.where(diff >= 0, jnp.exp(diff * log_gamma[:, None, None]), 0.0)
    q_dec = jnp.exp((i + 1.0) * log_gamma[:, None])
    k_dec = jnp.exp((RET_CHUNK - 1.0 - i) * log_gamma[:, None])
    c_dec = jnp.exp(RET_CHUNK * log_gamma)
    expand = lambda t: jnp.repeat(t.T, HEAD_DIM, axis=1)
    c_rows = jnp.repeat(c_dec, HEAD_DIM).reshape(RET_WIDTH // LANES, LANES, 1)
    c_blk = jnp.broadcast_to(c_rows, (RET_WIDTH // LANES, LANES, LANES))
    return intra, expand(q_dec), expand(k_dec), c_blk


def _retention(rq, rk, rv, rg, consts, gn_w, layer, batch, seq):
    intra, q_dec, k_dec, c_blk = consts
    rows = 2 * RET_CHUNK
    steps = seq // rows
    group = RET_BATCH if batch % RET_BATCH == 0 else 1
    blk = pl.BlockSpec((group, rows, RET_WIDTH), lambda b, s: (b, s, 0))
    full = lambda a: pl.BlockSpec(a.shape, lambda b, s: (0,) * a.ndim)
    per_seq = lambda t: t.reshape(batch, seq, RET_WIDTH)
    out = pl.pallas_call(
        _retention_kernel,
        grid=(batch // group, steps),
        in_specs=[blk, blk, blk, blk, full(intra), full(q_dec), full(k_dec), full(c_blk),
                  pl.BlockSpec((1, 1, RET_WIDTH), lambda b, s: (layer, 0, 0))],
        out_specs=blk,
        out_shape=jax.ShapeDtypeStruct((batch, seq, RET_WIDTH), BF16),
        scratch_shapes=[pltpu.VMEM((group, RET_WIDTH // LANES, LANES, LANES), F32)],
        compiler_params=_params("arbitrary", "arbitrary"),
        name="retention",
    )(per_seq(rq), per_seq(rk), per_seq(rv), per_seq(rg), intra, q_dec, k_dec, c_blk, gn_w)
    return out.reshape(batch * seq, RET_WIDTH)


def _compress_kernel(xk_ref, xv_ref, posa_ref, posb_ref, w1a_ref, w1b_ref, w2_ref, kc_ref, vc_ref):
    n_grp = xk_ref.shape[0] // CMP_STRIDE
    for kv, (x_ref, o_ref) in enumerate(((xk_ref, kc_ref), (xv_ref, vc_ref))):
        ya = jnp.zeros((n_grp, NSA_KV_HEADS * CMP_HIDDEN), F32)
        yb = jnp.zeros((n_grp, NSA_KV_HEADS * CMP_HIDDEN), F32)
        for r in range(CMP_STRIDE):
            x = x_ref[pl.ds(r, n_grp, stride=CMP_STRIDE), :]
            cols = slice(r * NSA_KV_WIDTH, (r + 1) * NSA_KV_WIDTH)
            ya = ya + _dot((x + posa_ref[0, kv, :, cols]).astype(BF16), w1a_ref[0, kv, cols, :])
            yb = yb + _dot((x + posb_ref[0, kv, :, cols]).astype(BF16), w1b_ref[0, kv, cols, :])
        hidden = ya + pltpu.roll(yb, n_grp - 1, 0)
        act = jax.nn.gelu(hidden)
        out = _dot(act.astype(BF16), w2_ref[0, kv])
        o_ref[0] = (out if kv == 0 else out.T).astype(BF16)


def _compress_weights(cmp_pos, cmp_w1, cmp_w2):
    depth = cmp_w1.shape[0]
    eye = jnp.eye(NSA_KV_HEADS, dtype=F32)
    a = cmp_w1.reshape(depth, 2, 2, CMP_STRIDE, HEAD_DIM, CMP_HIDDEN)
    w1 = jnp.einsum('lkardj,hg->lkarhdgj', a, eye)
    w1 = w1.reshape(depth, 2, 2, CMP_STRIDE * NSA_KV_WIDTH, NSA_KV_HEADS * CMP_HIDDEN).astype(BF16)
    pos = cmp_pos.reshape(depth, 2, 2, CMP_STRIDE, 1, HEAD_DIM)
    pos = jnp.broadcast_to(pos, (depth, 2, 2, CMP_STRIDE, NSA_KV_HEADS, HEAD_DIM))
    pos = pos.reshape(depth, 2, 2, 1, CMP_STRIDE * NSA_KV_WIDTH)
    w2 = jnp.einsum('lkje,hg->lkhjge', cmp_w2, eye)
    w2 = w2.reshape(depth, 2, NSA_KV_HEADS * CMP_HIDDEN, NSA_KV_WIDTH).astype(BF16)
    return w1[:, :, 0], w1[:, :, 1], pos[:, :, 0], pos[:, :, 1], w2


def _compress(kcx, vcx, cw, layer, batch, seq):
    w1a, w1b, posa, posb, w2 = cw
    n_grp = seq // CMP_STRIDE
    xblk = pl.BlockSpec((seq, NSA_KV_WIDTH), lambda b: (b, 0))
    lay = lambda a: pl.BlockSpec((1,) + a.shape[1:], lambda b: (layer,) + (0,) * (a.ndim - 1))
    oblk = pl.BlockSpec((1, n_grp, NSA_KV_WIDTH), lambda b: (b, 0, 0))
    return pl.pallas_call(
        _compress_kernel,
        grid=(batch,),
        in_specs=[xblk, xblk, lay(posa), lay(posb), lay(w1a), lay(w1b), lay(w2)],
        out_specs=[oblk, oblk],
        out_shape=[jax.ShapeDtypeStruct((batch, n_grp, NSA_KV_WIDTH), BF16)] * 2,
        compiler_params=_params("arbitrary"),
        name="nsa_compress",
    )(kcx, vcx, posa, posb, w1a, w1b, w2)


def _nsa_kernel(qu_ref, qr_ref, kc_ref, vc_ref, ks_ref, vs_ref, kw_ref, vw_ref, ng_ref, ov_ref, wbias_ref, ind_ref,
                o_ref):
    qb = pl.program_id(1)
    q0 = qb * Q_TILE
    qb_last = (qb + 1) * (Q_TILE // Q_BLOCK) - 1
    n_sel = ov_ref.shape[0]
    cols4 = NSA_GROUP * Q_TILE
    n_past = WINDOW // Q_BLOCK
    gsig = jax.nn.sigmoid(ng_ref[...])

    def padded_q(ref, h):
        zeros = jnp.zeros((HEAD_DIM, Q_TILE), BF16)
        parts = []
        for g in range(NSA_GROUP):
            hq = h * NSA_GROUP + g
            x = ref[hq * HEAD_DIM:(hq + 1) * HEAD_DIM, :]
            parts.append(jnp.concatenate([x, zeros] if h == 0 else [zeros, x], axis=0))
        return jnp.concatenate(parts, axis=1)

    def tile4(t):
        return jnp.concatenate([t] * NSA_GROUP, axis=1)

    def attend(state, keys, values_t, bias, q_t):
        m, acc = state
        s = _dot(keys, q_t)
        if bias is not None:
            s = s + bias
        m_new = jnp.maximum(m, jnp.max(s, axis=0, keepdims=True))
        e = jnp.exp2(s - m_new)
        acc = jnp.exp2(m - m_new) * acc + _dot(values_t, e.astype(BF16))
        return m_new, acc

    def normalised(state):
        acc = state[1]
        return acc[0:HEAD_DIM] * (1.0 / acc[HEAD_DIM:HEAD_DIM + 1])

    init = (jnp.full((1, cols4), NEG_INF, F32), jnp.zeros((V_ROWS, cols4), F32))
    head_rows = [slice(h * V_ROWS, (h + 1) * V_ROWS) for h in range(NSA_KV_HEADS)]
    q_rot = [padded_q(qr_ref, h) for h in range(NSA_KV_HEADS)]

    def position_bias(kb, windowed):
        first = qb * (Q_TILE // Q_BLOCK)
        return tile4(jnp.concatenate(
            [wbias_ref[jnp.clip(kb - (first + sub) + n_past + 1, 0 if windowed else 2, n_past + 2)]
             for sub in range(Q_TILE // Q_BLOCK)], axis=1))

    o_cmp = []
    q_sel = []
    for h in range(NSA_KV_HEADS):
        hr = slice(h * HEAD_DIM, (h + 1) * HEAD_DIM)
        qus = padded_q(qu_ref, h)

        s = _dot(kc_ref[0], qus)
        cid = lax.broadcasted_iota(jnp.int32, (LANES, Q_TILE), 0)
        tq = q0 + lax.broadcasted_iota(jnp.int32, (LANES, Q_TILE), 1)
        cvalid = tile4(jnp.where(cid * CMP_STRIDE + (CMP_LEN - 1) <= tq, 1.0, 0.0))
        sm = jnp.where(cvalid > 0.5, s, NEG_INF)
        e = jnp.exp2(sm - jnp.max(sm, axis=0, keepdims=True)) * cvalid
        l = jnp.sum(e, axis=0, keepdims=True)
        p = e * (1.0 / jnp.where(l > 0.0, l, 1.0))
        o_cmp.append(_dot(vc_ref[0, hr, :], p.astype(BF16)))

        psum = p[:, 0:Q_TILE]
        for g in range(1, NSA_GROUP):
            psum = psum + p[:, g * Q_TILE:(g + 1) * Q_TILE]
        p_hi = psum.astype(BF16)
        rem = psum - p_hi.astype(F32)
        p_mid = rem.astype(BF16)
        p_lo = (rem - p_mid.astype(F32)).astype(BF16)
        ov = ov_ref[...]
        imp = _dot(ov, p_hi) + _dot(ov, p_mid) + _dot(ov, p_lo)
        jid = lax.broadcasted_iota(jnp.int32, (n_sel, Q_TILE), 0)
        tid = q0 + lax.broadcasted_iota(jnp.int32, (n_sel, Q_TILE), 1)
        forced = (jid == 0) | (jid == jnp.right_shift(tid, 6))
        imp = jnp.where(forced, FORCE_SCORE, jnp.where(jid * SEL_LEN <= tid, imp, -FORCE_SCORE))
        rank = jnp.zeros((n_sel, Q_TILE), F32)
        for i in range(n_sel):
            ri = imp[i:i + 1, :]
            beats = (ri > imp) | ((ri == imp) & (jid > i))
            rank = rank + jnp.where(beats, 1.0, 0.0)
        sel_bias = tile4(jnp.where(rank < float(SEL_TOP), 0.0, NEG_INF)).astype(BF16)
        pad = jnp.zeros((LANES - n_sel, cols4), BF16)
        q_sel.append(jnp.concatenate([q_rot[h], sel_bias, pad], axis=0))

    def sel_step(gi, states, last):
        k0 = pl.multiple_of(gi * (SEL_GROUP * Q_BLOCK), SEL_GROUP * Q_BLOCK)
        keys = jnp.concatenate([ks_ref[pl.ds(k0, SEL_GROUP * Q_BLOCK), :], ind_ref[gi]], axis=1)
        bias = None
        if last:
            bias = jnp.concatenate([position_bias(gi * SEL_GROUP + u, False) for u in range(SEL_GROUP)], axis=0)
        return tuple(attend(states[h], keys, vs_ref[head_rows[h], pl.ds(k0, SEL_GROUP * Q_BLOCK)], bias, q_sel[h])
                     for h in range(NSA_KV_HEADS))

    last_group = qb_last // SEL_GROUP
    sel_states = lax.fori_loop(0, last_group, functools.partial(sel_step, last=False), (init,) * NSA_KV_HEADS)
    sel_states = sel_step(last_group, sel_states, True)

    ws = pl.multiple_of(jnp.maximum(q0 - WINDOW, 0), Q_BLOCK)
    n_win = n_past + Q_TILE // Q_BLOCK
    win_bias = jnp.concatenate([position_bias(ws // Q_BLOCK + u, True) for u in range(n_win)], axis=0)
    win_keys = kw_ref[pl.ds(ws, n_win * Q_BLOCK), :]
    heads = []
    for h in range(NSA_KV_HEADS):
        o_s = normalised(sel_states[h])
        o_w = normalised(attend(init, win_keys, vw_ref[head_rows[h], pl.ds(ws, n_win * Q_BLOCK)], win_bias, q_rot[h]))
        o_c = o_cmp[h]
        for g in range(NSA_GROUP):
            hq = h * NSA_GROUP + g
            c = slice(g * Q_TILE, (g + 1) * Q_TILE)
            heads.append(gsig[3 * hq:3 * hq + 1, :] * o_c[:, c] + gsig[3 * hq + 1:3 * hq + 2, :] * o_s[:, c]
                         + gsig[3 * hq + 2:3 * hq + 3, :] * o_w[:, c])
    cols_out = [jnp.concatenate(heads[2 * m:2 * m + 2], axis=0).T for m in range(NSA_HEADS // 2)]
    o_ref[...] = jnp.concatenate(cols_out, axis=1).astype(BF16)


def _nsa_consts(seq):
    n_cmp_pad = seq // CMP_STRIDE
    n_sel = seq // SEL_LEN
    c = np.arange(n_cmp_pad)
    j = np.arange(n_sel)
    n_cmp = (seq - CMP_LEN) // CMP_STRIDE + 1
    ov = ((c[None, :] * CMP_STRIDE < j[:, None] * SEL_LEN + SEL_LEN) &
          (j[:, None] * SEL_LEN <= c[None, :] * CMP_STRIDE + CMP_LEN - 1) & (c[None, :] < n_cmp))
    kl = np.arange(Q_BLOCK)[:, None]
    tl = np.arange(Q_BLOCK)[None, :]
    n_past = WINDOW // Q_BLOCK
    none = np.zeros((Q_BLOCK, Q_BLOCK), bool)
    valid = [none, kl > tl] + [np.ones((Q_BLOCK, Q_BLOCK), bool)] * (n_past - 1) + [kl <= tl, none]
    wbias = np.where(np.stack(valid), 0.0, NEG_INF).astype(np.float32)
    key_block = np.arange(seq) // SEL_LEN
    onehot = (key_block[:, None] == np.arange(LANES)[None, :]).reshape(-1, SEL_GROUP * Q_BLOCK, LANES)
    return jnp.asarray(ov, BF16), jnp.asarray(wbias), jnp.asarray(onehot, BF16)


def _nsa(qu, qr, kc, vc, ks, vs, kw, vw, ng, consts, batch, seq):
    ov, wbias, onehot = consts
    n_qb = seq // Q_TILE
    tokens = batch * seq
    qblk = pl.BlockSpec((NSA_WIDTH, Q_TILE), lambda b, q: (0, b * n_qb + q))
    cblk = pl.BlockSpec((1, seq // CMP_STRIDE, NSA_KV_WIDTH), lambda b, q: (b, 0, 0))
    kblk = pl.BlockSpec((seq, NSA_KV_WIDTH), lambda b, q: (b, 0))
    vblk = pl.BlockSpec((NSA_KV_HEADS * V_ROWS, seq), lambda b, q: (0, b))
    full = lambda a: pl.BlockSpec(a.shape, lambda b, q: (0,) * a.ndim)
    return pl.pallas_call(
        _nsa_kernel,
        grid=(batch, n_qb),
        in_specs=[qblk, qblk, cblk, cblk, kblk, vblk, kblk, vblk,
                  pl.BlockSpec((LANES, Q_TILE), lambda b, q: (0, b * n_qb + q)), full(ov), full(wbias),
                  full(onehot)],
        out_specs=pl.BlockSpec((Q_TILE, NSA_WIDTH), lambda b, q: (b * n_qb + q, 0)),
        out_shape=jax.ShapeDtypeStruct((tokens, NSA_WIDTH), BF16),
        compiler_params=_params("arbitrary", "arbitrary"),
        name="nsa_attention",
    )(qu, qr, kc, vc, ks, vs, kw, vw, ng, ov, wbias, onehot)


def _outproj_kernel(yr_ref, yc_ref, yn_ref, x_ref, mod_ref, w_ref, lng_ref, lnb_ref, rwh_ref, rwl_ref, rb_ref,
                    x1_ref, h2a_ref, h2b_ref, ti_ref, tw_ref, cnt_ref, *, alpha):
    tm = x_ref.shape[0]
    mix = (_dot(yr_ref[...], w_ref[0, 0:RET_WIDTH, :])
           + _dot(yc_ref[...], w_ref[0, RET_WIDTH:RET_WIDTH + CONV_WIDTH, :])
           + _dot(yn_ref[...], w_ref[0, RET_WIDTH + CONV_WIDTH:, :]))
    x1 = _layer_norm(alpha * x_ref[...] + (1.0 + mod_ref[2, 0]) * mix) * lng_ref[0, 0] + lnb_ref[0, 0]
    x1_ref[...] = x1
    h2 = _layer_norm(x1) * (1.0 + mod_ref[4, 0]) + mod_ref[3, 0]
    for j, part_ref in enumerate((h2a_ref, h2b_ref)):
        part_ref[...] = _pack_pairs(h2[:, j * 2 * SPLIT_WIDTH:(j + 1) * 2 * SPLIT_WIDTH])
    h_hi = h2.astype(BF16)
    h_lo = (h2 - h_hi.astype(F32)).astype(BF16)
    logits = _dot(h_hi, rwh_ref[0]) + _dot(h_lo, rwh_ref[0]) + _dot(h_hi, rwl_ref[0]) + rb_ref[0]

    lane = lax.broadcasted_iota(jnp.int32, (tm, LANES), 1)
    lanef = lane.astype(F32)
    rest = jnp.where(lane < N_EXPERTS, logits, -jnp.inf)
    vals, idxs = [], []
    for _ in range(TOP_K):
        top = jnp.max(rest, axis=-1, keepdims=True)
        idx = jnp.min(jnp.where(rest == top, lanef, float(LANES)), axis=-1, keepdims=True)
        vals.append(top)
        idxs.append(idx)
        rest = jnp.where(lanef == idx, -jnp.inf, rest)
    exps = [jnp.exp(v - vals[0]) for v in vals]
    inv = 1.0 / functools.reduce(lambda a, b: a + b, exps)
    top_w = jnp.zeros((tm, LANES), F32)
    top_i = jnp.zeros((tm, LANES), F32)
    member = jnp.zeros((tm, LANES), F32)
    for k in range(TOP_K):
        top_w = jnp.where(lane == k, exps[k] * inv, top_w)
        top_i = jnp.where(lane == k, idxs[k], top_i)
        member = member + jnp.where(lanef == idxs[k], 1.0, 0.0)
    tw_ref[...] = top_w
    ti_ref[...] = top_i.astype(jnp.int32)

    @pl.when(pl.program_id(0) == 0)
    def _():
        cnt_ref[...] = jnp.zeros_like(cnt_ref)

    cnt_ref[...] += jnp.broadcast_to(jnp.sum(member, axis=0, keepdims=True), cnt_ref.shape)


def _out_projection(y_ret, y_conv, y_nsa, x, mod_l, w_out, ln_g, ln_b, rw_hi, rw_lo, rb, layer, seq, alpha):
    tokens = x.shape[0]
    tm = ROW_TILE
    tiles_per_seq = seq // tm
    row = lambda w: pl.BlockSpec((tm, w), lambda i: (i, 0))
    lay3 = lambda a: pl.BlockSpec((1,) + a.shape[1:], lambda i: (layer,) + (0,) * (a.ndim - 1))
    return pl.pallas_call(
        functools.partial(_outproj_kernel, alpha=alpha),
        grid=(tokens // tm,),
        in_specs=[row(RET_WIDTH), row(CONV_WIDTH), row(NSA_WIDTH), row(D_MODEL),
                  pl.BlockSpec((6, 1, 1, D_MODEL), lambda i: (0, i // tiles_per_seq, 0, 0)),
                  lay3(w_out),
                  pl.BlockSpec((1, 1, 1, D_MODEL), lambda i: (layer, 0, 0, 0)),
                  pl.BlockSpec((1, 1, 1, D_MODEL), lambda i: (layer, 0, 0, 0)),
                  lay3(rw_hi), lay3(rw_lo), lay3(rb)],
        out_specs=[row(D_MODEL)] + [row(SPLIT_WIDTH)] * SC_SPLIT + [row(LANES), row(LANES),
                   pl.BlockSpec((8, LANES), lambda i: (0, 0))],
        out_shape=[jax.ShapeDtypeStruct((tokens, D_MODEL), F32)]
                  + [jax.ShapeDtypeStruct((tokens, SPLIT_WIDTH), jnp.uint32)] * SC_SPLIT
                  + [jax.ShapeDtypeStruct((tokens, LANES), jnp.int32), jax.ShapeDtypeStruct((tokens, LANES), F32),
                   jax.ShapeDtypeStruct((8, LANES), F32)],
        compiler_params=_params("arbitrary"),
        name="outproj_norm_router",
    )(y_ret, y_conv, y_nsa, x, mod_l, w_out, ln_g, ln_b, rw_hi, rw_lo, rb)


def _route_kernel(ti_ref, pstart_ref, tri_ref, pos_ref, carry_ref):
    @pl.when(pl.program_id(0) == 0)
    def _():
        carry_ref[...] = jnp.zeros_like(carry_ref)

    tm = ti_ref.shape[0]
    lane = lax.broadcasted_iota(jnp.int32, (tm, LANES), 1)
    top_i = ti_ref[...]
    onehots = [lane == top_i[:, k:k + 1] for k in range(TOP_K)]
    member = functools.reduce(lambda a, b: a + b, [jnp.where(o, 1.0, 0.0) for o in onehots])
    base = pstart_ref[...] + carry_ref[0:1] + _dot(tri_ref[...], member.astype(BF16))
    pos = jnp.zeros((tm, LANES), F32)
    for k in range(TOP_K):
        pos_k = jnp.sum(jnp.where(onehots[k], base, 0.0), axis=-1, keepdims=True)
        pos = jnp.where(lane == k, pos_k, pos)
    pos_ref[...] = pos.astype(jnp.int32)
    carry_ref[...] += jnp.broadcast_to(jnp.sum(member, axis=0, keepdims=True), carry_ref.shape)


def _route_positions(top_i, pstart):
    tokens = top_i.shape[0]
    tm = ROW_TILE
    r = np.arange(tm)
    tri = jnp.asarray(r[None, :] < r[:, None], BF16)
    return pl.pallas_call(
        _route_kernel,
        grid=(tokens // tm,),
        in_specs=[pl.BlockSpec((tm, LANES), lambda i: (i, 0)),
                  pl.BlockSpec((1, LANES), lambda i: (0, 0)),
                  pl.BlockSpec((tm, tm), lambda i: (0, 0))],
        out_specs=pl.BlockSpec((tm, LANES), lambda i: (i, 0)),
        out_shape=jax.ShapeDtypeStruct((tokens, LANES), jnp.int32),
        scratch_shapes=[pltpu.VMEM((8, LANES), F32)],
        compiler_params=_params("arbitrary"),
        name="route_positions",
    )(top_i, pstart, tri)


def _block_table(counts, n_blocks):
    tm = MOE_TILE
    cnt = counts[0, :N_EXPERTS].astype(jnp.int32)
    nblk = (cnt + tm - 1) // tm
    bend = jnp.cumsum(nblk)
    bstart = bend - nblk
    blocks = jnp.arange(n_blocks, dtype=jnp.int32)
    block_e = jnp.minimum(jnp.sum((bend[None, :] <= blocks[:, None]).astype(jnp.int32), axis=1), N_EXPERTS - 1)
    n_valid = jnp.clip(cnt[block_e] - (blocks - bstart[block_e]) * tm, 0, tm).astype(jnp.int32)
    n_valid = jnp.where(blocks < bend[-1], n_valid, 0)
    pstart = jnp.pad((bstart * tm).astype(F32), (0, LANES - N_EXPERTS))[None, :]
    return pstart, block_e.astype(jnp.int32), bend[-1:].astype(jnp.int32), n_valid


def _sc_mesh():
    return plsc.VectorSubcoreMesh(core_axis_name="core", subcore_axis_name="subcore")


def _scatter_rows(xs, idx_t, n_rows):
    tokens, width = xs[0].shape
    n_idx = idx_t.shape[0]
    n_x = len(xs)
    win = SC_WINDOW

    @functools.partial(pl.kernel, out_type=[jax.ShapeDtypeStruct((n_rows, width), xs[0].dtype)] * n_x,
                       mesh=_sc_mesh(), scratch_types=[], name="dispatch_scatter")
    def scatter(*refs):
        x_hbms, i_hbm, o_hbms = refs[:n_x], refs[n_x], refs[n_x + 1:]
        idx_specs = [pl.BlockSpec((1, win), functools.partial(lambda i, k: (k, i), k=k)) for k in range(n_idx)]
        for x_hbm, o_hbm in zip(x_hbms, o_hbms):
            def body(x_vmem, *i_vmems, o_hbm=o_hbm):
                for i_vmem in i_vmems:
                    pltpu.sync_copy(x_vmem, o_hbm.at[i_vmem.at[0]])

            pltpu.emit_pipeline(
                body,
                grid=(tokens // win,),
                in_specs=[pl.BlockSpec((win, width), lambda i: (i, 0))] + idx_specs,
                out_specs=[],
                core_axis_name=("core", "subcore"),
                dimension_semantics=(pltpu.PARALLEL,),
            )(x_hbm, *([i_hbm] * n_idx))

    return scatter(*xs, idx_t)


def _gather_rows(xs, idx):
    width = xs[0].shape[1]
    n = idx.shape[0]
    n_x = len(xs)
    win = SC_WINDOW

    @functools.partial(pl.kernel, out_type=[jax.ShapeDtypeStruct((n, width), xs[0].dtype)] * n_x,
                       mesh=_sc_mesh(), scratch_types=[], name="combine_gather")
    def gather(*refs):
        x_hbms, i_hbm, o_hbms = refs[:n_x], refs[n_x], refs[n_x + 1:]
        for x_hbm, o_hbm in zip(x_hbms, o_hbms):
            def body(i_vmem, o_vmem, x_hbm=x_hbm):
                pltpu.sync_copy(x_hbm.at[i_vmem.at[0]], o_vmem)

            pltpu.emit_pipeline(
                body,
                grid=(n // win,),
                in_specs=[pl.BlockSpec((1, win), lambda i: (0, i))],
                out_specs=[pl.BlockSpec((win, width), lambda i: (i, 0))],
                core_axis_name=("core", "subcore"),
                dimension_semantics=(pltpu.PARALLEL,),
            )(i_hbm, o_hbm)

    return gather(*xs, idx.reshape(1, n))


def _moe_kernel(be_ref, nb_ref, nv_ref, ra_ref, rb_ref, wgu_ref, bgu_ref, wd_ref, bd_ref,
                oa_ref, ob_ref, wgu_bf, wd_bf):
    row_refs = (ra_ref, rb_ref)
    out_refs = (oa_ref, ob_ref)
    i = pl.program_id(0)
    expert = be_ref[i]
    first = (i == 0) | (expert != be_ref[jnp.maximum(i - 1, 0)])
    valid = i < nb_ref[0]

    @pl.when(valid & first)
    def _():
        wgu_bf[...] = wgu_ref[0, 0].astype(BF16)
        wd_bf[...] = wd_ref[0, 0].astype(BF16)

    @pl.when(valid)
    def _():
        live = lax.broadcasted_iota(jnp.int32, ra_ref.shape, 0) < nv_ref[i]
        gu = bgu_ref[0, 0]
        for j, r_ref in enumerate(row_refs):
            halves = _unpack_pairs(jnp.where(live, r_ref[...], jnp.uint32(0)))
            for k, rows in enumerate(halves):
                w0 = (2 * j + k) * SPLIT_WIDTH
                gu = gu + _dot(rows.astype(BF16), wgu_bf[w0:w0 + SPLIT_WIDTH, :])
        g = jnp.minimum(gu[:, :D_EXPERT], SWIGLU_LIMIT)
        u = jnp.clip(gu[:, D_EXPERT:], -SWIGLU_LIMIT, SWIGLU_LIMIT)
        act = (u + 1.0) * (g * jax.nn.sigmoid(SWIGLU_ALPHA * g))
        y = _dot(act.astype(BF16), wd_bf[...]) + bd_ref[0, 0]
        for j, o_ref in enumerate(out_refs):
            o_ref[...] = _pack_pairs(y[:, j * 2 * SPLIT_WIDTH:(j + 1) * 2 * SPLIT_WIDTH])

    @pl.when(jnp.logical_not(valid))
    def _():
        for o_ref in out_refs:
            o_ref[...] = jnp.zeros_like(o_ref)


def _expert_ffn(rows, block_e, n_used, n_valid, w_gu, b_gu, w_down, b_down, layer):
    n_rows = rows[0].shape[0]
    tm = MOE_TILE
    part = pl.BlockSpec((tm, SPLIT_WIDTH), lambda i, be, nb, nv: (i, 0))
    grid_spec = pltpu.PrefetchScalarGridSpec(
        num_scalar_prefetch=3,
        grid=(n_rows // tm,),
        in_specs=[part] * SC_SPLIT + [
                  pl.BlockSpec((1, 1, D_MODEL, 2 * D_EXPERT), lambda i, be, nb, nv: (layer, be[i], 0, 0)),
                  pl.BlockSpec((1, 1, 1, 2 * D_EXPERT), lambda i, be, nb, nv: (layer, be[i], 0, 0)),
                  pl.BlockSpec((1, 1, D_EXPERT, D_MODEL), lambda i, be, nb, nv: (layer, be[i], 0, 0)),
                  pl.BlockSpec((1, 1, 1, D_MODEL), lambda i, be, nb, nv: (layer, be[i], 0, 0))],
        out_specs=[part] * SC_SPLIT,
        scratch_shapes=[pltpu.VMEM((D_MODEL, 2 * D_EXPERT), BF16), pltpu.VMEM((D_EXPERT, D_MODEL), BF16)])
    return pl.pallas_call(
        _moe_kernel,
        grid_spec=grid_spec,
        out_shape=[jax.ShapeDtypeStruct((n_rows, SPLIT_WIDTH), jnp.uint32)] * SC_SPLIT,
        compiler_params=_params("arbitrary"),
        name="expert_ffn",
    )(block_e, n_used, n_valid, *rows, w_gu, b_gu, w_down, b_down)


def _final_kernel(x_ref, tw_ref, mod_ref, lng_ref, lnb_ref, *rest, alpha):
    y_refs, o_ref = rest[:-1], rest[-1]
    top_w = tw_ref[...]
    parts = []
    for j in range(SC_SPLIT):
        acc = [0.0, 0.0]
        for k in range(TOP_K):
            for half, y in enumerate(_unpack_pairs(y_refs[j * TOP_K + k][...])):
                acc[half] = acc[half] + top_w[:, k:k + 1] * y
        parts += acc
    ffn = jnp.concatenate(parts, axis=1)
    o_ref[...] = (_layer_norm(alpha * x_ref[...] + (1.0 + mod_ref[5, 0]) * ffn) * lng_ref[0, 0] + lnb_ref[0, 0])


def _final_norm(x1, y_parts, top_w, mod_l, ln_g, ln_b, layer, seq, alpha):
    tokens = x1.shape[0]
    tm = ROW_TILE
    tiles = tokens // tm
    tiles_per_seq = seq // tm
    row = lambda w: pl.BlockSpec((tm, w), lambda i: (i, 0))
    vec = pl.BlockSpec((1, 1, 1, D_MODEL), lambda i: (layer, 1, 0, 0))
    y_specs, y_args = [], []
    for j in range(SC_SPLIT):
        for k in range(TOP_K):
            y_specs.append(pl.BlockSpec((tm, SPLIT_WIDTH), functools.partial(lambda i, k: (k * tiles + i, 0), k=k)))
            y_args.append(y_parts[j])
    return pl.pallas_call(
        functools.partial(_final_kernel, alpha=alpha),
        grid=(tiles,),
        in_specs=[row(D_MODEL), row(LANES),
                  pl.BlockSpec((6, 1, 1, D_MODEL), lambda i: (0, i // tiles_per_seq, 0, 0)), vec, vec] + y_specs,
        out_specs=row(D_MODEL),
        out_shape=jax.ShapeDtypeStruct(x1.shape, F32),
        compiler_params=_params("arbitrary"),
        name="combine_final_norm",
    )(x1, top_w, mod_l, ln_g, ln_b, *y_args)


def kernel(x, c, positions, w_in, w_out, ret_gn_w, conv_w, cmp_pos, cmp_w1, cmp_w2, ada_w, ada_b, ln_g, ln_b,
           router_w, router_b, w_gate_up, b_gate_up, w_down, b_down):
    batch, seq, _ = x.shape
    depth = w_in.shape[0]
    tokens = batch * seq
    n_rows = tokens * TOP_K + N_EXPERTS * MOE_TILE
    alpha = float((2 * depth) ** 0.25)

    w_in_r = jnp.pad(w_in, ((0, 0), (0, 0), (0, N_COLS - w_in.shape[2]))).astype(BF16)
    w_out_b = w_out.astype(BF16)
    rw = jnp.pad(router_w, ((0, 0), (0, 0), (0, LANES - N_EXPERTS)))
    rw_hi = rw.astype(BF16)
    rw_lo = (rw - rw_hi.astype(F32)).astype(BF16)
    rb = jnp.pad(router_b, ((0, 0), (0, LANES - N_EXPERTS))).reshape(depth, 1, LANES)
    cw = _compress_weights(cmp_pos, cmp_w1, cmp_w2)
    ret_consts = _retention_consts()
    nsa_consts = _nsa_consts(seq)
    gn_w = ret_gn_w.reshape(depth, 1, RET_WIDTH)
    ln_g4 = ln_g.reshape(depth, 2, 1, D_MODEL)
    ln_b4 = ln_b.reshape(depth, 2, 1, D_MODEL)
    b_gu4 = b_gate_up.reshape(depth, N_EXPERTS, 1, 2 * D_EXPERT)
    b_d4 = b_down.reshape(depth, N_EXPERTS, 1, D_MODEL)

    mod = _modulation(c, ada_w, ada_b)
    cos, sin = _rope_tables(positions)
    xt = x.reshape(tokens, D_MODEL)
    for l in range(depth):
        (rq, rk, rv, rg, y_conv, qu, qr, kcx, vcx, ks, vs, kw, vw, ng) = _input_projection(
            xt, mod[l], w_in_r, cos, sin, conv_w, l, seq)
        y_ret = _retention(rq, rk, rv, rg, ret_consts, gn_w, l, batch, seq)
        kc, vc = _compress(kcx, vcx, cw, l, batch, seq)
        y_nsa = _nsa(qu, qr, kc, vc, ks, vs, kw, vw, ng, nsa_consts, batch, seq)
        x1, *h2, top_i, top_w, counts = _out_projection(y_ret, y_conv, y_nsa, xt, mod[l], w_out_b, ln_g4, ln_b4,
                                                        rw_hi, rw_lo, rb, l, seq, alpha)
        pstart, block_e, n_used, n_valid = _block_table(counts, n_rows // MOE_TILE)
        pos_t = _route_positions(top_i, pstart)[:, :TOP_K].T
        rows = _scatter_rows(h2, pos_t, n_rows)
        y = _expert_ffn(rows, block_e, n_used, n_valid, w_gate_up, b_gu4, w_down, b_d4, l)
        y_tok = _gather_rows(y, pos_t.reshape(-1))
        xt = _final_norm(x1, y_tok, top_w, mod[l], ln_g4, ln_b4, l, seq, alpha)
    return xt.reshape(batch, seq, D_MODEL)
```

```python
import functools

import numpy as np
import jax
import jax.numpy as jnp
from jax import lax
from jax.experimental import pallas as pl
from jax.experimental.pallas import tpu as pltpu
from jax.experimental.pallas import tpu_sc as plsc

F32 = jnp.float32
BF16 = jnp.bfloat16

D_MODEL = 1024
HEAD_DIM = 64
RET_WIDTH = 256
RET_HEADS = 4
RET_CHUNK = 128
CONV_WIDTH = 256
CONV_K = 3
NSA_WIDTH = 512
NSA_HEADS = 8
NSA_KV_HEADS = 2
NSA_GROUP = 4
NSA_KV_WIDTH = 128
CMP_LEN = 32
CMP_STRIDE = 16
CMP_HIDDEN = 128
SEL_LEN = 64
SEL_TOP = 8
WINDOW = 512
Q_BLOCK = 128
Q_TILE = 2 * Q_BLOCK
ROPE_THETA = 10000.0
N_EXPERTS = 32
TOP_K = 4
D_EXPERT = 1024
SWIGLU_LIMIT = 7.0
SWIGLU_ALPHA = 1.702
LN_EPS = 1e-5
NEG_INF = -1e30
FORCE_SCORE = 1e9

LANES = 128
VMEM_LIMIT = 56 * 1024 * 1024

OFF_RQ, OFF_RK, OFF_RV, OFF_RG = 0, 256, 512, 768
OFF_CB, OFF_CC, OFF_CH = 1024, 1280, 1536
OFF_NQ = 1792
OFF_KC, OFF_VC = 2304, 2432
OFF_KS, OFF_VS, OFF_KW, OFF_VW = 2560, 2688, 2816, 2944
OFF_NG = 3072
N_GATE = NSA_HEADS * 3
N_COLS = OFF_NG + LANES
LOG2E = 1.4426950408889634

RET_BATCH = 4
ROW_TILE = 256
MOE_TILE = 256
ROUTE_TILE = 1024
SEL_GROUP = 4
V_ROWS = HEAD_DIM + 16
SC_WINDOW = LANES
SC_SPLIT = 2
SPLIT_WIDTH = D_MODEL // (2 * SC_SPLIT)


def _dot(a, b):
    return jnp.dot(a, b, preferred_element_type=F32)


def _dot_nt(a, b):
    return lax.dot_general(a, b, (((1,), (1,)), ((), ())), preferred_element_type=F32)


def _layer_norm(x):
    mu = jnp.mean(x, axis=-1, keepdims=True)
    xc = x - mu
    var = jnp.mean(xc * xc, axis=-1, keepdims=True)
    return xc * lax.rsqrt(var + LN_EPS)


def _pack_pairs(x):
    bits = lambda t: lax.bitcast_convert_type(t.astype(BF16).astype(F32), jnp.uint32)
    return (bits(x[:, SPLIT_WIDTH:]) & jnp.uint32(0xFFFF0000)) | (bits(x[:, :SPLIT_WIDTH]) >> 16)


def _unpack_pairs(p):
    low = lax.bitcast_convert_type(p << 16, F32)
    high = lax.bitcast_convert_type(p & jnp.uint32(0xFFFF0000), F32)
    return low, high


def _params(*sem):
    return pltpu.CompilerParams(dimension_semantics=sem, vmem_limit_bytes=VMEM_LIMIT)


def _mod_kernel(c_ref, w_ref, b_ref, o_ref):
    c = c_ref[...]
    ca = (c * jax.nn.sigmoid(c)).astype(BF16)
    o_ref[0, 0] = _dot(ca, w_ref[0].astype(BF16)) + b_ref[0]


def _modulation(c, ada_w, ada_b):
    depth = ada_w.shape[0]
    batch = c.shape[0]
    out = pl.pallas_call(
        _mod_kernel,
        grid=(depth, 6),
        in_specs=[pl.BlockSpec((batch, D_MODEL), lambda l, j: (0, 0)),
                  pl.BlockSpec((1, D_MODEL, D_MODEL), lambda l, j: (l, 0, j)),
                  pl.BlockSpec((1, 1, D_MODEL), lambda l, j: (l * 6 + j, 0, 0))],
        out_specs=pl.BlockSpec((1, 1, batch, D_MODEL), lambda l, j: (l, j, 0, 0)),
        out_shape=jax.ShapeDtypeStruct((depth, 6, batch, D_MODEL), F32),
        compiler_params=_params("arbitrary", "arbitrary"),
        name="adaln_mod",
    )(c, ada_w, ada_b.reshape(depth * 6, 1, D_MODEL))
    return out.reshape(depth, 6, batch, 1, D_MODEL)


def _rope_table_kernel(pos_ref, inv_ref, cos_ref, sin_ref):
    ang = pos_ref[...] * inv_ref[...]
    cos_ref[...] = jnp.cos(ang)
    sin_ref[...] = jnp.sin(ang)


def _rope_tables(positions):
    half = HEAD_DIM // 2
    per_row = LANES // half
    tokens = positions.size
    pos4 = jnp.repeat(positions.reshape(tokens // per_row, per_row).astype(F32), half, axis=1)
    inv = ROPE_THETA ** (-jnp.arange(half, dtype=F32) / half)
    inv4 = jnp.tile(inv, per_row)[None, :]
    rows = tokens // per_row
    tile = min(rows, 1024)
    cos4, sin4 = pl.pallas_call(
        _rope_table_kernel,
        grid=(rows // tile,),
        in_specs=[pl.BlockSpec((tile, LANES), lambda i: (i, 0)),
                  pl.BlockSpec((1, LANES), lambda i: (0, 0))],
        out_specs=[pl.BlockSpec((tile, LANES), lambda i: (i, 0))] * 2,
        out_shape=[jax.ShapeDtypeStruct((rows, LANES), F32)] * 2,
        compiler_params=_params("arbitrary"),
        name="rope_tables",
    )(pos4, inv4)
    cos = jnp.tile(cos4.reshape(tokens, half), (1, per_row))
    sign = jnp.tile(jnp.concatenate([-jnp.ones((half,), F32), jnp.ones((half,), F32)]), LANES // HEAD_DIM)
    sin = jnp.tile(sin4.reshape(tokens, half), (1, per_row)) * sign[None, :]
    return cos, sin


def _inproj_kernel(x_ref, mod_ref, w_ref, cos_ref, sin_ref, convw_ref,
                   rq_ref, rk_ref, rv_ref, rg_ref, yc_ref, qu_ref, qr_ref, kc_ref, vc_ref,
                   ks_ref, vs_ref, kw_ref, vw_ref, ng_ref, carry_ref, *, tiles_per_seq):
    i = pl.program_id(0)
    tm = x_ref.shape[0]
    h = (_layer_norm(x_ref[...]) * (1.0 + mod_ref[1, 0]) + mod_ref[0, 0]).astype(BF16)
    cosf = cos_ref[...]
    sinf = sin_ref[...]
    lane = lax.broadcasted_iota(jnp.int32, (tm, LANES), 1)
    first_half = (lane % HEAD_DIM) < (HEAD_DIM // 2)

    def proj(off, width):
        return _dot(h, w_ref[0, :, off:off + width])

    def rope(c):
        cols = []
        for j in range(c.shape[1] // LANES):
            cj = c[:, j * LANES:(j + 1) * LANES]
            swapped = jnp.where(first_half, pltpu.roll(cj, LANES - HEAD_DIM // 2, 1),
                                pltpu.roll(cj, HEAD_DIM // 2, 1))
            cols.append(cj * cosf + swapped * sinf)
        return jnp.concatenate(cols, axis=1) if len(cols) > 1 else cols[0]

    scale = HEAD_DIM ** -0.5
    rq_ref[...] = rope(proj(OFF_RQ, RET_WIDTH)).astype(BF16)
    rk_ref[...] = (rope(proj(OFF_RK, RET_WIDTH)) * scale).astype(BF16)
    rv_ref[...] = proj(OFF_RV, RET_WIDTH).astype(BF16)
    rg = proj(OFF_RG, RET_WIDTH)
    rg_ref[...] = (rg * jax.nn.sigmoid(rg)).astype(BF16)

    cb = proj(OFF_CB, CONV_WIDTH)
    u = proj(OFF_CC, CONV_WIDTH) * proj(OFF_CH, CONV_WIDTH)

    @pl.when(i % tiles_per_seq == 0)
    def _():
        carry_ref[...] = jnp.zeros_like(carry_ref)

    carry = carry_ref[...]
    row = lax.broadcasted_iota(jnp.int32, (tm, CONV_WIDTH), 0)
    prev1 = jnp.where(row == 0, carry[7:8], pltpu.roll(u, 1, 0))
    prev2 = jnp.where(row == 0, carry[6:7], jnp.where(row == 1, carry[7:8], pltpu.roll(u, 2, 0)))
    cw = convw_ref[0]
    yc_ref[...] = (cb * (cw[0:1] * prev2 + cw[1:2] * prev1 + cw[2:3] * u)).astype(BF16)
    carry_ref[...] = u[tm - 8:tm]

    nq = proj(OFF_NQ, NSA_WIDTH) * (scale * LOG2E)
    qu_ref[...] = nq.T.astype(BF16)
    qr_ref[...] = rope(nq).T.astype(BF16)
    kc_ref[...] = proj(OFF_KC, NSA_KV_WIDTH)
    vc_ref[...] = proj(OFF_VC, NSA_KV_WIDTH)
    def values_t(off):
        v_t = proj(off, NSA_KV_WIDTH).T
        ones = jnp.ones((V_ROWS - HEAD_DIM, tm), F32)
        parts = []
        for hd in range(NSA_KV_HEADS):
            parts += [v_t[hd * HEAD_DIM:(hd + 1) * HEAD_DIM], ones]
        return jnp.concatenate(parts, axis=0).astype(BF16)

    ks_ref[...] = rope(proj(OFF_KS, NSA_KV_WIDTH)).astype(BF16)
    vs_ref[...] = values_t(OFF_VS)
    kw_ref[...] = rope(proj(OFF_KW, NSA_KV_WIDTH)).astype(BF16)
    vw_ref[...] = values_t(OFF_VW)
    ng_ref[...] = proj(OFF_NG, LANES).T


def _input_projection(x, mod_l, w_l, cos, sin, conv_w, layer, seq):
    tokens = x.shape[0]
    tm = ROW_TILE
    tiles_per_seq = seq // tm
    row = lambda w: pl.BlockSpec((tm, w), lambda i: (i, 0))
    col = lambda w: pl.BlockSpec((w, tm), lambda i: (0, i))
    outs = [(RET_WIDTH, BF16, False)] * 4 + [(CONV_WIDTH, BF16, False), (NSA_WIDTH, BF16, True),
                                            (NSA_WIDTH, BF16, True), (NSA_KV_WIDTH, F32, False),
                                            (NSA_KV_WIDTH, F32, False), (NSA_KV_WIDTH, BF16, False),
                                            (NSA_KV_HEADS * V_ROWS, BF16, True), (NSA_KV_WIDTH, BF16, False),
                                            (NSA_KV_HEADS * V_ROWS, BF16, True), (LANES, F32, True)]
    return pl.pallas_call(
        functools.partial(_inproj_kernel, tiles_per_seq=tiles_per_seq),
        grid=(tokens // tm,),
        in_specs=[row(D_MODEL),
                  pl.BlockSpec((6, 1, 1, D_MODEL), lambda i: (0, i // tiles_per_seq, 0, 0)),
                  pl.BlockSpec((1, D_MODEL, N_COLS), lambda i: (layer, 0, 0)),
                  row(LANES), row(LANES),
                  pl.BlockSpec((1, CONV_K, CONV_WIDTH), lambda i: (layer, 0, 0))],
        out_specs=[col(w) if t else row(w) for w, _, t in outs],
        out_shape=[jax.ShapeDtypeStruct((w, tokens) if t else (tokens, w), dt) for w, dt, t in outs],
        scratch_shapes=[pltpu.VMEM((8, CONV_WIDTH), F32)],
        compiler_params=_params("arbitrary"),
        name="ln_inproj",
    )(x, mod_l, w_l, cos, sin, conv_w)


def _retention_kernel(q_ref, k_ref, v_ref, g_ref, intra_ref, qdec_ref, kdec_ref, cdec_ref, gn_ref,
                      o_ref, state_ref):
    @pl.when(pl.program_id(1) == 0)
    def _():
        state_ref[...] = jnp.zeros_like(state_ref)

    n_pairs = RET_WIDTH // LANES
    lane = lax.broadcasted_iota(jnp.int32, (RET_CHUNK, LANES), 1)
    low = lane < HEAD_DIM
    blk_r = lax.broadcasted_iota(jnp.int32, (LANES, LANES), 0) < HEAD_DIM
    blk_c = lax.broadcasted_iota(jnp.int32, (LANES, LANES), 1) < HEAD_DIM
    same_head = blk_r == blk_c
    states = {(b, p): state_ref[b, p] for b in range(q_ref.shape[0]) for p in range(n_pairs)}
    for ch, b in [(ch, b) for ch in range(q_ref.shape[1] // RET_CHUNK) for b in range(q_ref.shape[0])]:
        rows = slice(ch * RET_CHUNK, (ch + 1) * RET_CHUNK)
        outs = []
        for p in range(n_pairs):
            cols = slice(p * LANES, (p + 1) * LANES)
            qp = q_ref[b, rows, cols]
            kp = k_ref[b, rows, cols]
            vp = v_ref[b, rows, cols]
            out = jnp.zeros((RET_CHUNK, LANES), F32)
            for hh in range(2):
                keep = low if hh == 0 else jnp.logical_not(low)
                qm = jnp.where(keep, qp, jnp.zeros_like(qp))
                s = _dot_nt(qm, kp) * intra_ref[2 * p + hh]
                out = jnp.where(keep, _dot(s.astype(BF16), vp), out)
            state = states[b, p]
            qd = (qp.astype(F32) * qdec_ref[:, cols]).astype(BF16)
            out = out + _dot(qd, state.astype(BF16))
            kd_t = (kp.astype(F32) * kdec_ref[:, cols]).T.astype(BF16)
            upd = _dot(kd_t, vp)
            states[b, p] = state * cdec_ref[p] + jnp.where(same_head, upd, 0.0)
            def half_mean(t):
                lo_sum = jnp.sum(jnp.where(low, t, 0.0), axis=-1, keepdims=True)
                hi_sum = jnp.sum(jnp.where(low, 0.0, t), axis=-1, keepdims=True)
                return jnp.where(low, lo_sum, hi_sum) * (1.0 / HEAD_DIM)
            oc = out - half_mean(out)
            outs.append(oc * lax.rsqrt(half_mean(oc * oc) + LN_EPS))
        normed = jnp.concatenate(outs, axis=1)
        o_ref[b, rows, :] = (normed * gn_ref[0] * g_ref[b, rows, :].astype(F32)).astype(BF16)
    for (b, p), state in states.items():
        state_ref[b, p] = state


def _retention_consts():
    heads = jnp.arange(RET_HEADS, dtype=F32)
    log_gamma = jnp.log(1.0 - jnp.power(2.0, -5.0 - heads))
    i = jnp.arange(RET_CHUNK, dtype=F32)
    diff = i[:, None] - i[None, :]
    intra = jnp.where(diff >= 0, jnp.exp(diff * log_gamma[:, None, None]), 0.0)
    q_dec = jnp.exp((i + 1.0) * log_gamma[:, None])
    k_dec = jnp.exp((RET_CHUNK - 1.0 - i) * log_gamma[:, None])
    c_dec = jnp.exp(RET_CHUNK * log_gamma)
    expand = lambda t: jnp.repeat(t.T, HEAD_DIM, axis=1)
    c_rows = jnp.repeat(c_dec, HEAD_DIM).reshape(RET_WIDTH // LANES, LANES, 1)
    c_blk = jnp.broadcast_to(c_rows, (RET_WIDTH // LANES, LANES, LANES))
    return intra, expand(q_dec), expand(k_dec), c_blk


def _retention(rq, rk, rv, rg, consts, gn_w, layer, batch, seq):
    intra, q_dec, k_dec, c_blk = consts
    rows = 2 * RET_CHUNK
    steps = seq // rows
    group = RET_BATCH if batch % RET_BATCH == 0 else 1
    blk = pl.BlockSpec((group, rows, RET_WIDTH), lambda b, s: (b, s, 0))
    full = lambda a: pl.BlockSpec(a.shape, lambda b, s: (0,) * a.ndim)
    per_seq = lambda t: t.reshape(batch, seq, RET_WIDTH)
    out = pl.pallas_call(
        _retention_kernel,
        grid=(batch // group, steps),
        in_specs=[blk, blk, blk, blk, full(intra), full(q_dec), full(k_dec), full(c_blk),
                  pl.BlockSpec((1, 1, RET_WIDTH), lambda b, s: (layer, 0, 0))],
        out_specs=blk,
        out_shape=jax.ShapeDtypeStruct((batch, seq, RET_WIDTH), BF16),
        scratch_shapes=[pltpu.VMEM((group, RET_WIDTH // LANES, LANES, LANES), F32)],
        compiler_params=_params("arbitrary", "arbitrary"),
        name="retention",
    )(per_seq(rq), per_seq(rk), per_seq(rv), per_seq(rg), intra, q_dec, k_dec, c_blk, gn_w)
    return out.reshape(batch * seq, RET_WIDTH)


def _compress_kernel(xk_ref, xv_ref, posa_ref, posb_ref, w1a_ref, w1b_ref, w2_ref, kc_ref, vc_ref):
    n_grp = xk_ref.shape[0] // CMP_STRIDE
    for kv, (x_ref, o_ref) in enumerate(((xk_ref, kc_ref), (xv_ref, vc_ref))):
        ya = jnp.zeros((n_grp, NSA_KV_HEADS * CMP_HIDDEN), F32)
        yb = jnp.zeros((n_grp, NSA_KV_HEADS * CMP_HIDDEN), F32)
        for r in range(CMP_STRIDE):
            x = x_ref[pl.ds(r, n_grp, stride=CMP_STRIDE), :]
            cols = slice(r * NSA_KV_WIDTH, (r + 1) * NSA_KV_WIDTH)
            ya = ya + _dot((x + posa_ref[0, kv, :, cols]).astype(BF16), w1a_ref[0, kv, cols, :])
            yb = yb + _dot((x + posb_ref[0, kv, :, cols]).astype(BF16), w1b_ref[0, kv, cols, :])
        hidden = ya + pltpu.roll(yb, n_grp - 1, 0)
        act = jax.nn.gelu(hidden)
        out = _dot(act.astype(BF16), w2_ref[0, kv])
        o_ref[0] = (out if kv == 0 else out.T).astype(BF16)


def _compress_weights(cmp_pos, cmp_w1, cmp_w2):
    depth = cmp_w1.shape[0]
    eye = jnp.eye(NSA_KV_HEADS, dtype=F32)
    a = cmp_w1.reshape(depth, 2, 2, CMP_STRIDE, HEAD_DIM, CMP_HIDDEN)
    w1 = jnp.einsum('lkardj,hg->lkarhdgj', a, eye)
    w1 = w1.reshape(depth, 2, 2, CMP_STRIDE * NSA_KV_WIDTH, NSA_KV_HEADS * CMP_HIDDEN).astype(BF16)
    pos = cmp_pos.reshape(depth, 2, 2, CMP_STRIDE, 1, HEAD_DIM)
    pos = jnp.broadcast_to(pos, (depth, 2, 2, CMP_STRIDE, NSA_KV_HEADS, HEAD_DIM))
    pos = pos.reshape(depth, 2, 2, 1, CMP_STRIDE * NSA_KV_WIDTH)
    w2 = jnp.einsum('lkje,hg->lkhjge', cmp_w2, eye)
    w2 = w2.reshape(depth, 2, NSA_KV_HEADS * CMP_HIDDEN, NSA_KV_WIDTH).astype(BF16)
    return w1[:, :, 0], w1[:, :, 1], pos[:, :, 0], pos[:, :, 1], w2


def _compress(kcx, vcx, cw, layer, batch, seq):
    w1a, w1b, posa, posb, w2 = cw
    n_grp = seq // CMP_STRIDE
    xblk = pl.BlockSpec((seq, NSA_KV_WIDTH), lambda b: (b, 0))
    lay = lambda a: pl.BlockSpec((1,) + a.shape[1:], lambda b: (layer,) + (0,) * (a.ndim - 1))
    oblk = pl.BlockSpec((1, n_grp, NSA_KV_WIDTH), lambda b: (b, 0, 0))
    return pl.pallas_call(
        _compress_kernel,
        grid=(batch,),
        in_specs=[xblk, xblk, lay(posa), lay(posb), lay(w1a), lay(w1b), lay(w2)],
        out_specs=[oblk, oblk],
        out_shape=[jax.ShapeDtypeStruct((batch, n_grp, NSA_KV_WIDTH), BF16)] * 2,
        compiler_params=_params("arbitrary"),
        name="nsa_compress",
    )(kcx, vcx, posa, posb, w1a, w1b, w2)


def _nsa_kernel(qu_ref, qr_ref, kc_ref, vc_ref, ks_ref, vs_ref, kw_ref, vw_ref, ng_ref, ov_ref, wbias_ref, ind_ref,
                o_ref):
    qb = pl.program_id(1)
    q0 = qb * Q_TILE
    qb_last = (qb + 1) * (Q_TILE // Q_BLOCK) - 1
    n_sel = ov_ref.shape[0]
    cols4 = NSA_GROUP * Q_TILE
    n_past = WINDOW // Q_BLOCK
    gsig = jax.nn.sigmoid(ng_ref[...])

    def padded_q(ref, h):
        zeros = jnp.zeros((HEAD_DIM, Q_TILE), BF16)
        parts = []
        for g in range(NSA_GROUP):
            hq = h * NSA_GROUP + g
            x = ref[hq * HEAD_DIM:(hq + 1) * HEAD_DIM, :]
            parts.append(jnp.concatenate([x, zeros] if h == 0 else [zeros, x], axis=0))
        return jnp.concatenate(parts, axis=1)

    def tile4(t):
        return jnp.concatenate([t] * NSA_GROUP, axis=1)

    def attend(state, keys, values_t, bias, q_t):
        m, acc = state
        s = _dot(keys, q_t)
        if bias is not None:
            s = s + bias
        m_new = jnp.maximum(m, jnp.max(s, axis=0, keepdims=True))
        e = jnp.exp2(s - m_new)
        acc = jnp.exp2(m - m_new) * acc + _dot(values_t, e.astype(BF16))
        return m_new, acc

    def normalised(state):
        acc = state[1]
        return acc[0:HEAD_DIM] * (1.0 / acc[HEAD_DIM:HEAD_DIM + 1])

    init = (jnp.full((1, cols4), NEG_INF, F32), jnp.zeros((V_ROWS, cols4), F32))
    head_rows = [slice(h * V_ROWS, (h + 1) * V_ROWS) for h in range(NSA_KV_HEADS)]
    q_rot = [padded_q(qr_ref, h) for h in range(NSA_KV_HEADS)]

    def position_bias(kb, windowed):
        first = qb * (Q_TILE // Q_BLOCK)
        return tile4(jnp.concatenate(
            [wbias_ref[jnp.clip(kb - (first + sub) + n_past + 1, 0 if windowed else 2, n_past + 2)]
             for sub in range(Q_TILE // Q_BLOCK)], axis=1))

    o_cmp = []
    q_sel = []
    for h in range(NSA_KV_HEADS):
        hr = slice(h * HEAD_DIM, (h + 1) * HEAD_DIM)
        qus = padded_q(qu_ref, h)

        s = _dot(kc_ref[0], qus)
        cid = lax.broadcasted_iota(jnp.int32, (LANES, Q_TILE), 0)
        tq = q0 + lax.broadcasted_iota(jnp.int32, (LANES, Q_TILE), 1)
        cvalid = tile4(jnp.where(cid * CMP_STRIDE + (CMP_LEN - 1) <= tq, 1.0, 0.0))
        sm = jnp.where(cvalid > 0.5, s, NEG_INF)
        e = jnp.exp2(sm - jnp.max(sm, axis=0, keepdims=True)) * cvalid
        l = jnp.sum(e, axis=0, keepdims=True)
        p = e * (1.0 / jnp.where(l > 0.0, l, 1.0))
        o_cmp.append(_dot(vc_ref[0, hr, :], p.astype(BF16)))

        psum = p[:, 0:Q_TILE]
        for g in range(1, NSA_GROUP):
            psum = psum + p[:, g * Q_TILE:(g + 1) * Q_TILE]
        p_hi = psum.astype(BF16)
        rem = psum - p_hi.astype(F32)
        p_mid = rem.astype(BF16)
        p_lo = (rem - p_mid.astype(F32)).astype(BF16)
        ov = ov_ref[...]
        imp = _dot(ov, p_hi) + _dot(ov, p_mid) + _dot(ov, p_lo)
        jid = lax.broadcasted_iota(jnp.int32, (n_sel, Q_TILE), 0)
        tid = q0 + lax.broadcasted_iota(jnp.int32, (n_sel, Q_TILE), 1)
        forced = (jid == 0) | (jid == jnp.right_shift(tid, 6))
        imp = jnp.where(forced, FORCE_SCORE, jnp.where(jid * SEL_LEN <= tid, imp, -FORCE_SCORE))
        rank = jnp.zeros((n_sel, Q_TILE), F32)
        for i in range(n_sel):
            ri = imp[i:i + 1, :]
            beats = (ri > imp) | ((ri == imp) & (jid > i))
            rank = rank + jnp.where(beats, 1.0, 0.0)
        sel_bias = tile4(jnp.where(rank < float(SEL_TOP), 0.0, NEG_INF)).astype(BF16)
        pad = jnp.zeros((LANES - n_sel, cols4), BF16)
        q_sel.append(jnp.concatenate([q_rot[h], sel_bias, pad], axis=0))

    def sel_step(gi, states, last):
        k0 = pl.multiple_of(gi * (SEL_GROUP * Q_BLOCK), SEL_GROUP * Q_BLOCK)
        keys = jnp.concatenate([ks_ref[pl.ds(k0, SEL_GROUP * Q_BLOCK), :], ind_ref[gi]], axis=1)
        bias = None
        if last:
            bias = jnp.concatenate([position_bias(gi * SEL_GROUP + u, False) for u in range(SEL_GROUP)], axis=0)
        return tuple(attend(states[h], keys, vs_ref[head_rows[h], pl.ds(k0, SEL_GROUP * Q_BLOCK)], bias, q_sel[h])
                     for h in range(NSA_KV_HEADS))

    last_group = qb_last // SEL_GROUP
    sel_states = lax.fori_loop(0, last_group, functools.partial(sel_step, last=False), (init,) * NSA_KV_HEADS)
    sel_states = sel_step(last_group, sel_states, True)

    ws = pl.multiple_of(jnp.maximum(q0 - WINDOW, 0), Q_BLOCK)
    n_win = n_past + Q_TILE // Q_BLOCK
    win_bias = jnp.concatenate([position_bias(ws // Q_BLOCK + u, True) for u in range(n_win)], axis=0)
    win_keys = kw_ref[pl.ds(ws, n_win * Q_BLOCK), :]
    heads = []
    for h in range(NSA_KV_HEADS):
        o_s = normalised(sel_states[h])
        o_w = normalised(attend(init, win_keys, vw_ref[head_rows[h], pl.ds(ws, n_win * Q_BLOCK)], win_bias, q_rot[h]))
        o_c = o_cmp[h]
        for g in range(NSA_GROUP):
            hq = h * NSA_GROUP + g
            c = slice(g * Q_TILE, (g + 1) * Q_TILE)
            heads.append(gsig[3 * hq:3 * hq + 1, :] * o_c[:, c] + gsig[3 * hq + 1:3 * hq + 2, :] * o_s[:, c]
                         + gsig[3 * hq + 2:3 * hq + 3, :] * o_w[:, c])
    cols_out = [jnp.concatenate(heads[2 * m:2 * m + 2], axis=0).T for m in range(NSA_HEADS // 2)]
    o_ref[...] = jnp.concatenate(cols_out, axis=1).astype(BF16)


def _nsa_consts(seq):
    n_cmp_pad = seq // CMP_STRIDE
    n_sel = seq // SEL_LEN
    c = np.arange(n_cmp_pad)
    j = np.arange(n_sel)
    n_cmp = (seq - CMP_LEN) // CMP_STRIDE + 1
    ov = ((c[None, :] * CMP_STRIDE < j[:, None] * SEL_LEN + SEL_LEN) &
          (j[:, None] * SEL_LEN <= c[None, :] * CMP_STRIDE + CMP_LEN - 1) & (c[None, :] < n_cmp))
    kl = np.arange(Q_BLOCK)[:, None]
    tl = np.arange(Q_BLOCK)[None, :]
    n_past = WINDOW // Q_BLOCK
    none = np.zeros((Q_BLOCK, Q_BLOCK), bool)
    valid = [none, kl > tl] + [np.ones((Q_BLOCK, Q_BLOCK), bool)] * (n_past - 1) + [kl <= tl, none]
    wbias = np.where(np.stack(valid), 0.0, NEG_INF).astype(np.float32)
    key_block = np.arange(seq) // SEL_LEN
    onehot = (key_block[:, None] == np.arange(LANES)[None, :]).reshape(-1, SEL_GROUP * Q_BLOCK, LANES)
    return jnp.asarray(ov, BF16), jnp.asarray(wbias), jnp.asarray(onehot, BF16)


def _nsa(qu, qr, kc, vc, ks, vs, kw, vw, ng, consts, batch, seq):
    ov, wbias, onehot = consts
    n_qb = seq // Q_TILE
    tokens = batch * seq
    qblk = pl.BlockSpec((NSA_WIDTH, Q_TILE), lambda b, q: (0, b * n_qb + q))
    cblk = pl.BlockSpec((1, seq // CMP_STRIDE, NSA_KV_WIDTH), lambda b, q: (b, 0, 0))
    kblk = pl.BlockSpec((seq, NSA_KV_WIDTH), lambda b, q: (b, 0))
    vblk = pl.BlockSpec((NSA_KV_HEADS * V_ROWS, seq), lambda b, q: (0, b))
    full = lambda a: pl.BlockSpec(a.shape, lambda b, q: (0,) * a.ndim)
    return pl.pallas_call(
        _nsa_kernel,
        grid=(batch, n_qb),
        in_specs=[qblk, qblk, cblk, cblk, kblk, vblk, kblk, vblk,
                  pl.BlockSpec((LANES, Q_TILE), lambda b, q: (0, b * n_qb + q)), full(ov), full(wbias),
                  full(onehot)],
        out_specs=pl.BlockSpec((Q_TILE, NSA_WIDTH), lambda b, q: (b * n_qb + q, 0)),
        out_shape=jax.ShapeDtypeStruct((tokens, NSA_WIDTH), BF16),
        compiler_params=_params("arbitrary", "arbitrary"),
        name="nsa_attention",
    )(qu, qr, kc, vc, ks, vs, kw, vw, ng, ov, wbias, onehot)


def _outproj_kernel(yr_ref, yc_ref, yn_ref, x_ref, mod_ref, w_ref, lng_ref, lnb_ref, rwh_ref, rwl_ref, rb_ref,
                    x1_ref, h2a_ref, h2b_ref, ti_ref, tw_ref, cnt_ref, *, alpha):
    tm = x_ref.shape[0]
    mix = (_dot(yr_ref[...], w_ref[0, 0:RET_WIDTH, :])
           + _dot(yc_ref[...], w_ref[0, RET_WIDTH:RET_WIDTH + CONV_WIDTH, :])
           + _dot(yn_ref[...], w_ref[0, RET_WIDTH + CONV_WIDTH:, :]))
    x1 = _layer_norm(alpha * x_ref[...] + (1.0 + mod_ref[2, 0]) * mix) * lng_ref[0, 0] + lnb_ref[0, 0]
    x1_ref[...] = x1
    h2 = _layer_norm(x1) * (1.0 + mod_ref[4, 0]) + mod_ref[3, 0]
    for j, part_ref in enumerate((h2a_ref, h2b_ref)):
        part_ref[...] = _pack_pairs(h2[:, j * 2 * SPLIT_WIDTH:(j + 1) * 2 * SPLIT_WIDTH])
    h_hi = h2.astype(BF16)
    h_lo = (h2 - h_hi.astype(F32)).astype(BF16)
    logits = _dot(h_hi, rwh_ref[0]) + _dot(h_lo, rwh_ref[0]) + _dot(h_hi, rwl_ref[0]) + rb_ref[0]

    lane = lax.broadcasted_iota(jnp.int32, (tm, LANES), 1)
    lanef = lane.astype(F32)
    rest = jnp.where(lane < N_EXPERTS, logits, -jnp.inf)
    vals, idxs = [], []
    for _ in range(TOP_K):
        top = jnp.max(rest, axis=-1, keepdims=True)
        idx = jnp.min(jnp.where(rest == top, lanef, float(LANES)), axis=-1, keepdims=True)
        vals.append(top)
        idxs.append(idx)
        rest = jnp.where(lanef == idx, -jnp.inf, rest)
    exps = [jnp.exp(v - vals[0]) for v in vals]
    inv = 1.0 / functools.reduce(lambda a, b: a + b, exps)
    top_w = jnp.zeros((tm, LANES), F32)
    top_i = jnp.zeros((tm, LANES), F32)
    member = jnp.zeros((tm, LANES), F32)
    for k in range(TOP_K):
        top_w = jnp.where(lane == k, exps[k] * inv, top_w)
        top_i = jnp.where(lane == k, idxs[k], top_i)
        member = member + jnp.where(lanef == idxs[k], 1.0, 0.0)
    tw_ref[...] = top_w
    ti_ref[...] = top_i.astype(jnp.int32)

    @pl.when(pl.program_id(0) == 0)
    def _():
        cnt_ref[...] = jnp.zeros_like(cnt_ref)

    cnt_ref[...] += jnp.broadcast_to(jnp.sum(member, axis=0, keepdims=True), cnt_ref.shape)


def _out_projection(y_ret, y_conv, y_nsa, x, mod_l, w_out, ln_g, ln_b, rw_hi, rw_lo, rb, layer, seq, alpha):
    tokens = x.shape[0]
    tm = ROW_TILE
    tiles_per_seq = seq // tm
    row = lambda w: pl.BlockSpec((tm, w), lambda i: (i, 0))
    lay3 = lambda a: pl.BlockSpec((1,) + a.shape[1:], lambda i: (layer,) + (0,) * (a.ndim - 1))
    return pl.pallas_call(
        functools.partial(_outproj_kernel, alpha=alpha),
        grid=(tokens // tm,),
        in_specs=[row(RET_WIDTH), row(CONV_WIDTH), row(NSA_WIDTH), row(D_MODEL),
                  pl.BlockSpec((6, 1, 1, D_MODEL), lambda i: (0, i // tiles_per_seq, 0, 0)),
                  lay3(w_out),
                  pl.BlockSpec((1, 1, 1, D_MODEL), lambda i: (layer, 0, 0, 0)),
                  pl.BlockSpec((1, 1, 1, D_MODEL), lambda i: (layer, 0, 0, 0)),
                  lay3(rw_hi), lay3(rw_lo), lay3(rb)],
        out_specs=[row(D_MODEL)] + [row(SPLIT_WIDTH)] * SC_SPLIT + [row(LANES), row(LANES),
                   pl.BlockSpec((8, LANES), lambda i: (0, 0))],
        out_shape=[jax.ShapeDtypeStruct((tokens, D_MODEL), F32)]
                  + [jax.ShapeDtypeStruct((tokens, SPLIT_WIDTH), jnp.uint32)] * SC_SPLIT
                  + [jax.ShapeDtypeStruct((tokens, LANES), jnp.int32), jax.ShapeDtypeStruct((tokens, LANES), F32),
                   jax.ShapeDtypeStruct((8, LANES), F32)],
        compiler_params=_params("arbitrary"),
        name="outproj_norm_router",
    )(y_ret, y_conv, y_nsa, x, mod_l, w_out, ln_g, ln_b, rw_hi, rw_lo, rb)


def _route_kernel(ti_ref, pstart_ref, tri_ref, pos_ref, carry_ref):
    @pl.when(pl.program_id(0) == 0)
    def _():
        carry_ref[...] = jnp.zeros_like(carry_ref)

    tm = ti_ref.shape[0]
    lane = lax.broadcasted_iota(jnp.int32, (tm, LANES), 1)
    top_i = ti_ref[...]
    onehots = [lane == top_i[:, k:k + 1] for k in range(TOP_K)]
    member = functools.reduce(lambda a, b: a + b, [jnp.where(o, 1.0, 0.0) for o in onehots])
    base = pstart_ref[...] + carry_ref[0:1] + _dot(tri_ref[...], member.astype(BF16))
    pos = jnp.zeros((tm, LANES), F32)
    for k in range(TOP_K):
        pos_k = jnp.sum(jnp.where(onehots[k], base, 0.0), axis=-1, keepdims=True)
        pos = jnp.where(lane == k, pos_k, pos)
    pos_ref[...] = pos.astype(jnp.int32)
    carry_ref[...] += jnp.broadcast_to(jnp.sum(member, axis=0, keepdims=True), carry_ref.shape)


def _route_positions(top_i, pstart):
    tokens = top_i.shape[0]
    tm = min(ROUTE_TILE, tokens)
    r = np.arange(tm)
    tri = jnp.asarray(r[None, :] < r[:, None], BF16)
    return pl.pallas_call(
        _route_kernel,
        grid=(tokens // tm,),
        in_specs=[pl.BlockSpec((tm, LANES), lambda i: (i, 0)),
                  pl.BlockSpec((1, LANES), lambda i: (0, 0)),
                  pl.BlockSpec((tm, tm), lambda i: (0, 0))],
        out_specs=pl.BlockSpec((tm, LANES), lambda i: (i, 0)),
        out_shape=jax.ShapeDtypeStruct((tokens, LANES), jnp.int32),
        scratch_shapes=[pltpu.VMEM((8, LANES), F32)],
        compiler_params=_params("arbitrary"),
        name="route_positions",
    )(top_i, pstart, tri)


def _block_table(counts, n_blocks):
    tm = MOE_TILE
    cnt = counts[0, :N_EXPERTS].astype(jnp.int32)
    nblk = (cnt + tm - 1) // tm
    bend = jnp.cumsum(nblk)
    bstart = bend - nblk
    blocks = jnp.arange(n_blocks, dtype=jnp.int32)
    block_e = jnp.minimum(jnp.sum((bend[None, :] <= blocks[:, None]).astype(jnp.int32), axis=1), N_EXPERTS - 1)
    n_valid = jnp.clip(cnt[block_e] - (blocks - bstart[block_e]) * tm, 0, tm).astype(jnp.int32)
    n_valid = jnp.where(blocks < bend[-1], n_valid, 0)
    pstart = jnp.pad((bstart * tm).astype(F32), (0, LANES - N_EXPERTS))[None, :]
    first = (blocks < bend[-1]) & ((blocks == 0) | (block_e != jnp.roll(block_e, 1)))
    slot = (jnp.cumsum(first.astype(jnp.int32)) - 1) % 2
    experts = jnp.arange(N_EXPERTS, dtype=jnp.int32)
    later = jnp.where((nblk[None, :] > 0) & (experts[None, :] > experts[:, None]), experts[None, :], N_EXPERTS)
    next_expert = jnp.min(later, axis=1)
    next_expert = jnp.where(next_expert < N_EXPERTS, next_expert, -1)[block_e]
    plan = jnp.stack([first.astype(jnp.int32), slot, next_expert]).astype(jnp.int32)
    return pstart, block_e.astype(jnp.int32), bend[-1:].astype(jnp.int32), n_valid, plan


def _sc_mesh():
    return plsc.VectorSubcoreMesh(core_axis_name="core", subcore_axis_name="subcore")


def _scatter_rows(xs, idx_t, n_rows):
    tokens, width = xs[0].shape
    n_idx = idx_t.shape[0]
    n_x = len(xs)
    win = SC_WINDOW

    @functools.partial(pl.kernel, out_type=[jax.ShapeDtypeStruct((n_rows, width), xs[0].dtype)] * n_x,
                       mesh=_sc_mesh(), scratch_types=[], name="dispatch_scatter")
    def scatter(*refs):
        x_hbms, i_hbm, o_hbms = refs[:n_x], refs[n_x], refs[n_x + 1:]
        idx_specs = [pl.BlockSpec((1, win), functools.partial(lambda i, k: (k, i), k=k)) for k in range(n_idx)]
        for x_hbm, o_hbm in zip(x_hbms, o_hbms):
            def body(x_vmem, *i_vmems, o_hbm=o_hbm):
                for i_vmem in i_vmems:
                    pltpu.sync_copy(x_vmem, o_hbm.at[i_vmem.at[0]])

            pltpu.emit_pipeline(
                body,
                grid=(tokens // win,),
                in_specs=[pl.BlockSpec((win, width), lambda i: (i, 0))] + idx_specs,
                out_specs=[],
                core_axis_name=("core", "subcore"),
                dimension_semantics=(pltpu.PARALLEL,),
            )(x_hbm, *([i_hbm] * n_idx))

    return scatter(*xs, idx_t)


def _gather_rows(xs, idx):
    width = xs[0].shape[1]
    n = idx.shape[0]
    n_x = len(xs)
    win = SC_WINDOW

    @functools.partial(pl.kernel, out_type=[jax.ShapeDtypeStruct((n, width), xs[0].dtype)] * n_x,
                       mesh=_sc_mesh(), scratch_types=[], name="combine_gather")
    def gather(*refs):
        x_hbms, i_hbm, o_hbms = refs[:n_x], refs[n_x], refs[n_x + 1:]
        for x_hbm, o_hbm in zip(x_hbms, o_hbms):
            def body(i_vmem, o_vmem, x_hbm=x_hbm):
                pltpu.sync_copy(x_hbm.at[i_vmem.at[0]], o_vmem)

            pltpu.emit_pipeline(
                body,
                grid=(n // win,),
                in_specs=[pl.BlockSpec((1, win), lambda i: (0, i))],
                out_specs=[pl.BlockSpec((win, width), lambda i: (i, 0))],
                core_axis_name=("core", "subcore"),
                dimension_semantics=(pltpu.PARALLEL,),
            )(i_hbm, o_hbm)

    return gather(*xs, idx.reshape(1, n))


def _moe_kernel(be_ref, nb_ref, nv_ref, plan_ref, ra_ref, rb_ref, wgu_hbm, bgu_ref, wd_hbm, bd_ref,
                oa_ref, ob_ref, wgu_f32, wd_f32, wgu_bf, wd_bf, sem, *, layer):
    row_refs = (ra_ref, rb_ref)
    out_refs = (oa_ref, ob_ref)
    i = pl.program_id(0)
    valid = i < nb_ref[0]

    def weight_copies(expert, slot):
        return (pltpu.make_async_copy(wgu_hbm.at[layer, expert], wgu_f32.at[slot], sem.at[0, slot]),
                pltpu.make_async_copy(wd_hbm.at[layer, expert], wd_f32.at[slot], sem.at[1, slot]))

    @pl.when(i == 0)
    def _():
        for copy in weight_copies(be_ref[0], 0):
            copy.start()

    @pl.when(valid & (plan_ref[0, i] == 1))
    def _():
        slot = plan_ref[1, i]
        for copy in weight_copies(be_ref[i], slot):
            copy.wait()
        upcoming = plan_ref[2, i]

        @pl.when(upcoming >= 0)
        def _():
            for copy in weight_copies(upcoming, 1 - slot):
                copy.start()

        wgu_bf[...] = wgu_f32[slot].astype(BF16)
        wd_bf[...] = wd_f32[slot].astype(BF16)

    @pl.when(valid)
    def _():
        live = lax.broadcasted_iota(jnp.int32, ra_ref.shape, 0) < nv_ref[i]
        gu = bgu_ref[0, 0]
        for j, r_ref in enumerate(row_refs):
            halves = _unpack_pairs(jnp.where(live, r_ref[...], jnp.uint32(0)))
            for k, rows in enumerate(halves):
                w0 = (2 * j + k) * SPLIT_WIDTH
                gu = gu + _dot(rows.astype(BF16), wgu_bf[w0:w0 + SPLIT_WIDTH, :])
        g = jnp.minimum(gu[:, :D_EXPERT], SWIGLU_LIMIT)
        u = jnp.clip(gu[:, D_EXPERT:], -SWIGLU_LIMIT, SWIGLU_LIMIT)
        act = (u + 1.0) * (g * jax.nn.sigmoid(SWIGLU_ALPHA * g))
        y = _dot(act.astype(BF16), wd_bf[...]) + bd_ref[0, 0]
        for j, o_ref in enumerate(out_refs):
            o_ref[...] = _pack_pairs(y[:, j * 2 * SPLIT_WIDTH:(j + 1) * 2 * SPLIT_WIDTH])

    @pl.when(jnp.logical_not(valid))
    def _():
        for o_ref in out_refs:
            o_ref[...] = jnp.zeros_like(o_ref)


def _expert_ffn(rows, block_e, n_used, n_valid, plan, w_gu, b_gu, w_down, b_down, layer):
    n_rows = rows[0].shape[0]
    tm = MOE_TILE
    part = pl.BlockSpec((tm, SPLIT_WIDTH), lambda i, be, nb, nv, pn: (i, 0))
    grid_spec = pltpu.PrefetchScalarGridSpec(
        num_scalar_prefetch=4,
        grid=(n_rows // tm,),
        in_specs=[part] * SC_SPLIT + [
                  pl.BlockSpec(memory_space=pl.ANY),
                  pl.BlockSpec((1, 1, 1, 2 * D_EXPERT), lambda i, be, nb, nv, pn: (layer, be[i], 0, 0)),
                  pl.BlockSpec(memory_space=pl.ANY),
                  pl.BlockSpec((1, 1, 1, D_MODEL), lambda i, be, nb, nv, pn: (layer, be[i], 0, 0))],
        out_specs=[part] * SC_SPLIT,
        scratch_shapes=[pltpu.VMEM((2, D_MODEL, 2 * D_EXPERT), F32), pltpu.VMEM((2, D_EXPERT, D_MODEL), F32),
                        pltpu.VMEM((D_MODEL, 2 * D_EXPERT), BF16), pltpu.VMEM((D_EXPERT, D_MODEL), BF16),
                        pltpu.SemaphoreType.DMA((2, 2))])
    return pl.pallas_call(
        functools.partial(_moe_kernel, layer=layer),
        grid_spec=grid_spec,
        out_shape=[jax.ShapeDtypeStruct((n_rows, SPLIT_WIDTH), jnp.uint32)] * SC_SPLIT,
        compiler_params=_params("arbitrary"),
        name="expert_ffn",
    )(block_e, n_used, n_valid, plan, *rows, w_gu, b_gu, w_down, b_down)


def _final_kernel(x_ref, tw_ref, mod_ref, lng_ref, lnb_ref, *rest, alpha):
    y_refs, o_ref = rest[:-1], rest[-1]
    top_w = tw_ref[...]
    parts = []
    for j in range(SC_SPLIT):
        acc = [0.0, 0.0]
        for k in range(TOP_K):
            for half, y in enumerate(_unpack_pairs(y_refs[j * TOP_K + k][...])):
                acc[half] = acc[half] + top_w[:, k:k + 1] * y
        parts += acc
    ffn = jnp.concatenate(parts, axis=1)
    o_ref[...] = (_layer_norm(alpha * x_ref[...] + (1.0 + mod_ref[5, 0]) * ffn) * lng_ref[0, 0] + lnb_ref[0, 0])


def _final_norm(x1, y_parts, top_w, mod_l, ln_g, ln_b, layer, seq, alpha):
    tokens = x1.shape[0]
    tm = ROW_TILE
    tiles = tokens // tm
    tiles_per_seq = seq // tm
    row = lambda w: pl.BlockSpec((tm, w), lambda i: (i, 0))
    vec = pl.BlockSpec((1, 1, 1, D_MODEL), lambda i: (layer, 1, 0, 0))
    y_specs, y_args = [], []
    for j in range(SC_SPLIT):
        for k in range(TOP_K):
            y_specs.append(pl.BlockSpec((tm, SPLIT_WIDTH), functools.partial(lambda i, k: (k * tiles + i, 0), k=k)))
            y_args.append(y_parts[j])
    return pl.pallas_call(
        functools.partial(_final_kernel, alpha=alpha),
        grid=(tiles,),
        in_specs=[row(D_MODEL), row(LANES),
                  pl.BlockSpec((6, 1, 1, D_MODEL), lambda i: (0, i // tiles_per_seq, 0, 0)), vec, vec] + y_specs,
        out_specs=row(D_MODEL),
        out_shape=jax.ShapeDtypeStruct(x1.shape, F32),
        compiler_params=_params("arbitrary"),
        name="combine_final_norm",
    )(x1, top_w, mod_l, ln_g, ln_b, *y_args)


def kernel(x, c, positions, w_in, w_out, ret_gn_w, conv_w, cmp_pos, cmp_w1, cmp_w2, ada_w, ada_b, ln_g, ln_b,
           router_w, router_b, w_gate_up, b_gate_up, w_down, b_down):
    batch, seq, _ = x.shape
    depth = w_in.shape[0]
    tokens = batch * seq
    n_rows = tokens * TOP_K + N_EXPERTS * MOE_TILE
    alpha = float((2 * depth) ** 0.25)

    w_in_r = jnp.pad(w_in, ((0, 0), (0, 0), (0, N_COLS - w_in.shape[2]))).astype(BF16)
    w_out_b = w_out.astype(BF16)
    rw = jnp.pad(router_w, ((0, 0), (0, 0), (0, LANES - N_EXPERTS)))
    rw_hi = rw.astype(BF16)
    rw_lo = (rw - rw_hi.astype(F32)).astype(BF16)
    rb = jnp.pad(router_b, ((0, 0), (0, LANES - N_EXPERTS))).reshape(depth, 1, LANES)
    cw = _compress_weights(cmp_pos, cmp_w1, cmp_w2)
    ret_consts = _retention_consts()
    nsa_consts = _nsa_consts(seq)
    gn_w = ret_gn_w.reshape(depth, 1, RET_WIDTH)
    ln_g4 = ln_g.reshape(depth, 2, 1, D_MODEL)
    ln_b4 = ln_b.reshape(depth, 2, 1, D_MODEL)
    b_gu4 = b_gate_up.reshape(depth, N_EXPERTS, 1, 2 * D_EXPERT)
    b_d4 = b_down.reshape(depth, N_EXPERTS, 1, D_MODEL)

    mod = _modulation(c, ada_w, ada_b)
    cos, sin = _rope_tables(positions)
    xt = x.reshape(tokens, D_MODEL)
    for l in range(depth):
        (rq, rk, rv, rg, y_conv, qu, qr, kcx, vcx, ks, vs, kw, vw, ng) = _input_projection(
            xt, mod[l], w_in_r, cos, sin, conv_w, l, seq)
        y_ret = _retention(rq, rk, rv, rg, ret_consts, gn_w, l, batch, seq)
        kc, vc = _compress(kcx, vcx, cw, l, batch, seq)
        y_nsa = _nsa(qu, qr, kc, vc, ks, vs, kw, vw, ng, nsa_consts, batch, seq)
        x1, *h2, top_i, top_w, counts = _out_projection(y_ret, y_conv, y_nsa, xt, mod[l], w_out_b, ln_g4, ln_b4,
                                                        rw_hi, rw_lo, rb, l, seq, alpha)
        pstart, block_e, n_used, n_valid, plan = _block_table(counts, n_rows // MOE_TILE)
        pos_t = _route_positions(top_i, pstart)[:, :TOP_K].T
        rows = _scatter_rows(h2, pos_t, n_rows)
        y = _expert_ffn(rows, block_e, n_used, n_valid, plan, w_gate_up, b_gu4, w_down, b_d4, l)
        y_tok = _gather_rows(y, pos_t.reshape(-1))
        xt = _final_norm(x1, y_tok, top_w, mod[l], ln_g4, ln_b4, l, seq, alpha)
    return xt.reshape(batch, seq, D_MODEL)
```

```python
import functools

import numpy as np
import jax
import jax.numpy as jnp
from jax import lax
from jax.experimental import pallas as pl
from jax.experimental.pallas import tpu as pltpu
from jax.experimental.pallas import tpu_sc as plsc

F32 = jnp.float32
BF16 = jnp.bfloat16

D_MODEL = 1024
HEAD_DIM = 64
RET_WIDTH = 256
RET_HEADS = 4
RET_CHUNK = 128
CONV_WIDTH = 256
CONV_K = 3
NSA_WIDTH = 512
NSA_HEADS = 8
NSA_KV_HEADS = 2
NSA_GROUP = 4
NSA_KV_WIDTH = 128
CMP_LEN = 32
CMP_STRIDE = 16
CMP_HIDDEN = 128
SEL_LEN = 64
SEL_TOP = 8
WINDOW = 512
Q_BLOCK = 128
Q_TILE = 2 * Q_BLOCK
ROPE_THETA = 10000.0
N_EXPERTS = 32
TOP_K = 4
D_EXPERT = 1024
SWIGLU_LIMIT = 7.0
SWIGLU_ALPHA = 1.702
LN_EPS = 1e-5
NEG_INF = -1e30
FORCE_SCORE = 1e9

LANES = 128
VMEM_LIMIT = 56 * 1024 * 1024

OFF_RQ, OFF_RK, OFF_RV, OFF_RG = 0, 256, 512, 768
OFF_CB, OFF_CC, OFF_CH = 1024, 1280, 1536
OFF_NQ = 1792
OFF_KC, OFF_VC = 2304, 2432
OFF_KS, OFF_VS, OFF_KW, OFF_VW = 2560, 2688, 2816, 2944
OFF_NG = 3072
N_GATE = NSA_HEADS * 3
N_COLS = OFF_NG + LANES
LOG2E = 1.4426950408889634
WT_NQ, WT_VS, WT_VW, WT_NG = 0, NSA_WIDTH, NSA_WIDTH + NSA_KV_WIDTH, NSA_WIDTH + 2 * NSA_KV_WIDTH
WT_ROWS = WT_NG + LANES

RET_BATCH = 4
ROW_TILE = 256
MOE_TILE = 256
FFN_CHUNK = 256
ROUTE_TILE = 1024
SEL_GROUP = 4
V_ROWS = HEAD_DIM + 16
SC_WINDOW = LANES
SC_SPLIT = 2
SPLIT_WIDTH = D_MODEL // (2 * SC_SPLIT)


def _dot(a, b):
    return jnp.dot(a, b, preferred_element_type=F32)


def _dot_nt(a, b):
    return lax.dot_general(a, b, (((1,), (1,)), ((), ())), preferred_element_type=F32)


def _layer_norm(x):
    mu = jnp.mean(x, axis=-1, keepdims=True)
    xc = x - mu
    var = jnp.mean(xc * xc, axis=-1, keepdims=True)
    return xc * lax.rsqrt(var + LN_EPS)


def _pack_pairs(x):
    bits = lambda t: lax.bitcast_convert_type(t.astype(BF16).astype(F32), jnp.uint32)
    return (bits(x[:, SPLIT_WIDTH:]) & jnp.uint32(0xFFFF0000)) | (bits(x[:, :SPLIT_WIDTH]) >> 16)


def _unpack_pairs(p):
    low = lax.bitcast_convert_type(p << 16, F32)
    high = lax.bitcast_convert_type(p & jnp.uint32(0xFFFF0000), F32)
    return low, high


def _params(*sem):
    return pltpu.CompilerParams(dimension_semantics=sem, vmem_limit_bytes=VMEM_LIMIT)


def _mod_kernel(c_ref, w_ref, b_ref, o_ref):
    c = c_ref[...]
    ca = (c * jax.nn.sigmoid(c)).astype(BF16)
    o_ref[0, 0] = _dot(ca, w_ref[0].astype(BF16)) + b_ref[0]


def _modulation(c, ada_w, ada_b):
    depth = ada_w.shape[0]
    batch = c.shape[0]
    out = pl.pallas_call(
        _mod_kernel,
        grid=(depth, 6),
        in_specs=[pl.BlockSpec((batch, D_MODEL), lambda l, j: (0, 0)),
                  pl.BlockSpec((1, D_MODEL, D_MODEL), lambda l, j: (l, 0, j)),
                  pl.BlockSpec((1, 1, D_MODEL), lambda l, j: (l * 6 + j, 0, 0))],
        out_specs=pl.BlockSpec((1, 1, batch, D_MODEL), lambda l, j: (l, j, 0, 0)),
        out_shape=jax.ShapeDtypeStruct((depth, 6, batch, D_MODEL), F32),
        compiler_params=_params("arbitrary", "arbitrary"),
        name="adaln_mod",
    )(c, ada_w, ada_b.reshape(depth * 6, 1, D_MODEL))
    return out.reshape(depth, 6, batch, 1, D_MODEL)


def _rope_table_kernel(pos_ref, inv_ref, cos_ref, sin_ref):
    ang = pos_ref[...] * inv_ref[...]
    cos_ref[...] = jnp.cos(ang)
    sin_ref[...] = jnp.sin(ang)


def _rope_tables(positions):
    half = HEAD_DIM // 2
    per_row = LANES // half
    tokens = positions.size
    pos4 = jnp.repeat(positions.reshape(tokens // per_row, per_row).astype(F32), half, axis=1)
    inv = ROPE_THETA ** (-jnp.arange(half, dtype=F32) / half)
    inv4 = jnp.tile(inv, per_row)[None, :]
    rows = tokens // per_row
    tile = min(rows, 1024)
    cos4, sin4 = pl.pallas_call(
        _rope_table_kernel,
        grid=(rows // tile,),
        in_specs=[pl.BlockSpec((tile, LANES), lambda i: (i, 0)),
                  pl.BlockSpec((1, LANES), lambda i: (0, 0))],
        out_specs=[pl.BlockSpec((tile, LANES), lambda i: (i, 0))] * 2,
        out_shape=[jax.ShapeDtypeStruct((rows, LANES), F32)] * 2,
        compiler_params=_params("arbitrary"),
        name="rope_tables",
    )(pos4, inv4)
    cos32 = cos4.reshape(tokens, half)
    sin32 = sin4.reshape(tokens, half)
    cos = jnp.tile(cos32, (1, per_row))
    sign = jnp.tile(jnp.concatenate([-jnp.ones((half,), F32), jnp.ones((half,), F32)]), LANES // HEAD_DIM)
    sin = jnp.tile(sin32, (1, per_row)) * sign[None, :]
    return cos, sin, cos32.T, sin32.T


def _inproj_kernel(x_ref, mod_ref, w_ref, wt_ref, cos_ref, sin_ref, cost_ref, sint_ref, convw_ref,
                   rq_ref, rk_ref, rv_ref, rg_ref, yc_ref, qu_ref, qr_ref, kc_ref, vc_ref,
                   ks_ref, vs_ref, kw_ref, vw_ref, ng_ref, carry_ref, *, tiles_per_seq):
    i = pl.program_id(0)
    tm = x_ref.shape[0]
    h = (_layer_norm(x_ref[...]) * (1.0 + mod_ref[1, 0]) + mod_ref[0, 0]).astype(BF16)
    cosf = cos_ref[...]
    sinf = sin_ref[...]
    lane = lax.broadcasted_iota(jnp.int32, (tm, LANES), 1)
    first_half = (lane % HEAD_DIM) < (HEAD_DIM // 2)

    def proj(off, width):
        return _dot(h, w_ref[0, :, off:off + width])

    def rope(c):
        cols = []
        for j in range(c.shape[1] // LANES):
            cj = c[:, j * LANES:(j + 1) * LANES]
            swapped = jnp.where(first_half, pltpu.roll(cj, LANES - HEAD_DIM // 2, 1),
                                pltpu.roll(cj, HEAD_DIM // 2, 1))
            cols.append(cj * cosf + swapped * sinf)
        return jnp.concatenate(cols, axis=1) if len(cols) > 1 else cols[0]

    scale = HEAD_DIM ** -0.5
    rq_ref[...] = rope(proj(OFF_RQ, RET_WIDTH)).astype(BF16)
    rk_ref[...] = (rope(proj(OFF_RK, RET_WIDTH)) * scale).astype(BF16)
    rv_ref[...] = proj(OFF_RV, RET_WIDTH).astype(BF16)
    rg = proj(OFF_RG, RET_WIDTH)
    rg_ref[...] = (rg * jax.nn.sigmoid(rg)).astype(BF16)

    cb = proj(OFF_CB, CONV_WIDTH)
    u = proj(OFF_CC, CONV_WIDTH) * proj(OFF_CH, CONV_WIDTH)

    @pl.when(i % tiles_per_seq == 0)
    def _():
        carry_ref[...] = jnp.zeros_like(carry_ref)

    carry = carry_ref[...]
    row = lax.broadcasted_iota(jnp.int32, (tm, CONV_WIDTH), 0)
    prev1 = jnp.where(row == 0, carry[7:8], pltpu.roll(u, 1, 0))
    prev2 = jnp.where(row == 0, carry[6:7], jnp.where(row == 1, carry[7:8], pltpu.roll(u, 2, 0)))
    cw = convw_ref[0]
    yc_ref[...] = (cb * (cw[0:1] * prev2 + cw[1:2] * prev1 + cw[2:3] * u)).astype(BF16)
    carry_ref[...] = u[tm - 8:tm]

    def proj_t(off, width):
        return _dot_nt(wt_ref[0, off:off + width, :], h)

    nq_t = proj_t(WT_NQ, NSA_WIDTH) * (scale * LOG2E)
    qu_ref[...] = nq_t.astype(BF16)
    cos_t = cost_ref[...]
    sin_t = sint_ref[...]
    half = HEAD_DIM // 2
    rotated = []
    for hq in range(NSA_HEADS):
        t1 = nq_t[hq * HEAD_DIM:hq * HEAD_DIM + half]
        t2 = nq_t[hq * HEAD_DIM + half:(hq + 1) * HEAD_DIM]
        rotated += [t1 * cos_t - t2 * sin_t, t2 * cos_t + t1 * sin_t]
    qr_ref[...] = jnp.concatenate(rotated, axis=0).astype(BF16)
    kc_ref[...] = proj(OFF_KC, NSA_KV_WIDTH)
    vc_ref[...] = proj(OFF_VC, NSA_KV_WIDTH)

    def values_t(off):
        v_t = proj_t(off, NSA_KV_WIDTH)
        ones = jnp.ones((V_ROWS - HEAD_DIM, tm), F32)
        parts = []
        for hd in range(NSA_KV_HEADS):
            parts += [v_t[hd * HEAD_DIM:(hd + 1) * HEAD_DIM], ones]
        return jnp.concatenate(parts, axis=0).astype(BF16)

    ks_ref[...] = rope(proj(OFF_KS, NSA_KV_WIDTH)).astype(BF16)
    vs_ref[...] = values_t(WT_VS)
    kw_ref[...] = rope(proj(OFF_KW, NSA_KV_WIDTH)).astype(BF16)
    vw_ref[...] = values_t(WT_VW)
    ng_ref[...] = proj_t(WT_NG, LANES)


def _input_projection(x, mod_l, w_l, w_t, cos, sin, cos_t, sin_t, conv_w, layer, seq):
    tokens = x.shape[0]
    tm = ROW_TILE
    tiles_per_seq = seq // tm
    row = lambda w: pl.BlockSpec((tm, w), lambda i: (i, 0))
    col = lambda w: pl.BlockSpec((w, tm), lambda i: (0, i))
    outs = [(RET_WIDTH, BF16, False)] * 4 + [(CONV_WIDTH, BF16, False), (NSA_WIDTH, BF16, True),
                                            (NSA_WIDTH, BF16, True), (NSA_KV_WIDTH, F32, False),
                                            (NSA_KV_WIDTH, F32, False), (NSA_KV_WIDTH, BF16, False),
                                            (NSA_KV_HEADS * V_ROWS, BF16, True), (NSA_KV_WIDTH, BF16, False),
                                            (NSA_KV_HEADS * V_ROWS, BF16, True), (LANES, F32, True)]
    return pl.pallas_call(
        functools.partial(_inproj_kernel, tiles_per_seq=tiles_per_seq),
        grid=(tokens // tm,),
        in_specs=[row(D_MODEL),
                  pl.BlockSpec((6, 1, 1, D_MODEL), lambda i: (0, i // tiles_per_seq, 0, 0)),
                  pl.BlockSpec((1, D_MODEL, N_COLS), lambda i: (layer, 0, 0)),
                  pl.BlockSpec((1, WT_ROWS, D_MODEL), lambda i: (layer, 0, 0)),
                  row(LANES), row(LANES), col(HEAD_DIM // 2), col(HEAD_DIM // 2),
                  pl.BlockSpec((1, CONV_K, CONV_WIDTH), lambda i: (layer, 0, 0))],
        out_specs=[col(w) if t else row(w) for w, _, t in outs],
        out_shape=[jax.ShapeDtypeStruct((w, tokens) if t else (tokens, w), dt) for w, dt, t in outs],
        scratch_shapes=[pltpu.VMEM((8, CONV_WIDTH), F32)],
        compiler_params=_params("arbitrary"),
        name="ln_inproj",
    )(x, mod_l, w_l, w_t, cos, sin, cos_t, sin_t, conv_w)


def _retention_kernel(q_ref, k_ref, v_ref, g_ref, intra_ref, qdec_ref, kdec_ref, cdec_ref, gn_ref,
                      o_ref, state_ref):
    @pl.when(pl.program_id(1) == 0)
    def _():
        state_ref[...] = jnp.zeros_like(state_ref)

    n_pairs = RET_WIDTH // LANES
    lane = lax.broadcasted_iota(jnp.int32, (RET_CHUNK, LANES), 1)
    low = lane < HEAD_DIM
    blk_r = lax.broadcasted_iota(jnp.int32, (LANES, LANES), 0) < HEAD_DIM
    blk_c = lax.broadcasted_iota(jnp.int32, (LANES, LANES), 1) < HEAD_DIM
    same_head = blk_r == blk_c
    states = {(b, p): state_ref[b, p] for b in range(q_ref.shape[0]) for p in range(n_pairs)}
    for ch, b in [(ch, b) for ch in range(q_ref.shape[1] // RET_CHUNK) for b in range(q_ref.shape[0])]:
        rows = slice(ch * RET_CHUNK, (ch + 1) * RET_CHUNK)
        outs = []
        for p in range(n_pairs):
            cols = slice(p * LANES, (p + 1) * LANES)
            qp = q_ref[b, rows, cols]
            kp = k_ref[b, rows, cols]
            vp = v_ref[b, rows, cols]
            out = jnp.zeros((RET_CHUNK, LANES), F32)
            for hh in range(2):
                keep = low if hh == 0 else jnp.logical_not(low)
                qm = jnp.where(keep, qp, jnp.zeros_like(qp))
                s = _dot_nt(qm, kp) * intra_ref[2 * p + hh]
                out = jnp.where(keep, _dot(s.astype(BF16), vp), out)
            state = states[b, p]
            qd = (qp.astype(F32) * qdec_ref[:, cols]).astype(BF16)
            out = out + _dot(qd, state.astype(BF16))
            kd_t = (kp.astype(F32) * kdec_ref[:, cols]).T.astype(BF16)
            upd = _dot(kd_t, vp)
            states[b, p] = state * cdec_ref[p] + jnp.where(same_head, upd, 0.0)
            def half_mean(t):
                lo_sum = jnp.sum(jnp.where(low, t, 0.0), axis=-1, keepdims=True)
                hi_sum = jnp.sum(jnp.where(low, 0.0, t), axis=-1, keepdims=True)
                return jnp.where(low, lo_sum, hi_sum) * (1.0 / HEAD_DIM)
            oc = out - half_mean(out)
            outs.append(oc * lax.rsqrt(half_mean(oc * oc) + LN_EPS))
        normed = jnp.concatenate(outs, axis=1)
        o_ref[b, rows, :] = (normed * gn_ref[0] * g_ref[b, rows, :].astype(F32)).astype(BF16)
    for (b, p), state in states.items():
        state_ref[b, p] = state


def _retention_consts():
    heads = jnp.arange(RET_HEADS, dtype=F32)
    log_gamma = jnp.log(1.0 - jnp.power(2.0, -5.0 - heads))
    i = jnp.arange(RET_CHUNK, dtype=F32)
    diff = i[:, None] - i[None, :]
    intra = jnp.where(diff >= 0, jnp.exp(diff * log_gamma[:, None, None]), 0.0)
    q_dec = jnp.exp((i + 1.0) * log_gamma[:, None])
    k_dec = jnp.exp((RET_CHUNK - 1.0 - i) * log_gamma[:, None])
    c_dec = jnp.exp(RET_CHUNK * log_gamma)
    expand = lambda t: jnp.repeat(t.T, HEAD_DIM, axis=1)
    c_rows = jnp.repeat(c_dec, HEAD_DIM).reshape(RET_WIDTH // LANES, LANES, 1)
    c_blk = jnp.broadcast_to(c_rows, (RET_WIDTH // LANES, LANES, LANES))
    return intra, expand(q_dec), expand(k_dec), c_blk


def _retention(rq, rk, rv, rg, consts, gn_w, layer, batch, seq):
    intra, q_dec, k_dec, c_blk = consts
    rows = 2 * RET_CHUNK
    steps = seq // rows
    group = RET_BATCH if batch % RET_BATCH == 0 else 1
    blk = pl.BlockSpec((group, rows, RET_WIDTH), lambda b, s: (b, s, 0))
    full = lambda a: pl.BlockSpec(a.shape, lambda b, s: (0,) * a.ndim)
    per_seq = lambda t: t.reshape(batch, seq, RET_WIDTH)
    out = pl.pallas_call(
        _retention_kernel,
        grid=(batch // group, steps),
        in_specs=[blk, blk, blk, blk, full(intra), full(q_dec), full(k_dec), full(c_blk),
                  pl.BlockSpec((1, 1, RET_WIDTH), lambda b, s: (layer, 0, 0))],
        out_specs=blk,
        out_shape=jax.ShapeDtypeStruct((batch, seq, RET_WIDTH), BF16),
        scratch_shapes=[pltpu.VMEM((group, RET_WIDTH // LANES, LANES, LANES), F32)],
        compiler_params=_params("arbitrary", "arbitrary"),
        name="retention",
    )(per_seq(rq), per_seq(rk), per_seq(rv), per_seq(rg), intra, q_dec, k_dec, c_blk, gn_w)
    return out.reshape(batch * seq, RET_WIDTH)


def _compress_kernel(xk_ref, xv_ref, posa_ref, posb_ref, w1a_ref, w1b_ref, w2_ref, kc_ref, vc_ref):
    n_grp = xk_ref.shape[0] // CMP_STRIDE
    for kv, (x_ref, o_ref) in enumerate(((xk_ref, kc_ref), (xv_ref, vc_ref))):
        ya = jnp.zeros((n_grp, NSA_KV_HEADS * CMP_HIDDEN), F32)
        yb = jnp.zeros((n_grp, NSA_KV_HEADS * CMP_HIDDEN), F32)
        for r in range(CMP_STRIDE):
            x = x_ref[pl.ds(r, n_grp, stride=CMP_STRIDE), :]
            cols = slice(r * NSA_KV_WIDTH, (r + 1) * NSA_KV_WIDTH)
            ya = ya + _dot((x + posa_ref[0, kv, :, cols]).astype(BF16), w1a_ref[0, kv, cols, :])
            yb = yb + _dot((x + posb_ref[0, kv, :, cols]).astype(BF16), w1b_ref[0, kv, cols, :])
        hidden = ya + pltpu.roll(yb, n_grp - 1, 0)
        act = jax.nn.gelu(hidden)
        out = _dot(act.astype(BF16), w2_ref[0, kv])
        o_ref[0] = (out if kv == 0 else out.T).astype(BF16)


def _compress_weights(cmp_pos, cmp_w1, cmp_w2):
    depth = cmp_w1.shape[0]
    eye = jnp.eye(NSA_KV_HEADS, dtype=F32)
    a = cmp_w1.reshape(depth, 2, 2, CMP_STRIDE, HEAD_DIM, CMP_HIDDEN)
    w1 = jnp.einsum('lkardj,hg->lkarhdgj', a, eye)
    w1 = w1.reshape(depth, 2, 2, CMP_STRIDE * NSA_KV_WIDTH, NSA_KV_HEADS * CMP_HIDDEN).astype(BF16)
    pos = cmp_pos.reshape(depth, 2, 2, CMP_STRIDE, 1, HEAD_DIM)
    pos = jnp.broadcast_to(pos, (depth, 2, 2, CMP_STRIDE, NSA_KV_HEADS, HEAD_DIM))
    pos = pos.reshape(depth, 2, 2, 1, CMP_STRIDE * NSA_KV_WIDTH)
    w2 = jnp.einsum('lkje,hg->lkhjge', cmp_w2, eye)
    w2 = w2.reshape(depth, 2, NSA_KV_HEADS * CMP_HIDDEN, NSA_KV_WIDTH).astype(BF16)
    return w1[:, :, 0], w1[:, :, 1], pos[:, :, 0], pos[:, :, 1], w2


def _compress(kcx, vcx, cw, layer, batch, seq):
    w1a, w1b, posa, posb, w2 = cw
    n_grp = seq // CMP_STRIDE
    xblk = pl.BlockSpec((seq, NSA_KV_WIDTH), lambda b: (b, 0))
    lay = lambda a: pl.BlockSpec((1,) + a.shape[1:], lambda b: (layer,) + (0,) * (a.ndim - 1))
    oblk = pl.BlockSpec((1, n_grp, NSA_KV_WIDTH), lambda b: (b, 0, 0))
    return pl.pallas_call(
        _compress_kernel,
        grid=(batch,),
        in_specs=[xblk, xblk, lay(posa), lay(posb), lay(w1a), lay(w1b), lay(w2)],
        out_specs=[oblk, oblk],
        out_shape=[jax.ShapeDtypeStruct((batch, n_grp, NSA_KV_WIDTH), BF16)] * 2,
        compiler_params=_params("arbitrary"),
        name="nsa_compress",
    )(kcx, vcx, posa, posb, w1a, w1b, w2)


def _nsa_kernel(qu_ref, qr_ref, kc_ref, vc_ref, ks_ref, vs_ref, kw_ref, vw_ref, ng_ref, ov_ref, wbias_ref, ind_ref,
                o_ref):
    qb = pl.program_id(1)
    q0 = qb * Q_TILE
    qb_last = (qb + 1) * (Q_TILE // Q_BLOCK) - 1
    n_sel = ov_ref.shape[0]
    cols4 = NSA_GROUP * Q_TILE
    n_past = WINDOW // Q_BLOCK
    gsig = jax.nn.sigmoid(ng_ref[...])

    def padded_q(ref, h):
        zeros = jnp.zeros((HEAD_DIM, Q_TILE), BF16)
        parts = []
        for g in range(NSA_GROUP):
            hq = h * NSA_GROUP + g
            x = ref[hq * HEAD_DIM:(hq + 1) * HEAD_DIM, :]
            parts.append(jnp.concatenate([x, zeros] if h == 0 else [zeros, x], axis=0))
        return jnp.concatenate(parts, axis=1)

    def tile4(t):
        return jnp.concatenate([t] * NSA_GROUP, axis=1)

    def attend(state, keys, values_t, bias, q_t):
        m, acc = state
        s = _dot(keys, q_t)
        if bias is not None:
            s = s + bias
        m_new = jnp.maximum(m, jnp.max(s, axis=0, keepdims=True))
        e = jnp.exp2(s - m_new)
        acc = jnp.exp2(m - m_new) * acc + _dot(values_t, e.astype(BF16))
        return m_new, acc

    def normalised(state):
        acc = state[1]
        return acc[0:HEAD_DIM] * (1.0 / acc[HEAD_DIM:HEAD_DIM + 1])

    init = (jnp.full((1, cols4), NEG_INF, F32), jnp.zeros((V_ROWS, cols4), F32))
    head_rows = [slice(h * V_ROWS, (h + 1) * V_ROWS) for h in range(NSA_KV_HEADS)]
    q_rot = [padded_q(qr_ref, h) for h in range(NSA_KV_HEADS)]

    def position_bias(kb, windowed):
        first = qb * (Q_TILE // Q_BLOCK)
        return tile4(jnp.concatenate(
            [wbias_ref[jnp.clip(kb - (first + sub) + n_past + 1, 0 if windowed else 2, n_past + 2)]
             for sub in range(Q_TILE // Q_BLOCK)], axis=1))

    o_cmp = []
    q_sel = []
    for h in range(NSA_KV_HEADS):
        hr = slice(h * HEAD_DIM, (h + 1) * HEAD_DIM)
        qus = padded_q(qu_ref, h)

        s = _dot(kc_ref[0], qus)
        cid = lax.broadcasted_iota(jnp.int32, (LANES, Q_TILE), 0)
        tq = q0 + lax.broadcasted_iota(jnp.int32, (LANES, Q_TILE), 1)
        cvalid = tile4(jnp.where(cid * CMP_STRIDE + (CMP_LEN - 1) <= tq, 1.0, 0.0))
        sm = jnp.where(cvalid > 0.5, s, NEG_INF)
        e = jnp.exp2(sm - jnp.max(sm, axis=0, keepdims=True)) * cvalid
        l = jnp.sum(e, axis=0, keepdims=True)
        p = e * (1.0 / jnp.where(l > 0.0, l, 1.0))
        o_cmp.append(_dot(vc_ref[0, hr, :], p.astype(BF16)))

        psum = p[:, 0:Q_TILE]
        for g in range(1, NSA_GROUP):
            psum = psum + p[:, g * Q_TILE:(g + 1) * Q_TILE]
        p_hi = psum.astype(BF16)
        rem = psum - p_hi.astype(F32)
        p_mid = rem.astype(BF16)
        p_lo = (rem - p_mid.astype(F32)).astype(BF16)
        ov = ov_ref[...]
        imp = _dot(ov, p_hi) + _dot(ov, p_mid) + _dot(ov, p_lo)
        jid = lax.broadcasted_iota(jnp.int32, (n_sel, Q_TILE), 0)
        tid = q0 + lax.broadcasted_iota(jnp.int32, (n_sel, Q_TILE), 1)
        forced = (jid == 0) | (jid == jnp.right_shift(tid, 6))
        imp = jnp.where(forced, FORCE_SCORE, jnp.where(jid * SEL_LEN <= tid, imp, -FORCE_SCORE))
        rank = jnp.zeros((n_sel, Q_TILE), F32)
        for i in range(n_sel):
            ri = imp[i:i + 1, :]
            beats = (ri > imp) | ((ri == imp) & (jid > i))
            rank = rank + jnp.where(beats, 1.0, 0.0)
        sel_bias = tile4(jnp.where(rank < float(SEL_TOP), 0.0, NEG_INF)).astype(BF16)
        pad = jnp.zeros((LANES - n_sel, cols4), BF16)
        q_sel.append(jnp.concatenate([q_rot[h], sel_bias, pad], axis=0))

    def sel_step(gi, states, last):
        k0 = pl.multiple_of(gi * (SEL_GROUP * Q_BLOCK), SEL_GROUP * Q_BLOCK)
        keys = jnp.concatenate([ks_ref[pl.ds(k0, SEL_GROUP * Q_BLOCK), :], ind_ref[gi]], axis=1)
        bias = None
        if last:
            bias = jnp.concatenate([position_bias(gi * SEL_GROUP + u, False) for u in range(SEL_GROUP)], axis=0)
        return tuple(attend(states[h], keys, vs_ref[head_rows[h], pl.ds(k0, SEL_GROUP * Q_BLOCK)], bias, q_sel[h])
                     for h in range(NSA_KV_HEADS))

    last_group = qb_last // SEL_GROUP
    sel_states = lax.fori_loop(0, last_group, functools.partial(sel_step, last=False), (init,) * NSA_KV_HEADS)
    sel_states = sel_step(last_group, sel_states, True)

    ws = pl.multiple_of(jnp.maximum(q0 - WINDOW, 0), Q_BLOCK)
    n_win = n_past + Q_TILE // Q_BLOCK
    win_bias = jnp.concatenate([position_bias(ws // Q_BLOCK + u, True) for u in range(n_win)], axis=0)
    win_keys = kw_ref[pl.ds(ws, n_win * Q_BLOCK), :]
    heads = []
    for h in range(NSA_KV_HEADS):
        o_s = normalised(sel_states[h])
        o_w = normalised(attend(init, win_keys, vw_ref[head_rows[h], pl.ds(ws, n_win * Q_BLOCK)], win_bias, q_rot[h]))
        o_c = o_cmp[h]
        for g in range(NSA_GROUP):
            hq = h * NSA_GROUP + g
            c = slice(g * Q_TILE, (g + 1) * Q_TILE)
            heads.append(gsig[3 * hq:3 * hq + 1, :] * o_c[:, c] + gsig[3 * hq + 1:3 * hq + 2, :] * o_s[:, c]
                         + gsig[3 * hq + 2:3 * hq + 3, :] * o_w[:, c])
    cols_out = [jnp.concatenate(heads[2 * m:2 * m + 2], axis=0).T for m in range(NSA_HEADS // 2)]
    o_ref[...] = jnp.concatenate(cols_out, axis=1).astype(BF16)


def _nsa_consts(seq):
    n_cmp_pad = seq // CMP_STRIDE
    n_sel = seq // SEL_LEN
    c = np.arange(n_cmp_pad)
    j = np.arange(n_sel)
    n_cmp = (seq - CMP_LEN) // CMP_STRIDE + 1
    ov = ((c[None, :] * CMP_STRIDE < j[:, None] * SEL_LEN + SEL_LEN) &
          (j[:, None] * SEL_LEN <= c[None, :] * CMP_STRIDE + CMP_LEN - 1) & (c[None, :] < n_cmp))
    kl = np.arange(Q_BLOCK)[:, None]
    tl = np.arange(Q_BLOCK)[None, :]
    n_past = WINDOW // Q_BLOCK
    none = np.zeros((Q_BLOCK, Q_BLOCK), bool)
    valid = [none, kl > tl] + [np.ones((Q_BLOCK, Q_BLOCK), bool)] * (n_past - 1) + [kl <= tl, none]
    wbias = np.where(np.stack(valid), 0.0, NEG_INF).astype(np.float32)
    key_block = np.arange(seq) // SEL_LEN
    onehot = (key_block[:, None] == np.arange(LANES)[None, :]).reshape(-1, SEL_GROUP * Q_BLOCK, LANES)
    return jnp.asarray(ov, BF16), jnp.asarray(wbias), jnp.asarray(onehot, BF16)


def _nsa(qu, qr, kc, vc, ks, vs, kw, vw, ng, consts, batch, seq):
    ov, wbias, onehot = consts
    n_qb = seq // Q_TILE
    tokens = batch * seq
    qblk = pl.BlockSpec((NSA_WIDTH, Q_TILE), lambda b, q: (0, b * n_qb + q))
    cblk = pl.BlockSpec((1, seq // CMP_STRIDE, NSA_KV_WIDTH), lambda b, q: (b, 0, 0))
    kblk = pl.BlockSpec((seq, NSA_KV_WIDTH), lambda b, q: (b, 0))
    vblk = pl.BlockSpec((NSA_KV_HEADS * V_ROWS, seq), lambda b, q: (0, b))
    full = lambda a: pl.BlockSpec(a.shape, lambda b, q: (0,) * a.ndim)
    return pl.pallas_call(
        _nsa_kernel,
        grid=(batch, n_qb),
        in_specs=[qblk, qblk, cblk, cblk, kblk, vblk, kblk, vblk,
                  pl.BlockSpec((LANES, Q_TILE), lambda b, q: (0, b * n_qb + q)), full(ov), full(wbias),
                  full(onehot)],
        out_specs=pl.BlockSpec((Q_TILE, NSA_WIDTH), lambda b, q: (b * n_qb + q, 0)),
        out_shape=jax.ShapeDtypeStruct((tokens, NSA_WIDTH), BF16),
        compiler_params=_params("arbitrary", "arbitrary"),
        name="nsa_attention",
    )(qu, qr, kc, vc, ks, vs, kw, vw, ng, ov, wbias, onehot)


def _outproj_kernel(yr_ref, yc_ref, yn_ref, x_ref, mod_ref, w_ref, lng_ref, lnb_ref, rwh_ref, rwl_ref, rb_ref,
                    x1_ref, h2a_ref, h2b_ref, ti_ref, tw_ref, cnt_ref, *, alpha):
    tm = x_ref.shape[0]
    mix = (_dot(yr_ref[...], w_ref[0, 0:RET_WIDTH, :])
           + _dot(yc_ref[...], w_ref[0, RET_WIDTH:RET_WIDTH + CONV_WIDTH, :])
           + _dot(yn_ref[...], w_ref[0, RET_WIDTH + CONV_WIDTH:, :]))
    x1 = _layer_norm(alpha * x_ref[...] + (1.0 + mod_ref[2, 0]) * mix) * lng_ref[0, 0] + lnb_ref[0, 0]
    x1_ref[...] = x1
    h2 = _layer_norm(x1) * (1.0 + mod_ref[4, 0]) + mod_ref[3, 0]
    for j, part_ref in enumerate((h2a_ref, h2b_ref)):
        part_ref[...] = _pack_pairs(h2[:, j * 2 * SPLIT_WIDTH:(j + 1) * 2 * SPLIT_WIDTH])
    h_hi = h2.astype(BF16)
    h_lo = (h2 - h_hi.astype(F32)).astype(BF16)
    logits = _dot(h_hi, rwh_ref[0]) + _dot(h_lo, rwh_ref[0]) + _dot(h_hi, rwl_ref[0]) + rb_ref[0]

    lane = lax.broadcasted_iota(jnp.int32, (tm, LANES), 1)
    lanef = lane.astype(F32)
    rest = jnp.where(lane < N_EXPERTS, logits, -jnp.inf)
    vals, idxs = [], []
    for _ in range(TOP_K):
        top = jnp.max(rest, axis=-1, keepdims=True)
        idx = jnp.min(jnp.where(rest == top, lanef, float(LANES)), axis=-1, keepdims=True)
        vals.append(top)
        idxs.append(idx)
        rest = jnp.where(lanef == idx, -jnp.inf, rest)
    exps = [jnp.exp(v - vals[0]) for v in vals]
    inv = 1.0 / functools.reduce(lambda a, b: a + b, exps)
    top_w = jnp.zeros((tm, LANES), F32)
    top_i = jnp.zeros((tm, LANES), F32)
    member = jnp.zeros((tm, LANES), F32)
    for k in range(TOP_K):
        top_w = jnp.where(lane == k, exps[k] * inv, top_w)
        top_i = jnp.where(lane == k, idxs[k], top_i)
        member = member + jnp.where(lanef == idxs[k], 1.0, 0.0)
    tw_ref[...] = top_w
    ti_ref[...] = top_i.astype(jnp.int32)

    @pl.when(pl.program_id(0) == 0)
    def _():
        cnt_ref[...] = jnp.zeros_like(cnt_ref)

    cnt_ref[...] += jnp.broadcast_to(jnp.sum(member, axis=0, keepdims=True), cnt_ref.shape)


def _out_projection(y_ret, y_conv, y_nsa, x, mod_l, w_out, ln_g, ln_b, rw_hi, rw_lo, rb, layer, seq, alpha):
    tokens = x.shape[0]
    tm = ROW_TILE
    tiles_per_seq = seq // tm
    row = lambda w: pl.BlockSpec((tm, w), lambda i: (i, 0))
    lay3 = lambda a: pl.BlockSpec((1,) + a.shape[1:], lambda i: (layer,) + (0,) * (a.ndim - 1))
    return pl.pallas_call(
        functools.partial(_outproj_kernel, alpha=alpha),
        grid=(tokens // tm,),
        in_specs=[row(RET_WIDTH), row(CONV_WIDTH), row(NSA_WIDTH), row(D_MODEL),
                  pl.BlockSpec((6, 1, 1, D_MODEL), lambda i: (0, i // tiles_per_seq, 0, 0)),
                  lay3(w_out),
                  pl.BlockSpec((1, 1, 1, D_MODEL), lambda i: (layer, 0, 0, 0)),
                  pl.BlockSpec((1, 1, 1, D_MODEL), lambda i: (layer, 0, 0, 0)),
                  lay3(rw_hi), lay3(rw_lo), lay3(rb)],
        out_specs=[row(D_MODEL)] + [row(SPLIT_WIDTH)] * SC_SPLIT + [row(LANES), row(LANES),
                   pl.BlockSpec((8, LANES), lambda i: (0, 0))],
        out_shape=[jax.ShapeDtypeStruct((tokens, D_MODEL), F32)]
                  + [jax.ShapeDtypeStruct((tokens, SPLIT_WIDTH), jnp.uint32)] * SC_SPLIT
                  + [jax.ShapeDtypeStruct((tokens, LANES), jnp.int32), jax.ShapeDtypeStruct((tokens, LANES), F32),
                   jax.ShapeDtypeStruct((8, LANES), F32)],
        compiler_params=_params("arbitrary"),
        name="outproj_norm_router",
    )(y_ret, y_conv, y_nsa, x, mod_l, w_out, ln_g, ln_b, rw_hi, rw_lo, rb)


def _route_kernel(ti_ref, pstart_ref, tri_ref, pos_ref, carry_ref):
    @pl.when(pl.program_id(0) == 0)
    def _():
        carry_ref[...] = jnp.zeros_like(carry_ref)

    tm = ti_ref.shape[0]
    lane = lax.broadcasted_iota(jnp.int32, (tm, LANES), 1)
    top_i = ti_ref[...]
    onehots = [lane == top_i[:, k:k + 1] for k in range(TOP_K)]
    member = functools.reduce(lambda a, b: a + b, [jnp.where(o, 1.0, 0.0) for o in onehots])
    base = pstart_ref[...] + carry_ref[0:1] + _dot(tri_ref[...], member.astype(BF16))
    pos = jnp.zeros((tm, LANES), F32)
    for k in range(TOP_K):
        pos_k = jnp.sum(jnp.where(onehots[k], base, 0.0), axis=-1, keepdims=True)
        pos = jnp.where(lane == k, pos_k, pos)
    pos_ref[...] = pos.astype(jnp.int32)
    carry_ref[...] += jnp.broadcast_to(jnp.sum(member, axis=0, keepdims=True), carry_ref.shape)


def _route_positions(top_i, pstart):
    tokens = top_i.shape[0]
    tm = min(ROUTE_TILE, tokens)
    r = np.arange(tm)
    tri = jnp.asarray(r[None, :] < r[:, None], BF16)
    return pl.pallas_call(
        _route_kernel,
        grid=(tokens // tm,),
        in_specs=[pl.BlockSpec((tm, LANES), lambda i: (i, 0)),
                  pl.BlockSpec((1, LANES), lambda i: (0, 0)),
                  pl.BlockSpec((tm, tm), lambda i: (0, 0))],
        out_specs=pl.BlockSpec((tm, LANES), lambda i: (i, 0)),
        out_shape=jax.ShapeDtypeStruct((tokens, LANES), jnp.int32),
        scratch_shapes=[pltpu.VMEM((8, LANES), F32)],
        compiler_params=_params("arbitrary"),
        name="route_positions",
    )(top_i, pstart, tri)


def _block_table(counts, n_blocks):
    tm = MOE_TILE
    cnt = counts[0, :N_EXPERTS].astype(jnp.int32)
    nblk = (cnt + tm - 1) // tm
    bend = jnp.cumsum(nblk)
    bstart = bend - nblk
    blocks = jnp.arange(n_blocks, dtype=jnp.int32)
    block_e = jnp.minimum(jnp.sum((bend[None, :] <= blocks[:, None]).astype(jnp.int32), axis=1), N_EXPERTS - 1)
    n_valid = jnp.clip(cnt[block_e] - (blocks - bstart[block_e]) * tm, 0, tm).astype(jnp.int32)
    n_valid = jnp.where(blocks < bend[-1], n_valid, 0)
    pstart = jnp.pad((bstart * tm).astype(F32), (0, LANES - N_EXPERTS))[None, :]
    first = (blocks < bend[-1]) & ((blocks == 0) | (block_e != jnp.roll(block_e, 1)))
    slot = (jnp.cumsum(first.astype(jnp.int32)) - 1) % 2
    experts = jnp.arange(N_EXPERTS, dtype=jnp.int32)
    later = jnp.where((nblk[None, :] > 0) & (experts[None, :] > experts[:, None]), experts[None, :], N_EXPERTS)
    next_expert = jnp.min(later, axis=1)
    next_expert = jnp.where(next_expert < N_EXPERTS, next_expert, -1)[block_e]
    plan = jnp.stack([first.astype(jnp.int32), slot, next_expert]).astype(jnp.int32)
    return pstart, block_e.astype(jnp.int32), bend[-1:].astype(jnp.int32), n_valid, plan


def _sc_mesh():
    return plsc.VectorSubcoreMesh(core_axis_name="core", subcore_axis_name="subcore")


def _scatter_rows(xs, idx_t, n_rows):
    tokens, width = xs[0].shape
    n_idx = idx_t.shape[0]
    n_x = len(xs)
    win = SC_WINDOW

    @functools.partial(pl.kernel, out_type=[jax.ShapeDtypeStruct((n_rows, width), xs[0].dtype)] * n_x,
                       mesh=_sc_mesh(), scratch_types=[], name="dispatch_scatter")
    def scatter(*refs):
        x_hbms, i_hbm, o_hbms = refs[:n_x], refs[n_x], refs[n_x + 1:]
        idx_specs = [pl.BlockSpec((1, win), functools.partial(lambda i, k: (k, i), k=k)) for k in range(n_idx)]
        for x_hbm, o_hbm in zip(x_hbms, o_hbms):
            def body(x_vmem, *i_vmems, o_hbm=o_hbm):
                for i_vmem in i_vmems:
                    pltpu.sync_copy(x_vmem, o_hbm.at[i_vmem.at[0]])

            pltpu.emit_pipeline(
                body,
                grid=(tokens // win,),
                in_specs=[pl.BlockSpec((win, width), lambda i: (i, 0))] + idx_specs,
                out_specs=[],
                core_axis_name=("core", "subcore"),
                dimension_semantics=(pltpu.PARALLEL,),
            )(x_hbm, *([i_hbm] * n_idx))

    return scatter(*xs, idx_t)


def _gather_rows(xs, idx):
    width = xs[0].shape[1]
    n = idx.shape[0]
    n_x = len(xs)
    win = SC_WINDOW

    @functools.partial(pl.kernel, out_type=[jax.ShapeDtypeStruct((n, width), xs[0].dtype)] * n_x,
                       mesh=_sc_mesh(), scratch_types=[], name="combine_gather")
    def gather(*refs):
        x_hbms, i_hbm, o_hbms = refs[:n_x], refs[n_x], refs[n_x + 1:]
        for x_hbm, o_hbm in zip(x_hbms, o_hbms):
            def body(i_vmem, o_vmem, x_hbm=x_hbm):
                pltpu.sync_copy(x_hbm.at[i_vmem.at[0]], o_vmem)

            pltpu.emit_pipeline(
                body,
                grid=(n // win,),
                in_specs=[pl.BlockSpec((1, win), lambda i: (0, i))],
                out_specs=[pl.BlockSpec((win, width), lambda i: (i, 0))],
                core_axis_name=("core", "subcore"),
                dimension_semantics=(pltpu.PARALLEL,),
            )(i_hbm, o_hbm)

    return gather(*xs, idx.reshape(1, n))


def _moe_kernel(be_ref, nb_ref, nv_ref, plan_ref, ra_ref, rb_ref, wgu_hbm, bgu_ref, wd_hbm, bd_ref,
                oa_ref, ob_ref, wgu_f32, wd_f32, wgu_bf, wd_bf, sem, *, layer):
    row_refs = (ra_ref, rb_ref)
    out_refs = (oa_ref, ob_ref)
    i = pl.program_id(0)
    valid = i < nb_ref[0]

    def weight_copies(expert, slot):
        return (pltpu.make_async_copy(wgu_hbm.at[layer, expert], wgu_f32.at[slot], sem.at[0, slot]),
                pltpu.make_async_copy(wd_hbm.at[layer, expert], wd_f32.at[slot], sem.at[1, slot]))

    @pl.when(i == 0)
    def _():
        for copy in weight_copies(be_ref[0], 0):
            copy.start()

    @pl.when(valid & (plan_ref[0, i] == 1))
    def _():
        slot = plan_ref[1, i]
        for copy in weight_copies(be_ref[i], slot):
            copy.wait()
        upcoming = plan_ref[2, i]

        @pl.when(upcoming >= 0)
        def _():
            for copy in weight_copies(upcoming, 1 - slot):
                copy.start()

        wgu_bf[...] = wgu_f32[slot].astype(BF16)
        wd_bf[...] = wd_f32[slot].astype(BF16)

    @pl.when(valid)
    def _():
        live = lax.broadcasted_iota(jnp.int32, ra_ref.shape, 0) < nv_ref[i]
        rows = jnp.concatenate(
            [half.astype(BF16) for r_ref in row_refs
             for half in _unpack_pairs(jnp.where(live, r_ref[...], jnp.uint32(0)))], axis=1)
        acts = []
        for c in range(D_EXPERT // FFN_CHUNK):
            cg = slice(c * FFN_CHUNK, (c + 1) * FFN_CHUNK)
            cu = slice(D_EXPERT + c * FFN_CHUNK, D_EXPERT + (c + 1) * FFN_CHUNK)
            g = jnp.minimum(_dot(rows, wgu_bf[:, cg]) + bgu_ref[0, 0, :, cg], SWIGLU_LIMIT)
            u = jnp.clip(_dot(rows, wgu_bf[:, cu]) + bgu_ref[0, 0, :, cu], -SWIGLU_LIMIT, SWIGLU_LIMIT)
            acts.append(((u + 1.0) * (g * jax.nn.sigmoid(SWIGLU_ALPHA * g))).astype(BF16))
        y = _dot(jnp.concatenate(acts, axis=1), wd_bf[...]) + bd_ref[0, 0]
        for j, o_ref in enumerate(out_refs):
            o_ref[...] = _pack_pairs(y[:, j * 2 * SPLIT_WIDTH:(j + 1) * 2 * SPLIT_WIDTH])

    @pl.when(jnp.logical_not(valid))
    def _():
        for o_ref in out_refs:
            o_ref[...] = jnp.zeros_like(o_ref)


def _expert_ffn(rows, block_e, n_used, n_valid, plan, w_gu, b_gu, w_down, b_down, layer):
    n_rows = rows[0].shape[0]
    tm = MOE_TILE
    part = pl.BlockSpec((tm, SPLIT_WIDTH), lambda i, be, nb, nv, pn: (i, 0))
    grid_spec = pltpu.PrefetchScalarGridSpec(
        num_scalar_prefetch=4,
        grid=(n_rows // tm,),
        in_specs=[part] * SC_SPLIT + [
                  pl.BlockSpec(memory_space=pl.ANY),
                  pl.BlockSpec((1, 1, 1, 2 * D_EXPERT), lambda i, be, nb, nv, pn: (layer, be[i], 0, 0)),
                  pl.BlockSpec(memory_space=pl.ANY),
                  pl.BlockSpec((1, 1, 1, D_MODEL), lambda i, be, nb, nv, pn: (layer, be[i], 0, 0))],
        out_specs=[part] * SC_SPLIT,
        scratch_shapes=[pltpu.VMEM((2, D_MODEL, 2 * D_EXPERT), F32), pltpu.VMEM((2, D_EXPERT, D_MODEL), F32),
                        pltpu.VMEM((D_MODEL, 2 * D_EXPERT), BF16), pltpu.VMEM((D_EXPERT, D_MODEL), BF16),
                        pltpu.SemaphoreType.DMA((2, 2))])
    return pl.pallas_call(
        functools.partial(_moe_kernel, layer=layer),
        grid_spec=grid_spec,
        out_shape=[jax.ShapeDtypeStruct((n_rows, SPLIT_WIDTH), jnp.uint32)] * SC_SPLIT,
        compiler_params=_params("arbitrary"),
        name="expert_ffn",
    )(block_e, n_used, n_valid, plan, *rows, w_gu, b_gu, w_down, b_down)


def _final_kernel(x_ref, tw_ref, mod_ref, lng_ref, lnb_ref, *rest, alpha):
    y_refs, o_ref = rest[:-1], rest[-1]
    top_w = tw_ref[...]
    parts = []
    for j in range(SC_SPLIT):
        acc = [0.0, 0.0]
        for k in range(TOP_K):
            for half, y in enumerate(_unpack_pairs(y_refs[j * TOP_K + k][...])):
                acc[half] = acc[half] + top_w[:, k:k + 1] * y
        parts += acc
    ffn = jnp.concatenate(parts, axis=1)
    o_ref[...] = (_layer_norm(alpha * x_ref[...] + (1.0 + mod_ref[5, 0]) * ffn) * lng_ref[0, 0] + lnb_ref[0, 0])


def _final_norm(x1, y_parts, top_w, mod_l, ln_g, ln_b, layer, seq, alpha):
    tokens = x1.shape[0]
    tm = ROW_TILE
    tiles = tokens // tm
    tiles_per_seq = seq // tm
    row = lambda w: pl.BlockSpec((tm, w), lambda i: (i, 0))
    vec = pl.BlockSpec((1, 1, 1, D_MODEL), lambda i: (layer, 1, 0, 0))
    y_specs, y_args = [], []
    for j in range(SC_SPLIT):
        for k in range(TOP_K):
            y_specs.append(pl.BlockSpec((tm, SPLIT_WIDTH), functools.partial(lambda i, k: (k * tiles + i, 0), k=k)))
            y_args.append(y_parts[j])
    return pl.pallas_call(
        functools.partial(_final_kernel, alpha=alpha),
        grid=(tiles,),
        in_specs=[row(D_MODEL), row(LANES),
                  pl.BlockSpec((6, 1, 1, D_MODEL), lambda i: (0, i // tiles_per_seq, 0, 0)), vec, vec] + y_specs,
        out_specs=row(D_MODEL),
        out_shape=jax.ShapeDtypeStruct(x1.shape, F32),
        compiler_params=_params("arbitrary"),
        name="combine_final_norm",
    )(x1, top_w, mod_l, ln_g, ln_b, *y_args)


def kernel(x, c, positions, w_in, w_out, ret_gn_w, conv_w, cmp_pos, cmp_w1, cmp_w2, ada_w, ada_b, ln_g, ln_b,
           router_w, router_b, w_gate_up, b_gate_up, w_down, b_down):
    batch, seq, _ = x.shape
    depth = w_in.shape[0]
    tokens = batch * seq
    n_rows = tokens * TOP_K + N_EXPERTS * MOE_TILE
    alpha = float((2 * depth) ** 0.25)

    w_in_r = jnp.pad(w_in, ((0, 0), (0, 0), (0, N_COLS - w_in.shape[2]))).astype(BF16)
    w_in_t = jnp.concatenate([w_in_r[:, :, off:off + width] for off, width in
                              ((OFF_NQ, NSA_WIDTH), (OFF_VS, NSA_KV_WIDTH), (OFF_VW, NSA_KV_WIDTH), (OFF_NG, LANES))],
                             axis=2).transpose(0, 2, 1)
    w_out_b = w_out.astype(BF16)
    rw = jnp.pad(router_w, ((0, 0), (0, 0), (0, LANES - N_EXPERTS)))
    rw_hi = rw.astype(BF16)
    rw_lo = (rw - rw_hi.astype(F32)).astype(BF16)
    rb = jnp.pad(router_b, ((0, 0), (0, LANES - N_EXPERTS))).reshape(depth, 1, LANES)
    cw = _compress_weights(cmp_pos, cmp_w1, cmp_w2)
    ret_consts = _retention_consts()
    nsa_consts = _nsa_consts(seq)
    gn_w = ret_gn_w.reshape(depth, 1, RET_WIDTH)
    ln_g4 = ln_g.reshape(depth, 2, 1, D_MODEL)
    ln_b4 = ln_b.reshape(depth, 2, 1, D_MODEL)
    b_gu4 = b_gate_up.reshape(depth, N_EXPERTS, 1, 2 * D_EXPERT)
    b_d4 = b_down.reshape(depth, N_EXPERTS, 1, D_MODEL)

    mod = _modulation(c, ada_w, ada_b)
    cos, sin, cos_t, sin_t = _rope_tables(positions)
    xt = x.reshape(tokens, D_MODEL)
    for l in range(depth):
        (rq, rk, rv, rg, y_conv, qu, qr, kcx, vcx, ks, vs, kw, vw, ng) = _input_projection(
            xt, mod[l], w_in_r, w_in_t, cos, sin, cos_t, sin_t, conv_w, l, seq)
        y_ret = _retention(rq, rk, rv, rg, ret_consts, gn_w, l, batch, seq)
        kc, vc = _compress(kcx, vcx, cw, l, batch, seq)
        y_nsa = _nsa(qu, qr, kc, vc, ks, vs, kw, vw, ng, nsa_consts, batch, seq)
        x1, *h2, top_i, top_w, counts = _out_projection(y_ret, y_conv, y_nsa, xt, mod[l], w_out_b, ln_g4, ln_b4,
                                                        rw_hi, rw_lo, rb, l, seq, alpha)
        pstart, block_e, n_used, n_valid, plan = _block_table(counts, n_rows // MOE_TILE)
        pos_t = _route_positions(top_i, pstart)[:, :TOP_K].T
        rows = _scatter_rows(h2, pos_t, n_rows)
        y = _expert_ffn(rows, block_e, n_used, n_valid, plan, w_gate_up, b_gu4, w_down, b_d4, l)
        y_tok = _gather_rows(y, pos_t.reshape(-1))
        xt = _final_norm(x1, y_tok, top_w, mod[l], ln_g4, ln_b4, l, seq, alpha)
    return xt.reshape(batch, seq, D_MODEL)
```

```python
import functools

import numpy as np
import jax
import jax.numpy as jnp
from jax import lax
from jax.experimental import pallas as pl
from jax.experimental.pallas import tpu as pltpu
from jax.experimental.pallas import tpu_sc as plsc

F32 = jnp.float32
BF16 = jnp.bfloat16

D_MODEL = 1024
HEAD_DIM = 64
RET_WIDTH = 256
RET_HEADS = 4
RET_CHUNK = 128
CONV_WIDTH = 256
CONV_K = 3
NSA_WIDTH = 512
NSA_HEADS = 8
NSA_KV_HEADS = 2
NSA_GROUP = 4
NSA_KV_WIDTH = 128
CMP_LEN = 32
CMP_STRIDE = 16
CMP_HIDDEN = 128
SEL_LEN = 64
SEL_TOP = 8
WINDOW = 512
Q_BLOCK = 128
Q_TILE = 2 * Q_BLOCK
ROPE_THETA = 10000.0
N_EXPERTS = 32
TOP_K = 4
D_EXPERT = 1024
SWIGLU_LIMIT = 7.0
SWIGLU_ALPHA = 1.702
LN_EPS = 1e-5
NEG_INF = -1e30
FORCE_SCORE = 1e9

LANES = 128
VMEM_LIMIT = 56 * 1024 * 1024

OFF_RQ, OFF_RK, OFF_RV, OFF_RG = 0, 256, 512, 768
OFF_CB, OFF_CC, OFF_CH = 1024, 1280, 1536
OFF_NQ = 1792
OFF_KC, OFF_VC = 2304, 2432
OFF_KS, OFF_VS, OFF_KW, OFF_VW = 2560, 2688, 2816, 2944
OFF_NG = 3072
N_GATE = NSA_HEADS * 3
N_COLS = OFF_NG + LANES
LOG2E = 1.4426950408889634
WT_NQ, WT_VS, WT_VW, WT_NG = 0, NSA_WIDTH, NSA_WIDTH + NSA_KV_WIDTH, NSA_WIDTH + 2 * NSA_KV_WIDTH
WT_ROWS = WT_NG + LANES

RET_BATCH = 4
ROW_TILE = 256
MOE_TILE = 256
FFN_CHUNK = 256
ROUTE_TILE = 1024
SEL_GROUP = 4
V_ROWS = HEAD_DIM + 16
SC_WINDOW = LANES
SC_SPLIT = 2
SPLIT_WIDTH = D_MODEL // (2 * SC_SPLIT)


def _dot(a, b):
    return jnp.dot(a, b, preferred_element_type=F32)


def _dot_nt(a, b):
    return lax.dot_general(a, b, (((1,), (1,)), ((), ())), preferred_element_type=F32)


def _layer_norm(x):
    mu = jnp.mean(x, axis=-1, keepdims=True)
    xc = x - mu
    var = jnp.mean(xc * xc, axis=-1, keepdims=True)
    return xc * lax.rsqrt(var + LN_EPS)


def _pack_pairs(x):
    bits = lambda t: lax.bitcast_convert_type(t.astype(BF16).astype(F32), jnp.uint32)
    return (bits(x[:, SPLIT_WIDTH:]) & jnp.uint32(0xFFFF0000)) | (bits(x[:, :SPLIT_WIDTH]) >> 16)


def _unpack_pairs(p):
    low = lax.bitcast_convert_type(p << 16, F32)
    high = lax.bitcast_convert_type(p & jnp.uint32(0xFFFF0000), F32)
    return low, high


def _params(*sem):
    return pltpu.CompilerParams(dimension_semantics=sem, vmem_limit_bytes=VMEM_LIMIT)


def _mod_kernel(c_ref, w_ref, b_ref, o_ref):
    c = c_ref[...]
    ca = (c * jax.nn.sigmoid(c)).astype(BF16)
    o_ref[0, 0] = _dot(ca, w_ref[0].astype(BF16)) + b_ref[0]


def _modulation(c, ada_w, ada_b):
    depth = ada_w.shape[0]
    batch = c.shape[0]
    out = pl.pallas_call(
        _mod_kernel,
        grid=(depth, 6),
        in_specs=[pl.BlockSpec((batch, D_MODEL), lambda l, j: (0, 0)),
                  pl.BlockSpec((1, D_MODEL, D_MODEL), lambda l, j: (l, 0, j)),
                  pl.BlockSpec((1, 1, D_MODEL), lambda l, j: (l * 6 + j, 0, 0))],
        out_specs=pl.BlockSpec((1, 1, batch, D_MODEL), lambda l, j: (l, j, 0, 0)),
        out_shape=jax.ShapeDtypeStruct((depth, 6, batch, D_MODEL), F32),
        compiler_params=_params("arbitrary", "arbitrary"),
        name="adaln_mod",
    )(c, ada_w, ada_b.reshape(depth * 6, 1, D_MODEL))
    return out.reshape(depth, 6, batch, 1, D_MODEL)


def _rope_table_kernel(pos_ref, inv_ref, cos_ref, sin_ref):
    ang = pos_ref[...] * inv_ref[...]
    cos_ref[...] = jnp.cos(ang)
    sin_ref[...] = jnp.sin(ang)


def _rope_tables(positions):
    half = HEAD_DIM // 2
    per_row = LANES // half
    tokens = positions.size
    pos4 = jnp.repeat(positions.reshape(tokens // per_row, per_row).astype(F32), half, axis=1)
    inv = ROPE_THETA ** (-jnp.arange(half, dtype=F32) / half)
    inv4 = jnp.tile(inv, per_row)[None, :]
    rows = tokens // per_row
    tile = min(rows, 1024)
    cos4, sin4 = pl.pallas_call(
        _rope_table_kernel,
        grid=(rows // tile,),
        in_specs=[pl.BlockSpec((tile, LANES), lambda i: (i, 0)),
                  pl.BlockSpec((1, LANES), lambda i: (0, 0))],
        out_specs=[pl.BlockSpec((tile, LANES), lambda i: (i, 0))] * 2,
        out_shape=[jax.ShapeDtypeStruct((rows, LANES), F32)] * 2,
        compiler_params=_params("arbitrary"),
        name="rope_tables",
    )(pos4, inv4)
    cos32 = cos4.reshape(tokens, half)
    sin32 = sin4.reshape(tokens, half)
    cos = jnp.tile(cos32, (1, per_row))
    sign = jnp.tile(jnp.concatenate([-jnp.ones((half,), F32), jnp.ones((half,), F32)]), LANES // HEAD_DIM)
    sin = jnp.tile(sin32, (1, per_row)) * sign[None, :]
    return cos, sin, cos32.T, sin32.T


def _inproj_kernel(x_ref, mod_ref, w_ref, wt_ref, cos_ref, sin_ref, cost_ref, sint_ref, convw_ref,
                   rq_ref, rk_ref, rv_ref, rg_ref, yc_ref, qu_ref, qr_ref, kc_ref, vc_ref,
                   ks_ref, vs_ref, kw_ref, vw_ref, ng_ref, carry_ref, *, tiles_per_seq):
    i = pl.program_id(0)
    tm = x_ref.shape[0]
    h = (_layer_norm(x_ref[...]) * (1.0 + mod_ref[1, 0]) + mod_ref[0, 0]).astype(BF16)
    cosf = cos_ref[...]
    sinf = sin_ref[...]
    lane = lax.broadcasted_iota(jnp.int32, (tm, LANES), 1)
    first_half = (lane % HEAD_DIM) < (HEAD_DIM // 2)

    def proj(off, width):
        return _dot(h, w_ref[0, :, off:off + width])

    def rope(c):
        cols = []
        for j in range(c.shape[1] // LANES):
            cj = c[:, j * LANES:(j + 1) * LANES]
            swapped = jnp.where(first_half, pltpu.roll(cj, LANES - HEAD_DIM // 2, 1),
                                pltpu.roll(cj, HEAD_DIM // 2, 1))
            cols.append(cj * cosf + swapped * sinf)
        return jnp.concatenate(cols, axis=1) if len(cols) > 1 else cols[0]

    scale = HEAD_DIM ** -0.5
    rq_ref[...] = rope(proj(OFF_RQ, RET_WIDTH)).astype(BF16)
    rk_ref[...] = (rope(proj(OFF_RK, RET_WIDTH)) * scale).astype(BF16)
    rv_ref[...] = proj(OFF_RV, RET_WIDTH).astype(BF16)
    rg = proj(OFF_RG, RET_WIDTH)
    rg_ref[...] = (rg * jax.nn.sigmoid(rg)).astype(BF16)

    cb = proj(OFF_CB, CONV_WIDTH)
    u = proj(OFF_CC, CONV_WIDTH) * proj(OFF_CH, CONV_WIDTH)

    @pl.when(i % tiles_per_seq == 0)
    def _():
        carry_ref[...] = jnp.zeros_like(carry_ref)

    carry = carry_ref[...]
    row = lax.broadcasted_iota(jnp.int32, (tm, CONV_WIDTH), 0)
    prev1 = jnp.where(row == 0, carry[7:8], pltpu.roll(u, 1, 0))
    prev2 = jnp.where(row == 0, carry[6:7], jnp.where(row == 1, carry[7:8], pltpu.roll(u, 2, 0)))
    cw = convw_ref[0]
    yc_ref[...] = (cb * (cw[0:1] * prev2 + cw[1:2] * prev1 + cw[2:3] * u)).astype(BF16)
    carry_ref[...] = u[tm - 8:tm]

    def proj_t(off, width):
        return _dot_nt(wt_ref[0, off:off + width, :], h)

    nq_t = proj_t(WT_NQ, NSA_WIDTH) * (scale * LOG2E)
    qu_ref[...] = nq_t.astype(BF16)
    cos_t = cost_ref[...]
    sin_t = sint_ref[...]
    half = HEAD_DIM // 2
    rotated = []
    for hq in range(NSA_HEADS):
        t1 = nq_t[hq * HEAD_DIM:hq * HEAD_DIM + half]
        t2 = nq_t[hq * HEAD_DIM + half:(hq + 1) * HEAD_DIM]
        rotated += [t1 * cos_t - t2 * sin_t, t2 * cos_t + t1 * sin_t]
    qr_ref[...] = jnp.concatenate(rotated, axis=0).astype(BF16)
    kc_ref[...] = proj(OFF_KC, NSA_KV_WIDTH)
    vc_ref[...] = proj(OFF_VC, NSA_KV_WIDTH)

    def values_t(off):
        v_t = proj_t(off, NSA_KV_WIDTH)
        ones = jnp.ones((V_ROWS - HEAD_DIM, tm), F32)
        parts = []
        for hd in range(NSA_KV_HEADS):
            parts += [v_t[hd * HEAD_DIM:(hd + 1) * HEAD_DIM], ones]
        return jnp.concatenate(parts, axis=0).astype(BF16)

    ks_ref[...] = rope(proj(OFF_KS, NSA_KV_WIDTH)).astype(BF16)
    vs_ref[...] = values_t(WT_VS)
    kw_ref[...] = rope(proj(OFF_KW, NSA_KV_WIDTH)).astype(BF16)
    vw_ref[...] = values_t(WT_VW)
    ng_ref[...] = proj_t(WT_NG, LANES)


def _input_projection(x, mod_l, w_l, w_t, cos, sin, cos_t, sin_t, conv_w, layer, seq):
    tokens = x.shape[0]
    tm = ROW_TILE
    tiles_per_seq = seq // tm
    row = lambda w: pl.BlockSpec((tm, w), lambda i: (i, 0))
    col = lambda w: pl.BlockSpec((w, tm), lambda i: (0, i))
    outs = [(RET_WIDTH, BF16, False)] * 4 + [(CONV_WIDTH, BF16, False), (NSA_WIDTH, BF16, True),
                                            (NSA_WIDTH, BF16, True), (NSA_KV_WIDTH, F32, False),
                                            (NSA_KV_WIDTH, F32, False), (NSA_KV_WIDTH, BF16, False),
                                            (NSA_KV_HEADS * V_ROWS, BF16, True), (NSA_KV_WIDTH, BF16, False),
                                            (NSA_KV_HEADS * V_ROWS, BF16, True), (LANES, F32, True)]
    return pl.pallas_call(
        functools.partial(_inproj_kernel, tiles_per_seq=tiles_per_seq),
        grid=(tokens // tm,),
        in_specs=[row(D_MODEL),
                  pl.BlockSpec((6, 1, 1, D_MODEL), lambda i: (0, i // tiles_per_seq, 0, 0)),
                  pl.BlockSpec((1, D_MODEL, N_COLS), lambda i: (layer, 0, 0)),
                  pl.BlockSpec((1, WT_ROWS, D_MODEL), lambda i: (layer, 0, 0)),
                  row(LANES), row(LANES), col(HEAD_DIM // 2), col(HEAD_DIM // 2),
                  pl.BlockSpec((1, CONV_K, CONV_WIDTH), lambda i: (layer, 0, 0))],
        out_specs=[col(w) if t else row(w) for w, _, t in outs],
        out_shape=[jax.ShapeDtypeStruct((w, tokens) if t else (tokens, w), dt) for w, dt, t in outs],
        scratch_shapes=[pltpu.VMEM((8, CONV_WIDTH), F32)],
        compiler_params=_params("arbitrary"),
        name="ln_inproj",
    )(x, mod_l, w_l, w_t, cos, sin, cos_t, sin_t, conv_w)


def _retention_kernel(q_ref, k_ref, v_ref, g_ref, intra_ref, qdec_ref, kdec_ref, cdec_ref, gn_ref,
                      o_ref, state_ref):
    @pl.when(pl.program_id(1) == 0)
    def _():
        state_ref[...] = jnp.zeros_like(state_ref)

    n_pairs = RET_WIDTH // LANES
    lane = lax.broadcasted_iota(jnp.int32, (RET_CHUNK, LANES), 1)
    low = lane < HEAD_DIM
    blk_r = lax.broadcasted_iota(jnp.int32, (LANES, LANES), 0) < HEAD_DIM
    blk_c = lax.broadcasted_iota(jnp.int32, (LANES, LANES), 1) < HEAD_DIM
    same_head = blk_r == blk_c
    states = {(b, p): state_ref[b, p] for b in range(q_ref.shape[0]) for p in range(n_pairs)}
    for ch, b in [(ch, b) for ch in range(q_ref.shape[1] // RET_CHUNK) for b in range(q_ref.shape[0])]:
        rows = slice(ch * RET_CHUNK, (ch + 1) * RET_CHUNK)
        outs = []
        for p in range(n_pairs):
            cols = slice(p * LANES, (p + 1) * LANES)
            qp = q_ref[b, rows, cols]
            kp = k_ref[b, rows, cols]
            vp = v_ref[b, rows, cols]
            out = jnp.zeros((RET_CHUNK, LANES), F32)
            for hh in range(2):
                keep = low if hh == 0 else jnp.logical_not(low)
                qm = jnp.where(keep, qp, jnp.zeros_like(qp))
                s = _dot_nt(qm, kp) * intra_ref[2 * p + hh]
                out = jnp.where(keep, _dot(s.astype(BF16), vp), out)
            state = states[b, p]
            qd = (qp.astype(F32) * qdec_ref[:, cols]).astype(BF16)
            out = out + _dot(qd, state.astype(BF16))
            kd_t = (kp.astype(F32) * kdec_ref[:, cols]).T.astype(BF16)
            upd = _dot(kd_t, vp)
            states[b, p] = state * cdec_ref[p] + jnp.where(same_head, upd, 0.0)
            def half_mean(t):
                lo_sum = jnp.sum(jnp.where(low, t, 0.0), axis=-1, keepdims=True)
                hi_sum = jnp.sum(jnp.where(low, 0.0, t), axis=-1, keepdims=True)
                return jnp.where(low, lo_sum, hi_sum) * (1.0 / HEAD_DIM)
            oc = out - half_mean(out)
            outs.append(oc * lax.rsqrt(half_mean(oc * oc) + LN_EPS))
        normed = jnp.concatenate(outs, axis=1)
        o_ref[b, rows, :] = (normed * gn_ref[0] * g_ref[b, rows, :].astype(F32)).astype(BF16)
    for (b, p), state in states.items():
        state_ref[b, p] = state


def _retention_consts():
    heads = jnp.arange(RET_HEADS, dtype=F32)
    log_gamma = jnp.log(1.0 - jnp.power(2.0, -5.0 - heads))
    i = jnp.arange(RET_CHUNK, dtype=F32)
    diff = i[:, None] - i[None, :]
    intra = jnp.where(diff >= 0, jnp.exp(diff * log_gamma[:, None, None]), 0.0)
    q_dec = jnp.exp((i + 1.0) * log_gamma[:, None])
    k_dec = jnp.exp((RET_CHUNK - 1.0 - i) * log_gamma[:, None])
    c_dec = jnp.exp(RET_CHUNK * log_gamma)
    expand = lambda t: jnp.repeat(t.T, HEAD_DIM, axis=1)
    c_rows = jnp.repeat(c_dec, HEAD_DIM).reshape(RET_WIDTH // LANES, LANES, 1)
    c_blk = jnp.broadcast_to(c_rows, (RET_WIDTH // LANES, LANES, LANES))
    return intra, expand(q_dec), expand(k_dec), c_blk


def _retention(rq, rk, rv, rg, consts, gn_w, layer, batch, seq):
    intra, q_dec, k_dec, c_blk = consts
    rows = 2 * RET_CHUNK
    steps = seq // rows
    group = RET_BATCH if batch % RET_BATCH == 0 else 1
    blk = pl.BlockSpec((group, rows, RET_WIDTH), lambda b, s: (b, s, 0))
    full = lambda a: pl.BlockSpec(a.shape, lambda b, s: (0,) * a.ndim)
    per_seq = lambda t: t.reshape(batch, seq, RET_WIDTH)
    out = pl.pallas_call(
        _retention_kernel,
        grid=(batch // group, steps),
        in_specs=[blk, blk, blk, blk, full(intra), full(q_dec), full(k_dec), full(c_blk),
                  pl.BlockSpec((1, 1, RET_WIDTH), lambda b, s: (layer, 0, 0))],
        out_specs=blk,
        out_shape=jax.ShapeDtypeStruct((batch, seq, RET_WIDTH), BF16),
        scratch_shapes=[pltpu.VMEM((group, RET_WIDTH // LANES, LANES, LANES), F32)],
        compiler_params=_params("arbitrary", "arbitrary"),
        name="retention",
    )(per_seq(rq), per_seq(rk), per_seq(rv), per_seq(rg), intra, q_dec, k_dec, c_blk, gn_w)
    return out.reshape(batch * seq, RET_WIDTH)


def _compress_kernel(xk_ref, xv_ref, posa_ref, posb_ref, w1a_ref, w1b_ref, w2_ref, kc_ref, vc_ref):
    n_grp = xk_ref.shape[0] // CMP_STRIDE
    for kv, (x_ref, o_ref) in enumerate(((xk_ref, kc_ref), (xv_ref, vc_ref))):
        ya = jnp.zeros((n_grp, NSA_KV_HEADS * CMP_HIDDEN), F32)
        yb = jnp.zeros((n_grp, NSA_KV_HEADS * CMP_HIDDEN), F32)
        for r in range(CMP_STRIDE):
            x = x_ref[pl.ds(r, n_grp, stride=CMP_STRIDE), :]
            cols = slice(r * NSA_KV_WIDTH, (r + 1) * NSA_KV_WIDTH)
            ya = ya + _dot((x + posa_ref[0, kv, :, cols]).astype(BF16), w1a_ref[0, kv, cols, :])
            yb = yb + _dot((x + posb_ref[0, kv, :, cols]).astype(BF16), w1b_ref[0, kv, cols, :])
        hidden = ya + pltpu.roll(yb, n_grp - 1, 0)
        act = jax.nn.gelu(hidden)
        out = _dot(act.astype(BF16), w2_ref[0, kv])
        o_ref[0] = (out if kv == 0 else out.T).astype(BF16)


def _compress_weights(cmp_pos, cmp_w1, cmp_w2):
    depth = cmp_w1.shape[0]
    eye = jnp.eye(NSA_KV_HEADS, dtype=F32)
    a = cmp_w1.reshape(depth, 2, 2, CMP_STRIDE, HEAD_DIM, CMP_HIDDEN)
    w1 = jnp.einsum('lkardj,hg->lkarhdgj', a, eye)
    w1 = w1.reshape(depth, 2, 2, CMP_STRIDE * NSA_KV_WIDTH, NSA_KV_HEADS * CMP_HIDDEN).astype(BF16)
    pos = cmp_pos.reshape(depth, 2, 2, CMP_STRIDE, 1, HEAD_DIM)
    pos = jnp.broadcast_to(pos, (depth, 2, 2, CMP_STRIDE, NSA_KV_HEADS, HEAD_DIM))
    pos = pos.reshape(depth, 2, 2, 1, CMP_STRIDE * NSA_KV_WIDTH)
    w2 = jnp.einsum('lkje,hg->lkhjge', cmp_w2, eye)
    w2 = w2.reshape(depth, 2, NSA_KV_HEADS * CMP_HIDDEN, NSA_KV_WIDTH).astype(BF16)
    return w1[:, :, 0], w1[:, :, 1], pos[:, :, 0], pos[:, :, 1], w2


def _compress(kcx, vcx, cw, layer, batch, seq):
    w1a, w1b, posa, posb, w2 = cw
    n_grp = seq // CMP_STRIDE
    xblk = pl.BlockSpec((seq, NSA_KV_WIDTH), lambda b: (b, 0))
    lay = lambda a: pl.BlockSpec((1,) + a.shape[1:], lambda b: (layer,) + (0,) * (a.ndim - 1))
    oblk = pl.BlockSpec((1, n_grp, NSA_KV_WIDTH), lambda b: (b, 0, 0))
    return pl.pallas_call(
        _compress_kernel,
        grid=(batch,),
        in_specs=[xblk, xblk, lay(posa), lay(posb), lay(w1a), lay(w1b), lay(w2)],
        out_specs=[oblk, oblk],
        out_shape=[jax.ShapeDtypeStruct((batch, n_grp, NSA_KV_WIDTH), BF16)] * 2,
        compiler_params=_params("arbitrary"),
        name="nsa_compress",
    )(kcx, vcx, posa, posb, w1a, w1b, w2)


def _nsa_kernel(qu_ref, qr_ref, kc_ref, vc_ref, ks_ref, vs_ref, kw_ref, vw_ref, ng_ref, ov_ref, wbias_ref, ind_ref,
                o_ref):
    qb = pl.program_id(1)
    q0 = qb * Q_TILE
    qb_last = (qb + 1) * (Q_TILE // Q_BLOCK) - 1
    n_sel = ov_ref.shape[0]
    cols4 = NSA_GROUP * Q_TILE
    n_past = WINDOW // Q_BLOCK
    gsig = jax.nn.sigmoid(ng_ref[...])

    def padded_q(ref, h):
        zeros = jnp.zeros((HEAD_DIM, Q_TILE), BF16)
        parts = []
        for g in range(NSA_GROUP):
            hq = h * NSA_GROUP + g
            x = ref[hq * HEAD_DIM:(hq + 1) * HEAD_DIM, :]
            parts.append(jnp.concatenate([x, zeros] if h == 0 else [zeros, x], axis=0))
        return jnp.concatenate(parts, axis=1)

    def tile4(t):
        return jnp.concatenate([t] * NSA_GROUP, axis=1)

    def attend(state, keys, values_ref, k0, bias, q_t):
        m, acc = state
        s = _dot(keys, q_t)
        if bias is not None:
            s = s + bias
        m_new = jnp.maximum(m, jnp.max(s, axis=0, keepdims=True))
        e = jnp.exp2(s - m_new).astype(BF16)
        pv = [_dot(values_ref[h * V_ROWS:(h + 1) * V_ROWS, pl.ds(k0, keys.shape[0])],
                   e[:, h * cols4:(h + 1) * cols4]) for h in range(NSA_KV_HEADS)]
        return m_new, jnp.exp2(m - m_new) * acc + jnp.concatenate(pv, axis=1)

    def normalised(state):
        acc = state[1]
        return acc[0:HEAD_DIM] * (1.0 / acc[HEAD_DIM:HEAD_DIM + 1])

    cols_all = NSA_KV_HEADS * cols4
    init = (jnp.full((1, cols_all), NEG_INF, F32), jnp.zeros((V_ROWS, cols_all), F32))
    q_rot = [padded_q(qr_ref, h) for h in range(NSA_KV_HEADS)]

    def position_bias(kb, windowed):
        first = qb * (Q_TILE // Q_BLOCK)
        return jnp.concatenate(
            [wbias_ref[jnp.clip(kb - (first + sub) + n_past + 1, 0 if windowed else 2, n_past + 2)]
             for sub in range(Q_TILE // Q_BLOCK)] * (NSA_GROUP * NSA_KV_HEADS), axis=1)

    o_cmp = []
    q_sel = []
    for h in range(NSA_KV_HEADS):
        hr = slice(h * HEAD_DIM, (h + 1) * HEAD_DIM)
        qus = padded_q(qu_ref, h)

        s = _dot(kc_ref[0], qus)
        cid = lax.broadcasted_iota(jnp.int32, (LANES, Q_TILE), 0)
        tq = q0 + lax.broadcasted_iota(jnp.int32, (LANES, Q_TILE), 1)
        cvalid = tile4(jnp.where(cid * CMP_STRIDE + (CMP_LEN - 1) <= tq, 1.0, 0.0))
        sm = jnp.where(cvalid > 0.5, s, NEG_INF)
        e = jnp.exp2(sm - jnp.max(sm, axis=0, keepdims=True)) * cvalid
        l = jnp.sum(e, axis=0, keepdims=True)
        p = e * (1.0 / jnp.where(l > 0.0, l, 1.0))
        o_cmp.append(_dot(vc_ref[0, hr, :], p.astype(BF16)))

        psum = p[:, 0:Q_TILE]
        for g in range(1, NSA_GROUP):
            psum = psum + p[:, g * Q_TILE:(g + 1) * Q_TILE]
        p_hi = psum.astype(BF16)
        rem = psum - p_hi.astype(F32)
        p_mid = rem.astype(BF16)
        p_lo = (rem - p_mid.astype(F32)).astype(BF16)
        ov = ov_ref[...]
        imp = _dot(ov, p_hi) + _dot(ov, p_mid) + _dot(ov, p_lo)
        jid = lax.broadcasted_iota(jnp.int32, (n_sel, Q_TILE), 0)
        tid = q0 + lax.broadcasted_iota(jnp.int32, (n_sel, Q_TILE), 1)
        forced = (jid == 0) | (jid == jnp.right_shift(tid, 6))
        imp = jnp.where(forced, FORCE_SCORE, jnp.where(jid * SEL_LEN <= tid, imp, -FORCE_SCORE))
        rank = jnp.zeros((n_sel, Q_TILE), F32)
        for i in range(n_sel):
            ri = imp[i:i + 1, :]
            beats = (ri > imp) | ((ri == imp) & (jid > i))
            rank = rank + jnp.where(beats, 1.0, 0.0)
        sel_bias = tile4(jnp.where(rank < float(SEL_TOP), 0.0, NEG_INF)).astype(BF16)
        pad = jnp.zeros((LANES - n_sel, cols4), BF16)
        q_sel.append(jnp.concatenate([q_rot[h], sel_bias, pad], axis=0))

    q_sel_all = jnp.concatenate(q_sel, axis=1)
    q_rot_all = jnp.concatenate(q_rot, axis=1)

    def sel_step(gi, state, last):
        k0 = pl.multiple_of(gi * (SEL_GROUP * Q_BLOCK), SEL_GROUP * Q_BLOCK)
        keys = jnp.concatenate([ks_ref[pl.ds(k0, SEL_GROUP * Q_BLOCK), :], ind_ref[gi]], axis=1)
        bias = None
        if last:
            bias = jnp.concatenate([position_bias(gi * SEL_GROUP + u, False) for u in range(SEL_GROUP)], axis=0)
        return attend(state, keys, vs_ref, k0, bias, q_sel_all)

    last_group = qb_last // SEL_GROUP
    sel_state = lax.fori_loop(0, last_group, functools.partial(sel_step, last=False), init)
    o_sel = normalised(sel_step(last_group, sel_state, True))

    ws = pl.multiple_of(jnp.maximum(q0 - WINDOW, 0), Q_BLOCK)
    n_win = n_past + Q_TILE // Q_BLOCK
    win_bias = jnp.concatenate([position_bias(ws // Q_BLOCK + u, True) for u in range(n_win)], axis=0)
    win_keys = kw_ref[pl.ds(ws, n_win * Q_BLOCK), :]
    o_win = normalised(attend(init, win_keys, vw_ref, ws, win_bias, q_rot_all))
    heads = []
    for h in range(NSA_KV_HEADS):
        o_c = o_cmp[h]
        for g in range(NSA_GROUP):
            hq = h * NSA_GROUP + g
            c = slice(g * Q_TILE, (g + 1) * Q_TILE)
            ca = slice(hq * Q_TILE, (hq + 1) * Q_TILE)
            heads.append(gsig[3 * hq:3 * hq + 1, :] * o_c[:, c] + gsig[3 * hq + 1:3 * hq + 2, :] * o_sel[:, ca]
                         + gsig[3 * hq + 2:3 * hq + 3, :] * o_win[:, ca])
    cols_out = [jnp.concatenate(heads[2 * m:2 * m + 2], axis=0).T for m in range(NSA_HEADS // 2)]
    o_ref[...] = jnp.concatenate(cols_out, axis=1).astype(BF16)


def _nsa_consts(seq):
    n_cmp_pad = seq // CMP_STRIDE
    n_sel = seq // SEL_LEN
    c = np.arange(n_cmp_pad)
    j = np.arange(n_sel)
    n_cmp = (seq - CMP_LEN) // CMP_STRIDE + 1
    ov = ((c[None, :] * CMP_STRIDE < j[:, None] * SEL_LEN + SEL_LEN) &
          (j[:, None] * SEL_LEN <= c[None, :] * CMP_STRIDE + CMP_LEN - 1) & (c[None, :] < n_cmp))
    kl = np.arange(Q_BLOCK)[:, None]
    tl = np.arange(Q_BLOCK)[None, :]
    n_past = WINDOW // Q_BLOCK
    none = np.zeros((Q_BLOCK, Q_BLOCK), bool)
    valid = [none, kl > tl] + [np.ones((Q_BLOCK, Q_BLOCK), bool)] * (n_past - 1) + [kl <= tl, none]
    wbias = np.where(np.stack(valid), 0.0, NEG_INF).astype(np.float32)
    key_block = np.arange(seq) // SEL_LEN
    onehot = (key_block[:, None] == np.arange(LANES)[None, :]).reshape(-1, SEL_GROUP * Q_BLOCK, LANES)
    return jnp.asarray(ov, BF16), jnp.asarray(wbias), jnp.asarray(onehot, BF16)


def _nsa(qu, qr, kc, vc, ks, vs, kw, vw, ng, consts, batch, seq):
    ov, wbias, onehot = consts
    n_qb = seq // Q_TILE
    tokens = batch * seq
    qblk = pl.BlockSpec((NSA_WIDTH, Q_TILE), lambda b, q: (0, b * n_qb + q))
    cblk = pl.BlockSpec((1, seq // CMP_STRIDE, NSA_KV_WIDTH), lambda b, q: (b, 0, 0))
    kblk = pl.BlockSpec((seq, NSA_KV_WIDTH), lambda b, q: (b, 0))
    vblk = pl.BlockSpec((NSA_KV_HEADS * V_ROWS, seq), lambda b, q: (0, b))
    full = lambda a: pl.BlockSpec(a.shape, lambda b, q: (0,) * a.ndim)
    return pl.pallas_call(
        _nsa_kernel,
        grid=(batch, n_qb),
        in_specs=[qblk, qblk, cblk, cblk, kblk, vblk, kblk, vblk,
                  pl.BlockSpec((LANES, Q_TILE), lambda b, q: (0, b * n_qb + q)), full(ov), full(wbias),
                  full(onehot)],
        out_specs=pl.BlockSpec((Q_TILE, NSA_WIDTH), lambda b, q: (b * n_qb + q, 0)),
        out_shape=jax.ShapeDtypeStruct((tokens, NSA_WIDTH), BF16),
        compiler_params=_params("arbitrary", "arbitrary"),
        name="nsa_attention",
    )(qu, qr, kc, vc, ks, vs, kw, vw, ng, ov, wbias, onehot)


def _outproj_kernel(yr_ref, yc_ref, yn_ref, x_ref, mod_ref, w_ref, lng_ref, lnb_ref, rwh_ref, rwl_ref, rb_ref,
                    x1_ref, h2a_ref, h2b_ref, ti_ref, tw_ref, cnt_ref, *, alpha):
    tm = x_ref.shape[0]
    mix = (_dot(yr_ref[...], w_ref[0, 0:RET_WIDTH, :])
           + _dot(yc_ref[...], w_ref[0, RET_WIDTH:RET_WIDTH + CONV_WIDTH, :])
           + _dot(yn_ref[...], w_ref[0, RET_WIDTH + CONV_WIDTH:, :]))
    x1 = _layer_norm(alpha * x_ref[...] + (1.0 + mod_ref[2, 0]) * mix) * lng_ref[0, 0] + lnb_ref[0, 0]
    x1_ref[...] = x1
    h2 = _layer_norm(x1) * (1.0 + mod_ref[4, 0]) + mod_ref[3, 0]
    for j, part_ref in enumerate((h2a_ref, h2b_ref)):
        part_ref[...] = _pack_pairs(h2[:, j * 2 * SPLIT_WIDTH:(j + 1) * 2 * SPLIT_WIDTH])
    h_hi = h2.astype(BF16)
    h_lo = (h2 - h_hi.astype(F32)).astype(BF16)
    logits = _dot(h_hi, rwh_ref[0]) + _dot(h_lo, rwh_ref[0]) + _dot(h_hi, rwl_ref[0]) + rb_ref[0]

    lane = lax.broadcasted_iota(jnp.int32, (tm, LANES), 1)
    lanef = lane.astype(F32)
    rest = jnp.where(lane < N_EXPERTS, logits, -jnp.inf)
    vals, idxs = [], []
    for _ in range(TOP_K):
        top = jnp.max(rest, axis=-1, keepdims=True)
        idx = jnp.min(jnp.where(rest == top, lanef, float(LANES)), axis=-1, keepdims=True)
        vals.append(top)
        idxs.append(idx)
        rest = jnp.where(lanef == idx, -jnp.inf, rest)
    exps = [jnp.exp(v - vals[0]) for v in vals]
    inv = 1.0 / functools.reduce(lambda a, b: a + b, exps)
    top_w = jnp.zeros((tm, LANES), F32)
    top_i = jnp.zeros((tm, LANES), F32)
    member = jnp.zeros((tm, LANES), F32)
    for k in range(TOP_K):
        top_w = jnp.where(lane == k, exps[k] * inv, top_w)
        top_i = jnp.where(lane == k, idxs[k], top_i)
        member = member + jnp.where(lanef == idxs[k], 1.0, 0.0)
    tw_ref[...] = top_w
    ti_ref[...] = top_i.astype(jnp.int32)

    @pl.when(pl.program_id(0) == 0)
    def _():
        cnt_ref[...] = jnp.zeros_like(cnt_ref)

    cnt_ref[...] += jnp.broadcast_to(jnp.sum(member, axis=0, keepdims=True), cnt_ref.shape)


def _out_projection(y_ret, y_conv, y_nsa, x, mod_l, w_out, ln_g, ln_b, rw_hi, rw_lo, rb, layer, seq, alpha):
    tokens = x.shape[0]
    tm = ROW_TILE
    tiles_per_seq = seq // tm
    row = lambda w: pl.BlockSpec((tm, w), lambda i: (i, 0))
    lay3 = lambda a: pl.BlockSpec((1,) + a.shape[1:], lambda i: (layer,) + (0,) * (a.ndim - 1))
    return pl.pallas_call(
        functools.partial(_outproj_kernel, alpha=alpha),
        grid=(tokens // tm,),
        in_specs=[row(RET_WIDTH), row(CONV_WIDTH), row(NSA_WIDTH), row(D_MODEL),
                  pl.BlockSpec((6, 1, 1, D_MODEL), lambda i: (0, i // tiles_per_seq, 0, 0)),
                  lay3(w_out),
                  pl.BlockSpec((1, 1, 1, D_MODEL), lambda i: (layer, 0, 0, 0)),
                  pl.BlockSpec((1, 1, 1, D_MODEL), lambda i: (layer, 0, 0, 0)),
                  lay3(rw_hi), lay3(rw_lo), lay3(rb)],
        out_specs=[row(D_MODEL)] + [row(SPLIT_WIDTH)] * SC_SPLIT + [row(LANES), row(LANES),
                   pl.BlockSpec((8, LANES), lambda i: (0, 0))],
        out_shape=[jax.ShapeDtypeStruct((tokens, D_MODEL), F32)]
                  + [jax.ShapeDtypeStruct((tokens, SPLIT_WIDTH), jnp.uint32)] * SC_SPLIT
                  + [jax.ShapeDtypeStruct((tokens, LANES), jnp.int32), jax.ShapeDtypeStruct((tokens, LANES), F32),
                   jax.ShapeDtypeStruct((8, LANES), F32)],
        compiler_params=_params("arbitrary"),
        name="outproj_norm_router",
    )(y_ret, y_conv, y_nsa, x, mod_l, w_out, ln_g, ln_b, rw_hi, rw_lo, rb)


def _route_kernel(ti_ref, pstart_ref, tri_ref, pos_ref, carry_ref):
    @pl.when(pl.program_id(0) == 0)
    def _():
        carry_ref[...] = jnp.zeros_like(carry_ref)

    tm = ti_ref.shape[0]
    lane = lax.broadcasted_iota(jnp.int32, (tm, LANES), 1)
    top_i = ti_ref[...]
    onehots = [lane == top_i[:, k:k + 1] for k in range(TOP_K)]
    member = functools.reduce(lambda a, b: a + b, [jnp.where(o, 1.0, 0.0) for o in onehots])
    base = pstart_ref[...] + carry_ref[0:1] + _dot(tri_ref[...], member.astype(BF16))
    pos = jnp.zeros((tm, LANES), F32)
    for k in range(TOP_K):
        pos_k = jnp.sum(jnp.where(onehots[k], base, 0.0), axis=-1, keepdims=True)
        pos = jnp.where(lane == k, pos_k, pos)
    pos_ref[...] = pos.astype(jnp.int32)
    carry_ref[...] += jnp.broadcast_to(jnp.sum(member, axis=0, keepdims=True), carry_ref.shape)


def _route_positions(top_i, pstart):
    tokens = top_i.shape[0]
    tm = min(ROUTE_TILE, tokens)
    r = np.arange(tm)
    tri = jnp.asarray(r[None, :] < r[:, None], BF16)
    return pl.pallas_call(
        _route_kernel,
        grid=(tokens // tm,),
        in_specs=[pl.BlockSpec((tm, LANES), lambda i: (i, 0)),
                  pl.BlockSpec((1, LANES), lambda i: (0, 0)),
                  pl.BlockSpec((tm, tm), lambda i: (0, 0))],
        out_specs=pl.BlockSpec((tm, LANES), lambda i: (i, 0)),
        out_shape=jax.ShapeDtypeStruct((tokens, LANES), jnp.int32),
        scratch_shapes=[pltpu.VMEM((8, LANES), F32)],
        compiler_params=_params("arbitrary"),
        name="route_positions",
    )(top_i, pstart, tri)


def _block_table(counts, n_blocks):
    tm = MOE_TILE
    cnt = counts[0, :N_EXPERTS].astype(jnp.int32)
    nblk = (cnt + tm - 1) // tm
    bend = jnp.cumsum(nblk)
    bstart = bend - nblk
    blocks = jnp.arange(n_blocks, dtype=jnp.int32)
    block_e = jnp.minimum(jnp.sum((bend[None, :] <= blocks[:, None]).astype(jnp.int32), axis=1), N_EXPERTS - 1)
    n_valid = jnp.clip(cnt[block_e] - (blocks - bstart[block_e]) * tm, 0, tm).astype(jnp.int32)
    n_valid = jnp.where(blocks < bend[-1], n_valid, 0)
    pstart = jnp.pad((bstart * tm).astype(F32), (0, LANES - N_EXPERTS))[None, :]
    first = (blocks < bend[-1]) & ((blocks == 0) | (block_e != jnp.roll(block_e, 1)))
    slot = (jnp.cumsum(first.astype(jnp.int32)) - 1) % 2
    experts = jnp.arange(N_EXPERTS, dtype=jnp.int32)
    later = jnp.where((nblk[None, :] > 0) & (experts[None, :] > experts[:, None]), experts[None, :], N_EXPERTS)
    next_expert = jnp.min(later, axis=1)
    next_expert = jnp.where(next_expert < N_EXPERTS, next_expert, -1)[block_e]
    plan = jnp.stack([first.astype(jnp.int32), slot, next_expert]).astype(jnp.int32)
    return pstart, block_e.astype(jnp.int32), bend[-1:].astype(jnp.int32), n_valid, plan


def _sc_mesh():
    return plsc.VectorSubcoreMesh(core_axis_name="core", subcore_axis_name="subcore")


def _scatter_rows(xs, idx_t, n_rows):
    tokens, width = xs[0].shape
    n_idx = idx_t.shape[0]
    n_x = len(xs)
    win = SC_WINDOW

    @functools.partial(pl.kernel, out_type=[jax.ShapeDtypeStruct((n_rows, width), xs[0].dtype)] * n_x,
                       mesh=_sc_mesh(), scratch_types=[], name="dispatch_scatter")
    def scatter(*refs):
        x_hbms, i_hbm, o_hbms = refs[:n_x], refs[n_x], refs[n_x + 1:]
        idx_specs = [pl.BlockSpec((1, win), functools.partial(lambda i, k: (k, i), k=k)) for k in range(n_idx)]
        for x_hbm, o_hbm in zip(x_hbms, o_hbms):
            def body(x_vmem, *i_vmems, o_hbm=o_hbm):
                for i_vmem in i_vmems:
                    pltpu.sync_copy(x_vmem, o_hbm.at[i_vmem.at[0]])

            pltpu.emit_pipeline(
                body,
                grid=(tokens // win,),
                in_specs=[pl.BlockSpec((win, width), lambda i: (i, 0))] + idx_specs,
                out_specs=[],
                core_axis_name=("core", "subcore"),
                dimension_semantics=(pltpu.PARALLEL,),
            )(x_hbm, *([i_hbm] * n_idx))

    return scatter(*xs, idx_t)


def _gather_rows(xs, idx):
    width = xs[0].shape[1]
    n = idx.shape[0]
    n_x = len(xs)
    win = SC_WINDOW

    @functools.partial(pl.kernel, out_type=[jax.ShapeDtypeStruct((n, width), xs[0].dtype)] * n_x,
                       mesh=_sc_mesh(), scratch_types=[], name="combine_gather")
    def gather(*refs):
        x_hbms, i_hbm, o_hbms = refs[:n_x], refs[n_x], refs[n_x + 1:]
        for x_hbm, o_hbm in zip(x_hbms, o_hbms):
            def body(i_vmem, o_vmem, x_hbm=x_hbm):
                pltpu.sync_copy(x_hbm.at[i_vmem.at[0]], o_vmem)

            pltpu.emit_pipeline(
                body,
                grid=(n // win,),
                in_specs=[pl.BlockSpec((1, win), lambda i: (0, i))],
                out_specs=[pl.BlockSpec((win, width), lambda i: (i, 0))],
                core_axis_name=("core", "subcore"),
                dimension_semantics=(pltpu.PARALLEL,),
            )(i_hbm, o_hbm)

    return gather(*xs, idx.reshape(1, n))


def _moe_kernel(be_ref, nb_ref, nv_ref, plan_ref, ra_ref, rb_ref, wgu_hbm, bgu_ref, wd_hbm, bd_ref,
                oa_ref, ob_ref, wgu_f32, wd_f32, wgu_bf, wd_bf, sem, *, layer):
    row_refs = (ra_ref, rb_ref)
    out_refs = (oa_ref, ob_ref)
    i = pl.program_id(0)
    valid = i < nb_ref[0]

    def weight_copies(expert, slot):
        return (pltpu.make_async_copy(wgu_hbm.at[layer, expert], wgu_f32.at[slot], sem.at[0, slot]),
                pltpu.make_async_copy(wd_hbm.at[layer, expert], wd_f32.at[slot], sem.at[1, slot]))

    @pl.when(i == 0)
    def _():
        for copy in weight_copies(be_ref[0], 0):
            copy.start()

    @pl.when(valid & (plan_ref[0, i] == 1))
    def _():
        slot = plan_ref[1, i]
        for copy in weight_copies(be_ref[i], slot):
            copy.wait()
        upcoming = plan_ref[2, i]

        @pl.when(upcoming >= 0)
        def _():
            for copy in weight_copies(upcoming, 1 - slot):
                copy.start()

        wgu_bf[...] = wgu_f32[slot].astype(BF16)
        wd_bf[...] = wd_f32[slot].astype(BF16)

    @pl.when(valid)
    def _():
        live = lax.broadcasted_iota(jnp.int32, ra_ref.shape, 0) < nv_ref[i]
        rows = jnp.concatenate(
            [half.astype(BF16) for r_ref in row_refs
             for half in _unpack_pairs(jnp.where(live, r_ref[...], jnp.uint32(0)))], axis=1)
        acts = []
        for c in range(D_EXPERT // FFN_CHUNK):
            cg = slice(c * FFN_CHUNK, (c + 1) * FFN_CHUNK)
            cu = slice(D_EXPERT + c * FFN_CHUNK, D_EXPERT + (c + 1) * FFN_CHUNK)
            g = jnp.minimum(_dot(rows, wgu_bf[:, cg]) + bgu_ref[0, 0, :, cg], SWIGLU_LIMIT)
            u = jnp.clip(_dot(rows, wgu_bf[:, cu]) + bgu_ref[0, 0, :, cu], -SWIGLU_LIMIT, SWIGLU_LIMIT)
            acts.append(((u + 1.0) * (g * jax.nn.sigmoid(SWIGLU_ALPHA * g))).astype(BF16))
        y = _dot(jnp.concatenate(acts, axis=1), wd_bf[...]) + bd_ref[0, 0]
        for j, o_ref in enumerate(out_refs):
            o_ref[...] = _pack_pairs(y[:, j * 2 * SPLIT_WIDTH:(j + 1) * 2 * SPLIT_WIDTH])

    @pl.when(jnp.logical_not(valid))
    def _():
        for o_ref in out_refs:
            o_ref[...] = jnp.zeros_like(o_ref)


def _expert_ffn(rows, block_e, n_used, n_valid, plan, w_gu, b_gu, w_down, b_down, layer):
    n_rows = rows[0].shape[0]
    tm = MOE_TILE
    part = pl.BlockSpec((tm, SPLIT_WIDTH), lambda i, be, nb, nv, pn: (i, 0))
    grid_spec = pltpu.PrefetchScalarGridSpec(
        num_scalar_prefetch=4,
        grid=(n_rows // tm,),
        in_specs=[part] * SC_SPLIT + [
                  pl.BlockSpec(memory_space=pl.ANY),
                  pl.BlockSpec((1, 1, 1, 2 * D_EXPERT), lambda i, be, nb, nv, pn: (layer, be[i], 0, 0)),
                  pl.BlockSpec(memory_space=pl.ANY),
                  pl.BlockSpec((1, 1, 1, D_MODEL), lambda i, be, nb, nv, pn: (layer, be[i], 0, 0))],
        out_specs=[part] * SC_SPLIT,
        scratch_shapes=[pltpu.VMEM((2, D_MODEL, 2 * D_EXPERT), F32), pltpu.VMEM((2, D_EXPERT, D_MODEL), F32),
                        pltpu.VMEM((D_MODEL, 2 * D_EXPERT), BF16), pltpu.VMEM((D_EXPERT, D_MODEL), BF16),
                        pltpu.SemaphoreType.DMA((2, 2))])
    return pl.pallas_call(
        functools.partial(_moe_kernel, layer=layer),
        grid_spec=grid_spec,
        out_shape=[jax.ShapeDtypeStruct((n_rows, SPLIT_WIDTH), jnp.uint32)] * SC_SPLIT,
        compiler_params=_params("arbitrary"),
        name="expert_ffn",
    )(block_e, n_used, n_valid, plan, *rows, w_gu, b_gu, w_down, b_down)


def _final_kernel(x_ref, tw_ref, mod_ref, lng_ref, lnb_ref, *rest, alpha):
    y_refs, o_ref = rest[:-1], rest[-1]
    top_w = tw_ref[...]
    parts = []
    for j in range(SC_SPLIT):
        acc = [0.0, 0.0]
        for k in range(TOP_K):
            for half, y in enumerate(_unpack_pairs(y_refs[j * TOP_K + k][...])):
                acc[half] = acc[half] + top_w[:, k:k + 1] * y
        parts += acc
    ffn = jnp.concatenate(parts, axis=1)
    o_ref[...] = (_layer_norm(alpha * x_ref[...] + (1.0 + mod_ref[5, 0]) * ffn) * lng_ref[0, 0] + lnb_ref[0, 0])


def _final_norm(x1, y_parts, top_w, mod_l, ln_g, ln_b, layer, seq, alpha):
    tokens = x1.shape[0]
    tm = ROW_TILE
    tiles = tokens // tm
    tiles_per_seq = seq // tm
    row = lambda w: pl.BlockSpec((tm, w), lambda i: (i, 0))
    vec = pl.BlockSpec((1, 1, 1, D_MODEL), lambda i: (layer, 1, 0, 0))
    y_specs, y_args = [], []
    for j in range(SC_SPLIT):
        for k in range(TOP_K):
            y_specs.append(pl.BlockSpec((tm, SPLIT_WIDTH), functools.partial(lambda i, k: (k * tiles + i, 0), k=k)))
            y_args.append(y_parts[j])
    return pl.pallas_call(
        functools.partial(_final_kernel, alpha=alpha),
        grid=(tiles,),
        in_specs=[row(D_MODEL), row(LANES),
                  pl.BlockSpec((6, 1, 1, D_MODEL), lambda i: (0, i // tiles_per_seq, 0, 0)), vec, vec] + y_specs,
        out_specs=row(D_MODEL),
        out_shape=jax.ShapeDtypeStruct(x1.shape, F32),
        compiler_params=_params("arbitrary"),
        name="combine_final_norm",
    )(x1, top_w, mod_l, ln_g, ln_b, *y_args)


def kernel(x, c, positions, w_in, w_out, ret_gn_w, conv_w, cmp_pos, cmp_w1, cmp_w2, ada_w, ada_b, ln_g, ln_b,
           router_w, router_b, w_gate_up, b_gate_up, w_down, b_down):
    batch, seq, _ = x.shape
    depth = w_in.shape[0]
    tokens = batch * seq
    n_rows = tokens * TOP_K + N_EXPERTS * MOE_TILE
    alpha = float((2 * depth) ** 0.25)

    w_in_r = jnp.pad(w_in, ((0, 0), (0, 0), (0, N_COLS - w_in.shape[2]))).astype(BF16)
    w_in_t = jnp.concatenate([w_in_r[:, :, off:off + width] for off, width in
                              ((OFF_NQ, NSA_WIDTH), (OFF_VS, NSA_KV_WIDTH), (OFF_VW, NSA_KV_WIDTH), (OFF_NG, LANES))],
                             axis=2).transpose(0, 2, 1)
    w_out_b = w_out.astype(BF16)
    rw = jnp.pad(router_w, ((0, 0), (0, 0), (0, LANES - N_EXPERTS)))
    rw_hi = rw.astype(BF16)
    rw_lo = (rw - rw_hi.astype(F32)).astype(BF16)
    rb = jnp.pad(router_b, ((0, 0), (0, LANES - N_EXPERTS))).reshape(depth, 1, LANES)
    cw = _compress_weights(cmp_pos, cmp_w1, cmp_w2)
    ret_consts = _retention_consts()
    nsa_consts = _nsa_consts(seq)
    gn_w = ret_gn_w.reshape(depth, 1, RET_WIDTH)
    ln_g4 = ln_g.reshape(depth, 2, 1, D_MODEL)
    ln_b4 = ln_b.reshape(depth, 2, 1, D_MODEL)
    b_gu4 = b_gate_up.reshape(depth, N_EXPERTS, 1, 2 * D_EXPERT)
    b_d4 = b_down.reshape(depth, N_EXPERTS, 1, D_MODEL)

    mod = _modulation(c, ada_w, ada_b)
    cos, sin, cos_t, sin_t = _rope_tables(positions)
    xt = x.reshape(tokens, D_MODEL)
    for l in range(depth):
        (rq, rk, rv, rg, y_conv, qu, qr, kcx, vcx, ks, vs, kw, vw, ng) = _input_projection(
            xt, mod[l], w_in_r, w_in_t, cos, sin, cos_t, sin_t, conv_w, l, seq)
        y_ret = _retention(rq, rk, rv, rg, ret_consts, gn_w, l, batch, seq)
        kc, vc = _compress(kcx, vcx, cw, l, batch, seq)
        y_nsa = _nsa(qu, qr, kc, vc, ks, vs, kw, vw, ng, nsa_consts, batch, seq)
        x1, *h2, top_i, top_w, counts = _out_projection(y_ret, y_conv, y_nsa, xt, mod[l], w_out_b, ln_g4, ln_b4,
                                                        rw_hi, rw_lo, rb, l, seq, alpha)
        pstart, block_e, n_used, n_valid, plan = _block_table(counts, n_rows // MOE_TILE)
        pos_t = _route_positions(top_i, pstart)[:, :TOP_K].T
        rows = _scatter_rows(h2, pos_t, n_rows)
        y = _expert_ffn(rows, block_e, n_used, n_valid, plan, w_gate_up, b_gu4, w_down, b_d4, l)
        y_tok = _gather_rows(y, pos_t.reshape(-1))
        xt = _final_norm(x1, y_tok, top_w, mod[l], ln_g4, ln_b4, l, seq, alpha)
    return xt.reshape(batch, seq, D_MODEL)
```

```python
import functools

import numpy as np
import jax
import jax.numpy as jnp
from jax import lax
from jax.experimental import pallas as pl
from jax.experimental.pallas import tpu as pltpu
from jax.experimental.pallas import tpu_sc as plsc

F32 = jnp.float32
BF16 = jnp.bfloat16

D_MODEL = 1024
HEAD_DIM = 64
RET_WIDTH = 256
RET_HEADS = 4
RET_CHUNK = 128
CONV_WIDTH = 256
CONV_K = 3
NSA_WIDTH = 512
NSA_HEADS = 8
NSA_KV_HEADS = 2
NSA_GROUP = 4
NSA_KV_WIDTH = 128
CMP_LEN = 32
CMP_STRIDE = 16
CMP_HIDDEN = 128
SEL_LEN = 64
SEL_TOP = 8
WINDOW = 512
Q_BLOCK = 128
Q_TILE = 2 * Q_BLOCK
ROPE_THETA = 10000.0
N_EXPERTS = 32
TOP_K = 4
D_EXPERT = 1024
SWIGLU_LIMIT = 7.0
SWIGLU_ALPHA = 1.702
LN_EPS = 1e-5
NEG_INF = -1e30
FORCE_SCORE = 1e9

LANES = 128
VMEM_LIMIT = 56 * 1024 * 1024

OFF_RQ, OFF_RK, OFF_RV, OFF_RG = 0, 256, 512, 768
OFF_CB, OFF_CC, OFF_CH = 1024, 1280, 1536
OFF_NQ = 1792
OFF_KC, OFF_VC = 2304, 2432
OFF_KS, OFF_VS, OFF_KW, OFF_VW = 2560, 2688, 2816, 2944
OFF_NG = 3072
N_GATE = NSA_HEADS * 3
N_COLS = OFF_NG
LOG2E = 1.4426950408889634
WT_NQ, WT_VS, WT_VW, WT_NG = 0, NSA_WIDTH, NSA_WIDTH + NSA_KV_WIDTH, NSA_WIDTH + 2 * NSA_KV_WIDTH
WT_ROWS = WT_NG + LANES

RET_BATCH = 4
ROW_TILE = 256
MOE_TILE = 512
MOE_SUB = 256
FFN_CHUNK = 256
ROUTE_TILE = 1024
SEL_GROUP = 4
V_ROWS = HEAD_DIM + 16
SC_WINDOW = LANES
SC_SPLIT = 2
SPLIT_WIDTH = D_MODEL // (2 * SC_SPLIT)


def _dot(a, b):
    return jnp.dot(a, b, preferred_element_type=F32)


def _dot_nt(a, b):
    return lax.dot_general(a, b, (((1,), (1,)), ((), ())), preferred_element_type=F32)


def _layer_norm(x):
    mu = jnp.mean(x, axis=-1, keepdims=True)
    xc = x - mu
    var = jnp.mean(xc * xc, axis=-1, keepdims=True)
    return xc * lax.rsqrt(var + LN_EPS)


def _pack_pairs(x):
    bits = lambda t: lax.bitcast_convert_type(t.astype(BF16).astype(F32), jnp.uint32)
    return (bits(x[:, SPLIT_WIDTH:]) & jnp.uint32(0xFFFF0000)) | (bits(x[:, :SPLIT_WIDTH]) >> 16)


def _unpack_pairs(p):
    low = lax.bitcast_convert_type(p << 16, F32)
    high = lax.bitcast_convert_type(p & jnp.uint32(0xFFFF0000), F32)
    return low, high


def _params(*sem):
    return pltpu.CompilerParams(dimension_semantics=sem, vmem_limit_bytes=VMEM_LIMIT)


def _mod_kernel(c_ref, w_ref, b_ref, o_ref):
    c = c_ref[...]
    ca = (c * jax.nn.sigmoid(c)).astype(BF16)
    o_ref[0, 0] = _dot(ca, w_ref[0].astype(BF16)) + b_ref[0]


def _modulation(c, ada_w, ada_b):
    depth = ada_w.shape[0]
    batch = c.shape[0]
    out = pl.pallas_call(
        _mod_kernel,
        grid=(depth, 6),
        in_specs=[pl.BlockSpec((batch, D_MODEL), lambda l, j: (0, 0)),
                  pl.BlockSpec((1, D_MODEL, D_MODEL), lambda l, j: (l, 0, j)),
                  pl.BlockSpec((1, 1, D_MODEL), lambda l, j: (l * 6 + j, 0, 0))],
        out_specs=pl.BlockSpec((1, 1, batch, D_MODEL), lambda l, j: (l, j, 0, 0)),
        out_shape=jax.ShapeDtypeStruct((depth, 6, batch, D_MODEL), F32),
        compiler_params=_params("arbitrary", "arbitrary"),
        name="adaln_mod",
    )(c, ada_w, ada_b.reshape(depth * 6, 1, D_MODEL))
    return out.reshape(depth, 6, batch, 1, D_MODEL)


def _rope_table_kernel(pos_ref, inv_ref, cos_ref, sin_ref):
    ang = pos_ref[...] * inv_ref[...]
    cos_ref[...] = jnp.cos(ang)
    sin_ref[...] = jnp.sin(ang)


def _rope_tables(positions):
    half = HEAD_DIM // 2
    per_row = LANES // half
    tokens = positions.size
    pos4 = jnp.repeat(positions.reshape(tokens // per_row, per_row).astype(F32), half, axis=1)
    inv = ROPE_THETA ** (-jnp.arange(half, dtype=F32) / half)
    inv4 = jnp.tile(inv, per_row)[None, :]
    rows = tokens // per_row
    tile = min(rows, 1024)
    cos4, sin4 = pl.pallas_call(
        _rope_table_kernel,
        grid=(rows // tile,),
        in_specs=[pl.BlockSpec((tile, LANES), lambda i: (i, 0)),
                  pl.BlockSpec((1, LANES), lambda i: (0, 0))],
        out_specs=[pl.BlockSpec((tile, LANES), lambda i: (i, 0))] * 2,
        out_shape=[jax.ShapeDtypeStruct((rows, LANES), F32)] * 2,
        compiler_params=_params("arbitrary"),
        name="rope_tables",
    )(pos4, inv4)
    cos32 = cos4.reshape(tokens, half)
    sin32 = sin4.reshape(tokens, half)
    cos = jnp.tile(cos32, (1, per_row))
    sign = jnp.tile(jnp.concatenate([-jnp.ones((half,), F32), jnp.ones((half,), F32)]), LANES // HEAD_DIM)
    sin = jnp.tile(sin32, (1, per_row)) * sign[None, :]
    return cos, sin, cos32.T, sin32.T


def _inproj_kernel(x_ref, mod_ref, w_ref, wt_ref, cos_ref, sin_ref, cost_ref, sint_ref, convw_ref,
                   rq_ref, rk_ref, rv_ref, rg_ref, yc_ref, qu_ref, qr_ref, kc_ref, vc_ref,
                   ks_ref, vs_ref, kw_ref, vw_ref, ng_ref, carry_ref, *, tiles_per_seq):
    i = pl.program_id(0)
    tm = x_ref.shape[0]
    h = (_layer_norm(x_ref[...]) * (1.0 + mod_ref[1, 0]) + mod_ref[0, 0]).astype(BF16)
    cosf = cos_ref[...]
    sinf = sin_ref[...]
    lane = lax.broadcasted_iota(jnp.int32, (tm, LANES), 1)
    first_half = (lane % HEAD_DIM) < (HEAD_DIM // 2)

    def proj(off, width):
        return _dot(h, w_ref[0, :, off:off + width])

    def rope(c):
        cols = []
        for j in range(c.shape[1] // LANES):
            cj = c[:, j * LANES:(j + 1) * LANES]
            swapped = jnp.where(first_half, pltpu.roll(cj, LANES - HEAD_DIM // 2, 1),
                                pltpu.roll(cj, HEAD_DIM // 2, 1))
            cols.append(cj * cosf + swapped * sinf)
        return jnp.concatenate(cols, axis=1) if len(cols) > 1 else cols[0]

    scale = HEAD_DIM ** -0.5
    rq_ref[...] = rope(proj(OFF_RQ, RET_WIDTH)).astype(BF16)
    rk_ref[...] = (rope(proj(OFF_RK, RET_WIDTH)) * scale).astype(BF16)
    rv_ref[...] = proj(OFF_RV, RET_WIDTH).astype(BF16)
    rg = proj(OFF_RG, RET_WIDTH)
    rg_ref[...] = (rg * jax.nn.sigmoid(rg)).astype(BF16)

    cb = proj(OFF_CB, CONV_WIDTH)
    u = proj(OFF_CC, CONV_WIDTH) * proj(OFF_CH, CONV_WIDTH)

    @pl.when(i % tiles_per_seq == 0)
    def _():
        carry_ref[...] = jnp.zeros_like(carry_ref)

    carry = carry_ref[...]
    row = lax.broadcasted_iota(jnp.int32, (tm, CONV_WIDTH), 0)
    prev1 = jnp.where(row == 0, carry[7:8], pltpu.roll(u, 1, 0))
    prev2 = jnp.where(row == 0, carry[6:7], jnp.where(row == 1, carry[7:8], pltpu.roll(u, 2, 0)))
    cw = convw_ref[0]
    yc_ref[...] = (cb * (cw[0:1] * prev2 + cw[1:2] * prev1 + cw[2:3] * u)).astype(BF16)
    carry_ref[...] = u[tm - 8:tm]

    def proj_t(off, width):
        return _dot_nt(wt_ref[0, off:off + width, :], h)

    nq_t = proj_t(WT_NQ, NSA_WIDTH) * (scale * LOG2E)
    qu_ref[...] = nq_t.astype(BF16)
    cos_t = cost_ref[...]
    sin_t = sint_ref[...]
    half = HEAD_DIM // 2
    rotated = []
    for hq in range(NSA_HEADS):
        t1 = nq_t[hq * HEAD_DIM:hq * HEAD_DIM + half]
        t2 = nq_t[hq * HEAD_DIM + half:(hq + 1) * HEAD_DIM]
        rotated += [t1 * cos_t - t2 * sin_t, t2 * cos_t + t1 * sin_t]
    qr_ref[...] = jnp.concatenate(rotated, axis=0).astype(BF16)
    kc_ref[...] = proj(OFF_KC, NSA_KV_WIDTH)
    vc_ref[...] = proj(OFF_VC, NSA_KV_WIDTH)

    def values_t(off):
        v_t = proj_t(off, NSA_KV_WIDTH)
        ones = jnp.ones((V_ROWS - HEAD_DIM, tm), F32)
        parts = []
        for hd in range(NSA_KV_HEADS):
            parts += [v_t[hd * HEAD_DIM:(hd + 1) * HEAD_DIM], ones]
        return jnp.concatenate(parts, axis=0).astype(BF16)

    ks_ref[...] = rope(proj(OFF_KS, NSA_KV_WIDTH)).astype(BF16)
    vs_ref[...] = values_t(WT_VS)
    kw_ref[...] = rope(proj(OFF_KW, NSA_KV_WIDTH)).astype(BF16)
    vw_ref[...] = values_t(WT_VW)
    ng_ref[...] = proj_t(WT_NG, LANES)


def _input_projection(x, mod_l, w_l, w_t, cos, sin, cos_t, sin_t, conv_w, layer, seq):
    tokens = x.shape[0]
    tm = ROW_TILE
    tiles_per_seq = seq // tm
    row = lambda w: pl.BlockSpec((tm, w), lambda i: (i, 0))
    col = lambda w: pl.BlockSpec((w, tm), lambda i: (0, i))
    outs = [(RET_WIDTH, BF16, False)] * 4 + [(CONV_WIDTH, BF16, False), (NSA_WIDTH, BF16, True),
                                            (NSA_WIDTH, BF16, True), (NSA_KV_WIDTH, F32, False),
                                            (NSA_KV_WIDTH, F32, False), (NSA_KV_WIDTH, BF16, False),
                                            (NSA_KV_HEADS * V_ROWS, BF16, True), (NSA_KV_WIDTH, BF16, False),
                                            (NSA_KV_HEADS * V_ROWS, BF16, True), (LANES, F32, True)]
    return pl.pallas_call(
        functools.partial(_inproj_kernel, tiles_per_seq=tiles_per_seq),
        grid=(tokens // tm,),
        in_specs=[row(D_MODEL),
                  pl.BlockSpec((6, 1, 1, D_MODEL), lambda i: (0, i // tiles_per_seq, 0, 0)),
                  pl.BlockSpec((1, D_MODEL, N_COLS), lambda i: (layer, 0, 0)),
                  pl.BlockSpec((1, WT_ROWS, D_MODEL), lambda i: (layer, 0, 0)),
                  row(LANES), row(LANES), col(HEAD_DIM // 2), col(HEAD_DIM // 2),
                  pl.BlockSpec((1, CONV_K, CONV_WIDTH), lambda i: (layer, 0, 0))],
        out_specs=[col(w) if t else row(w) for w, _, t in outs],
        out_shape=[jax.ShapeDtypeStruct((w, tokens) if t else (tokens, w), dt) for w, dt, t in outs],
        scratch_shapes=[pltpu.VMEM((8, CONV_WIDTH), F32)],
        compiler_params=_params("arbitrary"),
        name="ln_inproj",
    )(x, mod_l, w_l, w_t, cos, sin, cos_t, sin_t, conv_w)


def _retention_kernel(q_ref, k_ref, v_ref, g_ref, intra_ref, qdec_ref, kdec_ref, cdec_ref, gn_ref,
                      o_ref, state_ref):
    @pl.when(pl.program_id(1) == 0)
    def _():
        state_ref[...] = jnp.zeros_like(state_ref)

    n_pairs = RET_WIDTH // LANES
    lane = lax.broadcasted_iota(jnp.int32, (RET_CHUNK, LANES), 1)
    low = lane < HEAD_DIM
    blk_r = lax.broadcasted_iota(jnp.int32, (LANES, LANES), 0) < HEAD_DIM
    blk_c = lax.broadcasted_iota(jnp.int32, (LANES, LANES), 1) < HEAD_DIM
    same_head = blk_r == blk_c
    states = {(b, p): state_ref[b, p] for b in range(q_ref.shape[0]) for p in range(n_pairs)}
    for ch, b in [(ch, b) for ch in range(q_ref.shape[1] // RET_CHUNK) for b in range(q_ref.shape[0])]:
        rows = slice(ch * RET_CHUNK, (ch + 1) * RET_CHUNK)
        outs = []
        for p in range(n_pairs):
            cols = slice(p * LANES, (p + 1) * LANES)
            qp = q_ref[b, rows, cols]
            kp = k_ref[b, rows, cols]
            vp = v_ref[b, rows, cols]
            out = jnp.zeros((RET_CHUNK, LANES), F32)
            for hh in range(2):
                keep = low if hh == 0 else jnp.logical_not(low)
                qm = jnp.where(keep, qp, jnp.zeros_like(qp))
                s = _dot_nt(qm, kp) * intra_ref[2 * p + hh]
                out = jnp.where(keep, _dot(s.astype(BF16), vp), out)
            state = states[b, p]
            qd = (qp.astype(F32) * qdec_ref[:, cols]).astype(BF16)
            out = out + _dot(qd, state.astype(BF16))
            kd_t = (kp.astype(F32) * kdec_ref[:, cols]).T.astype(BF16)
            upd = _dot(kd_t, vp)
            states[b, p] = state * cdec_ref[p] + jnp.where(same_head, upd, 0.0)
            def half_mean(t):
                lo_sum = jnp.sum(jnp.where(low, t, 0.0), axis=-1, keepdims=True)
                hi_sum = jnp.sum(jnp.where(low, 0.0, t), axis=-1, keepdims=True)
                return jnp.where(low, lo_sum, hi_sum) * (1.0 / HEAD_DIM)
            oc = out - half_mean(out)
            outs.append(oc * lax.rsqrt(half_mean(oc * oc) + LN_EPS))
        normed = jnp.concatenate(outs, axis=1)
        o_ref[b, rows, :] = (normed * gn_ref[0] * g_ref[b, rows, :].astype(F32)).astype(BF16)
    for (b, p), state in states.items():
        state_ref[b, p] = state


def _retention_consts():
    heads = jnp.arange(RET_HEADS, dtype=F32)
    log_gamma = jnp.log(1.0 - jnp.power(2.0, -5.0 - heads))
    i = jnp.arange(RET_CHUNK, dtype=F32)
    diff = i[:, None] - i[None, :]
    intra = jnp.where(diff >= 0, jnp.exp(diff * log_gamma[:, None, None]), 0.0)
    q_dec = jnp.exp((i + 1.0) * log_gamma[:, None])
    k_dec = jnp.exp((RET_CHUNK - 1.0 - i) * log_gamma[:, None])
    c_dec = jnp.exp(RET_CHUNK * log_gamma)
    expand = lambda t: jnp.repeat(t.T, HEAD_DIM, axis=1)
    c_rows = jnp.repeat(c_dec, HEAD_DIM).reshape(RET_WIDTH // LANES, LANES, 1)
    c_blk = jnp.broadcast_to(c_rows, (RET_WIDTH // LANES, LANES, LANES))
    return intra, expand(q_dec), expand(k_dec), c_blk


def _retention(rq, rk, rv, rg, consts, gn_w, layer, batch, seq):
    intra, q_dec, k_dec, c_blk = consts
    rows = 2 * RET_CHUNK
    steps = seq // rows
    group = RET_BATCH if batch % RET_BATCH == 0 else 1
    blk = pl.BlockSpec((group, rows, RET_WIDTH), lambda b, s: (b, s, 0))
    full = lambda a: pl.BlockSpec(a.shape, lambda b, s: (0,) * a.ndim)
    per_seq = lambda t: t.reshape(batch, seq, RET_WIDTH)
    out = pl.pallas_call(
        _retention_kernel,
        grid=(batch // group, steps),
        in_specs=[blk, blk, blk, blk, full(intra), full(q_dec), full(k_dec), full(c_blk),
                  pl.BlockSpec((1, 1, RET_WIDTH), lambda b, s: (layer, 0, 0))],
        out_specs=blk,
        out_shape=jax.ShapeDtypeStruct((batch, seq, RET_WIDTH), BF16),
        scratch_shapes=[pltpu.VMEM((group, RET_WIDTH // LANES, LANES, LANES), F32)],
        compiler_params=_params("arbitrary", "arbitrary"),
        name="retention",
    )(per_seq(rq), per_seq(rk), per_seq(rv), per_seq(rg), intra, q_dec, k_dec, c_blk, gn_w)
    return out.reshape(batch * seq, RET_WIDTH)


def _compress_kernel(xk_ref, xv_ref, posa_ref, posb_ref, w1a_ref, w1b_ref, w2_ref, kc_ref, vc_ref):
    n_grp = xk_ref.shape[0] // CMP_STRIDE
    for kv, (x_ref, o_ref) in enumerate(((xk_ref, kc_ref), (xv_ref, vc_ref))):
        ya = jnp.zeros((n_grp, NSA_KV_HEADS * CMP_HIDDEN), F32)
        yb = jnp.zeros((n_grp, NSA_KV_HEADS * CMP_HIDDEN), F32)
        for r in range(CMP_STRIDE):
            x = x_ref[pl.ds(r, n_grp, stride=CMP_STRIDE), :]
            cols = slice(r * NSA_KV_WIDTH, (r + 1) * NSA_KV_WIDTH)
            ya = ya + _dot((x + posa_ref[0, kv, :, cols]).astype(BF16), w1a_ref[0, kv, cols, :])
            yb = yb + _dot((x + posb_ref[0, kv, :, cols]).astype(BF16), w1b_ref[0, kv, cols, :])
        hidden = ya + pltpu.roll(yb, n_grp - 1, 0)
        act = jax.nn.gelu(hidden)
        out = _dot(act.astype(BF16), w2_ref[0, kv])
        o_ref[0] = (out if kv == 0 else out.T).astype(BF16)


def _compress_weights(cmp_pos, cmp_w1, cmp_w2):
    depth = cmp_w1.shape[0]
    eye = jnp.eye(NSA_KV_HEADS, dtype=F32)
    a = cmp_w1.reshape(depth, 2, 2, CMP_STRIDE, HEAD_DIM, CMP_HIDDEN)
    w1 = jnp.einsum('lkardj,hg->lkarhdgj', a, eye)
    w1 = w1.reshape(depth, 2, 2, CMP_STRIDE * NSA_KV_WIDTH, NSA_KV_HEADS * CMP_HIDDEN).astype(BF16)
    pos = cmp_pos.reshape(depth, 2, 2, CMP_STRIDE, 1, HEAD_DIM)
    pos = jnp.broadcast_to(pos, (depth, 2, 2, CMP_STRIDE, NSA_KV_HEADS, HEAD_DIM))
    pos = pos.reshape(depth, 2, 2, 1, CMP_STRIDE * NSA_KV_WIDTH)
    w2 = jnp.einsum('lkje,hg->lkhjge', cmp_w2, eye)
    w2 = w2.reshape(depth, 2, NSA_KV_HEADS * CMP_HIDDEN, NSA_KV_WIDTH).astype(BF16)
    return w1[:, :, 0], w1[:, :, 1], pos[:, :, 0], pos[:, :, 1], w2


def _compress(kcx, vcx, cw, layer, batch, seq):
    w1a, w1b, posa, posb, w2 = cw
    n_grp = seq // CMP_STRIDE
    xblk = pl.BlockSpec((seq, NSA_KV_WIDTH), lambda b: (b, 0))
    lay = lambda a: pl.BlockSpec((1,) + a.shape[1:], lambda b: (layer,) + (0,) * (a.ndim - 1))
    oblk = pl.BlockSpec((1, n_grp, NSA_KV_WIDTH), lambda b: (b, 0, 0))
    return pl.pallas_call(
        _compress_kernel,
        grid=(batch,),
        in_specs=[xblk, xblk, lay(posa), lay(posb), lay(w1a), lay(w1b), lay(w2)],
        out_specs=[oblk, oblk],
        out_shape=[jax.ShapeDtypeStruct((batch, n_grp, NSA_KV_WIDTH), BF16)] * 2,
        compiler_params=_params("arbitrary"),
        name="nsa_compress",
    )(kcx, vcx, posa, posb, w1a, w1b, w2)


def _nsa_kernel(qu_ref, qr_ref, kc_ref, vc_ref, ks_ref, vs_ref, kw_ref, vw_ref, ng_ref, ov_ref, wbias_ref, ind_ref,
                o_ref):
    qb = pl.program_id(1)
    q0 = qb * Q_TILE
    qb_last = (qb + 1) * (Q_TILE // Q_BLOCK) - 1
    n_sel = ov_ref.shape[0]
    cols4 = NSA_GROUP * Q_TILE
    n_past = WINDOW // Q_BLOCK
    gsig = jax.nn.sigmoid(ng_ref[...])

    def padded_q(ref, h):
        zeros = jnp.zeros((HEAD_DIM, Q_TILE), BF16)
        parts = []
        for g in range(NSA_GROUP):
            hq = h * NSA_GROUP + g
            x = ref[hq * HEAD_DIM:(hq + 1) * HEAD_DIM, :]
            parts.append(jnp.concatenate([x, zeros] if h == 0 else [zeros, x], axis=0))
        return jnp.concatenate(parts, axis=1)

    def tile4(t):
        return jnp.concatenate([t] * NSA_GROUP, axis=1)

    def attend(state, keys, values_ref, k0, bias, q_t):
        m, acc = state
        s = _dot(keys, q_t)
        if bias is not None:
            s = s + bias
        m_new = jnp.maximum(m, jnp.max(s, axis=0, keepdims=True))
        e = jnp.exp2(s - m_new).astype(BF16)
        pv = [_dot(values_ref[h * V_ROWS:(h + 1) * V_ROWS, pl.ds(k0, keys.shape[0])],
                   e[:, h * cols4:(h + 1) * cols4]) for h in range(NSA_KV_HEADS)]
        return m_new, jnp.exp2(m - m_new) * acc + jnp.concatenate(pv, axis=1)

    def normalised(state):
        acc = state[1]
        return acc[0:HEAD_DIM] * (1.0 / acc[HEAD_DIM:HEAD_DIM + 1])

    cols_all = NSA_KV_HEADS * cols4
    init = (jnp.full((1, cols_all), NEG_INF, F32), jnp.zeros((V_ROWS, cols_all), F32))
    q_rot = [padded_q(qr_ref, h) for h in range(NSA_KV_HEADS)]

    def position_bias(kb, windowed):
        first = qb * (Q_TILE // Q_BLOCK)
        return jnp.concatenate(
            [wbias_ref[jnp.clip(kb - (first + sub) + n_past + 1, 0 if windowed else 2, n_past + 2)]
             for sub in range(Q_TILE // Q_BLOCK)] * (NSA_GROUP * NSA_KV_HEADS), axis=1)

    o_cmp = []
    q_sel = []
    for h in range(NSA_KV_HEADS):
        hr = slice(h * HEAD_DIM, (h + 1) * HEAD_DIM)
        qus = padded_q(qu_ref, h)

        s = _dot(kc_ref[0], qus)
        cid = lax.broadcasted_iota(jnp.int32, (LANES, Q_TILE), 0)
        tq = q0 + lax.broadcasted_iota(jnp.int32, (LANES, Q_TILE), 1)
        cvalid = tile4(jnp.where(cid * CMP_STRIDE + (CMP_LEN - 1) <= tq, 1.0, 0.0))
        sm = jnp.where(cvalid > 0.5, s, NEG_INF)
        e = jnp.exp2(sm - jnp.max(sm, axis=0, keepdims=True)) * cvalid
        l = jnp.sum(e, axis=0, keepdims=True)
        p = e * (1.0 / jnp.where(l > 0.0, l, 1.0))
        o_cmp.append(_dot(vc_ref[0, hr, :], p.astype(BF16)))

        psum = p[:, 0:Q_TILE]
        for g in range(1, NSA_GROUP):
            psum = psum + p[:, g * Q_TILE:(g + 1) * Q_TILE]
        p_hi = psum.astype(BF16)
        rem = psum - p_hi.astype(F32)
        p_mid = rem.astype(BF16)
        p_lo = (rem - p_mid.astype(F32)).astype(BF16)
        ov = ov_ref[...]
        imp = _dot(ov, p_hi) + _dot(ov, p_mid) + _dot(ov, p_lo)
        jid = lax.broadcasted_iota(jnp.int32, (n_sel, Q_TILE), 0)
        tid = q0 + lax.broadcasted_iota(jnp.int32, (n_sel, Q_TILE), 1)
        forced = (jid == 0) | (jid == jnp.right_shift(tid, 6))
        imp = jnp.where(forced, FORCE_SCORE, jnp.where(jid * SEL_LEN <= tid, imp, -FORCE_SCORE))
        rank = jnp.zeros((n_sel, Q_TILE), F32)
        for i in range(n_sel):
            ri = imp[i:i + 1, :]
            beats = (ri > imp) | ((ri == imp) & (jid > i))
            rank = rank + jnp.where(beats, 1.0, 0.0)
        sel_bias = tile4(jnp.where(rank < float(SEL_TOP), 0.0, NEG_INF)).astype(BF16)
        pad = jnp.zeros((LANES - n_sel, cols4), BF16)
        q_sel.append(jnp.concatenate([q_rot[h], sel_bias, pad], axis=0))

    q_sel_all = jnp.concatenate(q_sel, axis=1)
    q_rot_all = jnp.concatenate(q_rot, axis=1)

    def sel_step(gi, state, last):
        k0 = pl.multiple_of(gi * (SEL_GROUP * Q_BLOCK), SEL_GROUP * Q_BLOCK)
        keys = jnp.concatenate([ks_ref[pl.ds(k0, SEL_GROUP * Q_BLOCK), :], ind_ref[gi]], axis=1)
        bias = None
        if last:
            bias = jnp.concatenate([position_bias(gi * SEL_GROUP + u, False) for u in range(SEL_GROUP)], axis=0)
        return attend(state, keys, vs_ref, k0, bias, q_sel_all)

    last_group = qb_last // SEL_GROUP
    sel_state = lax.fori_loop(0, last_group, functools.partial(sel_step, last=False), init)
    o_sel = normalised(sel_step(last_group, sel_state, True))

    ws = pl.multiple_of(jnp.maximum(q0 - WINDOW, 0), Q_BLOCK)
    n_win = n_past + Q_TILE // Q_BLOCK
    win_bias = jnp.concatenate([position_bias(ws // Q_BLOCK + u, True) for u in range(n_win)], axis=0)
    win_keys = kw_ref[pl.ds(ws, n_win * Q_BLOCK), :]
    o_win = normalised(attend(init, win_keys, vw_ref, ws, win_bias, q_rot_all))
    heads = []
    for h in range(NSA_KV_HEADS):
        o_c = o_cmp[h]
        for g in range(NSA_GROUP):
            hq = h * NSA_GROUP + g
            c = slice(g * Q_TILE, (g + 1) * Q_TILE)
            ca = slice(hq * Q_TILE, (hq + 1) * Q_TILE)
            heads.append(gsig[3 * hq:3 * hq + 1, :] * o_c[:, c] + gsig[3 * hq + 1:3 * hq + 2, :] * o_sel[:, ca]
                         + gsig[3 * hq + 2:3 * hq + 3, :] * o_win[:, ca])
    cols_out = [jnp.concatenate(heads[2 * m:2 * m + 2], axis=0).T for m in range(NSA_HEADS // 2)]
    o_ref[...] = jnp.concatenate(cols_out, axis=1).astype(BF16)


def _nsa_consts(seq):
    n_cmp_pad = seq // CMP_STRIDE
    n_sel = seq // SEL_LEN
    c = np.arange(n_cmp_pad)
    j = np.arange(n_sel)
    n_cmp = (seq - CMP_LEN) // CMP_STRIDE + 1
    ov = ((c[None, :] * CMP_STRIDE < j[:, None] * SEL_LEN + SEL_LEN) &
          (j[:, None] * SEL_LEN <= c[None, :] * CMP_STRIDE + CMP_LEN - 1) & (c[None, :] < n_cmp))
    kl = np.arange(Q_BLOCK)[:, None]
    tl = np.arange(Q_BLOCK)[None, :]
    n_past = WINDOW // Q_BLOCK
    none = np.zeros((Q_BLOCK, Q_BLOCK), bool)
    valid = [none, kl > tl] + [np.ones((Q_BLOCK, Q_BLOCK), bool)] * (n_past - 1) + [kl <= tl, none]
    wbias = np.where(np.stack(valid), 0.0, NEG_INF).astype(np.float32)
    key_block = np.arange(seq) // SEL_LEN
    onehot = (key_block[:, None] == np.arange(LANES)[None, :]).reshape(-1, SEL_GROUP * Q_BLOCK, LANES)
    return jnp.asarray(ov, BF16), jnp.asarray(wbias), jnp.asarray(onehot, BF16)


def _nsa(qu, qr, kc, vc, ks, vs, kw, vw, ng, consts, batch, seq):
    ov, wbias, onehot = consts
    n_qb = seq // Q_TILE
    tokens = batch * seq
    qblk = pl.BlockSpec((NSA_WIDTH, Q_TILE), lambda b, q: (0, b * n_qb + q))
    cblk = pl.BlockSpec((1, seq // CMP_STRIDE, NSA_KV_WIDTH), lambda b, q: (b, 0, 0))
    kblk = pl.BlockSpec((seq, NSA_KV_WIDTH), lambda b, q: (b, 0))
    vblk = pl.BlockSpec((NSA_KV_HEADS * V_ROWS, seq), lambda b, q: (0, b))
    full = lambda a: pl.BlockSpec(a.shape, lambda b, q: (0,) * a.ndim)
    return pl.pallas_call(
        _nsa_kernel,
        grid=(batch, n_qb),
        in_specs=[qblk, qblk, cblk, cblk, kblk, vblk, kblk, vblk,
                  pl.BlockSpec((LANES, Q_TILE), lambda b, q: (0, b * n_qb + q)), full(ov), full(wbias),
                  full(onehot)],
        out_specs=pl.BlockSpec((Q_TILE, NSA_WIDTH), lambda b, q: (b * n_qb + q, 0)),
        out_shape=jax.ShapeDtypeStruct((tokens, NSA_WIDTH), BF16),
        compiler_params=_params("arbitrary", "arbitrary"),
        name="nsa_attention",
    )(qu, qr, kc, vc, ks, vs, kw, vw, ng, ov, wbias, onehot)


def _outproj_kernel(yr_ref, yc_ref, yn_ref, x_ref, mod_ref, w_ref, lng_ref, lnb_ref, rwh_ref, rwl_ref, rb_ref,
                    x1_ref, h2a_ref, h2b_ref, ti_ref, tw_ref, cnt_ref, *, alpha):
    tm = x_ref.shape[0]
    mix = (_dot(yr_ref[...], w_ref[0, 0:RET_WIDTH, :])
           + _dot(yc_ref[...], w_ref[0, RET_WIDTH:RET_WIDTH + CONV_WIDTH, :])
           + _dot(yn_ref[...], w_ref[0, RET_WIDTH + CONV_WIDTH:, :]))
    x1 = _layer_norm(alpha * x_ref[...] + (1.0 + mod_ref[2, 0]) * mix) * lng_ref[0, 0] + lnb_ref[0, 0]
    x1_ref[...] = x1
    h2 = _layer_norm(x1) * (1.0 + mod_ref[4, 0]) + mod_ref[3, 0]
    for j, part_ref in enumerate((h2a_ref, h2b_ref)):
        part_ref[...] = _pack_pairs(h2[:, j * 2 * SPLIT_WIDTH:(j + 1) * 2 * SPLIT_WIDTH])
    h_hi = h2.astype(BF16)
    h_lo = (h2 - h_hi.astype(F32)).astype(BF16)
    logits = _dot(h_hi, rwh_ref[0]) + _dot(h_lo, rwh_ref[0]) + _dot(h_hi, rwl_ref[0]) + rb_ref[0]

    lane = lax.broadcasted_iota(jnp.int32, (tm, LANES), 1)
    lanef = lane.astype(F32)
    rest = jnp.where(lane < N_EXPERTS, logits, -jnp.inf)
    vals, idxs = [], []
    for _ in range(TOP_K):
        top = jnp.max(rest, axis=-1, keepdims=True)
        idx = jnp.min(jnp.where(rest == top, lanef, float(LANES)), axis=-1, keepdims=True)
        vals.append(top)
        idxs.append(idx)
        rest = jnp.where(lanef == idx, -jnp.inf, rest)
    exps = [jnp.exp(v - vals[0]) for v in vals]
    inv = 1.0 / functools.reduce(lambda a, b: a + b, exps)
    top_w = jnp.zeros((tm, LANES), F32)
    top_i = jnp.zeros((tm, LANES), F32)
    member = jnp.zeros((tm, LANES), F32)
    for k in range(TOP_K):
        top_w = jnp.where(lane == k, exps[k] * inv, top_w)
        top_i = jnp.where(lane == k, idxs[k], top_i)
        member = member + jnp.where(lanef == idxs[k], 1.0, 0.0)
    tw_ref[...] = top_w
    ti_ref[...] = top_i.astype(jnp.int32)

    @pl.when(pl.program_id(0) == 0)
    def _():
        cnt_ref[...] = jnp.zeros_like(cnt_ref)

    cnt_ref[...] += jnp.broadcast_to(jnp.sum(member, axis=0, keepdims=True), cnt_ref.shape)


def _out_projection(y_ret, y_conv, y_nsa, x, mod_l, w_out, ln_g, ln_b, rw_hi, rw_lo, rb, layer, seq, alpha):
    tokens = x.shape[0]
    tm = ROW_TILE
    tiles_per_seq = seq // tm
    row = lambda w: pl.BlockSpec((tm, w), lambda i: (i, 0))
    lay3 = lambda a: pl.BlockSpec((1,) + a.shape[1:], lambda i: (layer,) + (0,) * (a.ndim - 1))
    return pl.pallas_call(
        functools.partial(_outproj_kernel, alpha=alpha),
        grid=(tokens // tm,),
        in_specs=[row(RET_WIDTH), row(CONV_WIDTH), row(NSA_WIDTH), row(D_MODEL),
                  pl.BlockSpec((6, 1, 1, D_MODEL), lambda i: (0, i // tiles_per_seq, 0, 0)),
                  lay3(w_out),
                  pl.BlockSpec((1, 1, 1, D_MODEL), lambda i: (layer, 0, 0, 0)),
                  pl.BlockSpec((1, 1, 1, D_MODEL), lambda i: (layer, 0, 0, 0)),
                  lay3(rw_hi), lay3(rw_lo), lay3(rb)],
        out_specs=[row(D_MODEL)] + [row(SPLIT_WIDTH)] * SC_SPLIT + [row(LANES), row(LANES),
                   pl.BlockSpec((8, LANES), lambda i: (0, 0))],
        out_shape=[jax.ShapeDtypeStruct((tokens, D_MODEL), F32)]
                  + [jax.ShapeDtypeStruct((tokens, SPLIT_WIDTH), jnp.uint32)] * SC_SPLIT
                  + [jax.ShapeDtypeStruct((tokens, LANES), jnp.int32), jax.ShapeDtypeStruct((tokens, LANES), F32),
                   jax.ShapeDtypeStruct((8, LANES), F32)],
        compiler_params=_params("arbitrary"),
        name="outproj_norm_router",
    )(y_ret, y_conv, y_nsa, x, mod_l, w_out, ln_g, ln_b, rw_hi, rw_lo, rb)


def _route_kernel(ti_ref, pstart_ref, tri_ref, pos_ref, carry_ref):
    @pl.when(pl.program_id(0) == 0)
    def _():
        carry_ref[...] = jnp.zeros_like(carry_ref)

    tm = ti_ref.shape[0]
    lane = lax.broadcasted_iota(jnp.int32, (tm, LANES), 1)
    top_i = ti_ref[...]
    onehots = [lane == top_i[:, k:k + 1] for k in range(TOP_K)]
    member = functools.reduce(lambda a, b: a + b, [jnp.where(o, 1.0, 0.0) for o in onehots])
    base = pstart_ref[...] + carry_ref[0:1] + _dot(tri_ref[...], member.astype(BF16))
    pos = jnp.zeros((tm, LANES), F32)
    for k in range(TOP_K):
        pos_k = jnp.sum(jnp.where(onehots[k], base, 0.0), axis=-1, keepdims=True)
        pos = jnp.where(lane == k, pos_k, pos)
    pos_ref[...] = pos.astype(jnp.int32)
    carry_ref[...] += jnp.broadcast_to(jnp.sum(member, axis=0, keepdims=True), carry_ref.shape)


def _route_positions(top_i, pstart):
    tokens = top_i.shape[0]
    tm = min(ROUTE_TILE, tokens)
    r = np.arange(tm)
    tri = jnp.asarray(r[None, :] < r[:, None], BF16)
    return pl.pallas_call(
        _route_kernel,
        grid=(tokens // tm,),
        in_specs=[pl.BlockSpec((tm, LANES), lambda i: (i, 0)),
                  pl.BlockSpec((1, LANES), lambda i: (0, 0)),
                  pl.BlockSpec((tm, tm), lambda i: (0, 0))],
        out_specs=pl.BlockSpec((tm, LANES), lambda i: (i, 0)),
        out_shape=jax.ShapeDtypeStruct((tokens, LANES), jnp.int32),
        scratch_shapes=[pltpu.VMEM((8, LANES), F32)],
        compiler_params=_params("arbitrary"),
        name="route_positions",
    )(top_i, pstart, tri)


def _block_table(counts, n_blocks):
    tm = MOE_TILE
    cnt = counts[0, :N_EXPERTS].astype(jnp.int32)
    nblk = (cnt + tm - 1) // tm
    bend = jnp.cumsum(nblk)
    bstart = bend - nblk
    blocks = jnp.arange(n_blocks, dtype=jnp.int32)
    block_e = jnp.minimum(jnp.sum((bend[None, :] <= blocks[:, None]).astype(jnp.int32), axis=1), N_EXPERTS - 1)
    n_valid = jnp.clip(cnt[block_e] - (blocks - bstart[block_e]) * tm, 0, tm).astype(jnp.int32)
    n_valid = jnp.where(blocks < bend[-1], n_valid, 0)
    pstart = jnp.pad((bstart * tm).astype(F32), (0, LANES - N_EXPERTS))[None, :]
    first = (blocks < bend[-1]) & ((blocks == 0) | (block_e != jnp.roll(block_e, 1)))
    slot = (jnp.cumsum(first.astype(jnp.int32)) - 1) % 2
    experts = jnp.arange(N_EXPERTS, dtype=jnp.int32)
    later = jnp.where((nblk[None, :] > 0) & (experts[None, :] > experts[:, None]), experts[None, :], N_EXPERTS)
    next_expert = jnp.min(later, axis=1)
    next_expert = jnp.where(next_expert < N_EXPERTS, next_expert, -1)[block_e]
    plan = jnp.stack([first.astype(jnp.int32), slot, next_expert]).astype(jnp.int32)
    return pstart, block_e.astype(jnp.int32), bend[-1:].astype(jnp.int32), n_valid, plan


def _sc_mesh():
    return plsc.VectorSubcoreMesh(core_axis_name="core", subcore_axis_name="subcore")


def _scatter_rows(xs, idx_t, n_rows):
    tokens, width = xs[0].shape
    n_idx = idx_t.shape[0]
    n_x = len(xs)
    win = SC_WINDOW

    @functools.partial(pl.kernel, out_type=[jax.ShapeDtypeStruct((n_rows, width), xs[0].dtype)] * n_x,
                       mesh=_sc_mesh(), scratch_types=[], name="dispatch_scatter")
    def scatter(*refs):
        x_hbms, i_hbm, o_hbms = refs[:n_x], refs[n_x], refs[n_x + 1:]
        idx_specs = [pl.BlockSpec((1, win), functools.partial(lambda i, k: (k, i), k=k)) for k in range(n_idx)]
        for x_hbm, o_hbm in zip(x_hbms, o_hbms):
            def body(x_vmem, *i_vmems, o_hbm=o_hbm):
                for i_vmem in i_vmems:
                    pltpu.sync_copy(x_vmem, o_hbm.at[i_vmem.at[0]])

            pltpu.emit_pipeline(
                body,
                grid=(tokens // win,),
                in_specs=[pl.BlockSpec((win, width), lambda i: (i, 0))] + idx_specs,
                out_specs=[],
                core_axis_name=("core", "subcore"),
                dimension_semantics=(pltpu.PARALLEL,),
            )(x_hbm, *([i_hbm] * n_idx))

    return scatter(*xs, idx_t)


def _gather_rows(xs, idx):
    width = xs[0].shape[1]
    n = idx.shape[0]
    n_x = len(xs)
    win = SC_WINDOW

    @functools.partial(pl.kernel, out_type=[jax.ShapeDtypeStruct((n, width), xs[0].dtype)] * n_x,
                       mesh=_sc_mesh(), scratch_types=[], name="combine_gather")
    def gather(*refs):
        x_hbms, i_hbm, o_hbms = refs[:n_x], refs[n_x], refs[n_x + 1:]
        for x_hbm, o_hbm in zip(x_hbms, o_hbms):
            def body(i_vmem, o_vmem, x_hbm=x_hbm):
                pltpu.sync_copy(x_hbm.at[i_vmem.at[0]], o_vmem)

            pltpu.emit_pipeline(
                body,
                grid=(n // win,),
                in_specs=[pl.BlockSpec((1, win), lambda i: (0, i))],
                out_specs=[pl.BlockSpec((win, width), lambda i: (i, 0))],
                core_axis_name=("core", "subcore"),
                dimension_semantics=(pltpu.PARALLEL,),
            )(i_hbm, o_hbm)

    return gather(*xs, idx.reshape(1, n))


def _moe_kernel(be_ref, nb_ref, nv_ref, plan_ref, ra_ref, rb_ref, wgu_hbm, bgu_ref, wd_hbm, bd_ref,
                oa_ref, ob_ref, wgu_f32, wd_f32, wgu_bf, wd_bf, sem, *, layer):
    row_refs = (ra_ref, rb_ref)
    out_refs = (oa_ref, ob_ref)
    i = pl.program_id(0)
    valid = i < nb_ref[0]

    def weight_copies(expert, slot):
        return (pltpu.make_async_copy(wgu_hbm.at[layer, expert], wgu_f32.at[slot], sem.at[0, slot]),
                pltpu.make_async_copy(wd_hbm.at[layer, expert], wd_f32.at[slot], sem.at[1, slot]))

    @pl.when(i == 0)
    def _():
        for copy in weight_copies(be_ref[0], 0):
            copy.start()

    @pl.when(valid & (plan_ref[0, i] == 1))
    def _():
        slot = plan_ref[1, i]
        for copy in weight_copies(be_ref[i], slot):
            copy.wait()
        upcoming = plan_ref[2, i]

        @pl.when(upcoming >= 0)
        def _():
            for copy in weight_copies(upcoming, 1 - slot):
                copy.start()

        wgu_bf[...] = wgu_f32[slot].astype(BF16)
        wd_bf[...] = wd_f32[slot].astype(BF16)

    def ffn(rs):
        live = rs.start + lax.broadcasted_iota(jnp.int32, (MOE_SUB, SPLIT_WIDTH), 0) < nv_ref[i]
        rows = jnp.concatenate(
            [half.astype(BF16) for r_ref in row_refs
             for half in _unpack_pairs(jnp.where(live, r_ref[rs, :], jnp.uint32(0)))], axis=1)
        acts = []
        for c in range(D_EXPERT // FFN_CHUNK):
            cg = slice(c * FFN_CHUNK, (c + 1) * FFN_CHUNK)
            cu = slice(D_EXPERT + c * FFN_CHUNK, D_EXPERT + (c + 1) * FFN_CHUNK)
            g = jnp.minimum(_dot(rows, wgu_bf[:, cg]) + bgu_ref[0, 0, :, cg], SWIGLU_LIMIT)
            u = jnp.clip(_dot(rows, wgu_bf[:, cu]) + bgu_ref[0, 0, :, cu], -SWIGLU_LIMIT, SWIGLU_LIMIT)
            acts.append(((u + 1.0) * (g * jax.nn.sigmoid(SWIGLU_ALPHA * g))).astype(BF16))
        y = _dot(jnp.concatenate(acts, axis=1), wd_bf[...]) + bd_ref[0, 0]
        for j, o_ref in enumerate(out_refs):
            o_ref[rs, :] = _pack_pairs(y[:, j * 2 * SPLIT_WIDTH:(j + 1) * 2 * SPLIT_WIDTH])

    for sub in range(MOE_TILE // MOE_SUB):
        rs = slice(sub * MOE_SUB, (sub + 1) * MOE_SUB)
        has_rows = valid & (nv_ref[i] > sub * MOE_SUB)
        pl.when(has_rows)(functools.partial(ffn, rs))

        @pl.when(jnp.logical_not(has_rows))
        def _(rs=rs):
            for o_ref in out_refs:
                o_ref[rs, :] = jnp.zeros((MOE_SUB, SPLIT_WIDTH), jnp.uint32)


def _expert_ffn(rows, block_e, n_used, n_valid, plan, w_gu, b_gu, w_down, b_down, layer):
    n_rows = rows[0].shape[0]
    tm = MOE_TILE
    part = pl.BlockSpec((tm, SPLIT_WIDTH), lambda i, be, nb, nv, pn: (i, 0))
    grid_spec = pltpu.PrefetchScalarGridSpec(
        num_scalar_prefetch=4,
        grid=(n_rows // tm,),
        in_specs=[part] * SC_SPLIT + [
                  pl.BlockSpec(memory_space=pl.ANY),
                  pl.BlockSpec((1, 1, 1, 2 * D_EXPERT), lambda i, be, nb, nv, pn: (layer, be[i], 0, 0)),
                  pl.BlockSpec(memory_space=pl.ANY),
                  pl.BlockSpec((1, 1, 1, D_MODEL), lambda i, be, nb, nv, pn: (layer, be[i], 0, 0))],
        out_specs=[part] * SC_SPLIT,
        scratch_shapes=[pltpu.VMEM((2, D_MODEL, 2 * D_EXPERT), F32), pltpu.VMEM((2, D_EXPERT, D_MODEL), F32),
                        pltpu.VMEM((D_MODEL, 2 * D_EXPERT), BF16), pltpu.VMEM((D_EXPERT, D_MODEL), BF16),
                        pltpu.SemaphoreType.DMA((2, 2))])
    return pl.pallas_call(
        functools.partial(_moe_kernel, layer=layer),
        grid_spec=grid_spec,
        out_shape=[jax.ShapeDtypeStruct((n_rows, SPLIT_WIDTH), jnp.uint32)] * SC_SPLIT,
        compiler_params=_params("arbitrary"),
        name="expert_ffn",
    )(block_e, n_used, n_valid, plan, *rows, w_gu, b_gu, w_down, b_down)


def _final_kernel(x_ref, tw_ref, mod_ref, lng_ref, lnb_ref, *rest, alpha):
    y_refs, o_ref = rest[:-1], rest[-1]
    top_w = tw_ref[...]
    parts = []
    for j in range(SC_SPLIT):
        acc = [0.0, 0.0]
        for k in range(TOP_K):
            for half, y in enumerate(_unpack_pairs(y_refs[j * TOP_K + k][...])):
                acc[half] = acc[half] + top_w[:, k:k + 1] * y
        parts += acc
    ffn = jnp.concatenate(parts, axis=1)
    o_ref[...] = (_layer_norm(alpha * x_ref[...] + (1.0 + mod_ref[5, 0]) * ffn) * lng_ref[0, 0] + lnb_ref[0, 0])


def _final_norm(x1, y_parts, top_w, mod_l, ln_g, ln_b, layer, seq, alpha):
    tokens = x1.shape[0]
    tm = ROW_TILE
    tiles = tokens // tm
    tiles_per_seq = seq // tm
    row = lambda w: pl.BlockSpec((tm, w), lambda i: (i, 0))
    vec = pl.BlockSpec((1, 1, 1, D_MODEL), lambda i: (layer, 1, 0, 0))
    y_specs, y_args = [], []
    for j in range(SC_SPLIT):
        for k in range(TOP_K):
            y_specs.append(pl.BlockSpec((tm, SPLIT_WIDTH), functools.partial(lambda i, k: (k * tiles + i, 0), k=k)))
            y_args.append(y_parts[j])
    return pl.pallas_call(
        functools.partial(_final_kernel, alpha=alpha),
        grid=(tiles,),
        in_specs=[row(D_MODEL), row(LANES),
                  pl.BlockSpec((6, 1, 1, D_MODEL), lambda i: (0, i // tiles_per_seq, 0, 0)), vec, vec] + y_specs,
        out_specs=row(D_MODEL),
        out_shape=jax.ShapeDtypeStruct(x1.shape, F32),
        compiler_params=_params("arbitrary"),
        name="combine_final_norm",
    )(x1, top_w, mod_l, ln_g, ln_b, *y_args)


def _inproj_weights(w_in):
    gate_cols = jnp.pad(w_in[:, :, OFF_NG:], ((0, 0), (0, 0), (0, LANES - N_GATE)))
    w_t = jnp.concatenate([w_in[:, :, off:off + width] for off, width in
                           ((OFF_NQ, NSA_WIDTH), (OFF_VS, NSA_KV_WIDTH), (OFF_VW, NSA_KV_WIDTH))] + [gate_cols],
                          axis=2).transpose(0, 2, 1)
    return w_in[:, :, :N_COLS].astype(BF16), w_t.astype(BF16)


def kernel(x, c, positions, w_in, w_out, ret_gn_w, conv_w, cmp_pos, cmp_w1, cmp_w2, ada_w, ada_b, ln_g, ln_b,
           router_w, router_b, w_gate_up, b_gate_up, w_down, b_down):
    batch, seq, _ = x.shape
    depth = w_in.shape[0]
    tokens = batch * seq
    n_rows = tokens * TOP_K + N_EXPERTS * MOE_TILE
    alpha = float((2 * depth) ** 0.25)

    w_in_r, w_in_t = _inproj_weights(w_in)
    w_out_b = w_out.astype(BF16)
    rw = jnp.pad(router_w, ((0, 0), (0, 0), (0, LANES - N_EXPERTS)))
    rw_hi = rw.astype(BF16)
    rw_lo = (rw - rw_hi.astype(F32)).astype(BF16)
    rb = jnp.pad(router_b, ((0, 0), (0, LANES - N_EXPERTS))).reshape(depth, 1, LANES)
    cw = _compress_weights(cmp_pos, cmp_w1, cmp_w2)
    ret_consts = _retention_consts()
    nsa_consts = _nsa_consts(seq)
    gn_w = ret_gn_w.reshape(depth, 1, RET_WIDTH)
    ln_g4 = ln_g.reshape(depth, 2, 1, D_MODEL)
    ln_b4 = ln_b.reshape(depth, 2, 1, D_MODEL)
    b_gu4 = b_gate_up.reshape(depth, N_EXPERTS, 1, 2 * D_EXPERT)
    b_d4 = b_down.reshape(depth, N_EXPERTS, 1, D_MODEL)

    mod = _modulation(c, ada_w, ada_b)
    cos, sin, cos_t, sin_t = _rope_tables(positions)
    xt = x.reshape(tokens, D_MODEL)
    for l in range(depth):
        (rq, rk, rv, rg, y_conv, qu, qr, kcx, vcx, ks, vs, kw, vw, ng) = _input_projection(
            xt, mod[l], w_in_r, w_in_t, cos, sin, cos_t, sin_t, conv_w, l, seq)
        y_ret = _retention(rq, rk, rv, rg, ret_consts, gn_w, l, batch, seq)
        kc, vc = _compress(kcx, vcx, cw, l, batch, seq)
        y_nsa = _nsa(qu, qr, kc, vc, ks, vs, kw, vw, ng, nsa_consts, batch, seq)
        x1, *h2, top_i, top_w, counts = _out_projection(y_ret, y_conv, y_nsa, xt, mod[l], w_out_b, ln_g4, ln_b4,
                                                        rw_hi, rw_lo, rb, l, seq, alpha)
        pstart, block_e, n_used, n_valid, plan = _block_table(counts, n_rows // MOE_TILE)
        pos_t = _route_positions(top_i, pstart)[:, :TOP_K].T
        rows = _scatter_rows(h2, pos_t, n_rows)
        y = _expert_ffn(rows, block_e, n_used, n_valid, plan, w_gate_up, b_gu4, w_down, b_d4, l)
        y_tok = _gather_rows(y, pos_t.reshape(-1))
        xt = _final_norm(x1, y_tok, top_w, mod[l], ln_g4, ln_b4, l, seq, alpha)
    return xt.reshape(batch, seq, D_MODEL)
```

```python
import functools

import numpy as np
import jax
import jax.numpy as jnp
from jax import lax
from jax.experimental import pallas as pl
from jax.experimental.pallas import tpu as pltpu
from jax.experimental.pallas import tpu_sc as plsc

F32 = jnp.float32
BF16 = jnp.bfloat16

D_MODEL = 1024
HEAD_DIM = 64
RET_WIDTH = 256
RET_HEADS = 4
RET_CHUNK = 128
CONV_WIDTH = 256
CONV_K = 3
NSA_WIDTH = 512
NSA_HEADS = 8
NSA_KV_HEADS = 2
NSA_GROUP = 4
NSA_KV_WIDTH = 128
CMP_LEN = 32
CMP_STRIDE = 16
CMP_HIDDEN = 128
SEL_LEN = 64
SEL_TOP = 8
WINDOW = 512
Q_BLOCK = 128
Q_TILE = 2 * Q_BLOCK
ROPE_THETA = 10000.0
N_EXPERTS = 32
TOP_K = 4
D_EXPERT = 1024
SWIGLU_LIMIT = 7.0
SWIGLU_ALPHA = 1.702
LN_EPS = 1e-5
NEG_INF = -1e30
FORCE_SCORE = 1e9

LANES = 128
VMEM_LIMIT = 56 * 1024 * 1024

OFF_RQ, OFF_RK, OFF_RV, OFF_RG = 0, 256, 512, 768
OFF_CB, OFF_CC, OFF_CH = 1024, 1280, 1536
OFF_NQ = 1792
OFF_KC, OFF_VC = 2304, 2432
OFF_KS, OFF_VS, OFF_KW, OFF_VW = 2560, 2688, 2816, 2944
OFF_NG = 3072
N_GATE = NSA_HEADS * 3
N_COLS = OFF_NG
LOG2E = 1.4426950408889634
WT_NQ, WT_VS, WT_VW, WT_NG = 0, NSA_WIDTH, NSA_WIDTH + NSA_KV_WIDTH, NSA_WIDTH + 2 * NSA_KV_WIDTH
WT_ROWS = WT_NG + LANES

RET_BATCH = 4
ROW_TILE = 256
MOE_TILE = 512
MOE_SUB = 256
FFN_CHUNK = 256
ROUTE_TILE = 1024
SEL_GROUP = 4
V_ROWS = HEAD_DIM + 16
SC_WINDOW = LANES
SC_SPLIT = 2
SPLIT_WIDTH = D_MODEL // (2 * SC_SPLIT)


def _dot(a, b):
    return jnp.dot(a, b, preferred_element_type=F32)


def _dot_nt(a, b):
    return lax.dot_general(a, b, (((1,), (1,)), ((), ())), preferred_element_type=F32)


def _layer_norm(x):
    mu = jnp.mean(x, axis=-1, keepdims=True)
    xc = x - mu
    var = jnp.mean(xc * xc, axis=-1, keepdims=True)
    return xc * lax.rsqrt(var + LN_EPS)


def _pack_pairs(x):
    bits = lambda t: lax.bitcast_convert_type(t.astype(BF16).astype(F32), jnp.uint32)
    return (bits(x[:, SPLIT_WIDTH:]) & jnp.uint32(0xFFFF0000)) | (bits(x[:, :SPLIT_WIDTH]) >> 16)


def _unpack_pairs(p):
    low = lax.bitcast_convert_type(p << 16, F32)
    high = lax.bitcast_convert_type(p & jnp.uint32(0xFFFF0000), F32)
    return low, high


def _params(*sem):
    return pltpu.CompilerParams(dimension_semantics=sem, vmem_limit_bytes=VMEM_LIMIT)


def _mod_kernel(c_ref, w_ref, b_ref, o_ref):
    c = c_ref[...]
    ca = (c * jax.nn.sigmoid(c)).astype(BF16)
    o_ref[0, 0] = _dot(ca, w_ref[0].astype(BF16)) + b_ref[0]


def _modulation(c, ada_w, ada_b):
    depth = ada_w.shape[0]
    batch = c.shape[0]
    out = pl.pallas_call(
        _mod_kernel,
        grid=(depth, 6),
        in_specs=[pl.BlockSpec((batch, D_MODEL), lambda l, j: (0, 0)),
                  pl.BlockSpec((1, D_MODEL, D_MODEL), lambda l, j: (l, 0, j)),
                  pl.BlockSpec((1, 1, D_MODEL), lambda l, j: (l * 6 + j, 0, 0))],
        out_specs=pl.BlockSpec((1, 1, batch, D_MODEL), lambda l, j: (l, j, 0, 0)),
        out_shape=jax.ShapeDtypeStruct((depth, 6, batch, D_MODEL), F32),
        compiler_params=_params("arbitrary", "arbitrary"),
        name="adaln_mod",
    )(c, ada_w, ada_b.reshape(depth * 6, 1, D_MODEL))
    return out.reshape(depth, 6, batch, 1, D_MODEL)


def _rope_table_kernel(pos_ref, inv_ref, cos_ref, sin_ref):
    ang = pos_ref[...] * inv_ref[...]
    cos_ref[...] = jnp.cos(ang)
    sin_ref[...] = jnp.sin(ang)


def _rope_tables(positions):
    half = HEAD_DIM // 2
    per_row = LANES // half
    tokens = positions.size
    pos4 = jnp.repeat(positions.reshape(tokens // per_row, per_row).astype(F32), half, axis=1)
    inv = ROPE_THETA ** (-jnp.arange(half, dtype=F32) / half)
    inv4 = jnp.tile(inv, per_row)[None, :]
    rows = tokens // per_row
    tile = min(rows, 1024)
    cos4, sin4 = pl.pallas_call(
        _rope_table_kernel,
        grid=(rows // tile,),
        in_specs=[pl.BlockSpec((tile, LANES), lambda i: (i, 0)),
                  pl.BlockSpec((1, LANES), lambda i: (0, 0))],
        out_specs=[pl.BlockSpec((tile, LANES), lambda i: (i, 0))] * 2,
        out_shape=[jax.ShapeDtypeStruct((rows, LANES), F32)] * 2,
        compiler_params=_params("arbitrary"),
        name="rope_tables",
    )(pos4, inv4)
    cos32 = cos4.reshape(tokens, half)
    sin32 = sin4.reshape(tokens, half)
    cos = jnp.tile(cos32, (1, per_row))
    sign = jnp.tile(jnp.concatenate([-jnp.ones((half,), F32), jnp.ones((half,), F32)]), LANES // HEAD_DIM)
    sin = jnp.tile(sin32, (1, per_row)) * sign[None, :]
    return cos, sin, cos32.T, sin32.T


def _inproj_kernel(x_ref, mod_ref, w_ref, wt_ref, cos_ref, sin_ref, cost_ref, sint_ref, convw_ref,
                   rq_ref, rk_ref, rv_ref, rg_ref, yc_ref, qu_ref, qr_ref, kc_ref, vc_ref,
                   ks_ref, vs_ref, kw_ref, vw_ref, ng_ref, carry_ref, *, tiles_per_seq):
    i = pl.program_id(0)
    tm = x_ref.shape[0]
    h = (_layer_norm(x_ref[...]) * (1.0 + mod_ref[1, 0]) + mod_ref[0, 0]).astype(BF16)
    cosf = cos_ref[...]
    sinf = sin_ref[...]
    lane = lax.broadcasted_iota(jnp.int32, (tm, LANES), 1)
    first_half = (lane % HEAD_DIM) < (HEAD_DIM // 2)

    def proj(off, width):
        return _dot(h, w_ref[0, :, off:off + width])

    def rope(c):
        cols = []
        for j in range(c.shape[1] // LANES):
            cj = c[:, j * LANES:(j + 1) * LANES]
            swapped = jnp.where(first_half, pltpu.roll(cj, LANES - HEAD_DIM // 2, 1),
                                pltpu.roll(cj, HEAD_DIM // 2, 1))
            cols.append(cj * cosf + swapped * sinf)
        return jnp.concatenate(cols, axis=1) if len(cols) > 1 else cols[0]

    scale = HEAD_DIM ** -0.5
    rq_ref[...] = rope(proj(OFF_RQ, RET_WIDTH)).astype(BF16)
    rk_ref[...] = (rope(proj(OFF_RK, RET_WIDTH)) * scale).astype(BF16)
    rv_ref[...] = proj(OFF_RV, RET_WIDTH).astype(BF16)
    rg = proj(OFF_RG, RET_WIDTH)
    rg_ref[...] = (rg * jax.nn.sigmoid(rg)).astype(BF16)

    cb = proj(OFF_CB, CONV_WIDTH)
    u = proj(OFF_CC, CONV_WIDTH) * proj(OFF_CH, CONV_WIDTH)

    @pl.when(i % tiles_per_seq == 0)
    def _():
        carry_ref[...] = jnp.zeros_like(carry_ref)

    carry = carry_ref[...]
    row = lax.broadcasted_iota(jnp.int32, (tm, CONV_WIDTH), 0)
    prev1 = jnp.where(row == 0, carry[7:8], pltpu.roll(u, 1, 0))
    prev2 = jnp.where(row == 0, carry[6:7], jnp.where(row == 1, carry[7:8], pltpu.roll(u, 2, 0)))
    cw = convw_ref[0]
    yc_ref[...] = (cb * (cw[0:1] * prev2 + cw[1:2] * prev1 + cw[2:3] * u)).astype(BF16)
    carry_ref[...] = u[tm - 8:tm]

    def proj_t(off, width):
        return _dot_nt(wt_ref[0, off:off + width, :], h)

    nq_t = proj_t(WT_NQ, NSA_WIDTH) * (scale * LOG2E)
    qu_ref[...] = nq_t.astype(BF16)
    cos_t = cost_ref[...]
    sin_t = sint_ref[...]
    half = HEAD_DIM // 2
    rotated = []
    for hq in range(NSA_HEADS):
        t1 = nq_t[hq * HEAD_DIM:hq * HEAD_DIM + half]
        t2 = nq_t[hq * HEAD_DIM + half:(hq + 1) * HEAD_DIM]
        rotated += [t1 * cos_t - t2 * sin_t, t2 * cos_t + t1 * sin_t]
    qr_ref[...] = jnp.concatenate(rotated, axis=0).astype(BF16)
    kc_ref[...] = proj(OFF_KC, NSA_KV_WIDTH)
    vc_ref[...] = proj(OFF_VC, NSA_KV_WIDTH)

    def values_t(off):
        v_t = proj_t(off, NSA_KV_WIDTH)
        ones = jnp.ones((V_ROWS - HEAD_DIM, tm), F32)
        parts = []
        for hd in range(NSA_KV_HEADS):
            parts += [v_t[hd * HEAD_DIM:(hd + 1) * HEAD_DIM], ones]
        return jnp.concatenate(parts, axis=0).astype(BF16)

    ks_ref[...] = rope(proj(OFF_KS, NSA_KV_WIDTH)).astype(BF16)
    vs_ref[...] = values_t(WT_VS)
    kw_ref[...] = rope(proj(OFF_KW, NSA_KV_WIDTH)).astype(BF16)
    vw_ref[...] = values_t(WT_VW)
    ng_ref[...] = proj_t(WT_NG, LANES)


def _input_projection(x, mod_l, w_l, w_t, cos, sin, cos_t, sin_t, conv_w, layer, seq):
    tokens = x.shape[0]
    tm = ROW_TILE
    tiles_per_seq = seq // tm
    row = lambda w: pl.BlockSpec((tm, w), lambda i: (i, 0))
    col = lambda w: pl.BlockSpec((w, tm), lambda i: (0, i))
    outs = [(RET_WIDTH, BF16, False)] * 4 + [(CONV_WIDTH, BF16, False), (NSA_WIDTH, BF16, True),
                                            (NSA_WIDTH, BF16, True), (NSA_KV_WIDTH, F32, False),
                                            (NSA_KV_WIDTH, F32, False), (NSA_KV_WIDTH, BF16, False),
                                            (NSA_KV_HEADS * V_ROWS, BF16, True), (NSA_KV_WIDTH, BF16, False),
                                            (NSA_KV_HEADS * V_ROWS, BF16, True), (LANES, F32, True)]
    return pl.pallas_call(
        functools.partial(_inproj_kernel, tiles_per_seq=tiles_per_seq),
        grid=(tokens // tm,),
        in_specs=[row(D_MODEL),
                  pl.BlockSpec((6, 1, 1, D_MODEL), lambda i: (0, i // tiles_per_seq, 0, 0)),
                  pl.BlockSpec((1, D_MODEL, N_COLS), lambda i: (layer, 0, 0)),
                  pl.BlockSpec((1, WT_ROWS, D_MODEL), lambda i: (layer, 0, 0)),
                  row(LANES), row(LANES), col(HEAD_DIM // 2), col(HEAD_DIM // 2),
                  pl.BlockSpec((1, CONV_K, CONV_WIDTH), lambda i: (layer, 0, 0))],
        out_specs=[col(w) if t else row(w) for w, _, t in outs],
        out_shape=[jax.ShapeDtypeStruct((w, tokens) if t else (tokens, w), dt) for w, dt, t in outs],
        scratch_shapes=[pltpu.VMEM((8, CONV_WIDTH), F32)],
        compiler_params=_params("arbitrary"),
        name="ln_inproj",
    )(x, mod_l, w_l, w_t, cos, sin, cos_t, sin_t, conv_w)


def _retention_kernel(q_ref, k_ref, v_ref, g_ref, intra_ref, qdec_ref, kdec_ref, cdec_ref, gn_ref,
                      o_ref, state_ref):
    @pl.when(pl.program_id(1) == 0)
    def _():
        state_ref[...] = jnp.zeros_like(state_ref)

    n_pairs = RET_WIDTH // LANES
    lane = lax.broadcasted_iota(jnp.int32, (RET_CHUNK, LANES), 1)
    low = lane < HEAD_DIM
    blk_r = lax.broadcasted_iota(jnp.int32, (LANES, LANES), 0) < HEAD_DIM
    blk_c = lax.broadcasted_iota(jnp.int32, (LANES, LANES), 1) < HEAD_DIM
    same_head = blk_r == blk_c
    states = {(b, p): state_ref[b, p] for b in range(q_ref.shape[0]) for p in range(n_pairs)}
    for ch, b in [(ch, b) for ch in range(q_ref.shape[1] // RET_CHUNK) for b in range(q_ref.shape[0])]:
        rows = slice(ch * RET_CHUNK, (ch + 1) * RET_CHUNK)
        outs = []
        for p in range(n_pairs):
            cols = slice(p * LANES, (p + 1) * LANES)
            qp = q_ref[b, rows, cols]
            kp = k_ref[b, rows, cols]
            vp = v_ref[b, rows, cols]
            out = jnp.zeros((RET_CHUNK, LANES), F32)
            for hh in range(2):
                keep = low if hh == 0 else jnp.logical_not(low)
                qm = jnp.where(keep, qp, jnp.zeros_like(qp))
                s = _dot_nt(qm, kp) * intra_ref[2 * p + hh]
                out = jnp.where(keep, _dot(s.astype(BF16), vp), out)
            state = states[b, p]
            qd = (qp.astype(F32) * qdec_ref[:, cols]).astype(BF16)
            out = out + _dot(qd, state.astype(BF16))
            kd_t = (kp.astype(F32) * kdec_ref[:, cols]).T.astype(BF16)
            upd = _dot(kd_t, vp)
            states[b, p] = state * cdec_ref[p] + jnp.where(same_head, upd, 0.0)
            def half_mean(t):
                lo_sum = jnp.sum(jnp.where(low, t, 0.0), axis=-1, keepdims=True)
                hi_sum = jnp.sum(jnp.where(low, 0.0, t), axis=-1, keepdims=True)
                return jnp.where(low, lo_sum, hi_sum) * (1.0 / HEAD_DIM)
            oc = out - half_mean(out)
            outs.append(oc * lax.rsqrt(half_mean(oc * oc) + LN_EPS))
        normed = jnp.concatenate(outs, axis=1)
        o_ref[b, rows, :] = (normed * gn_ref[0] * g_ref[b, rows, :].astype(F32)).astype(BF16)
    for (b, p), state in states.items():
        state_ref[b, p] = state


def _retention_consts():
    heads = jnp.arange(RET_HEADS, dtype=F32)
    log_gamma = jnp.log(1.0 - jnp.power(2.0, -5.0 - heads))
    i = jnp.arange(RET_CHUNK, dtype=F32)
    diff = i[:, None] - i[None, :]
    intra = jnp.where(diff >= 0, jnp.exp(diff * log_gamma[:, None, None]), 0.0)
    q_dec = jnp.exp((i + 1.0) * log_gamma[:, None])
    k_dec = jnp.exp((RET_CHUNK - 1.0 - i) * log_gamma[:, None])
    c_dec = jnp.exp(RET_CHUNK * log_gamma)
    expand = lambda t: jnp.repeat(t.T, HEAD_DIM, axis=1)
    c_rows = jnp.repeat(c_dec, HEAD_DIM).reshape(RET_WIDTH // LANES, LANES, 1)
    c_blk = jnp.broadcast_to(c_rows, (RET_WIDTH // LANES, LANES, LANES))
    return intra, expand(q_dec), expand(k_dec), c_blk


def _retention(rq, rk, rv, rg, consts, gn_w, layer, batch, seq):
    intra, q_dec, k_dec, c_blk = consts
    rows = 2 * RET_CHUNK
    steps = seq // rows
    group = RET_BATCH if batch % RET_BATCH == 0 else 1
    blk = pl.BlockSpec((group, rows, RET_WIDTH), lambda b, s: (b, s, 0))
    full = lambda a: pl.BlockSpec(a.shape, lambda b, s: (0,) * a.ndim)
    per_seq = lambda t: t.reshape(batch, seq, RET_WIDTH)
    out = pl.pallas_call(
        _retention_kernel,
        grid=(batch // group, steps),
        in_specs=[blk, blk, blk, blk, full(intra), full(q_dec), full(k_dec), full(c_blk),
                  pl.BlockSpec((1, 1, RET_WIDTH), lambda b, s: (layer, 0, 0))],
        out_specs=blk,
        out_shape=jax.ShapeDtypeStruct((batch, seq, RET_WIDTH), BF16),
        scratch_shapes=[pltpu.VMEM((group, RET_WIDTH // LANES, LANES, LANES), F32)],
        compiler_params=_params("arbitrary", "arbitrary"),
        name="retention",
    )(per_seq(rq), per_seq(rk), per_seq(rv), per_seq(rg), intra, q_dec, k_dec, c_blk, gn_w)
    return out.reshape(batch * seq, RET_WIDTH)


def _compress_kernel(xk_ref, xv_ref, posa_ref, posb_ref, w1a_ref, w1b_ref, w2_ref, kc_ref, vc_ref):
    n_grp = xk_ref.shape[0] // CMP_STRIDE
    for kv, (x_ref, o_ref) in enumerate(((xk_ref, kc_ref), (xv_ref, vc_ref))):
        ya = jnp.zeros((n_grp, NSA_KV_HEADS * CMP_HIDDEN), F32)
        yb = jnp.zeros((n_grp, NSA_KV_HEADS * CMP_HIDDEN), F32)
        for r in range(CMP_STRIDE):
            x = x_ref[pl.ds(r, n_grp, stride=CMP_STRIDE), :]
            cols = slice(r * NSA_KV_WIDTH, (r + 1) * NSA_KV_WIDTH)
            ya = ya + _dot((x + posa_ref[0, kv, :, cols]).astype(BF16), w1a_ref[0, kv, cols, :])
            yb = yb + _dot((x + posb_ref[0, kv, :, cols]).astype(BF16), w1b_ref[0, kv, cols, :])
        hidden = ya + pltpu.roll(yb, n_grp - 1, 0)
        act = jax.nn.gelu(hidden)
        out = _dot(act.astype(BF16), w2_ref[0, kv])
        o_ref[0] = (out if kv == 0 else out.T).astype(BF16)


def _compress_weights(cmp_pos, cmp_w1, cmp_w2):
    depth = cmp_w1.shape[0]
    eye = jnp.eye(NSA_KV_HEADS, dtype=F32)
    a = cmp_w1.reshape(depth, 2, 2, CMP_STRIDE, HEAD_DIM, CMP_HIDDEN)
    w1 = jnp.einsum('lkardj,hg->lkarhdgj', a, eye)
    w1 = w1.reshape(depth, 2, 2, CMP_STRIDE * NSA_KV_WIDTH, NSA_KV_HEADS * CMP_HIDDEN).astype(BF16)
    pos = cmp_pos.reshape(depth, 2, 2, CMP_STRIDE, 1, HEAD_DIM)
    pos = jnp.broadcast_to(pos, (depth, 2, 2, CMP_STRIDE, NSA_KV_HEADS, HEAD_DIM))
    pos = pos.reshape(depth, 2, 2, 1, CMP_STRIDE * NSA_KV_WIDTH)
    w2 = jnp.einsum('lkje,hg->lkhjge', cmp_w2, eye)
    w2 = w2.reshape(depth, 2, NSA_KV_HEADS * CMP_HIDDEN, NSA_KV_WIDTH).astype(BF16)
    return w1[:, :, 0], w1[:, :, 1], pos[:, :, 0], pos[:, :, 1], w2


def _compress(kcx, vcx, cw, layer, batch, seq):
    w1a, w1b, posa, posb, w2 = cw
    n_grp = seq // CMP_STRIDE
    xblk = pl.BlockSpec((seq, NSA_KV_WIDTH), lambda b: (b, 0))
    lay = lambda a: pl.BlockSpec((1,) + a.shape[1:], lambda b: (layer,) + (0,) * (a.ndim - 1))
    oblk = pl.BlockSpec((1, n_grp, NSA_KV_WIDTH), lambda b: (b, 0, 0))
    return pl.pallas_call(
        _compress_kernel,
        grid=(batch,),
        in_specs=[xblk, xblk, lay(posa), lay(posb), lay(w1a), lay(w1b), lay(w2)],
        out_specs=[oblk, oblk],
        out_shape=[jax.ShapeDtypeStruct((batch, n_grp, NSA_KV_WIDTH), BF16)] * 2,
        compiler_params=_params("arbitrary"),
        name="nsa_compress",
    )(kcx, vcx, posa, posb, w1a, w1b, w2)


def _nsa_kernel(qu_ref, qr_ref, kc_ref, vc_ref, ks_ref, vs_ref, kw_ref, vw_ref, ng_ref, ov_ref, wbias_ref, ind_ref,
                o_ref):
    qb = pl.program_id(1)
    q0 = qb * Q_TILE
    qb_last = (qb + 1) * (Q_TILE // Q_BLOCK) - 1
    n_sel = ov_ref.shape[0]
    cols4 = NSA_GROUP * Q_TILE
    n_past = WINDOW // Q_BLOCK
    gsig = jax.nn.sigmoid(ng_ref[...])

    def padded_q(ref, h):
        zeros = jnp.zeros((HEAD_DIM, Q_TILE), BF16)
        parts = []
        for g in range(NSA_GROUP):
            hq = h * NSA_GROUP + g
            x = ref[hq * HEAD_DIM:(hq + 1) * HEAD_DIM, :]
            parts.append(jnp.concatenate([x, zeros] if h == 0 else [zeros, x], axis=0))
        return jnp.concatenate(parts, axis=1)

    def tile4(t):
        return jnp.concatenate([t] * NSA_GROUP, axis=1)

    def attend(state, keys, values_ref, k0, bias, q_t):
        m, acc = state
        s = _dot(keys, q_t)
        if bias is not None:
            s = s + bias
        m_new = jnp.maximum(m, jnp.max(s, axis=0, keepdims=True))
        e = jnp.exp2(s - m_new).astype(BF16)
        pv = [_dot(values_ref[h * V_ROWS:(h + 1) * V_ROWS, pl.ds(k0, keys.shape[0])],
                   e[:, h * cols4:(h + 1) * cols4]) for h in range(NSA_KV_HEADS)]
        return m_new, jnp.exp2(m - m_new) * acc + jnp.concatenate(pv, axis=1)

    def normalised(state):
        acc = state[1]
        return acc[0:HEAD_DIM] * (1.0 / acc[HEAD_DIM:HEAD_DIM + 1])

    cols_all = NSA_KV_HEADS * cols4
    init = (jnp.full((1, cols_all), NEG_INF, F32), jnp.zeros((V_ROWS, cols_all), F32))
    q_rot = [padded_q(qr_ref, h) for h in range(NSA_KV_HEADS)]

    def position_bias(kb, windowed):
        first = qb * (Q_TILE // Q_BLOCK)
        return jnp.concatenate(
            [wbias_ref[jnp.clip(kb - (first + sub) + n_past + 1, 0 if windowed else 2, n_past + 2)]
             for sub in range(Q_TILE // Q_BLOCK)] * (NSA_GROUP * NSA_KV_HEADS), axis=1)

    o_cmp = []
    q_sel = []
    for h in range(NSA_KV_HEADS):
        hr = slice(h * HEAD_DIM, (h + 1) * HEAD_DIM)
        qus = padded_q(qu_ref, h)

        s = _dot(kc_ref[0], qus)
        cid = lax.broadcasted_iota(jnp.int32, (LANES, Q_TILE), 0)
        tq = q0 + lax.broadcasted_iota(jnp.int32, (LANES, Q_TILE), 1)
        cvalid = tile4(jnp.where(cid * CMP_STRIDE + (CMP_LEN - 1) <= tq, 1.0, 0.0))
        sm = jnp.where(cvalid > 0.5, s, NEG_INF)
        e = jnp.exp2(sm - jnp.max(sm, axis=0, keepdims=True)) * cvalid
        l = jnp.sum(e, axis=0, keepdims=True)
        p = e * (1.0 / jnp.where(l > 0.0, l, 1.0))
        o_cmp.append(_dot(vc_ref[0, hr, :], p.astype(BF16)))

        psum = p[:, 0:Q_TILE]
        for g in range(1, NSA_GROUP):
            psum = psum + p[:, g * Q_TILE:(g + 1) * Q_TILE]
        p_hi = psum.astype(BF16)
        rem = psum - p_hi.astype(F32)
        p_mid = rem.astype(BF16)
        p_lo = (rem - p_mid.astype(F32)).astype(BF16)
        ov = ov_ref[...]
        imp = _dot(ov, p_hi) + _dot(ov, p_mid) + _dot(ov, p_lo)
        jid = lax.broadcasted_iota(jnp.int32, (n_sel, Q_TILE), 0)
        tid = q0 + lax.broadcasted_iota(jnp.int32, (n_sel, Q_TILE), 1)
        forced = (jid == 0) | (jid == jnp.right_shift(tid, 6))
        imp = jnp.where(forced, FORCE_SCORE, jnp.where(jid * SEL_LEN <= tid, imp, -FORCE_SCORE))
        rank = jnp.zeros((n_sel, Q_TILE), F32)
        for i in range(n_sel):
            ri = imp[i:i + 1, :]
            beats = (ri > imp) | ((ri == imp) & (jid > i))
            rank = rank + jnp.where(beats, 1.0, 0.0)
        sel_bias = tile4(jnp.where(rank < float(SEL_TOP), 0.0, NEG_INF)).astype(BF16)
        pad = jnp.zeros((LANES - n_sel, cols4), BF16)
        q_sel.append(jnp.concatenate([q_rot[h], sel_bias, pad], axis=0))

    q_sel_all = jnp.concatenate(q_sel, axis=1)
    q_rot_all = jnp.concatenate(q_rot, axis=1)

    def sel_step(gi, state, last):
        k0 = pl.multiple_of(gi * (SEL_GROUP * Q_BLOCK), SEL_GROUP * Q_BLOCK)
        keys = jnp.concatenate([ks_ref[pl.ds(k0, SEL_GROUP * Q_BLOCK), :], ind_ref[gi]], axis=1)
        bias = None
        if last:
            bias = jnp.concatenate([position_bias(gi * SEL_GROUP + u, False) for u in range(SEL_GROUP)], axis=0)
        return attend(state, keys, vs_ref, k0, bias, q_sel_all)

    last_group = qb_last // SEL_GROUP
    sel_state = lax.fori_loop(0, last_group, functools.partial(sel_step, last=False), init)
    o_sel = normalised(sel_step(last_group, sel_state, True))

    ws = pl.multiple_of(jnp.maximum(q0 - WINDOW, 0), Q_BLOCK)
    n_win = n_past + Q_TILE // Q_BLOCK
    win_bias = jnp.concatenate([position_bias(ws // Q_BLOCK + u, True) for u in range(n_win)], axis=0)
    win_keys = kw_ref[pl.ds(ws, n_win * Q_BLOCK), :]
    o_win = normalised(attend(init, win_keys, vw_ref, ws, win_bias, q_rot_all))
    heads = []
    for h in range(NSA_KV_HEADS):
        o_c = o_cmp[h]
        for g in range(NSA_GROUP):
            hq = h * NSA_GROUP + g
            c = slice(g * Q_TILE, (g + 1) * Q_TILE)
            ca = slice(hq * Q_TILE, (hq + 1) * Q_TILE)
            heads.append(gsig[3 * hq:3 * hq + 1, :] * o_c[:, c] + gsig[3 * hq + 1:3 * hq + 2, :] * o_sel[:, ca]
                         + gsig[3 * hq + 2:3 * hq + 3, :] * o_win[:, ca])
    cols_out = [jnp.concatenate(heads[2 * m:2 * m + 2], axis=0).T for m in range(NSA_HEADS // 2)]
    o_ref[...] = jnp.concatenate(cols_out, axis=1).astype(BF16)


def _nsa_consts(seq):
    n_cmp_pad = seq // CMP_STRIDE
    n_sel = seq // SEL_LEN
    c = np.arange(n_cmp_pad)
    j = np.arange(n_sel)
    n_cmp = (seq - CMP_LEN) // CMP_STRIDE + 1
    ov = ((c[None, :] * CMP_STRIDE < j[:, None] * SEL_LEN + SEL_LEN) &
          (j[:, None] * SEL_LEN <= c[None, :] * CMP_STRIDE + CMP_LEN - 1) & (c[None, :] < n_cmp))
    kl = np.arange(Q_BLOCK)[:, None]
    tl = np.arange(Q_BLOCK)[None, :]
    n_past = WINDOW // Q_BLOCK
    none = np.zeros((Q_BLOCK, Q_BLOCK), bool)
    valid = [none, kl > tl] + [np.ones((Q_BLOCK, Q_BLOCK), bool)] * (n_past - 1) + [kl <= tl, none]
    wbias = np.where(np.stack(valid), 0.0, NEG_INF).astype(np.float32)
    key_block = np.arange(seq) // SEL_LEN
    onehot = (key_block[:, None] == np.arange(LANES)[None, :]).reshape(-1, SEL_GROUP * Q_BLOCK, LANES)
    return jnp.asarray(ov, BF16), jnp.asarray(wbias), jnp.asarray(onehot, BF16)


def _nsa(qu, qr, kc, vc, ks, vs, kw, vw, ng, consts, batch, seq):
    ov, wbias, onehot = consts
    n_qb = seq // Q_TILE
    tokens = batch * seq
    qblk = pl.BlockSpec((NSA_WIDTH, Q_TILE), lambda b, q: (0, b * n_qb + q))
    cblk = pl.BlockSpec((1, seq // CMP_STRIDE, NSA_KV_WIDTH), lambda b, q: (b, 0, 0))
    kblk = pl.BlockSpec((seq, NSA_KV_WIDTH), lambda b, q: (b, 0))
    vblk = pl.BlockSpec((NSA_KV_HEADS * V_ROWS, seq), lambda b, q: (0, b))
    full = lambda a: pl.BlockSpec(a.shape, lambda b, q: (0,) * a.ndim)
    return pl.pallas_call(
        _nsa_kernel,
        grid=(batch, n_qb),
        in_specs=[qblk, qblk, cblk, cblk, kblk, vblk, kblk, vblk,
                  pl.BlockSpec((LANES, Q_TILE), lambda b, q: (0, b * n_qb + q)), full(ov), full(wbias),
                  full(onehot)],
        out_specs=pl.BlockSpec((Q_TILE, NSA_WIDTH), lambda b, q: (b * n_qb + q, 0)),
        out_shape=jax.ShapeDtypeStruct((tokens, NSA_WIDTH), BF16),
        compiler_params=_params("arbitrary", "arbitrary"),
        name="nsa_attention",
    )(qu, qr, kc, vc, ks, vs, kw, vw, ng, ov, wbias, onehot)


def _outproj_kernel(yr_ref, yc_ref, yn_ref, x_ref, mod_ref, w_ref, lng_ref, lnb_ref, rwh_ref, rwl_ref, rb_ref,
                    x1_ref, h2a_ref, h2b_ref, ti_ref, tw_ref, cnt_ref, *, alpha):
    tm = x_ref.shape[0]
    mix = (_dot(yr_ref[...], w_ref[0, 0:RET_WIDTH, :])
           + _dot(yc_ref[...], w_ref[0, RET_WIDTH:RET_WIDTH + CONV_WIDTH, :])
           + _dot(yn_ref[...], w_ref[0, RET_WIDTH + CONV_WIDTH:, :]))
    x1 = _layer_norm(alpha * x_ref[...] + (1.0 + mod_ref[2, 0]) * mix) * lng_ref[0, 0] + lnb_ref[0, 0]
    x1_ref[...] = x1
    h2 = _layer_norm(x1) * (1.0 + mod_ref[4, 0]) + mod_ref[3, 0]
    for j, part_ref in enumerate((h2a_ref, h2b_ref)):
        part_ref[...] = _pack_pairs(h2[:, j * 2 * SPLIT_WIDTH:(j + 1) * 2 * SPLIT_WIDTH])
    h_hi = h2.astype(BF16)
    h_lo = (h2 - h_hi.astype(F32)).astype(BF16)
    logits = _dot(h_hi, rwh_ref[0]) + _dot(h_lo, rwh_ref[0]) + _dot(h_hi, rwl_ref[0]) + rb_ref[0]

    lane = lax.broadcasted_iota(jnp.int32, (tm, LANES), 1)
    lanef = lane.astype(F32)
    rest = jnp.where(lane < N_EXPERTS, logits, -jnp.inf)
    vals, idxs = [], []
    for _ in range(TOP_K):
        top = jnp.max(rest, axis=-1, keepdims=True)
        idx = jnp.min(jnp.where(rest == top, lanef, float(LANES)), axis=-1, keepdims=True)
        vals.append(top)
        idxs.append(idx)
        rest = jnp.where(lanef == idx, -jnp.inf, rest)
    exps = [jnp.exp(v - vals[0]) for v in vals]
    inv = 1.0 / functools.reduce(lambda a, b: a + b, exps)
    top_w = jnp.zeros((tm, LANES), F32)
    top_i = jnp.zeros((tm, LANES), F32)
    member = jnp.zeros((tm, LANES), F32)
    for k in range(TOP_K):
        top_w = jnp.where(lane == k, exps[k] * inv, top_w)
        top_i = jnp.where(lane == k, idxs[k], top_i)
        member = member + jnp.where(lanef == idxs[k], 1.0, 0.0)
    tw_ref[...] = top_w
    ti_ref[...] = top_i.astype(jnp.int32)

    @pl.when(pl.program_id(0) == 0)
    def _():
        cnt_ref[...] = jnp.zeros_like(cnt_ref)

    cnt_ref[...] += jnp.broadcast_to(jnp.sum(member, axis=0, keepdims=True), cnt_ref.shape)


def _out_projection(y_ret, y_conv, y_nsa, x, mod_l, w_out, ln_g, ln_b, rw_hi, rw_lo, rb, layer, seq, alpha):
    tokens = x.shape[0]
    tm = ROW_TILE
    tiles_per_seq = seq // tm
    row = lambda w: pl.BlockSpec((tm, w), lambda i: (i, 0))
    lay3 = lambda a: pl.BlockSpec((1,) + a.shape[1:], lambda i: (layer,) + (0,) * (a.ndim - 1))
    return pl.pallas_call(
        functools.partial(_outproj_kernel, alpha=alpha),
        grid=(tokens // tm,),
        in_specs=[row(RET_WIDTH), row(CONV_WIDTH), row(NSA_WIDTH), row(D_MODEL),
                  pl.BlockSpec((6, 1, 1, D_MODEL), lambda i: (0, i // tiles_per_seq, 0, 0)),
                  lay3(w_out),
                  pl.BlockSpec((1, 1, 1, D_MODEL), lambda i: (layer, 0, 0, 0)),
                  pl.BlockSpec((1, 1, 1, D_MODEL), lambda i: (layer, 0, 0, 0)),
                  lay3(rw_hi), lay3(rw_lo), lay3(rb)],
        out_specs=[row(D_MODEL)] + [row(SPLIT_WIDTH)] * SC_SPLIT + [row(LANES), row(LANES),
                   pl.BlockSpec((8, LANES), lambda i: (0, 0))],
        out_shape=[jax.ShapeDtypeStruct((tokens, D_MODEL), F32)]
                  + [jax.ShapeDtypeStruct((tokens, SPLIT_WIDTH), jnp.uint32)] * SC_SPLIT
                  + [jax.ShapeDtypeStruct((tokens, LANES), jnp.int32), jax.ShapeDtypeStruct((tokens, LANES), F32),
                   jax.ShapeDtypeStruct((8, LANES), F32)],
        compiler_params=_params("arbitrary"),
        name="outproj_norm_router",
    )(y_ret, y_conv, y_nsa, x, mod_l, w_out, ln_g, ln_b, rw_hi, rw_lo, rb)


def _route_kernel(ti_ref, pstart_ref, tri_ref, pos_ref, carry_ref):
    @pl.when(pl.program_id(0) == 0)
    def _():
        carry_ref[...] = jnp.zeros_like(carry_ref)

    tm = ti_ref.shape[0]
    lane = lax.broadcasted_iota(jnp.int32, (tm, LANES), 1)
    top_i = ti_ref[...]
    onehots = [lane == top_i[:, k:k + 1] for k in range(TOP_K)]
    member = functools.reduce(lambda a, b: a + b, [jnp.where(o, 1.0, 0.0) for o in onehots])
    base = pstart_ref[...] + carry_ref[0:1] + _dot(tri_ref[...], member.astype(BF16))
    pos = jnp.zeros((tm, LANES), F32)
    for k in range(TOP_K):
        pos_k = jnp.sum(jnp.where(onehots[k], base, 0.0), axis=-1, keepdims=True)
        pos = jnp.where(lane == k, pos_k, pos)
    pos_ref[...] = pos.astype(jnp.int32)
    carry_ref[...] += jnp.broadcast_to(jnp.sum(member, axis=0, keepdims=True), carry_ref.shape)


def _route_positions(top_i, pstart):
    tokens = top_i.shape[0]
    tm = min(ROUTE_TILE, tokens)
    r = np.arange(tm)
    tri = jnp.asarray(r[None, :] < r[:, None], BF16)
    return pl.pallas_call(
        _route_kernel,
        grid=(tokens // tm,),
        in_specs=[pl.BlockSpec((tm, LANES), lambda i: (i, 0)),
                  pl.BlockSpec((1, LANES), lambda i: (0, 0)),
                  pl.BlockSpec((tm, tm), lambda i: (0, 0))],
        out_specs=pl.BlockSpec((tm, LANES), lambda i: (i, 0)),
        out_shape=jax.ShapeDtypeStruct((tokens, LANES), jnp.int32),
        scratch_shapes=[pltpu.VMEM((8, LANES), F32)],
        compiler_params=_params("arbitrary"),
        name="route_positions",
    )(top_i, pstart, tri)


def _block_table(counts, n_blocks):
    tm = MOE_TILE
    cnt = counts[0, :N_EXPERTS].astype(jnp.int32)
    nblk = (cnt + tm - 1) // tm
    bend = jnp.cumsum(nblk)
    bstart = bend - nblk
    blocks = jnp.arange(n_blocks, dtype=jnp.int32)
    block_e = jnp.minimum(jnp.sum((bend[None, :] <= blocks[:, None]).astype(jnp.int32), axis=1), N_EXPERTS - 1)
    n_valid = jnp.clip(cnt[block_e] - (blocks - bstart[block_e]) * tm, 0, tm).astype(jnp.int32)
    n_valid = jnp.where(blocks < bend[-1], n_valid, 0)
    pstart = jnp.pad((bstart * tm).astype(F32), (0, LANES - N_EXPERTS))[None, :]
    first = (blocks < bend[-1]) & ((blocks == 0) | (block_e != jnp.roll(block_e, 1)))
    slot = (jnp.cumsum(first.astype(jnp.int32)) - 1) % 2
    experts = jnp.arange(N_EXPERTS, dtype=jnp.int32)
    later = jnp.where((nblk[None, :] > 0) & (experts[None, :] > experts[:, None]), experts[None, :], N_EXPERTS)
    next_expert = jnp.min(later, axis=1)
    next_expert = jnp.where(next_expert < N_EXPERTS, next_expert, -1)[block_e]
    plan = jnp.stack([first.astype(jnp.int32), slot, next_expert]).astype(jnp.int32)
    return pstart, block_e.astype(jnp.int32), bend[-1:].astype(jnp.int32), n_valid, plan


def _sc_mesh():
    return plsc.VectorSubcoreMesh(core_axis_name="core", subcore_axis_name="subcore")


def _scatter_rows(xs, idx_t, n_rows):
    tokens, width = xs[0].shape
    n_idx = idx_t.shape[0]
    n_x = len(xs)
    win = SC_WINDOW

    @functools.partial(pl.kernel, out_type=[jax.ShapeDtypeStruct((n_rows, width), xs[0].dtype)] * n_x,
                       mesh=_sc_mesh(), scratch_types=[pltpu.SemaphoreType.DMA], name="dispatch_scatter")
    def scatter(*refs):
        x_hbms, i_hbm, o_hbms, sem = refs[:n_x], refs[n_x], refs[n_x + 1:-1], refs[-1]
        idx_specs = [pl.BlockSpec((1, win), functools.partial(lambda i, k: (k, i), k=k)) for k in range(n_idx)]
        for x_hbm, o_hbm in zip(x_hbms, o_hbms):
            def body(x_vmem, *i_vmems, o_hbm=o_hbm):
                copies = [pltpu.async_copy(x_vmem, o_hbm.at[i_vmem.at[0]], sem) for i_vmem in i_vmems]
                for copy in copies:
                    copy.wait()

            pltpu.emit_pipeline(
                body,
                grid=(tokens // win,),
                in_specs=[pl.BlockSpec((win, width), lambda i: (i, 0))] + idx_specs,
                out_specs=[],
                core_axis_name=("core", "subcore"),
                dimension_semantics=(pltpu.PARALLEL,),
            )(x_hbm, *([i_hbm] * n_idx))

    return scatter(*xs, idx_t)


def _gather_rows(xs, idx):
    width = xs[0].shape[1]
    n = idx.shape[0]
    n_x = len(xs)
    win = SC_WINDOW

    @functools.partial(pl.kernel, out_type=[jax.ShapeDtypeStruct((n, width), xs[0].dtype)] * n_x,
                       mesh=_sc_mesh(), scratch_types=[], name="combine_gather")
    def gather(*refs):
        x_hbms, i_hbm, o_hbms = refs[:n_x], refs[n_x], refs[n_x + 1:]
        for x_hbm, o_hbm in zip(x_hbms, o_hbms):
            def body(i_vmem, o_vmem, x_hbm=x_hbm):
                pltpu.sync_copy(x_hbm.at[i_vmem.at[0]], o_vmem)

            pltpu.emit_pipeline(
                body,
                grid=(n // win,),
                in_specs=[pl.BlockSpec((1, win), lambda i: (0, i))],
                out_specs=[pl.BlockSpec((win, width), lambda i: (i, 0))],
                core_axis_name=("core", "subcore"),
                dimension_semantics=(pltpu.PARALLEL,),
            )(i_hbm, o_hbm)

    return gather(*xs, idx.reshape(1, n))


def _moe_kernel(be_ref, nb_ref, nv_ref, plan_ref, ra_ref, rb_ref, wgu_hbm, bgu_ref, wd_hbm, bd_ref,
                oa_ref, ob_ref, wgu_f32, wd_f32, wgu_bf, wd_bf, sem, *, layer):
    row_refs = (ra_ref, rb_ref)
    out_refs = (oa_ref, ob_ref)
    i = pl.program_id(0)
    valid = i < nb_ref[0]

    def weight_copies(expert, slot):
        return (pltpu.make_async_copy(wgu_hbm.at[layer, expert], wgu_f32.at[slot], sem.at[0, slot]),
                pltpu.make_async_copy(wd_hbm.at[layer, expert], wd_f32.at[slot], sem.at[1, slot]))

    @pl.when(i == 0)
    def _():
        for copy in weight_copies(be_ref[0], 0):
            copy.start()

    @pl.when(valid & (plan_ref[0, i] == 1))
    def _():
        slot = plan_ref[1, i]
        for copy in weight_copies(be_ref[i], slot):
            copy.wait()
        upcoming = plan_ref[2, i]

        @pl.when(upcoming >= 0)
        def _():
            for copy in weight_copies(upcoming, 1 - slot):
                copy.start()

        wgu_bf[...] = wgu_f32[slot].astype(BF16)
        wd_bf[...] = wd_f32[slot].astype(BF16)

    def ffn(rs):
        live = rs.start + lax.broadcasted_iota(jnp.int32, (MOE_SUB, SPLIT_WIDTH), 0) < nv_ref[i]
        rows = jnp.concatenate(
            [half.astype(BF16) for r_ref in row_refs
             for half in _unpack_pairs(jnp.where(live, r_ref[rs, :], jnp.uint32(0)))], axis=1)
        acts = []
        for c in range(D_EXPERT // FFN_CHUNK):
            cg = slice(c * FFN_CHUNK, (c + 1) * FFN_CHUNK)
            cu = slice(D_EXPERT + c * FFN_CHUNK, D_EXPERT + (c + 1) * FFN_CHUNK)
            g = jnp.minimum(_dot(rows, wgu_bf[:, cg]) + bgu_ref[0, 0, :, cg], SWIGLU_LIMIT)
            u = jnp.clip(_dot(rows, wgu_bf[:, cu]) + bgu_ref[0, 0, :, cu], -SWIGLU_LIMIT, SWIGLU_LIMIT)
            acts.append(((u + 1.0) * (g * jax.nn.sigmoid(SWIGLU_ALPHA * g))).astype(BF16))
        y = _dot(jnp.concatenate(acts, axis=1), wd_bf[...]) + bd_ref[0, 0]
        for j, o_ref in enumerate(out_refs):
            o_ref[rs, :] = _pack_pairs(y[:, j * 2 * SPLIT_WIDTH:(j + 1) * 2 * SPLIT_WIDTH])

    for sub in range(MOE_TILE // MOE_SUB):
        rs = slice(sub * MOE_SUB, (sub + 1) * MOE_SUB)
        has_rows = valid & (nv_ref[i] > sub * MOE_SUB)
        pl.when(has_rows)(functools.partial(ffn, rs))

        @pl.when(jnp.logical_not(has_rows))
        def _(rs=rs):
            for o_ref in out_refs:
                o_ref[rs, :] = jnp.zeros((MOE_SUB, SPLIT_WIDTH), jnp.uint32)


def _expert_ffn(rows, block_e, n_used, n_valid, plan, w_gu, b_gu, w_down, b_down, layer):
    n_rows = rows[0].shape[0]
    tm = MOE_TILE
    part = pl.BlockSpec((tm, SPLIT_WIDTH), lambda i, be, nb, nv, pn: (i, 0))
    grid_spec = pltpu.PrefetchScalarGridSpec(
        num_scalar_prefetch=4,
        grid=(n_rows // tm,),
        in_specs=[part] * SC_SPLIT + [
                  pl.BlockSpec(memory_space=pl.ANY),
                  pl.BlockSpec((1, 1, 1, 2 * D_EXPERT), lambda i, be, nb, nv, pn: (layer, be[i], 0, 0)),
                  pl.BlockSpec(memory_space=pl.ANY),
                  pl.BlockSpec((1, 1, 1, D_MODEL), lambda i, be, nb, nv, pn: (layer, be[i], 0, 0))],
        out_specs=[part] * SC_SPLIT,
        scratch_shapes=[pltpu.VMEM((2, D_MODEL, 2 * D_EXPERT), F32), pltpu.VMEM((2, D_EXPERT, D_MODEL), F32),
                        pltpu.VMEM((D_MODEL, 2 * D_EXPERT), BF16), pltpu.VMEM((D_EXPERT, D_MODEL), BF16),
                        pltpu.SemaphoreType.DMA((2, 2))])
    return pl.pallas_call(
        functools.partial(_moe_kernel, layer=layer),
        grid_spec=grid_spec,
        out_shape=[jax.ShapeDtypeStruct((n_rows, SPLIT_WIDTH), jnp.uint32)] * SC_SPLIT,
        compiler_params=_params("arbitrary"),
        name="expert_ffn",
    )(block_e, n_used, n_valid, plan, *rows, w_gu, b_gu, w_down, b_down)


def _final_kernel(x_ref, tw_ref, mod_ref, lng_ref, lnb_ref, *rest, alpha):
    y_refs, o_ref = rest[:-1], rest[-1]
    top_w = tw_ref[...]
    parts = []
    for j in range(SC_SPLIT):
        acc = [0.0, 0.0]
        for k in range(TOP_K):
            for half, y in enumerate(_unpack_pairs(y_refs[j * TOP_K + k][...])):
                acc[half] = acc[half] + top_w[:, k:k + 1] * y
        parts += acc
    ffn = jnp.concatenate(parts, axis=1)
    o_ref[...] = (_layer_norm(alpha * x_ref[...] + (1.0 + mod_ref[5, 0]) * ffn) * lng_ref[0, 0] + lnb_ref[0, 0])


def _final_norm(x1, y_parts, top_w, mod_l, ln_g, ln_b, layer, seq, alpha):
    tokens = x1.shape[0]
    tm = ROW_TILE
    tiles = tokens // tm
    tiles_per_seq = seq // tm
    row = lambda w: pl.BlockSpec((tm, w), lambda i: (i, 0))
    vec = pl.BlockSpec((1, 1, 1, D_MODEL), lambda i: (layer, 1, 0, 0))
    y_specs, y_args = [], []
    for j in range(SC_SPLIT):
        for k in range(TOP_K):
            y_specs.append(pl.BlockSpec((tm, SPLIT_WIDTH), functools.partial(lambda i, k: (k * tiles + i, 0), k=k)))
            y_args.append(y_parts[j])
    return pl.pallas_call(
        functools.partial(_final_kernel, alpha=alpha),
        grid=(tiles,),
        in_specs=[row(D_MODEL), row(LANES),
                  pl.BlockSpec((6, 1, 1, D_MODEL), lambda i: (0, i // tiles_per_seq, 0, 0)), vec, vec] + y_specs,
        out_specs=row(D_MODEL),
        out_shape=jax.ShapeDtypeStruct(x1.shape, F32),
        compiler_params=_params("arbitrary"),
        name="combine_final_norm",
    )(x1, top_w, mod_l, ln_g, ln_b, *y_args)


def _inproj_weights(w_in):
    gate_cols = jnp.pad(w_in[:, :, OFF_NG:], ((0, 0), (0, 0), (0, LANES - N_GATE)))
    w_t = jnp.concatenate([w_in[:, :, off:off + width] for off, width in
                           ((OFF_NQ, NSA_WIDTH), (OFF_VS, NSA_KV_WIDTH), (OFF_VW, NSA_KV_WIDTH))] + [gate_cols],
                          axis=2).transpose(0, 2, 1)
    return w_in[:, :, :N_COLS].astype(BF16), w_t.astype(BF16)


def kernel(x, c, positions, w_in, w_out, ret_gn_w, conv_w, cmp_pos, cmp_w1, cmp_w2, ada_w, ada_b, ln_g, ln_b,
           router_w, router_b, w_gate_up, b_gate_up, w_down, b_down):
    batch, seq, _ = x.shape
    depth = w_in.shape[0]
    tokens = batch * seq
    n_rows = tokens * TOP_K + N_EXPERTS * MOE_TILE
    alpha = float((2 * depth) ** 0.25)

    w_in_r, w_in_t = _inproj_weights(w_in)
    w_out_b = w_out.astype(BF16)
    rw = jnp.pad(router_w, ((0, 0), (0, 0), (0, LANES - N_EXPERTS)))
    rw_hi = rw.astype(BF16)
    rw_lo = (rw - rw_hi.astype(F32)).astype(BF16)
    rb = jnp.pad(router_b, ((0, 0), (0, LANES - N_EXPERTS))).reshape(depth, 1, LANES)
    cw = _compress_weights(cmp_pos, cmp_w1, cmp_w2)
    ret_consts = _retention_consts()
    nsa_consts = _nsa_consts(seq)
    gn_w = ret_gn_w.reshape(depth, 1, RET_WIDTH)
    ln_g4 = ln_g.reshape(depth, 2, 1, D_MODEL)
    ln_b4 = ln_b.reshape(depth, 2, 1, D_MODEL)
    b_gu4 = b_gate_up.reshape(depth, N_EXPERTS, 1, 2 * D_EXPERT)
    b_d4 = b_down.reshape(depth, N_EXPERTS, 1, D_MODEL)

    mod = _modulation(c, ada_w, ada_b)
    cos, sin, cos_t, sin_t = _rope_tables(positions)
    xt = x.reshape(tokens, D_MODEL)
    for l in range(depth):
        (rq, rk, rv, rg, y_conv, qu, qr, kcx, vcx, ks, vs, kw, vw, ng) = _input_projection(
            xt, mod[l], w_in_r, w_in_t, cos, sin, cos_t, sin_t, conv_w, l, seq)
        y_ret = _retention(rq, rk, rv, rg, ret_consts, gn_w, l, batch, seq)
        kc, vc = _compress(kcx, vcx, cw, l, batch, seq)
        y_nsa = _nsa(qu, qr, kc, vc, ks, vs, kw, vw, ng, nsa_consts, batch, seq)
        x1, *h2, top_i, top_w, counts = _out_projection(y_ret, y_conv, y_nsa, xt, mod[l], w_out_b, ln_g4, ln_b4,
                                                        rw_hi, rw_lo, rb, l, seq, alpha)
        pstart, block_e, n_used, n_valid, plan = _block_table(counts, n_rows // MOE_TILE)
        pos_t = _route_positions(top_i, pstart)[:, :TOP_K].T
        rows = _scatter_rows(h2, pos_t, n_rows)
        y = _expert_ffn(rows, block_e, n_used, n_valid, plan, w_gate_up, b_gu4, w_down, b_d4, l)
        y_tok = _gather_rows(y, pos_t.reshape(-1))
        xt = _final_norm(x1, y_tok, top_w, mod[l], ln_g4, ln_b4, l, seq, alpha)
    return xt.reshape(batch, seq, D_MODEL)
```

```python
import functools

import numpy as np
import jax
import jax.numpy as jnp
from jax import lax
from jax.experimental import pallas as pl
from jax.experimental.pallas import tpu as pltpu
from jax.experimental.pallas import tpu_sc as plsc

F32 = jnp.float32
BF16 = jnp.bfloat16

D_MODEL = 1024
HEAD_DIM = 64
RET_WIDTH = 256
RET_HEADS = 4
RET_CHUNK = 128
CONV_WIDTH = 256
CONV_K = 3
NSA_WIDTH = 512
NSA_HEADS = 8
NSA_KV_HEADS = 2
NSA_GROUP = 4
NSA_KV_WIDTH = 128
CMP_LEN = 32
CMP_STRIDE = 16
CMP_HIDDEN = 128
SEL_LEN = 64
SEL_TOP = 8
WINDOW = 512
Q_BLOCK = 128
Q_TILE = 2 * Q_BLOCK
ROPE_THETA = 10000.0
N_EXPERTS = 32
TOP_K = 4
D_EXPERT = 1024
SWIGLU_LIMIT = 7.0
SWIGLU_ALPHA = 1.702
LN_EPS = 1e-5
NEG_INF = -1e30
FORCE_SCORE = 1e9

LANES = 128
VMEM_LIMIT = 56 * 1024 * 1024

OFF_RQ, OFF_RK, OFF_RV, OFF_RG = 0, 256, 512, 768
OFF_CB, OFF_CC, OFF_CH = 1024, 1280, 1536
OFF_NQ = 1792
OFF_KC, OFF_VC = 2304, 2432
OFF_KS, OFF_VS, OFF_KW, OFF_VW = 2560, 2688, 2816, 2944
OFF_NG = 3072
N_GATE = NSA_HEADS * 3
N_COLS = OFF_NG
LOG2E = 1.4426950408889634
WT_NQ, WT_VS, WT_VW, WT_NG = 0, NSA_WIDTH, NSA_WIDTH + NSA_KV_WIDTH, NSA_WIDTH + 2 * NSA_KV_WIDTH
WT_ROWS = WT_NG + LANES

RET_BATCH = 4
ROW_TILE = 256
MOE_TILE = 512
MOE_SUB = 256
FFN_CHUNK = 256
ROUTE_TILE = 1024
SEL_GROUP = 4
V_ROWS = HEAD_DIM + 16
SC_WINDOW = LANES
SC_SPLIT = 2
SPLIT_WIDTH = D_MODEL // (2 * SC_SPLIT)


def _dot(a, b):
    return jnp.dot(a, b, preferred_element_type=F32)


def _dot_nt(a, b):
    return lax.dot_general(a, b, (((1,), (1,)), ((), ())), preferred_element_type=F32)


def _layer_norm(x):
    mu = jnp.mean(x, axis=-1, keepdims=True)
    xc = x - mu
    var = jnp.mean(xc * xc, axis=-1, keepdims=True)
    return xc * lax.rsqrt(var + LN_EPS)


def _pack_pairs(x):
    bits = lambda t: lax.bitcast_convert_type(t.astype(BF16).astype(F32), jnp.uint32)
    return (bits(x[:, SPLIT_WIDTH:]) & jnp.uint32(0xFFFF0000)) | (bits(x[:, :SPLIT_WIDTH]) >> 16)


def _unpack_pairs(p):
    low = lax.bitcast_convert_type(p << 16, F32)
    high = lax.bitcast_convert_type(p & jnp.uint32(0xFFFF0000), F32)
    return low, high


def _params(*sem):
    return pltpu.CompilerParams(dimension_semantics=sem, vmem_limit_bytes=VMEM_LIMIT)


def _mod_kernel(c_ref, w_ref, b_ref, o_ref):
    c = c_ref[...]
    ca = (c * jax.nn.sigmoid(c)).astype(BF16)
    o_ref[0, 0] = _dot(ca, w_ref[0].astype(BF16)) + b_ref[0]


def _modulation(c, ada_w, ada_b):
    depth = ada_w.shape[0]
    batch = c.shape[0]
    out = pl.pallas_call(
        _mod_kernel,
        grid=(depth, 6),
        in_specs=[pl.BlockSpec((batch, D_MODEL), lambda l, j: (0, 0)),
                  pl.BlockSpec((1, D_MODEL, D_MODEL), lambda l, j: (l, 0, j)),
                  pl.BlockSpec((1, 1, D_MODEL), lambda l, j: (l * 6 + j, 0, 0))],
        out_specs=pl.BlockSpec((1, 1, batch, D_MODEL), lambda l, j: (l, j, 0, 0)),
        out_shape=jax.ShapeDtypeStruct((depth, 6, batch, D_MODEL), F32),
        compiler_params=_params("arbitrary", "arbitrary"),
        name="adaln_mod",
    )(c, ada_w, ada_b.reshape(depth * 6, 1, D_MODEL))
    return out.reshape(depth, 6, batch, 1, D_MODEL)


def _rope_table_kernel(pos_ref, inv_ref, cos_ref, sin_ref):
    ang = pos_ref[...] * inv_ref[...]
    cos_ref[...] = jnp.cos(ang)
    sin_ref[...] = jnp.sin(ang)


def _rope_tables(positions):
    half = HEAD_DIM // 2
    per_row = LANES // half
    tokens = positions.size
    pos4 = jnp.repeat(positions.reshape(tokens // per_row, per_row).astype(F32), half, axis=1)
    inv = ROPE_THETA ** (-jnp.arange(half, dtype=F32) / half)
    inv4 = jnp.tile(inv, per_row)[None, :]
    rows = tokens // per_row
    tile = min(rows, 1024)
    cos4, sin4 = pl.pallas_call(
        _rope_table_kernel,
        grid=(rows // tile,),
        in_specs=[pl.BlockSpec((tile, LANES), lambda i: (i, 0)),
                  pl.BlockSpec((1, LANES), lambda i: (0, 0))],
        out_specs=[pl.BlockSpec((tile, LANES), lambda i: (i, 0))] * 2,
        out_shape=[jax.ShapeDtypeStruct((rows, LANES), F32)] * 2,
        compiler_params=_params("arbitrary"),
        name="rope_tables",
    )(pos4, inv4)
    cos32 = cos4.reshape(tokens, half)
    sin32 = sin4.reshape(tokens, half)
    cos = jnp.tile(cos32, (1, per_row))
    sign = jnp.tile(jnp.concatenate([-jnp.ones((half,), F32), jnp.ones((half,), F32)]), LANES // HEAD_DIM)
    sin = jnp.tile(sin32, (1, per_row)) * sign[None, :]
    return cos, sin, cos32.T, sin32.T


def _inproj_kernel(x_ref, mod_ref, w_ref, wt_ref, cos_ref, sin_ref, cost_ref, sint_ref, convw_ref,
                   rq_ref, rk_ref, rv_ref, rg_ref, yc_ref, qu_ref, qr_ref, kc_ref, vc_ref,
                   ks_ref, vs_ref, kw_ref, vw_ref, ng_ref, carry_ref, *, tiles_per_seq):
    i = pl.program_id(0)
    tm = x_ref.shape[0]
    h = (_layer_norm(x_ref[...]) * (1.0 + mod_ref[1, 0]) + mod_ref[0, 0]).astype(BF16)
    cosf = cos_ref[...]
    sinf = sin_ref[...]
    lane = lax.broadcasted_iota(jnp.int32, (tm, LANES), 1)
    first_half = (lane % HEAD_DIM) < (HEAD_DIM // 2)

    def proj(off, width):
        return _dot(h, w_ref[0, :, off:off + width])

    def rope(c):
        cols = []
        for j in range(c.shape[1] // LANES):
            cj = c[:, j * LANES:(j + 1) * LANES]
            swapped = jnp.where(first_half, pltpu.roll(cj, LANES - HEAD_DIM // 2, 1),
                                pltpu.roll(cj, HEAD_DIM // 2, 1))
            cols.append(cj * cosf + swapped * sinf)
        return jnp.concatenate(cols, axis=1) if len(cols) > 1 else cols[0]

    scale = HEAD_DIM ** -0.5
    rq_ref[...] = rope(proj(OFF_RQ, RET_WIDTH)).astype(BF16)
    rk_ref[...] = (rope(proj(OFF_RK, RET_WIDTH)) * scale).astype(BF16)
    rv_ref[...] = proj(OFF_RV, RET_WIDTH).astype(BF16)
    rg = proj(OFF_RG, RET_WIDTH)
    rg_ref[...] = (rg * jax.nn.sigmoid(rg)).astype(BF16)

    cb = proj(OFF_CB, CONV_WIDTH)
    u = proj(OFF_CC, CONV_WIDTH) * proj(OFF_CH, CONV_WIDTH)

    @pl.when(i % tiles_per_seq == 0)
    def _():
        carry_ref[...] = jnp.zeros_like(carry_ref)

    carry = carry_ref[...]
    row = lax.broadcasted_iota(jnp.int32, (tm, CONV_WIDTH), 0)
    prev1 = jnp.where(row == 0, carry[7:8], pltpu.roll(u, 1, 0))
    prev2 = jnp.where(row == 0, carry[6:7], jnp.where(row == 1, carry[7:8], pltpu.roll(u, 2, 0)))
    cw = convw_ref[0]
    yc_ref[...] = (cb * (cw[0:1] * prev2 + cw[1:2] * prev1 + cw[2:3] * u)).astype(BF16)
    carry_ref[...] = u[tm - 8:tm]

    def proj_t(off, width):
        return _dot_nt(wt_ref[0, off:off + width, :], h)

    nq_t = proj_t(WT_NQ, NSA_WIDTH) * (scale * LOG2E)
    qu_ref[...] = nq_t.astype(BF16)
    cos_t = cost_ref[...]
    sin_t = sint_ref[...]
    half = HEAD_DIM // 2
    rotated = []
    for hq in range(NSA_HEADS):
        t1 = nq_t[hq * HEAD_DIM:hq * HEAD_DIM + half]
        t2 = nq_t[hq * HEAD_DIM + half:(hq + 1) * HEAD_DIM]
        rotated += [t1 * cos_t - t2 * sin_t, t2 * cos_t + t1 * sin_t]
    qr_ref[...] = jnp.concatenate(rotated, axis=0).astype(BF16)
    kc_ref[...] = proj(OFF_KC, NSA_KV_WIDTH)
    vc_ref[...] = proj(OFF_VC, NSA_KV_WIDTH)

    def values_t(off):
        v_t = proj_t(off, NSA_KV_WIDTH)
        ones = jnp.ones((V_ROWS - HEAD_DIM, tm), F32)
        parts = []
        for hd in range(NSA_KV_HEADS):
            parts += [v_t[hd * HEAD_DIM:(hd + 1) * HEAD_DIM], ones]
        return jnp.concatenate(parts, axis=0).astype(BF16)

    ks_ref[...] = rope(proj(OFF_KS, NSA_KV_WIDTH)).astype(BF16)
    vs_ref[...] = values_t(WT_VS)
    kw_ref[...] = rope(proj(OFF_KW, NSA_KV_WIDTH)).astype(BF16)
    vw_ref[...] = values_t(WT_VW)
    ng_ref[...] = proj_t(WT_NG, LANES)


def _input_projection(x, mod_l, w_l, w_t, cos, sin, cos_t, sin_t, conv_w, layer, seq):
    tokens = x.shape[0]
    tm = ROW_TILE
    tiles_per_seq = seq // tm
    row = lambda w: pl.BlockSpec((tm, w), lambda i: (i, 0))
    col = lambda w: pl.BlockSpec((w, tm), lambda i: (0, i))
    outs = [(RET_WIDTH, BF16, False)] * 4 + [(CONV_WIDTH, BF16, False), (NSA_WIDTH, BF16, True),
                                            (NSA_WIDTH, BF16, True), (NSA_KV_WIDTH, F32, False),
                                            (NSA_KV_WIDTH, F32, False), (NSA_KV_WIDTH, BF16, False),
                                            (NSA_KV_HEADS * V_ROWS, BF16, True), (NSA_KV_WIDTH, BF16, False),
                                            (NSA_KV_HEADS * V_ROWS, BF16, True), (LANES, F32, True)]
    return pl.pallas_call(
        functools.partial(_inproj_kernel, tiles_per_seq=tiles_per_seq),
        grid=(tokens // tm,),
        in_specs=[row(D_MODEL),
                  pl.BlockSpec((6, 1, 1, D_MODEL), lambda i: (0, i // tiles_per_seq, 0, 0)),
                  pl.BlockSpec((1, D_MODEL, N_COLS), lambda i: (layer, 0, 0)),
                  pl.BlockSpec((1, WT_ROWS, D_MODEL), lambda i: (layer, 0, 0)),
                  row(LANES), row(LANES), col(HEAD_DIM // 2), col(HEAD_DIM // 2),
                  pl.BlockSpec((1, CONV_K, CONV_WIDTH), lambda i: (layer, 0, 0))],
        out_specs=[col(w) if t else row(w) for w, _, t in outs],
        out_shape=[jax.ShapeDtypeStruct((w, tokens) if t else (tokens, w), dt) for w, dt, t in outs],
        scratch_shapes=[pltpu.VMEM((8, CONV_WIDTH), F32)],
        compiler_params=_params("arbitrary"),
        name="ln_inproj",
    )(x, mod_l, w_l, w_t, cos, sin, cos_t, sin_t, conv_w)


def _retention_kernel(q_ref, k_ref, v_ref, g_ref, intra_ref, qdec_ref, kdec_ref, cdec_ref, gn_ref,
                      o_ref, state_ref):
    @pl.when(pl.program_id(1) == 0)
    def _():
        state_ref[...] = jnp.zeros_like(state_ref)

    n_pairs = RET_WIDTH // LANES
    lane = lax.broadcasted_iota(jnp.int32, (RET_CHUNK, LANES), 1)
    low = lane < HEAD_DIM
    blk_r = lax.broadcasted_iota(jnp.int32, (LANES, LANES), 0) < HEAD_DIM
    blk_c = lax.broadcasted_iota(jnp.int32, (LANES, LANES), 1) < HEAD_DIM
    same_head = blk_r == blk_c
    states = {(b, p): state_ref[b, p] for b in range(q_ref.shape[0]) for p in range(n_pairs)}
    for ch, b in [(ch, b) for ch in range(q_ref.shape[1] // RET_CHUNK) for b in range(q_ref.shape[0])]:
        rows = slice(ch * RET_CHUNK, (ch + 1) * RET_CHUNK)
        outs = []
        for p in range(n_pairs):
            cols = slice(p * LANES, (p + 1) * LANES)
            qp = q_ref[b, rows, cols]
            kp = k_ref[b, rows, cols]
            vp = v_ref[b, rows, cols]
            out = jnp.zeros((RET_CHUNK, LANES), F32)
            for hh in range(2):
                keep = low if hh == 0 else jnp.logical_not(low)
                qm = jnp.where(keep, qp, jnp.zeros_like(qp))
                s = _dot_nt(qm, kp) * intra_ref[2 * p + hh]
                out = jnp.where(keep, _dot(s.astype(BF16), vp), out)
            state = states[b, p]
            qd = (qp.astype(F32) * qdec_ref[:, cols]).astype(BF16)
            out = out + _dot(qd, state.astype(BF16))
            kd_t = (kp.astype(F32) * kdec_ref[:, cols]).T.astype(BF16)
            upd = _dot(kd_t, vp)
            states[b, p] = state * cdec_ref[p] + jnp.where(same_head, upd, 0.0)
            def half_mean(t):
                lo_sum = jnp.sum(jnp.where(low, t, 0.0), axis=-1, keepdims=True)
                hi_sum = jnp.sum(jnp.where(low, 0.0, t), axis=-1, keepdims=True)
                return jnp.where(low, lo_sum, hi_sum) * (1.0 / HEAD_DIM)
            oc = out - half_mean(out)
            outs.append(oc * lax.rsqrt(half_mean(oc * oc) + LN_EPS))
        normed = jnp.concatenate(outs, axis=1)
        o_ref[b, rows, :] = (normed * gn_ref[0] * g_ref[b, rows, :].astype(F32)).astype(BF16)
    for (b, p), state in states.items():
        state_ref[b, p] = state


def _retention_consts():
    heads = jnp.arange(RET_HEADS, dtype=F32)
    log_gamma = jnp.log(1.0 - jnp.power(2.0, -5.0 - heads))
    i = jnp.arange(RET_CHUNK, dtype=F32)
    diff = i[:, None] - i[None, :]
    intra = jnp.where(diff >= 0, jnp.exp(diff * log_gamma[:, None, None]), 0.0)
    q_dec = jnp.exp((i + 1.0) * log_gamma[:, None])
    k_dec = jnp.exp((RET_CHUNK - 1.0 - i) * log_gamma[:, None])
    c_dec = jnp.exp(RET_CHUNK * log_gamma)
    expand = lambda t: jnp.repeat(t.T, HEAD_DIM, axis=1)
    c_rows = jnp.repeat(c_dec, HEAD_DIM).reshape(RET_WIDTH // LANES, LANES, 1)
    c_blk = jnp.broadcast_to(c_rows, (RET_WIDTH // LANES, LANES, LANES))
    return intra, expand(q_dec), expand(k_dec), c_blk


def _retention(rq, rk, rv, rg, consts, gn_w, layer, batch, seq):
    intra, q_dec, k_dec, c_blk = consts
    rows = 2 * RET_CHUNK
    steps = seq // rows
    group = RET_BATCH if batch % RET_BATCH == 0 else 1
    blk = pl.BlockSpec((group, rows, RET_WIDTH), lambda b, s: (b, s, 0))
    full = lambda a: pl.BlockSpec(a.shape, lambda b, s: (0,) * a.ndim)
    per_seq = lambda t: t.reshape(batch, seq, RET_WIDTH)
    out = pl.pallas_call(
        _retention_kernel,
        grid=(batch // group, steps),
        in_specs=[blk, blk, blk, blk, full(intra), full(q_dec), full(k_dec), full(c_blk),
                  pl.BlockSpec((1, 1, RET_WIDTH), lambda b, s: (layer, 0, 0))],
        out_specs=blk,
        out_shape=jax.ShapeDtypeStruct((batch, seq, RET_WIDTH), BF16),
        scratch_shapes=[pltpu.VMEM((group, RET_WIDTH // LANES, LANES, LANES), F32)],
        compiler_params=_params("arbitrary", "arbitrary"),
        name="retention",
    )(per_seq(rq), per_seq(rk), per_seq(rv), per_seq(rg), intra, q_dec, k_dec, c_blk, gn_w)
    return out.reshape(batch * seq, RET_WIDTH)


def _compress_kernel(xk_ref, xv_ref, posa_ref, posb_ref, w1a_ref, w1b_ref, w2_ref, kc_ref, vc_ref):
    n_grp = xk_ref.shape[0] // CMP_STRIDE
    for kv, (x_ref, o_ref) in enumerate(((xk_ref, kc_ref), (xv_ref, vc_ref))):
        ya = jnp.zeros((n_grp, NSA_KV_HEADS * CMP_HIDDEN), F32)
        yb = jnp.zeros((n_grp, NSA_KV_HEADS * CMP_HIDDEN), F32)
        for r in range(CMP_STRIDE):
            x = x_ref[pl.ds(r, n_grp, stride=CMP_STRIDE), :]
            cols = slice(r * NSA_KV_WIDTH, (r + 1) * NSA_KV_WIDTH)
            ya = ya + _dot((x + posa_ref[0, kv, :, cols]).astype(BF16), w1a_ref[0, kv, cols, :])
            yb = yb + _dot((x + posb_ref[0, kv, :, cols]).astype(BF16), w1b_ref[0, kv, cols, :])
        hidden = ya + pltpu.roll(yb, n_grp - 1, 0)
        act = jax.nn.gelu(hidden)
        out = _dot(act.astype(BF16), w2_ref[0, kv])
        o_ref[0] = (out if kv == 0 else out.T).astype(BF16)


def _compress_weights(cmp_pos, cmp_w1, cmp_w2):
    depth = cmp_w1.shape[0]
    eye = jnp.eye(NSA_KV_HEADS, dtype=F32)
    a = cmp_w1.reshape(depth, 2, 2, CMP_STRIDE, HEAD_DIM, CMP_HIDDEN)
    w1 = jnp.einsum('lkardj,hg->lkarhdgj', a, eye)
    w1 = w1.reshape(depth, 2, 2, CMP_STRIDE * NSA_KV_WIDTH, NSA_KV_HEADS * CMP_HIDDEN).astype(BF16)
    pos = cmp_pos.reshape(depth, 2, 2, CMP_STRIDE, 1, HEAD_DIM)
    pos = jnp.broadcast_to(pos, (depth, 2, 2, CMP_STRIDE, NSA_KV_HEADS, HEAD_DIM))
    pos = pos.reshape(depth, 2, 2, 1, CMP_STRIDE * NSA_KV_WIDTH)
    w2 = jnp.einsum('lkje,hg->lkhjge', cmp_w2, eye)
    w2 = w2.reshape(depth, 2, NSA_KV_HEADS * CMP_HIDDEN, NSA_KV_WIDTH).astype(BF16)
    return w1[:, :, 0], w1[:, :, 1], pos[:, :, 0], pos[:, :, 1], w2


def _compress(kcx, vcx, cw, layer, batch, seq):
    w1a, w1b, posa, posb, w2 = cw
    n_grp = seq // CMP_STRIDE
    xblk = pl.BlockSpec((seq, NSA_KV_WIDTH), lambda b: (b, 0))
    lay = lambda a: pl.BlockSpec((1,) + a.shape[1:], lambda b: (layer,) + (0,) * (a.ndim - 1))
    oblk = pl.BlockSpec((1, n_grp, NSA_KV_WIDTH), lambda b: (b, 0, 0))
    return pl.pallas_call(
        _compress_kernel,
        grid=(batch,),
        in_specs=[xblk, xblk, lay(posa), lay(posb), lay(w1a), lay(w1b), lay(w2)],
        out_specs=[oblk, oblk],
        out_shape=[jax.ShapeDtypeStruct((batch, n_grp, NSA_KV_WIDTH), BF16)] * 2,
        compiler_params=_params("arbitrary"),
        name="nsa_compress",
    )(kcx, vcx, posa, posb, w1a, w1b, w2)


def _nsa_kernel(qu_ref, qr_ref, kc_ref, vc_ref, ks_ref, vs_ref, kw_ref, vw_ref, ng_ref, ov_ref, wbias_ref, ind_ref,
                o_ref):
    qb = pl.program_id(1)
    q0 = qb * Q_TILE
    qb_last = (qb + 1) * (Q_TILE // Q_BLOCK) - 1
    n_sel = ov_ref.shape[0]
    cols4 = NSA_GROUP * Q_TILE
    n_past = WINDOW // Q_BLOCK
    gsig = jax.nn.sigmoid(ng_ref[...])

    def padded_q(ref, h):
        zeros = jnp.zeros((HEAD_DIM, Q_TILE), BF16)
        parts = []
        for g in range(NSA_GROUP):
            hq = h * NSA_GROUP + g
            x = ref[hq * HEAD_DIM:(hq + 1) * HEAD_DIM, :]
            parts.append(jnp.concatenate([x, zeros] if h == 0 else [zeros, x], axis=0))
        return jnp.concatenate(parts, axis=1)

    def tile4(t):
        return jnp.concatenate([t] * NSA_GROUP, axis=1)

    def attend(state, keys, values_ref, k0, bias, q_t):
        m, acc = state
        s = _dot(keys, q_t)
        if bias is not None:
            s = s + bias
        m_new = jnp.maximum(m, jnp.max(s, axis=0, keepdims=True))
        e = jnp.exp2(s - m_new).astype(BF16)
        pv = [_dot(values_ref[h * V_ROWS:(h + 1) * V_ROWS, pl.ds(k0, keys.shape[0])],
                   e[:, h * cols4:(h + 1) * cols4]) for h in range(NSA_KV_HEADS)]
        return m_new, jnp.exp2(m - m_new) * acc + jnp.concatenate(pv, axis=1)

    def normalised(state):
        acc = state[1]
        return acc[0:HEAD_DIM] * (1.0 / acc[HEAD_DIM:HEAD_DIM + 1])

    cols_all = NSA_KV_HEADS * cols4
    init = (jnp.full((1, cols_all), NEG_INF, F32), jnp.zeros((V_ROWS, cols_all), F32))
    q_rot = [padded_q(qr_ref, h) for h in range(NSA_KV_HEADS)]

    def position_bias(kb, windowed):
        first = qb * (Q_TILE // Q_BLOCK)
        return jnp.concatenate(
            [wbias_ref[jnp.clip(kb - (first + sub) + n_past + 1, 0 if windowed else 2, n_past + 2)]
             for sub in range(Q_TILE // Q_BLOCK)] * (NSA_GROUP * NSA_KV_HEADS), axis=1)

    o_cmp = []
    q_sel = []
    for h in range(NSA_KV_HEADS):
        hr = slice(h * HEAD_DIM, (h + 1) * HEAD_DIM)
        qus = padded_q(qu_ref, h)

        s = _dot(kc_ref[0], qus)
        cid = lax.broadcasted_iota(jnp.int32, (LANES, Q_TILE), 0)
        tq = q0 + lax.broadcasted_iota(jnp.int32, (LANES, Q_TILE), 1)
        cvalid = tile4(jnp.where(cid * CMP_STRIDE + (CMP_LEN - 1) <= tq, 1.0, 0.0))
        sm = jnp.where(cvalid > 0.5, s, NEG_INF)
        e = jnp.exp2(sm - jnp.max(sm, axis=0, keepdims=True)) * cvalid
        l = jnp.sum(e, axis=0, keepdims=True)
        p = e * (1.0 / jnp.where(l > 0.0, l, 1.0))
        o_cmp.append(_dot(vc_ref[0, hr, :], p.astype(BF16)))

        psum = p[:, 0:Q_TILE]
        for g in range(1, NSA_GROUP):
            psum = psum + p[:, g * Q_TILE:(g + 1) * Q_TILE]
        p_hi = psum.astype(BF16)
        rem = psum - p_hi.astype(F32)
        p_mid = rem.astype(BF16)
        p_lo = (rem - p_mid.astype(F32)).astype(BF16)
        ov = ov_ref[...]
        imp = _dot(ov, p_hi) + _dot(ov, p_mid) + _dot(ov, p_lo)
        jid = lax.broadcasted_iota(jnp.int32, (n_sel, Q_TILE), 0)
        tid = q0 + lax.broadcasted_iota(jnp.int32, (n_sel, Q_TILE), 1)
        forced = (jid == 0) | (jid == jnp.right_shift(tid, 6))
        imp = jnp.where(forced, FORCE_SCORE, jnp.where(jid * SEL_LEN <= tid, imp, -FORCE_SCORE))
        rank = jnp.zeros((n_sel, Q_TILE), F32)
        for i in range(n_sel):
            ri = imp[i:i + 1, :]
            beats = (ri > imp) | ((ri == imp) & (jid > i))
            rank = rank + jnp.where(beats, 1.0, 0.0)
        sel_bias = tile4(jnp.where(rank < float(SEL_TOP), 0.0, NEG_INF)).astype(BF16)
        pad = jnp.zeros((LANES - n_sel, cols4), BF16)
        q_sel.append(jnp.concatenate([q_rot[h], sel_bias, pad], axis=0))

    q_sel_all = jnp.concatenate(q_sel, axis=1)
    q_rot_all = jnp.concatenate(q_rot, axis=1)

    def sel_step(gi, state, last):
        k0 = pl.multiple_of(gi * (SEL_GROUP * Q_BLOCK), SEL_GROUP * Q_BLOCK)
        keys = jnp.concatenate([ks_ref[pl.ds(k0, SEL_GROUP * Q_BLOCK), :], ind_ref[gi]], axis=1)
        bias = None
        if last:
            bias = jnp.concatenate([position_bias(gi * SEL_GROUP + u, False) for u in range(SEL_GROUP)], axis=0)
        return attend(state, keys, vs_ref, k0, bias, q_sel_all)

    last_group = qb_last // SEL_GROUP
    sel_state = lax.fori_loop(0, last_group, functools.partial(sel_step, last=False), init)
    o_sel = normalised(sel_step(last_group, sel_state, True))

    ws = pl.multiple_of(jnp.maximum(q0 - WINDOW, 0), Q_BLOCK)
    n_win = n_past + Q_TILE // Q_BLOCK
    win_bias = jnp.concatenate([position_bias(ws // Q_BLOCK + u, True) for u in range(n_win)], axis=0)
    win_keys = kw_ref[pl.ds(ws, n_win * Q_BLOCK), :]
    o_win = normalised(attend(init, win_keys, vw_ref, ws, win_bias, q_rot_all))
    heads = []
    for h in range(NSA_KV_HEADS):
        o_c = o_cmp[h]
        for g in range(NSA_GROUP):
            hq = h * NSA_GROUP + g
            c = slice(g * Q_TILE, (g + 1) * Q_TILE)
            ca = slice(hq * Q_TILE, (hq + 1) * Q_TILE)
            heads.append(gsig[3 * hq:3 * hq + 1, :] * o_c[:, c] + gsig[3 * hq + 1:3 * hq + 2, :] * o_sel[:, ca]
                         + gsig[3 * hq + 2:3 * hq + 3, :] * o_win[:, ca])
    cols_out = [jnp.concatenate(heads[2 * m:2 * m + 2], axis=0).T for m in range(NSA_HEADS // 2)]
    o_ref[...] = jnp.concatenate(cols_out, axis=1).astype(BF16)


def _nsa_consts(seq):
    n_cmp_pad = seq // CMP_STRIDE
    n_sel = seq // SEL_LEN
    c = np.arange(n_cmp_pad)
    j = np.arange(n_sel)
    n_cmp = (seq - CMP_LEN) // CMP_STRIDE + 1
    ov = ((c[None, :] * CMP_STRIDE < j[:, None] * SEL_LEN + SEL_LEN) &
          (j[:, None] * SEL_LEN <= c[None, :] * CMP_STRIDE + CMP_LEN - 1) & (c[None, :] < n_cmp))
    kl = np.arange(Q_BLOCK)[:, None]
    tl = np.arange(Q_BLOCK)[None, :]
    n_past = WINDOW // Q_BLOCK
    none = np.zeros((Q_BLOCK, Q_BLOCK), bool)
    valid = [none, kl > tl] + [np.ones((Q_BLOCK, Q_BLOCK), bool)] * (n_past - 1) + [kl <= tl, none]
    wbias = np.where(np.stack(valid), 0.0, NEG_INF).astype(np.float32)
    key_block = np.arange(seq) // SEL_LEN
    onehot = (key_block[:, None] == np.arange(LANES)[None, :]).reshape(-1, SEL_GROUP * Q_BLOCK, LANES)
    return jnp.asarray(ov, BF16), jnp.asarray(wbias), jnp.asarray(onehot, BF16)


def _nsa(qu, qr, kc, vc, ks, vs, kw, vw, ng, consts, batch, seq):
    ov, wbias, onehot = consts
    n_qb = seq // Q_TILE
    tokens = batch * seq
    qblk = pl.BlockSpec((NSA_WIDTH, Q_TILE), lambda b, q: (0, b * n_qb + q))
    cblk = pl.BlockSpec((1, seq // CMP_STRIDE, NSA_KV_WIDTH), lambda b, q: (b, 0, 0))
    kblk = pl.BlockSpec((seq, NSA_KV_WIDTH), lambda b, q: (b, 0))
    vblk = pl.BlockSpec((NSA_KV_HEADS * V_ROWS, seq), lambda b, q: (0, b))
    full = lambda a: pl.BlockSpec(a.shape, lambda b, q: (0,) * a.ndim)
    return pl.pallas_call(
        _nsa_kernel,
        grid=(batch, n_qb),
        in_specs=[qblk, qblk, cblk, cblk, kblk, vblk, kblk, vblk,
                  pl.BlockSpec((LANES, Q_TILE), lambda b, q: (0, b * n_qb + q)), full(ov), full(wbias),
                  full(onehot)],
        out_specs=pl.BlockSpec((Q_TILE, NSA_WIDTH), lambda b, q: (b * n_qb + q, 0)),
        out_shape=jax.ShapeDtypeStruct((tokens, NSA_WIDTH), BF16),
        compiler_params=_params("arbitrary", "arbitrary"),
        name="nsa_attention",
    )(qu, qr, kc, vc, ks, vs, kw, vw, ng, ov, wbias, onehot)


def _outproj_kernel(yr_ref, yc_ref, yn_ref, x_ref, mod_ref, w_ref, lng_ref, lnb_ref, rwh_ref, rwl_ref, rb_ref,
                    x1_ref, h2a_ref, h2b_ref, ti_ref, tw_ref, cnt_ref, *, alpha):
    tm = x_ref.shape[0]
    mix = (_dot(yr_ref[...], w_ref[0, 0:RET_WIDTH, :])
           + _dot(yc_ref[...], w_ref[0, RET_WIDTH:RET_WIDTH + CONV_WIDTH, :])
           + _dot(yn_ref[...], w_ref[0, RET_WIDTH + CONV_WIDTH:, :]))
    x1 = _layer_norm(alpha * x_ref[...] + (1.0 + mod_ref[2, 0]) * mix) * lng_ref[0, 0] + lnb_ref[0, 0]
    x1_ref[...] = x1
    h2 = _layer_norm(x1) * (1.0 + mod_ref[4, 0]) + mod_ref[3, 0]
    for j, part_ref in enumerate((h2a_ref, h2b_ref)):
        part_ref[...] = _pack_pairs(h2[:, j * 2 * SPLIT_WIDTH:(j + 1) * 2 * SPLIT_WIDTH])
    h_hi = h2.astype(BF16)
    h_lo = (h2 - h_hi.astype(F32)).astype(BF16)
    logits = _dot(h_hi, rwh_ref[0]) + _dot(h_lo, rwh_ref[0]) + _dot(h_hi, rwl_ref[0]) + rb_ref[0]

    lane = lax.broadcasted_iota(jnp.int32, (tm, LANES), 1)
    lanef = lane.astype(F32)
    rest = jnp.where(lane < N_EXPERTS, logits, -jnp.inf)
    vals, idxs = [], []
    for _ in range(TOP_K):
        top = jnp.max(rest, axis=-1, keepdims=True)
        idx = jnp.min(jnp.where(rest == top, lanef, float(LANES)), axis=-1, keepdims=True)
        vals.append(top)
        idxs.append(idx)
        rest = jnp.where(lanef == idx, -jnp.inf, rest)
    exps = [jnp.exp(v - vals[0]) for v in vals]
    inv = 1.0 / functools.reduce(lambda a, b: a + b, exps)
    top_w = jnp.zeros((tm, LANES), F32)
    top_i = jnp.zeros((tm, LANES), F32)
    member = jnp.zeros((tm, LANES), F32)
    for k in range(TOP_K):
        top_w = jnp.where(lane == k, exps[k] * inv, top_w)
        top_i = jnp.where(lane == k, idxs[k], top_i)
        member = member + jnp.where(lanef == idxs[k], 1.0, 0.0)
    tw_ref[...] = top_w
    ti_ref[...] = top_i.astype(jnp.int32)

    @pl.when(pl.program_id(0) == 0)
    def _():
        cnt_ref[...] = jnp.zeros_like(cnt_ref)

    cnt_ref[...] += jnp.broadcast_to(jnp.sum(member, axis=0, keepdims=True), cnt_ref.shape)


def _out_projection(y_ret, y_conv, y_nsa, x, mod_l, w_out, ln_g, ln_b, rw_hi, rw_lo, rb, layer, seq, alpha):
    tokens = x.shape[0]
    tm = ROW_TILE
    tiles_per_seq = seq // tm
    row = lambda w: pl.BlockSpec((tm, w), lambda i: (i, 0))
    lay3 = lambda a: pl.BlockSpec((1,) + a.shape[1:], lambda i: (layer,) + (0,) * (a.ndim - 1))
    return pl.pallas_call(
        functools.partial(_outproj_kernel, alpha=alpha),
        grid=(tokens // tm,),
        in_specs=[row(RET_WIDTH), row(CONV_WIDTH), row(NSA_WIDTH), row(D_MODEL),
                  pl.BlockSpec((6, 1, 1, D_MODEL), lambda i: (0, i // tiles_per_seq, 0, 0)),
                  lay3(w_out),
                  pl.BlockSpec((1, 1, 1, D_MODEL), lambda i: (layer, 0, 0, 0)),
                  pl.BlockSpec((1, 1, 1, D_MODEL), lambda i: (layer, 0, 0, 0)),
                  lay3(rw_hi), lay3(rw_lo), lay3(rb)],
        out_specs=[row(D_MODEL)] + [row(SPLIT_WIDTH)] * SC_SPLIT + [row(LANES), row(LANES),
                   pl.BlockSpec((8, LANES), lambda i: (0, 0))],
        out_shape=[jax.ShapeDtypeStruct((tokens, D_MODEL), F32)]
                  + [jax.ShapeDtypeStruct((tokens, SPLIT_WIDTH), jnp.uint32)] * SC_SPLIT
                  + [jax.ShapeDtypeStruct((tokens, LANES), jnp.int32), jax.ShapeDtypeStruct((tokens, LANES), F32),
                   jax.ShapeDtypeStruct((8, LANES), F32)],
        compiler_params=_params("arbitrary"),
        name="outproj_norm_router",
    )(y_ret, y_conv, y_nsa, x, mod_l, w_out, ln_g, ln_b, rw_hi, rw_lo, rb)


def _route_kernel(ti_ref, pstart_ref, tri_ref, pos_ref, carry_ref):
    @pl.when(pl.program_id(0) == 0)
    def _():
        carry_ref[...] = jnp.zeros_like(carry_ref)

    tm = ti_ref.shape[0]
    lane = lax.broadcasted_iota(jnp.int32, (tm, LANES), 1)
    top_i = ti_ref[...]
    onehots = [lane == top_i[:, k:k + 1] for k in range(TOP_K)]
    member = functools.reduce(lambda a, b: a + b, [jnp.where(o, 1.0, 0.0) for o in onehots])
    base = pstart_ref[...] + carry_ref[0:1] + _dot(tri_ref[...], member.astype(BF16))
    pos = jnp.zeros((tm, LANES), F32)
    for k in range(TOP_K):
        pos_k = jnp.sum(jnp.where(onehots[k], base, 0.0), axis=-1, keepdims=True)
        pos = jnp.where(lane == k, pos_k, pos)
    pos_ref[...] = pos.astype(jnp.int32)
    carry_ref[...] += jnp.broadcast_to(jnp.sum(member, axis=0, keepdims=True), carry_ref.shape)


def _route_positions(top_i, pstart):
    tokens = top_i.shape[0]
    tm = min(ROUTE_TILE, tokens)
    r = np.arange(tm)
    tri = jnp.asarray(r[None, :] < r[:, None], BF16)
    return pl.pallas_call(
        _route_kernel,
        grid=(tokens // tm,),
        in_specs=[pl.BlockSpec((tm, LANES), lambda i: (i, 0)),
                  pl.BlockSpec((1, LANES), lambda i: (0, 0)),
                  pl.BlockSpec((tm, tm), lambda i: (0, 0))],
        out_specs=pl.BlockSpec((tm, LANES), lambda i: (i, 0)),
        out_shape=jax.ShapeDtypeStruct((tokens, LANES), jnp.int32),
        scratch_shapes=[pltpu.VMEM((8, LANES), F32)],
        compiler_params=_params("arbitrary"),
        name="route_positions",
    )(top_i, pstart, tri)


def _block_table(counts, n_blocks):
    tm = MOE_TILE
    cnt = counts[0, :N_EXPERTS].astype(jnp.int32)
    nblk = (cnt + tm - 1) // tm
    experts = jnp.arange(N_EXPERTS, dtype=jnp.int32)
    before = experts[:, None] < experts[None, :]
    bstart = jnp.sum(jnp.where(before, nblk[:, None], 0), axis=0)
    bend = bstart + nblk
    n_used = jnp.sum(nblk)
    blocks = jnp.arange(n_blocks, dtype=jnp.int32)
    block_e = jnp.minimum(jnp.sum((bend[None, :] <= blocks[:, None]).astype(jnp.int32), axis=1), N_EXPERTS - 1)
    onehot = block_e[:, None] == experts[None, :]
    lookup = lambda per_expert: jnp.sum(jnp.where(onehot, per_expert[None, :], 0), axis=1)
    live = blocks < n_used
    n_valid = jnp.where(live, jnp.clip(lookup(cnt) - (blocks - lookup(bstart)) * tm, 0, tm), 0)
    pstart = jnp.pad((bstart * tm).astype(F32), (0, LANES - N_EXPERTS))[None, :]
    used = (nblk > 0).astype(jnp.int32)
    first = live & (blocks == lookup(bstart))
    slot = lookup(jnp.sum(jnp.where(before, used[:, None], 0), axis=0)) % 2
    later = jnp.where((used[None, :] > 0) & before, experts[None, :], N_EXPERTS)
    next_expert = jnp.min(later, axis=1)
    next_expert = lookup(jnp.where(next_expert < N_EXPERTS, next_expert, -1))
    plan = jnp.stack([first.astype(jnp.int32), slot, next_expert]).astype(jnp.int32)
    return pstart, block_e, n_used.reshape(1), n_valid.astype(jnp.int32), plan


def _sc_mesh():
    return plsc.VectorSubcoreMesh(core_axis_name="core", subcore_axis_name="subcore")


def _scatter_rows(xs, idx_t, n_rows):
    tokens, width = xs[0].shape
    n_idx = idx_t.shape[0]
    n_x = len(xs)
    win = SC_WINDOW

    @functools.partial(pl.kernel, out_type=[jax.ShapeDtypeStruct((n_rows, width), xs[0].dtype)] * n_x,
                       mesh=_sc_mesh(), scratch_types=[pltpu.SemaphoreType.DMA], name="dispatch_scatter")
    def scatter(*refs):
        x_hbms, i_hbm, o_hbms, sem = refs[:n_x], refs[n_x], refs[n_x + 1:-1], refs[-1]
        idx_specs = [pl.BlockSpec((1, win), functools.partial(lambda i, k: (k, i), k=k)) for k in range(n_idx)]
        for x_hbm, o_hbm in zip(x_hbms, o_hbms):
            def body(x_vmem, *i_vmems, o_hbm=o_hbm):
                copies = [pltpu.async_copy(x_vmem, o_hbm.at[i_vmem.at[0]], sem) for i_vmem in i_vmems]
                for copy in copies:
                    copy.wait()

            pltpu.emit_pipeline(
                body,
                grid=(tokens // win,),
                in_specs=[pl.BlockSpec((win, width), lambda i: (i, 0))] + idx_specs,
                out_specs=[],
                core_axis_name=("core", "subcore"),
                dimension_semantics=(pltpu.PARALLEL,),
            )(x_hbm, *([i_hbm] * n_idx))

    return scatter(*xs, idx_t)


def _gather_rows(xs, idx):
    width = xs[0].shape[1]
    n = idx.shape[0]
    n_x = len(xs)
    win = SC_WINDOW

    @functools.partial(pl.kernel, out_type=[jax.ShapeDtypeStruct((n, width), xs[0].dtype)] * n_x,
                       mesh=_sc_mesh(), scratch_types=[], name="combine_gather")
    def gather(*refs):
        x_hbms, i_hbm, o_hbms = refs[:n_x], refs[n_x], refs[n_x + 1:]
        for x_hbm, o_hbm in zip(x_hbms, o_hbms):
            def body(i_vmem, o_vmem, x_hbm=x_hbm):
                pltpu.sync_copy(x_hbm.at[i_vmem.at[0]], o_vmem)

            pltpu.emit_pipeline(
                body,
                grid=(n // win,),
                in_specs=[pl.BlockSpec((1, win), lambda i: (0, i))],
                out_specs=[pl.BlockSpec((win, width), lambda i: (i, 0))],
                core_axis_name=("core", "subcore"),
                dimension_semantics=(pltpu.PARALLEL,),
            )(i_hbm, o_hbm)

    return gather(*xs, idx.reshape(1, n))


def _moe_kernel(be_ref, nb_ref, nv_ref, plan_ref, ra_ref, rb_ref, wgu_hbm, bgu_ref, wd_hbm, bd_ref,
                oa_ref, ob_ref, wgu_f32, wd_f32, wgu_bf, wd_bf, sem, *, layer):
    row_refs = (ra_ref, rb_ref)
    out_refs = (oa_ref, ob_ref)
    i = pl.program_id(0)
    valid = i < nb_ref[0]

    def weight_copies(expert, slot):
        return (pltpu.make_async_copy(wgu_hbm.at[layer, expert], wgu_f32.at[slot], sem.at[0, slot]),
                pltpu.make_async_copy(wd_hbm.at[layer, expert], wd_f32.at[slot], sem.at[1, slot]))

    @pl.when(i == 0)
    def _():
        for copy in weight_copies(be_ref[0], 0):
            copy.start()

    @pl.when(valid & (plan_ref[0, i] == 1))
    def _():
        slot = plan_ref[1, i]
        for copy in weight_copies(be_ref[i], slot):
            copy.wait()
        upcoming = plan_ref[2, i]

        @pl.when(upcoming >= 0)
        def _():
            for copy in weight_copies(upcoming, 1 - slot):
                copy.start()

        wgu_bf[...] = wgu_f32[slot].astype(BF16)
        wd_bf[...] = wd_f32[slot].astype(BF16)

    def ffn(rs):
        live = rs.start + lax.broadcasted_iota(jnp.int32, (MOE_SUB, SPLIT_WIDTH), 0) < nv_ref[i]
        rows = jnp.concatenate(
            [half.astype(BF16) for r_ref in row_refs
             for half in _unpack_pairs(jnp.where(live, r_ref[rs, :], jnp.uint32(0)))], axis=1)
        acts = []
        for c in range(D_EXPERT // FFN_CHUNK):
            cg = slice(c * FFN_CHUNK, (c + 1) * FFN_CHUNK)
            cu = slice(D_EXPERT + c * FFN_CHUNK, D_EXPERT + (c + 1) * FFN_CHUNK)
            g = jnp.minimum(_dot(rows, wgu_bf[:, cg]) + bgu_ref[0, 0, :, cg], SWIGLU_LIMIT)
            u = jnp.clip(_dot(rows, wgu_bf[:, cu]) + bgu_ref[0, 0, :, cu], -SWIGLU_LIMIT, SWIGLU_LIMIT)
            acts.append(((u + 1.0) * (g * jax.nn.sigmoid(SWIGLU_ALPHA * g))).astype(BF16))
        y = _dot(jnp.concatenate(acts, axis=1), wd_bf[...]) + bd_ref[0, 0]
        for j, o_ref in enumerate(out_refs):
            o_ref[rs, :] = _pack_pairs(y[:, j * 2 * SPLIT_WIDTH:(j + 1) * 2 * SPLIT_WIDTH])

    for sub in range(MOE_TILE // MOE_SUB):
        rs = slice(sub * MOE_SUB, (sub + 1) * MOE_SUB)
        has_rows = valid & (nv_ref[i] > sub * MOE_SUB)
        pl.when(has_rows)(functools.partial(ffn, rs))

        @pl.when(jnp.logical_not(has_rows))
        def _(rs=rs):
            for o_ref in out_refs:
                o_ref[rs, :] = jnp.zeros((MOE_SUB, SPLIT_WIDTH), jnp.uint32)


def _expert_ffn(rows, block_e, n_used, n_valid, plan, w_gu, b_gu, w_down, b_down, layer):
    n_rows = rows[0].shape[0]
    tm = MOE_TILE
    part = pl.BlockSpec((tm, SPLIT_WIDTH), lambda i, be, nb, nv, pn: (i, 0))
    grid_spec = pltpu.PrefetchScalarGridSpec(
        num_scalar_prefetch=4,
        grid=(n_rows // tm,),
        in_specs=[part] * SC_SPLIT + [
                  pl.BlockSpec(memory_space=pl.ANY),
                  pl.BlockSpec((1, 1, 1, 2 * D_EXPERT), lambda i, be, nb, nv, pn: (layer, be[i], 0, 0)),
                  pl.BlockSpec(memory_space=pl.ANY),
                  pl.BlockSpec((1, 1, 1, D_MODEL), lambda i, be, nb, nv, pn: (layer, be[i], 0, 0))],
        out_specs=[part] * SC_SPLIT,
        scratch_shapes=[pltpu.VMEM((2, D_MODEL, 2 * D_EXPERT), F32), pltpu.VMEM((2, D_EXPERT, D_MODEL), F32),
                        pltpu.VMEM((D_MODEL, 2 * D_EXPERT), BF16), pltpu.VMEM((D_EXPERT, D_MODEL), BF16),
                        pltpu.SemaphoreType.DMA((2, 2))])
    return pl.pallas_call(
        functools.partial(_moe_kernel, layer=layer),
        grid_spec=grid_spec,
        out_shape=[jax.ShapeDtypeStruct((n_rows, SPLIT_WIDTH), jnp.uint32)] * SC_SPLIT,
        compiler_params=_params("arbitrary"),
        name="expert_ffn",
    )(block_e, n_used, n_valid, plan, *rows, w_gu, b_gu, w_down, b_down)


def _final_kernel(x_ref, tw_ref, mod_ref, lng_ref, lnb_ref, *rest, alpha):
    y_refs, o_ref = rest[:-1], rest[-1]
    top_w = tw_ref[...]
    parts = []
    for j in range(SC_SPLIT):
        acc = [0.0, 0.0]
        for k in range(TOP_K):
            for half, y in enumerate(_unpack_pairs(y_refs[j * TOP_K + k][...])):
                acc[half] = acc[half] + top_w[:, k:k + 1] * y
        parts += acc
    ffn = jnp.concatenate(parts, axis=1)
    o_ref[...] = (_layer_norm(alpha * x_ref[...] + (1.0 + mod_ref[5, 0]) * ffn) * lng_ref[0, 0] + lnb_ref[0, 0])


def _final_norm(x1, y_parts, top_w, mod_l, ln_g, ln_b, layer, seq, alpha):
    tokens = x1.shape[0]
    tm = ROW_TILE
    tiles = tokens // tm
    tiles_per_seq = seq // tm
    row = lambda w: pl.BlockSpec((tm, w), lambda i: (i, 0))
    vec = pl.BlockSpec((1, 1, 1, D_MODEL), lambda i: (layer, 1, 0, 0))
    y_specs, y_args = [], []
    for j in range(SC_SPLIT):
        for k in range(TOP_K):
            y_specs.append(pl.BlockSpec((tm, SPLIT_WIDTH), functools.partial(lambda i, k: (k * tiles + i, 0), k=k)))
            y_args.append(y_parts[j])
    return pl.pallas_call(
        functools.partial(_final_kernel, alpha=alpha),
        grid=(tiles,),
        in_specs=[row(D_MODEL), row(LANES),
                  pl.BlockSpec((6, 1, 1, D_MODEL), lambda i: (0, i // tiles_per_seq, 0, 0)), vec, vec] + y_specs,
        out_specs=row(D_MODEL),
        out_shape=jax.ShapeDtypeStruct(x1.shape, F32),
        compiler_params=_params("arbitrary"),
        name="combine_final_norm",
    )(x1, top_w, mod_l, ln_g, ln_b, *y_args)


def _inproj_weights(w_in):
    gate_cols = jnp.pad(w_in[:, :, OFF_NG:], ((0, 0), (0, 0), (0, LANES - N_GATE)))
    w_t = jnp.concatenate([w_in[:, :, off:off + width] for off, width in
                           ((OFF_NQ, NSA_WIDTH), (OFF_VS, NSA_KV_WIDTH), (OFF_VW, NSA_KV_WIDTH))] + [gate_cols],
                          axis=2).transpose(0, 2, 1)
    return w_in[:, :, :N_COLS].astype(BF16), w_t.astype(BF16)


def kernel(x, c, positions, w_in, w_out, ret_gn_w, conv_w, cmp_pos, cmp_w1, cmp_w2, ada_w, ada_b, ln_g, ln_b,
           router_w, router_b, w_gate_up, b_gate_up, w_down, b_down):
    batch, seq, _ = x.shape
    depth = w_in.shape[0]
    tokens = batch * seq
    n_rows = tokens * TOP_K + N_EXPERTS * MOE_TILE
    alpha = float((2 * depth) ** 0.25)

    w_in_r, w_in_t = _inproj_weights(w_in)
    w_out_b = w_out.astype(BF16)
    rw = jnp.pad(router_w, ((0, 0), (0, 0), (0, LANES - N_EXPERTS)))
    rw_hi = rw.astype(BF16)
    rw_lo = (rw - rw_hi.astype(F32)).astype(BF16)
    rb = jnp.pad(router_b, ((0, 0), (0, LANES - N_EXPERTS))).reshape(depth, 1, LANES)
    cw = _compress_weights(cmp_pos, cmp_w1, cmp_w2)
    ret_consts = _retention_consts()
    nsa_consts = _nsa_consts(seq)
    gn_w = ret_gn_w.reshape(depth, 1, RET_WIDTH)
    ln_g4 = ln_g.reshape(depth, 2, 1, D_MODEL)
    ln_b4 = ln_b.reshape(depth, 2, 1, D_MODEL)
    b_gu4 = b_gate_up.reshape(depth, N_EXPERTS, 1, 2 * D_EXPERT)
    b_d4 = b_down.reshape(depth, N_EXPERTS, 1, D_MODEL)

    mod = _modulation(c, ada_w, ada_b)
    cos, sin, cos_t, sin_t = _rope_tables(positions)
    xt = x.reshape(tokens, D_MODEL)
    for l in range(depth):
        (rq, rk, rv, rg, y_conv, qu, qr, kcx, vcx, ks, vs, kw, vw, ng) = _input_projection(
            xt, mod[l], w_in_r, w_in_t, cos, sin, cos_t, sin_t, conv_w, l, seq)
        y_ret = _retention(rq, rk, rv, rg, ret_consts, gn_w, l, batch, seq)
        kc, vc = _compress(kcx, vcx, cw, l, batch, seq)
        y_nsa = _nsa(qu, qr, kc, vc, ks, vs, kw, vw, ng, nsa_consts, batch, seq)
        x1, *h2, top_i, top_w, counts = _out_projection(y_ret, y_conv, y_nsa, xt, mod[l], w_out_b, ln_g4, ln_b4,
                                                        rw_hi, rw_lo, rb, l, seq, alpha)
        pstart, block_e, n_used, n_valid, plan = _block_table(counts, n_rows // MOE_TILE)
        pos_t = _route_positions(top_i, pstart)[:, :TOP_K].T
        rows = _scatter_rows(h2, pos_t, n_rows)
        y = _expert_ffn(rows, block_e, n_used, n_valid, plan, w_gate_up, b_gu4, w_down, b_d4, l)
        y_tok = _gather_rows(y, pos_t.reshape(-1))
        xt = _final_norm(x1, y_tok, top_w, mod[l], ln_g4, ln_b4, l, seq, alpha)
    return xt.reshape(batch, seq, D_MODEL)
```

```python
import functools

import numpy as np
import jax
import jax.numpy as jnp
from jax import lax
from jax.experimental import pallas as pl
from jax.experimental.pallas import tpu as pltpu
from jax.experimental.pallas import tpu_sc as plsc

F32 = jnp.float32
BF16 = jnp.bfloat16

D_MODEL = 1024
HEAD_DIM = 64
RET_WIDTH = 256
RET_HEADS = 4
RET_CHUNK = 128
CONV_WIDTH = 256
CONV_K = 3
NSA_WIDTH = 512
NSA_HEADS = 8
NSA_KV_HEADS = 2
NSA_GROUP = 4
NSA_KV_WIDTH = 128
CMP_LEN = 32
CMP_STRIDE = 16
CMP_HIDDEN = 128
SEL_LEN = 64
SEL_TOP = 8
WINDOW = 512
Q_BLOCK = 128
Q_TILE = 2 * Q_BLOCK
ROPE_THETA = 10000.0
N_EXPERTS = 32
TOP_K = 4
D_EXPERT = 1024
SWIGLU_LIMIT = 7.0
SWIGLU_ALPHA = 1.702
LN_EPS = 1e-5
NEG_INF = -1e30
FORCE_SCORE = 1e9

LANES = 128
VMEM_LIMIT = 56 * 1024 * 1024

OFF_RQ, OFF_RK, OFF_RV, OFF_RG = 0, 256, 512, 768
OFF_CB, OFF_CC, OFF_CH = 1024, 1280, 1536
OFF_NQ = 1792
OFF_KC, OFF_VC = 2304, 2432
OFF_KS, OFF_VS, OFF_KW, OFF_VW = 2560, 2688, 2816, 2944
OFF_NG = 3072
N_GATE = NSA_HEADS * 3
N_COLS = OFF_NG
LOG2E = 1.4426950408889634
WT_NQ, WT_VS, WT_VW, WT_NG = 0, NSA_WIDTH, NSA_WIDTH + NSA_KV_WIDTH, NSA_WIDTH + 2 * NSA_KV_WIDTH
WT_ROWS = WT_NG + LANES

RET_BATCH = 4
ROW_TILE = 256
MOE_TILE = 512
MOE_SUB = 256
FFN_CHUNK = 256
ROUTE_TILE = 1024
SEL_GROUP = 4
V_ROWS = HEAD_DIM + 16
SC_WINDOW = LANES
SC_SPLIT = 2
SPLIT_WIDTH = D_MODEL // (2 * SC_SPLIT)


def _dot(a, b):
    return jnp.dot(a, b, preferred_element_type=F32)


def _dot_nt(a, b):
    return lax.dot_general(a, b, (((1,), (1,)), ((), ())), preferred_element_type=F32)


def _layer_norm(x):
    mu = jnp.mean(x, axis=-1, keepdims=True)
    xc = x - mu
    var = jnp.mean(xc * xc, axis=-1, keepdims=True)
    return xc * lax.rsqrt(var + LN_EPS)


def _pack_pairs(x):
    bits = lambda t: lax.bitcast_convert_type(t.astype(BF16).astype(F32), jnp.uint32)
    return (bits(x[:, SPLIT_WIDTH:]) & jnp.uint32(0xFFFF0000)) | (bits(x[:, :SPLIT_WIDTH]) >> 16)


def _unpack_pairs(p):
    low = lax.bitcast_convert_type(p << 16, F32)
    high = lax.bitcast_convert_type(p & jnp.uint32(0xFFFF0000), F32)
    return low, high


def _params(*sem):
    return pltpu.CompilerParams(dimension_semantics=sem, vmem_limit_bytes=VMEM_LIMIT)


def _mod_kernel(c_ref, w_ref, b_ref, o_ref):
    c = c_ref[...]
    ca = (c * jax.nn.sigmoid(c)).astype(BF16)
    o_ref[0, 0] = _dot(ca, w_ref[0].astype(BF16)) + b_ref[0]


def _modulation(c, ada_w, ada_b):
    depth = ada_w.shape[0]
    batch = c.shape[0]
    out = pl.pallas_call(
        _mod_kernel,
        grid=(depth, 6),
        in_specs=[pl.BlockSpec((batch, D_MODEL), lambda l, j: (0, 0)),
                  pl.BlockSpec((1, D_MODEL, D_MODEL), lambda l, j: (l, 0, j)),
                  pl.BlockSpec((1, 1, D_MODEL), lambda l, j: (l * 6 + j, 0, 0))],
        out_specs=pl.BlockSpec((1, 1, batch, D_MODEL), lambda l, j: (l, j, 0, 0)),
        out_shape=jax.ShapeDtypeStruct((depth, 6, batch, D_MODEL), F32),
        compiler_params=_params("arbitrary", "arbitrary"),
        name="adaln_mod",
    )(c, ada_w, ada_b.reshape(depth * 6, 1, D_MODEL))
    return out.reshape(depth, 6, batch, 1, D_MODEL)


def _rope_table_kernel(pos_ref, inv_ref, cos_ref, sin_ref):
    ang = pos_ref[...] * inv_ref[...]
    cos_ref[...] = jnp.cos(ang)
    sin_ref[...] = jnp.sin(ang)


def _rope_tables(positions):
    half = HEAD_DIM // 2
    per_row = LANES // half
    tokens = positions.size
    pos4 = jnp.repeat(positions.reshape(tokens // per_row, per_row).astype(F32), half, axis=1)
    inv = ROPE_THETA ** (-jnp.arange(half, dtype=F32) / half)
    inv4 = jnp.tile(inv, per_row)[None, :]
    rows = tokens // per_row
    tile = min(rows, 1024)
    cos4, sin4 = pl.pallas_call(
        _rope_table_kernel,
        grid=(rows // tile,),
        in_specs=[pl.BlockSpec((tile, LANES), lambda i: (i, 0)),
                  pl.BlockSpec((1, LANES), lambda i: (0, 0))],
        out_specs=[pl.BlockSpec((tile, LANES), lambda i: (i, 0))] * 2,
        out_shape=[jax.ShapeDtypeStruct((rows, LANES), F32)] * 2,
        compiler_params=_params("arbitrary"),
        name="rope_tables",
    )(pos4, inv4)
    cos32 = cos4.reshape(tokens, half)
    sin32 = sin4.reshape(tokens, half)
    cos = jnp.tile(cos32, (1, per_row))
    sign = jnp.tile(jnp.concatenate([-jnp.ones((half,), F32), jnp.ones((half,), F32)]), LANES // HEAD_DIM)
    sin = jnp.tile(sin32, (1, per_row)) * sign[None, :]
    return cos, sin, cos32.T, sin32.T


def _inproj_kernel(x_ref, mod_ref, w_ref, wt_ref, cos_ref, sin_ref, cost_ref, sint_ref, convw_ref,
                   rq_ref, rk_ref, rv_ref, rg_ref, yc_ref, qu_ref, qr_ref, kc_ref, vc_ref,
                   ks_ref, vs_ref, kw_ref, vw_ref, ng_ref, carry_ref, *, tiles_per_seq):
    i = pl.program_id(0)
    tm = x_ref.shape[0]
    h = (_layer_norm(x_ref[...]) * (1.0 + mod_ref[1, 0]) + mod_ref[0, 0]).astype(BF16)
    cosf = cos_ref[...]
    sinf = sin_ref[...]
    lane = lax.broadcasted_iota(jnp.int32, (tm, LANES), 1)
    first_half = (lane % HEAD_DIM) < (HEAD_DIM // 2)

    def proj(off, width):
        return _dot(h, w_ref[0, :, off:off + width])

    def rope(c):
        cols = []
        for j in range(c.shape[1] // LANES):
            cj = c[:, j * LANES:(j + 1) * LANES]
            swapped = jnp.where(first_half, pltpu.roll(cj, LANES - HEAD_DIM // 2, 1),
                                pltpu.roll(cj, HEAD_DIM // 2, 1))
            cols.append(cj * cosf + swapped * sinf)
        return jnp.concatenate(cols, axis=1) if len(cols) > 1 else cols[0]

    scale = HEAD_DIM ** -0.5

    def proj_t(off, width):
        return _dot_nt(wt_ref[0, off:off + width, :], h)

    def store(ref, value):
        ref[...] = value

    def conv_out(dots):
        cb, cc, ch = dots
        u = cc * ch

        @pl.when(i % tiles_per_seq == 0)
        def _():
            carry_ref[...] = jnp.zeros_like(carry_ref)

        carry = carry_ref[...]
        row = lax.broadcasted_iota(jnp.int32, (tm, CONV_WIDTH), 0)
        prev1 = jnp.where(row == 0, carry[7:8], pltpu.roll(u, 1, 0))
        prev2 = jnp.where(row == 0, carry[6:7], jnp.where(row == 1, carry[7:8], pltpu.roll(u, 2, 0)))
        cw = convw_ref[0]
        yc_ref[...] = (cb * (cw[0:1] * prev2 + cw[1:2] * prev1 + cw[2:3] * u)).astype(BF16)
        carry_ref[...] = u[tm - 8:tm]

    def queries_out(nq_t):
        nq_t = nq_t * (scale * LOG2E)
        qu_ref[...] = nq_t.astype(BF16)
        cos_t = cost_ref[...]
        sin_t = sint_ref[...]
        half = HEAD_DIM // 2
        rotated = []
        for hq in range(NSA_HEADS):
            t1 = nq_t[hq * HEAD_DIM:hq * HEAD_DIM + half]
            t2 = nq_t[hq * HEAD_DIM + half:(hq + 1) * HEAD_DIM]
            rotated += [t1 * cos_t - t2 * sin_t, t2 * cos_t + t1 * sin_t]
        qr_ref[...] = jnp.concatenate(rotated, axis=0).astype(BF16)

    def values_t(v_t):
        ones = jnp.ones((V_ROWS - HEAD_DIM, tm), F32)
        parts = []
        for hd in range(NSA_KV_HEADS):
            parts += [v_t[hd * HEAD_DIM:(hd + 1) * HEAD_DIM], ones]
        return jnp.concatenate(parts, axis=0).astype(BF16)

    stages = [
        (lambda: proj(OFF_RQ, RET_WIDTH), lambda d: store(rq_ref, rope(d).astype(BF16))),
        (lambda: proj(OFF_RK, RET_WIDTH), lambda d: store(rk_ref, (rope(d) * scale).astype(BF16))),
        (lambda: proj(OFF_RV, RET_WIDTH), lambda d: store(rv_ref, d.astype(BF16))),
        (lambda: proj(OFF_RG, RET_WIDTH), lambda d: store(rg_ref, (d * jax.nn.sigmoid(d)).astype(BF16))),
        (lambda: (proj(OFF_CB, CONV_WIDTH), proj(OFF_CC, CONV_WIDTH), proj(OFF_CH, CONV_WIDTH)), conv_out),
        (lambda: proj_t(WT_NQ, NSA_WIDTH), queries_out),
        (lambda: proj(OFF_KC, NSA_KV_WIDTH), lambda d: store(kc_ref, d)),
        (lambda: proj(OFF_VC, NSA_KV_WIDTH), lambda d: store(vc_ref, d)),
        (lambda: proj(OFF_KS, NSA_KV_WIDTH), lambda d: store(ks_ref, rope(d).astype(BF16))),
        (lambda: proj_t(WT_VS, NSA_KV_WIDTH), lambda d: store(vs_ref, values_t(d))),
        (lambda: proj(OFF_KW, NSA_KV_WIDTH), lambda d: store(kw_ref, rope(d).astype(BF16))),
        (lambda: proj_t(WT_VW, NSA_KV_WIDTH), lambda d: store(vw_ref, values_t(d))),
        (lambda: proj_t(WT_NG, LANES), lambda d: store(ng_ref, d)),
    ]
    for matmuls, epilogue in stages:
        epilogue(matmuls())


def _input_projection(x, mod_l, w_l, w_t, cos, sin, cos_t, sin_t, conv_w, layer, seq):
    tokens = x.shape[0]
    tm = ROW_TILE
    tiles_per_seq = seq // tm
    row = lambda w: pl.BlockSpec((tm, w), lambda i: (i, 0))
    col = lambda w: pl.BlockSpec((w, tm), lambda i: (0, i))
    outs = [(RET_WIDTH, BF16, False)] * 4 + [(CONV_WIDTH, BF16, False), (NSA_WIDTH, BF16, True),
                                            (NSA_WIDTH, BF16, True), (NSA_KV_WIDTH, F32, False),
                                            (NSA_KV_WIDTH, F32, False), (NSA_KV_WIDTH, BF16, False),
                                            (NSA_KV_HEADS * V_ROWS, BF16, True), (NSA_KV_WIDTH, BF16, False),
                                            (NSA_KV_HEADS * V_ROWS, BF16, True), (LANES, F32, True)]
    return pl.pallas_call(
        functools.partial(_inproj_kernel, tiles_per_seq=tiles_per_seq),
        grid=(tokens // tm,),
        in_specs=[row(D_MODEL),
                  pl.BlockSpec((6, 1, 1, D_MODEL), lambda i: (0, i // tiles_per_seq, 0, 0)),
                  pl.BlockSpec((1, D_MODEL, N_COLS), lambda i: (layer, 0, 0)),
                  pl.BlockSpec((1, WT_ROWS, D_MODEL), lambda i: (layer, 0, 0)),
                  row(LANES), row(LANES), col(HEAD_DIM // 2), col(HEAD_DIM // 2),
                  pl.BlockSpec((1, CONV_K, CONV_WIDTH), lambda i: (layer, 0, 0))],
        out_specs=[col(w) if t else row(w) for w, _, t in outs],
        out_shape=[jax.ShapeDtypeStruct((w, tokens) if t else (tokens, w), dt) for w, dt, t in outs],
        scratch_shapes=[pltpu.VMEM((8, CONV_WIDTH), F32)],
        compiler_params=_params("arbitrary"),
        name="ln_inproj",
    )(x, mod_l, w_l, w_t, cos, sin, cos_t, sin_t, conv_w)


def _retention_kernel(q_ref, k_ref, v_ref, g_ref, intra_ref, qdec_ref, kdec_ref, cdec_ref, gn_ref,
                      o_ref, state_ref):
    @pl.when(pl.program_id(1) == 0)
    def _():
        state_ref[...] = jnp.zeros_like(state_ref)

    n_pairs = RET_WIDTH // LANES
    lane = lax.broadcasted_iota(jnp.int32, (RET_CHUNK, LANES), 1)
    low = lane < HEAD_DIM
    blk_r = lax.broadcasted_iota(jnp.int32, (LANES, LANES), 0) < HEAD_DIM
    blk_c = lax.broadcasted_iota(jnp.int32, (LANES, LANES), 1) < HEAD_DIM
    same_head = blk_r == blk_c
    def half_mean(t):
        lo_sum = jnp.sum(jnp.where(low, t, 0.0), axis=-1, keepdims=True)
        hi_sum = jnp.sum(jnp.where(low, 0.0, t), axis=-1, keepdims=True)
        return jnp.where(low, lo_sum, hi_sum) * (1.0 / HEAD_DIM)

    combos = [(b, p) for b in range(q_ref.shape[0]) for p in range(n_pairs)]
    cols = {p: slice(p * LANES, (p + 1) * LANES) for p in range(n_pairs)}
    keeps = (low, jnp.logical_not(low))
    states = {c: state_ref[c[0], c[1]] for c in combos}
    for ch in range(q_ref.shape[1] // RET_CHUNK):
        rows = slice(ch * RET_CHUNK, (ch + 1) * RET_CHUNK)
        qp = {(b, p): q_ref[b, rows, cols[p]] for b, p in combos}
        kp = {(b, p): k_ref[b, rows, cols[p]] for b, p in combos}
        vp = {(b, p): v_ref[b, rows, cols[p]] for b, p in combos}
        scores = {(c, hh): _dot_nt(jnp.where(keeps[hh], qp[c], jnp.zeros_like(qp[c])), kp[c])
                  for c in combos for hh in range(2)}
        scores = {(c, hh): (s * intra_ref[2 * c[1] + hh]).astype(BF16) for (c, hh), s in scores.items()}
        intra = {key: _dot(s, vp[key[0]]) for key, s in scores.items()}
        inter = {c: _dot((qp[c].astype(F32) * qdec_ref[:, cols[c[1]]]).astype(BF16), states[c].astype(BF16))
                 for c in combos}
        upd = {c: _dot((kp[c].astype(F32) * kdec_ref[:, cols[c[1]]]).T.astype(BF16), vp[c]) for c in combos}
        states = {c: states[c] * cdec_ref[c[1]] + jnp.where(same_head, upd[c], 0.0) for c in combos}
        out = {c: jnp.where(low, intra[c, 0], intra[c, 1]) + inter[c] for c in combos}
        centred = {c: out[c] - half_mean(out[c]) for c in combos}
        normed = {c: centred[c] * lax.rsqrt(half_mean(centred[c] * centred[c]) + LN_EPS) for c in combos}
        for b in range(q_ref.shape[0]):
            full = jnp.concatenate([normed[b, p] for p in range(n_pairs)], axis=1)
            o_ref[b, rows, :] = (full * gn_ref[0] * g_ref[b, rows, :].astype(F32)).astype(BF16)
    for (b, p), state in states.items():
        state_ref[b, p] = state


def _retention_consts():
    heads = jnp.arange(RET_HEADS, dtype=F32)
    log_gamma = jnp.log(1.0 - jnp.power(2.0, -5.0 - heads))
    i = jnp.arange(RET_CHUNK, dtype=F32)
    diff = i[:, None] - i[None, :]
    intra = jnp.where(diff >= 0, jnp.exp(diff * log_gamma[:, None, None]), 0.0)
    q_dec = jnp.exp((i + 1.0) * log_gamma[:, None])
    k_dec = jnp.exp((RET_CHUNK - 1.0 - i) * log_gamma[:, None])
    c_dec = jnp.exp(RET_CHUNK * log_gamma)
    expand = lambda t: jnp.repeat(t.T, HEAD_DIM, axis=1)
    c_rows = jnp.repeat(c_dec, HEAD_DIM).reshape(RET_WIDTH // LANES, LANES, 1)
    c_blk = jnp.broadcast_to(c_rows, (RET_WIDTH // LANES, LANES, LANES))
    return intra, expand(q_dec), expand(k_dec), c_blk


def _retention(rq, rk, rv, rg, consts, gn_w, layer, batch, seq):
    intra, q_dec, k_dec, c_blk = consts
    rows = 2 * RET_CHUNK
    steps = seq // rows
    group = RET_BATCH if batch % RET_BATCH == 0 else 1
    blk = pl.BlockSpec((group, rows, RET_WIDTH), lambda b, s: (b, s, 0))
    full = lambda a: pl.BlockSpec(a.shape, lambda b, s: (0,) * a.ndim)
    per_seq = lambda t: t.reshape(batch, seq, RET_WIDTH)
    out = pl.pallas_call(
        _retention_kernel,
        grid=(batch // group, steps),
        in_specs=[blk, blk, blk, blk, full(intra), full(q_dec), full(k_dec), full(c_blk),
                  pl.BlockSpec((1, 1, RET_WIDTH), lambda b, s: (layer, 0, 0))],
        out_specs=blk,
        out_shape=jax.ShapeDtypeStruct((batch, seq, RET_WIDTH), BF16),
        scratch_shapes=[pltpu.VMEM((group, RET_WIDTH // LANES, LANES, LANES), F32)],
        compiler_params=_params("arbitrary", "arbitrary"),
        name="retention",
    )(per_seq(rq), per_seq(rk), per_seq(rv), per_seq(rg), intra, q_dec, k_dec, c_blk, gn_w)
    return out.reshape(batch * seq, RET_WIDTH)


def _compress_kernel(xk_ref, xv_ref, posa_ref, posb_ref, w1a_ref, w1b_ref, w2_ref, kc_ref, vc_ref):
    n_grp = xk_ref.shape[0] // CMP_STRIDE
    for kv, (x_ref, o_ref) in enumerate(((xk_ref, kc_ref), (xv_ref, vc_ref))):
        ya = jnp.zeros((n_grp, NSA_KV_HEADS * CMP_HIDDEN), F32)
        yb = jnp.zeros((n_grp, NSA_KV_HEADS * CMP_HIDDEN), F32)
        for r in range(CMP_STRIDE):
            x = x_ref[pl.ds(r, n_grp, stride=CMP_STRIDE), :]
            cols = slice(r * NSA_KV_WIDTH, (r + 1) * NSA_KV_WIDTH)
            ya = ya + _dot((x + posa_ref[0, kv, :, cols]).astype(BF16), w1a_ref[0, kv, cols, :])
            yb = yb + _dot((x + posb_ref[0, kv, :, cols]).astype(BF16), w1b_ref[0, kv, cols, :])
        hidden = ya + pltpu.roll(yb, n_grp - 1, 0)
        act = jax.nn.gelu(hidden)
        out = _dot(act.astype(BF16), w2_ref[0, kv])
        o_ref[0] = (out if kv == 0 else out.T).astype(BF16)


def _compress_weights(cmp_pos, cmp_w1, cmp_w2):
    depth = cmp_w1.shape[0]
    eye = jnp.eye(NSA_KV_HEADS, dtype=F32)
    a = cmp_w1.reshape(depth, 2, 2, CMP_STRIDE, HEAD_DIM, CMP_HIDDEN)
    w1 = jnp.einsum('lkardj,hg->lkarhdgj', a, eye)
    w1 = w1.reshape(depth, 2, 2, CMP_STRIDE * NSA_KV_WIDTH, NSA_KV_HEADS * CMP_HIDDEN).astype(BF16)
    pos = cmp_pos.reshape(depth, 2, 2, CMP_STRIDE, 1, HEAD_DIM)
    pos = jnp.broadcast_to(pos, (depth, 2, 2, CMP_STRIDE, NSA_KV_HEADS, HEAD_DIM))
    pos = pos.reshape(depth, 2, 2, 1, CMP_STRIDE * NSA_KV_WIDTH)
    w2 = jnp.einsum('lkje,hg->lkhjge', cmp_w2, eye)
    w2 = w2.reshape(depth, 2, NSA_KV_HEADS * CMP_HIDDEN, NSA_KV_WIDTH).astype(BF16)
    return w1[:, :, 0], w1[:, :, 1], pos[:, :, 0], pos[:, :, 1], w2


def _compress(kcx, vcx, cw, layer, batch, seq):
    w1a, w1b, posa, posb, w2 = cw
    n_grp = seq // CMP_STRIDE
    xblk = pl.BlockSpec((seq, NSA_KV_WIDTH), lambda b: (b, 0))
    lay = lambda a: pl.BlockSpec((1,) + a.shape[1:], lambda b: (layer,) + (0,) * (a.ndim - 1))
    oblk = pl.BlockSpec((1, n_grp, NSA_KV_WIDTH), lambda b: (b, 0, 0))
    return pl.pallas_call(
        _compress_kernel,
        grid=(batch,),
        in_specs=[xblk, xblk, lay(posa), lay(posb), lay(w1a), lay(w1b), lay(w2)],
        out_specs=[oblk, oblk],
        out_shape=[jax.ShapeDtypeStruct((batch, n_grp, NSA_KV_WIDTH), BF16)] * 2,
        compiler_params=_params("arbitrary"),
        name="nsa_compress",
    )(kcx, vcx, posa, posb, w1a, w1b, w2)


def _nsa_kernel(qu_ref, qr_ref, kc_ref, vc_ref, ks_ref, vs_ref, kw_ref, vw_ref, ng_ref, ov_ref, wbias_ref, ind_ref,
                o_ref):
    qb = pl.program_id(1)
    q0 = qb * Q_TILE
    qb_last = (qb + 1) * (Q_TILE // Q_BLOCK) - 1
    n_sel = ov_ref.shape[0]
    cols4 = NSA_GROUP * Q_TILE
    n_past = WINDOW // Q_BLOCK
    gsig = jax.nn.sigmoid(ng_ref[...])

    def padded_q(ref, h):
        zeros = jnp.zeros((HEAD_DIM, Q_TILE), BF16)
        parts = []
        for g in range(NSA_GROUP):
            hq = h * NSA_GROUP + g
            x = ref[hq * HEAD_DIM:(hq + 1) * HEAD_DIM, :]
            parts.append(jnp.concatenate([x, zeros] if h == 0 else [zeros, x], axis=0))
        return jnp.concatenate(parts, axis=1)

    def tile4(t):
        return jnp.concatenate([t] * NSA_GROUP, axis=1)

    def attend(state, keys, values_ref, k0, bias, q_t):
        m, acc = state
        s = _dot(keys, q_t)
        if bias is not None:
            s = s + bias
        m_new = jnp.maximum(m, jnp.max(s, axis=0, keepdims=True))
        e = jnp.exp2(s - m_new).astype(BF16)
        pv = [_dot(values_ref[h * V_ROWS:(h + 1) * V_ROWS, pl.ds(k0, keys.shape[0])],
                   e[:, h * cols4:(h + 1) * cols4]) for h in range(NSA_KV_HEADS)]
        return m_new, jnp.exp2(m - m_new) * acc + jnp.concatenate(pv, axis=1)

    def normalised(state):
        acc = state[1]
        return acc[0:HEAD_DIM] * (1.0 / acc[HEAD_DIM:HEAD_DIM + 1])

    cols_all = NSA_KV_HEADS * cols4
    init = (jnp.full((1, cols_all), NEG_INF, F32), jnp.zeros((V_ROWS, cols_all), F32))
    q_rot = [padded_q(qr_ref, h) for h in range(NSA_KV_HEADS)]

    def position_bias(kb, windowed):
        first = qb * (Q_TILE // Q_BLOCK)
        return jnp.concatenate(
            [wbias_ref[jnp.clip(kb - (first + sub) + n_past + 1, 0 if windowed else 2, n_past + 2)]
             for sub in range(Q_TILE // Q_BLOCK)] * (NSA_GROUP * NSA_KV_HEADS), axis=1)

    n_cols = NSA_KV_HEADS * NSA_GROUP
    qus_all = jnp.concatenate([padded_q(qu_ref, h) for h in range(NSA_KV_HEADS)], axis=1)
    s = _dot(kc_ref[0], qus_all)
    cid = lax.broadcasted_iota(jnp.int32, (LANES, Q_TILE), 0)
    tq = q0 + lax.broadcasted_iota(jnp.int32, (LANES, Q_TILE), 1)
    cvalid = jnp.concatenate([jnp.where(cid * CMP_STRIDE + (CMP_LEN - 1) <= tq, 1.0, 0.0)] * n_cols, axis=1)
    sm = jnp.where(cvalid > 0.5, s, NEG_INF)
    e = jnp.exp2(sm - jnp.max(sm, axis=0, keepdims=True)) * cvalid
    l = jnp.sum(e, axis=0, keepdims=True)
    p = e * (1.0 / jnp.where(l > 0.0, l, 1.0))
    p_bf = p.astype(BF16)
    o_cmp = [_dot(vc_ref[0, h * HEAD_DIM:(h + 1) * HEAD_DIM, :], p_bf[:, h * cols4:(h + 1) * cols4])
             for h in range(NSA_KV_HEADS)]

    psum = jnp.concatenate(
        [functools.reduce(lambda a, b: a + b, [p[:, (h * NSA_GROUP + g) * Q_TILE:(h * NSA_GROUP + g + 1) * Q_TILE]
                                               for g in range(NSA_GROUP)]) for h in range(NSA_KV_HEADS)], axis=1)
    p_hi = psum.astype(BF16)
    rem = psum - p_hi.astype(F32)
    p_mid = rem.astype(BF16)
    p_lo = (rem - p_mid.astype(F32)).astype(BF16)
    ov = ov_ref[...]
    imp = _dot(ov, p_hi) + _dot(ov, p_mid) + _dot(ov, p_lo)
    sel_cols = NSA_KV_HEADS * Q_TILE
    jid = lax.broadcasted_iota(jnp.int32, (n_sel, sel_cols), 0)
    tid = q0 + lax.broadcasted_iota(jnp.int32, (n_sel, sel_cols), 1) % Q_TILE
    forced = (jid == 0) | (jid == jnp.right_shift(tid, 6))
    imp = jnp.where(forced, FORCE_SCORE, jnp.where(jid * SEL_LEN <= tid, imp, -FORCE_SCORE))
    rank = jnp.zeros((n_sel, sel_cols), F32)
    for i in range(n_sel):
        ri = imp[i:i + 1, :]
        beats = (ri > imp) | ((ri == imp) & (jid > i))
        rank = rank + jnp.where(beats, 1.0, 0.0)
    sel_bias = jnp.where(rank < float(SEL_TOP), 0.0, NEG_INF).astype(BF16)
    pad = jnp.zeros((LANES - n_sel, cols4), BF16)
    q_sel = [jnp.concatenate([q_rot[h], tile4(sel_bias[:, h * Q_TILE:(h + 1) * Q_TILE]), pad], axis=0)
             for h in range(NSA_KV_HEADS)]

    q_sel_all = jnp.concatenate(q_sel, axis=1)
    q_rot_all = jnp.concatenate(q_rot, axis=1)

    def sel_step(gi, state, last):
        k0 = pl.multiple_of(gi * (SEL_GROUP * Q_BLOCK), SEL_GROUP * Q_BLOCK)
        keys = jnp.concatenate([ks_ref[pl.ds(k0, SEL_GROUP * Q_BLOCK), :], ind_ref[gi]], axis=1)
        bias = None
        if last:
            bias = jnp.concatenate([position_bias(gi * SEL_GROUP + u, False) for u in range(SEL_GROUP)], axis=0)
        return attend(state, keys, vs_ref, k0, bias, q_sel_all)

    last_group = qb_last // SEL_GROUP
    sel_state = lax.fori_loop(0, last_group, functools.partial(sel_step, last=False), init)
    o_sel = normalised(sel_step(last_group, sel_state, True))

    ws = pl.multiple_of(jnp.maximum(q0 - WINDOW, 0), Q_BLOCK)
    n_win = n_past + Q_TILE // Q_BLOCK
    win_bias = jnp.concatenate([position_bias(ws // Q_BLOCK + u, True) for u in range(n_win)], axis=0)
    win_keys = kw_ref[pl.ds(ws, n_win * Q_BLOCK), :]
    o_win = normalised(attend(init, win_keys, vw_ref, ws, win_bias, q_rot_all))
    heads = []
    for h in range(NSA_KV_HEADS):
        o_c = o_cmp[h]
        for g in range(NSA_GROUP):
            hq = h * NSA_GROUP + g
            c = slice(g * Q_TILE, (g + 1) * Q_TILE)
            ca = slice(hq * Q_TILE, (hq + 1) * Q_TILE)
            heads.append(gsig[3 * hq:3 * hq + 1, :] * o_c[:, c] + gsig[3 * hq + 1:3 * hq + 2, :] * o_sel[:, ca]
                         + gsig[3 * hq + 2:3 * hq + 3, :] * o_win[:, ca])
    cols_out = [jnp.concatenate(heads[2 * m:2 * m + 2], axis=0).T for m in range(NSA_HEADS // 2)]
    o_ref[...] = jnp.concatenate(cols_out, axis=1).astype(BF16)


def _nsa_consts(seq):
    n_cmp_pad = seq // CMP_STRIDE
    n_sel = seq // SEL_LEN
    c = np.arange(n_cmp_pad)
    j = np.arange(n_sel)
    n_cmp = (seq - CMP_LEN) // CMP_STRIDE + 1
    ov = ((c[None, :] * CMP_STRIDE < j[:, None] * SEL_LEN + SEL_LEN) &
          (j[:, None] * SEL_LEN <= c[None, :] * CMP_STRIDE + CMP_LEN - 1) & (c[None, :] < n_cmp))
    kl = np.arange(Q_BLOCK)[:, None]
    tl = np.arange(Q_BLOCK)[None, :]
    n_past = WINDOW // Q_BLOCK
    none = np.zeros((Q_BLOCK, Q_BLOCK), bool)
    valid = [none, kl > tl] + [np.ones((Q_BLOCK, Q_BLOCK), bool)] * (n_past - 1) + [kl <= tl, none]
    wbias = np.where(np.stack(valid), 0.0, NEG_INF).astype(np.float32)
    key_block = np.arange(seq) // SEL_LEN
    onehot = (key_block[:, None] == np.arange(LANES)[None, :]).reshape(-1, SEL_GROUP * Q_BLOCK, LANES)
    return jnp.asarray(ov, BF16), jnp.asarray(wbias), jnp.asarray(onehot, BF16)


def _nsa(qu, qr, kc, vc, ks, vs, kw, vw, ng, consts, batch, seq):
    ov, wbias, onehot = consts
    n_qb = seq // Q_TILE
    tokens = batch * seq
    qblk = pl.BlockSpec((NSA_WIDTH, Q_TILE), lambda b, q: (0, b * n_qb + q))
    cblk = pl.BlockSpec((1, seq // CMP_STRIDE, NSA_KV_WIDTH), lambda b, q: (b, 0, 0))
    kblk = pl.BlockSpec((seq, NSA_KV_WIDTH), lambda b, q: (b, 0))
    vblk = pl.BlockSpec((NSA_KV_HEADS * V_ROWS, seq), lambda b, q: (0, b))
    full = lambda a: pl.BlockSpec(a.shape, lambda b, q: (0,) * a.ndim)
    return pl.pallas_call(
        _nsa_kernel,
        grid=(batch, n_qb),
        in_specs=[qblk, qblk, cblk, cblk, kblk, vblk, kblk, vblk,
                  pl.BlockSpec((LANES, Q_TILE), lambda b, q: (0, b * n_qb + q)), full(ov), full(wbias),
                  full(onehot)],
        out_specs=pl.BlockSpec((Q_TILE, NSA_WIDTH), lambda b, q: (b * n_qb + q, 0)),
        out_shape=jax.ShapeDtypeStruct((tokens, NSA_WIDTH), BF16),
        compiler_params=_params("arbitrary", "arbitrary"),
        name="nsa_attention",
    )(qu, qr, kc, vc, ks, vs, kw, vw, ng, ov, wbias, onehot)


def _outproj_kernel(yr_ref, yc_ref, yn_ref, x_ref, mod_ref, w_ref, lng_ref, lnb_ref, rwh_ref, rwl_ref, rb_ref,
                    x1_ref, h2a_ref, h2b_ref, ti_ref, tw_ref, cnt_ref, *, alpha):
    tm = x_ref.shape[0]
    mix = (_dot(yr_ref[...], w_ref[0, 0:RET_WIDTH, :])
           + _dot(yc_ref[...], w_ref[0, RET_WIDTH:RET_WIDTH + CONV_WIDTH, :])
           + _dot(yn_ref[...], w_ref[0, RET_WIDTH + CONV_WIDTH:, :]))
    x1 = _layer_norm(alpha * x_ref[...] + (1.0 + mod_ref[2, 0]) * mix) * lng_ref[0, 0] + lnb_ref[0, 0]
    x1_ref[...] = x1
    h2 = _layer_norm(x1) * (1.0 + mod_ref[4, 0]) + mod_ref[3, 0]
    for j, part_ref in enumerate((h2a_ref, h2b_ref)):
        part_ref[...] = _pack_pairs(h2[:, j * 2 * SPLIT_WIDTH:(j + 1) * 2 * SPLIT_WIDTH])
    h_hi = h2.astype(BF16)
    h_lo = (h2 - h_hi.astype(F32)).astype(BF16)
    logits = _dot(h_hi, rwh_ref[0]) + _dot(h_lo, rwh_ref[0]) + _dot(h_hi, rwl_ref[0]) + rb_ref[0]

    lane = lax.broadcasted_iota(jnp.int32, (tm, LANES), 1)
    lanef = lane.astype(F32)
    rest = jnp.where(lane < N_EXPERTS, logits, -jnp.inf)
    vals, idxs = [], []
    for _ in range(TOP_K):
        top = jnp.max(rest, axis=-1, keepdims=True)
        idx = jnp.min(jnp.where(rest == top, lanef, float(LANES)), axis=-1, keepdims=True)
        vals.append(top)
        idxs.append(idx)
        rest = jnp.where(lanef == idx, -jnp.inf, rest)
    exps = [jnp.exp(v - vals[0]) for v in vals]
    inv = 1.0 / functools.reduce(lambda a, b: a + b, exps)
    top_w = jnp.zeros((tm, LANES), F32)
    top_i = jnp.zeros((tm, LANES), F32)
    member = jnp.zeros((tm, LANES), F32)
    for k in range(TOP_K):
        top_w = jnp.where(lane == k, exps[k] * inv, top_w)
        top_i = jnp.where(lane == k, idxs[k], top_i)
        member = member + jnp.where(lanef == idxs[k], 1.0, 0.0)
    tw_ref[...] = top_w
    ti_ref[...] = top_i.astype(jnp.int32)

    @pl.when(pl.program_id(0) == 0)
    def _():
        cnt_ref[...] = jnp.zeros_like(cnt_ref)

    cnt_ref[...] += jnp.broadcast_to(jnp.sum(member, axis=0, keepdims=True), cnt_ref.shape)


def _out_projection(y_ret, y_conv, y_nsa, x, mod_l, w_out, ln_g, ln_b, rw_hi, rw_lo, rb, layer, seq, alpha):
    tokens = x.shape[0]
    tm = ROW_TILE
    tiles_per_seq = seq // tm
    row = lambda w: pl.BlockSpec((tm, w), lambda i: (i, 0))
    lay3 = lambda a: pl.BlockSpec((1,) + a.shape[1:], lambda i: (layer,) + (0,) * (a.ndim - 1))
    return pl.pallas_call(
        functools.partial(_outproj_kernel, alpha=alpha),
        grid=(tokens // tm,),
        in_specs=[row(RET_WIDTH), row(CONV_WIDTH), row(NSA_WIDTH), row(D_MODEL),
                  pl.BlockSpec((6, 1, 1, D_MODEL), lambda i: (0, i // tiles_per_seq, 0, 0)),
                  lay3(w_out),
                  pl.BlockSpec((1, 1, 1, D_MODEL), lambda i: (layer, 0, 0, 0)),
                  pl.BlockSpec((1, 1, 1, D_MODEL), lambda i: (layer, 0, 0, 0)),
                  lay3(rw_hi), lay3(rw_lo), lay3(rb)],
        out_specs=[row(D_MODEL)] + [row(SPLIT_WIDTH)] * SC_SPLIT + [row(LANES), row(LANES),
                   pl.BlockSpec((8, LANES), lambda i: (0, 0))],
        out_shape=[jax.ShapeDtypeStruct((tokens, D_MODEL), F32)]
                  + [jax.ShapeDtypeStruct((tokens, SPLIT_WIDTH), jnp.uint32)] * SC_SPLIT
                  + [jax.ShapeDtypeStruct((tokens, LANES), jnp.int32), jax.ShapeDtypeStruct((tokens, LANES), F32),
                   jax.ShapeDtypeStruct((8, LANES), F32)],
        compiler_params=_params("arbitrary"),
        name="outproj_norm_router",
    )(y_ret, y_conv, y_nsa, x, mod_l, w_out, ln_g, ln_b, rw_hi, rw_lo, rb)


def _route_kernel(ti_ref, pstart_ref, tri_ref, pos_ref, carry_ref):
    @pl.when(pl.program_id(0) == 0)
    def _():
        carry_ref[...] = jnp.zeros_like(carry_ref)

    tm = ti_ref.shape[0]
    lane = lax.broadcasted_iota(jnp.int32, (tm, LANES), 1)
    top_i = ti_ref[...]
    onehots = [lane == top_i[:, k:k + 1] for k in range(TOP_K)]
    member = functools.reduce(lambda a, b: a + b, [jnp.where(o, 1.0, 0.0) for o in onehots])
    base = pstart_ref[...] + carry_ref[0:1] + _dot(tri_ref[...], member.astype(BF16))
    pos = jnp.zeros((tm, LANES), F32)
    for k in range(TOP_K):
        pos_k = jnp.sum(jnp.where(onehots[k], base, 0.0), axis=-1, keepdims=True)
        pos = jnp.where(lane == k, pos_k, pos)
    pos_ref[...] = pos.astype(jnp.int32)
    carry_ref[...] += jnp.broadcast_to(jnp.sum(member, axis=0, keepdims=True), carry_ref.shape)


def _route_positions(top_i, pstart):
    tokens = top_i.shape[0]
    tm = min(ROUTE_TILE, tokens)
    r = np.arange(tm)
    tri = jnp.asarray(r[None, :] < r[:, None], BF16)
    return pl.pallas_call(
        _route_kernel,
        grid=(tokens // tm,),
        in_specs=[pl.BlockSpec((tm, LANES), lambda i: (i, 0)),
                  pl.BlockSpec((1, LANES), lambda i: (0, 0)),
                  pl.BlockSpec((tm, tm), lambda i: (0, 0))],
        out_specs=pl.BlockSpec((tm, LANES), lambda i: (i, 0)),
        out_shape=jax.ShapeDtypeStruct((tokens, LANES), jnp.int32),
        scratch_shapes=[pltpu.VMEM((8, LANES), F32)],
        compiler_params=_params("arbitrary"),
        name="route_positions",
    )(top_i, pstart, tri)


def _block_table(counts, n_blocks):
    tm = MOE_TILE
    cnt = counts[0, :N_EXPERTS].astype(jnp.int32)
    nblk = (cnt + tm - 1) // tm
    experts = jnp.arange(N_EXPERTS, dtype=jnp.int32)
    before = experts[:, None] < experts[None, :]
    bstart = jnp.sum(jnp.where(before, nblk[:, None], 0), axis=0)
    bend = bstart + nblk
    n_used = jnp.sum(nblk)
    blocks = jnp.arange(n_blocks, dtype=jnp.int32)
    block_e = jnp.minimum(jnp.sum((bend[None, :] <= blocks[:, None]).astype(jnp.int32), axis=1), N_EXPERTS - 1)
    onehot = block_e[:, None] == experts[None, :]
    lookup = lambda per_expert: jnp.sum(jnp.where(onehot, per_expert[None, :], 0), axis=1)
    live = blocks < n_used
    n_valid = jnp.where(live, jnp.clip(lookup(cnt) - (blocks - lookup(bstart)) * tm, 0, tm), 0)
    pstart = jnp.pad((bstart * tm).astype(F32), (0, LANES - N_EXPERTS))[None, :]
    used = (nblk > 0).astype(jnp.int32)
    first = live & (blocks == lookup(bstart))
    slot = lookup(jnp.sum(jnp.where(before, used[:, None], 0), axis=0)) % 2
    later = jnp.where((used[None, :] > 0) & before, experts[None, :], N_EXPERTS)
    next_expert = jnp.min(later, axis=1)
    next_expert = lookup(jnp.where(next_expert < N_EXPERTS, next_expert, -1))
    plan = jnp.stack([first.astype(jnp.int32), slot, next_expert]).astype(jnp.int32)
    return pstart, block_e, n_used.reshape(1), n_valid.astype(jnp.int32), plan


def _sc_mesh():
    return plsc.VectorSubcoreMesh(core_axis_name="core", subcore_axis_name="subcore")


def _scatter_rows(xs, idx_t, n_rows):
    tokens, width = xs[0].shape
    n_idx = idx_t.shape[0]
    n_x = len(xs)
    win = SC_WINDOW

    @functools.partial(pl.kernel, out_type=[jax.ShapeDtypeStruct((n_rows, width), xs[0].dtype)] * n_x,
                       mesh=_sc_mesh(), scratch_types=[pltpu.SemaphoreType.DMA], name="dispatch_scatter")
    def scatter(*refs):
        x_hbms, i_hbm, o_hbms, sem = refs[:n_x], refs[n_x], refs[n_x + 1:-1], refs[-1]
        idx_specs = [pl.BlockSpec((1, win), functools.partial(lambda i, k: (k, i), k=k)) for k in range(n_idx)]
        for x_hbm, o_hbm in zip(x_hbms, o_hbms):
            def body(x_vmem, *i_vmems, o_hbm=o_hbm):
                copies = [pltpu.async_copy(x_vmem, o_hbm.at[i_vmem.at[0]], sem) for i_vmem in i_vmems]
                for copy in copies:
                    copy.wait()

            pltpu.emit_pipeline(
                body,
                grid=(tokens // win,),
                in_specs=[pl.BlockSpec((win, width), lambda i: (i, 0))] + idx_specs,
                out_specs=[],
                core_axis_name=("core", "subcore"),
                dimension_semantics=(pltpu.PARALLEL,),
            )(x_hbm, *([i_hbm] * n_idx))

    return scatter(*xs, idx_t)


def _gather_rows(xs, idx):
    width = xs[0].shape[1]
    n = idx.shape[0]
    n_x = len(xs)
    win = SC_WINDOW

    @functools.partial(pl.kernel, out_type=[jax.ShapeDtypeStruct((n, width), xs[0].dtype)] * n_x,
                       mesh=_sc_mesh(), scratch_types=[], name="combine_gather")
    def gather(*refs):
        x_hbms, i_hbm, o_hbms = refs[:n_x], refs[n_x], refs[n_x + 1:]
        for x_hbm, o_hbm in zip(x_hbms, o_hbms):
            def body(i_vmem, o_vmem, x_hbm=x_hbm):
                pltpu.sync_copy(x_hbm.at[i_vmem.at[0]], o_vmem)

            pltpu.emit_pipeline(
                body,
                grid=(n // win,),
                in_specs=[pl.BlockSpec((1, win), lambda i: (0, i))],
                out_specs=[pl.BlockSpec((win, width), lambda i: (i, 0))],
                core_axis_name=("core", "subcore"),
                dimension_semantics=(pltpu.PARALLEL,),
            )(i_hbm, o_hbm)

    return gather(*xs, idx.reshape(1, n))


def _moe_kernel(be_ref, nb_ref, nv_ref, plan_ref, ra_ref, rb_ref, wgu_hbm, bgu_ref, wd_hbm, bd_ref,
                oa_ref, ob_ref, wgu_f32, wd_f32, wgu_bf, wd_bf, sem, *, layer):
    row_refs = (ra_ref, rb_ref)
    out_refs = (oa_ref, ob_ref)
    i = pl.program_id(0)
    valid = i < nb_ref[0]

    def weight_copies(expert, slot):
        return (pltpu.make_async_copy(wgu_hbm.at[layer, expert], wgu_f32.at[slot], sem.at[0, slot]),
                pltpu.make_async_copy(wd_hbm.at[layer, expert], wd_f32.at[slot], sem.at[1, slot]))

    @pl.when(i == 0)
    def _():
        for copy in weight_copies(be_ref[0], 0):
            copy.start()

    @pl.when(valid & (plan_ref[0, i] == 1))
    def _():
        slot = plan_ref[1, i]
        for copy in weight_copies(be_ref[i], slot):
            copy.wait()
        upcoming = plan_ref[2, i]

        @pl.when(upcoming >= 0)
        def _():
            for copy in weight_copies(upcoming, 1 - slot):
                copy.start()

        wgu_bf[...] = wgu_f32[slot].astype(BF16)
        wd_bf[...] = wd_f32[slot].astype(BF16)

    def ffn(rs):
        live = rs.start + lax.broadcasted_iota(jnp.int32, (MOE_SUB, SPLIT_WIDTH), 0) < nv_ref[i]
        rows = jnp.concatenate(
            [half.astype(BF16) for r_ref in row_refs
             for half in _unpack_pairs(jnp.where(live, r_ref[rs, :], jnp.uint32(0)))], axis=1)
        def gate_up(c):
            cg = slice(c * FFN_CHUNK, (c + 1) * FFN_CHUNK)
            cu = slice(D_EXPERT + c * FFN_CHUNK, D_EXPERT + (c + 1) * FFN_CHUNK)
            return (_dot(rows, wgu_bf[:, cg]) + bgu_ref[0, 0, :, cg], _dot(rows, wgu_bf[:, cu]) + bgu_ref[0, 0, :, cu])

        def activation(gu):
            g = jnp.minimum(gu[0], SWIGLU_LIMIT)
            u = jnp.clip(gu[1], -SWIGLU_LIMIT, SWIGLU_LIMIT)
            return ((u + 1.0) * (g * jax.nn.sigmoid(SWIGLU_ALPHA * g))).astype(BF16)

        acts = []
        pending = gate_up(0)
        for c in range(1, D_EXPERT // FFN_CHUNK):
            upcoming = gate_up(c)
            acts.append(activation(pending))
            pending = upcoming
        acts.append(activation(pending))
        y = _dot(jnp.concatenate(acts, axis=1), wd_bf[...]) + bd_ref[0, 0]
        for j, o_ref in enumerate(out_refs):
            o_ref[rs, :] = _pack_pairs(y[:, j * 2 * SPLIT_WIDTH:(j + 1) * 2 * SPLIT_WIDTH])

    for sub in range(MOE_TILE // MOE_SUB):
        rs = slice(sub * MOE_SUB, (sub + 1) * MOE_SUB)
        has_rows = valid & (nv_ref[i] > sub * MOE_SUB)
        pl.when(has_rows)(functools.partial(ffn, rs))

        @pl.when(jnp.logical_not(has_rows))
        def _(rs=rs):
            for o_ref in out_refs:
                o_ref[rs, :] = jnp.zeros((MOE_SUB, SPLIT_WIDTH), jnp.uint32)


def _expert_ffn(rows, block_e, n_used, n_valid, plan, w_gu, b_gu, w_down, b_down, layer):
    n_rows = rows[0].shape[0]
    tm = MOE_TILE
    part = pl.BlockSpec((tm, SPLIT_WIDTH), lambda i, be, nb, nv, pn: (i, 0))
    grid_spec = pltpu.PrefetchScalarGridSpec(
        num_scalar_prefetch=4,
        grid=(n_rows // tm,),
        in_specs=[part] * SC_SPLIT + [
                  pl.BlockSpec(memory_space=pl.ANY),
                  pl.BlockSpec((1, 1, 1, 2 * D_EXPERT), lambda i, be, nb, nv, pn: (layer, be[i], 0, 0)),
                  pl.BlockSpec(memory_space=pl.ANY),
                  pl.BlockSpec((1, 1, 1, D_MODEL), lambda i, be, nb, nv, pn: (layer, be[i], 0, 0))],
        out_specs=[part] * SC_SPLIT,
        scratch_shapes=[pltpu.VMEM((2, D_MODEL, 2 * D_EXPERT), F32), pltpu.VMEM((2, D_EXPERT, D_MODEL), F32),
                        pltpu.VMEM((D_MODEL, 2 * D_EXPERT), BF16), pltpu.VMEM((D_EXPERT, D_MODEL), BF16),
                        pltpu.SemaphoreType.DMA((2, 2))])
    return pl.pallas_call(
        functools.partial(_moe_kernel, layer=layer),
        grid_spec=grid_spec,
        out_shape=[jax.ShapeDtypeStruct((n_rows, SPLIT_WIDTH), jnp.uint32)] * SC_SPLIT,
        compiler_params=_params("arbitrary"),
        name="expert_ffn",
    )(block_e, n_used, n_valid, plan, *rows, w_gu, b_gu, w_down, b_down)


def _final_kernel(x_ref, tw_ref, mod_ref, lng_ref, lnb_ref, *rest, alpha):
    y_refs, o_ref = rest[:-1], rest[-1]
    top_w = tw_ref[...]
    parts = []
    for j in range(SC_SPLIT):
        acc = [0.0, 0.0]
        for k in range(TOP_K):
            for half, y in enumerate(_unpack_pairs(y_refs[j * TOP_K + k][...])):
                acc[half] = acc[half] + top_w[:, k:k + 1] * y
        parts += acc
    ffn = jnp.concatenate(parts, axis=1)
    o_ref[...] = (_layer_norm(alpha * x_ref[...] + (1.0 + mod_ref[5, 0]) * ffn) * lng_ref[0, 0] + lnb_ref[0, 0])


def _final_norm(x1, y_parts, top_w, mod_l, ln_g, ln_b, layer, seq, alpha):
    tokens = x1.shape[0]
    tm = ROW_TILE
    tiles = tokens // tm
    tiles_per_seq = seq // tm
    row = lambda w: pl.BlockSpec((tm, w), lambda i: (i, 0))
    vec = pl.BlockSpec((1, 1, 1, D_MODEL), lambda i: (layer, 1, 0, 0))
    y_specs, y_args = [], []
    for j in range(SC_SPLIT):
        for k in range(TOP_K):
            y_specs.append(pl.BlockSpec((tm, SPLIT_WIDTH), functools.partial(lambda i, k: (k * tiles + i, 0), k=k)))
            y_args.append(y_parts[j])
    return pl.pallas_call(
        functools.partial(_final_kernel, alpha=alpha),
        grid=(tiles,),
        in_specs=[row(D_MODEL), row(LANES),
                  pl.BlockSpec((6, 1, 1, D_MODEL), lambda i: (0, i // tiles_per_seq, 0, 0)), vec, vec] + y_specs,
        out_specs=row(D_MODEL),
        out_shape=jax.ShapeDtypeStruct(x1.shape, F32),
        compiler_params=_params("arbitrary"),
        name="combine_final_norm",
    )(x1, top_w, mod_l, ln_g, ln_b, *y_args)


def _inproj_weights(w_in):
    gate_cols = jnp.pad(w_in[:, :, OFF_NG:], ((0, 0), (0, 0), (0, LANES - N_GATE)))
    w_t = jnp.concatenate([w_in[:, :, off:off + width] for off, width in
                           ((OFF_NQ, NSA_WIDTH), (OFF_VS, NSA_KV_WIDTH), (OFF_VW, NSA_KV_WIDTH))] + [gate_cols],
                          axis=2).transpose(0, 2, 1)
    return w_in[:, :, :N_COLS].astype(BF16), w_t.astype(BF16)


def kernel(x, c, positions, w_in, w_out, ret_gn_w, conv_w, cmp_pos, cmp_w1, cmp_w2, ada_w, ada_b, ln_g, ln_b,
           router_w, router_b, w_gate_up, b_gate_up, w_down, b_down):
    batch, seq, _ = x.shape
    depth = w_in.shape[0]
    tokens = batch * seq
    n_rows = tokens * TOP_K + N_EXPERTS * MOE_TILE
    alpha = float((2 * depth) ** 0.25)

    w_in_r, w_in_t = _inproj_weights(w_in)
    w_out_b = w_out.astype(BF16)
    rw = jnp.pad(router_w, ((0, 0), (0, 0), (0, LANES - N_EXPERTS)))
    rw_hi = rw.astype(BF16)
    rw_lo = (rw - rw_hi.astype(F32)).astype(BF16)
    rb = jnp.pad(router_b, ((0, 0), (0, LANES - N_EXPERTS))).reshape(depth, 1, LANES)
    cw = _compress_weights(cmp_pos, cmp_w1, cmp_w2)
    ret_consts = _retention_consts()
    nsa_consts = _nsa_consts(seq)
    gn_w = ret_gn_w.reshape(depth, 1, RET_WIDTH)
    ln_g4 = ln_g.reshape(depth, 2, 1, D_MODEL)
    ln_b4 = ln_b.reshape(depth, 2, 1, D_MODEL)
    b_gu4 = b_gate_up.reshape(depth, N_EXPERTS, 1, 2 * D_EXPERT)
    b_d4 = b_down.reshape(depth, N_EXPERTS, 1, D_MODEL)

    mod = _modulation(c, ada_w, ada_b)
    cos, sin, cos_t, sin_t = _rope_tables(positions)
    xt = x.reshape(tokens, D_MODEL)
    for l in range(depth):
        (rq, rk, rv, rg, y_conv, qu, qr, kcx, vcx, ks, vs, kw, vw, ng) = _input_projection(
            xt, mod[l], w_in_r, w_in_t, cos, sin, cos_t, sin_t, conv_w, l, seq)
        y_ret = _retention(rq, rk, rv, rg, ret_consts, gn_w, l, batch, seq)
        kc, vc = _compress(kcx, vcx, cw, l, batch, seq)
        y_nsa = _nsa(qu, qr, kc, vc, ks, vs, kw, vw, ng, nsa_consts, batch, seq)
        x1, *h2, top_i, top_w, counts = _out_projection(y_ret, y_conv, y_nsa, xt, mod[l], w_out_b, ln_g4, ln_b4,
                                                        rw_hi, rw_lo, rb, l, seq, alpha)
        pstart, block_e, n_used, n_valid, plan = _block_table(counts, n_rows // MOE_TILE)
        pos_t = _route_positions(top_i, pstart)[:, :TOP_K].T
        rows = _scatter_rows(h2, pos_t, n_rows)
        y = _expert_ffn(rows, block_e, n_used, n_valid, plan, w_gate_up, b_gu4, w_down, b_d4, l)
        y_tok = _gather_rows(y, pos_t.reshape(-1))
        xt = _final_norm(x1, y_tok, top_w, mod[l], ln_g4, ln_b4, l, seq, alpha)
    return xt.reshape(batch, seq, D_MODEL)
```

```python
import functools

import numpy as np
import jax
import jax.numpy as jnp
from jax import lax
from jax.experimental import pallas as pl
from jax.experimental.pallas import tpu as pltpu
from jax.experimental.pallas import tpu_sc as plsc

F32 = jnp.float32
BF16 = jnp.bfloat16

D_MODEL = 1024
HEAD_DIM = 64
RET_WIDTH = 256
RET_HEADS = 4
RET_CHUNK = 128
CONV_WIDTH = 256
CONV_K = 3
NSA_WIDTH = 512
NSA_HEADS = 8
NSA_KV_HEADS = 2
NSA_GROUP = 4
NSA_KV_WIDTH = 128
CMP_LEN = 32
CMP_STRIDE = 16
CMP_HIDDEN = 128
SEL_LEN = 64
SEL_TOP = 8
WINDOW = 512
Q_BLOCK = 128
Q_TILE = 2 * Q_BLOCK
ROPE_THETA = 10000.0
N_EXPERTS = 32
TOP_K = 4
D_EXPERT = 1024
SWIGLU_LIMIT = 7.0
SWIGLU_ALPHA = 1.702
LN_EPS = 1e-5
NEG_INF = -1e30
FORCE_SCORE = 1e9

LANES = 128
VMEM_LIMIT = 56 * 1024 * 1024

OFF_RQ, OFF_RK, OFF_RV, OFF_RG = 0, 256, 512, 768
OFF_CB, OFF_CC, OFF_CH = 1024, 1280, 1536
OFF_NQ = 1792
OFF_KC, OFF_VC = 2304, 2432
OFF_KS, OFF_VS, OFF_KW, OFF_VW = 2560, 2688, 2816, 2944
OFF_NG = 3072
N_GATE = NSA_HEADS * 3
N_COLS = OFF_NG
LOG2E = 1.4426950408889634
WT_NQ, WT_VS, WT_VW, WT_NG = 0, NSA_WIDTH, NSA_WIDTH + NSA_KV_WIDTH, NSA_WIDTH + 2 * NSA_KV_WIDTH
WT_ROWS = WT_NG + LANES

RET_BATCH = 4
ROW_TILE = 256
MOE_TILE = 512
MOE_SUB = 256
FFN_CHUNK = 256
ROUTE_TILE = 1024
SEL_GROUP = 4
V_ROWS = HEAD_DIM + 16
SC_WINDOW = LANES
SC_SPLIT = 2
SPLIT_WIDTH = D_MODEL // (2 * SC_SPLIT)


def _dot(a, b):
    return jnp.dot(a, b, preferred_element_type=F32)


def _dot_nt(a, b):
    return lax.dot_general(a, b, (((1,), (1,)), ((), ())), preferred_element_type=F32)


def _layer_norm(x):
    mu = jnp.mean(x, axis=-1, keepdims=True)
    xc = x - mu
    var = jnp.mean(xc * xc, axis=-1, keepdims=True)
    return xc * lax.rsqrt(var + LN_EPS)


def _pack_pairs(x):
    bits = lambda t: lax.bitcast_convert_type(t.astype(BF16).astype(F32), jnp.uint32)
    return (bits(x[:, SPLIT_WIDTH:]) & jnp.uint32(0xFFFF0000)) | (bits(x[:, :SPLIT_WIDTH]) >> 16)


def _unpack_pairs(p):
    low = lax.bitcast_convert_type(p << 16, F32)
    high = lax.bitcast_convert_type(p & jnp.uint32(0xFFFF0000), F32)
    return low, high


def _params(*sem):
    return pltpu.CompilerParams(dimension_semantics=sem, vmem_limit_bytes=VMEM_LIMIT)


def _mod_kernel(c_ref, w_ref, b_ref, o_ref):
    c = c_ref[...]
    ca = (c * jax.nn.sigmoid(c)).astype(BF16)
    o_ref[0, 0] = _dot(ca, w_ref[0].astype(BF16)) + b_ref[0]


def _modulation(c, ada_w, ada_b):
    depth = ada_w.shape[0]
    batch = c.shape[0]
    out = pl.pallas_call(
        _mod_kernel,
        grid=(depth, 6),
        in_specs=[pl.BlockSpec((batch, D_MODEL), lambda l, j: (0, 0)),
                  pl.BlockSpec((1, D_MODEL, D_MODEL), lambda l, j: (l, 0, j)),
                  pl.BlockSpec((1, 1, D_MODEL), lambda l, j: (l * 6 + j, 0, 0))],
        out_specs=pl.BlockSpec((1, 1, batch, D_MODEL), lambda l, j: (l, j, 0, 0)),
        out_shape=jax.ShapeDtypeStruct((depth, 6, batch, D_MODEL), F32),
        compiler_params=_params("arbitrary", "arbitrary"),
        name="adaln_mod",
    )(c, ada_w, ada_b.reshape(depth * 6, 1, D_MODEL))
    return out.reshape(depth, 6, batch, 1, D_MODEL)


def _rope_table_kernel(pos_ref, inv_ref, cos_ref, sin_ref):
    ang = pos_ref[...] * inv_ref[...]
    cos_ref[...] = jnp.cos(ang)
    sin_ref[...] = jnp.sin(ang)


def _rope_tables(positions):
    half = HEAD_DIM // 2
    per_row = LANES // half
    tokens = positions.size
    pos4 = jnp.repeat(positions.reshape(tokens // per_row, per_row).astype(F32), half, axis=1)
    inv = ROPE_THETA ** (-jnp.arange(half, dtype=F32) / half)
    inv4 = jnp.tile(inv, per_row)[None, :]
    rows = tokens // per_row
    tile = min(rows, 1024)
    cos4, sin4 = pl.pallas_call(
        _rope_table_kernel,
        grid=(rows // tile,),
        in_specs=[pl.BlockSpec((tile, LANES), lambda i: (i, 0)),
                  pl.BlockSpec((1, LANES), lambda i: (0, 0))],
        out_specs=[pl.BlockSpec((tile, LANES), lambda i: (i, 0))] * 2,
        out_shape=[jax.ShapeDtypeStruct((rows, LANES), F32)] * 2,
        compiler_params=_params("arbitrary"),
        name="rope_tables",
    )(pos4, inv4)
    cos32 = cos4.reshape(tokens, half)
    sin32 = sin4.reshape(tokens, half)
    cos = jnp.tile(cos32, (1, per_row))
    sign = jnp.tile(jnp.concatenate([-jnp.ones((half,), F32), jnp.ones((half,), F32)]), LANES // HEAD_DIM)
    sin = jnp.tile(sin32, (1, per_row)) * sign[None, :]
    return cos, sin, cos32.T, sin32.T


def _inproj_kernel(x_ref, mod_ref, w_ref, wt_ref, cos_ref, sin_ref, cost_ref, sint_ref, convw_ref,
                   rq_ref, rk_ref, rv_ref, rg_ref, yc_ref, qu_ref, qr_ref, kc_ref, vc_ref,
                   ks_ref, vs_ref, kw_ref, vw_ref, ng_ref, carry_ref, *, tiles_per_seq):
    i = pl.program_id(0)
    tm = x_ref.shape[0]
    h = (_layer_norm(x_ref[...]) * (1.0 + mod_ref[1, 0]) + mod_ref[0, 0]).astype(BF16)
    cosf = cos_ref[...]
    sinf = sin_ref[...]
    lane = lax.broadcasted_iota(jnp.int32, (tm, LANES), 1)
    first_half = (lane % HEAD_DIM) < (HEAD_DIM // 2)

    def proj(off, width):
        return _dot(h, w_ref[0, :, off:off + width])

    def rope(c):
        cols = []
        for j in range(c.shape[1] // LANES):
            cj = c[:, j * LANES:(j + 1) * LANES]
            swapped = jnp.where(first_half, pltpu.roll(cj, LANES - HEAD_DIM // 2, 1),
                                pltpu.roll(cj, HEAD_DIM // 2, 1))
            cols.append(cj * cosf + swapped * sinf)
        return jnp.concatenate(cols, axis=1) if len(cols) > 1 else cols[0]

    scale = HEAD_DIM ** -0.5

    def proj_t(off, width):
        return _dot_nt(wt_ref[0, off:off + width, :], h)

    def store(ref, value):
        ref[...] = value

    def conv_out(dots):
        cb, cc, ch = dots
        u = cc * ch

        @pl.when(i % tiles_per_seq == 0)
        def _():
            carry_ref[...] = jnp.zeros_like(carry_ref)

        carry = carry_ref[...]
        row = lax.broadcasted_iota(jnp.int32, (tm, CONV_WIDTH), 0)
        prev1 = jnp.where(row == 0, carry[7:8], pltpu.roll(u, 1, 0))
        prev2 = jnp.where(row == 0, carry[6:7], jnp.where(row == 1, carry[7:8], pltpu.roll(u, 2, 0)))
        cw = convw_ref[0]
        yc_ref[...] = (cb * (cw[0:1] * prev2 + cw[1:2] * prev1 + cw[2:3] * u)).astype(BF16)
        carry_ref[...] = u[tm - 8:tm]

    def queries_out(nq_t):
        nq_t = nq_t * (scale * LOG2E)
        qu_ref[...] = nq_t.astype(BF16)
        cos_t = cost_ref[...]
        sin_t = sint_ref[...]
        half = HEAD_DIM // 2
        rotated = []
        for hq in range(NSA_HEADS):
            t1 = nq_t[hq * HEAD_DIM:hq * HEAD_DIM + half]
            t2 = nq_t[hq * HEAD_DIM + half:(hq + 1) * HEAD_DIM]
            rotated += [t1 * cos_t - t2 * sin_t, t2 * cos_t + t1 * sin_t]
        qr_ref[...] = jnp.concatenate(rotated, axis=0).astype(BF16)

    def values_t(v_t):
        ones = jnp.ones((V_ROWS - HEAD_DIM, tm), F32)
        parts = []
        for hd in range(NSA_KV_HEADS):
            parts += [v_t[hd * HEAD_DIM:(hd + 1) * HEAD_DIM], ones]
        return jnp.concatenate(parts, axis=0).astype(BF16)

    stages = [
        (lambda: proj(OFF_RQ, RET_WIDTH), lambda d: store(rq_ref, rope(d).astype(BF16))),
        (lambda: proj(OFF_RK, RET_WIDTH), lambda d: store(rk_ref, (rope(d) * scale).astype(BF16))),
        (lambda: proj(OFF_RV, RET_WIDTH), lambda d: store(rv_ref, d.astype(BF16))),
        (lambda: proj(OFF_RG, RET_WIDTH), lambda d: store(rg_ref, (d * jax.nn.sigmoid(d)).astype(BF16))),
        (lambda: (proj(OFF_CB, CONV_WIDTH), proj(OFF_CC, CONV_WIDTH), proj(OFF_CH, CONV_WIDTH)), conv_out),
        (lambda: proj_t(WT_NQ, NSA_WIDTH), queries_out),
        (lambda: proj(OFF_KC, NSA_KV_WIDTH), lambda d: store(kc_ref, d)),
        (lambda: proj(OFF_VC, NSA_KV_WIDTH), lambda d: store(vc_ref, d)),
        (lambda: proj(OFF_KS, NSA_KV_WIDTH), lambda d: store(ks_ref, rope(d).astype(BF16))),
        (lambda: proj_t(WT_VS, NSA_KV_WIDTH), lambda d: store(vs_ref, values_t(d))),
        (lambda: proj(OFF_KW, NSA_KV_WIDTH), lambda d: store(kw_ref, rope(d).astype(BF16))),
        (lambda: proj_t(WT_VW, NSA_KV_WIDTH), lambda d: store(vw_ref, values_t(d))),
        (lambda: proj_t(WT_NG, LANES), lambda d: store(ng_ref, d)),
    ]
    for matmuls, epilogue in stages:
        epilogue(matmuls())


def _input_projection(x, mod_l, w_l, w_t, cos, sin, cos_t, sin_t, conv_w, layer, seq):
    tokens = x.shape[0]
    tm = ROW_TILE
    tiles_per_seq = seq // tm
    row = lambda w: pl.BlockSpec((tm, w), lambda i: (i, 0))
    col = lambda w: pl.BlockSpec((w, tm), lambda i: (0, i))
    outs = [(RET_WIDTH, BF16, False)] * 4 + [(CONV_WIDTH, BF16, False), (NSA_WIDTH, BF16, True),
                                            (NSA_WIDTH, BF16, True), (NSA_KV_WIDTH, F32, False),
                                            (NSA_KV_WIDTH, F32, False), (NSA_KV_WIDTH, BF16, False),
                                            (NSA_KV_HEADS * V_ROWS, BF16, True), (NSA_KV_WIDTH, BF16, False),
                                            (NSA_KV_HEADS * V_ROWS, BF16, True), (LANES, F32, True)]
    return pl.pallas_call(
        functools.partial(_inproj_kernel, tiles_per_seq=tiles_per_seq),
        grid=(tokens // tm,),
        in_specs=[row(D_MODEL),
                  pl.BlockSpec((6, 1, 1, D_MODEL), lambda i: (0, i // tiles_per_seq, 0, 0)),
                  pl.BlockSpec((1, D_MODEL, N_COLS), lambda i: (layer, 0, 0)),
                  pl.BlockSpec((1, WT_ROWS, D_MODEL), lambda i: (layer, 0, 0)),
                  row(LANES), row(LANES), col(HEAD_DIM // 2), col(HEAD_DIM // 2),
                  pl.BlockSpec((1, CONV_K, CONV_WIDTH), lambda i: (layer, 0, 0))],
        out_specs=[col(w) if t else row(w) for w, _, t in outs],
        out_shape=[jax.ShapeDtypeStruct((w, tokens) if t else (tokens, w), dt) for w, dt, t in outs],
        scratch_shapes=[pltpu.VMEM((8, CONV_WIDTH), F32)],
        compiler_params=_params("arbitrary"),
        name="ln_inproj",
    )(x, mod_l, w_l, w_t, cos, sin, cos_t, sin_t, conv_w)


def _retention_kernel(q_ref, k_ref, v_ref, g_ref, intra_ref, qdec_ref, kdec_ref, cdec_ref, gn_ref,
                      o_ref, state_ref):
    @pl.when(pl.program_id(1) == 0)
    def _():
        state_ref[...] = jnp.zeros_like(state_ref)

    n_pairs = RET_WIDTH // LANES
    lane = lax.broadcasted_iota(jnp.int32, (RET_CHUNK, LANES), 1)
    low = lane < HEAD_DIM
    blk_r = lax.broadcasted_iota(jnp.int32, (LANES, LANES), 0) < HEAD_DIM
    blk_c = lax.broadcasted_iota(jnp.int32, (LANES, LANES), 1) < HEAD_DIM
    same_head = blk_r == blk_c
    def half_mean(t):
        lo_sum = jnp.sum(jnp.where(low, t, 0.0), axis=-1, keepdims=True)
        hi_sum = jnp.sum(jnp.where(low, 0.0, t), axis=-1, keepdims=True)
        return jnp.where(low, lo_sum, hi_sum) * (1.0 / HEAD_DIM)

    combos = [(b, p) for b in range(q_ref.shape[0]) for p in range(n_pairs)]
    cols = {p: slice(p * LANES, (p + 1) * LANES) for p in range(n_pairs)}
    keeps = (low, jnp.logical_not(low))
    states = {c: state_ref[c[0], c[1]] for c in combos}
    for ch in range(q_ref.shape[1] // RET_CHUNK):
        rows = slice(ch * RET_CHUNK, (ch + 1) * RET_CHUNK)
        qp = {(b, p): q_ref[b, rows, cols[p]] for b, p in combos}
        kp = {(b, p): k_ref[b, rows, cols[p]] for b, p in combos}
        vp = {(b, p): v_ref[b, rows, cols[p]] for b, p in combos}
        scores = {(c, hh): _dot_nt(jnp.where(keeps[hh], qp[c], jnp.zeros_like(qp[c])), kp[c])
                  for c in combos for hh in range(2)}
        scores = {(c, hh): (s * intra_ref[2 * c[1] + hh]).astype(BF16) for (c, hh), s in scores.items()}
        intra = {key: _dot(s, vp[key[0]]) for key, s in scores.items()}
        inter = {c: _dot((qp[c].astype(F32) * qdec_ref[:, cols[c[1]]]).astype(BF16), states[c].astype(BF16))
                 for c in combos}
        upd = {c: _dot((kp[c].astype(F32) * kdec_ref[:, cols[c[1]]]).T.astype(BF16), vp[c]) for c in combos}
        states = {c: states[c] * cdec_ref[c[1]] + jnp.where(same_head, upd[c], 0.0) for c in combos}
        out = {c: jnp.where(low, intra[c, 0], intra[c, 1]) + inter[c] for c in combos}
        centred = {c: out[c] - half_mean(out[c]) for c in combos}
        normed = {c: centred[c] * lax.rsqrt(half_mean(centred[c] * centred[c]) + LN_EPS) for c in combos}
        for b in range(q_ref.shape[0]):
            full = jnp.concatenate([normed[b, p] for p in range(n_pairs)], axis=1)
            o_ref[b, rows, :] = (full * gn_ref[0] * g_ref[b, rows, :].astype(F32)).astype(BF16)
    for (b, p), state in states.items():
        state_ref[b, p] = state


def _retention_consts():
    heads = jnp.arange(RET_HEADS, dtype=F32)
    log_gamma = jnp.log(1.0 - jnp.power(2.0, -5.0 - heads))
    i = jnp.arange(RET_CHUNK, dtype=F32)
    diff = i[:, None] - i[None, :]
    intra = jnp.where(diff >= 0, jnp.exp(diff * log_gamma[:, None, None]), 0.0)
    q_dec = jnp.exp((i + 1.0) * log_gamma[:, None])
    k_dec = jnp.exp((RET_CHUNK - 1.0 - i) * log_gamma[:, None])
    c_dec = jnp.exp(RET_CHUNK * log_gamma)
    expand = lambda t: jnp.repeat(t.T, HEAD_DIM, axis=1)
    c_rows = jnp.repeat(c_dec, HEAD_DIM).reshape(RET_WIDTH // LANES, LANES, 1)
    c_blk = jnp.broadcast_to(c_rows, (RET_WIDTH // LANES, LANES, LANES))
    return intra, expand(q_dec), expand(k_dec), c_blk


def _retention(rq, rk, rv, rg, consts, gn_w, layer, batch, seq):
    intra, q_dec, k_dec, c_blk = consts
    rows = 2 * RET_CHUNK
    steps = seq // rows
    group = RET_BATCH if batch % RET_BATCH == 0 else 1
    blk = pl.BlockSpec((group, rows, RET_WIDTH), lambda b, s: (b, s, 0))
    full = lambda a: pl.BlockSpec(a.shape, lambda b, s: (0,) * a.ndim)
    per_seq = lambda t: t.reshape(batch, seq, RET_WIDTH)
    out = pl.pallas_call(
        _retention_kernel,
        grid=(batch // group, steps),
        in_specs=[blk, blk, blk, blk, full(intra), full(q_dec), full(k_dec), full(c_blk),
                  pl.BlockSpec((1, 1, RET_WIDTH), lambda b, s: (layer, 0, 0))],
        out_specs=blk,
        out_shape=jax.ShapeDtypeStruct((batch, seq, RET_WIDTH), BF16),
        scratch_shapes=[pltpu.VMEM((group, RET_WIDTH // LANES, LANES, LANES), F32)],
        compiler_params=_params("arbitrary", "arbitrary"),
        name="retention",
    )(per_seq(rq), per_seq(rk), per_seq(rv), per_seq(rg), intra, q_dec, k_dec, c_blk, gn_w)
    return out.reshape(batch * seq, RET_WIDTH)


def _compress_kernel(xk_ref, xv_ref, posa_ref, posb_ref, w1a_ref, w1b_ref, w2_ref, kc_ref, vc_ref):
    n_grp = xk_ref.shape[0] // CMP_STRIDE
    for kv, (x_ref, o_ref) in enumerate(((xk_ref, kc_ref), (xv_ref, vc_ref))):
        ya = jnp.zeros((n_grp, NSA_KV_HEADS * CMP_HIDDEN), F32)
        yb = jnp.zeros((n_grp, NSA_KV_HEADS * CMP_HIDDEN), F32)
        for r in range(CMP_STRIDE):
            x = x_ref[pl.ds(r, n_grp, stride=CMP_STRIDE), :]
            cols = slice(r * NSA_KV_WIDTH, (r + 1) * NSA_KV_WIDTH)
            ya = ya + _dot((x + posa_ref[0, kv, :, cols]).astype(BF16), w1a_ref[0, kv, cols, :])
            yb = yb + _dot((x + posb_ref[0, kv, :, cols]).astype(BF16), w1b_ref[0, kv, cols, :])
        hidden = ya + pltpu.roll(yb, n_grp - 1, 0)
        act = jax.nn.gelu(hidden)
        out = _dot(act.astype(BF16), w2_ref[0, kv])
        o_ref[0] = (out if kv == 0 else out.T).astype(BF16)


def _compress_weights(cmp_pos, cmp_w1, cmp_w2):
    depth = cmp_w1.shape[0]
    eye = jnp.eye(NSA_KV_HEADS, dtype=F32)
    a = cmp_w1.reshape(depth, 2, 2, CMP_STRIDE, HEAD_DIM, CMP_HIDDEN)
    w1 = jnp.einsum('lkardj,hg->lkarhdgj', a, eye)
    w1 = w1.reshape(depth, 2, 2, CMP_STRIDE * NSA_KV_WIDTH, NSA_KV_HEADS * CMP_HIDDEN).astype(BF16)
    pos = cmp_pos.reshape(depth, 2, 2, CMP_STRIDE, 1, HEAD_DIM)
    pos = jnp.broadcast_to(pos, (depth, 2, 2, CMP_STRIDE, NSA_KV_HEADS, HEAD_DIM))
    pos = pos.reshape(depth, 2, 2, 1, CMP_STRIDE * NSA_KV_WIDTH)
    w2 = jnp.einsum('lkje,hg->lkhjge', cmp_w2, eye)
    w2 = w2.reshape(depth, 2, NSA_KV_HEADS * CMP_HIDDEN, NSA_KV_WIDTH).astype(BF16)
    return w1[:, :, 0], w1[:, :, 1], pos[:, :, 0], pos[:, :, 1], w2


def _compress(kcx, vcx, cw, layer, batch, seq):
    w1a, w1b, posa, posb, w2 = cw
    n_grp = seq // CMP_STRIDE
    xblk = pl.BlockSpec((seq, NSA_KV_WIDTH), lambda b: (b, 0))
    lay = lambda a: pl.BlockSpec((1,) + a.shape[1:], lambda b: (layer,) + (0,) * (a.ndim - 1))
    oblk = pl.BlockSpec((1, n_grp, NSA_KV_WIDTH), lambda b: (b, 0, 0))
    return pl.pallas_call(
        _compress_kernel,
        grid=(batch,),
        in_specs=[xblk, xblk, lay(posa), lay(posb), lay(w1a), lay(w1b), lay(w2)],
        out_specs=[oblk, oblk],
        out_shape=[jax.ShapeDtypeStruct((batch, n_grp, NSA_KV_WIDTH), BF16)] * 2,
        compiler_params=_params("arbitrary"),
        name="nsa_compress",
    )(kcx, vcx, posa, posb, w1a, w1b, w2)


def _nsa_kernel(qu_ref, qr_ref, kc_ref, vc_ref, ks_ref, vs_ref, kw_ref, vw_ref, ng_ref, ov_ref, wbias_ref, ind_ref,
                o_ref):
    qb = pl.program_id(1)
    q0 = qb * Q_TILE
    qb_last = (qb + 1) * (Q_TILE // Q_BLOCK) - 1
    n_sel = ov_ref.shape[0]
    cols4 = NSA_GROUP * Q_TILE
    n_past = WINDOW // Q_BLOCK
    gsig = jax.nn.sigmoid(ng_ref[...])

    def padded_q(ref, h):
        zeros = jnp.zeros((HEAD_DIM, Q_TILE), BF16)
        parts = []
        for g in range(NSA_GROUP):
            hq = h * NSA_GROUP + g
            x = ref[hq * HEAD_DIM:(hq + 1) * HEAD_DIM, :]
            parts.append(jnp.concatenate([x, zeros] if h == 0 else [zeros, x], axis=0))
        return jnp.concatenate(parts, axis=1)

    def tile4(t):
        return jnp.concatenate([t] * NSA_GROUP, axis=1)

    def attend(state, keys, values_ref, k0, bias, q_t):
        m, acc = state
        s = _dot(keys, q_t)
        if bias is not None:
            s = s + bias
        m_new = jnp.maximum(m, jnp.max(s, axis=0, keepdims=True))
        e = jnp.exp2(s - m_new).astype(BF16)
        pv = [_dot(values_ref[h * V_ROWS:(h + 1) * V_ROWS, pl.ds(k0, keys.shape[0])],
                   e[:, h * cols4:(h + 1) * cols4]) for h in range(NSA_KV_HEADS)]
        return m_new, jnp.exp2(m - m_new) * acc + jnp.concatenate(pv, axis=1)

    def normalised(state):
        acc = state[1]
        return acc[0:HEAD_DIM] * (1.0 / acc[HEAD_DIM:HEAD_DIM + 1])

    cols_all = NSA_KV_HEADS * cols4
    init = (jnp.full((1, cols_all), NEG_INF, F32), jnp.zeros((V_ROWS, cols_all), F32))
    q_rot = [padded_q(qr_ref, h) for h in range(NSA_KV_HEADS)]

    def position_bias(kb, windowed):
        first = qb * (Q_TILE // Q_BLOCK)
        return jnp.concatenate(
            [wbias_ref[jnp.clip(kb - (first + sub) + n_past + 1, 0 if windowed else 2, n_past + 2)]
             for sub in range(Q_TILE // Q_BLOCK)] * (NSA_GROUP * NSA_KV_HEADS), axis=1)

    n_cols = NSA_KV_HEADS * NSA_GROUP
    qus_all = jnp.concatenate([padded_q(qu_ref, h) for h in range(NSA_KV_HEADS)], axis=1)
    s = _dot(kc_ref[0], qus_all)
    cid = lax.broadcasted_iota(jnp.int32, (LANES, Q_TILE), 0)
    tq = q0 + lax.broadcasted_iota(jnp.int32, (LANES, Q_TILE), 1)
    cvalid = jnp.concatenate([jnp.where(cid * CMP_STRIDE + (CMP_LEN - 1) <= tq, 1.0, 0.0)] * n_cols, axis=1)
    sm = jnp.where(cvalid > 0.5, s, NEG_INF)
    e = jnp.exp2(sm - jnp.max(sm, axis=0, keepdims=True)) * cvalid
    l = jnp.sum(e, axis=0, keepdims=True)
    p = e * (1.0 / jnp.where(l > 0.0, l, 1.0))
    p_bf = p.astype(BF16)
    o_cmp = [_dot(vc_ref[0, h * HEAD_DIM:(h + 1) * HEAD_DIM, :], p_bf[:, h * cols4:(h + 1) * cols4])
             for h in range(NSA_KV_HEADS)]

    psum = jnp.concatenate(
        [functools.reduce(lambda a, b: a + b, [p[:, (h * NSA_GROUP + g) * Q_TILE:(h * NSA_GROUP + g + 1) * Q_TILE]
                                               for g in range(NSA_GROUP)]) for h in range(NSA_KV_HEADS)], axis=1)
    p_hi = psum.astype(BF16)
    rem = psum - p_hi.astype(F32)
    p_mid = rem.astype(BF16)
    p_lo = (rem - p_mid.astype(F32)).astype(BF16)
    ov = ov_ref[...]
    imp = _dot(ov, p_hi) + _dot(ov, p_mid) + _dot(ov, p_lo)
    sel_cols = NSA_KV_HEADS * Q_TILE
    jid = lax.broadcasted_iota(jnp.int32, (n_sel, sel_cols), 0)
    tid = q0 + lax.broadcasted_iota(jnp.int32, (n_sel, sel_cols), 1) % Q_TILE
    forced = (jid == 0) | (jid == jnp.right_shift(tid, 6))
    imp = jnp.where(forced, FORCE_SCORE, jnp.where(jid * SEL_LEN <= tid, imp, -FORCE_SCORE))
    rank = jnp.zeros((n_sel, sel_cols), F32)
    for i in range(n_sel):
        ri = imp[i:i + 1, :]
        beats = (ri > imp) | ((ri == imp) & (jid > i))
        rank = rank + jnp.where(beats, 1.0, 0.0)
    sel_bias = jnp.where(rank < float(SEL_TOP), 0.0, NEG_INF).astype(BF16)
    pad = jnp.zeros((LANES - n_sel, cols4), BF16)
    q_sel = [jnp.concatenate([q_rot[h], tile4(sel_bias[:, h * Q_TILE:(h + 1) * Q_TILE]), pad], axis=0)
             for h in range(NSA_KV_HEADS)]

    q_sel_all = jnp.concatenate(q_sel, axis=1)
    q_rot_all = jnp.concatenate(q_rot, axis=1)

    def sel_step(gi, state, last):
        k0 = pl.multiple_of(gi * (SEL_GROUP * Q_BLOCK), SEL_GROUP * Q_BLOCK)
        keys = jnp.concatenate([ks_ref[pl.ds(k0, SEL_GROUP * Q_BLOCK), :], ind_ref[gi]], axis=1)
        bias = None
        if last:
            bias = jnp.concatenate([position_bias(gi * SEL_GROUP + u, False) for u in range(SEL_GROUP)], axis=0)
        return attend(state, keys, vs_ref, k0, bias, q_sel_all)

    last_group = qb_last // SEL_GROUP
    sel_state = lax.fori_loop(0, last_group, functools.partial(sel_step, last=False), init)
    o_sel = normalised(sel_step(last_group, sel_state, True))

    ws = pl.multiple_of(jnp.maximum(q0 - WINDOW, 0), Q_BLOCK)
    n_win = n_past + Q_TILE // Q_BLOCK
    win_bias = jnp.concatenate([position_bias(ws // Q_BLOCK + u, True) for u in range(n_win)], axis=0)
    win_keys = kw_ref[pl.ds(ws, n_win * Q_BLOCK), :]
    o_win = normalised(attend(init, win_keys, vw_ref, ws, win_bias, q_rot_all))
    heads = []
    for h in range(NSA_KV_HEADS):
        o_c = o_cmp[h]
        for g in range(NSA_GROUP):
            hq = h * NSA_GROUP + g
            c = slice(g * Q_TILE, (g + 1) * Q_TILE)
            ca = slice(hq * Q_TILE, (hq + 1) * Q_TILE)
            heads.append(gsig[3 * hq:3 * hq + 1, :] * o_c[:, c] + gsig[3 * hq + 1:3 * hq + 2, :] * o_sel[:, ca]
                         + gsig[3 * hq + 2:3 * hq + 3, :] * o_win[:, ca])
    cols_out = [jnp.concatenate(heads[2 * m:2 * m + 2], axis=0).T for m in range(NSA_HEADS // 2)]
    o_ref[...] = jnp.concatenate(cols_out, axis=1).astype(BF16)


def _nsa_consts(seq):
    n_cmp_pad = seq // CMP_STRIDE
    n_sel = seq // SEL_LEN
    c = np.arange(n_cmp_pad)
    j = np.arange(n_sel)
    n_cmp = (seq - CMP_LEN) // CMP_STRIDE + 1
    ov = ((c[None, :] * CMP_STRIDE < j[:, None] * SEL_LEN + SEL_LEN) &
          (j[:, None] * SEL_LEN <= c[None, :] * CMP_STRIDE + CMP_LEN - 1) & (c[None, :] < n_cmp))
    kl = np.arange(Q_BLOCK)[:, None]
    tl = np.arange(Q_BLOCK)[None, :]
    n_past = WINDOW // Q_BLOCK
    none = np.zeros((Q_BLOCK, Q_BLOCK), bool)
    valid = [none, kl > tl] + [np.ones((Q_BLOCK, Q_BLOCK), bool)] * (n_past - 1) + [kl <= tl, none]
    wbias = np.where(np.stack(valid), 0.0, NEG_INF).astype(np.float32)
    key_block = np.arange(seq) // SEL_LEN
    onehot = (key_block[:, None] == np.arange(LANES)[None, :]).reshape(-1, SEL_GROUP * Q_BLOCK, LANES)
    return jnp.asarray(ov, BF16), jnp.asarray(wbias), jnp.asarray(onehot, BF16)


def _nsa(qu, qr, kc, vc, ks, vs, kw, vw, ng, consts, batch, seq):
    ov, wbias, onehot = consts
    n_qb = seq // Q_TILE
    tokens = batch * seq
    qblk = pl.BlockSpec((NSA_WIDTH, Q_TILE), lambda b, q: (0, b * n_qb + q))
    cblk = pl.BlockSpec((1, seq // CMP_STRIDE, NSA_KV_WIDTH), lambda b, q: (b, 0, 0))
    kblk = pl.BlockSpec((seq, NSA_KV_WIDTH), lambda b, q: (b, 0))
    vblk = pl.BlockSpec((NSA_KV_HEADS * V_ROWS, seq), lambda b, q: (0, b))
    full = lambda a: pl.BlockSpec(a.shape, lambda b, q: (0,) * a.ndim)
    return pl.pallas_call(
        _nsa_kernel,
        grid=(batch, n_qb),
        in_specs=[qblk, qblk, cblk, cblk, kblk, vblk, kblk, vblk,
                  pl.BlockSpec((LANES, Q_TILE), lambda b, q: (0, b * n_qb + q)), full(ov), full(wbias),
                  full(onehot)],
        out_specs=pl.BlockSpec((Q_TILE, NSA_WIDTH), lambda b, q: (b * n_qb + q, 0)),
        out_shape=jax.ShapeDtypeStruct((tokens, NSA_WIDTH), BF16),
        compiler_params=_params("arbitrary", "arbitrary"),
        name="nsa_attention",
    )(qu, qr, kc, vc, ks, vs, kw, vw, ng, ov, wbias, onehot)


def _outproj_kernel(yr_ref, yc_ref, yn_ref, x_ref, mod_ref, w_ref, lng_ref, lnb_ref, rwh_ref, rwl_ref, rb_ref,
                    x1_ref, h2a_ref, h2b_ref, ti_ref, tw_ref, cnt_ref, *, alpha):
    tm = x_ref.shape[0]
    mix = (_dot(yr_ref[...], w_ref[0, 0:RET_WIDTH, :])
           + _dot(yc_ref[...], w_ref[0, RET_WIDTH:RET_WIDTH + CONV_WIDTH, :])
           + _dot(yn_ref[...], w_ref[0, RET_WIDTH + CONV_WIDTH:, :]))
    x1 = _layer_norm(alpha * x_ref[...] + (1.0 + mod_ref[2, 0]) * mix) * lng_ref[0, 0] + lnb_ref[0, 0]
    x1_ref[...] = x1
    h2 = _layer_norm(x1) * (1.0 + mod_ref[4, 0]) + mod_ref[3, 0]
    for j, part_ref in enumerate((h2a_ref, h2b_ref)):
        part_ref[...] = _pack_pairs(h2[:, j * 2 * SPLIT_WIDTH:(j + 1) * 2 * SPLIT_WIDTH])
    h_hi = h2.astype(BF16)
    h_lo = (h2 - h_hi.astype(F32)).astype(BF16)
    logits = _dot(h_hi, rwh_ref[0]) + _dot(h_lo, rwh_ref[0]) + _dot(h_hi, rwl_ref[0]) + rb_ref[0]

    lane = lax.broadcasted_iota(jnp.int32, (tm, LANES), 1)
    lanef = lane.astype(F32)
    rest = jnp.where(lane < N_EXPERTS, logits, -jnp.inf)
    vals, idxs = [], []
    for _ in range(TOP_K):
        top = jnp.max(rest, axis=-1, keepdims=True)
        idx = jnp.min(jnp.where(rest == top, lanef, float(LANES)), axis=-1, keepdims=True)
        vals.append(top)
        idxs.append(idx)
        rest = jnp.where(lanef == idx, -jnp.inf, rest)
    exps = [jnp.exp(v - vals[0]) for v in vals]
    inv = 1.0 / functools.reduce(lambda a, b: a + b, exps)
    top_w = jnp.zeros((tm, LANES), F32)
    top_i = jnp.zeros((tm, LANES), F32)
    member = jnp.zeros((tm, LANES), F32)
    for k in range(TOP_K):
        top_w = jnp.where(lane == k, exps[k] * inv, top_w)
        top_i = jnp.where(lane == k, idxs[k], top_i)
        member = member + jnp.where(lanef == idxs[k], 1.0, 0.0)
    tw_ref[...] = top_w
    ti_ref[...] = top_i.astype(jnp.int32)

    @pl.when(pl.program_id(0) == 0)
    def _():
        cnt_ref[...] = jnp.zeros_like(cnt_ref)

    cnt_ref[...] += jnp.broadcast_to(jnp.sum(member, axis=0, keepdims=True), cnt_ref.shape)


def _out_projection(y_ret, y_conv, y_nsa, x, mod_l, w_out, ln_g, ln_b, rw_hi, rw_lo, rb, layer, seq, alpha):
    tokens = x.shape[0]
    tm = ROW_TILE
    tiles_per_seq = seq // tm
    row = lambda w: pl.BlockSpec((tm, w), lambda i: (i, 0))
    lay3 = lambda a: pl.BlockSpec((1,) + a.shape[1:], lambda i: (layer,) + (0,) * (a.ndim - 1))
    return pl.pallas_call(
        functools.partial(_outproj_kernel, alpha=alpha),
        grid=(tokens // tm,),
        in_specs=[row(RET_WIDTH), row(CONV_WIDTH), row(NSA_WIDTH), row(D_MODEL),
                  pl.BlockSpec((6, 1, 1, D_MODEL), lambda i: (0, i // tiles_per_seq, 0, 0)),
                  lay3(w_out),
                  pl.BlockSpec((1, 1, 1, D_MODEL), lambda i: (layer, 0, 0, 0)),
                  pl.BlockSpec((1, 1, 1, D_MODEL), lambda i: (layer, 0, 0, 0)),
                  lay3(rw_hi), lay3(rw_lo), lay3(rb)],
        out_specs=[row(D_MODEL)] + [row(SPLIT_WIDTH)] * SC_SPLIT + [row(LANES), row(LANES),
                   pl.BlockSpec((8, LANES), lambda i: (0, 0))],
        out_shape=[jax.ShapeDtypeStruct((tokens, D_MODEL), F32)]
                  + [jax.ShapeDtypeStruct((tokens, SPLIT_WIDTH), jnp.uint32)] * SC_SPLIT
                  + [jax.ShapeDtypeStruct((tokens, LANES), jnp.int32), jax.ShapeDtypeStruct((tokens, LANES), F32),
                   jax.ShapeDtypeStruct((8, LANES), F32)],
        compiler_params=_params("arbitrary"),
        name="outproj_norm_router",
    )(y_ret, y_conv, y_nsa, x, mod_l, w_out, ln_g, ln_b, rw_hi, rw_lo, rb)


def _route_kernel(ti_ref, pstart_ref, tri_ref, pos_ref, carry_ref):
    @pl.when(pl.program_id(0) == 0)
    def _():
        carry_ref[...] = jnp.zeros_like(carry_ref)

    tm = ti_ref.shape[0]
    lane = lax.broadcasted_iota(jnp.int32, (tm, LANES), 1)
    top_i = ti_ref[...]
    onehots = [lane == top_i[:, k:k + 1] for k in range(TOP_K)]
    member = functools.reduce(lambda a, b: a + b, [jnp.where(o, 1.0, 0.0) for o in onehots])
    base = pstart_ref[...] + carry_ref[0:1] + _dot(tri_ref[...], member.astype(BF16))
    pos = jnp.zeros((tm, LANES), F32)
    for k in range(TOP_K):
        pos_k = jnp.sum(jnp.where(onehots[k], base, 0.0), axis=-1, keepdims=True)
        pos = jnp.where(lane == k, pos_k, pos)
    pos_ref[...] = pos.astype(jnp.int32)
    carry_ref[...] += jnp.broadcast_to(jnp.sum(member, axis=0, keepdims=True), carry_ref.shape)


def _route_positions(top_i, pstart):
    tokens = top_i.shape[0]
    tm = min(ROUTE_TILE, tokens)
    r = np.arange(tm)
    tri = jnp.asarray(r[None, :] < r[:, None], BF16)
    return pl.pallas_call(
        _route_kernel,
        grid=(tokens // tm,),
        in_specs=[pl.BlockSpec((tm, LANES), lambda i: (i, 0)),
                  pl.BlockSpec((1, LANES), lambda i: (0, 0)),
                  pl.BlockSpec((tm, tm), lambda i: (0, 0))],
        out_specs=pl.BlockSpec((tm, LANES), lambda i: (i, 0)),
        out_shape=jax.ShapeDtypeStruct((tokens, LANES), jnp.int32),
        scratch_shapes=[pltpu.VMEM((8, LANES), F32)],
        compiler_params=_params("arbitrary"),
        name="route_positions",
    )(top_i, pstart, tri)


def _block_table(counts, n_blocks):
    tm = MOE_TILE
    cnt = counts[0, :N_EXPERTS].astype(jnp.int32)
    nblk = (cnt + tm - 1) // tm
    experts = jnp.arange(N_EXPERTS, dtype=jnp.int32)
    before = experts[:, None] < experts[None, :]
    bstart = jnp.sum(jnp.where(before, nblk[:, None], 0), axis=0)
    bend = bstart + nblk
    n_used = jnp.sum(nblk)
    blocks = jnp.arange(n_blocks, dtype=jnp.int32)
    block_e = jnp.minimum(jnp.sum((bend[None, :] <= blocks[:, None]).astype(jnp.int32), axis=1), N_EXPERTS - 1)
    onehot = block_e[:, None] == experts[None, :]
    lookup = lambda per_expert: jnp.sum(jnp.where(onehot, per_expert[None, :], 0), axis=1)
    live = blocks < n_used
    n_valid = jnp.where(live, jnp.clip(lookup(cnt) - (blocks - lookup(bstart)) * tm, 0, tm), 0)
    pstart = jnp.pad((bstart * tm).astype(F32), (0, LANES - N_EXPERTS))[None, :]
    used = (nblk > 0).astype(jnp.int32)
    first = live & (blocks == lookup(bstart))
    slot = lookup(jnp.sum(jnp.where(before, used[:, None], 0), axis=0)) % 2
    later = jnp.where((used[None, :] > 0) & before, experts[None, :], N_EXPERTS)
    next_expert = jnp.min(later, axis=1)
    next_expert = lookup(jnp.where(next_expert < N_EXPERTS, next_expert, -1))
    plan = jnp.stack([first.astype(jnp.int32), slot, next_expert]).astype(jnp.int32)
    return pstart, block_e, n_used.reshape(1), n_valid.astype(jnp.int32), plan


def _sc_mesh():
    return plsc.VectorSubcoreMesh(core_axis_name="core", subcore_axis_name="subcore")


def _scatter_rows(xs, idx_t, n_rows):
    tokens, width = xs[0].shape
    n_idx = idx_t.shape[0]
    n_x = len(xs)
    win = SC_WINDOW

    @functools.partial(pl.kernel, out_type=[jax.ShapeDtypeStruct((n_rows, width), xs[0].dtype)] * n_x,
                       mesh=_sc_mesh(), scratch_types=[pltpu.SemaphoreType.DMA], name="dispatch_scatter")
    def scatter(*refs):
        x_hbms, i_hbm, o_hbms, sem = refs[:n_x], refs[n_x], refs[n_x + 1:-1], refs[-1]
        idx_specs = [pl.BlockSpec((1, win), functools.partial(lambda i, k: (k, i), k=k)) for k in range(n_idx)]
        for x_hbm, o_hbm in zip(x_hbms, o_hbms):
            def body(x_vmem, *i_vmems, o_hbm=o_hbm):
                copies = [pltpu.async_copy(x_vmem, o_hbm.at[i_vmem.at[0]], sem) for i_vmem in i_vmems]
                for copy in copies:
                    copy.wait()

            pltpu.emit_pipeline(
                body,
                grid=(tokens // win,),
                in_specs=[pl.BlockSpec((win, width), lambda i: (i, 0))] + idx_specs,
                out_specs=[],
                core_axis_name=("core", "subcore"),
                dimension_semantics=(pltpu.PARALLEL,),
            )(x_hbm, *([i_hbm] * n_idx))

    return scatter(*xs, idx_t)


def _gather_rows(xs, idx):
    width = xs[0].shape[1]
    n = idx.shape[0]
    n_x = len(xs)
    win = SC_WINDOW

    @functools.partial(pl.kernel, out_type=[jax.ShapeDtypeStruct((n, width), xs[0].dtype)] * n_x,
                       mesh=_sc_mesh(), scratch_types=[], name="combine_gather")
    def gather(*refs):
        x_hbms, i_hbm, o_hbms = refs[:n_x], refs[n_x], refs[n_x + 1:]
        for x_hbm, o_hbm in zip(x_hbms, o_hbms):
            def body(i_vmem, o_vmem, x_hbm=x_hbm):
                pltpu.sync_copy(x_hbm.at[i_vmem.at[0]], o_vmem)

            pltpu.emit_pipeline(
                body,
                grid=(n // win,),
                in_specs=[pl.BlockSpec((1, win), lambda i: (0, i))],
                out_specs=[pl.BlockSpec((win, width), lambda i: (i, 0))],
                core_axis_name=("core", "subcore"),
                dimension_semantics=(pltpu.PARALLEL,),
            )(i_hbm, o_hbm)

    return gather(*xs, idx.reshape(1, n))


def _moe_kernel(be_ref, nb_ref, nv_ref, plan_ref, ra_ref, rb_ref, wgu_hbm, bgu_ref, wd_hbm, bd_ref,
                oa_ref, ob_ref, wgu_f32, wd_f32, wgu_bf, wd_bf, sem, *, layer):
    row_refs = (ra_ref, rb_ref)
    out_refs = (oa_ref, ob_ref)
    i = pl.program_id(0)
    valid = i < nb_ref[0]

    def weight_copies(expert, slot):
        return (pltpu.make_async_copy(wgu_hbm.at[layer, expert], wgu_f32.at[slot], sem.at[0, slot]),
                pltpu.make_async_copy(wd_hbm.at[layer, expert], wd_f32.at[slot], sem.at[1, slot]))

    @pl.when(i == 0)
    def _():
        for copy in weight_copies(be_ref[0], 0):
            copy.start()

    @pl.when(valid & (plan_ref[0, i] == 1))
    def _():
        slot = plan_ref[1, i]
        for copy in weight_copies(be_ref[i], slot):
            copy.wait()
        upcoming = plan_ref[2, i]

        @pl.when(upcoming >= 0)
        def _():
            for copy in weight_copies(upcoming, 1 - slot):
                copy.start()

        wgu_bf[...] = wgu_f32[slot].astype(BF16)
        wd_bf[...] = wd_f32[slot].astype(BF16)

    def ffn(rs):
        live = lax.broadcasted_iota(jnp.int32, (rs.stop - rs.start, SPLIT_WIDTH), 0) < nv_ref[i] - rs.start
        rows = jnp.concatenate(
            [half.astype(BF16) for r_ref in row_refs
             for half in _unpack_pairs(jnp.where(live, r_ref[rs, :], jnp.uint32(0)))], axis=1)
        def gate_up(c):
            cg = slice(c * FFN_CHUNK, (c + 1) * FFN_CHUNK)
            cu = slice(D_EXPERT + c * FFN_CHUNK, D_EXPERT + (c + 1) * FFN_CHUNK)
            return (_dot(rows, wgu_bf[:, cg]) + bgu_ref[0, 0, :, cg], _dot(rows, wgu_bf[:, cu]) + bgu_ref[0, 0, :, cu])

        def activation(gu):
            g = jnp.minimum(gu[0], SWIGLU_LIMIT)
            u = jnp.clip(gu[1], -SWIGLU_LIMIT, SWIGLU_LIMIT)
            return ((u + 1.0) * (g * jax.nn.sigmoid(SWIGLU_ALPHA * g))).astype(BF16)

        acts = []
        pending = gate_up(0)
        for c in range(1, D_EXPERT // FFN_CHUNK):
            upcoming = gate_up(c)
            acts.append(activation(pending))
            pending = upcoming
        acts.append(activation(pending))
        y = _dot(jnp.concatenate(acts, axis=1), wd_bf[...]) + bd_ref[0, 0]
        for j, o_ref in enumerate(out_refs):
            o_ref[rs, :] = _pack_pairs(y[:, j * 2 * SPLIT_WIDTH:(j + 1) * 2 * SPLIT_WIDTH])

    tail = slice(MOE_SUB, MOE_TILE)
    pl.when(valid & (nv_ref[i] > MOE_SUB))(functools.partial(ffn, slice(0, MOE_TILE)))
    pl.when(valid & (nv_ref[i] <= MOE_SUB))(functools.partial(ffn, slice(0, MOE_SUB)))

    @pl.when(jnp.logical_not(valid))
    def _():
        for o_ref in out_refs:
            o_ref[0:MOE_SUB, :] = jnp.zeros((MOE_SUB, SPLIT_WIDTH), jnp.uint32)

    @pl.when(jnp.logical_not(valid & (nv_ref[i] > MOE_SUB)))
    def _():
        for o_ref in out_refs:
            o_ref[tail, :] = jnp.zeros((MOE_TILE - MOE_SUB, SPLIT_WIDTH), jnp.uint32)


def _expert_ffn(rows, block_e, n_used, n_valid, plan, w_gu, b_gu, w_down, b_down, layer):
    n_rows = rows[0].shape[0]
    tm = MOE_TILE
    part = pl.BlockSpec((tm, SPLIT_WIDTH), lambda i, be, nb, nv, pn: (i, 0))
    grid_spec = pltpu.PrefetchScalarGridSpec(
        num_scalar_prefetch=4,
        grid=(n_rows // tm,),
        in_specs=[part] * SC_SPLIT + [
                  pl.BlockSpec(memory_space=pl.ANY),
                  pl.BlockSpec((1, 1, 1, 2 * D_EXPERT), lambda i, be, nb, nv, pn: (layer, be[i], 0, 0)),
                  pl.BlockSpec(memory_space=pl.ANY),
                  pl.BlockSpec((1, 1, 1, D_MODEL), lambda i, be, nb, nv, pn: (layer, be[i], 0, 0))],
        out_specs=[part] * SC_SPLIT,
        scratch_shapes=[pltpu.VMEM((2, D_MODEL, 2 * D_EXPERT), F32), pltpu.VMEM((2, D_EXPERT, D_MODEL), F32),
                        pltpu.VMEM((D_MODEL, 2 * D_EXPERT), BF16), pltpu.VMEM((D_EXPERT, D_MODEL), BF16),
                        pltpu.SemaphoreType.DMA((2, 2))])
    return pl.pallas_call(
        functools.partial(_moe_kernel, layer=layer),
        grid_spec=grid_spec,
        out_shape=[jax.ShapeDtypeStruct((n_rows, SPLIT_WIDTH), jnp.uint32)] * SC_SPLIT,
        compiler_params=_params("arbitrary"),
        name="expert_ffn",
    )(block_e, n_used, n_valid, plan, *rows, w_gu, b_gu, w_down, b_down)


def _final_kernel(x_ref, tw_ref, mod_ref, lng_ref, lnb_ref, *rest, alpha):
    y_refs, o_ref = rest[:-1], rest[-1]
    top_w = tw_ref[...]
    parts = []
    for j in range(SC_SPLIT):
        acc = [0.0, 0.0]
        for k in range(TOP_K):
            for half, y in enumerate(_unpack_pairs(y_refs[j * TOP_K + k][...])):
                acc[half] = acc[half] + top_w[:, k:k + 1] * y
        parts += acc
    ffn = jnp.concatenate(parts, axis=1)
    o_ref[...] = (_layer_norm(alpha * x_ref[...] + (1.0 + mod_ref[5, 0]) * ffn) * lng_ref[0, 0] + lnb_ref[0, 0])


def _final_norm(x1, y_parts, top_w, mod_l, ln_g, ln_b, layer, seq, alpha):
    tokens = x1.shape[0]
    tm = ROW_TILE
    tiles = tokens // tm
    tiles_per_seq = seq // tm
    row = lambda w: pl.BlockSpec((tm, w), lambda i: (i, 0))
    vec = pl.BlockSpec((1, 1, 1, D_MODEL), lambda i: (layer, 1, 0, 0))
    y_specs, y_args = [], []
    for j in range(SC_SPLIT):
        for k in range(TOP_K):
            y_specs.append(pl.BlockSpec((tm, SPLIT_WIDTH), functools.partial(lambda i, k: (k * tiles + i, 0), k=k)))
            y_args.append(y_parts[j])
    return pl.pallas_call(
        functools.partial(_final_kernel, alpha=alpha),
        grid=(tiles,),
        in_specs=[row(D_MODEL), row(LANES),
                  pl.BlockSpec((6, 1, 1, D_MODEL), lambda i: (0, i // tiles_per_seq, 0, 0)), vec, vec] + y_specs,
        out_specs=row(D_MODEL),
        out_shape=jax.ShapeDtypeStruct(x1.shape, F32),
        compiler_params=_params("arbitrary"),
        name="combine_final_norm",
    )(x1, top_w, mod_l, ln_g, ln_b, *y_args)


def _inproj_weights(w_in):
    gate_cols = jnp.pad(w_in[:, :, OFF_NG:], ((0, 0), (0, 0), (0, LANES - N_GATE)))
    w_t = jnp.concatenate([w_in[:, :, off:off + width] for off, width in
                           ((OFF_NQ, NSA_WIDTH), (OFF_VS, NSA_KV_WIDTH), (OFF_VW, NSA_KV_WIDTH))] + [gate_cols],
                          axis=2).transpose(0, 2, 1)
    return w_in[:, :, :N_COLS].astype(BF16), w_t.astype(BF16)


def kernel(x, c, positions, w_in, w_out, ret_gn_w, conv_w, cmp_pos, cmp_w1, cmp_w2, ada_w, ada_b, ln_g, ln_b,
           router_w, router_b, w_gate_up, b_gate_up, w_down, b_down):
    batch, seq, _ = x.shape
    depth = w_in.shape[0]
    tokens = batch * seq
    n_rows = tokens * TOP_K + N_EXPERTS * MOE_TILE
    alpha = float((2 * depth) ** 0.25)

    w_in_r, w_in_t = _inproj_weights(w_in)
    w_out_b = w_out.astype(BF16)
    rw = jnp.pad(router_w, ((0, 0), (0, 0), (0, LANES - N_EXPERTS)))
    rw_hi = rw.astype(BF16)
    rw_lo = (rw - rw_hi.astype(F32)).astype(BF16)
    rb = jnp.pad(router_b, ((0, 0), (0, LANES - N_EXPERTS))).reshape(depth, 1, LANES)
    cw = _compress_weights(cmp_pos, cmp_w1, cmp_w2)
    ret_consts = _retention_consts()
    nsa_consts = _nsa_consts(seq)
    gn_w = ret_gn_w.reshape(depth, 1, RET_WIDTH)
    ln_g4 = ln_g.reshape(depth, 2, 1, D_MODEL)
    ln_b4 = ln_b.reshape(depth, 2, 1, D_MODEL)
    b_gu4 = b_gate_up.reshape(depth, N_EXPERTS, 1, 2 * D_EXPERT)
    b_d4 = b_down.reshape(depth, N_EXPERTS, 1, D_MODEL)

    mod = _modulation(c, ada_w, ada_b)
    cos, sin, cos_t, sin_t = _rope_tables(positions)
    xt = x.reshape(tokens, D_MODEL)
    for l in range(depth):
        (rq, rk, rv, rg, y_conv, qu, qr, kcx, vcx, ks, vs, kw, vw, ng) = _input_projection(
            xt, mod[l], w_in_r, w_in_t, cos, sin, cos_t, sin_t, conv_w, l, seq)
        y_ret = _retention(rq, rk, rv, rg, ret_consts, gn_w, l, batch, seq)
        kc, vc = _compress(kcx, vcx, cw, l, batch, seq)
        y_nsa = _nsa(qu, qr, kc, vc, ks, vs, kw, vw, ng, nsa_consts, batch, seq)
        x1, *h2, top_i, top_w, counts = _out_projection(y_ret, y_conv, y_nsa, xt, mod[l], w_out_b, ln_g4, ln_b4,
                                                        rw_hi, rw_lo, rb, l, seq, alpha)
        pstart, block_e, n_used, n_valid, plan = _block_table(counts, n_rows // MOE_TILE)
        pos_t = _route_positions(top_i, pstart)[:, :TOP_K].T
        rows = _scatter_rows(h2, pos_t, n_rows)
        y = _expert_ffn(rows, block_e, n_used, n_valid, plan, w_gate_up, b_gu4, w_down, b_d4, l)
        y_tok = _gather_rows(y, pos_t.reshape(-1))
        xt = _final_norm(x1, y_tok, top_w, mod[l], ln_g4, ln_b4, l, seq, alpha)
    return xt.reshape(batch, seq, D_MODEL)
```

```python
import functools

import numpy as np
import jax
import jax.numpy as jnp
from jax import lax
from jax.experimental import pallas as pl
from jax.experimental.pallas import tpu as pltpu
from jax.experimental.pallas import tpu_sc as plsc

F32 = jnp.float32
BF16 = jnp.bfloat16

D_MODEL = 1024
HEAD_DIM = 64
RET_WIDTH = 256
RET_HEADS = 4
RET_CHUNK = 128
CONV_WIDTH = 256
CONV_K = 3
NSA_WIDTH = 512
NSA_HEADS = 8
NSA_KV_HEADS = 2
NSA_GROUP = 4
NSA_KV_WIDTH = 128
CMP_LEN = 32
CMP_STRIDE = 16
CMP_HIDDEN = 128
SEL_LEN = 64
SEL_TOP = 8
WINDOW = 512
Q_BLOCK = 128
Q_TILE = 2 * Q_BLOCK
ROPE_THETA = 10000.0
N_EXPERTS = 32
TOP_K = 4
D_EXPERT = 1024
SWIGLU_LIMIT = 7.0
SWIGLU_ALPHA = 1.702
LN_EPS = 1e-5
NEG_INF = -1e30
FORCE_SCORE = 1e9

LANES = 128
VMEM_LIMIT = 56 * 1024 * 1024

OFF_RQ, OFF_RK, OFF_RV, OFF_RG = 0, 256, 512, 768
OFF_CB, OFF_CC, OFF_CH = 1024, 1280, 1536
OFF_NQ = 1792
OFF_KC, OFF_VC = 2304, 2432
OFF_KS, OFF_VS, OFF_KW, OFF_VW = 2560, 2688, 2816, 2944
OFF_NG = 3072
N_GATE = NSA_HEADS * 3
N_COLS = OFF_NG
LOG2E = 1.4426950408889634
WT_NQ, WT_VS, WT_VW, WT_NG = 0, NSA_WIDTH, NSA_WIDTH + NSA_KV_WIDTH, NSA_WIDTH + 2 * NSA_KV_WIDTH
WT_ROWS = WT_NG + LANES

RET_BATCH = 4
ROW_TILE = 512
INPROJ_TILE = 512
MOE_TILE = 512
MOE_SUB = 256
FFN_CHUNK = 256
ROUTE_TILE = 1024
SEL_GROUP = 4
V_ROWS = HEAD_DIM + 16
SC_WINDOW = LANES
SC_SPLIT = 2
SPLIT_WIDTH = D_MODEL // (2 * SC_SPLIT)


def _dot(a, b):
    return jnp.dot(a, b, preferred_element_type=F32)


def _dot_nt(a, b):
    return lax.dot_general(a, b, (((1,), (1,)), ((), ())), preferred_element_type=F32)


def _layer_norm(x):
    mu = jnp.mean(x, axis=-1, keepdims=True)
    xc = x - mu
    var = jnp.mean(xc * xc, axis=-1, keepdims=True)
    return xc * lax.rsqrt(var + LN_EPS)


def _pack_pairs(x):
    bits = lambda t: lax.bitcast_convert_type(t.astype(BF16).astype(F32), jnp.uint32)
    return (bits(x[:, SPLIT_WIDTH:]) & jnp.uint32(0xFFFF0000)) | (bits(x[:, :SPLIT_WIDTH]) >> 16)


def _unpack_pairs(p):
    low = lax.bitcast_convert_type(p << 16, F32)
    high = lax.bitcast_convert_type(p & jnp.uint32(0xFFFF0000), F32)
    return low, high


def _params(*sem):
    return pltpu.CompilerParams(dimension_semantics=sem, vmem_limit_bytes=VMEM_LIMIT)


def _mod_kernel(c_ref, w_ref, b_ref, o_ref):
    c = c_ref[...]
    ca = (c * jax.nn.sigmoid(c)).astype(BF16)
    o_ref[0, 0] = _dot(ca, w_ref[0].astype(BF16)) + b_ref[0]


def _modulation(c, ada_w, ada_b):
    depth = ada_w.shape[0]
    batch = c.shape[0]
    out = pl.pallas_call(
        _mod_kernel,
        grid=(depth, 6),
        in_specs=[pl.BlockSpec((batch, D_MODEL), lambda l, j: (0, 0)),
                  pl.BlockSpec((1, D_MODEL, D_MODEL), lambda l, j: (l, 0, j)),
                  pl.BlockSpec((1, 1, D_MODEL), lambda l, j: (l * 6 + j, 0, 0))],
        out_specs=pl.BlockSpec((1, 1, batch, D_MODEL), lambda l, j: (l, j, 0, 0)),
        out_shape=jax.ShapeDtypeStruct((depth, 6, batch, D_MODEL), F32),
        compiler_params=_params("arbitrary", "arbitrary"),
        name="adaln_mod",
    )(c, ada_w, ada_b.reshape(depth * 6, 1, D_MODEL))
    return out.reshape(depth, 6, batch, 1, D_MODEL)


def _rope_table_kernel(pos_ref, inv_ref, cos_ref, sin_ref):
    ang = pos_ref[...] * inv_ref[...]
    cos_ref[...] = jnp.cos(ang)
    sin_ref[...] = jnp.sin(ang)


def _rope_tables(positions):
    half = HEAD_DIM // 2
    per_row = LANES // half
    tokens = positions.size
    pos4 = jnp.repeat(positions.reshape(tokens // per_row, per_row).astype(F32), half, axis=1)
    inv = ROPE_THETA ** (-jnp.arange(half, dtype=F32) / half)
    inv4 = jnp.tile(inv, per_row)[None, :]
    rows = tokens // per_row
    tile = min(rows, 1024)
    cos4, sin4 = pl.pallas_call(
        _rope_table_kernel,
        grid=(rows // tile,),
        in_specs=[pl.BlockSpec((tile, LANES), lambda i: (i, 0)),
                  pl.BlockSpec((1, LANES), lambda i: (0, 0))],
        out_specs=[pl.BlockSpec((tile, LANES), lambda i: (i, 0))] * 2,
        out_shape=[jax.ShapeDtypeStruct((rows, LANES), F32)] * 2,
        compiler_params=_params("arbitrary"),
        name="rope_tables",
    )(pos4, inv4)
    cos32 = cos4.reshape(tokens, half)
    sin32 = sin4.reshape(tokens, half)
    cos = jnp.tile(cos32, (1, per_row))
    sign = jnp.tile(jnp.concatenate([-jnp.ones((half,), F32), jnp.ones((half,), F32)]), LANES // HEAD_DIM)
    sin = jnp.tile(sin32, (1, per_row)) * sign[None, :]
    return cos, sin, cos32.T, sin32.T


def _inproj_kernel(x_ref, mod_ref, w_ref, wt_ref, cos_ref, sin_ref, cost_ref, sint_ref, convw_ref,
                   rq_ref, rk_ref, rv_ref, rg_ref, yc_ref, qu_ref, qr_ref, kc_ref, vc_ref,
                   ks_ref, vs_ref, kw_ref, vw_ref, ng_ref, carry_ref, *, tiles_per_seq):
    i = pl.program_id(0)
    tm = x_ref.shape[0]
    h = (_layer_norm(x_ref[...]) * (1.0 + mod_ref[1, 0]) + mod_ref[0, 0]).astype(BF16)
    cosf = cos_ref[...]
    sinf = sin_ref[...]
    lane = lax.broadcasted_iota(jnp.int32, (tm, LANES), 1)
    first_half = (lane % HEAD_DIM) < (HEAD_DIM // 2)

    def proj(off, width):
        return _dot(h, w_ref[0, :, off:off + width])

    def rope(c):
        cols = []
        for j in range(c.shape[1] // LANES):
            cj = c[:, j * LANES:(j + 1) * LANES]
            swapped = jnp.where(first_half, pltpu.roll(cj, LANES - HEAD_DIM // 2, 1),
                                pltpu.roll(cj, HEAD_DIM // 2, 1))
            cols.append(cj * cosf + swapped * sinf)
        return jnp.concatenate(cols, axis=1) if len(cols) > 1 else cols[0]

    scale = HEAD_DIM ** -0.5

    def proj_t(off, width):
        return _dot_nt(wt_ref[0, off:off + width, :], h)

    def store(ref, value):
        ref[...] = value

    def conv_out(dots):
        cb, cc, ch = dots
        u = cc * ch

        @pl.when(i % tiles_per_seq == 0)
        def _():
            carry_ref[...] = jnp.zeros_like(carry_ref)

        carry = carry_ref[...]
        row = lax.broadcasted_iota(jnp.int32, (tm, CONV_WIDTH), 0)
        prev1 = jnp.where(row == 0, carry[7:8], pltpu.roll(u, 1, 0))
        prev2 = jnp.where(row == 0, carry[6:7], jnp.where(row == 1, carry[7:8], pltpu.roll(u, 2, 0)))
        cw = convw_ref[0]
        yc_ref[...] = (cb * (cw[0:1] * prev2 + cw[1:2] * prev1 + cw[2:3] * u)).astype(BF16)
        carry_ref[...] = u[tm - 8:tm]

    def queries_out(nq_t):
        nq_t = nq_t * (scale * LOG2E)
        qu_ref[...] = nq_t.astype(BF16)
        cos_t = cost_ref[...]
        sin_t = sint_ref[...]
        half = HEAD_DIM // 2
        rotated = []
        for hq in range(NSA_HEADS):
            t1 = nq_t[hq * HEAD_DIM:hq * HEAD_DIM + half]
            t2 = nq_t[hq * HEAD_DIM + half:(hq + 1) * HEAD_DIM]
            rotated += [t1 * cos_t - t2 * sin_t, t2 * cos_t + t1 * sin_t]
        qr_ref[...] = jnp.concatenate(rotated, axis=0).astype(BF16)

    def values_t(v_t):
        ones = jnp.ones((V_ROWS - HEAD_DIM, tm), F32)
        parts = []
        for hd in range(NSA_KV_HEADS):
            parts += [v_t[hd * HEAD_DIM:(hd + 1) * HEAD_DIM], ones]
        return jnp.concatenate(parts, axis=0).astype(BF16)

    stages = [
        (lambda: proj(OFF_RQ, RET_WIDTH), lambda d: store(rq_ref, rope(d).astype(BF16))),
        (lambda: proj(OFF_RK, RET_WIDTH), lambda d: store(rk_ref, (rope(d) * scale).astype(BF16))),
        (lambda: proj(OFF_RV, RET_WIDTH), lambda d: store(rv_ref, d.astype(BF16))),
        (lambda: proj(OFF_RG, RET_WIDTH), lambda d: store(rg_ref, (d * jax.nn.sigmoid(d)).astype(BF16))),
        (lambda: (proj(OFF_CB, CONV_WIDTH), proj(OFF_CC, CONV_WIDTH), proj(OFF_CH, CONV_WIDTH)), conv_out),
        (lambda: proj_t(WT_NQ, NSA_WIDTH), queries_out),
        (lambda: proj(OFF_KC, NSA_KV_WIDTH), lambda d: store(kc_ref, d)),
        (lambda: proj(OFF_VC, NSA_KV_WIDTH), lambda d: store(vc_ref, d)),
        (lambda: proj(OFF_KS, NSA_KV_WIDTH), lambda d: store(ks_ref, rope(d).astype(BF16))),
        (lambda: proj_t(WT_VS, NSA_KV_WIDTH), lambda d: store(vs_ref, values_t(d))),
        (lambda: proj(OFF_KW, NSA_KV_WIDTH), lambda d: store(kw_ref, rope(d).astype(BF16))),
        (lambda: proj_t(WT_VW, NSA_KV_WIDTH), lambda d: store(vw_ref, values_t(d))),
        (lambda: proj_t(WT_NG, LANES), lambda d: store(ng_ref, d)),
    ]
    for matmuls, epilogue in stages:
        epilogue(matmuls())


def _input_projection(x, mod_l, w_l, w_t, cos, sin, cos_t, sin_t, conv_w, layer, seq):
    tokens = x.shape[0]
    tm = INPROJ_TILE
    tiles_per_seq = seq // tm
    row = lambda w: pl.BlockSpec((tm, w), lambda i: (i, 0))
    col = lambda w: pl.BlockSpec((w, tm), lambda i: (0, i))
    outs = [(RET_WIDTH, BF16, False)] * 4 + [(CONV_WIDTH, BF16, False), (NSA_WIDTH, BF16, True),
                                            (NSA_WIDTH, BF16, True), (NSA_KV_WIDTH, F32, False),
                                            (NSA_KV_WIDTH, F32, False), (NSA_KV_WIDTH, BF16, False),
                                            (NSA_KV_HEADS * V_ROWS, BF16, True), (NSA_KV_WIDTH, BF16, False),
                                            (NSA_KV_HEADS * V_ROWS, BF16, True), (LANES, F32, True)]
    return pl.pallas_call(
        functools.partial(_inproj_kernel, tiles_per_seq=tiles_per_seq),
        grid=(tokens // tm,),
        in_specs=[row(D_MODEL),
                  pl.BlockSpec((6, 1, 1, D_MODEL), lambda i: (0, i // tiles_per_seq, 0, 0)),
                  pl.BlockSpec((1, D_MODEL, N_COLS), lambda i: (layer, 0, 0)),
                  pl.BlockSpec((1, WT_ROWS, D_MODEL), lambda i: (layer, 0, 0)),
                  row(LANES), row(LANES), col(HEAD_DIM // 2), col(HEAD_DIM // 2),
                  pl.BlockSpec((1, CONV_K, CONV_WIDTH), lambda i: (layer, 0, 0))],
        out_specs=[col(w) if t else row(w) for w, _, t in outs],
        out_shape=[jax.ShapeDtypeStruct((w, tokens) if t else (tokens, w), dt) for w, dt, t in outs],
        scratch_shapes=[pltpu.VMEM((8, CONV_WIDTH), F32)],
        compiler_params=_params("arbitrary"),
        name="ln_inproj",
    )(x, mod_l, w_l, w_t, cos, sin, cos_t, sin_t, conv_w)


def _retention_kernel(q_ref, k_ref, v_ref, g_ref, intra_ref, qdec_ref, kdec_ref, cdec_ref, gn_ref,
                      o_ref, state_ref):
    @pl.when(pl.program_id(1) == 0)
    def _():
        state_ref[...] = jnp.zeros_like(state_ref)

    n_pairs = RET_WIDTH // LANES
    lane = lax.broadcasted_iota(jnp.int32, (RET_CHUNK, LANES), 1)
    low = lane < HEAD_DIM
    blk_r = lax.broadcasted_iota(jnp.int32, (LANES, LANES), 0) < HEAD_DIM
    blk_c = lax.broadcasted_iota(jnp.int32, (LANES, LANES), 1) < HEAD_DIM
    same_head = blk_r == blk_c
    def half_mean(t):
        lo_sum = jnp.sum(jnp.where(low, t, 0.0), axis=-1, keepdims=True)
        hi_sum = jnp.sum(jnp.where(low, 0.0, t), axis=-1, keepdims=True)
        return jnp.where(low, lo_sum, hi_sum) * (1.0 / HEAD_DIM)

    combos = [(b, p) for b in range(q_ref.shape[0]) for p in range(n_pairs)]
    cols = {p: slice(p * LANES, (p + 1) * LANES) for p in range(n_pairs)}
    keeps = (low, jnp.logical_not(low))
    states = {c: state_ref[c[0], c[1]] for c in combos}
    for ch in range(q_ref.shape[1] // RET_CHUNK):
        rows = slice(ch * RET_CHUNK, (ch + 1) * RET_CHUNK)
        qp = {(b, p): q_ref[b, rows, cols[p]] for b, p in combos}
        kp = {(b, p): k_ref[b, rows, cols[p]] for b, p in combos}
        vp = {(b, p): v_ref[b, rows, cols[p]] for b, p in combos}
        scores = {(c, hh): _dot_nt(jnp.where(keeps[hh], qp[c], jnp.zeros_like(qp[c])), kp[c])
                  for c in combos for hh in range(2)}
        scores = {(c, hh): (s * intra_ref[2 * c[1] + hh]).astype(BF16) for (c, hh), s in scores.items()}
        intra = {key: _dot(s, vp[key[0]]) for key, s in scores.items()}
        inter = {c: _dot((qp[c].astype(F32) * qdec_ref[:, cols[c[1]]]).astype(BF16), states[c].astype(BF16))
                 for c in combos}
        upd = {c: _dot((kp[c].astype(F32) * kdec_ref[:, cols[c[1]]]).T.astype(BF16), vp[c]) for c in combos}
        states = {c: states[c] * cdec_ref[c[1]] + jnp.where(same_head, upd[c], 0.0) for c in combos}
        out = {c: jnp.where(low, intra[c, 0], intra[c, 1]) + inter[c] for c in combos}
        centred = {c: out[c] - half_mean(out[c]) for c in combos}
        normed = {c: centred[c] * lax.rsqrt(half_mean(centred[c] * centred[c]) + LN_EPS) for c in combos}
        for b in range(q_ref.shape[0]):
            full = jnp.concatenate([normed[b, p] for p in range(n_pairs)], axis=1)
            o_ref[b, rows, :] = (full * gn_ref[0] * g_ref[b, rows, :].astype(F32)).astype(BF16)
    for (b, p), state in states.items():
        state_ref[b, p] = state


def _retention_consts():
    heads = jnp.arange(RET_HEADS, dtype=F32)
    log_gamma = jnp.log(1.0 - jnp.power(2.0, -5.0 - heads))
    i = jnp.arange(RET_CHUNK, dtype=F32)
    diff = i[:, None] - i[None, :]
    intra = jnp.where(diff >= 0, jnp.exp(diff * log_gamma[:, None, None]), 0.0)
    q_dec = jnp.exp((i + 1.0) * log_gamma[:, None])
    k_dec = jnp.exp((RET_CHUNK - 1.0 - i) * log_gamma[:, None])
    c_dec = jnp.exp(RET_CHUNK * log_gamma)
    expand = lambda t: jnp.repeat(t.T, HEAD_DIM, axis=1)
    c_rows = jnp.repeat(c_dec, HEAD_DIM).reshape(RET_WIDTH // LANES, LANES, 1)
    c_blk = jnp.broadcast_to(c_rows, (RET_WIDTH // LANES, LANES, LANES))
    return intra, expand(q_dec), expand(k_dec), c_blk


def _retention(rq, rk, rv, rg, consts, gn_w, layer, batch, seq):
    intra, q_dec, k_dec, c_blk = consts
    rows = 2 * RET_CHUNK
    steps = seq // rows
    group = RET_BATCH if batch % RET_BATCH == 0 else 1
    blk = pl.BlockSpec((group, rows, RET_WIDTH), lambda b, s: (b, s, 0))
    full = lambda a: pl.BlockSpec(a.shape, lambda b, s: (0,) * a.ndim)
    per_seq = lambda t: t.reshape(batch, seq, RET_WIDTH)
    out = pl.pallas_call(
        _retention_kernel,
        grid=(batch // group, steps),
        in_specs=[blk, blk, blk, blk, full(intra), full(q_dec), full(k_dec), full(c_blk),
                  pl.BlockSpec((1, 1, RET_WIDTH), lambda b, s: (layer, 0, 0))],
        out_specs=blk,
        out_shape=jax.ShapeDtypeStruct((batch, seq, RET_WIDTH), BF16),
        scratch_shapes=[pltpu.VMEM((group, RET_WIDTH // LANES, LANES, LANES), F32)],
        compiler_params=_params("arbitrary", "arbitrary"),
        name="retention",
    )(per_seq(rq), per_seq(rk), per_seq(rv), per_seq(rg), intra, q_dec, k_dec, c_blk, gn_w)
    return out.reshape(batch * seq, RET_WIDTH)


def _compress_kernel(xk_ref, xv_ref, posa_ref, posb_ref, w1a_ref, w1b_ref, w2_ref, kc_ref, vc_ref):
    n_grp = xk_ref.shape[0] // CMP_STRIDE
    for kv, (x_ref, o_ref) in enumerate(((xk_ref, kc_ref), (xv_ref, vc_ref))):
        ya = jnp.zeros((n_grp, NSA_KV_HEADS * CMP_HIDDEN), F32)
        yb = jnp.zeros((n_grp, NSA_KV_HEADS * CMP_HIDDEN), F32)
        for r in range(CMP_STRIDE):
            x = x_ref[pl.ds(r, n_grp, stride=CMP_STRIDE), :]
            cols = slice(r * NSA_KV_WIDTH, (r + 1) * NSA_KV_WIDTH)
            ya = ya + _dot((x + posa_ref[0, kv, :, cols]).astype(BF16), w1a_ref[0, kv, cols, :])
            yb = yb + _dot((x + posb_ref[0, kv, :, cols]).astype(BF16), w1b_ref[0, kv, cols, :])
        hidden = ya + pltpu.roll(yb, n_grp - 1, 0)
        act = jax.nn.gelu(hidden)
        out = _dot(act.astype(BF16), w2_ref[0, kv])
        o_ref[0] = (out if kv == 0 else out.T).astype(BF16)


def _compress_weights(cmp_pos, cmp_w1, cmp_w2):
    depth = cmp_w1.shape[0]
    eye = jnp.eye(NSA_KV_HEADS, dtype=F32)
    a = cmp_w1.reshape(depth, 2, 2, CMP_STRIDE, HEAD_DIM, CMP_HIDDEN)
    w1 = jnp.einsum('lkardj,hg->lkarhdgj', a, eye)
    w1 = w1.reshape(depth, 2, 2, CMP_STRIDE * NSA_KV_WIDTH, NSA_KV_HEADS * CMP_HIDDEN).astype(BF16)
    pos = cmp_pos.reshape(depth, 2, 2, CMP_STRIDE, 1, HEAD_DIM)
    pos = jnp.broadcast_to(pos, (depth, 2, 2, CMP_STRIDE, NSA_KV_HEADS, HEAD_DIM))
    pos = pos.reshape(depth, 2, 2, 1, CMP_STRIDE * NSA_KV_WIDTH)
    w2 = jnp.einsum('lkje,hg->lkhjge', cmp_w2, eye)
    w2 = w2.reshape(depth, 2, NSA_KV_HEADS * CMP_HIDDEN, NSA_KV_WIDTH).astype(BF16)
    return w1[:, :, 0], w1[:, :, 1], pos[:, :, 0], pos[:, :, 1], w2


def _compress(kcx, vcx, cw, layer, batch, seq):
    w1a, w1b, posa, posb, w2 = cw
    n_grp = seq // CMP_STRIDE
    xblk = pl.BlockSpec((seq, NSA_KV_WIDTH), lambda b: (b, 0))
    lay = lambda a: pl.BlockSpec((1,) + a.shape[1:], lambda b: (layer,) + (0,) * (a.ndim - 1))
    oblk = pl.BlockSpec((1, n_grp, NSA_KV_WIDTH), lambda b: (b, 0, 0))
    return pl.pallas_call(
        _compress_kernel,
        grid=(batch,),
        in_specs=[xblk, xblk, lay(posa), lay(posb), lay(w1a), lay(w1b), lay(w2)],
        out_specs=[oblk, oblk],
        out_shape=[jax.ShapeDtypeStruct((batch, n_grp, NSA_KV_WIDTH), BF16)] * 2,
        compiler_params=_params("arbitrary"),
        name="nsa_compress",
    )(kcx, vcx, posa, posb, w1a, w1b, w2)


def _nsa_kernel(qu_ref, qr_ref, kc_ref, vc_ref, ks_ref, vs_ref, kw_ref, vw_ref, ng_ref, ov_ref, wbias_ref, ind_ref,
                o_ref):
    qb = pl.program_id(1)
    q0 = qb * Q_TILE
    qb_last = (qb + 1) * (Q_TILE // Q_BLOCK) - 1
    n_sel = ov_ref.shape[0]
    cols4 = NSA_GROUP * Q_TILE
    n_past = WINDOW // Q_BLOCK
    gsig = jax.nn.sigmoid(ng_ref[...])

    def padded_q(ref, h):
        zeros = jnp.zeros((HEAD_DIM, Q_TILE), BF16)
        parts = []
        for g in range(NSA_GROUP):
            hq = h * NSA_GROUP + g
            x = ref[hq * HEAD_DIM:(hq + 1) * HEAD_DIM, :]
            parts.append(jnp.concatenate([x, zeros] if h == 0 else [zeros, x], axis=0))
        return jnp.concatenate(parts, axis=1)

    def tile4(t):
        return jnp.concatenate([t] * NSA_GROUP, axis=1)

    def attend(state, keys, values_ref, k0, bias, q_t):
        m, acc = state
        s = _dot(keys, q_t)
        if bias is not None:
            s = s + bias
        m_new = jnp.maximum(m, jnp.max(s, axis=0, keepdims=True))
        e = jnp.exp2(s - m_new).astype(BF16)
        pv = [_dot(values_ref[h * V_ROWS:(h + 1) * V_ROWS, pl.ds(k0, keys.shape[0])],
                   e[:, h * cols4:(h + 1) * cols4]) for h in range(NSA_KV_HEADS)]
        return m_new, jnp.exp2(m - m_new) * acc + jnp.concatenate(pv, axis=1)

    def normalised(state):
        acc = state[1]
        return acc[0:HEAD_DIM] * (1.0 / acc[HEAD_DIM:HEAD_DIM + 1])

    cols_all = NSA_KV_HEADS * cols4
    init = (jnp.full((1, cols_all), NEG_INF, F32), jnp.zeros((V_ROWS, cols_all), F32))
    q_rot = [padded_q(qr_ref, h) for h in range(NSA_KV_HEADS)]

    def position_bias(kb, windowed):
        first = qb * (Q_TILE // Q_BLOCK)
        return jnp.concatenate(
            [wbias_ref[jnp.clip(kb - (first + sub) + n_past + 1, 0 if windowed else 2, n_past + 2)]
             for sub in range(Q_TILE // Q_BLOCK)] * (NSA_GROUP * NSA_KV_HEADS), axis=1)

    n_cols = NSA_KV_HEADS * NSA_GROUP
    qus_all = jnp.concatenate([padded_q(qu_ref, h) for h in range(NSA_KV_HEADS)], axis=1)
    s = _dot(kc_ref[0], qus_all)
    cid = lax.broadcasted_iota(jnp.int32, (LANES, Q_TILE), 0)
    tq = q0 + lax.broadcasted_iota(jnp.int32, (LANES, Q_TILE), 1)
    cvalid = jnp.concatenate([jnp.where(cid * CMP_STRIDE + (CMP_LEN - 1) <= tq, 1.0, 0.0)] * n_cols, axis=1)
    sm = jnp.where(cvalid > 0.5, s, NEG_INF)
    e = jnp.exp2(sm - jnp.max(sm, axis=0, keepdims=True)) * cvalid
    l = jnp.sum(e, axis=0, keepdims=True)
    p = e * (1.0 / jnp.where(l > 0.0, l, 1.0))
    p_bf = p.astype(BF16)
    o_cmp = [_dot(vc_ref[0, h * HEAD_DIM:(h + 1) * HEAD_DIM, :], p_bf[:, h * cols4:(h + 1) * cols4])
             for h in range(NSA_KV_HEADS)]

    psum = jnp.concatenate(
        [functools.reduce(lambda a, b: a + b, [p[:, (h * NSA_GROUP + g) * Q_TILE:(h * NSA_GROUP + g + 1) * Q_TILE]
                                               for g in range(NSA_GROUP)]) for h in range(NSA_KV_HEADS)], axis=1)
    p_hi = psum.astype(BF16)
    rem = psum - p_hi.astype(F32)
    p_mid = rem.astype(BF16)
    p_lo = (rem - p_mid.astype(F32)).astype(BF16)
    ov = ov_ref[...]
    imp = _dot(ov, p_hi) + _dot(ov, p_mid) + _dot(ov, p_lo)
    sel_cols = NSA_KV_HEADS * Q_TILE
    jid = lax.broadcasted_iota(jnp.int32, (n_sel, sel_cols), 0)
    tid = q0 + lax.broadcasted_iota(jnp.int32, (n_sel, sel_cols), 1) % Q_TILE
    forced = (jid == 0) | (jid == jnp.right_shift(tid, 6))
    imp = jnp.where(forced, FORCE_SCORE, jnp.where(jid * SEL_LEN <= tid, imp, -FORCE_SCORE))
    rank = jnp.zeros((n_sel, sel_cols), F32)
    for i in range(n_sel):
        ri = imp[i:i + 1, :]
        beats = (ri > imp) | ((ri == imp) & (jid > i))
        rank = rank + jnp.where(beats, 1.0, 0.0)
    sel_bias = jnp.where(rank < float(SEL_TOP), 0.0, NEG_INF).astype(BF16)
    pad = jnp.zeros((LANES - n_sel, cols4), BF16)
    q_sel = [jnp.concatenate([q_rot[h], tile4(sel_bias[:, h * Q_TILE:(h + 1) * Q_TILE]), pad], axis=0)
             for h in range(NSA_KV_HEADS)]

    q_sel_all = jnp.concatenate(q_sel, axis=1)
    q_rot_all = jnp.concatenate(q_rot, axis=1)

    def sel_step(gi, state, last):
        k0 = pl.multiple_of(gi * (SEL_GROUP * Q_BLOCK), SEL_GROUP * Q_BLOCK)
        keys = jnp.concatenate([ks_ref[pl.ds(k0, SEL_GROUP * Q_BLOCK), :], ind_ref[gi]], axis=1)
        bias = None
        if last:
            bias = jnp.concatenate([position_bias(gi * SEL_GROUP + u, False) for u in range(SEL_GROUP)], axis=0)
        return attend(state, keys, vs_ref, k0, bias, q_sel_all)

    last_group = qb_last // SEL_GROUP
    sel_state = lax.fori_loop(0, last_group, functools.partial(sel_step, last=False), init)
    o_sel = normalised(sel_step(last_group, sel_state, True))

    ws = pl.multiple_of(jnp.maximum(q0 - WINDOW, 0), Q_BLOCK)
    n_win = n_past + Q_TILE // Q_BLOCK
    win_bias = jnp.concatenate([position_bias(ws // Q_BLOCK + u, True) for u in range(n_win)], axis=0)
    win_keys = kw_ref[pl.ds(ws, n_win * Q_BLOCK), :]
    o_win = normalised(attend(init, win_keys, vw_ref, ws, win_bias, q_rot_all))
    heads = []
    for h in range(NSA_KV_HEADS):
        o_c = o_cmp[h]
        for g in range(NSA_GROUP):
            hq = h * NSA_GROUP + g
            c = slice(g * Q_TILE, (g + 1) * Q_TILE)
            ca = slice(hq * Q_TILE, (hq + 1) * Q_TILE)
            heads.append(gsig[3 * hq:3 * hq + 1, :] * o_c[:, c] + gsig[3 * hq + 1:3 * hq + 2, :] * o_sel[:, ca]
                         + gsig[3 * hq + 2:3 * hq + 3, :] * o_win[:, ca])
    cols_out = [jnp.concatenate(heads[2 * m:2 * m + 2], axis=0).T for m in range(NSA_HEADS // 2)]
    o_ref[...] = jnp.concatenate(cols_out, axis=1).astype(BF16)


def _nsa_consts(seq):
    n_cmp_pad = seq // CMP_STRIDE
    n_sel = seq // SEL_LEN
    c = np.arange(n_cmp_pad)
    j = np.arange(n_sel)
    n_cmp = (seq - CMP_LEN) // CMP_STRIDE + 1
    ov = ((c[None, :] * CMP_STRIDE < j[:, None] * SEL_LEN + SEL_LEN) &
          (j[:, None] * SEL_LEN <= c[None, :] * CMP_STRIDE + CMP_LEN - 1) & (c[None, :] < n_cmp))
    kl = np.arange(Q_BLOCK)[:, None]
    tl = np.arange(Q_BLOCK)[None, :]
    n_past = WINDOW // Q_BLOCK
    none = np.zeros((Q_BLOCK, Q_BLOCK), bool)
    valid = [none, kl > tl] + [np.ones((Q_BLOCK, Q_BLOCK), bool)] * (n_past - 1) + [kl <= tl, none]
    wbias = np.where(np.stack(valid), 0.0, NEG_INF).astype(np.float32)
    key_block = np.arange(seq) // SEL_LEN
    onehot = (key_block[:, None] == np.arange(LANES)[None, :]).reshape(-1, SEL_GROUP * Q_BLOCK, LANES)
    return jnp.asarray(ov, BF16), jnp.asarray(wbias), jnp.asarray(onehot, BF16)


def _nsa(qu, qr, kc, vc, ks, vs, kw, vw, ng, consts, batch, seq):
    ov, wbias, onehot = consts
    n_qb = seq // Q_TILE
    tokens = batch * seq
    qblk = pl.BlockSpec((NSA_WIDTH, Q_TILE), lambda b, q: (0, b * n_qb + q))
    cblk = pl.BlockSpec((1, seq // CMP_STRIDE, NSA_KV_WIDTH), lambda b, q: (b, 0, 0))
    kblk = pl.BlockSpec((seq, NSA_KV_WIDTH), lambda b, q: (b, 0))
    vblk = pl.BlockSpec((NSA_KV_HEADS * V_ROWS, seq), lambda b, q: (0, b))
    full = lambda a: pl.BlockSpec(a.shape, lambda b, q: (0,) * a.ndim)
    return pl.pallas_call(
        _nsa_kernel,
        grid=(batch, n_qb),
        in_specs=[qblk, qblk, cblk, cblk, kblk, vblk, kblk, vblk,
                  pl.BlockSpec((LANES, Q_TILE), lambda b, q: (0, b * n_qb + q)), full(ov), full(wbias),
                  full(onehot)],
        out_specs=pl.BlockSpec((Q_TILE, NSA_WIDTH), lambda b, q: (b * n_qb + q, 0)),
        out_shape=jax.ShapeDtypeStruct((tokens, NSA_WIDTH), BF16),
        compiler_params=_params("arbitrary", "arbitrary"),
        name="nsa_attention",
    )(qu, qr, kc, vc, ks, vs, kw, vw, ng, ov, wbias, onehot)


def _outproj_kernel(yr_ref, yc_ref, yn_ref, x_ref, mod_ref, w_ref, lng_ref, lnb_ref, rwh_ref, rwl_ref, rb_ref,
                    x1_ref, h2a_ref, h2b_ref, ti_ref, tw_ref, cnt_ref, *, alpha):
    tm = x_ref.shape[0]
    mix = (_dot(yr_ref[...], w_ref[0, 0:RET_WIDTH, :])
           + _dot(yc_ref[...], w_ref[0, RET_WIDTH:RET_WIDTH + CONV_WIDTH, :])
           + _dot(yn_ref[...], w_ref[0, RET_WIDTH + CONV_WIDTH:, :]))
    x1 = _layer_norm(alpha * x_ref[...] + (1.0 + mod_ref[2, 0]) * mix) * lng_ref[0, 0] + lnb_ref[0, 0]
    x1_ref[...] = x1
    h2 = _layer_norm(x1) * (1.0 + mod_ref[4, 0]) + mod_ref[3, 0]
    for j, part_ref in enumerate((h2a_ref, h2b_ref)):
        part_ref[...] = _pack_pairs(h2[:, j * 2 * SPLIT_WIDTH:(j + 1) * 2 * SPLIT_WIDTH])
    h_hi = h2.astype(BF16)
    h_lo = (h2 - h_hi.astype(F32)).astype(BF16)
    logits = _dot(h_hi, rwh_ref[0]) + _dot(h_lo, rwh_ref[0]) + _dot(h_hi, rwl_ref[0]) + rb_ref[0]

    lane = lax.broadcasted_iota(jnp.int32, (tm, LANES), 1)
    lanef = lane.astype(F32)
    rest = jnp.where(lane < N_EXPERTS, logits, -jnp.inf)
    vals, idxs = [], []
    for _ in range(TOP_K):
        top = jnp.max(rest, axis=-1, keepdims=True)
        idx = jnp.min(jnp.where(rest == top, lanef, float(LANES)), axis=-1, keepdims=True)
        vals.append(top)
        idxs.append(idx)
        rest = jnp.where(lanef == idx, -jnp.inf, rest)
    exps = [jnp.exp(v - vals[0]) for v in vals]
    inv = 1.0 / functools.reduce(lambda a, b: a + b, exps)
    top_w = jnp.zeros((tm, LANES), F32)
    top_i = jnp.zeros((tm, LANES), F32)
    member = jnp.zeros((tm, LANES), F32)
    for k in range(TOP_K):
        top_w = jnp.where(lane == k, exps[k] * inv, top_w)
        top_i = jnp.where(lane == k, idxs[k], top_i)
        member = member + jnp.where(lanef == idxs[k], 1.0, 0.0)
    tw_ref[...] = top_w
    ti_ref[...] = top_i.astype(jnp.int32)

    @pl.when(pl.program_id(0) == 0)
    def _():
        cnt_ref[...] = jnp.zeros_like(cnt_ref)

    cnt_ref[...] += jnp.broadcast_to(jnp.sum(member, axis=0, keepdims=True), cnt_ref.shape)


def _out_projection(y_ret, y_conv, y_nsa, x, mod_l, w_out, ln_g, ln_b, rw_hi, rw_lo, rb, layer, seq, alpha):
    tokens = x.shape[0]
    tm = ROW_TILE
    tiles_per_seq = seq // tm
    row = lambda w: pl.BlockSpec((tm, w), lambda i: (i, 0))
    lay3 = lambda a: pl.BlockSpec((1,) + a.shape[1:], lambda i: (layer,) + (0,) * (a.ndim - 1))
    return pl.pallas_call(
        functools.partial(_outproj_kernel, alpha=alpha),
        grid=(tokens // tm,),
        in_specs=[row(RET_WIDTH), row(CONV_WIDTH), row(NSA_WIDTH), row(D_MODEL),
                  pl.BlockSpec((6, 1, 1, D_MODEL), lambda i: (0, i // tiles_per_seq, 0, 0)),
                  lay3(w_out),
                  pl.BlockSpec((1, 1, 1, D_MODEL), lambda i: (layer, 0, 0, 0)),
                  pl.BlockSpec((1, 1, 1, D_MODEL), lambda i: (layer, 0, 0, 0)),
                  lay3(rw_hi), lay3(rw_lo), lay3(rb)],
        out_specs=[row(D_MODEL)] + [row(SPLIT_WIDTH)] * SC_SPLIT + [row(LANES), row(LANES),
                   pl.BlockSpec((8, LANES), lambda i: (0, 0))],
        out_shape=[jax.ShapeDtypeStruct((tokens, D_MODEL), F32)]
                  + [jax.ShapeDtypeStruct((tokens, SPLIT_WIDTH), jnp.uint32)] * SC_SPLIT
                  + [jax.ShapeDtypeStruct((tokens, LANES), jnp.int32), jax.ShapeDtypeStruct((tokens, LANES), F32),
                   jax.ShapeDtypeStruct((8, LANES), F32)],
        compiler_params=_params("arbitrary"),
        name="outproj_norm_router",
    )(y_ret, y_conv, y_nsa, x, mod_l, w_out, ln_g, ln_b, rw_hi, rw_lo, rb)


def _route_kernel(ti_ref, pstart_ref, tri_ref, pos_ref, carry_ref):
    @pl.when(pl.program_id(0) == 0)
    def _():
        carry_ref[...] = jnp.zeros_like(carry_ref)

    tm = ti_ref.shape[0]
    lane = lax.broadcasted_iota(jnp.int32, (tm, LANES), 1)
    top_i = ti_ref[...]
    onehots = [lane == top_i[:, k:k + 1] for k in range(TOP_K)]
    member = functools.reduce(lambda a, b: a + b, [jnp.where(o, 1.0, 0.0) for o in onehots])
    base = pstart_ref[...] + carry_ref[0:1] + _dot(tri_ref[...], member.astype(BF16))
    pos = jnp.zeros((tm, LANES), F32)
    for k in range(TOP_K):
        pos_k = jnp.sum(jnp.where(onehots[k], base, 0.0), axis=-1, keepdims=True)
        pos = jnp.where(lane == k, pos_k, pos)
    pos_ref[...] = pos.astype(jnp.int32)
    carry_ref[...] += jnp.broadcast_to(jnp.sum(member, axis=0, keepdims=True), carry_ref.shape)


def _route_positions(top_i, pstart):
    tokens = top_i.shape[0]
    tm = min(ROUTE_TILE, tokens)
    r = np.arange(tm)
    tri = jnp.asarray(r[None, :] < r[:, None], BF16)
    return pl.pallas_call(
        _route_kernel,
        grid=(tokens // tm,),
        in_specs=[pl.BlockSpec((tm, LANES), lambda i: (i, 0)),
                  pl.BlockSpec((1, LANES), lambda i: (0, 0)),
                  pl.BlockSpec((tm, tm), lambda i: (0, 0))],
        out_specs=pl.BlockSpec((tm, LANES), lambda i: (i, 0)),
        out_shape=jax.ShapeDtypeStruct((tokens, LANES), jnp.int32),
        scratch_shapes=[pltpu.VMEM((8, LANES), F32)],
        compiler_params=_params("arbitrary"),
        name="route_positions",
    )(top_i, pstart, tri)


def _block_table(counts, n_blocks):
    tm = MOE_TILE
    cnt = counts[0, :N_EXPERTS].astype(jnp.int32)
    nblk = (cnt + tm - 1) // tm
    experts = jnp.arange(N_EXPERTS, dtype=jnp.int32)
    before = experts[:, None] < experts[None, :]
    bstart = jnp.sum(jnp.where(before, nblk[:, None], 0), axis=0)
    bend = bstart + nblk
    n_used = jnp.sum(nblk)
    blocks = jnp.arange(n_blocks, dtype=jnp.int32)
    block_e = jnp.minimum(jnp.sum((bend[None, :] <= blocks[:, None]).astype(jnp.int32), axis=1), N_EXPERTS - 1)
    onehot = block_e[:, None] == experts[None, :]
    lookup = lambda per_expert: jnp.sum(jnp.where(onehot, per_expert[None, :], 0), axis=1)
    live = blocks < n_used
    n_valid = jnp.where(live, jnp.clip(lookup(cnt) - (blocks - lookup(bstart)) * tm, 0, tm), 0)
    pstart = jnp.pad((bstart * tm).astype(F32), (0, LANES - N_EXPERTS))[None, :]
    used = (nblk > 0).astype(jnp.int32)
    first = live & (blocks == lookup(bstart))
    slot = lookup(jnp.sum(jnp.where(before, used[:, None], 0), axis=0)) % 2
    later = jnp.where((used[None, :] > 0) & before, experts[None, :], N_EXPERTS)
    next_expert = jnp.min(later, axis=1)
    next_expert = lookup(jnp.where(next_expert < N_EXPERTS, next_expert, -1))
    plan = jnp.stack([first.astype(jnp.int32), slot, next_expert]).astype(jnp.int32)
    return pstart, block_e, n_used.reshape(1), n_valid.astype(jnp.int32), plan


def _sc_mesh():
    return plsc.VectorSubcoreMesh(core_axis_name="core", subcore_axis_name="subcore")


def _scatter_rows(xs, idx_t, n_rows):
    tokens, width = xs[0].shape
    n_idx = idx_t.shape[0]
    n_x = len(xs)
    win = SC_WINDOW

    @functools.partial(pl.kernel, out_type=[jax.ShapeDtypeStruct((n_rows, width), xs[0].dtype)] * n_x,
                       mesh=_sc_mesh(), scratch_types=[pltpu.SemaphoreType.DMA], name="dispatch_scatter")
    def scatter(*refs):
        x_hbms, i_hbm, o_hbms, sem = refs[:n_x], refs[n_x], refs[n_x + 1:-1], refs[-1]
        idx_specs = [pl.BlockSpec((1, win), functools.partial(lambda i, k: (k, i), k=k)) for k in range(n_idx)]
        for x_hbm, o_hbm in zip(x_hbms, o_hbms):
            def body(x_vmem, *i_vmems, o_hbm=o_hbm):
                copies = [pltpu.async_copy(x_vmem, o_hbm.at[i_vmem.at[0]], sem) for i_vmem in i_vmems]
                for copy in copies:
                    copy.wait()

            pltpu.emit_pipeline(
                body,
                grid=(tokens // win,),
                in_specs=[pl.BlockSpec((win, width), lambda i: (i, 0))] + idx_specs,
                out_specs=[],
                core_axis_name=("core", "subcore"),
                dimension_semantics=(pltpu.PARALLEL,),
            )(x_hbm, *([i_hbm] * n_idx))

    return scatter(*xs, idx_t)


def _gather_rows(xs, idx):
    width = xs[0].shape[1]
    n = idx.shape[0]
    n_x = len(xs)
    win = SC_WINDOW

    @functools.partial(pl.kernel, out_type=[jax.ShapeDtypeStruct((n, width), xs[0].dtype)] * n_x,
                       mesh=_sc_mesh(), scratch_types=[], name="combine_gather")
    def gather(*refs):
        x_hbms, i_hbm, o_hbms = refs[:n_x], refs[n_x], refs[n_x + 1:]
        for x_hbm, o_hbm in zip(x_hbms, o_hbms):
            def body(i_vmem, o_vmem, x_hbm=x_hbm):
                pltpu.sync_copy(x_hbm.at[i_vmem.at[0]], o_vmem)

            pltpu.emit_pipeline(
                body,
                grid=(n // win,),
                in_specs=[pl.BlockSpec((1, win), lambda i: (0, i))],
                out_specs=[pl.BlockSpec((win, width), lambda i: (i, 0))],
                core_axis_name=("core", "subcore"),
                dimension_semantics=(pltpu.PARALLEL,),
            )(i_hbm, o_hbm)

    return gather(*xs, idx.reshape(1, n))


def _moe_kernel(be_ref, nb_ref, nv_ref, plan_ref, ra_ref, rb_ref, wgu_hbm, bgu_ref, wd_hbm, bd_ref,
                oa_ref, ob_ref, wgu_f32, wd_f32, wgu_bf, wd_bf, sem, *, layer):
    row_refs = (ra_ref, rb_ref)
    out_refs = (oa_ref, ob_ref)
    i = pl.program_id(0)
    valid = i < nb_ref[0]

    def weight_copies(expert, slot):
        return (pltpu.make_async_copy(wgu_hbm.at[layer, expert], wgu_f32.at[slot], sem.at[0, slot]),
                pltpu.make_async_copy(wd_hbm.at[layer, expert], wd_f32.at[slot], sem.at[1, slot]))

    @pl.when(i == 0)
    def _():
        for copy in weight_copies(be_ref[0], 0):
            copy.start()

    @pl.when(valid & (plan_ref[0, i] == 1))
    def _():
        slot = plan_ref[1, i]
        for copy in weight_copies(be_ref[i], slot):
            copy.wait()
        upcoming = plan_ref[2, i]

        @pl.when(upcoming >= 0)
        def _():
            for copy in weight_copies(upcoming, 1 - slot):
                copy.start()

        wgu_bf[...] = wgu_f32[slot].astype(BF16)
        wd_bf[...] = wd_f32[slot].astype(BF16)

    def ffn(rs):
        live = lax.broadcasted_iota(jnp.int32, (rs.stop - rs.start, SPLIT_WIDTH), 0) < nv_ref[i] - rs.start
        rows = jnp.concatenate(
            [half.astype(BF16) for r_ref in row_refs
             for half in _unpack_pairs(jnp.where(live, r_ref[rs, :], jnp.uint32(0)))], axis=1)
        def gate_up(c):
            cg = slice(c * FFN_CHUNK, (c + 1) * FFN_CHUNK)
            cu = slice(D_EXPERT + c * FFN_CHUNK, D_EXPERT + (c + 1) * FFN_CHUNK)
            return (_dot(rows, wgu_bf[:, cg]) + bgu_ref[0, 0, :, cg], _dot(rows, wgu_bf[:, cu]) + bgu_ref[0, 0, :, cu])

        def activation(gu):
            g = jnp.minimum(gu[0], SWIGLU_LIMIT)
            u = jnp.clip(gu[1], -SWIGLU_LIMIT, SWIGLU_LIMIT)
            return ((u + 1.0) * (g * jax.nn.sigmoid(SWIGLU_ALPHA * g))).astype(BF16)

        acts = []
        pending = gate_up(0)
        for c in range(1, D_EXPERT // FFN_CHUNK):
            upcoming = gate_up(c)
            acts.append(activation(pending))
            pending = upcoming
        acts.append(activation(pending))
        y = _dot(jnp.concatenate(acts, axis=1), wd_bf[...]) + bd_ref[0, 0]
        for j, o_ref in enumerate(out_refs):
            o_ref[rs, :] = _pack_pairs(y[:, j * 2 * SPLIT_WIDTH:(j + 1) * 2 * SPLIT_WIDTH])

    tail = slice(MOE_SUB, MOE_TILE)
    pl.when(valid & (nv_ref[i] > MOE_SUB))(functools.partial(ffn, slice(0, MOE_TILE)))
    pl.when(valid & (nv_ref[i] <= MOE_SUB))(functools.partial(ffn, slice(0, MOE_SUB)))

    @pl.when(jnp.logical_not(valid))
    def _():
        for o_ref in out_refs:
            o_ref[0:MOE_SUB, :] = jnp.zeros((MOE_SUB, SPLIT_WIDTH), jnp.uint32)

    @pl.when(jnp.logical_not(valid & (nv_ref[i] > MOE_SUB)))
    def _():
        for o_ref in out_refs:
            o_ref[tail, :] = jnp.zeros((MOE_TILE - MOE_SUB, SPLIT_WIDTH), jnp.uint32)


def _expert_ffn(rows, block_e, n_used, n_valid, plan, w_gu, b_gu, w_down, b_down, layer):
    n_rows = rows[0].shape[0]
    tm = MOE_TILE
    part = pl.BlockSpec((tm, SPLIT_WIDTH), lambda i, be, nb, nv, pn: (i, 0))
    grid_spec = pltpu.PrefetchScalarGridSpec(
        num_scalar_prefetch=4,
        grid=(n_rows // tm,),
        in_specs=[part] * SC_SPLIT + [
                  pl.BlockSpec(memory_space=pl.ANY),
                  pl.BlockSpec((1, 1, 1, 2 * D_EXPERT), lambda i, be, nb, nv, pn: (layer, be[i], 0, 0)),
                  pl.BlockSpec(memory_space=pl.ANY),
                  pl.BlockSpec((1, 1, 1, D_MODEL), lambda i, be, nb, nv, pn: (layer, be[i], 0, 0))],
        out_specs=[part] * SC_SPLIT,
        scratch_shapes=[pltpu.VMEM((2, D_MODEL, 2 * D_EXPERT), F32), pltpu.VMEM((2, D_EXPERT, D_MODEL), F32),
                        pltpu.VMEM((D_MODEL, 2 * D_EXPERT), BF16), pltpu.VMEM((D_EXPERT, D_MODEL), BF16),
                        pltpu.SemaphoreType.DMA((2, 2))])
    return pl.pallas_call(
        functools.partial(_moe_kernel, layer=layer),
        grid_spec=grid_spec,
        out_shape=[jax.ShapeDtypeStruct((n_rows, SPLIT_WIDTH), jnp.uint32)] * SC_SPLIT,
        compiler_params=_params("arbitrary"),
        name="expert_ffn",
    )(block_e, n_used, n_valid, plan, *rows, w_gu, b_gu, w_down, b_down)


def _final_kernel(x_ref, tw_ref, mod_ref, lng_ref, lnb_ref, *rest, alpha):
    y_refs, o_ref = rest[:-1], rest[-1]
    top_w = tw_ref[...]
    parts = []
    for j in range(SC_SPLIT):
        acc = [0.0, 0.0]
        for k in range(TOP_K):
            for half, y in enumerate(_unpack_pairs(y_refs[j * TOP_K + k][...])):
                acc[half] = acc[half] + top_w[:, k:k + 1] * y
        parts += acc
    ffn = jnp.concatenate(parts, axis=1)
    o_ref[...] = (_layer_norm(alpha * x_ref[...] + (1.0 + mod_ref[5, 0]) * ffn) * lng_ref[0, 0] + lnb_ref[0, 0])


def _final_norm(x1, y_parts, top_w, mod_l, ln_g, ln_b, layer, seq, alpha):
    tokens = x1.shape[0]
    tm = ROW_TILE
    tiles = tokens // tm
    tiles_per_seq = seq // tm
    row = lambda w: pl.BlockSpec((tm, w), lambda i: (i, 0))
    vec = pl.BlockSpec((1, 1, 1, D_MODEL), lambda i: (layer, 1, 0, 0))
    y_specs, y_args = [], []
    for j in range(SC_SPLIT):
        for k in range(TOP_K):
            y_specs.append(pl.BlockSpec((tm, SPLIT_WIDTH), functools.partial(lambda i, k: (k * tiles + i, 0), k=k)))
            y_args.append(y_parts[j])
    return pl.pallas_call(
        functools.partial(_final_kernel, alpha=alpha),
        grid=(tiles,),
        in_specs=[row(D_MODEL), row(LANES),
                  pl.BlockSpec((6, 1, 1, D_MODEL), lambda i: (0, i // tiles_per_seq, 0, 0)), vec, vec] + y_specs,
        out_specs=row(D_MODEL),
        out_shape=jax.ShapeDtypeStruct(x1.shape, F32),
        compiler_params=_params("arbitrary"),
        name="combine_final_norm",
    )(x1, top_w, mod_l, ln_g, ln_b, *y_args)


def _inproj_weights(w_in):
    gate_cols = jnp.pad(w_in[:, :, OFF_NG:], ((0, 0), (0, 0), (0, LANES - N_GATE)))
    w_t = jnp.concatenate([w_in[:, :, off:off + width] for off, width in
                           ((OFF_NQ, NSA_WIDTH), (OFF_VS, NSA_KV_WIDTH), (OFF_VW, NSA_KV_WIDTH))] + [gate_cols],
                          axis=2).transpose(0, 2, 1)
    return w_in[:, :, :N_COLS].astype(BF16), w_t.astype(BF16)


def kernel(x, c, positions, w_in, w_out, ret_gn_w, conv_w, cmp_pos, cmp_w1, cmp_w2, ada_w, ada_b, ln_g, ln_b,
           router_w, router_b, w_gate_up, b_gate_up, w_down, b_down):
    batch, seq, _ = x.shape
    depth = w_in.shape[0]
    tokens = batch * seq
    n_rows = tokens * TOP_K + N_EXPERTS * MOE_TILE
    alpha = float((2 * depth) ** 0.25)

    w_in_r, w_in_t = _inproj_weights(w_in)
    w_out_b = w_out.astype(BF16)
    rw = jnp.pad(router_w, ((0, 0), (0, 0), (0, LANES - N_EXPERTS)))
    rw_hi = rw.astype(BF16)
    rw_lo = (rw - rw_hi.astype(F32)).astype(BF16)
    rb = jnp.pad(router_b, ((0, 0), (0, LANES - N_EXPERTS))).reshape(depth, 1, LANES)
    cw = _compress_weights(cmp_pos, cmp_w1, cmp_w2)
    ret_consts = _retention_consts()
    nsa_consts = _nsa_consts(seq)
    gn_w = ret_gn_w.reshape(depth, 1, RET_WIDTH)
    ln_g4 = ln_g.reshape(depth, 2, 1, D_MODEL)
    ln_b4 = ln_b.reshape(depth, 2, 1, D_MODEL)
    b_gu4 = b_gate_up.reshape(depth, N_EXPERTS, 1, 2 * D_EXPERT)
    b_d4 = b_down.reshape(depth, N_EXPERTS, 1, D_MODEL)

    mod = _modulation(c, ada_w, ada_b)
    cos, sin, cos_t, sin_t = _rope_tables(positions)
    xt = x.reshape(tokens, D_MODEL)
    for l in range(depth):
        (rq, rk, rv, rg, y_conv, qu, qr, kcx, vcx, ks, vs, kw, vw, ng) = _input_projection(
            xt, mod[l], w_in_r, w_in_t, cos, sin, cos_t, sin_t, conv_w, l, seq)
        y_ret = _retention(rq, rk, rv, rg, ret_consts, gn_w, l, batch, seq)
        kc, vc = _compress(kcx, vcx, cw, l, batch, seq)
        y_nsa = _nsa(qu, qr, kc, vc, ks, vs, kw, vw, ng, nsa_consts, batch, seq)
        x1, *h2, top_i, top_w, counts = _out_projection(y_ret, y_conv, y_nsa, xt, mod[l], w_out_b, ln_g4, ln_b4,
                                                        rw_hi, rw_lo, rb, l, seq, alpha)
        pstart, block_e, n_used, n_valid, plan = _block_table(counts, n_rows // MOE_TILE)
        pos_t = _route_positions(top_i, pstart)[:, :TOP_K].T
        rows = _scatter_rows(h2, pos_t, n_rows)
        y = _expert_ffn(rows, block_e, n_used, n_valid, plan, w_gate_up, b_gu4, w_down, b_d4, l)
        y_tok = _gather_rows(y, pos_t.reshape(-1))
        xt = _final_norm(x1, y_tok, top_w, mod[l], ln_g4, ln_b4, l, seq, alpha)
    return xt.reshape(batch, seq, D_MODEL)
```

```python
import functools

import numpy as np
import jax
import jax.numpy as jnp
from jax import lax
from jax.experimental import pallas as pl
from jax.experimental.pallas import tpu as pltpu
from jax.experimental.pallas import tpu_sc as plsc

F32 = jnp.float32
BF16 = jnp.bfloat16

D_MODEL = 1024
HEAD_DIM = 64
RET_WIDTH = 256
RET_HEADS = 4
RET_CHUNK = 128
CONV_WIDTH = 256
CONV_K = 3
NSA_WIDTH = 512
NSA_HEADS = 8
NSA_KV_HEADS = 2
NSA_GROUP = 4
NSA_KV_WIDTH = 128
CMP_LEN = 32
CMP_STRIDE = 16
CMP_HIDDEN = 128
SEL_LEN = 64
SEL_TOP = 8
WINDOW = 512
Q_BLOCK = 128
Q_TILE = 2 * Q_BLOCK
ROPE_THETA = 10000.0
N_EXPERTS = 32
TOP_K = 4
D_EXPERT = 1024
SWIGLU_LIMIT = 7.0
SWIGLU_ALPHA = 1.702
LN_EPS = 1e-5
NEG_INF = -1e30
FORCE_SCORE = 1e9

LANES = 128
VMEM_LIMIT = 56 * 1024 * 1024

OFF_RQ, OFF_RK, OFF_RV, OFF_RG = 0, 256, 512, 768
OFF_CB, OFF_CC, OFF_CH = 1024, 1280, 1536
OFF_NQ = 1792
OFF_KC, OFF_VC = 2304, 2432
OFF_KS, OFF_VS, OFF_KW, OFF_VW = 2560, 2688, 2816, 2944
OFF_NG = 3072
N_GATE = NSA_HEADS * 3
N_COLS = OFF_NG
LOG2E = 1.4426950408889634
WT_NQ, WT_VS, WT_VW, WT_NG = 0, NSA_WIDTH, NSA_WIDTH + NSA_KV_WIDTH, NSA_WIDTH + 2 * NSA_KV_WIDTH
WT_ROWS = WT_NG + LANES

RET_BATCH = 4
ROW_TILE = 1024
INPROJ_TILE = 1024
MOE_TILE = 512
MOE_SUB = 256
FFN_CHUNK = 256
ROUTE_TILE = 1024
SEL_GROUP = 4
V_ROWS = HEAD_DIM + 16
SC_WINDOW = LANES
SC_SPLIT = 2
SPLIT_WIDTH = D_MODEL // (2 * SC_SPLIT)


def _dot(a, b):
    return jnp.dot(a, b, preferred_element_type=F32)


def _dot_nt(a, b):
    return lax.dot_general(a, b, (((1,), (1,)), ((), ())), preferred_element_type=F32)


def _layer_norm(x):
    mu = jnp.mean(x, axis=-1, keepdims=True)
    xc = x - mu
    var = jnp.mean(xc * xc, axis=-1, keepdims=True)
    return xc * lax.rsqrt(var + LN_EPS)


def _pack_pairs(x):
    bits = lambda t: lax.bitcast_convert_type(t.astype(BF16).astype(F32), jnp.uint32)
    return (bits(x[:, SPLIT_WIDTH:]) & jnp.uint32(0xFFFF0000)) | (bits(x[:, :SPLIT_WIDTH]) >> 16)


def _unpack_pairs(p):
    low = lax.bitcast_convert_type(p << 16, F32)
    high = lax.bitcast_convert_type(p & jnp.uint32(0xFFFF0000), F32)
    return low, high


def _params(*sem):
    return pltpu.CompilerParams(dimension_semantics=sem, vmem_limit_bytes=VMEM_LIMIT)


def _mod_kernel(c_ref, w_ref, b_ref, o_ref):
    c = c_ref[...]
    ca = (c * jax.nn.sigmoid(c)).astype(BF16)
    o_ref[0, 0] = _dot(ca, w_ref[0].astype(BF16)) + b_ref[0]


def _modulation(c, ada_w, ada_b):
    depth = ada_w.shape[0]
    batch = c.shape[0]
    out = pl.pallas_call(
        _mod_kernel,
        grid=(depth, 6),
        in_specs=[pl.BlockSpec((batch, D_MODEL), lambda l, j: (0, 0)),
                  pl.BlockSpec((1, D_MODEL, D_MODEL), lambda l, j: (l, 0, j)),
                  pl.BlockSpec((1, 1, D_MODEL), lambda l, j: (l * 6 + j, 0, 0))],
        out_specs=pl.BlockSpec((1, 1, batch, D_MODEL), lambda l, j: (l, j, 0, 0)),
        out_shape=jax.ShapeDtypeStruct((depth, 6, batch, D_MODEL), F32),
        compiler_params=_params("arbitrary", "arbitrary"),
        name="adaln_mod",
    )(c, ada_w, ada_b.reshape(depth * 6, 1, D_MODEL))
    return out.reshape(depth, 6, batch, 1, D_MODEL)


def _rope_table_kernel(pos_ref, inv_ref, cos_ref, sin_ref):
    ang = pos_ref[...] * inv_ref[...]
    cos_ref[...] = jnp.cos(ang)
    sin_ref[...] = jnp.sin(ang)


def _rope_tables(positions):
    half = HEAD_DIM // 2
    per_row = LANES // half
    tokens = positions.size
    pos4 = jnp.repeat(positions.reshape(tokens // per_row, per_row).astype(F32), half, axis=1)
    inv = ROPE_THETA ** (-jnp.arange(half, dtype=F32) / half)
    inv4 = jnp.tile(inv, per_row)[None, :]
    rows = tokens // per_row
    tile = min(rows, 1024)
    cos4, sin4 = pl.pallas_call(
        _rope_table_kernel,
        grid=(rows // tile,),
        in_specs=[pl.BlockSpec((tile, LANES), lambda i: (i, 0)),
                  pl.BlockSpec((1, LANES), lambda i: (0, 0))],
        out_specs=[pl.BlockSpec((tile, LANES), lambda i: (i, 0))] * 2,
        out_shape=[jax.ShapeDtypeStruct((rows, LANES), F32)] * 2,
        compiler_params=_params("arbitrary"),
        name="rope_tables",
    )(pos4, inv4)
    cos32 = cos4.reshape(tokens, half)
    sin32 = sin4.reshape(tokens, half)
    cos = jnp.tile(cos32, (1, per_row))
    sign = jnp.tile(jnp.concatenate([-jnp.ones((half,), F32), jnp.ones((half,), F32)]), LANES // HEAD_DIM)
    sin = jnp.tile(sin32, (1, per_row)) * sign[None, :]
    return cos, sin, cos32.T, sin32.T


def _inproj_kernel(x_ref, mod_ref, w_ref, wt_ref, cos_ref, sin_ref, cost_ref, sint_ref, convw_ref,
                   rq_ref, rk_ref, rv_ref, rg_ref, yc_ref, qu_ref, qr_ref, kc_ref, vc_ref,
                   ks_ref, vs_ref, kw_ref, vw_ref, ng_ref, carry_ref, *, tiles_per_seq):
    i = pl.program_id(0)
    tm = x_ref.shape[0]
    h = (_layer_norm(x_ref[...]) * (1.0 + mod_ref[1, 0]) + mod_ref[0, 0]).astype(BF16)
    cosf = cos_ref[...]
    sinf = sin_ref[...]
    lane = lax.broadcasted_iota(jnp.int32, (tm, LANES), 1)
    first_half = (lane % HEAD_DIM) < (HEAD_DIM // 2)

    def proj(off, width):
        return _dot(h, w_ref[0, :, off:off + width])

    def rope(c):
        cols = []
        for j in range(c.shape[1] // LANES):
            cj = c[:, j * LANES:(j + 1) * LANES]
            swapped = jnp.where(first_half, pltpu.roll(cj, LANES - HEAD_DIM // 2, 1),
                                pltpu.roll(cj, HEAD_DIM // 2, 1))
            cols.append(cj * cosf + swapped * sinf)
        return jnp.concatenate(cols, axis=1) if len(cols) > 1 else cols[0]

    scale = HEAD_DIM ** -0.5

    def proj_t(off, width):
        return _dot_nt(wt_ref[0, off:off + width, :], h)

    def store(ref, value):
        ref[...] = value

    def conv_out(dots):
        cb, cc, ch = dots
        u = cc * ch

        @pl.when(i % tiles_per_seq == 0)
        def _():
            carry_ref[...] = jnp.zeros_like(carry_ref)

        carry = carry_ref[...]
        row = lax.broadcasted_iota(jnp.int32, (tm, CONV_WIDTH), 0)
        prev1 = jnp.where(row == 0, carry[7:8], pltpu.roll(u, 1, 0))
        prev2 = jnp.where(row == 0, carry[6:7], jnp.where(row == 1, carry[7:8], pltpu.roll(u, 2, 0)))
        cw = convw_ref[0]
        yc_ref[...] = (cb * (cw[0:1] * prev2 + cw[1:2] * prev1 + cw[2:3] * u)).astype(BF16)
        carry_ref[...] = u[tm - 8:tm]

    def queries_out(nq_t):
        nq_t = nq_t * (scale * LOG2E)
        qu_ref[...] = nq_t.astype(BF16)
        cos_t = cost_ref[...]
        sin_t = sint_ref[...]
        half = HEAD_DIM // 2
        rotated = []
        for hq in range(NSA_HEADS):
            t1 = nq_t[hq * HEAD_DIM:hq * HEAD_DIM + half]
            t2 = nq_t[hq * HEAD_DIM + half:(hq + 1) * HEAD_DIM]
            rotated += [t1 * cos_t - t2 * sin_t, t2 * cos_t + t1 * sin_t]
        qr_ref[...] = jnp.concatenate(rotated, axis=0).astype(BF16)

    def values_t(v_t):
        ones = jnp.ones((V_ROWS - HEAD_DIM, tm), F32)
        parts = []
        for hd in range(NSA_KV_HEADS):
            parts += [v_t[hd * HEAD_DIM:(hd + 1) * HEAD_DIM], ones]
        return jnp.concatenate(parts, axis=0).astype(BF16)

    stages = [
        (lambda: proj(OFF_RQ, RET_WIDTH), lambda d: store(rq_ref, rope(d).astype(BF16))),
        (lambda: proj(OFF_RK, RET_WIDTH), lambda d: store(rk_ref, (rope(d) * scale).astype(BF16))),
        (lambda: proj(OFF_RV, RET_WIDTH), lambda d: store(rv_ref, d.astype(BF16))),
        (lambda: proj(OFF_RG, RET_WIDTH), lambda d: store(rg_ref, (d * jax.nn.sigmoid(d)).astype(BF16))),
        (lambda: (proj(OFF_CB, CONV_WIDTH), proj(OFF_CC, CONV_WIDTH), proj(OFF_CH, CONV_WIDTH)), conv_out),
        (lambda: proj_t(WT_NQ, NSA_WIDTH), queries_out),
        (lambda: proj(OFF_KC, NSA_KV_WIDTH), lambda d: store(kc_ref, d)),
        (lambda: proj(OFF_VC, NSA_KV_WIDTH), lambda d: store(vc_ref, d)),
        (lambda: proj(OFF_KS, NSA_KV_WIDTH), lambda d: store(ks_ref, rope(d).astype(BF16))),
        (lambda: proj_t(WT_VS, NSA_KV_WIDTH), lambda d: store(vs_ref, values_t(d))),
        (lambda: proj(OFF_KW, NSA_KV_WIDTH), lambda d: store(kw_ref, rope(d).astype(BF16))),
        (lambda: proj_t(WT_VW, NSA_KV_WIDTH), lambda d: store(vw_ref, values_t(d))),
        (lambda: proj_t(WT_NG, LANES), lambda d: store(ng_ref, d)),
    ]
    for matmuls, epilogue in stages:
        epilogue(matmuls())


def _input_projection(x, mod_l, w_l, w_t, cos, sin, cos_t, sin_t, conv_w, layer, seq):
    tokens = x.shape[0]
    tm = INPROJ_TILE
    tiles_per_seq = seq // tm
    row = lambda w: pl.BlockSpec((tm, w), lambda i: (i, 0))
    col = lambda w: pl.BlockSpec((w, tm), lambda i: (0, i))
    outs = [(RET_WIDTH, BF16, False)] * 4 + [(CONV_WIDTH, BF16, False), (NSA_WIDTH, BF16, True),
                                            (NSA_WIDTH, BF16, True), (NSA_KV_WIDTH, F32, False),
                                            (NSA_KV_WIDTH, F32, False), (NSA_KV_WIDTH, BF16, False),
                                            (NSA_KV_HEADS * V_ROWS, BF16, True), (NSA_KV_WIDTH, BF16, False),
                                            (NSA_KV_HEADS * V_ROWS, BF16, True), (LANES, F32, True)]
    return pl.pallas_call(
        functools.partial(_inproj_kernel, tiles_per_seq=tiles_per_seq),
        grid=(tokens // tm,),
        in_specs=[row(D_MODEL),
                  pl.BlockSpec((6, 1, 1, D_MODEL), lambda i: (0, i // tiles_per_seq, 0, 0)),
                  pl.BlockSpec((1, D_MODEL, N_COLS), lambda i: (layer, 0, 0)),
                  pl.BlockSpec((1, WT_ROWS, D_MODEL), lambda i: (layer, 0, 0)),
                  row(LANES), row(LANES), col(HEAD_DIM // 2), col(HEAD_DIM // 2),
                  pl.BlockSpec((1, CONV_K, CONV_WIDTH), lambda i: (layer, 0, 0))],
        out_specs=[col(w) if t else row(w) for w, _, t in outs],
        out_shape=[jax.ShapeDtypeStruct((w, tokens) if t else (tokens, w), dt) for w, dt, t in outs],
        scratch_shapes=[pltpu.VMEM((8, CONV_WIDTH), F32)],
        compiler_params=_params("arbitrary"),
        name="ln_inproj",
    )(x, mod_l, w_l, w_t, cos, sin, cos_t, sin_t, conv_w)


def _retention_kernel(q_ref, k_ref, v_ref, g_ref, intra_ref, qdec_ref, kdec_ref, cdec_ref, gn_ref,
                      o_ref, state_ref):
    @pl.when(pl.program_id(1) == 0)
    def _():
        state_ref[...] = jnp.zeros_like(state_ref)

    n_pairs = RET_WIDTH // LANES
    lane = lax.broadcasted_iota(jnp.int32, (RET_CHUNK, LANES), 1)
    low = lane < HEAD_DIM
    blk_r = lax.broadcasted_iota(jnp.int32, (LANES, LANES), 0) < HEAD_DIM
    blk_c = lax.broadcasted_iota(jnp.int32, (LANES, LANES), 1) < HEAD_DIM
    same_head = blk_r == blk_c
    def half_mean(t):
        lo_sum = jnp.sum(jnp.where(low, t, 0.0), axis=-1, keepdims=True)
        hi_sum = jnp.sum(jnp.where(low, 0.0, t), axis=-1, keepdims=True)
        return jnp.where(low, lo_sum, hi_sum) * (1.0 / HEAD_DIM)

    combos = [(b, p) for b in range(q_ref.shape[0]) for p in range(n_pairs)]
    cols = {p: slice(p * LANES, (p + 1) * LANES) for p in range(n_pairs)}
    keeps = (low, jnp.logical_not(low))
    states = {c: state_ref[c[0], c[1]] for c in combos}
    for ch in range(q_ref.shape[1] // RET_CHUNK):
        rows = slice(ch * RET_CHUNK, (ch + 1) * RET_CHUNK)
        qp = {(b, p): q_ref[b, rows, cols[p]] for b, p in combos}
        kp = {(b, p): k_ref[b, rows, cols[p]] for b, p in combos}
        vp = {(b, p): v_ref[b, rows, cols[p]] for b, p in combos}
        scores = {(c, hh): _dot_nt(jnp.where(keeps[hh], qp[c], jnp.zeros_like(qp[c])), kp[c])
                  for c in combos for hh in range(2)}
        scores = {(c, hh): (s * intra_ref[2 * c[1] + hh]).astype(BF16) for (c, hh), s in scores.items()}
        intra = {key: _dot(s, vp[key[0]]) for key, s in scores.items()}
        inter = {c: _dot((qp[c].astype(F32) * qdec_ref[:, cols[c[1]]]).astype(BF16), states[c].astype(BF16))
                 for c in combos}
        upd = {c: _dot((kp[c].astype(F32) * kdec_ref[:, cols[c[1]]]).T.astype(BF16), vp[c]) for c in combos}
        states = {c: states[c] * cdec_ref[c[1]] + jnp.where(same_head, upd[c], 0.0) for c in combos}
        out = {c: jnp.where(low, intra[c, 0], intra[c, 1]) + inter[c] for c in combos}
        centred = {c: out[c] - half_mean(out[c]) for c in combos}
        normed = {c: centred[c] * lax.rsqrt(half_mean(centred[c] * centred[c]) + LN_EPS) for c in combos}
        for b in range(q_ref.shape[0]):
            full = jnp.concatenate([normed[b, p] for p in range(n_pairs)], axis=1)
            o_ref[b, rows, :] = (full * gn_ref[0] * g_ref[b, rows, :].astype(F32)).astype(BF16)
    for (b, p), state in states.items():
        state_ref[b, p] = state


def _retention_consts():
    heads = jnp.arange(RET_HEADS, dtype=F32)
    log_gamma = jnp.log(1.0 - jnp.power(2.0, -5.0 - heads))
    i = jnp.arange(RET_CHUNK, dtype=F32)
    diff = i[:, None] - i[None, :]
    intra = jnp.where(diff >= 0, jnp.exp(diff * log_gamma[:, None, None]), 0.0)
    q_dec = jnp.exp((i + 1.0) * log_gamma[:, None])
    k_dec = jnp.exp((RET_CHUNK - 1.0 - i) * log_gamma[:, None])
    c_dec = jnp.exp(RET_CHUNK * log_gamma)
    expand = lambda t: jnp.repeat(t.T, HEAD_DIM, axis=1)
    c_rows = jnp.repeat(c_dec, HEAD_DIM).reshape(RET_WIDTH // LANES, LANES, 1)
    c_blk = jnp.broadcast_to(c_rows, (RET_WIDTH // LANES, LANES, LANES))
    return intra, expand(q_dec), expand(k_dec), c_blk


def _retention(rq, rk, rv, rg, consts, gn_w, layer, batch, seq):
    intra, q_dec, k_dec, c_blk = consts
    rows = 2 * RET_CHUNK
    steps = seq // rows
    group = RET_BATCH if batch % RET_BATCH == 0 else 1
    blk = pl.BlockSpec((group, rows, RET_WIDTH), lambda b, s: (b, s, 0))
    full = lambda a: pl.BlockSpec(a.shape, lambda b, s: (0,) * a.ndim)
    per_seq = lambda t: t.reshape(batch, seq, RET_WIDTH)
    out = pl.pallas_call(
        _retention_kernel,
        grid=(batch // group, steps),
        in_specs=[blk, blk, blk, blk, full(intra), full(q_dec), full(k_dec), full(c_blk),
                  pl.BlockSpec((1, 1, RET_WIDTH), lambda b, s: (layer, 0, 0))],
        out_specs=blk,
        out_shape=jax.ShapeDtypeStruct((batch, seq, RET_WIDTH), BF16),
        scratch_shapes=[pltpu.VMEM((group, RET_WIDTH // LANES, LANES, LANES), F32)],
        compiler_params=_params("arbitrary", "arbitrary"),
        name="retention",
    )(per_seq(rq), per_seq(rk), per_seq(rv), per_seq(rg), intra, q_dec, k_dec, c_blk, gn_w)
    return out.reshape(batch * seq, RET_WIDTH)


def _compress_kernel(xk_ref, xv_ref, posa_ref, posb_ref, w1a_ref, w1b_ref, w2_ref, kc_ref, vc_ref):
    n_grp = xk_ref.shape[0] // CMP_STRIDE
    for kv, (x_ref, o_ref) in enumerate(((xk_ref, kc_ref), (xv_ref, vc_ref))):
        ya = jnp.zeros((n_grp, NSA_KV_HEADS * CMP_HIDDEN), F32)
        yb = jnp.zeros((n_grp, NSA_KV_HEADS * CMP_HIDDEN), F32)
        for r in range(CMP_STRIDE):
            x = x_ref[pl.ds(r, n_grp, stride=CMP_STRIDE), :]
            cols = slice(r * NSA_KV_WIDTH, (r + 1) * NSA_KV_WIDTH)
            ya = ya + _dot((x + posa_ref[0, kv, :, cols]).astype(BF16), w1a_ref[0, kv, cols, :])
            yb = yb + _dot((x + posb_ref[0, kv, :, cols]).astype(BF16), w1b_ref[0, kv, cols, :])
        hidden = ya + pltpu.roll(yb, n_grp - 1, 0)
        act = jax.nn.gelu(hidden)
        out = _dot(act.astype(BF16), w2_ref[0, kv])
        o_ref[0] = (out if kv == 0 else out.T).astype(BF16)


def _compress_weights(cmp_pos, cmp_w1, cmp_w2):
    depth = cmp_w1.shape[0]
    eye = jnp.eye(NSA_KV_HEADS, dtype=F32)
    a = cmp_w1.reshape(depth, 2, 2, CMP_STRIDE, HEAD_DIM, CMP_HIDDEN)
    w1 = jnp.einsum('lkardj,hg->lkarhdgj', a, eye)
    w1 = w1.reshape(depth, 2, 2, CMP_STRIDE * NSA_KV_WIDTH, NSA_KV_HEADS * CMP_HIDDEN).astype(BF16)
    pos = cmp_pos.reshape(depth, 2, 2, CMP_STRIDE, 1, HEAD_DIM)
    pos = jnp.broadcast_to(pos, (depth, 2, 2, CMP_STRIDE, NSA_KV_HEADS, HEAD_DIM))
    pos = pos.reshape(depth, 2, 2, 1, CMP_STRIDE * NSA_KV_WIDTH)
    w2 = jnp.einsum('lkje,hg->lkhjge', cmp_w2, eye)
    w2 = w2.reshape(depth, 2, NSA_KV_HEADS * CMP_HIDDEN, NSA_KV_WIDTH).astype(BF16)
    return w1[:, :, 0], w1[:, :, 1], pos[:, :, 0], pos[:, :, 1], w2


def _compress(kcx, vcx, cw, layer, batch, seq):
    w1a, w1b, posa, posb, w2 = cw
    n_grp = seq // CMP_STRIDE
    xblk = pl.BlockSpec((seq, NSA_KV_WIDTH), lambda b: (b, 0))
    lay = lambda a: pl.BlockSpec((1,) + a.shape[1:], lambda b: (layer,) + (0,) * (a.ndim - 1))
    oblk = pl.BlockSpec((1, n_grp, NSA_KV_WIDTH), lambda b: (b, 0, 0))
    return pl.pallas_call(
        _compress_kernel,
        grid=(batch,),
        in_specs=[xblk, xblk, lay(posa), lay(posb), lay(w1a), lay(w1b), lay(w2)],
        out_specs=[oblk, oblk],
        out_shape=[jax.ShapeDtypeStruct((batch, n_grp, NSA_KV_WIDTH), BF16)] * 2,
        compiler_params=_params("arbitrary"),
        name="nsa_compress",
    )(kcx, vcx, posa, posb, w1a, w1b, w2)


def _nsa_kernel(qu_ref, qr_ref, kc_ref, vc_ref, ks_ref, vs_ref, kw_ref, vw_ref, ng_ref, ov_ref, wbias_ref, ind_ref,
                o_ref):
    qb = pl.program_id(1)
    q0 = qb * Q_TILE
    qb_last = (qb + 1) * (Q_TILE // Q_BLOCK) - 1
    n_sel = ov_ref.shape[0]
    cols4 = NSA_GROUP * Q_TILE
    n_past = WINDOW // Q_BLOCK
    gsig = jax.nn.sigmoid(ng_ref[...])

    def padded_q(ref, h):
        zeros = jnp.zeros((HEAD_DIM, Q_TILE), BF16)
        parts = []
        for g in range(NSA_GROUP):
            hq = h * NSA_GROUP + g
            x = ref[hq * HEAD_DIM:(hq + 1) * HEAD_DIM, :]
            parts.append(jnp.concatenate([x, zeros] if h == 0 else [zeros, x], axis=0))
        return jnp.concatenate(parts, axis=1)

    def tile4(t):
        return jnp.concatenate([t] * NSA_GROUP, axis=1)

    def attend(state, keys, values_ref, k0, bias, q_t):
        m, acc = state
        s = _dot(keys, q_t)
        if bias is not None:
            s = s + bias
        m_new = jnp.maximum(m, jnp.max(s, axis=0, keepdims=True))
        e = jnp.exp2(s - m_new).astype(BF16)
        pv = [_dot(values_ref[h * V_ROWS:(h + 1) * V_ROWS, pl.ds(k0, keys.shape[0])],
                   e[:, h * cols4:(h + 1) * cols4]) for h in range(NSA_KV_HEADS)]
        return m_new, jnp.exp2(m - m_new) * acc + jnp.concatenate(pv, axis=1)

    def normalised(state):
        acc = state[1]
        return acc[0:HEAD_DIM] * (1.0 / acc[HEAD_DIM:HEAD_DIM + 1])

    cols_all = NSA_KV_HEADS * cols4
    init = (jnp.full((1, cols_all), NEG_INF, F32), jnp.zeros((V_ROWS, cols_all), F32))
    q_rot = [padded_q(qr_ref, h) for h in range(NSA_KV_HEADS)]

    def position_bias(kb, windowed):
        first = qb * (Q_TILE // Q_BLOCK)
        return jnp.concatenate(
            [wbias_ref[jnp.clip(kb - (first + sub) + n_past + 1, 0 if windowed else 2, n_past + 2)]
             for sub in range(Q_TILE // Q_BLOCK)] * (NSA_GROUP * NSA_KV_HEADS), axis=1)

    n_cols = NSA_KV_HEADS * NSA_GROUP
    qus_all = jnp.concatenate([padded_q(qu_ref, h) for h in range(NSA_KV_HEADS)], axis=1)
    s = _dot(kc_ref[0], qus_all)
    cid = lax.broadcasted_iota(jnp.int32, (LANES, Q_TILE), 0)
    tq = q0 + lax.broadcasted_iota(jnp.int32, (LANES, Q_TILE), 1)
    cvalid = jnp.concatenate([jnp.where(cid * CMP_STRIDE + (CMP_LEN - 1) <= tq, 1.0, 0.0)] * n_cols, axis=1)
    sm = jnp.where(cvalid > 0.5, s, NEG_INF)
    e = jnp.exp2(sm - jnp.max(sm, axis=0, keepdims=True)) * cvalid
    l = jnp.sum(e, axis=0, keepdims=True)
    p = e * (1.0 / jnp.where(l > 0.0, l, 1.0))
    p_bf = p.astype(BF16)
    o_cmp = [_dot(vc_ref[0, h * HEAD_DIM:(h + 1) * HEAD_DIM, :], p_bf[:, h * cols4:(h + 1) * cols4])
             for h in range(NSA_KV_HEADS)]

    psum = jnp.concatenate(
        [functools.reduce(lambda a, b: a + b, [p[:, (h * NSA_GROUP + g) * Q_TILE:(h * NSA_GROUP + g + 1) * Q_TILE]
                                               for g in range(NSA_GROUP)]) for h in range(NSA_KV_HEADS)], axis=1)
    p_hi = psum.astype(BF16)
    rem = psum - p_hi.astype(F32)
    p_mid = rem.astype(BF16)
    p_lo = (rem - p_mid.astype(F32)).astype(BF16)
    ov = ov_ref[...]
    imp = _dot(ov, p_hi) + _dot(ov, p_mid) + _dot(ov, p_lo)
    sel_cols = NSA_KV_HEADS * Q_TILE
    jid = lax.broadcasted_iota(jnp.int32, (n_sel, sel_cols), 0)
    tid = q0 + lax.broadcasted_iota(jnp.int32, (n_sel, sel_cols), 1) % Q_TILE
    forced = (jid == 0) | (jid == jnp.right_shift(tid, 6))
    imp = jnp.where(forced, FORCE_SCORE, jnp.where(jid * SEL_LEN <= tid, imp, -FORCE_SCORE))
    rank = jnp.zeros((n_sel, sel_cols), F32)
    for i in range(n_sel):
        ri = imp[i:i + 1, :]
        beats = (ri > imp) | ((ri == imp) & (jid > i))
        rank = rank + jnp.where(beats, 1.0, 0.0)
    sel_bias = jnp.where(rank < float(SEL_TOP), 0.0, NEG_INF).astype(BF16)
    pad = jnp.zeros((LANES - n_sel, cols4), BF16)
    q_sel = [jnp.concatenate([q_rot[h], tile4(sel_bias[:, h * Q_TILE:(h + 1) * Q_TILE]), pad], axis=0)
             for h in range(NSA_KV_HEADS)]

    q_sel_all = jnp.concatenate(q_sel, axis=1)
    q_rot_all = jnp.concatenate(q_rot, axis=1)

    def sel_step(gi, state, last):
        k0 = pl.multiple_of(gi * (SEL_GROUP * Q_BLOCK), SEL_GROUP * Q_BLOCK)
        keys = jnp.concatenate([ks_ref[pl.ds(k0, SEL_GROUP * Q_BLOCK), :], ind_ref[gi]], axis=1)
        bias = None
        if last:
            bias = jnp.concatenate([position_bias(gi * SEL_GROUP + u, False) for u in range(SEL_GROUP)], axis=0)
        return attend(state, keys, vs_ref, k0, bias, q_sel_all)

    last_group = qb_last // SEL_GROUP
    sel_state = lax.fori_loop(0, last_group, functools.partial(sel_step, last=False), init)
    o_sel = normalised(sel_step(last_group, sel_state, True))

    ws = pl.multiple_of(jnp.maximum(q0 - WINDOW, 0), Q_BLOCK)
    n_win = n_past + Q_TILE // Q_BLOCK
    win_bias = jnp.concatenate([position_bias(ws // Q_BLOCK + u, True) for u in range(n_win)], axis=0)
    win_keys = kw_ref[pl.ds(ws, n_win * Q_BLOCK), :]
    o_win = normalised(attend(init, win_keys, vw_ref, ws, win_bias, q_rot_all))
    heads = []
    for h in range(NSA_KV_HEADS):
        o_c = o_cmp[h]
        for g in range(NSA_GROUP):
            hq = h * NSA_GROUP + g
            c = slice(g * Q_TILE, (g + 1) * Q_TILE)
            ca = slice(hq * Q_TILE, (hq + 1) * Q_TILE)
            heads.append(gsig[3 * hq:3 * hq + 1, :] * o_c[:, c] + gsig[3 * hq + 1:3 * hq + 2, :] * o_sel[:, ca]
                         + gsig[3 * hq + 2:3 * hq + 3, :] * o_win[:, ca])
    cols_out = [jnp.concatenate(heads[2 * m:2 * m + 2], axis=0).T for m in range(NSA_HEADS // 2)]
    o_ref[...] = jnp.concatenate(cols_out, axis=1).astype(BF16)


def _nsa_consts(seq):
    n_cmp_pad = seq // CMP_STRIDE
    n_sel = seq // SEL_LEN
    c = np.arange(n_cmp_pad)
    j = np.arange(n_sel)
    n_cmp = (seq - CMP_LEN) // CMP_STRIDE + 1
    ov = ((c[None, :] * CMP_STRIDE < j[:, None] * SEL_LEN + SEL_LEN) &
          (j[:, None] * SEL_LEN <= c[None, :] * CMP_STRIDE + CMP_LEN - 1) & (c[None, :] < n_cmp))
    kl = np.arange(Q_BLOCK)[:, None]
    tl = np.arange(Q_BLOCK)[None, :]
    n_past = WINDOW // Q_BLOCK
    none = np.zeros((Q_BLOCK, Q_BLOCK), bool)
    valid = [none, kl > tl] + [np.ones((Q_BLOCK, Q_BLOCK), bool)] * (n_past - 1) + [kl <= tl, none]
    wbias = np.where(np.stack(valid), 0.0, NEG_INF).astype(np.float32)
    key_block = np.arange(seq) // SEL_LEN
    onehot = (key_block[:, None] == np.arange(LANES)[None, :]).reshape(-1, SEL_GROUP * Q_BLOCK, LANES)
    return jnp.asarray(ov, BF16), jnp.asarray(wbias), jnp.asarray(onehot, BF16)


def _nsa(qu, qr, kc, vc, ks, vs, kw, vw, ng, consts, batch, seq):
    ov, wbias, onehot = consts
    n_qb = seq // Q_TILE
    tokens = batch * seq
    qblk = pl.BlockSpec((NSA_WIDTH, Q_TILE), lambda b, q: (0, b * n_qb + q))
    cblk = pl.BlockSpec((1, seq // CMP_STRIDE, NSA_KV_WIDTH), lambda b, q: (b, 0, 0))
    kblk = pl.BlockSpec((seq, NSA_KV_WIDTH), lambda b, q: (b, 0))
    vblk = pl.BlockSpec((NSA_KV_HEADS * V_ROWS, seq), lambda b, q: (0, b))
    full = lambda a: pl.BlockSpec(a.shape, lambda b, q: (0,) * a.ndim)
    return pl.pallas_call(
        _nsa_kernel,
        grid=(batch, n_qb),
        in_specs=[qblk, qblk, cblk, cblk, kblk, vblk, kblk, vblk,
                  pl.BlockSpec((LANES, Q_TILE), lambda b, q: (0, b * n_qb + q)), full(ov), full(wbias),
                  full(onehot)],
        out_specs=pl.BlockSpec((Q_TILE, NSA_WIDTH), lambda b, q: (b * n_qb + q, 0)),
        out_shape=jax.ShapeDtypeStruct((tokens, NSA_WIDTH), BF16),
        compiler_params=_params("arbitrary", "arbitrary"),
        name="nsa_attention",
    )(qu, qr, kc, vc, ks, vs, kw, vw, ng, ov, wbias, onehot)


def _outproj_kernel(yr_ref, yc_ref, yn_ref, x_ref, mod_ref, w_ref, lng_ref, lnb_ref, rwh_ref, rwl_ref, rb_ref,
                    x1_ref, h2a_ref, h2b_ref, ti_ref, tw_ref, cnt_ref, *, alpha):
    tm = x_ref.shape[0]
    mix = (_dot(yr_ref[...], w_ref[0, 0:RET_WIDTH, :])
           + _dot(yc_ref[...], w_ref[0, RET_WIDTH:RET_WIDTH + CONV_WIDTH, :])
           + _dot(yn_ref[...], w_ref[0, RET_WIDTH + CONV_WIDTH:, :]))
    x1 = _layer_norm(alpha * x_ref[...] + (1.0 + mod_ref[2, 0]) * mix) * lng_ref[0, 0] + lnb_ref[0, 0]
    x1_ref[...] = x1
    h2 = _layer_norm(x1) * (1.0 + mod_ref[4, 0]) + mod_ref[3, 0]
    for j, part_ref in enumerate((h2a_ref, h2b_ref)):
        part_ref[...] = _pack_pairs(h2[:, j * 2 * SPLIT_WIDTH:(j + 1) * 2 * SPLIT_WIDTH])
    h_hi = h2.astype(BF16)
    h_lo = (h2 - h_hi.astype(F32)).astype(BF16)
    logits = _dot(h_hi, rwh_ref[0]) + _dot(h_lo, rwh_ref[0]) + _dot(h_hi, rwl_ref[0]) + rb_ref[0]

    lane = lax.broadcasted_iota(jnp.int32, (tm, LANES), 1)
    lanef = lane.astype(F32)
    rest = jnp.where(lane < N_EXPERTS, logits, -jnp.inf)
    vals, idxs = [], []
    for _ in range(TOP_K):
        top = jnp.max(rest, axis=-1, keepdims=True)
        idx = jnp.min(jnp.where(rest == top, lanef, float(LANES)), axis=-1, keepdims=True)
        vals.append(top)
        idxs.append(idx)
        rest = jnp.where(lanef == idx, -jnp.inf, rest)
    exps = [jnp.exp(v - vals[0]) for v in vals]
    inv = 1.0 / functools.reduce(lambda a, b: a + b, exps)
    top_w = jnp.zeros((tm, LANES), F32)
    top_i = jnp.zeros((tm, LANES), F32)
    member = jnp.zeros((tm, LANES), F32)
    for k in range(TOP_K):
        top_w = jnp.where(lane == k, exps[k] * inv, top_w)
        top_i = jnp.where(lane == k, idxs[k], top_i)
        member = member + jnp.where(lanef == idxs[k], 1.0, 0.0)
    tw_ref[...] = top_w
    ti_ref[...] = top_i.astype(jnp.int32)

    @pl.when(pl.program_id(0) == 0)
    def _():
        cnt_ref[...] = jnp.zeros_like(cnt_ref)

    cnt_ref[...] += jnp.broadcast_to(jnp.sum(member, axis=0, keepdims=True), cnt_ref.shape)


def _out_projection(y_ret, y_conv, y_nsa, x, mod_l, w_out, ln_g, ln_b, rw_hi, rw_lo, rb, layer, seq, alpha):
    tokens = x.shape[0]
    tm = ROW_TILE
    tiles_per_seq = seq // tm
    row = lambda w: pl.BlockSpec((tm, w), lambda i: (i, 0))
    lay3 = lambda a: pl.BlockSpec((1,) + a.shape[1:], lambda i: (layer,) + (0,) * (a.ndim - 1))
    return pl.pallas_call(
        functools.partial(_outproj_kernel, alpha=alpha),
        grid=(tokens // tm,),
        in_specs=[row(RET_WIDTH), row(CONV_WIDTH), row(NSA_WIDTH), row(D_MODEL),
                  pl.BlockSpec((6, 1, 1, D_MODEL), lambda i: (0, i // tiles_per_seq, 0, 0)),
                  lay3(w_out),
                  pl.BlockSpec((1, 1, 1, D_MODEL), lambda i: (layer, 0, 0, 0)),
                  pl.BlockSpec((1, 1, 1, D_MODEL), lambda i: (layer, 0, 0, 0)),
                  lay3(rw_hi), lay3(rw_lo), lay3(rb)],
        out_specs=[row(D_MODEL)] + [row(SPLIT_WIDTH)] * SC_SPLIT + [row(LANES), row(LANES),
                   pl.BlockSpec((8, LANES), lambda i: (0, 0))],
        out_shape=[jax.ShapeDtypeStruct((tokens, D_MODEL), F32)]
                  + [jax.ShapeDtypeStruct((tokens, SPLIT_WIDTH), jnp.uint32)] * SC_SPLIT
                  + [jax.ShapeDtypeStruct((tokens, LANES), jnp.int32), jax.ShapeDtypeStruct((tokens, LANES), F32),
                   jax.ShapeDtypeStruct((8, LANES), F32)],
        compiler_params=_params("arbitrary"),
        name="outproj_norm_router",
    )(y_ret, y_conv, y_nsa, x, mod_l, w_out, ln_g, ln_b, rw_hi, rw_lo, rb)


def _route_kernel(ti_ref, pstart_ref, tri_ref, pos_ref, carry_ref):
    @pl.when(pl.program_id(0) == 0)
    def _():
        carry_ref[...] = jnp.zeros_like(carry_ref)

    tm = ti_ref.shape[0]
    lane = lax.broadcasted_iota(jnp.int32, (tm, LANES), 1)
    top_i = ti_ref[...]
    onehots = [lane == top_i[:, k:k + 1] for k in range(TOP_K)]
    member = functools.reduce(lambda a, b: a + b, [jnp.where(o, 1.0, 0.0) for o in onehots])
    base = pstart_ref[...] + carry_ref[0:1] + _dot(tri_ref[...], member.astype(BF16))
    pos = jnp.zeros((tm, LANES), F32)
    for k in range(TOP_K):
        pos_k = jnp.sum(jnp.where(onehots[k], base, 0.0), axis=-1, keepdims=True)
        pos = jnp.where(lane == k, pos_k, pos)
    pos_ref[...] = pos.astype(jnp.int32)
    carry_ref[...] += jnp.broadcast_to(jnp.sum(member, axis=0, keepdims=True), carry_ref.shape)


def _route_positions(top_i, pstart):
    tokens = top_i.shape[0]
    tm = min(ROUTE_TILE, tokens)
    r = np.arange(tm)
    tri = jnp.asarray(r[None, :] < r[:, None], BF16)
    return pl.pallas_call(
        _route_kernel,
        grid=(tokens // tm,),
        in_specs=[pl.BlockSpec((tm, LANES), lambda i: (i, 0)),
                  pl.BlockSpec((1, LANES), lambda i: (0, 0)),
                  pl.BlockSpec((tm, tm), lambda i: (0, 0))],
        out_specs=pl.BlockSpec((tm, LANES), lambda i: (i, 0)),
        out_shape=jax.ShapeDtypeStruct((tokens, LANES), jnp.int32),
        scratch_shapes=[pltpu.VMEM((8, LANES), F32)],
        compiler_params=_params("arbitrary"),
        name="route_positions",
    )(top_i, pstart, tri)


def _block_table(counts, n_blocks):
    tm = MOE_TILE
    cnt = counts[0, :N_EXPERTS].astype(jnp.int32)
    nblk = (cnt + tm - 1) // tm
    experts = jnp.arange(N_EXPERTS, dtype=jnp.int32)
    before = experts[:, None] < experts[None, :]
    bstart = jnp.sum(jnp.where(before, nblk[:, None], 0), axis=0)
    bend = bstart + nblk
    n_used = jnp.sum(nblk)
    blocks = jnp.arange(n_blocks, dtype=jnp.int32)
    block_e = jnp.minimum(jnp.sum((bend[None, :] <= blocks[:, None]).astype(jnp.int32), axis=1), N_EXPERTS - 1)
    onehot = block_e[:, None] == experts[None, :]
    lookup = lambda per_expert: jnp.sum(jnp.where(onehot, per_expert[None, :], 0), axis=1)
    live = blocks < n_used
    n_valid = jnp.where(live, jnp.clip(lookup(cnt) - (blocks - lookup(bstart)) * tm, 0, tm), 0)
    pstart = jnp.pad((bstart * tm).astype(F32), (0, LANES - N_EXPERTS))[None, :]
    used = (nblk > 0).astype(jnp.int32)
    first = live & (blocks == lookup(bstart))
    slot = lookup(jnp.sum(jnp.where(before, used[:, None], 0), axis=0)) % 2
    later = jnp.where((used[None, :] > 0) & before, experts[None, :], N_EXPERTS)
    next_expert = jnp.min(later, axis=1)
    next_expert = lookup(jnp.where(next_expert < N_EXPERTS, next_expert, -1))
    plan = jnp.stack([first.astype(jnp.int32), slot, next_expert]).astype(jnp.int32)
    return pstart, block_e, n_used.reshape(1), n_valid.astype(jnp.int32), plan


def _sc_mesh():
    return plsc.VectorSubcoreMesh(core_axis_name="core", subcore_axis_name="subcore")


def _scatter_rows(xs, idx_t, n_rows):
    tokens, width = xs[0].shape
    n_idx = idx_t.shape[0]
    n_x = len(xs)
    win = SC_WINDOW

    @functools.partial(pl.kernel, out_type=[jax.ShapeDtypeStruct((n_rows, width), xs[0].dtype)] * n_x,
                       mesh=_sc_mesh(), scratch_types=[pltpu.SemaphoreType.DMA], name="dispatch_scatter")
    def scatter(*refs):
        x_hbms, i_hbm, o_hbms, sem = refs[:n_x], refs[n_x], refs[n_x + 1:-1], refs[-1]
        idx_specs = [pl.BlockSpec((1, win), functools.partial(lambda i, k: (k, i), k=k)) for k in range(n_idx)]
        for x_hbm, o_hbm in zip(x_hbms, o_hbms):
            def body(x_vmem, *i_vmems, o_hbm=o_hbm):
                copies = [pltpu.async_copy(x_vmem, o_hbm.at[i_vmem.at[0]], sem) for i_vmem in i_vmems]
                for copy in copies:
                    copy.wait()

            pltpu.emit_pipeline(
                body,
                grid=(tokens // win,),
                in_specs=[pl.BlockSpec((win, width), lambda i: (i, 0))] + idx_specs,
                out_specs=[],
                core_axis_name=("core", "subcore"),
                dimension_semantics=(pltpu.PARALLEL,),
            )(x_hbm, *([i_hbm] * n_idx))

    return scatter(*xs, idx_t)


def _gather_rows(xs, idx):
    width = xs[0].shape[1]
    n = idx.shape[0]
    n_x = len(xs)
    win = SC_WINDOW

    @functools.partial(pl.kernel, out_type=[jax.ShapeDtypeStruct((n, width), xs[0].dtype)] * n_x,
                       mesh=_sc_mesh(), scratch_types=[], name="combine_gather")
    def gather(*refs):
        x_hbms, i_hbm, o_hbms = refs[:n_x], refs[n_x], refs[n_x + 1:]
        for x_hbm, o_hbm in zip(x_hbms, o_hbms):
            def body(i_vmem, o_vmem, x_hbm=x_hbm):
                pltpu.sync_copy(x_hbm.at[i_vmem.at[0]], o_vmem)

            pltpu.emit_pipeline(
                body,
                grid=(n // win,),
                in_specs=[pl.BlockSpec((1, win), lambda i: (0, i))],
                out_specs=[pl.BlockSpec((win, width), lambda i: (i, 0))],
                core_axis_name=("core", "subcore"),
                dimension_semantics=(pltpu.PARALLEL,),
            )(i_hbm, o_hbm)

    return gather(*xs, idx.reshape(1, n))


def _moe_kernel(be_ref, nb_ref, nv_ref, plan_ref, ra_ref, rb_ref, wgu_hbm, bgu_ref, wd_hbm, bd_ref,
                oa_ref, ob_ref, wgu_f32, wd_f32, wgu_bf, wd_bf, sem, *, layer):
    row_refs = (ra_ref, rb_ref)
    out_refs = (oa_ref, ob_ref)
    i = pl.program_id(0)
    valid = i < nb_ref[0]

    def weight_copies(expert, slot):
        return (pltpu.make_async_copy(wgu_hbm.at[layer, expert], wgu_f32.at[slot], sem.at[0, slot]),
                pltpu.make_async_copy(wd_hbm.at[layer, expert], wd_f32.at[slot], sem.at[1, slot]))

    @pl.when(i == 0)
    def _():
        for copy in weight_copies(be_ref[0], 0):
            copy.start()

    @pl.when(valid & (plan_ref[0, i] == 1))
    def _():
        slot = plan_ref[1, i]
        for copy in weight_copies(be_ref[i], slot):
            copy.wait()
        upcoming = plan_ref[2, i]

        @pl.when(upcoming >= 0)
        def _():
            for copy in weight_copies(upcoming, 1 - slot):
                copy.start()

        wgu_bf[...] = wgu_f32[slot].astype(BF16)
        wd_bf[...] = wd_f32[slot].astype(BF16)

    def ffn(rs):
        live = lax.broadcasted_iota(jnp.int32, (rs.stop - rs.start, SPLIT_WIDTH), 0) < nv_ref[i] - rs.start
        rows = jnp.concatenate(
            [half.astype(BF16) for r_ref in row_refs
             for half in _unpack_pairs(jnp.where(live, r_ref[rs, :], jnp.uint32(0)))], axis=1)
        def gate_up(c):
            cg = slice(c * FFN_CHUNK, (c + 1) * FFN_CHUNK)
            cu = slice(D_EXPERT + c * FFN_CHUNK, D_EXPERT + (c + 1) * FFN_CHUNK)
            return (_dot(rows, wgu_bf[:, cg]) + bgu_ref[0, 0, :, cg], _dot(rows, wgu_bf[:, cu]) + bgu_ref[0, 0, :, cu])

        def activation(gu):
            g = jnp.minimum(gu[0], SWIGLU_LIMIT)
            u = jnp.clip(gu[1], -SWIGLU_LIMIT, SWIGLU_LIMIT)
            return ((u + 1.0) * (g * jax.nn.sigmoid(SWIGLU_ALPHA * g))).astype(BF16)

        acts = []
        pending = gate_up(0)
        for c in range(1, D_EXPERT // FFN_CHUNK):
            upcoming = gate_up(c)
            acts.append(activation(pending))
            pending = upcoming
        acts.append(activation(pending))
        y = _dot(jnp.concatenate(acts, axis=1), wd_bf[...]) + bd_ref[0, 0]
        for j, o_ref in enumerate(out_refs):
            o_ref[rs, :] = _pack_pairs(y[:, j * 2 * SPLIT_WIDTH:(j + 1) * 2 * SPLIT_WIDTH])

    tail = slice(MOE_SUB, MOE_TILE)
    pl.when(valid & (nv_ref[i] > MOE_SUB))(functools.partial(ffn, slice(0, MOE_TILE)))
    pl.when(valid & (nv_ref[i] <= MOE_SUB))(functools.partial(ffn, slice(0, MOE_SUB)))

    @pl.when(jnp.logical_not(valid))
    def _():
        for o_ref in out_refs:
            o_ref[0:MOE_SUB, :] = jnp.zeros((MOE_SUB, SPLIT_WIDTH), jnp.uint32)

    @pl.when(jnp.logical_not(valid & (nv_ref[i] > MOE_SUB)))
    def _():
        for o_ref in out_refs:
            o_ref[tail, :] = jnp.zeros((MOE_TILE - MOE_SUB, SPLIT_WIDTH), jnp.uint32)


def _expert_ffn(rows, block_e, n_used, n_valid, plan, w_gu, b_gu, w_down, b_down, layer):
    n_rows = rows[0].shape[0]
    tm = MOE_TILE
    part = pl.BlockSpec((tm, SPLIT_WIDTH), lambda i, be, nb, nv, pn: (i, 0))
    grid_spec = pltpu.PrefetchScalarGridSpec(
        num_scalar_prefetch=4,
        grid=(n_rows // tm,),
        in_specs=[part] * SC_SPLIT + [
                  pl.BlockSpec(memory_space=pl.ANY),
                  pl.BlockSpec((1, 1, 1, 2 * D_EXPERT), lambda i, be, nb, nv, pn: (layer, be[i], 0, 0)),
                  pl.BlockSpec(memory_space=pl.ANY),
                  pl.BlockSpec((1, 1, 1, D_MODEL), lambda i, be, nb, nv, pn: (layer, be[i], 0, 0))],
        out_specs=[part] * SC_SPLIT,
        scratch_shapes=[pltpu.VMEM((2, D_MODEL, 2 * D_EXPERT), F32), pltpu.VMEM((2, D_EXPERT, D_MODEL), F32),
                        pltpu.VMEM((D_MODEL, 2 * D_EXPERT), BF16), pltpu.VMEM((D_EXPERT, D_MODEL), BF16),
                        pltpu.SemaphoreType.DMA((2, 2))])
    return pl.pallas_call(
        functools.partial(_moe_kernel, layer=layer),
        grid_spec=grid_spec,
        out_shape=[jax.ShapeDtypeStruct((n_rows, SPLIT_WIDTH), jnp.uint32)] * SC_SPLIT,
        compiler_params=_params("arbitrary"),
        name="expert_ffn",
    )(block_e, n_used, n_valid, plan, *rows, w_gu, b_gu, w_down, b_down)


def _final_kernel(x_ref, tw_ref, mod_ref, lng_ref, lnb_ref, *rest, alpha):
    y_refs, o_ref = rest[:-1], rest[-1]
    top_w = tw_ref[...]
    parts = []
    for j in range(SC_SPLIT):
        acc = [0.0, 0.0]
        for k in range(TOP_K):
            for half, y in enumerate(_unpack_pairs(y_refs[j * TOP_K + k][...])):
                acc[half] = acc[half] + top_w[:, k:k + 1] * y
        parts += acc
    ffn = jnp.concatenate(parts, axis=1)
    o_ref[...] = (_layer_norm(alpha * x_ref[...] + (1.0 + mod_ref[5, 0]) * ffn) * lng_ref[0, 0] + lnb_ref[0, 0])


def _final_norm(x1, y_parts, top_w, mod_l, ln_g, ln_b, layer, seq, alpha):
    tokens = x1.shape[0]
    tm = ROW_TILE
    tiles = tokens // tm
    tiles_per_seq = seq // tm
    row = lambda w: pl.BlockSpec((tm, w), lambda i: (i, 0))
    vec = pl.BlockSpec((1, 1, 1, D_MODEL), lambda i: (layer, 1, 0, 0))
    y_specs, y_args = [], []
    for j in range(SC_SPLIT):
        for k in range(TOP_K):
            y_specs.append(pl.BlockSpec((tm, SPLIT_WIDTH), functools.partial(lambda i, k: (k * tiles + i, 0), k=k)))
            y_args.append(y_parts[j])
    return pl.pallas_call(
        functools.partial(_final_kernel, alpha=alpha),
        grid=(tiles,),
        in_specs=[row(D_MODEL), row(LANES),
                  pl.BlockSpec((6, 1, 1, D_MODEL), lambda i: (0, i // tiles_per_seq, 0, 0)), vec, vec] + y_specs,
        out_specs=row(D_MODEL),
        out_shape=jax.ShapeDtypeStruct(x1.shape, F32),
        compiler_params=_params("arbitrary"),
        name="combine_final_norm",
    )(x1, top_w, mod_l, ln_g, ln_b, *y_args)


def _inproj_weights(w_in):
    gate_cols = jnp.pad(w_in[:, :, OFF_NG:], ((0, 0), (0, 0), (0, LANES - N_GATE)))
    w_t = jnp.concatenate([w_in[:, :, off:off + width] for off, width in
                           ((OFF_NQ, NSA_WIDTH), (OFF_VS, NSA_KV_WIDTH), (OFF_VW, NSA_KV_WIDTH))] + [gate_cols],
                          axis=2).transpose(0, 2, 1)
    return w_in[:, :, :N_COLS].astype(BF16), w_t.astype(BF16)


def kernel(x, c, positions, w_in, w_out, ret_gn_w, conv_w, cmp_pos, cmp_w1, cmp_w2, ada_w, ada_b, ln_g, ln_b,
           router_w, router_b, w_gate_up, b_gate_up, w_down, b_down):
    batch, seq, _ = x.shape
    depth = w_in.shape[0]
    tokens = batch * seq
    n_rows = tokens * TOP_K + N_EXPERTS * MOE_TILE
    alpha = float((2 * depth) ** 0.25)

    w_in_r, w_in_t = _inproj_weights(w_in)
    w_out_b = w_out.astype(BF16)
    rw = jnp.pad(router_w, ((0, 0), (0, 0), (0, LANES - N_EXPERTS)))
    rw_hi = rw.astype(BF16)
    rw_lo = (rw - rw_hi.astype(F32)).astype(BF16)
    rb = jnp.pad(router_b, ((0, 0), (0, LANES - N_EXPERTS))).reshape(depth, 1, LANES)
    cw = _compress_weights(cmp_pos, cmp_w1, cmp_w2)
    ret_consts = _retention_consts()
    nsa_consts = _nsa_consts(seq)
    gn_w = ret_gn_w.reshape(depth, 1, RET_WIDTH)
    ln_g4 = ln_g.reshape(depth, 2, 1, D_MODEL)
    ln_b4 = ln_b.reshape(depth, 2, 1, D_MODEL)
    b_gu4 = b_gate_up.reshape(depth, N_EXPERTS, 1, 2 * D_EXPERT)
    b_d4 = b_down.reshape(depth, N_EXPERTS, 1, D_MODEL)

    mod = _modulation(c, ada_w, ada_b)
    cos, sin, cos_t, sin_t = _rope_tables(positions)
    xt = x.reshape(tokens, D_MODEL)
    for l in range(depth):
        (rq, rk, rv, rg, y_conv, qu, qr, kcx, vcx, ks, vs, kw, vw, ng) = _input_projection(
            xt, mod[l], w_in_r, w_in_t, cos, sin, cos_t, sin_t, conv_w, l, seq)
        y_ret = _retention(rq, rk, rv, rg, ret_consts, gn_w, l, batch, seq)
        kc, vc = _compress(kcx, vcx, cw, l, batch, seq)
        y_nsa = _nsa(qu, qr, kc, vc, ks, vs, kw, vw, ng, nsa_consts, batch, seq)
        x1, *h2, top_i, top_w, counts = _out_projection(y_ret, y_conv, y_nsa, xt, mod[l], w_out_b, ln_g4, ln_b4,
                                                        rw_hi, rw_lo, rb, l, seq, alpha)
        pstart, block_e, n_used, n_valid, plan = _block_table(counts, n_rows // MOE_TILE)
        pos_t = _route_positions(top_i, pstart)[:, :TOP_K].T
        rows = _scatter_rows(h2, pos_t, n_rows)
        y = _expert_ffn(rows, block_e, n_used, n_valid, plan, w_gate_up, b_gu4, w_down, b_d4, l)
        y_tok = _gather_rows(y, pos_t.reshape(-1))
        xt = _final_norm(x1, y_tok, top_w, mod[l], ln_g4, ln_b4, l, seq, alpha)
    return xt.reshape(batch, seq, D_MODEL)
```

```python
import functools

import numpy as np
import jax
import jax.numpy as jnp
from jax import lax
from jax.experimental import pallas as pl
from jax.experimental.pallas import tpu as pltpu
from jax.experimental.pallas import tpu_sc as plsc

F32 = jnp.float32
BF16 = jnp.bfloat16

D_MODEL = 1024
HEAD_DIM = 64
RET_WIDTH = 256
RET_HEADS = 4
RET_CHUNK = 128
CONV_WIDTH = 256
CONV_K = 3
NSA_WIDTH = 512
NSA_HEADS = 8
NSA_KV_HEADS = 2
NSA_GROUP = 4
NSA_KV_WIDTH = 128
CMP_LEN = 32
CMP_STRIDE = 16
CMP_HIDDEN = 128
SEL_LEN = 64
SEL_TOP = 8
WINDOW = 512
Q_BLOCK = 128
Q_TILE = 2 * Q_BLOCK
ROPE_THETA = 10000.0
N_EXPERTS = 32
TOP_K = 4
D_EXPERT = 1024
SWIGLU_LIMIT = 7.0
SWIGLU_ALPHA = 1.702
LN_EPS = 1e-5
NEG_INF = -1e30
FORCE_SCORE = 1e9

LANES = 128
VMEM_LIMIT = 56 * 1024 * 1024

OFF_RQ, OFF_RK, OFF_RV, OFF_RG = 0, 256, 512, 768
OFF_CB, OFF_CC, OFF_CH = 1024, 1280, 1536
OFF_NQ = 1792
OFF_KC, OFF_VC = 2304, 2432
OFF_KS, OFF_VS, OFF_KW, OFF_VW = 2560, 2688, 2816, 2944
OFF_NG = 3072
N_GATE = NSA_HEADS * 3
N_COLS = OFF_NG
LOG2E = 1.4426950408889634
WT_NQ, WT_VS, WT_VW, WT_NG = 0, NSA_WIDTH, NSA_WIDTH + NSA_KV_WIDTH, NSA_WIDTH + 2 * NSA_KV_WIDTH
WT_ROWS = WT_NG + LANES

RET_BATCH = 4
ROW_TILE = 1024
INPROJ_TILE = 1024
COMBINE_PARTS = 2
MOE_TILE = 512
MOE_SUB = 256
FFN_CHUNK = 256
ROUTE_TILE = 1024
SEL_GROUP = 4
V_ROWS = HEAD_DIM + 16
SC_WINDOW = LANES
SC_SPLIT = 2
SPLIT_WIDTH = D_MODEL // (2 * SC_SPLIT)


def _dot(a, b):
    return jnp.dot(a, b, preferred_element_type=F32)


def _dot_nt(a, b):
    return lax.dot_general(a, b, (((1,), (1,)), ((), ())), preferred_element_type=F32)


def _layer_norm(x):
    mu = jnp.mean(x, axis=-1, keepdims=True)
    xc = x - mu
    var = jnp.mean(xc * xc, axis=-1, keepdims=True)
    return xc * lax.rsqrt(var + LN_EPS)


def _pack_pairs(x):
    bits = lambda t: lax.bitcast_convert_type(t.astype(BF16).astype(F32), jnp.uint32)
    return (bits(x[:, SPLIT_WIDTH:]) & jnp.uint32(0xFFFF0000)) | (bits(x[:, :SPLIT_WIDTH]) >> 16)


def _unpack_pairs(p):
    low = lax.bitcast_convert_type(p << 16, F32)
    high = lax.bitcast_convert_type(p & jnp.uint32(0xFFFF0000), F32)
    return low, high


def _params(*sem):
    return pltpu.CompilerParams(dimension_semantics=sem, vmem_limit_bytes=VMEM_LIMIT)


def _mod_kernel(c_ref, w_ref, b_ref, o_ref):
    c = c_ref[...]
    ca = (c * jax.nn.sigmoid(c)).astype(BF16)
    o_ref[0, 0] = _dot(ca, w_ref[0].astype(BF16)) + b_ref[0]


def _modulation(c, ada_w, ada_b):
    depth = ada_w.shape[0]
    batch = c.shape[0]
    out = pl.pallas_call(
        _mod_kernel,
        grid=(depth, 6),
        in_specs=[pl.BlockSpec((batch, D_MODEL), lambda l, j: (0, 0)),
                  pl.BlockSpec((1, D_MODEL, D_MODEL), lambda l, j: (l, 0, j)),
                  pl.BlockSpec((1, 1, D_MODEL), lambda l, j: (l * 6 + j, 0, 0))],
        out_specs=pl.BlockSpec((1, 1, batch, D_MODEL), lambda l, j: (l, j, 0, 0)),
        out_shape=jax.ShapeDtypeStruct((depth, 6, batch, D_MODEL), F32),
        compiler_params=_params("arbitrary", "arbitrary"),
        name="adaln_mod",
    )(c, ada_w, ada_b.reshape(depth * 6, 1, D_MODEL))
    return out.reshape(depth, 6, batch, 1, D_MODEL)


def _rope_table_kernel(pos_ref, inv_ref, cos_ref, sin_ref):
    ang = pos_ref[...] * inv_ref[...]
    cos_ref[...] = jnp.cos(ang)
    sin_ref[...] = jnp.sin(ang)


def _rope_tables(positions):
    half = HEAD_DIM // 2
    per_row = LANES // half
    tokens = positions.size
    pos4 = jnp.repeat(positions.reshape(tokens // per_row, per_row).astype(F32), half, axis=1)
    inv = ROPE_THETA ** (-jnp.arange(half, dtype=F32) / half)
    inv4 = jnp.tile(inv, per_row)[None, :]
    rows = tokens // per_row
    tile = min(rows, 1024)
    cos4, sin4 = pl.pallas_call(
        _rope_table_kernel,
        grid=(rows // tile,),
        in_specs=[pl.BlockSpec((tile, LANES), lambda i: (i, 0)),
                  pl.BlockSpec((1, LANES), lambda i: (0, 0))],
        out_specs=[pl.BlockSpec((tile, LANES), lambda i: (i, 0))] * 2,
        out_shape=[jax.ShapeDtypeStruct((rows, LANES), F32)] * 2,
        compiler_params=_params("arbitrary"),
        name="rope_tables",
    )(pos4, inv4)
    cos32 = cos4.reshape(tokens, half)
    sin32 = sin4.reshape(tokens, half)
    cos = jnp.tile(cos32, (1, per_row))
    sign = jnp.tile(jnp.concatenate([-jnp.ones((half,), F32), jnp.ones((half,), F32)]), LANES // HEAD_DIM)
    sin = jnp.tile(sin32, (1, per_row)) * sign[None, :]
    return cos, sin, cos32.T, sin32.T


def _inproj_kernel(x_ref, mod_ref, w_ref, wt_ref, cos_ref, sin_ref, cost_ref, sint_ref, convw_ref,
                   rq_ref, rk_ref, rv_ref, rg_ref, yc_ref, qu_ref, qr_ref, kc_ref, vc_ref,
                   ks_ref, vs_ref, kw_ref, vw_ref, ng_ref, carry_ref, *, tiles_per_seq):
    i = pl.program_id(0)
    tm = x_ref.shape[0]
    h = (_layer_norm(x_ref[...]) * (1.0 + mod_ref[1, 0]) + mod_ref[0, 0]).astype(BF16)
    cosf = cos_ref[...]
    sinf = sin_ref[...]
    lane = lax.broadcasted_iota(jnp.int32, (tm, LANES), 1)
    first_half = (lane % HEAD_DIM) < (HEAD_DIM // 2)

    def proj(off, width):
        return _dot(h, w_ref[0, :, off:off + width])

    def rope(c):
        cols = []
        for j in range(c.shape[1] // LANES):
            cj = c[:, j * LANES:(j + 1) * LANES]
            swapped = jnp.where(first_half, pltpu.roll(cj, LANES - HEAD_DIM // 2, 1),
                                pltpu.roll(cj, HEAD_DIM // 2, 1))
            cols.append(cj * cosf + swapped * sinf)
        return jnp.concatenate(cols, axis=1) if len(cols) > 1 else cols[0]

    scale = HEAD_DIM ** -0.5

    def proj_t(off, width):
        return _dot_nt(wt_ref[0, off:off + width, :], h)

    def store(ref, value):
        ref[...] = value

    def conv_out(dots):
        cb, cc, ch = dots
        u = cc * ch

        @pl.when(i % tiles_per_seq == 0)
        def _():
            carry_ref[...] = jnp.zeros_like(carry_ref)

        carry = carry_ref[...]
        row = lax.broadcasted_iota(jnp.int32, (tm, CONV_WIDTH), 0)
        prev1 = jnp.where(row == 0, carry[7:8], pltpu.roll(u, 1, 0))
        prev2 = jnp.where(row == 0, carry[6:7], jnp.where(row == 1, carry[7:8], pltpu.roll(u, 2, 0)))
        cw = convw_ref[0]
        yc_ref[...] = (cb * (cw[0:1] * prev2 + cw[1:2] * prev1 + cw[2:3] * u)).astype(BF16)
        carry_ref[...] = u[tm - 8:tm]

    def queries_out(nq_t):
        nq_t = nq_t * (scale * LOG2E)
        qu_ref[...] = nq_t.astype(BF16)
        cos_t = cost_ref[...]
        sin_t = sint_ref[...]
        half = HEAD_DIM // 2
        rotated = []
        for hq in range(NSA_HEADS):
            t1 = nq_t[hq * HEAD_DIM:hq * HEAD_DIM + half]
            t2 = nq_t[hq * HEAD_DIM + half:(hq + 1) * HEAD_DIM]
            rotated += [t1 * cos_t - t2 * sin_t, t2 * cos_t + t1 * sin_t]
        qr_ref[...] = jnp.concatenate(rotated, axis=0).astype(BF16)

    def values_t(v_t):
        ones = jnp.ones((V_ROWS - HEAD_DIM, tm), F32)
        parts = []
        for hd in range(NSA_KV_HEADS):
            parts += [v_t[hd * HEAD_DIM:(hd + 1) * HEAD_DIM], ones]
        return jnp.concatenate(parts, axis=0).astype(BF16)

    stages = [
        (lambda: proj(OFF_RQ, RET_WIDTH), lambda d: store(rq_ref, rope(d).astype(BF16))),
        (lambda: proj(OFF_RK, RET_WIDTH), lambda d: store(rk_ref, (rope(d) * scale).astype(BF16))),
        (lambda: proj(OFF_RV, RET_WIDTH), lambda d: store(rv_ref, d.astype(BF16))),
        (lambda: proj(OFF_RG, RET_WIDTH), lambda d: store(rg_ref, (d * jax.nn.sigmoid(d)).astype(BF16))),
        (lambda: (proj(OFF_CB, CONV_WIDTH), proj(OFF_CC, CONV_WIDTH), proj(OFF_CH, CONV_WIDTH)), conv_out),
        (lambda: proj_t(WT_NQ, NSA_WIDTH), queries_out),
        (lambda: proj(OFF_KC, NSA_KV_WIDTH), lambda d: store(kc_ref, d)),
        (lambda: proj(OFF_VC, NSA_KV_WIDTH), lambda d: store(vc_ref, d)),
        (lambda: proj(OFF_KS, NSA_KV_WIDTH), lambda d: store(ks_ref, rope(d).astype(BF16))),
        (lambda: proj_t(WT_VS, NSA_KV_WIDTH), lambda d: store(vs_ref, values_t(d))),
        (lambda: proj(OFF_KW, NSA_KV_WIDTH), lambda d: store(kw_ref, rope(d).astype(BF16))),
        (lambda: proj_t(WT_VW, NSA_KV_WIDTH), lambda d: store(vw_ref, values_t(d))),
        (lambda: proj_t(WT_NG, LANES), lambda d: store(ng_ref, d)),
    ]
    for matmuls, epilogue in stages:
        epilogue(matmuls())


def _input_projection(x, mod_l, w_l, w_t, cos, sin, cos_t, sin_t, conv_w, layer, seq):
    tokens = x.shape[0]
    tm = INPROJ_TILE
    tiles_per_seq = seq // tm
    row = lambda w: pl.BlockSpec((tm, w), lambda i: (i, 0))
    col = lambda w: pl.BlockSpec((w, tm), lambda i: (0, i))
    outs = [(RET_WIDTH, BF16, False)] * 4 + [(CONV_WIDTH, BF16, False), (NSA_WIDTH, BF16, True),
                                            (NSA_WIDTH, BF16, True), (NSA_KV_WIDTH, F32, False),
                                            (NSA_KV_WIDTH, F32, False), (NSA_KV_WIDTH, BF16, False),
                                            (NSA_KV_HEADS * V_ROWS, BF16, True), (NSA_KV_WIDTH, BF16, False),
                                            (NSA_KV_HEADS * V_ROWS, BF16, True), (LANES, F32, True)]
    return pl.pallas_call(
        functools.partial(_inproj_kernel, tiles_per_seq=tiles_per_seq),
        grid=(tokens // tm,),
        in_specs=[row(D_MODEL),
                  pl.BlockSpec((6, 1, 1, D_MODEL), lambda i: (0, i // tiles_per_seq, 0, 0)),
                  pl.BlockSpec((1, D_MODEL, N_COLS), lambda i: (layer, 0, 0)),
                  pl.BlockSpec((1, WT_ROWS, D_MODEL), lambda i: (layer, 0, 0)),
                  row(LANES), row(LANES), col(HEAD_DIM // 2), col(HEAD_DIM // 2),
                  pl.BlockSpec((1, CONV_K, CONV_WIDTH), lambda i: (layer, 0, 0))],
        out_specs=[col(w) if t else row(w) for w, _, t in outs],
        out_shape=[jax.ShapeDtypeStruct((w, tokens) if t else (tokens, w), dt) for w, dt, t in outs],
        scratch_shapes=[pltpu.VMEM((8, CONV_WIDTH), F32)],
        compiler_params=_params("arbitrary"),
        name="ln_inproj",
    )(x, mod_l, w_l, w_t, cos, sin, cos_t, sin_t, conv_w)


def _retention_kernel(q_ref, k_ref, v_ref, g_ref, intra_ref, qdec_ref, kdec_ref, cdec_ref, gn_ref,
                      o_ref, state_ref):
    @pl.when(pl.program_id(1) == 0)
    def _():
        state_ref[...] = jnp.zeros_like(state_ref)

    n_pairs = RET_WIDTH // LANES
    lane = lax.broadcasted_iota(jnp.int32, (RET_CHUNK, LANES), 1)
    low = lane < HEAD_DIM
    blk_r = lax.broadcasted_iota(jnp.int32, (LANES, LANES), 0) < HEAD_DIM
    blk_c = lax.broadcasted_iota(jnp.int32, (LANES, LANES), 1) < HEAD_DIM
    same_head = blk_r == blk_c
    def half_mean(t):
        lo_sum = jnp.sum(jnp.where(low, t, 0.0), axis=-1, keepdims=True)
        hi_sum = jnp.sum(jnp.where(low, 0.0, t), axis=-1, keepdims=True)
        return jnp.where(low, lo_sum, hi_sum) * (1.0 / HEAD_DIM)

    combos = [(b, p) for b in range(q_ref.shape[0]) for p in range(n_pairs)]
    cols = {p: slice(p * LANES, (p + 1) * LANES) for p in range(n_pairs)}
    keeps = (low, jnp.logical_not(low))
    states = {c: state_ref[c[0], c[1]] for c in combos}
    for ch in range(q_ref.shape[1] // RET_CHUNK):
        rows = slice(ch * RET_CHUNK, (ch + 1) * RET_CHUNK)
        qp = {(b, p): q_ref[b, rows, cols[p]] for b, p in combos}
        kp = {(b, p): k_ref[b, rows, cols[p]] for b, p in combos}
        vp = {(b, p): v_ref[b, rows, cols[p]] for b, p in combos}
        scores = {(c, hh): _dot_nt(jnp.where(keeps[hh], qp[c], jnp.zeros_like(qp[c])), kp[c])
                  for c in combos for hh in range(2)}
        scores = {(c, hh): (s * intra_ref[2 * c[1] + hh]).astype(BF16) for (c, hh), s in scores.items()}
        intra = {key: _dot(s, vp[key[0]]) for key, s in scores.items()}
        inter = {c: _dot((qp[c].astype(F32) * qdec_ref[:, cols[c[1]]]).astype(BF16), states[c].astype(BF16))
                 for c in combos}
        upd = {c: _dot((kp[c].astype(F32) * kdec_ref[:, cols[c[1]]]).T.astype(BF16), vp[c]) for c in combos}
        states = {c: states[c] * cdec_ref[c[1]] + jnp.where(same_head, upd[c], 0.0) for c in combos}
        out = {c: jnp.where(low, intra[c, 0], intra[c, 1]) + inter[c] for c in combos}
        centred = {c: out[c] - half_mean(out[c]) for c in combos}
        normed = {c: centred[c] * lax.rsqrt(half_mean(centred[c] * centred[c]) + LN_EPS) for c in combos}
        for b in range(q_ref.shape[0]):
            full = jnp.concatenate([normed[b, p] for p in range(n_pairs)], axis=1)
            o_ref[b, rows, :] = (full * gn_ref[0] * g_ref[b, rows, :].astype(F32)).astype(BF16)
    for (b, p), state in states.items():
        state_ref[b, p] = state


def _retention_consts():
    heads = jnp.arange(RET_HEADS, dtype=F32)
    log_gamma = jnp.log(1.0 - jnp.power(2.0, -5.0 - heads))
    i = jnp.arange(RET_CHUNK, dtype=F32)
    diff = i[:, None] - i[None, :]
    intra = jnp.where(diff >= 0, jnp.exp(diff * log_gamma[:, None, None]), 0.0)
    q_dec = jnp.exp((i + 1.0) * log_gamma[:, None])
    k_dec = jnp.exp((RET_CHUNK - 1.0 - i) * log_gamma[:, None])
    c_dec = jnp.exp(RET_CHUNK * log_gamma)
    expand = lambda t: jnp.repeat(t.T, HEAD_DIM, axis=1)
    c_rows = jnp.repeat(c_dec, HEAD_DIM).reshape(RET_WIDTH // LANES, LANES, 1)
    c_blk = jnp.broadcast_to(c_rows, (RET_WIDTH // LANES, LANES, LANES))
    return intra, expand(q_dec), expand(k_dec), c_blk


def _retention(rq, rk, rv, rg, consts, gn_w, layer, batch, seq):
    intra, q_dec, k_dec, c_blk = consts
    rows = 2 * RET_CHUNK
    steps = seq // rows
    group = RET_BATCH if batch % RET_BATCH == 0 else 1
    blk = pl.BlockSpec((group, rows, RET_WIDTH), lambda b, s: (b, s, 0))
    full = lambda a: pl.BlockSpec(a.shape, lambda b, s: (0,) * a.ndim)
    per_seq = lambda t: t.reshape(batch, seq, RET_WIDTH)
    out = pl.pallas_call(
        _retention_kernel,
        grid=(batch // group, steps),
        in_specs=[blk, blk, blk, blk, full(intra), full(q_dec), full(k_dec), full(c_blk),
                  pl.BlockSpec((1, 1, RET_WIDTH), lambda b, s: (layer, 0, 0))],
        out_specs=blk,
        out_shape=jax.ShapeDtypeStruct((batch, seq, RET_WIDTH), BF16),
        scratch_shapes=[pltpu.VMEM((group, RET_WIDTH // LANES, LANES, LANES), F32)],
        compiler_params=_params("arbitrary", "arbitrary"),
        name="retention",
    )(per_seq(rq), per_seq(rk), per_seq(rv), per_seq(rg), intra, q_dec, k_dec, c_blk, gn_w)
    return out.reshape(batch * seq, RET_WIDTH)


def _compress_kernel(xk_ref, xv_ref, posa_ref, posb_ref, w1a_ref, w1b_ref, w2_ref, kc_ref, vc_ref):
    n_grp = xk_ref.shape[0] // CMP_STRIDE
    for kv, (x_ref, o_ref) in enumerate(((xk_ref, kc_ref), (xv_ref, vc_ref))):
        ya = jnp.zeros((n_grp, NSA_KV_HEADS * CMP_HIDDEN), F32)
        yb = jnp.zeros((n_grp, NSA_KV_HEADS * CMP_HIDDEN), F32)
        for r in range(CMP_STRIDE):
            x = x_ref[pl.ds(r, n_grp, stride=CMP_STRIDE), :]
            cols = slice(r * NSA_KV_WIDTH, (r + 1) * NSA_KV_WIDTH)
            ya = ya + _dot((x + posa_ref[0, kv, :, cols]).astype(BF16), w1a_ref[0, kv, cols, :])
            yb = yb + _dot((x + posb_ref[0, kv, :, cols]).astype(BF16), w1b_ref[0, kv, cols, :])
        hidden = ya + pltpu.roll(yb, n_grp - 1, 0)
        act = jax.nn.gelu(hidden)
        out = _dot(act.astype(BF16), w2_ref[0, kv])
        o_ref[0] = (out if kv == 0 else out.T).astype(BF16)


def _compress_weights(cmp_pos, cmp_w1, cmp_w2):
    depth = cmp_w1.shape[0]
    eye = jnp.eye(NSA_KV_HEADS, dtype=F32)
    a = cmp_w1.reshape(depth, 2, 2, CMP_STRIDE, HEAD_DIM, CMP_HIDDEN)
    w1 = jnp.einsum('lkardj,hg->lkarhdgj', a, eye)
    w1 = w1.reshape(depth, 2, 2, CMP_STRIDE * NSA_KV_WIDTH, NSA_KV_HEADS * CMP_HIDDEN).astype(BF16)
    pos = cmp_pos.reshape(depth, 2, 2, CMP_STRIDE, 1, HEAD_DIM)
    pos = jnp.broadcast_to(pos, (depth, 2, 2, CMP_STRIDE, NSA_KV_HEADS, HEAD_DIM))
    pos = pos.reshape(depth, 2, 2, 1, CMP_STRIDE * NSA_KV_WIDTH)
    w2 = jnp.einsum('lkje,hg->lkhjge', cmp_w2, eye)
    w2 = w2.reshape(depth, 2, NSA_KV_HEADS * CMP_HIDDEN, NSA_KV_WIDTH).astype(BF16)
    return w1[:, :, 0], w1[:, :, 1], pos[:, :, 0], pos[:, :, 1], w2


def _compress(kcx, vcx, cw, layer, batch, seq):
    w1a, w1b, posa, posb, w2 = cw
    n_grp = seq // CMP_STRIDE
    xblk = pl.BlockSpec((seq, NSA_KV_WIDTH), lambda b: (b, 0))
    lay = lambda a: pl.BlockSpec((1,) + a.shape[1:], lambda b: (layer,) + (0,) * (a.ndim - 1))
    oblk = pl.BlockSpec((1, n_grp, NSA_KV_WIDTH), lambda b: (b, 0, 0))
    return pl.pallas_call(
        _compress_kernel,
        grid=(batch,),
        in_specs=[xblk, xblk, lay(posa), lay(posb), lay(w1a), lay(w1b), lay(w2)],
        out_specs=[oblk, oblk],
        out_shape=[jax.ShapeDtypeStruct((batch, n_grp, NSA_KV_WIDTH), BF16)] * 2,
        compiler_params=_params("arbitrary"),
        name="nsa_compress",
    )(kcx, vcx, posa, posb, w1a, w1b, w2)


def _nsa_kernel(qu_ref, qr_ref, kc_ref, vc_ref, ks_ref, vs_ref, kw_ref, vw_ref, ng_ref, ov_ref, wbias_ref, ind_ref,
                o_ref):
    qb = pl.program_id(1)
    q0 = qb * Q_TILE
    qb_last = (qb + 1) * (Q_TILE // Q_BLOCK) - 1
    n_sel = ov_ref.shape[0]
    cols4 = NSA_GROUP * Q_TILE
    n_past = WINDOW // Q_BLOCK
    gsig = jax.nn.sigmoid(ng_ref[...])

    def padded_q(ref, h):
        zeros = jnp.zeros((HEAD_DIM, Q_TILE), BF16)
        parts = []
        for g in range(NSA_GROUP):
            hq = h * NSA_GROUP + g
            x = ref[hq * HEAD_DIM:(hq + 1) * HEAD_DIM, :]
            parts.append(jnp.concatenate([x, zeros] if h == 0 else [zeros, x], axis=0))
        return jnp.concatenate(parts, axis=1)

    def tile4(t):
        return jnp.concatenate([t] * NSA_GROUP, axis=1)

    def attend(state, keys, values_ref, k0, bias, q_t):
        m, acc = state
        s = _dot(keys, q_t)
        if bias is not None:
            s = s + bias
        m_new = jnp.maximum(m, jnp.max(s, axis=0, keepdims=True))
        e = jnp.exp2(s - m_new).astype(BF16)
        pv = [_dot(values_ref[h * V_ROWS:(h + 1) * V_ROWS, pl.ds(k0, keys.shape[0])],
                   e[:, h * cols4:(h + 1) * cols4]) for h in range(NSA_KV_HEADS)]
        return m_new, jnp.exp2(m - m_new) * acc + jnp.concatenate(pv, axis=1)

    def normalised(state):
        acc = state[1]
        return acc[0:HEAD_DIM] * (1.0 / acc[HEAD_DIM:HEAD_DIM + 1])

    cols_all = NSA_KV_HEADS * cols4
    init = (jnp.full((1, cols_all), NEG_INF, F32), jnp.zeros((V_ROWS, cols_all), F32))
    q_rot = [padded_q(qr_ref, h) for h in range(NSA_KV_HEADS)]

    def position_bias(kb, windowed):
        first = qb * (Q_TILE // Q_BLOCK)
        return jnp.concatenate(
            [wbias_ref[jnp.clip(kb - (first + sub) + n_past + 1, 0 if windowed else 2, n_past + 2)]
             for sub in range(Q_TILE // Q_BLOCK)] * (NSA_GROUP * NSA_KV_HEADS), axis=1)

    n_cols = NSA_KV_HEADS * NSA_GROUP
    qus_all = jnp.concatenate([padded_q(qu_ref, h) for h in range(NSA_KV_HEADS)], axis=1)
    s = _dot(kc_ref[0], qus_all)
    cid = lax.broadcasted_iota(jnp.int32, (LANES, Q_TILE), 0)
    tq = q0 + lax.broadcasted_iota(jnp.int32, (LANES, Q_TILE), 1)
    cvalid = jnp.concatenate([jnp.where(cid * CMP_STRIDE + (CMP_LEN - 1) <= tq, 1.0, 0.0)] * n_cols, axis=1)
    sm = jnp.where(cvalid > 0.5, s, NEG_INF)
    e = jnp.exp2(sm - jnp.max(sm, axis=0, keepdims=True)) * cvalid
    l = jnp.sum(e, axis=0, keepdims=True)
    p = e * (1.0 / jnp.where(l > 0.0, l, 1.0))
    p_bf = p.astype(BF16)
    o_cmp = [_dot(vc_ref[0, h * HEAD_DIM:(h + 1) * HEAD_DIM, :], p_bf[:, h * cols4:(h + 1) * cols4])
             for h in range(NSA_KV_HEADS)]

    psum = jnp.concatenate(
        [functools.reduce(lambda a, b: a + b, [p[:, (h * NSA_GROUP + g) * Q_TILE:(h * NSA_GROUP + g + 1) * Q_TILE]
                                               for g in range(NSA_GROUP)]) for h in range(NSA_KV_HEADS)], axis=1)
    p_hi = psum.astype(BF16)
    rem = psum - p_hi.astype(F32)
    p_mid = rem.astype(BF16)
    p_lo = (rem - p_mid.astype(F32)).astype(BF16)
    ov = ov_ref[...]
    imp = _dot(ov, p_hi) + _dot(ov, p_mid) + _dot(ov, p_lo)
    sel_cols = NSA_KV_HEADS * Q_TILE
    jid = lax.broadcasted_iota(jnp.int32, (n_sel, sel_cols), 0)
    tid = q0 + lax.broadcasted_iota(jnp.int32, (n_sel, sel_cols), 1) % Q_TILE
    forced = (jid == 0) | (jid == jnp.right_shift(tid, 6))
    imp = jnp.where(forced, FORCE_SCORE, jnp.where(jid * SEL_LEN <= tid, imp, -FORCE_SCORE))
    rank = jnp.zeros((n_sel, sel_cols), F32)
    for i in range(n_sel):
        ri = imp[i:i + 1, :]
        beats = (ri > imp) | ((ri == imp) & (jid > i))
        rank = rank + jnp.where(beats, 1.0, 0.0)
    sel_bias = jnp.where(rank < float(SEL_TOP), 0.0, NEG_INF).astype(BF16)
    pad = jnp.zeros((LANES - n_sel, cols4), BF16)
    q_sel = [jnp.concatenate([q_rot[h], tile4(sel_bias[:, h * Q_TILE:(h + 1) * Q_TILE]), pad], axis=0)
             for h in range(NSA_KV_HEADS)]

    q_sel_all = jnp.concatenate(q_sel, axis=1)
    q_rot_all = jnp.concatenate(q_rot, axis=1)

    def sel_step(gi, state, last):
        k0 = pl.multiple_of(gi * (SEL_GROUP * Q_BLOCK), SEL_GROUP * Q_BLOCK)
        keys = jnp.concatenate([ks_ref[pl.ds(k0, SEL_GROUP * Q_BLOCK), :], ind_ref[gi]], axis=1)
        bias = None
        if last:
            bias = jnp.concatenate([position_bias(gi * SEL_GROUP + u, False) for u in range(SEL_GROUP)], axis=0)
        return attend(state, keys, vs_ref, k0, bias, q_sel_all)

    last_group = qb_last // SEL_GROUP
    sel_state = lax.fori_loop(0, last_group, functools.partial(sel_step, last=False), init)
    o_sel = normalised(sel_step(last_group, sel_state, True))

    ws = pl.multiple_of(jnp.maximum(q0 - WINDOW, 0), Q_BLOCK)
    n_win = n_past + Q_TILE // Q_BLOCK
    win_bias = jnp.concatenate([position_bias(ws // Q_BLOCK + u, True) for u in range(n_win)], axis=0)
    win_keys = kw_ref[pl.ds(ws, n_win * Q_BLOCK), :]
    o_win = normalised(attend(init, win_keys, vw_ref, ws, win_bias, q_rot_all))
    heads = []
    for h in range(NSA_KV_HEADS):
        o_c = o_cmp[h]
        for g in range(NSA_GROUP):
            hq = h * NSA_GROUP + g
            c = slice(g * Q_TILE, (g + 1) * Q_TILE)
            ca = slice(hq * Q_TILE, (hq + 1) * Q_TILE)
            heads.append(gsig[3 * hq:3 * hq + 1, :] * o_c[:, c] + gsig[3 * hq + 1:3 * hq + 2, :] * o_sel[:, ca]
                         + gsig[3 * hq + 2:3 * hq + 3, :] * o_win[:, ca])
    cols_out = [jnp.concatenate(heads[2 * m:2 * m + 2], axis=0).T for m in range(NSA_HEADS // 2)]
    o_ref[...] = jnp.concatenate(cols_out, axis=1).astype(BF16)


def _nsa_consts(seq):
    n_cmp_pad = seq // CMP_STRIDE
    n_sel = seq // SEL_LEN
    c = np.arange(n_cmp_pad)
    j = np.arange(n_sel)
    n_cmp = (seq - CMP_LEN) // CMP_STRIDE + 1
    ov = ((c[None, :] * CMP_STRIDE < j[:, None] * SEL_LEN + SEL_LEN) &
          (j[:, None] * SEL_LEN <= c[None, :] * CMP_STRIDE + CMP_LEN - 1) & (c[None, :] < n_cmp))
    kl = np.arange(Q_BLOCK)[:, None]
    tl = np.arange(Q_BLOCK)[None, :]
    n_past = WINDOW // Q_BLOCK
    none = np.zeros((Q_BLOCK, Q_BLOCK), bool)
    valid = [none, kl > tl] + [np.ones((Q_BLOCK, Q_BLOCK), bool)] * (n_past - 1) + [kl <= tl, none]
    wbias = np.where(np.stack(valid), 0.0, NEG_INF).astype(np.float32)
    key_block = np.arange(seq) // SEL_LEN
    onehot = (key_block[:, None] == np.arange(LANES)[None, :]).reshape(-1, SEL_GROUP * Q_BLOCK, LANES)
    return jnp.asarray(ov, BF16), jnp.asarray(wbias), jnp.asarray(onehot, BF16)


def _nsa(qu, qr, kc, vc, ks, vs, kw, vw, ng, consts, batch, seq):
    ov, wbias, onehot = consts
    n_qb = seq // Q_TILE
    tokens = batch * seq
    qblk = pl.BlockSpec((NSA_WIDTH, Q_TILE), lambda b, q: (0, b * n_qb + q))
    cblk = pl.BlockSpec((1, seq // CMP_STRIDE, NSA_KV_WIDTH), lambda b, q: (b, 0, 0))
    kblk = pl.BlockSpec((seq, NSA_KV_WIDTH), lambda b, q: (b, 0))
    vblk = pl.BlockSpec((NSA_KV_HEADS * V_ROWS, seq), lambda b, q: (0, b))
    full = lambda a: pl.BlockSpec(a.shape, lambda b, q: (0,) * a.ndim)
    return pl.pallas_call(
        _nsa_kernel,
        grid=(batch, n_qb),
        in_specs=[qblk, qblk, cblk, cblk, kblk, vblk, kblk, vblk,
                  pl.BlockSpec((LANES, Q_TILE), lambda b, q: (0, b * n_qb + q)), full(ov), full(wbias),
                  full(onehot)],
        out_specs=pl.BlockSpec((Q_TILE, NSA_WIDTH), lambda b, q: (b * n_qb + q, 0)),
        out_shape=jax.ShapeDtypeStruct((tokens, NSA_WIDTH), BF16),
        compiler_params=_params("arbitrary", "arbitrary"),
        name="nsa_attention",
    )(qu, qr, kc, vc, ks, vs, kw, vw, ng, ov, wbias, onehot)


def _outproj_kernel(yr_ref, yc_ref, yn_ref, x_ref, mod_ref, w_ref, lng_ref, lnb_ref, rwh_ref, rwl_ref, rb_ref,
                    x1_ref, h2a_ref, h2b_ref, ti_ref, tw_ref, cnt_ref, *, alpha):
    tm = x_ref.shape[0]
    mix = (_dot(yr_ref[...], w_ref[0, 0:RET_WIDTH, :])
           + _dot(yc_ref[...], w_ref[0, RET_WIDTH:RET_WIDTH + CONV_WIDTH, :])
           + _dot(yn_ref[...], w_ref[0, RET_WIDTH + CONV_WIDTH:, :]))
    x1 = _layer_norm(alpha * x_ref[...] + (1.0 + mod_ref[2, 0]) * mix) * lng_ref[0, 0] + lnb_ref[0, 0]
    x1_ref[...] = x1
    h2 = _layer_norm(x1) * (1.0 + mod_ref[4, 0]) + mod_ref[3, 0]
    for j, part_ref in enumerate((h2a_ref, h2b_ref)):
        part_ref[...] = _pack_pairs(h2[:, j * 2 * SPLIT_WIDTH:(j + 1) * 2 * SPLIT_WIDTH])
    h_hi = h2.astype(BF16)
    h_lo = (h2 - h_hi.astype(F32)).astype(BF16)
    logits = _dot(h_hi, rwh_ref[0]) + _dot(h_lo, rwh_ref[0]) + _dot(h_hi, rwl_ref[0]) + rb_ref[0]

    lane = lax.broadcasted_iota(jnp.int32, (tm, LANES), 1)
    lanef = lane.astype(F32)
    rest = jnp.where(lane < N_EXPERTS, logits, -jnp.inf)
    vals, idxs = [], []
    for _ in range(TOP_K):
        top = jnp.max(rest, axis=-1, keepdims=True)
        idx = jnp.min(jnp.where(rest == top, lanef, float(LANES)), axis=-1, keepdims=True)
        vals.append(top)
        idxs.append(idx)
        rest = jnp.where(lanef == idx, -jnp.inf, rest)
    exps = [jnp.exp(v - vals[0]) for v in vals]
    inv = 1.0 / functools.reduce(lambda a, b: a + b, exps)
    top_w = jnp.zeros((tm, LANES), F32)
    top_i = jnp.zeros((tm, LANES), F32)
    member = jnp.zeros((tm, LANES), F32)
    for k in range(TOP_K):
        top_w = jnp.where(lane == k, exps[k] * inv, top_w)
        top_i = jnp.where(lane == k, idxs[k], top_i)
        member = member + jnp.where(lanef == idxs[k], 1.0, 0.0)
    tw_ref[...] = top_w
    ti_ref[...] = top_i.astype(jnp.int32)

    @pl.when(pl.program_id(0) == 0)
    def _():
        cnt_ref[...] = jnp.zeros_like(cnt_ref)

    cnt_ref[...] += jnp.broadcast_to(jnp.sum(member, axis=0, keepdims=True), cnt_ref.shape)


def _out_projection(y_ret, y_conv, y_nsa, x, mod_l, w_out, ln_g, ln_b, rw_hi, rw_lo, rb, layer, seq, alpha):
    tokens = x.shape[0]
    tm = ROW_TILE
    tiles_per_seq = seq // tm
    row = lambda w: pl.BlockSpec((tm, w), lambda i: (i, 0))
    lay3 = lambda a: pl.BlockSpec((1,) + a.shape[1:], lambda i: (layer,) + (0,) * (a.ndim - 1))
    return pl.pallas_call(
        functools.partial(_outproj_kernel, alpha=alpha),
        grid=(tokens // tm,),
        in_specs=[row(RET_WIDTH), row(CONV_WIDTH), row(NSA_WIDTH), row(D_MODEL),
                  pl.BlockSpec((6, 1, 1, D_MODEL), lambda i: (0, i // tiles_per_seq, 0, 0)),
                  lay3(w_out),
                  pl.BlockSpec((1, 1, 1, D_MODEL), lambda i: (layer, 0, 0, 0)),
                  pl.BlockSpec((1, 1, 1, D_MODEL), lambda i: (layer, 0, 0, 0)),
                  lay3(rw_hi), lay3(rw_lo), lay3(rb)],
        out_specs=[row(D_MODEL)] + [row(SPLIT_WIDTH)] * SC_SPLIT + [row(LANES), row(LANES),
                   pl.BlockSpec((8, LANES), lambda i: (0, 0))],
        out_shape=[jax.ShapeDtypeStruct((tokens, D_MODEL), F32)]
                  + [jax.ShapeDtypeStruct((tokens, SPLIT_WIDTH), jnp.uint32)] * SC_SPLIT
                  + [jax.ShapeDtypeStruct((tokens, LANES), jnp.int32), jax.ShapeDtypeStruct((tokens, LANES), F32),
                   jax.ShapeDtypeStruct((8, LANES), F32)],
        compiler_params=_params("arbitrary"),
        name="outproj_norm_router",
    )(y_ret, y_conv, y_nsa, x, mod_l, w_out, ln_g, ln_b, rw_hi, rw_lo, rb)


def _route_kernel(ti_ref, pstart_ref, tri_ref, pos_ref, carry_ref):
    @pl.when(pl.program_id(0) == 0)
    def _():
        carry_ref[...] = jnp.zeros_like(carry_ref)

    tm = ti_ref.shape[0]
    lane = lax.broadcasted_iota(jnp.int32, (tm, LANES), 1)
    top_i = ti_ref[...]
    onehots = [lane == top_i[:, k:k + 1] for k in range(TOP_K)]
    member = functools.reduce(lambda a, b: a + b, [jnp.where(o, 1.0, 0.0) for o in onehots])
    base = pstart_ref[...] + carry_ref[0:1] + _dot(tri_ref[...], member.astype(BF16))
    pos = jnp.zeros((tm, LANES), F32)
    for k in range(TOP_K):
        pos_k = jnp.sum(jnp.where(onehots[k], base, 0.0), axis=-1, keepdims=True)
        pos = jnp.where(lane == k, pos_k, pos)
    pos_ref[...] = pos.astype(jnp.int32)
    carry_ref[...] += jnp.broadcast_to(jnp.sum(member, axis=0, keepdims=True), carry_ref.shape)


def _route_positions(top_i, pstart):
    tokens = top_i.shape[0]
    tm = min(ROUTE_TILE, tokens)
    r = np.arange(tm)
    tri = jnp.asarray(r[None, :] < r[:, None], BF16)
    return pl.pallas_call(
        _route_kernel,
        grid=(tokens // tm,),
        in_specs=[pl.BlockSpec((tm, LANES), lambda i: (i, 0)),
                  pl.BlockSpec((1, LANES), lambda i: (0, 0)),
                  pl.BlockSpec((tm, tm), lambda i: (0, 0))],
        out_specs=pl.BlockSpec((tm, LANES), lambda i: (i, 0)),
        out_shape=jax.ShapeDtypeStruct((tokens, LANES), jnp.int32),
        scratch_shapes=[pltpu.VMEM((8, LANES), F32)],
        compiler_params=_params("arbitrary"),
        name="route_positions",
    )(top_i, pstart, tri)


def _block_table(counts, n_blocks):
    tm = MOE_TILE
    cnt = counts[0, :N_EXPERTS].astype(jnp.int32)
    nblk = (cnt + tm - 1) // tm
    experts = jnp.arange(N_EXPERTS, dtype=jnp.int32)
    before = experts[:, None] < experts[None, :]
    bstart = jnp.sum(jnp.where(before, nblk[:, None], 0), axis=0)
    bend = bstart + nblk
    n_used = jnp.sum(nblk)
    blocks = jnp.arange(n_blocks, dtype=jnp.int32)
    block_e = jnp.minimum(jnp.sum((bend[None, :] <= blocks[:, None]).astype(jnp.int32), axis=1), N_EXPERTS - 1)
    onehot = block_e[:, None] == experts[None, :]
    lookup = lambda per_expert: jnp.sum(jnp.where(onehot, per_expert[None, :], 0), axis=1)
    live = blocks < n_used
    n_valid = jnp.where(live, jnp.clip(lookup(cnt) - (blocks - lookup(bstart)) * tm, 0, tm), 0)
    pstart = jnp.pad((bstart * tm).astype(F32), (0, LANES - N_EXPERTS))[None, :]
    used = (nblk > 0).astype(jnp.int32)
    first = live & (blocks == lookup(bstart))
    slot = lookup(jnp.sum(jnp.where(before, used[:, None], 0), axis=0)) % 2
    later = jnp.where((used[None, :] > 0) & before, experts[None, :], N_EXPERTS)
    next_expert = jnp.min(later, axis=1)
    next_expert = lookup(jnp.where(next_expert < N_EXPERTS, next_expert, -1))
    plan = jnp.stack([first.astype(jnp.int32), slot, next_expert]).astype(jnp.int32)
    return pstart, block_e, n_used.reshape(1), n_valid.astype(jnp.int32), plan


def _sc_mesh():
    return plsc.VectorSubcoreMesh(core_axis_name="core", subcore_axis_name="subcore")


def _scatter_rows(xs, idx_t, n_rows):
    tokens, width = xs[0].shape
    n_idx = idx_t.shape[0]
    n_x = len(xs)
    win = SC_WINDOW

    @functools.partial(pl.kernel, out_type=[jax.ShapeDtypeStruct((n_rows, width), xs[0].dtype)] * n_x,
                       mesh=_sc_mesh(), scratch_types=[pltpu.SemaphoreType.DMA], name="dispatch_scatter")
    def scatter(*refs):
        x_hbms, i_hbm, o_hbms, sem = refs[:n_x], refs[n_x], refs[n_x + 1:-1], refs[-1]
        idx_specs = [pl.BlockSpec((1, win), functools.partial(lambda i, k: (k, i), k=k)) for k in range(n_idx)]
        for x_hbm, o_hbm in zip(x_hbms, o_hbms):
            def body(x_vmem, *i_vmems, o_hbm=o_hbm):
                copies = [pltpu.async_copy(x_vmem, o_hbm.at[i_vmem.at[0]], sem) for i_vmem in i_vmems]
                for copy in copies:
                    copy.wait()

            pltpu.emit_pipeline(
                body,
                grid=(tokens // win,),
                in_specs=[pl.BlockSpec((win, width), lambda i: (i, 0))] + idx_specs,
                out_specs=[],
                core_axis_name=("core", "subcore"),
                dimension_semantics=(pltpu.PARALLEL,),
            )(x_hbm, *([i_hbm] * n_idx))

    return scatter(*xs, idx_t)


def _gather_rows(xs, idx):
    width = xs[0].shape[1]
    n = idx.shape[0]
    n_x = len(xs)
    win = SC_WINDOW

    @functools.partial(pl.kernel, out_type=[jax.ShapeDtypeStruct((n, width), xs[0].dtype)] * n_x,
                       mesh=_sc_mesh(), scratch_types=[], name="combine_gather")
    def gather(*refs):
        x_hbms, i_hbm, o_hbms = refs[:n_x], refs[n_x], refs[n_x + 1:]
        for x_hbm, o_hbm in zip(x_hbms, o_hbms):
            def body(i_vmem, o_vmem, x_hbm=x_hbm):
                pltpu.sync_copy(x_hbm.at[i_vmem.at[0]], o_vmem)

            pltpu.emit_pipeline(
                body,
                grid=(n // win,),
                in_specs=[pl.BlockSpec((1, win), lambda i: (0, i))],
                out_specs=[pl.BlockSpec((win, width), lambda i: (i, 0))],
                core_axis_name=("core", "subcore"),
                dimension_semantics=(pltpu.PARALLEL,),
            )(i_hbm, o_hbm)

    return gather(*xs, idx.reshape(1, n))


def _moe_kernel(be_ref, nb_ref, nv_ref, plan_ref, ra_ref, rb_ref, wgu_hbm, bgu_ref, wd_hbm, bd_ref,
                oa_ref, ob_ref, wgu_f32, wd_f32, wgu_bf, wd_bf, sem, *, layer):
    row_refs = (ra_ref, rb_ref)
    out_refs = (oa_ref, ob_ref)
    i = pl.program_id(0)
    valid = i < nb_ref[0]

    def weight_copies(expert, slot):
        return (pltpu.make_async_copy(wgu_hbm.at[layer, expert], wgu_f32.at[slot], sem.at[0, slot]),
                pltpu.make_async_copy(wd_hbm.at[layer, expert], wd_f32.at[slot], sem.at[1, slot]))

    @pl.when(i == 0)
    def _():
        for copy in weight_copies(be_ref[0], 0):
            copy.start()

    @pl.when(valid & (plan_ref[0, i] == 1))
    def _():
        slot = plan_ref[1, i]
        for copy in weight_copies(be_ref[i], slot):
            copy.wait()
        upcoming = plan_ref[2, i]

        @pl.when(upcoming >= 0)
        def _():
            for copy in weight_copies(upcoming, 1 - slot):
                copy.start()

        wgu_bf[...] = wgu_f32[slot].astype(BF16)
        wd_bf[...] = wd_f32[slot].astype(BF16)

    def ffn(rs):
        live = lax.broadcasted_iota(jnp.int32, (rs.stop - rs.start, SPLIT_WIDTH), 0) < nv_ref[i] - rs.start
        rows = jnp.concatenate(
            [half.astype(BF16) for r_ref in row_refs
             for half in _unpack_pairs(jnp.where(live, r_ref[rs, :], jnp.uint32(0)))], axis=1)
        def gate_up(c):
            cg = slice(c * FFN_CHUNK, (c + 1) * FFN_CHUNK)
            cu = slice(D_EXPERT + c * FFN_CHUNK, D_EXPERT + (c + 1) * FFN_CHUNK)
            return (_dot(rows, wgu_bf[:, cg]) + bgu_ref[0, 0, :, cg], _dot(rows, wgu_bf[:, cu]) + bgu_ref[0, 0, :, cu])

        def activation(gu):
            g = jnp.minimum(gu[0], SWIGLU_LIMIT)
            u = jnp.clip(gu[1], -SWIGLU_LIMIT, SWIGLU_LIMIT)
            return ((u + 1.0) * (g * jax.nn.sigmoid(SWIGLU_ALPHA * g))).astype(BF16)

        acts = []
        pending = gate_up(0)
        for c in range(1, D_EXPERT // FFN_CHUNK):
            upcoming = gate_up(c)
            acts.append(activation(pending))
            pending = upcoming
        acts.append(activation(pending))
        y = _dot(jnp.concatenate(acts, axis=1), wd_bf[...]) + bd_ref[0, 0]
        for j, o_ref in enumerate(out_refs):
            o_ref[rs, :] = _pack_pairs(y[:, j * 2 * SPLIT_WIDTH:(j + 1) * 2 * SPLIT_WIDTH])

    tail = slice(MOE_SUB, MOE_TILE)
    pl.when(valid & (nv_ref[i] > MOE_SUB))(functools.partial(ffn, slice(0, MOE_TILE)))
    pl.when(valid & (nv_ref[i] <= MOE_SUB))(functools.partial(ffn, slice(0, MOE_SUB)))

    @pl.when(jnp.logical_not(valid))
    def _():
        for o_ref in out_refs:
            o_ref[0:MOE_SUB, :] = jnp.zeros((MOE_SUB, SPLIT_WIDTH), jnp.uint32)

    @pl.when(jnp.logical_not(valid & (nv_ref[i] > MOE_SUB)))
    def _():
        for o_ref in out_refs:
            o_ref[tail, :] = jnp.zeros((MOE_TILE - MOE_SUB, SPLIT_WIDTH), jnp.uint32)


def _expert_ffn(rows, block_e, n_used, n_valid, plan, w_gu, b_gu, w_down, b_down, layer):
    n_rows = rows[0].shape[0]
    tm = MOE_TILE
    part = pl.BlockSpec((tm, SPLIT_WIDTH), lambda i, be, nb, nv, pn: (i, 0))
    grid_spec = pltpu.PrefetchScalarGridSpec(
        num_scalar_prefetch=4,
        grid=(n_rows // tm,),
        in_specs=[part] * SC_SPLIT + [
                  pl.BlockSpec(memory_space=pl.ANY),
                  pl.BlockSpec((1, 1, 1, 2 * D_EXPERT), lambda i, be, nb, nv, pn: (layer, be[i], 0, 0)),
                  pl.BlockSpec(memory_space=pl.ANY),
                  pl.BlockSpec((1, 1, 1, D_MODEL), lambda i, be, nb, nv, pn: (layer, be[i], 0, 0))],
        out_specs=[part] * SC_SPLIT,
        scratch_shapes=[pltpu.VMEM((2, D_MODEL, 2 * D_EXPERT), F32), pltpu.VMEM((2, D_EXPERT, D_MODEL), F32),
                        pltpu.VMEM((D_MODEL, 2 * D_EXPERT), BF16), pltpu.VMEM((D_EXPERT, D_MODEL), BF16),
                        pltpu.SemaphoreType.DMA((2, 2))])
    return pl.pallas_call(
        functools.partial(_moe_kernel, layer=layer),
        grid_spec=grid_spec,
        out_shape=[jax.ShapeDtypeStruct((n_rows, SPLIT_WIDTH), jnp.uint32)] * SC_SPLIT,
        compiler_params=_params("arbitrary"),
        name="expert_ffn",
    )(block_e, n_used, n_valid, plan, *rows, w_gu, b_gu, w_down, b_down)


def _final_kernel(x_ref, tw_ref, mod_ref, lng_ref, lnb_ref, *rest, alpha):
    y_refs, o_ref = rest[:SC_SPLIT * TOP_K], rest[-1]
    top_w = tw_ref[...]
    parts = []
    for j in range(SC_SPLIT):
        acc = [0.0, 0.0]
        for k in range(TOP_K):
            for half, y in enumerate(_unpack_pairs(y_refs[j * TOP_K + k][...])):
                acc[half] = acc[half] + top_w[:, k:k + 1] * y
        parts += acc
    ffn = jnp.concatenate(parts, axis=1)
    o_ref[...] = (_layer_norm(alpha * x_ref[...] + (1.0 + mod_ref[5, 0]) * ffn) * lng_ref[0, 0] + lnb_ref[0, 0])


def _final_norm(x1, y_parts, top_w, mod_l, ln_g, ln_b, layer, seq, alpha, first_tile, earlier=None):
    tm = ROW_TILE
    tiles = y_parts[0].shape[0] // (TOP_K * tm)
    tiles_per_seq = seq // tm
    row = lambda w: pl.BlockSpec((tm, w), lambda i: (i + first_tile, 0))
    vec = pl.BlockSpec((1, 1, 1, D_MODEL), lambda i: (layer, 1, 0, 0))
    y_specs, y_args = [], []
    for j in range(SC_SPLIT):
        for k in range(TOP_K):
            y_specs.append(pl.BlockSpec((tm, SPLIT_WIDTH), functools.partial(lambda i, k: (k * tiles + i, 0), k=k)))
            y_args.append(y_parts[j])
    in_specs = [row(D_MODEL), row(LANES),
                pl.BlockSpec((6, 1, 1, D_MODEL), lambda i: (0, (i + first_tile) // tiles_per_seq, 0, 0)), vec, vec]
    args = [x1, top_w, mod_l, ln_g, ln_b, *y_args]
    aliases = {}
    if earlier is not None:
        aliases = {len(args): 0}
        args.append(earlier)
        y_specs = y_specs + [pl.BlockSpec(memory_space=pl.ANY)]
    return pl.pallas_call(
        functools.partial(_final_kernel, alpha=alpha),
        grid=(tiles,),
        in_specs=in_specs + y_specs,
        out_specs=row(D_MODEL),
        out_shape=jax.ShapeDtypeStruct(x1.shape, F32),
        input_output_aliases=aliases,
        compiler_params=_params("arbitrary"),
        name="combine_final_norm",
    )(*args)


def _inproj_weights(w_in):
    gate_cols = jnp.pad(w_in[:, :, OFF_NG:], ((0, 0), (0, 0), (0, LANES - N_GATE)))
    w_t = jnp.concatenate([w_in[:, :, off:off + width] for off, width in
                           ((OFF_NQ, NSA_WIDTH), (OFF_VS, NSA_KV_WIDTH), (OFF_VW, NSA_KV_WIDTH))] + [gate_cols],
                          axis=2).transpose(0, 2, 1)
    return w_in[:, :, :N_COLS].astype(BF16), w_t.astype(BF16)


def kernel(x, c, positions, w_in, w_out, ret_gn_w, conv_w, cmp_pos, cmp_w1, cmp_w2, ada_w, ada_b, ln_g, ln_b,
           router_w, router_b, w_gate_up, b_gate_up, w_down, b_down):
    batch, seq, _ = x.shape
    depth = w_in.shape[0]
    tokens = batch * seq
    n_rows = tokens * TOP_K + N_EXPERTS * MOE_TILE
    alpha = float((2 * depth) ** 0.25)

    w_in_r, w_in_t = _inproj_weights(w_in)
    w_out_b = w_out.astype(BF16)
    rw = jnp.pad(router_w, ((0, 0), (0, 0), (0, LANES - N_EXPERTS)))
    rw_hi = rw.astype(BF16)
    rw_lo = (rw - rw_hi.astype(F32)).astype(BF16)
    rb = jnp.pad(router_b, ((0, 0), (0, LANES - N_EXPERTS))).reshape(depth, 1, LANES)
    cw = _compress_weights(cmp_pos, cmp_w1, cmp_w2)
    ret_consts = _retention_consts()
    nsa_consts = _nsa_consts(seq)
    gn_w = ret_gn_w.reshape(depth, 1, RET_WIDTH)
    ln_g4 = ln_g.reshape(depth, 2, 1, D_MODEL)
    ln_b4 = ln_b.reshape(depth, 2, 1, D_MODEL)
    b_gu4 = b_gate_up.reshape(depth, N_EXPERTS, 1, 2 * D_EXPERT)
    b_d4 = b_down.reshape(depth, N_EXPERTS, 1, D_MODEL)

    mod = _modulation(c, ada_w, ada_b)
    cos, sin, cos_t, sin_t = _rope_tables(positions)
    xt = x.reshape(tokens, D_MODEL)
    for l in range(depth):
        (rq, rk, rv, rg, y_conv, qu, qr, kcx, vcx, ks, vs, kw, vw, ng) = _input_projection(
            xt, mod[l], w_in_r, w_in_t, cos, sin, cos_t, sin_t, conv_w, l, seq)
        y_ret = _retention(rq, rk, rv, rg, ret_consts, gn_w, l, batch, seq)
        kc, vc = _compress(kcx, vcx, cw, l, batch, seq)
        y_nsa = _nsa(qu, qr, kc, vc, ks, vs, kw, vw, ng, nsa_consts, batch, seq)
        x1, *h2, top_i, top_w, counts = _out_projection(y_ret, y_conv, y_nsa, xt, mod[l], w_out_b, ln_g4, ln_b4,
                                                        rw_hi, rw_lo, rb, l, seq, alpha)
        pstart, block_e, n_used, n_valid, plan = _block_table(counts, n_rows // MOE_TILE)
        pos_t = _route_positions(top_i, pstart)[:, :TOP_K].T
        rows = _scatter_rows(h2, pos_t, n_rows)
        y = _expert_ffn(rows, block_e, n_used, n_valid, plan, w_gate_up, b_gu4, w_down, b_d4, l)
        xt = None
        for part in range(COMBINE_PARTS):
            t0 = part * (tokens // COMBINE_PARTS)
            y_tok = _gather_rows(y, pos_t[:, t0:t0 + tokens // COMBINE_PARTS].reshape(-1))
            xt = _final_norm(x1, y_tok, top_w, mod[l], ln_g4, ln_b4, l, seq, alpha, t0 // ROW_TILE, xt)
    return xt.reshape(batch, seq, D_MODEL)
```
